```python
import math
import jax, jax.numpy as jnp
from jax import lax
import numpy as np

D_MODEL = 1024
BATCH = 8
SEQ = 8192
DEPTH = 2

HEAD_DIM = 64
CONV_CH = 256
CONV_K = 31
SC_CH = 256
SC_K = 3
SWA_Q_HEADS = 4
SWA_KV_HEADS = 2
WINDOW = 128
BLOCK = 128
FOX_HEADS = 4
N_BUCKETS = 32
MAX_DISTANCE = 128
D_FF = 2816
N_BRANCH = 4
EPS = 1e-6
NEG_INF = -1e30

SWA_COLS = (SWA_Q_HEADS + 2 * SWA_KV_HEADS) * HEAD_DIM
FOX_COLS = 3 * FOX_HEADS * HEAD_DIM + FOX_HEADS
_SIZES = (CONV_CH, CONV_CH,
          SC_CH, SC_CH, SC_CH,
          SWA_Q_HEADS * HEAD_DIM, SWA_KV_HEADS * HEAD_DIM, SWA_KV_HEADS * HEAD_DIM,
          FOX_HEADS * HEAD_DIM, FOX_HEADS * HEAD_DIM, FOX_HEADS * HEAD_DIM, FOX_HEADS,
          N_BRANCH * D_MODEL)
SPLITS = tuple(int(s) for s in np.cumsum(_SIZES)[:-1])
IN_COLS = int(sum(_SIZES))

kernel_name = "hybrid_gated_conformer_shortconv_swa_fox"


def _rmsnorm(x, g):
    xf = x.astype(jnp.float32)
    y = xf * lax.rsqrt(jnp.mean(xf * xf, axis=-1, keepdims=True) + EPS)
    return (y * g.astype(jnp.float32)).astype(x.dtype)


def _layernorm(x, g, b):
    xf = x.astype(jnp.float32)
    mu = jnp.mean(xf, axis=-1, keepdims=True)
    var = jnp.mean(jnp.square(xf - mu), axis=-1, keepdims=True)
    y = (xf - mu) * lax.rsqrt(var + EPS)
    return (y * g.astype(jnp.float32) + b.astype(jnp.float32)).astype(x.dtype)


def _swiglu(x, w_gate, w_up, w_down):
    return (jax.nn.silu(x @ w_gate) * (x @ w_up)) @ w_down


def _causal_depthwise_conv(x, w):
    k = w.shape[0]
    return lax.conv_general_dilated(
        x, w[:, None, :].astype(x.dtype), window_strides=(1,), padding=[(k - 1, 0)],
        dimension_numbers=("NWC", "WIO", "NWC"), feature_group_count=x.shape[-1])


def _t5_bucket(dist):
    max_exact = N_BUCKETS // 2
    d = jnp.maximum(dist, 1).astype(jnp.float32)
    large = max_exact + (jnp.log(d / max_exact) / math.log(MAX_DISTANCE / max_exact)
                         * (N_BUCKETS - max_exact)).astype(jnp.int32)
    large = jnp.minimum(large, N_BUCKETS - 1)
    return jnp.where(dist < max_exact, dist, large)


def _sliding_window_attention(q, k, v, sink, rel_bias):
    b, t, hq, dh = q.shape
    hkv = k.shape[2]
    grp = hq // hkv
    nb = t // BLOCK
    qb = q.reshape(b, nb, BLOCK, hkv, grp, dh)
    kb = k.reshape(b, nb, BLOCK, hkv, dh)
    vb = v.reshape(b, nb, BLOCK, hkv, dh)
    pad = ((0, 0), (1, 0), (0, 0), (0, 0), (0, 0))
    kk = jnp.concatenate([jnp.pad(kb, pad)[:, :-1], kb], axis=2)
    vv = jnp.concatenate([jnp.pad(vb, pad)[:, :-1], vb], axis=2)
    s = jnp.einsum("bnqhgd,bnkhd->bnhgqk", qb, kk,
                   preferred_element_type=jnp.float32) * (dh ** -0.5)
    qi = jnp.arange(BLOCK)[:, None] + BLOCK
    ki = jnp.arange(2 * BLOCK)[None, :]
    dist = qi - ki
    local_ok = (dist >= 0) & (dist < WINDOW)
    blk_ok = (jnp.arange(nb)[:, None, None] > 0) | (ki[None] >= BLOCK)
    mask = local_ok[None] & blk_ok
    bias = rel_bias[_t5_bucket(jnp.maximum(dist, 0))]
    bias = bias.transpose(2, 0, 1).reshape(hkv, grp, BLOCK, 2 * BLOCK).astype(jnp.float32)
    s = jnp.where(mask[None, :, None, None], s + bias, NEG_INF)
    sk = sink.astype(jnp.float32).reshape(hkv, grp)[:, :, None, None]
    m = jnp.maximum(jnp.max(s, axis=-1, keepdims=True), sk)
    p = jnp.exp(s - m)
    p = p / (jnp.sum(p, axis=-1, keepdims=True) + jnp.exp(sk - m))
    o = jnp.einsum("bnhgqk,bnkhd->bnqhgd", p.astype(v.dtype), vv)
    return o.reshape(b, t, hq * dh)


def _forgetting_attention(q, k, v, log_f):
    b, t, h, dh = q.shape
    nb = t // BLOCK
    cum = jnp.cumsum(log_f, axis=1)
    cum_k = cum.transpose(0, 2, 1)
    qb = q.reshape(b, nb, BLOCK, h, dh).transpose(1, 0, 2, 3, 4)
    cq = cum.reshape(b, nb, BLOCK, h).transpose(1, 0, 3, 2)
    kpos = jnp.arange(t)

    def one_block(args):
        q_blk, c_blk, bi = args
        s = jnp.einsum("bqhd,bkhd->bhqk", q_blk, k,
                       preferred_element_type=jnp.float32) * (dh ** -0.5)
        s = s + (c_blk[..., :, None] - cum_k[:, :, None, :])
        qpos = bi * BLOCK + jnp.arange(BLOCK)
        s = jnp.where(kpos[None, :] <= qpos[:, None], s, NEG_INF)
        p = jax.nn.softmax(s, axis=-1)
        return jnp.einsum("bhqk,bkhd->bqhd", p.astype(v.dtype), v)

    o = lax.map(one_block, (qb, cq, jnp.arange(nb)))
    return o.transpose(1, 0, 2, 3, 4).reshape(b, t, h * dh)


def _normal(k, shape, scale):
    return scale * jax.random.normal(k, shape, jnp.float32)


def _fwd_setup_inputs(seed: int = 0) -> dict:
    key = jax.random.key(seed)
    ks = jax.random.split(key, 32)
    L, D = DEPTH, D_MODEL
    return {
        "x": _normal(ks[0], (BATCH, SEQ, D), 1.0),
        "rel_bias": _normal(ks[1], (N_BUCKETS, SWA_Q_HEADS), 0.5),
        "ffn1_norm": 1.0 + _normal(ks[2], (L, D), 0.05),
        "ffn1_w_gate": _normal(ks[3], (L, D, D_FF), D ** -0.5),
        "ffn1_w_up": _normal(ks[4], (L, D, D_FF), D ** -0.5),
        "ffn1_w_down": _normal(ks[5], (L, D_FF, D), D_FF ** -0.5),
        "mix_norm": 1.0 + _normal(ks[6], (L, D), 0.05),
        "w_in": _normal(ks[7], (L, D, IN_COLS), D ** -0.5),
        "b_forget": 3.0 + _normal(ks[8], (L, FOX_HEADS), 0.5),
        "conf_dw": _normal(ks[9], (L, CONV_K, CONV_CH), CONV_K ** -0.5),
        "conf_dw_b": _normal(ks[10], (L, CONV_CH), 0.02),
        "conf_ln_g": 1.0 + _normal(ks[11], (L, CONV_CH), 0.05),
        "conf_ln_b": _normal(ks[12], (L, CONV_CH), 0.02),
        "conf_w_out": _normal(ks[13], (L, CONV_CH, D), CONV_CH ** -0.5),
        "sc_conv": _normal(ks[14], (L, SC_K, SC_CH), SC_K ** -0.5),
        "sc_w_out": _normal(ks[15], (L, SC_CH, D), SC_CH ** -0.5),
        "swa_q_norm": 1.0 + _normal(ks[16], (L, HEAD_DIM), 0.05),
        "swa_k_norm": 1.0 + _normal(ks[17], (L, HEAD_DIM), 0.05),
        "swa_sink": _normal(ks[18], (L, SWA_Q_HEADS), 0.5),
        "swa_w_o": _normal(ks[19], (L, SWA_Q_HEADS * HEAD_DIM, D), (SWA_Q_HEADS * HEAD_DIM) ** -0.5),
        "fox_q_norm": 1.0 + _normal(ks[20], (L, HEAD_DIM), 0.05),
        "fox_k_norm": 1.0 + _normal(ks[21], (L, HEAD_DIM), 0.05),
        "fox_w_o": _normal(ks[22], (L, FOX_HEADS * HEAD_DIM, D), (FOX_HEADS * HEAD_DIM) ** -0.5),
        "w_out": _normal(ks[23], (L, D, D), D ** -0.5),
        "ffn2_norm": 1.0 + _normal(ks[24], (L, D), 0.05),
        "ffn2_w_gate": _normal(ks[25], (L, D, D_FF), D ** -0.5),
        "ffn2_w_up": _normal(ks[26], (L, D, D_FF), D ** -0.5),
        "ffn2_w_down": _normal(ks[27], (L, D_FF, D), D_FF ** -0.5),
    }


def _fwd_reference(x, rel_bias, ffn1_norm, ffn1_w_gate, ffn1_w_up, ffn1_w_down, mix_norm, w_in,
              b_forget, conf_dw, conf_dw_b, conf_ln_g, conf_ln_b, conf_w_out, sc_conv, sc_w_out,
              swa_q_norm, swa_k_norm, swa_sink, swa_w_o, fox_q_norm, fox_k_norm, fox_w_o, w_out,
              ffn2_norm, ffn2_w_gate, ffn2_w_up, ffn2_w_down):
    bsz, t = x.shape[0], x.shape[1]
    for l in range(DEPTH):
        x = x + 0.5 * _swiglu(_rmsnorm(x, ffn1_norm[l]), ffn1_w_gate[l], ffn1_w_up[l], ffn1_w_down[l])

        h = _rmsnorm(x, mix_norm[l])
        z = h @ w_in[l]
        (c_a, c_b, s_b, s_c, s_x, a_q, a_k, a_v,
         f_q, f_k, f_v, f_f, z_gate) = jnp.split(z, SPLITS, axis=-1)

        u = c_a * jax.nn.sigmoid(c_b)
        u = _causal_depthwise_conv(u, conf_dw[l]) + conf_dw_b[l]
        u = jax.nn.silu(_layernorm(u, conf_ln_g[l], conf_ln_b[l]))
        p_conf = u @ conf_w_out[l]

        p_sc = (s_b * _causal_depthwise_conv(s_c * s_x, sc_conv[l])) @ sc_w_out[l]

        q = _rmsnorm(a_q.reshape(bsz, t, SWA_Q_HEADS, HEAD_DIM), swa_q_norm[l])
        k = _rmsnorm(a_k.reshape(bsz, t, SWA_KV_HEADS, HEAD_DIM), swa_k_norm[l])
        v = a_v.reshape(bsz, t, SWA_KV_HEADS, HEAD_DIM)
        p_swa = _sliding_window_attention(q, k, v, swa_sink[l], rel_bias) @ swa_w_o[l]

        q2 = _rmsnorm(f_q.reshape(bsz, t, FOX_HEADS, HEAD_DIM), fox_q_norm[l])
        k2 = _rmsnorm(f_k.reshape(bsz, t, FOX_HEADS, HEAD_DIM), fox_k_norm[l])
        v2 = f_v.reshape(bsz, t, FOX_HEADS, HEAD_DIM)
        log_f = jax.nn.log_sigmoid(f_f.astype(jnp.float32) + b_forget[l].astype(jnp.float32))
        p_fox = _forgetting_attention(q2, k2, v2, log_f) @ fox_w_o[l]

        g = jax.nn.sigmoid(z_gate).reshape(bsz, t, N_BRANCH, D_MODEL)
        merged = (g[:, :, 0] * p_conf + g[:, :, 1] * p_sc
                  + g[:, :, 2] * p_swa + g[:, :, 3] * p_fox)
        x = x + merged @ w_out[l]

        x = x + 0.5 * _swiglu(_rmsnorm(x, ffn2_norm[l]), ffn2_w_gate[l], ffn2_w_up[l], ffn2_w_down[l])
    return x


import jax as _jax
import jax.numpy as _jnp

TWIN_FORMAT = 'train_step'
FWD_PARAMS = ['x', 'rel_bias', 'ffn1_norm', 'ffn1_w_gate', 'ffn1_w_up', 'ffn1_w_down', 'mix_norm', 'w_in', 'b_forget', 'conf_dw', 'conf_dw_b', 'conf_ln_g', 'conf_ln_b', 'conf_w_out', 'sc_conv', 'sc_w_out', 'swa_q_norm', 'swa_k_norm', 'swa_sink', 'swa_w_o', 'fox_q_norm', 'fox_k_norm', 'fox_w_o', 'w_out', 'ffn2_norm', 'ffn2_w_gate', 'ffn2_w_up', 'ffn2_w_down']
TWIN_WEIGHTS = ['rel_bias', 'ffn1_norm', 'ffn1_w_gate', 'ffn1_w_up', 'ffn1_w_down', 'mix_norm', 'w_in', 'b_forget', 'conf_dw', 'conf_dw_b', 'conf_ln_g', 'conf_ln_b', 'conf_w_out', 'sc_conv', 'sc_w_out', 'swa_q_norm', 'swa_k_norm', 'swa_sink', 'swa_w_o', 'fox_q_norm', 'fox_k_norm', 'fox_w_o', 'w_out', 'ffn2_norm', 'ffn2_w_gate', 'ffn2_w_up', 'ffn2_w_down']
TWIN_DIFF_INPUT = 'x'
TWIN_INPUTS = ['x', 'rel_bias', 'ffn1_norm', 'ffn1_w_gate', 'ffn1_w_up', 'ffn1_w_down', 'mix_norm', 'w_in', 'b_forget', 'conf_dw', 'conf_dw_b', 'conf_ln_g', 'conf_ln_b', 'conf_w_out', 'sc_conv', 'sc_w_out', 'swa_q_norm', 'swa_k_norm', 'swa_sink', 'swa_w_o', 'fox_q_norm', 'fox_k_norm', 'fox_w_o', 'w_out', 'ffn2_norm', 'ffn2_w_gate', 'ffn2_w_up', 'ffn2_w_down', 'loss_target', 'm_rel_bias', 'm_ffn1_norm', 'm_ffn1_w_gate', 'm_ffn1_w_up', 'm_ffn1_w_down', 'm_mix_norm', 'm_w_in', 'm_b_forget', 'm_conf_dw', 'm_conf_dw_b', 'm_conf_ln_g', 'm_conf_ln_b', 'm_conf_w_out', 'm_sc_conv', 'm_sc_w_out', 'm_swa_q_norm', 'm_swa_k_norm', 'm_swa_sink', 'm_swa_w_o', 'm_fox_q_norm', 'm_fox_k_norm', 'm_fox_w_o', 'm_w_out', 'm_ffn2_norm', 'm_ffn2_w_gate', 'm_ffn2_w_up', 'm_ffn2_w_down', 'v_rel_bias', 'v_ffn1_norm', 'v_ffn1_w_gate', 'v_ffn1_w_up', 'v_ffn1_w_down', 'v_mix_norm', 'v_w_in', 'v_b_forget', 'v_conf_dw', 'v_conf_dw_b', 'v_conf_ln_g', 'v_conf_ln_b', 'v_conf_w_out', 'v_sc_conv', 'v_sc_w_out', 'v_swa_q_norm', 'v_swa_k_norm', 'v_swa_sink', 'v_swa_w_o', 'v_fox_q_norm', 'v_fox_k_norm', 'v_fox_w_o', 'v_w_out', 'v_ffn2_norm', 'v_ffn2_w_gate', 'v_ffn2_w_up', 'v_ffn2_w_down']
TWIN_OUTPUTS = ['loss', 'grad_x', 'grad_rel_bias', 'grad_ffn1_norm', 'grad_ffn1_w_gate', 'grad_ffn1_w_up', 'grad_ffn1_w_down', 'grad_mix_norm', 'grad_w_in', 'grad_b_forget', 'grad_conf_dw', 'grad_conf_dw_b', 'grad_conf_ln_g', 'grad_conf_ln_b', 'grad_conf_w_out', 'grad_sc_conv', 'grad_sc_w_out', 'grad_swa_q_norm', 'grad_swa_k_norm', 'grad_swa_sink', 'grad_swa_w_o', 'grad_fox_q_norm', 'grad_fox_k_norm', 'grad_fox_w_o', 'grad_w_out', 'grad_ffn2_norm', 'grad_ffn2_w_gate', 'grad_ffn2_w_up', 'grad_ffn2_w_down', 'delta_rel_bias', 'delta_ffn1_norm', 'delta_ffn1_w_gate', 'delta_ffn1_w_up', 'delta_ffn1_w_down', 'delta_mix_norm', 'delta_w_in', 'delta_b_forget', 'delta_conf_dw', 'delta_conf_dw_b', 'delta_conf_ln_g', 'delta_conf_ln_b', 'delta_conf_w_out', 'delta_sc_conv', 'delta_sc_w_out', 'delta_swa_q_norm', 'delta_swa_k_norm', 'delta_swa_sink', 'delta_swa_w_o', 'delta_fox_q_norm', 'delta_fox_k_norm', 'delta_fox_w_o', 'delta_w_out', 'delta_ffn2_norm', 'delta_ffn2_w_gate', 'delta_ffn2_w_up', 'delta_ffn2_w_down', 'new_m_rel_bias', 'new_m_ffn1_norm', 'new_m_ffn1_w_gate', 'new_m_ffn1_w_up', 'new_m_ffn1_w_down', 'new_m_mix_norm', 'new_m_w_in', 'new_m_b_forget', 'new_m_conf_dw', 'new_m_conf_dw_b', 'new_m_conf_ln_g', 'new_m_conf_ln_b', 'new_m_conf_w_out', 'new_m_sc_conv', 'new_m_sc_w_out', 'new_m_swa_q_norm', 'new_m_swa_k_norm', 'new_m_swa_sink', 'new_m_swa_w_o', 'new_m_fox_q_norm', 'new_m_fox_k_norm', 'new_m_fox_w_o', 'new_m_w_out', 'new_m_ffn2_norm', 'new_m_ffn2_w_gate', 'new_m_ffn2_w_up', 'new_m_ffn2_w_down', 'new_v_rel_bias', 'new_v_ffn1_norm', 'new_v_ffn1_w_gate', 'new_v_ffn1_w_up', 'new_v_ffn1_w_down', 'new_v_mix_norm', 'new_v_w_in', 'new_v_b_forget', 'new_v_conf_dw', 'new_v_conf_dw_b', 'new_v_conf_ln_g', 'new_v_conf_ln_b', 'new_v_conf_w_out', 'new_v_sc_conv', 'new_v_sc_w_out', 'new_v_swa_q_norm', 'new_v_swa_k_norm', 'new_v_swa_sink', 'new_v_swa_w_o', 'new_v_fox_q_norm', 'new_v_fox_k_norm', 'new_v_fox_w_o', 'new_v_w_out', 'new_v_ffn2_norm', 'new_v_ffn2_w_gate', 'new_v_ffn2_w_up', 'new_v_ffn2_w_down']
TWIN_LEAF_KINDS = {'loss': 'loss', 'grad_x': 'grad_x', 'grad_rel_bias': 'grad_w', 'grad_ffn1_norm': 'grad_w', 'grad_ffn1_w_gate': 'grad_w', 'grad_ffn1_w_up': 'grad_w', 'grad_ffn1_w_down': 'grad_w', 'grad_mix_norm': 'grad_w', 'grad_w_in': 'grad_w', 'grad_b_forget': 'grad_w', 'grad_conf_dw': 'grad_w', 'grad_conf_dw_b': 'grad_w', 'grad_conf_ln_g': 'grad_w', 'grad_conf_ln_b': 'grad_w', 'grad_conf_w_out': 'grad_w', 'grad_sc_conv': 'grad_w', 'grad_sc_w_out': 'grad_w', 'grad_swa_q_norm': 'grad_w', 'grad_swa_k_norm': 'grad_w', 'grad_swa_sink': 'grad_w', 'grad_swa_w_o': 'grad_w', 'grad_fox_q_norm': 'grad_w', 'grad_fox_k_norm': 'grad_w', 'grad_fox_w_o': 'grad_w', 'grad_w_out': 'grad_w', 'grad_ffn2_norm': 'grad_w', 'grad_ffn2_w_gate': 'grad_w', 'grad_ffn2_w_up': 'grad_w', 'grad_ffn2_w_down': 'grad_w', 'delta_rel_bias': 'delta_w', 'delta_ffn1_norm': 'delta_w', 'delta_ffn1_w_gate': 'delta_w', 'delta_ffn1_w_up': 'delta_w', 'delta_ffn1_w_down': 'delta_w', 'delta_mix_norm': 'delta_w', 'delta_w_in': 'delta_w', 'delta_b_forget': 'delta_w', 'delta_conf_dw': 'delta_w', 'delta_conf_dw_b': 'delta_w', 'delta_conf_ln_g': 'delta_w', 'delta_conf_ln_b': 'delta_w', 'delta_conf_w_out': 'delta_w', 'delta_sc_conv': 'delta_w', 'delta_sc_w_out': 'delta_w', 'delta_swa_q_norm': 'delta_w', 'delta_swa_k_norm': 'delta_w', 'delta_swa_sink': 'delta_w', 'delta_swa_w_o': 'delta_w', 'delta_fox_q_norm': 'delta_w', 'delta_fox_k_norm': 'delta_w', 'delta_fox_w_o': 'delta_w', 'delta_w_out': 'delta_w', 'delta_ffn2_norm': 'delta_w', 'delta_ffn2_w_gate': 'delta_w', 'delta_ffn2_w_up': 'delta_w', 'delta_ffn2_w_down': 'delta_w', 'new_m_rel_bias': 'new_m', 'new_m_ffn1_norm': 'new_m', 'new_m_ffn1_w_gate': 'new_m', 'new_m_ffn1_w_up': 'new_m', 'new_m_ffn1_w_down': 'new_m', 'new_m_mix_norm': 'new_m', 'new_m_w_in': 'new_m', 'new_m_b_forget': 'new_m', 'new_m_conf_dw': 'new_m', 'new_m_conf_dw_b': 'new_m', 'new_m_conf_ln_g': 'new_m', 'new_m_conf_ln_b': 'new_m', 'new_m_conf_w_out': 'new_m', 'new_m_sc_conv': 'new_m', 'new_m_sc_w_out': 'new_m', 'new_m_swa_q_norm': 'new_m', 'new_m_swa_k_norm': 'new_m', 'new_m_swa_sink': 'new_m', 'new_m_swa_w_o': 'new_m', 'new_m_fox_q_norm': 'new_m', 'new_m_fox_k_norm': 'new_m', 'new_m_fox_w_o': 'new_m', 'new_m_w_out': 'new_m', 'new_m_ffn2_norm': 'new_m', 'new_m_ffn2_w_gate': 'new_m', 'new_m_ffn2_w_up': 'new_m', 'new_m_ffn2_w_down': 'new_m', 'new_v_rel_bias': 'new_v', 'new_v_ffn1_norm': 'new_v', 'new_v_ffn1_w_gate': 'new_v', 'new_v_ffn1_w_up': 'new_v', 'new_v_ffn1_w_down': 'new_v', 'new_v_mix_norm': 'new_v', 'new_v_w_in': 'new_v', 'new_v_b_forget': 'new_v', 'new_v_conf_dw': 'new_v', 'new_v_conf_dw_b': 'new_v', 'new_v_conf_ln_g': 'new_v', 'new_v_conf_ln_b': 'new_v', 'new_v_conf_w_out': 'new_v', 'new_v_sc_conv': 'new_v', 'new_v_sc_w_out': 'new_v', 'new_v_swa_q_norm': 'new_v', 'new_v_swa_k_norm': 'new_v', 'new_v_swa_sink': 'new_v', 'new_v_swa_w_o': 'new_v', 'new_v_fox_q_norm': 'new_v', 'new_v_fox_k_norm': 'new_v', 'new_v_fox_w_o': 'new_v', 'new_v_w_out': 'new_v', 'new_v_ffn2_norm': 'new_v', 'new_v_ffn2_w_gate': 'new_v', 'new_v_ffn2_w_up': 'new_v', 'new_v_ffn2_w_down': 'new_v'}


def _forward(args):
    return _fwd_reference(*[args[k] for k in FWD_PARAMS])


def _output_shape():
    def fwd():
        inp = _fwd_setup_inputs(0)
        return _fwd_reference(*[inp[k] for k in FWD_PARAMS])
    out = _jax.eval_shape(fwd)
    return out.shape, out.dtype

N_MICROBATCH = 1
ADAM_LR = 0.001
ADAM_B1 = 0.9
ADAM_B2 = 0.999
ADAM_EPS = 1e-08
ADAM_WD = 0.01
ADAM_STEP = 10
PER_EXAMPLE_BATCH_AXIS = {'x': 0, 'loss_target': 0}
SHARED_INPUTS = []
_WEIGHT_DTYPES = {'rel_bias': _jnp.float32, 'ffn1_norm': _jnp.float32, 'ffn1_w_gate': _jnp.float32, 'ffn1_w_up': _jnp.float32, 'ffn1_w_down': _jnp.float32, 'mix_norm': _jnp.float32, 'w_in': _jnp.float32, 'b_forget': _jnp.float32, 'conf_dw': _jnp.float32, 'conf_dw_b': _jnp.float32, 'conf_ln_g': _jnp.float32, 'conf_ln_b': _jnp.float32, 'conf_w_out': _jnp.float32, 'sc_conv': _jnp.float32, 'sc_w_out': _jnp.float32, 'swa_q_norm': _jnp.float32, 'swa_k_norm': _jnp.float32, 'swa_sink': _jnp.float32, 'swa_w_o': _jnp.float32, 'fox_q_norm': _jnp.float32, 'fox_k_norm': _jnp.float32, 'fox_w_o': _jnp.float32, 'w_out': _jnp.float32, 'ffn2_norm': _jnp.float32, 'ffn2_w_gate': _jnp.float32, 'ffn2_w_up': _jnp.float32, 'ffn2_w_down': _jnp.float32}
MOMENT_SCALE = {'rel_bias': 2.452131e+00, 'ffn1_norm': 1.157933e+01, 'ffn1_w_gate': 2.811547e-01, 'ffn1_w_up': 2.975112e-01, 'ffn1_w_down': 4.951605e-01, 'mix_norm': 6.270523e+01, 'w_in': 8.823466e-01, 'b_forget': 1.822324e+02, 'conf_dw': 1.161537e+00, 'conf_dw_b': 1.763962e+01, 'conf_ln_g': 3.478350e+01, 'conf_ln_b': 2.434890e+01, 'conf_w_out': 1.712118e+00, 'sc_conv': 4.373146e+01, 'sc_w_out': 1.605194e+00, 'swa_q_norm': 3.985035e+00, 'swa_k_norm': 3.995204e+00, 'swa_sink': 2.032850e+00, 'swa_w_o': 2.297480e-01, 'fox_q_norm': 1.434904e+01, 'fox_k_norm': 1.432424e+01, 'fox_w_o': 3.222837e-01, 'w_out': 1.928178e+00, 'ffn2_norm': 1.238906e+01, 'ffn2_w_gate': 1.925543e-01, 'ffn2_w_up': 2.133070e-01, 'ffn2_w_down': 3.516299e-01}


def _to_microbatches(a, axis):
    t = _jnp.moveaxis(a, axis, 0)
    t = t.reshape((N_MICROBATCH, t.shape[0] // N_MICROBATCH) + t.shape[1:])
    return _jnp.moveaxis(t, 1, axis + 1)


def setup_inputs(seed: int = 0) -> dict:
    inp = _fwd_setup_inputs(seed)
    key = _jax.random.fold_in(_jax.random.key(seed), 7919)
    shape, _ = _output_shape()
    out = dict(inp)
    out["loss_target"] = _jax.random.normal(_jax.random.fold_in(key, 0), shape, _jnp.float32)
    for i, name in enumerate(TWIN_WEIGHTS):
        w = inp[name].astype(_jnp.float32)
        if MOMENT_SCALE is None:
            s = _jnp.sqrt(_jnp.mean(_jnp.square(w)) + 1e-30)
        else:
            s = MOMENT_SCALE[name]
        km, kv = _jax.random.split(_jax.random.fold_in(key, i + 1))
        out[name] = w
        out["m_" + name] = s * _jax.random.normal(km, w.shape, _jnp.float32)
        out["v_" + name] = (s * s) * _jax.random.uniform(kv, w.shape, _jnp.float32, 0.5, 1.5)
    if N_MICROBATCH > 1:
        for name, axis in PER_EXAMPLE_BATCH_AXIS.items():
            out[name] = _to_microbatches(out[name], axis)
    return {'x': out['x'], 'rel_bias': out['rel_bias'], 'ffn1_norm': out['ffn1_norm'], 'ffn1_w_gate': out['ffn1_w_gate'], 'ffn1_w_up': out['ffn1_w_up'], 'ffn1_w_down': out['ffn1_w_down'], 'mix_norm': out['mix_norm'], 'w_in': out['w_in'], 'b_forget': out['b_forget'], 'conf_dw': out['conf_dw'], 'conf_dw_b': out['conf_dw_b'], 'conf_ln_g': out['conf_ln_g'], 'conf_ln_b': out['conf_ln_b'], 'conf_w_out': out['conf_w_out'], 'sc_conv': out['sc_conv'], 'sc_w_out': out['sc_w_out'], 'swa_q_norm': out['swa_q_norm'], 'swa_k_norm': out['swa_k_norm'], 'swa_sink': out['swa_sink'], 'swa_w_o': out['swa_w_o'], 'fox_q_norm': out['fox_q_norm'], 'fox_k_norm': out['fox_k_norm'], 'fox_w_o': out['fox_w_o'], 'w_out': out['w_out'], 'ffn2_norm': out['ffn2_norm'], 'ffn2_w_gate': out['ffn2_w_gate'], 'ffn2_w_up': out['ffn2_w_up'], 'ffn2_w_down': out['ffn2_w_down'], 'loss_target': out['loss_target'], 'm_rel_bias': out['m_rel_bias'], 'm_ffn1_norm': out['m_ffn1_norm'], 'm_ffn1_w_gate': out['m_ffn1_w_gate'], 'm_ffn1_w_up': out['m_ffn1_w_up'], 'm_ffn1_w_down': out['m_ffn1_w_down'], 'm_mix_norm': out['m_mix_norm'], 'm_w_in': out['m_w_in'], 'm_b_forget': out['m_b_forget'], 'm_conf_dw': out['m_conf_dw'], 'm_conf_dw_b': out['m_conf_dw_b'], 'm_conf_ln_g': out['m_conf_ln_g'], 'm_conf_ln_b': out['m_conf_ln_b'], 'm_conf_w_out': out['m_conf_w_out'], 'm_sc_conv': out['m_sc_conv'], 'm_sc_w_out': out['m_sc_w_out'], 'm_swa_q_norm': out['m_swa_q_norm'], 'm_swa_k_norm': out['m_swa_k_norm'], 'm_swa_sink': out['m_swa_sink'], 'm_swa_w_o': out['m_swa_w_o'], 'm_fox_q_norm': out['m_fox_q_norm'], 'm_fox_k_norm': out['m_fox_k_norm'], 'm_fox_w_o': out['m_fox_w_o'], 'm_w_out': out['m_w_out'], 'm_ffn2_norm': out['m_ffn2_norm'], 'm_ffn2_w_gate': out['m_ffn2_w_gate'], 'm_ffn2_w_up': out['m_ffn2_w_up'], 'm_ffn2_w_down': out['m_ffn2_w_down'], 'v_rel_bias': out['v_rel_bias'], 'v_ffn1_norm': out['v_ffn1_norm'], 'v_ffn1_w_gate': out['v_ffn1_w_gate'], 'v_ffn1_w_up': out['v_ffn1_w_up'], 'v_ffn1_w_down': out['v_ffn1_w_down'], 'v_mix_norm': out['v_mix_norm'], 'v_w_in': out['v_w_in'], 'v_b_forget': out['v_b_forget'], 'v_conf_dw': out['v_conf_dw'], 'v_conf_dw_b': out['v_conf_dw_b'], 'v_conf_ln_g': out['v_conf_ln_g'], 'v_conf_ln_b': out['v_conf_ln_b'], 'v_conf_w_out': out['v_conf_w_out'], 'v_sc_conv': out['v_sc_conv'], 'v_sc_w_out': out['v_sc_w_out'], 'v_swa_q_norm': out['v_swa_q_norm'], 'v_swa_k_norm': out['v_swa_k_norm'], 'v_swa_sink': out['v_swa_sink'], 'v_swa_w_o': out['v_swa_w_o'], 'v_fox_q_norm': out['v_fox_q_norm'], 'v_fox_k_norm': out['v_fox_k_norm'], 'v_fox_w_o': out['v_fox_w_o'], 'v_w_out': out['v_w_out'], 'v_ffn2_norm': out['v_ffn2_norm'], 'v_ffn2_w_gate': out['v_ffn2_w_gate'], 'v_ffn2_w_up': out['v_ffn2_w_up'], 'v_ffn2_w_down': out['v_ffn2_w_down']}


def _loss(weights, diff, rest, loss_target):
    with _jax.named_scope("forward"):
        args = {**rest, TWIN_DIFF_INPUT: diff, **{k: w.astype(_WEIGHT_DTYPES[k]) for k, w in weights.items()}}
        y = _forward(args)
    with _jax.named_scope("loss_head"):
        err = _jnp.square(y.astype(_jnp.float32) - loss_target)
        return 0.5 * _jnp.sum(_jnp.mean(err, axis=-1)) if err.ndim else 0.5 * err


def _adamw(w, g, m, v):
    m = ADAM_B1 * m + (1.0 - ADAM_B1) * g
    v = ADAM_B2 * v + (1.0 - ADAM_B2) * _jnp.square(g)
    m_hat = m / (1.0 - ADAM_B1 ** ADAM_STEP)
    v_hat = v / (1.0 - ADAM_B2 ** ADAM_STEP)
    delta = -ADAM_LR * (m_hat / (_jnp.sqrt(v_hat) + ADAM_EPS) + ADAM_WD * w)
    return delta, m, v


def reference(x, rel_bias, ffn1_norm, ffn1_w_gate, ffn1_w_up, ffn1_w_down, mix_norm, w_in, b_forget, conf_dw, conf_dw_b, conf_ln_g, conf_ln_b, conf_w_out, sc_conv, sc_w_out, swa_q_norm, swa_k_norm, swa_sink, swa_w_o, fox_q_norm, fox_k_norm, fox_w_o, w_out, ffn2_norm, ffn2_w_gate, ffn2_w_up, ffn2_w_down, loss_target, m_rel_bias, m_ffn1_norm, m_ffn1_w_gate, m_ffn1_w_up, m_ffn1_w_down, m_mix_norm, m_w_in, m_b_forget, m_conf_dw, m_conf_dw_b, m_conf_ln_g, m_conf_ln_b, m_conf_w_out, m_sc_conv, m_sc_w_out, m_swa_q_norm, m_swa_k_norm, m_swa_sink, m_swa_w_o, m_fox_q_norm, m_fox_k_norm, m_fox_w_o, m_w_out, m_ffn2_norm, m_ffn2_w_gate, m_ffn2_w_up, m_ffn2_w_down, v_rel_bias, v_ffn1_norm, v_ffn1_w_gate, v_ffn1_w_up, v_ffn1_w_down, v_mix_norm, v_w_in, v_b_forget, v_conf_dw, v_conf_dw_b, v_conf_ln_g, v_conf_ln_b, v_conf_w_out, v_sc_conv, v_sc_w_out, v_swa_q_norm, v_swa_k_norm, v_swa_sink, v_swa_w_o, v_fox_q_norm, v_fox_k_norm, v_fox_w_o, v_w_out, v_ffn2_norm, v_ffn2_w_gate, v_ffn2_w_up, v_ffn2_w_down):
    given = dict(x=x, rel_bias=rel_bias, ffn1_norm=ffn1_norm, ffn1_w_gate=ffn1_w_gate, ffn1_w_up=ffn1_w_up, ffn1_w_down=ffn1_w_down, mix_norm=mix_norm, w_in=w_in, b_forget=b_forget, conf_dw=conf_dw, conf_dw_b=conf_dw_b, conf_ln_g=conf_ln_g, conf_ln_b=conf_ln_b, conf_w_out=conf_w_out, sc_conv=sc_conv, sc_w_out=sc_w_out, swa_q_norm=swa_q_norm, swa_k_norm=swa_k_norm, swa_sink=swa_sink, swa_w_o=swa_w_o, fox_q_norm=fox_q_norm, fox_k_norm=fox_k_norm, fox_w_o=fox_w_o, w_out=w_out, ffn2_norm=ffn2_norm, ffn2_w_gate=ffn2_w_gate, ffn2_w_up=ffn2_w_up, ffn2_w_down=ffn2_w_down, loss_target=loss_target, m_rel_bias=m_rel_bias, m_ffn1_norm=m_ffn1_norm, m_ffn1_w_gate=m_ffn1_w_gate, m_ffn1_w_up=m_ffn1_w_up, m_ffn1_w_down=m_ffn1_w_down, m_mix_norm=m_mix_norm, m_w_in=m_w_in, m_b_forget=m_b_forget, m_conf_dw=m_conf_dw, m_conf_dw_b=m_conf_dw_b, m_conf_ln_g=m_conf_ln_g, m_conf_ln_b=m_conf_ln_b, m_conf_w_out=m_conf_w_out, m_sc_conv=m_sc_conv, m_sc_w_out=m_sc_w_out, m_swa_q_norm=m_swa_q_norm, m_swa_k_norm=m_swa_k_norm, m_swa_sink=m_swa_sink, m_swa_w_o=m_swa_w_o, m_fox_q_norm=m_fox_q_norm, m_fox_k_norm=m_fox_k_norm, m_fox_w_o=m_fox_w_o, m_w_out=m_w_out, m_ffn2_norm=m_ffn2_norm, m_ffn2_w_gate=m_ffn2_w_gate, m_ffn2_w_up=m_ffn2_w_up, m_ffn2_w_down=m_ffn2_w_down, v_rel_bias=v_rel_bias, v_ffn1_norm=v_ffn1_norm, v_ffn1_w_gate=v_ffn1_w_gate, v_ffn1_w_up=v_ffn1_w_up, v_ffn1_w_down=v_ffn1_w_down, v_mix_norm=v_mix_norm, v_w_in=v_w_in, v_b_forget=v_b_forget, v_conf_dw=v_conf_dw, v_conf_dw_b=v_conf_dw_b, v_conf_ln_g=v_conf_ln_g, v_conf_ln_b=v_conf_ln_b, v_conf_w_out=v_conf_w_out, v_sc_conv=v_sc_conv, v_sc_w_out=v_sc_w_out, v_swa_q_norm=v_swa_q_norm, v_swa_k_norm=v_swa_k_norm, v_swa_sink=v_swa_sink, v_swa_w_o=v_swa_w_o, v_fox_q_norm=v_fox_q_norm, v_fox_k_norm=v_fox_k_norm, v_fox_w_o=v_fox_w_o, v_w_out=v_w_out, v_ffn2_norm=v_ffn2_norm, v_ffn2_w_gate=v_ffn2_w_gate, v_ffn2_w_up=v_ffn2_w_up, v_ffn2_w_down=v_ffn2_w_down)
    weights = {n: given[n] for n in TWIN_WEIGHTS}
    shared = {n: given[n] for n in SHARED_INPUTS}
    per_example = {n: given[n] for n in ['x']}
    grad_fn = _jax.value_and_grad(_loss, argnums=(0, 1))

    def one_microbatch(ex, loss_target):
        ex = dict(ex)
        diff = ex.pop(TWIN_DIFF_INPUT)
        return grad_fn(weights, diff, {**shared, **ex}, loss_target)

    if N_MICROBATCH == 1:
        loss, (grad_w, grad_x) = one_microbatch(per_example, given["loss_target"])
    else:
        def body(carry, xs):
            loss_sum, grad_sum = carry
            l_k, (gw_k, gx_k) = one_microbatch(xs[0], xs[1])
            with _jax.named_scope("update"):
                return (loss_sum + l_k, _jax.tree.map(_jnp.add, grad_sum, gw_k)), gx_k

        init = (_jnp.zeros((), _jnp.float32), _jax.tree.map(_jnp.zeros_like, weights))
        (loss, grad_w), grad_x = _jax.lax.scan(body, init, (per_example, given["loss_target"]))
    with _jax.named_scope("update"):
        delta_w, new_m, new_v = {}, {}, {}
        for n in TWIN_WEIGHTS:
            delta_w[n], new_m[n], new_v[n] = _adamw(weights[n], grad_w[n], given["m_" + n], given["v_" + n])
    return (loss, grad_x, *[grad_w[n] for n in TWIN_WEIGHTS], *[delta_w[n] for n in TWIN_WEIGHTS],
            *[new_m[n] for n in TWIN_WEIGHTS], *[new_v[n] for n in TWIN_WEIGHTS])
```

```python
import math

import numpy as np
import jax
import jax.numpy as jnp
from jax import lax
from jax.experimental import pallas as pl
from jax.experimental.pallas import tpu as pltpu

F32 = jnp.float32
_MXU = jnp.bfloat16
EPS = 1e-6
NEG_INF = -1e30
HEAD = 64
CH = 256
WINDOW = 128
CONV_K = 31
SC_K = 3
CONV_HALO = 32
SC_HALO = 8
N_BUCKETS = 32
MAX_DISTANCE = 128
N_DEV = 8
LANE = 128
ROW_TILE = 512
VMEM_LIMIT = 48 * 1024 * 1024
ADAM_LR, ADAM_B1, ADAM_B2, ADAM_EPS, ADAM_WD, ADAM_STEP = 0.001, 0.9, 0.999, 1e-08, 0.01, 10

_NN = (((1,), (0,)), ((), ()))
_NT = (((1,), (1,)), ((), ()))
_TN = (((0,), (0,)), ((), ()))


def _cp(*sem):
    return pltpu.CompilerParams(dimension_semantics=sem, vmem_limit_bytes=VMEM_LIMIT)


def _tile(n, pref, align=LANE):
    t = (min(n, pref) // align) * align
    while t >= align:
        if n % t == 0:
            return t
        t -= align
    return n


def _dot(a, b, dims=_NN):
    return lax.dot_general(a.astype(_MXU), b.astype(_MXU), dims, preferred_element_type=F32)


def _split3(x):
    hi = x.astype(_MXU)
    r1 = x - hi.astype(F32)
    mid = r1.astype(_MXU)
    lo = (r1 - mid.astype(F32)).astype(_MXU)
    return hi, mid, lo


def _exact_dot(a, b, dims, data):
    if data == "a":
        return sum(lax.dot_general(p, b.astype(_MXU), dims, preferred_element_type=F32) for p in _split3(a))
    return sum(lax.dot_general(a.astype(_MXU), p, dims, preferred_element_type=F32) for p in _split3(b))


def _sigmoid(x):
    return jax.nn.sigmoid(x)


def _lane_mask(width, h):
    lane = lax.broadcasted_iota(jnp.int32, (1, width), 1)
    return (lane >= h * HEAD) & (lane < (h + 1) * HEAD)


def _head_rms(x, g, nh):
    xx = x * x
    ms = jnp.zeros_like(x)
    for h in range(nh):
        mk = _lane_mask(x.shape[-1], h)
        s = jnp.sum(jnp.where(mk, xx, 0.0), axis=-1, keepdims=True) * (1.0 / HEAD)
        ms = jnp.where(mk, s, ms)
    r = lax.rsqrt(ms + EPS)
    return x * r * g, r


def _head_rms_bwd(dy, x, r, g, nh):
    w = dy * g
    wx = w * x
    c = jnp.zeros_like(x)
    for h in range(nh):
        mk = _lane_mask(x.shape[-1], h)
        s = jnp.sum(jnp.where(mk, wx, 0.0), axis=-1, keepdims=True) * (1.0 / HEAD)
        c = jnp.where(mk, s, c)
    dx = r * w - x * (r * r * r) * c
    dg = jnp.sum(dy * x * r, axis=0, keepdims=True)
    return dx, dg


def _mm(pairs, mode, out_dtype, name, scale=None, res=None, tm=512, tn=1024, tk=1024):
    a0, b0 = pairs[0]
    M = a0.shape[1] if mode == "tn" else a0.shape[0]
    N = b0.shape[0] if mode == "nt" else b0.shape[1]
    tm, tn = _tile(M, tm), _tile(N, tn)
    dims = {"nn": _NN, "nt": _NT, "tn": _TN}[mode]
    tks, nks, offs = [], [], []
    for a, _ in pairs:
        K = a.shape[0] if mode == "tn" else a.shape[1]
        t = _tile(K, tk)
        tks.append(t)
        nks.append(K // t)
        offs.append(sum(nks[:-1]))
    nk_tot = sum(nks)
    in_specs, args = [], []
    for (a, b), t, nk, off in zip(pairs, tks, nks, offs):
        def kk(k, off=off, nk=nk):
            return jnp.clip(k - off, 0, nk - 1)
        if mode == "tn":
            in_specs.append(pl.BlockSpec((t, tm), lambda i, j, k, kk=kk: (kk(k), i)))
        else:
            in_specs.append(pl.BlockSpec((tm, t), lambda i, j, k, kk=kk: (i, kk(k))))
        if mode == "nt":
            in_specs.append(pl.BlockSpec((tn, t), lambda i, j, k, kk=kk: (j, kk(k))))
        else:
            in_specs.append(pl.BlockSpec((t, tn), lambda i, j, k, kk=kk: (kk(k), j)))
        args += [a, b]
    if res is not None:
        in_specs.append(pl.BlockSpec((tm, tn), lambda i, j, k: (i, j)))
        args.append(res)
    npairs = len(pairs)

    def body(*refs):
        ab = refs[:2 * npairs]
        res_ref = refs[2 * npairs] if res is not None else None
        o_ref = refs[2 * npairs + (res is not None)]
        acc = refs[-1]
        k = pl.program_id(2)

        def finish(r):
            if scale is not None:
                r = r * scale
            if res_ref is not None:
                r = r + res_ref[...]
            o_ref[...] = r.astype(o_ref.dtype)

        if nk_tot == 1:
            finish(_dot(ab[0][...], ab[1][...], dims))
            return

        @pl.when(k == 0)
        def _():
            acc[...] = jnp.zeros_like(acc)

        for p in range(npairs):
            @pl.when(jnp.logical_and(k >= offs[p], k < offs[p] + nks[p]))
            def _(p=p):
                acc[...] += _dot(ab[2 * p][...], ab[2 * p + 1][...], dims)

        @pl.when(k == nk_tot - 1)
        def _():
            finish(acc[...])

    return pl.pallas_call(
        body, name=name, grid=(M // tm, N // tn, nk_tot), in_specs=in_specs,
        out_specs=pl.BlockSpec((tm, tn), lambda i, j, k: (i, j)),
        out_shape=jax.ShapeDtypeStruct((M, N), out_dtype),
        scratch_shapes=[pltpu.VMEM((tm, tn), F32)],
        compiler_params=_cp("parallel", "parallel", "arbitrary"))(*args)


def _rms_fwd(x, g, name):
    T, D = x.shape
    tm = _tile(T, ROW_TILE)

    def body(x_ref, g_ref, o_ref):
        xv = x_ref[...]
        r = lax.rsqrt(jnp.mean(xv * xv, axis=-1, keepdims=True) + EPS)
        o_ref[...] = (xv * r * g_ref[...]).astype(o_ref.dtype)

    return pl.pallas_call(
        body, name=name, grid=(T // tm,),
        in_specs=[pl.BlockSpec((tm, D), lambda i: (i, 0)), pl.BlockSpec((1, D), lambda i: (0, 0))],
        out_specs=pl.BlockSpec((tm, D), lambda i: (i, 0)),
        out_shape=jax.ShapeDtypeStruct((T, D), _MXU), compiler_params=_cp("parallel"))(x, g)


def _rms_bwd(da, x, g, dres, name):
    T, D = x.shape
    tm = _tile(T, ROW_TILE)

    def body(da_ref, x_ref, g_ref, dr_ref, dx_ref, dg_ref):
        @pl.when(pl.program_id(0) == 0)
        def _():
            dg_ref[...] = jnp.zeros_like(dg_ref)

        xv, dav = x_ref[...], da_ref[...]
        r = lax.rsqrt(jnp.mean(xv * xv, axis=-1, keepdims=True) + EPS)
        w = dav * g_ref[...]
        c = jnp.mean(w * xv, axis=-1, keepdims=True)
        dx_ref[...] = dr_ref[...] + (r * w - xv * (r * r * r) * c)
        dg_ref[...] += jnp.sum(dav * xv * r, axis=0, keepdims=True)

    row = pl.BlockSpec((tm, D), lambda i: (i, 0))
    vec = pl.BlockSpec((1, D), lambda i: (0, 0))
    return pl.pallas_call(
        body, name=name, grid=(T // tm,), in_specs=[row, row, vec, row], out_specs=[row, vec],
        out_shape=[jax.ShapeDtypeStruct((T, D), F32), jax.ShapeDtypeStruct((1, D), F32)],
        compiler_params=_cp("arbitrary"))(da, x, g, dres)


def _loss_grad(y, tgt):
    T, D = y.shape
    tm = _tile(T, ROW_TILE)

    def body(y_ref, t_ref, dy_ref, l_ref):
        @pl.when(pl.program_id(0) == 0)
        def _():
            l_ref[...] = jnp.zeros_like(l_ref)

        d = y_ref[...] - t_ref[...]
        dy_ref[...] = d * (1.0 / D)
        per_tok = jnp.mean(d * d, axis=-1, keepdims=True)
        l_ref[...] += 0.5 * jnp.sum(per_tok, axis=0, keepdims=True)

    row = pl.BlockSpec((tm, D), lambda i: (i, 0))
    return pl.pallas_call(
        body, name="loss_grad", grid=(T // tm,), in_specs=[row, row],
        out_specs=[row, pl.BlockSpec((1, 1), lambda i: (0, 0))],
        out_shape=[jax.ShapeDtypeStruct((T, D), F32), jax.ShapeDtypeStruct((1, 1), F32)],
        compiler_params=_cp("arbitrary"))(y, tgt)


def _ffn_up(a, wg, wu, name):
    T, D = a.shape
    Fp = wg.shape[1]
    tm, tn = _tile(T, ROW_TILE), _tile(Fp, 768)

    def body(a_ref, wg_ref, wu_ref, g_ref, u_ref, h_ref):
        av = a_ref[...]
        g = _dot(av, wg_ref[...])
        u = _dot(av, wu_ref[...])
        g_ref[...] = g.astype(g_ref.dtype)
        u_ref[...] = u.astype(u_ref.dtype)
        h_ref[...] = (g * _sigmoid(g) * u).astype(h_ref.dtype)

    wspec = pl.BlockSpec((D, tn), lambda j, i: (0, j))
    ospec = pl.BlockSpec((tm, tn), lambda j, i: (i, j))
    osh = jax.ShapeDtypeStruct((T, Fp), _MXU)
    return pl.pallas_call(
        body, name=name, grid=(Fp // tn, T // tm),
        in_specs=[pl.BlockSpec((tm, D), lambda j, i: (i, 0)), wspec, wspec],
        out_specs=[ospec, ospec, ospec], out_shape=[osh, osh, osh],
        compiler_params=_cp("parallel", "parallel"))(a, wg, wu)


def _ffn_bwd_hid(dxp, wd, gate, up, name):
    T, D = dxp.shape
    Fp = wd.shape[0]
    tm, tn = _tile(T, ROW_TILE), _tile(Fp, 768)

    def body(dx_ref, wd_ref, g_ref, u_ref, dg_ref, du_ref):
        dh = 0.5 * _dot(dx_ref[...], wd_ref[...], _NT)
        g = g_ref[...].astype(F32)
        u = u_ref[...].astype(F32)
        s = _sigmoid(g)
        du_ref[...] = (dh * (g * s)).astype(du_ref.dtype)
        dg_ref[...] = (dh * u * (s * (1.0 + g * (1.0 - s)))).astype(dg_ref.dtype)

    tspec = pl.BlockSpec((tm, tn), lambda j, i: (i, j))
    osh = jax.ShapeDtypeStruct((T, Fp), _MXU)
    return pl.pallas_call(
        body, name=name, grid=(Fp // tn, T // tm),
        in_specs=[pl.BlockSpec((tm, D), lambda j, i: (i, 0)), pl.BlockSpec((tn, D), lambda j, i: (j, 0)), tspec, tspec],
        out_specs=[tspec, tspec], out_shape=[osh, osh],
        compiler_params=_cp("parallel", "parallel"))(dxp, wd, gate, up)


def _conf_fwd(zc, dw, b, lng, lnb, name):
    T = zc.shape[0]
    tm = _tile(T, ROW_TILE)
    r = tm // CONV_HALO

    def body(z_ref, zh_ref, dw_ref, b_ref, g_ref, lb_ref, u1_ref, act_ref, ext):
        i = pl.program_id(0)
        cur = z_ref[...]
        ext[pl.ds(CONV_HALO, tm), :] = cur[:, :CH] * _sigmoid(cur[:, CH:])
        hal = zh_ref[...]
        ext[pl.ds(0, CONV_HALO), :] = jnp.where(i > 0, hal[:, :CH] * _sigmoid(hal[:, CH:]), 0.0)
        acc = jnp.zeros((tm, CH), F32)
        for k in range(CONV_K):
            acc = acc + dw_ref[pl.ds(k, 1), :] * ext[pl.ds(CONV_HALO - (CONV_K - 1) + k, tm), :]
        u1 = acc + b_ref[...]
        u1_ref[...] = u1
        mu = jnp.mean(u1, axis=-1, keepdims=True)
        var = jnp.mean(jnp.square(u1 - mu), axis=-1, keepdims=True)
        u2 = (u1 - mu) * lax.rsqrt(var + EPS) * g_ref[...] + lb_ref[...]
        act_ref[...] = u2 * _sigmoid(u2)

    vec = pl.BlockSpec((1, CH), lambda i: (0, 0))
    row = pl.BlockSpec((tm, CH), lambda i: (i, 0))
    osh = jax.ShapeDtypeStruct((T, CH), F32)
    return pl.pallas_call(
        body, name=name, grid=(T // tm,),
        in_specs=[pl.BlockSpec((tm, 2 * CH), lambda i: (i, 0)),
                  pl.BlockSpec((CONV_HALO, 2 * CH), lambda i: (jnp.maximum(i * r - 1, 0), 0)),
                  pl.BlockSpec((CONV_HALO, CH), lambda i: (0, 0)), vec, vec, vec],
        out_specs=[row, row], out_shape=[osh, osh],
        scratch_shapes=[pltpu.VMEM((tm + CONV_HALO, CH), F32)],
        compiler_params=_cp("parallel"))(zc, zc, dw, b, lng, lnb)


def _conf_bwd_ln(dact, u1, lng, lnb, name):
    T = u1.shape[0]
    tm = _tile(T, ROW_TILE)

    def body(da_ref, u_ref, g_ref, lb_ref, du_ref, sm_ref):
        @pl.when(pl.program_id(0) == 0)
        def _():
            sm_ref[...] = jnp.zeros_like(sm_ref)

        u1v = u_ref[...]
        mu = jnp.mean(u1v, axis=-1, keepdims=True)
        cen = u1v - mu
        rstd = lax.rsqrt(jnp.mean(cen * cen, axis=-1, keepdims=True) + EPS)
        y = cen * rstd
        u2 = y * g_ref[...] + lb_ref[...]
        s = _sigmoid(u2)
        du2 = da_ref[...] * (s * (1.0 + u2 * (1.0 - s)))
        dy = du2 * g_ref[...]
        du1 = rstd * (dy - jnp.mean(dy, axis=-1, keepdims=True) - y * jnp.mean(dy * y, axis=-1, keepdims=True))
        du_ref[...] = du1
        sm_ref[pl.ds(0, 1), :] += jnp.sum(du2 * y, axis=0, keepdims=True)
        sm_ref[pl.ds(1, 1), :] += jnp.sum(du2, axis=0, keepdims=True)
        sm_ref[pl.ds(2, 1), :] += jnp.sum(du1, axis=0, keepdims=True)

    vec = pl.BlockSpec((1, CH), lambda i: (0, 0))
    row = pl.BlockSpec((tm, CH), lambda i: (i, 0))
    return pl.pallas_call(
        body, name=name, grid=(T // tm,), in_specs=[row, row, vec, vec],
        out_specs=[row, pl.BlockSpec((8, CH), lambda i: (0, 0))],
        out_shape=[jax.ShapeDtypeStruct((T, CH), F32), jax.ShapeDtypeStruct((8, CH), F32)],
        compiler_params=_cp("arbitrary"))(dact, u1, lng, lnb)


def _conf_bwd_conv(zc, du1, dw, name):
    T = zc.shape[0]
    tm = _tile(T, ROW_TILE)
    r = tm // CONV_HALO
    nt = T // tm
    nh = T // CONV_HALO

    def body(z_ref, zh_ref, d_ref, dn_ref, dw_ref, dz_ref, ddw_ref, ext_u, ext_d):
        i = pl.program_id(0)

        @pl.when(i == 0)
        def _():
            ddw_ref[...] = jnp.zeros_like(ddw_ref)

        cur = z_ref[...]
        ca = cur[:, :CH]
        sg = _sigmoid(cur[:, CH:])
        ext_u[pl.ds(CONV_HALO, tm), :] = ca * sg
        hal = zh_ref[...]
        ext_u[pl.ds(0, CONV_HALO), :] = jnp.where(i > 0, hal[:, :CH] * _sigmoid(hal[:, CH:]), 0.0)
        d = d_ref[...]
        ext_d[pl.ds(0, tm), :] = d
        ext_d[pl.ds(tm, CONV_HALO), :] = jnp.where(i < nt - 1, dn_ref[...], 0.0)
        acc = jnp.zeros((tm, CH), F32)
        for k in range(CONV_K):
            acc = acc + dw_ref[pl.ds(k, 1), :] * ext_d[pl.ds(CONV_K - 1 - k, tm), :]
            ddw_ref[pl.ds(k, 1), :] += jnp.sum(
                d * ext_u[pl.ds(CONV_HALO - (CONV_K - 1) + k, tm), :], axis=0, keepdims=True)
        dz_ref[:, :CH] = (acc * sg).astype(dz_ref.dtype)
        dz_ref[:, CH:] = (acc * ca * sg * (1.0 - sg)).astype(dz_ref.dtype)

    return pl.pallas_call(
        body, name=name, grid=(nt,),
        in_specs=[pl.BlockSpec((tm, 2 * CH), lambda i: (i, 0)),
                  pl.BlockSpec((CONV_HALO, 2 * CH), lambda i: (jnp.maximum(i * r - 1, 0), 0)),
                  pl.BlockSpec((tm, CH), lambda i: (i, 0)),
                  pl.BlockSpec((CONV_HALO, CH), lambda i: (jnp.minimum((i + 1) * r, nh - 1), 0)),
                  pl.BlockSpec((CONV_HALO, CH), lambda i: (0, 0))],
        out_specs=[pl.BlockSpec((tm, 2 * CH), lambda i: (i, 0)), pl.BlockSpec((CONV_HALO, CH), lambda i: (0, 0))],
        out_shape=[jax.ShapeDtypeStruct((T, 2 * CH), _MXU), jax.ShapeDtypeStruct((CONV_HALO, CH), F32)],
        scratch_shapes=[pltpu.VMEM((tm + CONV_HALO, CH), F32), pltpu.VMEM((tm + CONV_HALO, CH), F32)],
        compiler_params=_cp("arbitrary"))(zc, zc, du1, du1, dw)


def _sc_fwd(zs, w, name):
    T = zs.shape[0]
    tm = _tile(T, ROW_TILE)
    r = tm // SC_HALO

    def body(z_ref, zh_ref, w_ref, act_ref, ext):
        i = pl.program_id(0)
        cur = z_ref[...]
        ext[pl.ds(SC_HALO, tm), :] = cur[:, CH:2 * CH] * cur[:, 2 * CH:]
        hal = zh_ref[...]
        ext[pl.ds(0, SC_HALO), :] = jnp.where(i > 0, hal[:, CH:2 * CH] * hal[:, 2 * CH:], 0.0)
        v1 = jnp.zeros((tm, CH), F32)
        for k in range(SC_K):
            v1 = v1 + w_ref[pl.ds(k, 1), :] * ext[pl.ds(SC_HALO - (SC_K - 1) + k, tm), :]
        act_ref[...] = cur[:, :CH] * v1

    return pl.pallas_call(
        body, name=name, grid=(T // tm,),
        in_specs=[pl.BlockSpec((tm, 3 * CH), lambda i: (i, 0)),
                  pl.BlockSpec((SC_HALO, 3 * CH), lambda i: (jnp.maximum(i * r - 1, 0), 0)),
                  pl.BlockSpec((SC_HALO, CH), lambda i: (0, 0))],
        out_specs=pl.BlockSpec((tm, CH), lambda i: (i, 0)),
        out_shape=jax.ShapeDtypeStruct((T, CH), F32),
        scratch_shapes=[pltpu.VMEM((tm + SC_HALO, CH), F32)],
        compiler_params=_cp("parallel"))(zs, zs, w)


def _sc_bwd(zs, dact, w, name):
    T = zs.shape[0]
    tm = _tile(T, ROW_TILE)
    r = tm // SC_HALO
    nt = T // tm
    nh = T // SC_HALO

    def body(z_ref, zh_ref, zn_ref, d_ref, dn_ref, w_ref, dz_ref, dw_ref, ext_v, ext_d):
        i = pl.program_id(0)

        @pl.when(i == 0)
        def _():
            dw_ref[...] = jnp.zeros_like(dw_ref)

        cur = z_ref[...]
        sb, sc, sx = cur[:, :CH], cur[:, CH:2 * CH], cur[:, 2 * CH:]
        ext_v[pl.ds(SC_HALO, tm), :] = sc * sx
        hal = zh_ref[...]
        ext_v[pl.ds(0, SC_HALO), :] = jnp.where(i > 0, hal[:, CH:2 * CH] * hal[:, 2 * CH:], 0.0)
        da = d_ref[...]
        dv1 = da * sb
        ext_d[pl.ds(0, tm), :] = dv1
        ext_d[pl.ds(tm, SC_HALO), :] = jnp.where(i < nt - 1, dn_ref[...] * zn_ref[...][:, :CH], 0.0)
        v1 = jnp.zeros((tm, CH), F32)
        dv0 = jnp.zeros((tm, CH), F32)
        for k in range(SC_K):
            shifted = ext_v[pl.ds(SC_HALO - (SC_K - 1) + k, tm), :]
            v1 = v1 + w_ref[pl.ds(k, 1), :] * shifted
            dv0 = dv0 + w_ref[pl.ds(k, 1), :] * ext_d[pl.ds(SC_K - 1 - k, tm), :]
            dw_ref[pl.ds(k, 1), :] += jnp.sum(dv1 * shifted, axis=0, keepdims=True)
        dz_ref[:, :CH] = (da * v1).astype(dz_ref.dtype)
        dz_ref[:, CH:2 * CH] = (dv0 * sx).astype(dz_ref.dtype)
        dz_ref[:, 2 * CH:] = (dv0 * sc).astype(dz_ref.dtype)

    return pl.pallas_call(
        body, name=name, grid=(nt,),
        in_specs=[pl.BlockSpec((tm, 3 * CH), lambda i: (i, 0)),
                  pl.BlockSpec((SC_HALO, 3 * CH), lambda i: (jnp.maximum(i * r - 1, 0), 0)),
                  pl.BlockSpec((SC_HALO, 3 * CH), lambda i: (jnp.minimum((i + 1) * r, nh - 1), 0)),
                  pl.BlockSpec((tm, CH), lambda i: (i, 0)),
                  pl.BlockSpec((SC_HALO, CH), lambda i: (jnp.minimum((i + 1) * r, nh - 1), 0)),
                  pl.BlockSpec((SC_HALO, CH), lambda i: (0, 0))],
        out_specs=[pl.BlockSpec((tm, 3 * CH), lambda i: (i, 0)), pl.BlockSpec((SC_HALO, CH), lambda i: (0, 0))],
        out_shape=[jax.ShapeDtypeStruct((T, 3 * CH), _MXU), jax.ShapeDtypeStruct((SC_HALO, CH), F32)],
        scratch_shapes=[pltpu.VMEM((tm + SC_HALO, CH), F32), pltpu.VMEM((tm + SC_HALO, CH), F32)],
        compiler_params=_cp("arbitrary"))(zs, zs, zs, dact, dact, w)


SWA_TQ = 256


def _t5_bucket_np(dist):
    max_exact = N_BUCKETS // 2
    d = np.maximum(dist, 1).astype(np.float32)
    large = max_exact + (np.log(d / np.float32(max_exact)) / np.float32(math.log(MAX_DISTANCE / max_exact))
                         * np.float32(N_BUCKETS - max_exact)).astype(np.int32)
    large = np.minimum(large, N_BUCKETS - 1)
    return np.where(dist < max_exact, dist, large).astype(np.int32)


def _swa_bucket_matrix(tq):
    dist = WINDOW + np.arange(tq)[:, None] - np.arange(tq + WINDOW)[None, :]
    ok = (dist >= 0) & (dist < WINDOW)
    return np.where(ok, _t5_bucket_np(np.maximum(dist, 0)), -1).astype(np.int32)


def _kv_expand_matrix():
    e = np.zeros((2 * HEAD, 4 * HEAD), np.float32)
    for h in range(4):
        for d in range(HEAD):
            e[(h // 2) * HEAD + d, h * HEAD + d] = 1.0
    return e


def _swa_bias(rel_bias, bucket, name):
    tq, tk = bucket.shape

    def body(rb_ref, bk_ref, o_ref):
        h = pl.program_id(0)
        bk = bk_ref[...]
        acc = jnp.full((tq, tk), NEG_INF, F32)
        for b in range(N_BUCKETS):
            acc = jnp.where(bk == b, rb_ref[b, h], acc)
        o_ref[0] = acc

    return pl.pallas_call(
        body, name=name, grid=(4,),
        in_specs=[pl.BlockSpec(memory_space=pltpu.SMEM), pl.BlockSpec((tq, tk), lambda h: (0, 0))],
        out_specs=pl.BlockSpec((1, tq, tk), lambda h: (h, 0, 0)),
        out_shape=jax.ShapeDtypeStruct((4, tq, tk), F32), compiler_params=_cp("parallel"))(rel_bias, bucket)


def _swa_probs(qh, kx, bm, first_col, sk):
    s = _dot(qh, kx, _NT) * (HEAD ** -0.5)
    col = lax.broadcasted_iota(jnp.int32, s.shape, 1)
    valid = (bm > 0.5 * NEG_INF) & (col >= first_col)
    s = jnp.where(valid, s + bm, NEG_INF)
    m = jnp.maximum(jnp.max(s, axis=-1, keepdims=True), sk)
    p = jnp.exp(s - m)
    den = jnp.sum(p, axis=-1, keepdims=True) + jnp.exp(sk - m)
    return p / den, m, den


def _swa_fwd(zw, gq, gk, sink, bias, expand, name):
    T = zw.shape[0]
    tq = bias.shape[1]
    r = tq // WINDOW

    def body(z_ref, zh_ref, gq_ref, gk_ref, sink_ref, b_ref, e_ref, o_ref, kext, vext):
        i = pl.program_id(0)
        cur = z_ref[...]
        qn, _ = _head_rms(cur[:, :4 * HEAD], gq_ref[...], 4)
        kc, _ = _head_rms(cur[:, 4 * HEAD:6 * HEAD], gk_ref[...], 2)
        hal = zh_ref[...]
        kp, _ = _head_rms(hal[:, :2 * HEAD], gk_ref[...], 2)
        kext[pl.ds(0, WINDOW), :] = kp
        kext[pl.ds(WINDOW, tq), :] = kc
        vext[pl.ds(0, WINDOW), :] = hal[:, 2 * HEAD:]
        vext[pl.ds(WINDOW, tq), :] = cur[:, 6 * HEAD:]
        kx = _dot(kext[...], e_ref[...]).astype(_MXU)
        vx = _dot(vext[...], e_ref[...]).astype(_MXU)
        first_col = jnp.where(i > 0, 0, WINDOW)
        out = jnp.zeros((tq, 4 * HEAD), F32)
        for h in range(4):
            mk = _lane_mask(4 * HEAD, h)
            qh = jnp.where(mk, qn, 0.0)
            pn, _, _ = _swa_probs(qh, kx, b_ref[h], first_col, sink_ref[0, h])
            out = jnp.where(mk, _dot(pn, vx), out)
        o_ref[...] = out

    return pl.pallas_call(
        body, name=name, grid=(T // tq,),
        in_specs=[pl.BlockSpec((tq, 8 * HEAD), lambda i: (i, 0)),
                  pl.BlockSpec((WINDOW, 4 * HEAD), lambda i: (jnp.maximum(i * r - 1, 0), 1)),
                  pl.BlockSpec((1, 4 * HEAD), lambda i: (0, 0)), pl.BlockSpec((1, 2 * HEAD), lambda i: (0, 0)),
                  pl.BlockSpec(memory_space=pltpu.SMEM),
                  pl.BlockSpec(bias.shape, lambda i: (0, 0, 0)),
                  pl.BlockSpec(expand.shape, lambda i: (0, 0))],
        out_specs=pl.BlockSpec((tq, 4 * HEAD), lambda i: (i, 0)),
        out_shape=jax.ShapeDtypeStruct((T, 4 * HEAD), F32),
        scratch_shapes=[pltpu.VMEM((tq + WINDOW, 2 * HEAD), F32), pltpu.VMEM((tq + WINDOW, 2 * HEAD), F32)],
        compiler_params=_cp("parallel"))(zw, zw, gq, gk, sink, bias, expand)


def _swa_bwd(zw, dact, gq, gk, sink, bias, bucket, expand, name):
    T = zw.shape[0]
    tq = bias.shape[1]
    tk = tq + WINDOW
    r = tq // WINDOW
    nt = T // tq
    nb = T // WINDOW
    scale = HEAD ** -0.5

    def body(z_ref, zh_ref, zn_ref, d_ref, dn_ref, gq_ref, gk_ref, sink_ref, b_ref, bk_ref, e_ref,
             dz_ref, dgq_ref, dgk_ref, dsk_ref, drb_ref, kext, vext, dk_s, dv_s, db_s):
        i = pl.program_id(0)

        @pl.when(i == 0)
        def _():
            dgq_ref[...] = jnp.zeros_like(dgq_ref)
            dgk_ref[...] = jnp.zeros_like(dgk_ref)
            dsk_ref[...] = jnp.zeros_like(dsk_ref)
            drb_ref[...] = jnp.zeros_like(drb_ref)
            db_s[...] = jnp.zeros_like(db_s)

        lane = lax.broadcasted_iota(jnp.int32, (1, LANE), 1)
        cur = z_ref[...]
        q_raw, k_raw = cur[:, :4 * HEAD], cur[:, 4 * HEAD:6 * HEAD]
        qn, q_r = _head_rms(q_raw, gq_ref[...], 4)
        kc, k_r = _head_rms(k_raw, gk_ref[...], 2)
        hal = zh_ref[...]
        kp, _ = _head_rms(hal[:, :2 * HEAD], gk_ref[...], 2)
        kext[pl.ds(0, WINDOW), :] = kp
        kext[pl.ds(WINDOW, tq), :] = kc
        vext[pl.ds(0, WINDOW), :] = hal[:, 2 * HEAD:]
        vext[pl.ds(WINDOW, tq), :] = cur[:, 6 * HEAD:]
        ev = e_ref[...]
        kx = _dot(kext[...], ev).astype(_MXU)
        vx = _dot(vext[...], ev).astype(_MXU)
        first_col = jnp.where(i > 0, 0, WINDOW)
        do = d_ref[...]
        dq = jnp.zeros((tq, 4 * HEAD), F32)
        dkx = jnp.zeros((tk, 4 * HEAD), F32)
        dvx = jnp.zeros((tk, 4 * HEAD), F32)
        dsk = jnp.zeros((1, LANE), F32)
        for h in range(4):
            mk = _lane_mask(4 * HEAD, h)
            qh = jnp.where(mk, qn, 0.0).astype(_MXU)
            sk = sink_ref[0, h]
            pn, m, den = _swa_probs(qh, kx, b_ref[h], first_col, sk)
            doh = jnp.where(mk, do, 0.0).astype(_MXU)
            dpn = _dot(doh, vx, _NT)
            delta = jnp.sum(pn * dpn, axis=-1, keepdims=True)
            ds = pn * (dpn - delta)
            psink = jnp.exp(sk - m) / den
            dsk = dsk + jnp.where(lane == h, jnp.sum(-psink * delta, axis=0, keepdims=True), 0.0)
            db_s[h] += ds
            dss = (ds * scale).astype(_MXU)
            dq = dq + jnp.where(mk, _dot(dss, kx), 0.0)
            dkx = dkx + _dot(dss, qh, _TN)
            dvx = dvx + _dot(pn, doh, _TN)
        dsk_ref[...] += dsk
        dk_ext = _exact_dot(dkx, ev, _NT, "a")
        dv_ext = _exact_dot(dvx, ev, _NT, "a")
        dk_s[...] = dk_ext[WINDOW:, :]
        dv_s[...] = dv_ext[WINDOW:, :]

        @pl.when(i < nt - 1)
        def _():
            nxt = zn_ref[...]
            q2, _ = _head_rms(nxt[:, :4 * HEAD], gq_ref[...], 4)
            k2n, _ = _head_rms(nxt[:, 4 * HEAD:6 * HEAD], gk_ref[...], 2)
            k2 = jnp.concatenate([kc[tq - WINDOW:, :], k2n], axis=0)
            v2 = jnp.concatenate([cur[tq - WINDOW:, 6 * HEAD:], nxt[:, 6 * HEAD:]], axis=0)
            k2x = _dot(k2, ev).astype(_MXU)
            v2x = _dot(v2, ev).astype(_MXU)
            do2 = dn_ref[...]
            dk2x = jnp.zeros((2 * WINDOW, 4 * HEAD), F32)
            dv2x = jnp.zeros((2 * WINDOW, 4 * HEAD), F32)
            for h in range(4):
                mk = _lane_mask(4 * HEAD, h)
                qh = jnp.where(mk, q2, 0.0).astype(_MXU)
                pn, _, _ = _swa_probs(qh, k2x, b_ref[h][:WINDOW, :2 * WINDOW], 0, sink_ref[0, h])
                doh = jnp.where(mk, do2, 0.0).astype(_MXU)
                dpn = _dot(doh, v2x, _NT)
                ds = pn * (dpn - jnp.sum(pn * dpn, axis=-1, keepdims=True))
                dk2x = dk2x + _dot((ds * scale).astype(_MXU), qh, _TN)
                dv2x = dv2x + _dot(pn, doh, _TN)
            dk_s[pl.ds(tq - WINDOW, WINDOW), :] += _exact_dot(dk2x, ev, _NT, "a")[:WINDOW, :]
            dv_s[pl.ds(tq - WINDOW, WINDOW), :] += _exact_dot(dv2x, ev, _NT, "a")[:WINDOW, :]

        dq_raw, dgq = _head_rms_bwd(dq, q_raw, q_r, gq_ref[...], 4)
        dk_raw, dgk = _head_rms_bwd(dk_s[...], k_raw, k_r, gk_ref[...], 2)
        dgq_ref[...] += dgq
        dgk_ref[...] += dgk
        dz_ref[:, :4 * HEAD] = dq_raw.astype(dz_ref.dtype)
        dz_ref[:, 4 * HEAD:6 * HEAD] = dk_raw.astype(dz_ref.dtype)
        dz_ref[:, 6 * HEAD:] = dv_s[...].astype(dz_ref.dtype)

        @pl.when(i == nt - 1)
        def _():
            bk = bk_ref[...]
            for b in range(N_BUCKETS):
                rowv = jnp.zeros((1, LANE), F32)
                for h in range(4):
                    s1 = jnp.sum(jnp.where(bk == b, db_s[h], 0.0), axis=0, keepdims=True)
                    rowv = jnp.where(lane == h, jnp.sum(s1, axis=1, keepdims=True), rowv)
                drb_ref[pl.ds(b, 1), :] = rowv

    const2 = lambda i: (0, 0)
    return pl.pallas_call(
        body, name=name, grid=(nt,),
        in_specs=[pl.BlockSpec((tq, 8 * HEAD), lambda i: (i, 0)),
                  pl.BlockSpec((WINDOW, 4 * HEAD), lambda i: (jnp.maximum(i * r - 1, 0), 1)),
                  pl.BlockSpec((WINDOW, 8 * HEAD), lambda i: (jnp.minimum((i + 1) * r, nb - 1), 0)),
                  pl.BlockSpec((tq, 4 * HEAD), lambda i: (i, 0)),
                  pl.BlockSpec((WINDOW, 4 * HEAD), lambda i: (jnp.minimum((i + 1) * r, nb - 1), 0)),
                  pl.BlockSpec((1, 4 * HEAD), const2), pl.BlockSpec((1, 2 * HEAD), const2),
                  pl.BlockSpec(memory_space=pltpu.SMEM),
                  pl.BlockSpec(bias.shape, lambda i: (0, 0, 0)),
                  pl.BlockSpec(bucket.shape, const2), pl.BlockSpec(expand.shape, const2)],
        out_specs=[pl.BlockSpec((tq, 8 * HEAD), lambda i: (i, 0)),
                   pl.BlockSpec((1, 4 * HEAD), const2), pl.BlockSpec((1, 2 * HEAD), const2),
                   pl.BlockSpec((1, LANE), const2), pl.BlockSpec((N_BUCKETS, LANE), const2)],
        out_shape=[jax.ShapeDtypeStruct((T, 8 * HEAD), _MXU), jax.ShapeDtypeStruct((1, 4 * HEAD), F32),
                   jax.ShapeDtypeStruct((1, 2 * HEAD), F32), jax.ShapeDtypeStruct((1, LANE), F32),
                   jax.ShapeDtypeStruct((N_BUCKETS, LANE), F32)],
        scratch_shapes=[pltpu.VMEM((tk, 2 * HEAD), F32), pltpu.VMEM((tk, 2 * HEAD), F32),
                        pltpu.VMEM((tq, 2 * HEAD), F32), pltpu.VMEM((tq, 2 * HEAD), F32),
                        pltpu.VMEM((4, tq, tk), F32)],
        compiler_params=_cp("arbitrary"))(zw, zw, zw, dact, dact, gq, gk, sink, bias, bucket, expand)


FOX_B = 512
FOX_TM = 256


def _tri(n, lower):
    m = np.tril(np.ones((n, n), np.float32)) if lower else np.triu(np.ones((n, n), np.float32))
    return m


def _log_sigmoid(x):
    return jnp.minimum(x, 0.0) - jnp.log1p(jnp.exp(-jnp.abs(x)))


def _fox_prep(zf, gq, gk, bf, name):
    T = zf.shape[0]
    tm = _tile(T, FOX_TM)
    lower = jnp.asarray(_tri(tm, True), _MXU)

    def body(z_ref, gq_ref, gk_ref, bf_ref, l_ref, q_ref, k_ref, v_ref, f_ref, ft_ref, carry):
        @pl.when(pl.program_id(0) == 0)
        def _():
            carry[...] = jnp.zeros_like(carry)

        z = z_ref[...]
        q, _ = _head_rms(z[:, :CH], gq_ref[...], 4)
        k, _ = _head_rms(z[:, CH:2 * CH], gk_ref[...], 4)
        q_ref[...] = q.astype(q_ref.dtype)
        k_ref[...] = k.astype(k_ref.dtype)
        v_ref[...] = z[:, 2 * CH:3 * CH].astype(v_ref.dtype)
        lane = lax.broadcasted_iota(jnp.int32, (1, LANE), 1)
        lf = jnp.where(lane < 4, _log_sigmoid(z[:, 3 * CH:] + bf_ref[...]), 0.0)
        fv = _exact_dot(l_ref[...], lf, _NN, "b") + carry[pl.ds(0, 1), :]
        f_ref[...] = fv
        ft_ref[...] = fv.T
        carry[pl.ds(0, 1), :] = f_ref[pl.ds(tm - 1, 1), :]

    row = pl.BlockSpec((tm, CH), lambda i: (i, 0))
    vec = pl.BlockSpec((1, CH), lambda i: (0, 0))
    qsh = jax.ShapeDtypeStruct((T, CH), _MXU)
    return pl.pallas_call(
        body, name=name, grid=(T // tm,),
        in_specs=[pl.BlockSpec((tm, 3 * CH + LANE), lambda i: (i, 0)), vec, vec,
                  pl.BlockSpec((1, LANE), lambda i: (0, 0)), pl.BlockSpec((tm, tm), lambda i: (0, 0))],
        out_specs=[row, row, row, pl.BlockSpec((tm, LANE), lambda i: (i, 0)), pl.BlockSpec((LANE, tm), lambda i: (0, i))],
        out_shape=[qsh, qsh, qsh, jax.ShapeDtypeStruct((T, LANE), F32), jax.ShapeDtypeStruct((LANE, T), F32)],
        scratch_shapes=[pltpu.VMEM((8, LANE), F32)],
        compiler_params=_cp("arbitrary"))(zf, gq, gk, bf, lower)


def _lane_col(x, h):
    lane = lax.broadcasted_iota(jnp.int32, (1, x.shape[-1]), 1)
    return jnp.sum(jnp.where(lane == h, x, 0.0), axis=-1, keepdims=True)


def _fox_scores(qh, k, fq, ft_ref, h, qi, ki, B):
    s = _dot(qh, k, _NT) * (HEAD ** -0.5)
    s = s + (fq - ft_ref[pl.ds(h, 1), :])
    row = qi * B + lax.broadcasted_iota(jnp.int32, s.shape, 0)
    col = ki * B + lax.broadcasted_iota(jnp.int32, s.shape, 1)
    return jnp.where(col <= row, s, NEG_INF)


def _fox_fwd(q, k, v, f, ft, name):
    T = q.shape[0]
    B = _tile(T, FOX_B)
    n = T // B

    def body(q_ref, k_ref, v_ref, f_ref, ft_ref, o_ref, lse_ref, m_s, l_s, acc):
        qi, ki = pl.program_id(0), pl.program_id(1)

        @pl.when(ki == 0)
        def _():
            m_s[...] = jnp.full_like(m_s, NEG_INF)
            l_s[...] = jnp.zeros_like(l_s)
            acc[...] = jnp.zeros_like(acc)

        @pl.when(ki <= qi)
        def _():
            qv, kv, vv, fv = q_ref[...], k_ref[...], v_ref[...], f_ref[...]
            for h in range(4):
                mk = _lane_mask(CH, h)
                qh = jnp.where(mk, qv, jnp.zeros_like(qv))
                s = _fox_scores(qh, kv, _lane_col(fv, h), ft_ref, h, qi, ki, B)
                m_old = m_s[h]
                m_new = jnp.maximum(m_old, jnp.max(s, axis=-1, keepdims=True))
                alpha = jnp.exp(m_old - m_new)
                p = jnp.exp(s - m_new)
                l_s[h] = alpha * l_s[h] + jnp.sum(p, axis=-1, keepdims=True)
                m_s[h] = m_new
                acc[...] = jnp.where(mk, acc[...] * alpha + _dot(p, vv), acc[...])

        @pl.when(ki == qi)
        def _():
            lane = lax.broadcasted_iota(jnp.int32, (1, LANE), 1)
            out = acc[...]
            lse = jnp.zeros((B, LANE), F32)
            for h in range(4):
                out = jnp.where(_lane_mask(CH, h), out / l_s[h], out)
                lse = jnp.where(lane == h, m_s[h] + jnp.log(l_s[h]), lse)
            o_ref[...] = out
            lse_ref[...] = lse

    qspec = pl.BlockSpec((B, CH), lambda qi, ki: (qi, 0))
    kspec = pl.BlockSpec((B, CH), lambda qi, ki: (jnp.minimum(ki, qi), 0))
    return pl.pallas_call(
        body, name=name, grid=(n, n),
        in_specs=[qspec, kspec, kspec, pl.BlockSpec((B, LANE), lambda qi, ki: (qi, 0)),
                  pl.BlockSpec((8, B), lambda qi, ki: (0, jnp.minimum(ki, qi)))],
        out_specs=[qspec, pl.BlockSpec((B, LANE), lambda qi, ki: (qi, 0))],
        out_shape=[jax.ShapeDtypeStruct((T, CH), F32), jax.ShapeDtypeStruct((T, LANE), F32)],
        scratch_shapes=[pltpu.VMEM((4, B, 1), F32), pltpu.VMEM((4, B, 1), F32), pltpu.VMEM((B, CH), F32)],
        compiler_params=_cp("parallel", "arbitrary"))(q, k, v, f, ft)


def _fox_delta(o, do, name):
    T = o.shape[0]
    tm = _tile(T, ROW_TILE)

    def body(o_ref, d_ref, out_ref):
        prod = o_ref[...] * d_ref[...]
        lane = lax.broadcasted_iota(jnp.int32, (1, LANE), 1)
        out = jnp.zeros((tm, LANE), F32)
        for h in range(4):
            s = jnp.sum(jnp.where(_lane_mask(CH, h), prod, 0.0), axis=-1, keepdims=True)
            out = jnp.where(lane == h, s, out)
        out_ref[...] = out

    row = pl.BlockSpec((tm, CH), lambda i: (i, 0))
    return pl.pallas_call(
        body, name=name, grid=(T // tm,), in_specs=[row, row],
        out_specs=pl.BlockSpec((tm, LANE), lambda i: (i, 0)),
        out_shape=jax.ShapeDtypeStruct((T, LANE), F32), compiler_params=_cp("parallel"))(o, do)


def _fox_bwd_dq(q, k, v, f, ft, lse, delta, do, name):
    T = q.shape[0]
    B = _tile(T, FOX_B)
    n = T // B

    def body(q_ref, k_ref, v_ref, f_ref, ft_ref, lse_ref, dl_ref, do_ref, dq_ref, dfq_ref, dq_s, df_s):
        qi, ki = pl.program_id(0), pl.program_id(1)

        @pl.when(ki == 0)
        def _():
            dq_s[...] = jnp.zeros_like(dq_s)
            df_s[...] = jnp.zeros_like(df_s)

        @pl.when(ki <= qi)
        def _():
            qv, kv, vv, fv = q_ref[...], k_ref[...], v_ref[...], f_ref[...]
            lsev, dlv, dov = lse_ref[...], dl_ref[...], do_ref[...]
            lane = lax.broadcasted_iota(jnp.int32, (1, LANE), 1)
            for h in range(4):
                mk = _lane_mask(CH, h)
                qh = jnp.where(mk, qv, jnp.zeros_like(qv))
                s = _fox_scores(qh, kv, _lane_col(fv, h), ft_ref, h, qi, ki, B)
                p = jnp.exp(s - _lane_col(lsev, h))
                doh = jnp.where(mk, dov, 0.0)
                ds = p * (_dot(doh, vv, _NT) - _lane_col(dlv, h))
                dq_s[...] += jnp.where(mk, _dot(ds * (HEAD ** -0.5), kv), 0.0)
                df_s[...] += jnp.where(lane == h, jnp.sum(ds, axis=-1, keepdims=True), 0.0)

        @pl.when(ki == qi)
        def _():
            dq_ref[...] = dq_s[...]
            dfq_ref[...] = df_s[...]

    qspec = pl.BlockSpec((B, CH), lambda qi, ki: (qi, 0))
    kspec = pl.BlockSpec((B, CH), lambda qi, ki: (jnp.minimum(ki, qi), 0))
    lspec = pl.BlockSpec((B, LANE), lambda qi, ki: (qi, 0))
    return pl.pallas_call(
        body, name=name, grid=(n, n),
        in_specs=[qspec, kspec, kspec, lspec, pl.BlockSpec((8, B), lambda qi, ki: (0, jnp.minimum(ki, qi))),
                  lspec, lspec, qspec],
        out_specs=[qspec, lspec],
        out_shape=[jax.ShapeDtypeStruct((T, CH), F32), jax.ShapeDtypeStruct((T, LANE), F32)],
        scratch_shapes=[pltpu.VMEM((B, CH), F32), pltpu.VMEM((B, LANE), F32)],
        compiler_params=_cp("parallel", "arbitrary"))(q, k, v, f, ft, lse, delta, do)


def _fox_bwd_dkv(q, k, v, f, ft, lse, delta, do, name):
    T = q.shape[0]
    B = _tile(T, FOX_B)
    n = T // B

    def body(q_ref, k_ref, v_ref, f_ref, ft_ref, lse_ref, dl_ref, do_ref, dk_ref, dv_ref, dft_ref, dk_s, dv_s, df_s):
        ki, qi = pl.program_id(0), pl.program_id(1)

        @pl.when(qi == 0)
        def _():
            dk_s[...] = jnp.zeros_like(dk_s)
            dv_s[...] = jnp.zeros_like(dv_s)
            df_s[...] = jnp.zeros_like(df_s)

        @pl.when(qi >= ki)
        def _():
            qv, kv, vv, fv = q_ref[...], k_ref[...], v_ref[...], f_ref[...]
            lsev, dlv, dov = lse_ref[...], dl_ref[...], do_ref[...]
            for h in range(4):
                mk = _lane_mask(CH, h)
                qh = jnp.where(mk, qv, jnp.zeros_like(qv))
                s = _fox_scores(qh, kv, _lane_col(fv, h), ft_ref, h, qi, ki, B)
                p = jnp.exp(s - _lane_col(lsev, h))
                doh = jnp.where(mk, dov, 0.0)
                ds = p * (_dot(doh, vv, _NT) - _lane_col(dlv, h))
                dv_s[...] += _dot(p, doh, _TN)
                dk_s[...] += _dot(ds * (HEAD ** -0.5), qh, _TN)
                df_s[pl.ds(h, 1), :] -= jnp.sum(ds, axis=0, keepdims=True)

        @pl.when(qi == n - 1)
        def _():
            dk_ref[...] = dk_s[...]
            dv_ref[...] = dv_s[...]
            dft_ref[...] = jnp.zeros_like(dft_ref)
            dft_ref[pl.ds(0, 8), :] = df_s[...]

    qspec = pl.BlockSpec((B, CH), lambda ki, qi: (jnp.maximum(qi, ki), 0))
    kspec = pl.BlockSpec((B, CH), lambda ki, qi: (ki, 0))
    lspec = pl.BlockSpec((B, LANE), lambda ki, qi: (jnp.maximum(qi, ki), 0))
    return pl.pallas_call(
        body, name=name, grid=(n, n),
        in_specs=[qspec, kspec, kspec, lspec, pl.BlockSpec((8, B), lambda ki, qi: (0, ki)), lspec, lspec, qspec],
        out_specs=[kspec, kspec, pl.BlockSpec((LANE, B), lambda ki, qi: (0, ki))],
        out_shape=[jax.ShapeDtypeStruct((T, CH), F32), jax.ShapeDtypeStruct((T, CH), F32),
                   jax.ShapeDtypeStruct((LANE, T), F32)],
        scratch_shapes=[pltpu.VMEM((B, CH), F32), pltpu.VMEM((B, CH), F32), pltpu.VMEM((8, B), F32)],
        compiler_params=_cp("parallel", "arbitrary"))(q, k, v, f, ft, lse, delta, do)


def _fox_post(zf, dqn, dkn, dv, dfq, dft, gq, gk, bf, name):
    T = zf.shape[0]
    tm = _tile(T, FOX_TM)
    nt = T // tm
    upper = jnp.asarray(_tri(tm, False), _MXU)

    def body(z_ref, dq_ref, dk_ref, dv_ref, dfq_ref, dft_ref, gq_ref, gk_ref, bf_ref, u_ref, dz_ref, sm_ref, carry, rc_s):
        @pl.when(pl.program_id(0) == 0)
        def _():
            carry[...] = jnp.zeros_like(carry)
            sm_ref[...] = jnp.zeros_like(sm_ref)

        z = z_ref[...]
        q_raw, k_raw = z[:, :CH], z[:, CH:2 * CH]
        _, q_r = _head_rms(q_raw, gq_ref[...], 4)
        _, k_r = _head_rms(k_raw, gk_ref[...], 4)
        dq, dgq = _head_rms_bwd(dq_ref[...], q_raw, q_r, gq_ref[...], 4)
        dk, dgk = _head_rms_bwd(dk_ref[...], k_raw, k_r, gk_ref[...], 4)
        df = dfq_ref[...] + dft_ref[...].T
        rc_s[...] = _exact_dot(u_ref[...], df, _NN, "b") + carry[pl.ds(0, 1), :]
        carry[pl.ds(0, 1), :] = rc_s[pl.ds(0, 1), :]
        lane = lax.broadcasted_iota(jnp.int32, (1, LANE), 1)
        x = z[:, 3 * CH:] + bf_ref[...]
        dff = jnp.where(lane < 4, rc_s[...] * _sigmoid(-x), 0.0)
        dz_ref[:, :CH] = dq.astype(dz_ref.dtype)
        dz_ref[:, CH:2 * CH] = dk.astype(dz_ref.dtype)
        dz_ref[:, 2 * CH:3 * CH] = dv_ref[...].astype(dz_ref.dtype)
        dz_ref[:, 3 * CH:] = dff.astype(dz_ref.dtype)
        sm_ref[pl.ds(0, 1), :] += dgq
        sm_ref[pl.ds(1, 1), :] += dgk
        sm_ref[pl.ds(2, 1), :LANE] += jnp.sum(dff, axis=0, keepdims=True)

    rev = lambda i: (nt - 1 - i, 0)
    row = pl.BlockSpec((tm, CH), rev)
    lrow = pl.BlockSpec((tm, LANE), rev)
    vec = pl.BlockSpec((1, CH), lambda i: (0, 0))
    return pl.pallas_call(
        body, name=name, grid=(nt,),
        in_specs=[pl.BlockSpec((tm, 3 * CH + LANE), rev), row, row, row, lrow,
                  pl.BlockSpec((LANE, tm), lambda i: (0, nt - 1 - i)), vec, vec,
                  pl.BlockSpec((1, LANE), lambda i: (0, 0)), pl.BlockSpec((tm, tm), lambda i: (0, 0))],
        out_specs=[pl.BlockSpec((tm, 3 * CH + LANE), rev), pl.BlockSpec((8, CH), lambda i: (0, 0))],
        out_shape=[jax.ShapeDtypeStruct((T, 3 * CH + LANE), _MXU), jax.ShapeDtypeStruct((8, CH), F32)],
        scratch_shapes=[pltpu.VMEM((8, LANE), F32), pltpu.VMEM((tm, LANE), F32)],
        compiler_params=_cp("arbitrary"))(zf, dqn, dkn, dv, dfq, dft, gq, gk, bf, upper)


def _merge_fwd(acts, zg, wbr, wout, x1, name):
    T, D = x1.shape
    tm = _tile(T, 256)

    def body(a0, a1, a2, a3, zg_ref, wbr_ref, wout_ref, x_ref, o_ref, mg_ref):
        merged = None
        for i, a_ref in enumerate((a0, a1, a2, a3)):
            term = _sigmoid(zg_ref[:, i * D:(i + 1) * D]) * _dot(a_ref[...], wbr_ref[i])
            merged = term if merged is None else merged + term
        mg_ref[...] = merged.astype(mg_ref.dtype)
        o_ref[...] = x_ref[...] + _dot(merged, wout_ref[...])

    arow = pl.BlockSpec((tm, CH), lambda i: (i, 0))
    xrow = pl.BlockSpec((tm, D), lambda i: (i, 0))
    return pl.pallas_call(
        body, name=name, grid=(T // tm,),
        in_specs=[arow, arow, arow, arow, pl.BlockSpec((tm, 4 * D), lambda i: (i, 0)),
                  pl.BlockSpec((4, CH, D), lambda i: (0, 0, 0)), pl.BlockSpec((D, D), lambda i: (0, 0)), xrow],
        out_specs=[xrow, xrow],
        out_shape=[jax.ShapeDtypeStruct((T, D), F32), jax.ShapeDtypeStruct((T, D), _MXU)],
        compiler_params=_cp("parallel"))(*acts, zg, wbr, wout, x1)


def _merge_bwd(dx2, acts, zg, wbr, wout, name):
    T, D = dx2.shape
    tm = _tile(T, 256)
    nt = T // tm

    def body(dx_ref, a0, a1, a2, a3, zg_ref, wbr_ref, wout_ref, d0, d1, d2, d3, dzg_ref, dw_ref, dw_s):
        i = pl.program_id(0)

        @pl.when(i == 0)
        def _():
            dw_s[...] = jnp.zeros_like(dw_s)

        dm = _dot(dx_ref[...], wout_ref[...], _NT)
        for b, (a_ref, d_ref) in enumerate(((a0, d0), (a1, d1), (a2, d2), (a3, d3))):
            av = a_ref[...].astype(_MXU)
            g = _sigmoid(zg_ref[:, b * D:(b + 1) * D])
            p = _dot(av, wbr_ref[b])
            dzg_ref[:, b * D:(b + 1) * D] = (dm * p * (g * (1.0 - g))).astype(dzg_ref.dtype)
            dp = (dm * g).astype(_MXU)
            d_ref[...] = _dot(dp, wbr_ref[b], _NT)
            dw_s[b] += _dot(av, dp, _TN)

        @pl.when(i == nt - 1)
        def _():
            dw_ref[...] = dw_s[...].astype(dw_ref.dtype)

    arow = pl.BlockSpec((tm, CH), lambda i: (i, 0))
    xrow = pl.BlockSpec((tm, D), lambda i: (i, 0))
    grow = pl.BlockSpec((tm, 4 * D), lambda i: (i, 0))
    wspec = pl.BlockSpec((4, CH, D), lambda i: (0, 0, 0))
    ash = jax.ShapeDtypeStruct((T, CH), F32)
    return pl.pallas_call(
        body, name=name, grid=(nt,),
        in_specs=[xrow, arow, arow, arow, arow, grow, wspec, pl.BlockSpec((D, D), lambda i: (0, 0))],
        out_specs=[arow, arow, arow, arow, grow, wspec],
        out_shape=[ash, ash, ash, ash, jax.ShapeDtypeStruct((T, 4 * D), _MXU), jax.ShapeDtypeStruct((4, CH, D), _MXU)],
        scratch_shapes=[pltpu.VMEM((4, CH, D), F32)],
        compiler_params=_cp("arbitrary"))(dx2, *acts, zg, wbr, wout)


def _rows_2d(a):
    return a.reshape((-1, a.shape[-1])) if a.ndim > 1 else a.reshape((1, -1))


def _row_tile(rows, cols, n_bufs):
    padded = -(-cols // LANE) * LANE
    cap = max(8, (VMEM_LIMIT // 3) // (2 * n_bufs * 4 * padded))
    return _tile(rows, cap, 8)


def _sum8(recv, name):
    shape = recv.shape[1:]
    r2 = recv.reshape((N_DEV, -1, shape[-1]))
    rows, cols = r2.shape[1:]
    tr = _row_tile(rows, cols, N_DEV // 2 + 1)

    def body(r_ref, o_ref):
        acc = r_ref[0].astype(F32)
        for d in range(1, N_DEV):
            acc = acc + r_ref[d].astype(F32)
        o_ref[...] = acc

    out = pl.pallas_call(
        body, name=name, grid=(rows // tr,),
        in_specs=[pl.BlockSpec((N_DEV, tr, cols), lambda i: (0, i, 0))],
        out_specs=pl.BlockSpec((tr, cols), lambda i: (i, 0)),
        out_shape=jax.ShapeDtypeStruct((rows, cols), F32), compiler_params=_cp("parallel"))(r2)
    return out.reshape(shape)


def _adamw(w, g, m, v, name):
    shape = w.shape
    w2, g2, m2, v2 = (_rows_2d(a) for a in (w, g, m, v))
    rows, cols = w2.shape
    tr = _row_tile(rows, cols, 7)

    def body(w_ref, g_ref, m_ref, v_ref, d_ref, nm_ref, nv_ref):
        gv = g_ref[...]
        nm = ADAM_B1 * m_ref[...] + (1.0 - ADAM_B1) * gv
        nv = ADAM_B2 * v_ref[...] + (1.0 - ADAM_B2) * jnp.square(gv)
        m_hat = nm / (1.0 - ADAM_B1 ** ADAM_STEP)
        v_hat = nv / (1.0 - ADAM_B2 ** ADAM_STEP)
        d_ref[...] = -ADAM_LR * (m_hat / (jnp.sqrt(v_hat) + ADAM_EPS) + ADAM_WD * w_ref[...])
        nm_ref[...] = nm
        nv_ref[...] = nv

    spec = pl.BlockSpec((tr, cols), lambda i: (i, 0))
    osh = jax.ShapeDtypeStruct((rows, cols), F32)
    outs = pl.pallas_call(
        body, name=name, grid=(rows // tr,), in_specs=[spec] * 4, out_specs=[spec] * 3,
        out_shape=[osh] * 3, compiler_params=_cp("parallel"))(w2, g2, m2, v2)
    return tuple(o.reshape(shape) for o in outs)


def _exchange(items, name):
    n = len(items)
    widths, out_shapes = [], []
    for src, kind, ax in items:
        if kind == "gather":
            w = src.shape[ax]
            shp = list(src.shape)
            shp[ax] = N_DEV * w
        else:
            w = src.shape[ax] // N_DEV
            shp = list(src.shape)
            shp[ax] = w
            shp = [N_DEV] + shp
        widths.append(w)
        out_shapes.append(jax.ShapeDtypeStruct(tuple(shp), src.dtype))

    def body(*refs):
        srcs, outs = refs[:n], refs[n:2 * n]
        send, recv, lsem = refs[2 * n:]
        x, y, c = lax.axis_index("x"), lax.axis_index("y"), lax.axis_index("c")
        me = 4 * x + 2 * y + c

        def peer(k):
            b = k + 1
            px = 1 - x if b & 4 else x
            py = 1 - y if b & 2 else y
            pc = 1 - c if b & 1 else c
            return (px, py, pc), 4 * px + 2 * py + pc

        def win(ref, ax, idx, w):
            return ref.at[tuple([slice(None)] * ax + [pl.ds(idx * w, w)])]

        def ends(j, mine, theirs):
            _, kind, ax = items[j]
            if kind == "gather":
                return srcs[j], win(outs[j], ax, mine, widths[j])
            return win(srcs[j], ax, theirs, widths[j]), outs[j].at[mine]

        local, sent = [], []
        for j in range(n):
            s, d = ends(j, me, me)
            cp = pltpu.make_async_copy(s, d, lsem.at[j])
            cp.start()
            local.append(cp)
            for k in range(N_DEV - 1):
                dev, pid = peer(k)
                s, d = ends(j, me, pid)
                cp = pltpu.make_async_remote_copy(s, d, send.at[j, k], recv.at[j, k], device_id=dev,
                                                  device_id_type=pl.DeviceIdType.MESH)
                cp.start()
                sent.append(cp)
        for j in range(n):
            for k in range(N_DEV - 1):
                dev, pid = peer(k)
                s, d = ends(j, pid, me)
                pltpu.make_async_remote_copy(s, d, send.at[j, k], recv.at[j, k], device_id=dev,
                                             device_id_type=pl.DeviceIdType.MESH).wait_recv()
        for cp in sent:
            cp.wait_send()
        for cp in local:
            cp.wait()

    hbm = pl.BlockSpec(memory_space=pl.ANY)
    return pl.pallas_call(
        body, name=name, in_specs=[hbm] * n, out_specs=[hbm] * n, out_shape=out_shapes,
        scratch_shapes=[pltpu.SemaphoreType.DMA((n, N_DEV - 1)), pltpu.SemaphoreType.DMA((n, N_DEV - 1)),
                        pltpu.SemaphoreType.DMA((n,))],
        compiler_params=pltpu.CompilerParams(has_side_effects=True))(*[it[0] for it in items])


def _pack(arrs):
    flat = jnp.concatenate([a.reshape(-1).astype(F32) for a in arrs])
    n = flat.shape[0]
    rows = -(-n // (8 * LANE)) * 8
    return jnp.pad(flat, (0, rows * LANE - n)).reshape(rows, LANE)


def _unpack(buf, shapes):
    flat = buf.reshape(-1)
    out, off = [], 0
    for s in shapes:
        sz = int(np.prod(s))
        out.append(flat[off:off + sz].reshape(s))
        off += sz
    return out


def _pad_axis(a, axis, size):
    pad = [(0, 0)] * a.ndim
    pad[axis] = (0, size - a.shape[axis])
    return jnp.pad(a, pad)


def _ffn_forward(x, g, wg, wu, wd, tag):
    a = _rms_fwd(x, g, f"{tag}_rms")
    gate, up, hid = _ffn_up(a, wg, wu, f"{tag}_up")
    out = _mm([(hid, wd)], "nn", F32, f"{tag}_down", scale=0.5, res=x)
    return out, (x, a, gate, up, hid)


def _ffn_backward(dxp, saved, g, wg, wu, wd, tag):
    x, a, gate, up, hid = saved
    d_gate, d_up = _ffn_bwd_hid(dxp, wd, gate, up, f"{tag}_bwd_hid")
    d_wd = _mm([(hid, dxp)], "tn", _MXU, f"{tag}_dwd", scale=0.5, tm=1024, tk=512)
    d_wg = _mm([(a, d_gate)], "tn", _MXU, f"{tag}_dwg", tm=1024, tk=512)
    d_wu = _mm([(a, d_up)], "tn", _MXU, f"{tag}_dwu", tm=1024, tk=512)
    d_a = _mm([(d_gate, wg), (d_up, wu)], "nt", F32, f"{tag}_da")
    dx, dg = _rms_bwd(d_a, x, g, dxp, f"{tag}_rms_bwd")
    return dx, dg, d_wg, d_wu, d_wd


def _tile_vec(v, reps):
    return jnp.tile(v.reshape(1, -1), (1, reps))


def _mixer_forward(x1, p, consts, tag):
    h = _rms_fwd(x1, p["mix_norm"], f"{tag}_rms")
    zg = _mm([(h, p["w_zg"])], "nn", F32, f"{tag}_in_g")
    zc = _mm([(h, p["w_conf"])], "nn", F32, f"{tag}_in_c")
    zs = _mm([(h, p["w_sc"])], "nn", F32, f"{tag}_in_s")
    zw = _mm([(h, p["w_swa"])], "nn", F32, f"{tag}_in_w")
    zf = _mm([(h, p["w_fox"])], "nn", F32, f"{tag}_in_f")
    u1, act_c = _conf_fwd(zc, p["conf_dw"], p["conf_dw_b"], p["conf_ln_g"], p["conf_ln_b"], f"{tag}_conf")
    act_s = _sc_fwd(zs, p["sc_conv"], f"{tag}_sc")
    act_w = _swa_fwd(zw, p["swa_q_norm"], p["swa_k_norm"], p["swa_sink"], consts["bias"], consts["expand"], f"{tag}_swa")
    fq, fk, fv, f, ft = _fox_prep(zf, p["fox_q_norm"], p["fox_k_norm"], p["b_forget"], f"{tag}_fox_prep")
    act_f, lse = _fox_fwd(fq, fk, fv, f, ft, f"{tag}_fox")
    acts = (act_c, act_s, act_w, act_f)
    x2, merged = _merge_fwd(acts, zg, p["w_br"], p["w_out"], x1, f"{tag}_merge")
    saved = (x1, h, zg, zc, zs, zw, zf, u1, acts, fq, fk, fv, f, ft, lse, merged)
    return x2, saved


def _mixer_backward(dx2, saved, p, consts, tag):
    x1, h, zg, zc, zs, zw, zf, u1, acts, fq, fk, fv, f, ft, lse, merged = saved
    g = {}
    g["w_out"] = _mm([(merged, dx2)], "tn", _MXU, f"{tag}_dwout", tm=1024, tk=512)
    d_c, d_s, d_w, d_f, dzg, g["w_br"] = _merge_bwd(dx2, acts, zg, p["w_br"], p["w_out"], f"{tag}_merge_bwd")
    du1, sm_c = _conf_bwd_ln(d_c, u1, p["conf_ln_g"], p["conf_ln_b"], f"{tag}_conf_bwd_ln")
    dzc, g["conf_dw"] = _conf_bwd_conv(zc, du1, p["conf_dw"], f"{tag}_conf_bwd_conv")
    g["conf_ln_g"], g["conf_ln_b"], g["conf_dw_b"] = sm_c[0], sm_c[1], sm_c[2]
    dzs, g["sc_conv"] = _sc_bwd(zs, d_s, p["sc_conv"], f"{tag}_sc_bwd")
    dzw, dgq, dgk, g["swa_sink"], g["rel_bias"] = _swa_bwd(
        zw, d_w, p["swa_q_norm"], p["swa_k_norm"], p["swa_sink"], consts["bias"], consts["bucket"], consts["expand"],
        f"{tag}_swa_bwd")
    g["swa_q_norm"], g["swa_k_norm"] = dgq, dgk
    delta = _fox_delta(acts[3], d_f, f"{tag}_fox_delta")
    dqn, dfq = _fox_bwd_dq(fq, fk, fv, f, ft, lse, delta, d_f, f"{tag}_fox_bwd_dq")
    dkn, dv, dft = _fox_bwd_dkv(fq, fk, fv, f, ft, lse, delta, d_f, f"{tag}_fox_bwd_dkv")
    dzf, sm_f = _fox_post(zf, dqn, dkn, dv, dfq, dft, p["fox_q_norm"], p["fox_k_norm"], p["b_forget"], f"{tag}_fox_post")
    g["fox_q_norm"], g["fox_k_norm"], g["b_forget"] = sm_f[0], sm_f[1], sm_f[2]
    parts = ((dzg, "w_zg"), (dzc, "w_conf"), (dzs, "w_sc"), (dzw, "w_swa"), (dzf, "w_fox"))
    for dz, wname in parts:
        g[wname] = _mm([(h, dz)], "tn", _MXU, f"{tag}_d{wname}", tm=1024, tk=512)
    dh = _mm([(dz, p[wname]) for dz, wname in parts], "nt", F32, f"{tag}_dh")
    dx1, g["mix_norm"] = _rms_bwd(dh, x1, p["mix_norm"], dx2, f"{tag}_rms_bwd")
    return dx1, g


W_NAMES = ['rel_bias', 'ffn1_norm', 'ffn1_w_gate', 'ffn1_w_up', 'ffn1_w_down', 'mix_norm', 'w_in', 'b_forget', 'conf_dw',
           'conf_dw_b', 'conf_ln_g', 'conf_ln_b', 'conf_w_out', 'sc_conv', 'sc_w_out', 'swa_q_norm', 'swa_k_norm',
           'swa_sink', 'swa_w_o', 'fox_q_norm', 'fox_k_norm', 'fox_w_o', 'w_out', 'ffn2_norm', 'ffn2_w_gate',
           'ffn2_w_up', 'ffn2_w_down']
SMALL = ['rel_bias', 'ffn1_norm', 'mix_norm', 'b_forget', 'conf_dw', 'conf_dw_b', 'conf_ln_g', 'conf_ln_b', 'sc_conv',
         'swa_q_norm', 'swa_k_norm', 'swa_sink', 'fox_q_norm', 'fox_k_norm', 'ffn2_norm']
BRANCH_W = ['conf_w_out', 'sc_w_out', 'swa_w_o', 'fox_w_o']
IN_CONF, IN_SC, IN_SWA, IN_FOX, IN_FF = (0, 512), (512, 1280), (1280, 1792), (1792, 2560), (2560, 2564)


def _step(w, m, v, x, loss_target):
    T, D = x.shape
    L = w["w_out"].shape[0]
    fs = w["ffn1_w_gate"].shape[2]
    fsp = -(-fs // LANE) * LANE
    dev = 4 * lax.axis_index("x") + 2 * lax.axis_index("y") + lax.axis_index("c")

    def cast(a):
        return a.astype(_MXU)

    win = w["w_in"]
    fox_cols = jnp.concatenate([win[..., IN_FOX[0]:IN_FF[1]],
                                jnp.zeros(win.shape[:2] + (LANE - (IN_FF[1] - IN_FF[0]),), win.dtype)], axis=-1)
    shards = {
        "ffn1_w_gate": (cast(_pad_axis(w["ffn1_w_gate"], 2, fsp)), 2),
        "ffn1_w_up": (cast(_pad_axis(w["ffn1_w_up"], 2, fsp)), 2),
        "ffn1_w_down": (cast(_pad_axis(w["ffn1_w_down"], 1, fsp)), 1),
        "ffn2_w_gate": (cast(_pad_axis(w["ffn2_w_gate"], 2, fsp)), 2),
        "ffn2_w_up": (cast(_pad_axis(w["ffn2_w_up"], 2, fsp)), 2),
        "ffn2_w_down": (cast(_pad_axis(w["ffn2_w_down"], 1, fsp)), 1),
        "w_zg": (cast(win[..., IN_FF[1]:]), 1),
        "w_conf": (cast(win[..., IN_CONF[0]:IN_CONF[1]]), 1),
        "w_sc": (cast(win[..., IN_SC[0]:IN_SC[1]]), 1),
        "w_swa": (cast(win[..., IN_SWA[0]:IN_SWA[1]]), 1),
        "w_fox": (cast(fox_cols), 1),
        "w_out": (cast(w["w_out"]), 1),
        "w_br": (cast(jnp.stack([w[n] for n in BRANCH_W], axis=1)), 3),
    }
    big = list(shards)
    conv_shard = jnp.concatenate([jnp.swapaxes(w["conf_dw"], 1, 2), jnp.swapaxes(w["sc_conv"], 1, 2)], axis=2)
    gathered = _exchange([(shards[n][0], "gather", shards[n][1]) for n in big] + [(conv_shard, "gather", 1)],
                         "gather_weights")
    full = dict(zip(big, gathered[:-1]))
    conv_full = jnp.swapaxes(gathered[-1], 1, 2)
    conf_dw_full = _pad_axis(conv_full[:, :CONV_K], 1, CONV_HALO)
    sc_conv_full = _pad_axis(conv_full[:, CONV_K:], 1, SC_HALO)

    bucket = jnp.asarray(_swa_bucket_matrix(min(SWA_TQ, T)))
    consts = {"bucket": bucket, "expand": jnp.asarray(_kv_expand_matrix(), _MXU),
              "bias": _swa_bias(w["rel_bias"], bucket, "swa_bias")}

    def layer_params(l):
        p = {n: full[n][l] for n in big}
        for n in ("ffn1_norm", "mix_norm", "ffn2_norm", "conf_dw_b", "conf_ln_g", "conf_ln_b"):
            p[n] = w[n][l].reshape(1, -1)
        p["conf_dw"], p["sc_conv"] = conf_dw_full[l], sc_conv_full[l]
        p["swa_q_norm"], p["fox_q_norm"] = _tile_vec(w["swa_q_norm"][l], 4), _tile_vec(w["fox_q_norm"][l], 4)
        p["swa_k_norm"], p["fox_k_norm"] = _tile_vec(w["swa_k_norm"][l], 2), _tile_vec(w["fox_k_norm"][l], 4)
        p["swa_sink"] = w["swa_sink"][l].reshape(1, 4)
        p["b_forget"] = _pad_axis(w["b_forget"][l].reshape(1, 4), 1, LANE)
        return p

    params = [layer_params(l) for l in range(L)]
    saved = []
    cur = x
    for l, p in enumerate(params):
        x1, s1 = _ffn_forward(cur, p["ffn1_norm"], p["ffn1_w_gate"], p["ffn1_w_up"], p["ffn1_w_down"], f"l{l}_ffn1")
        x2, s2 = _mixer_forward(x1, p, consts, f"l{l}_mix")
        cur, s3 = _ffn_forward(x2, p["ffn2_norm"], p["ffn2_w_gate"], p["ffn2_w_up"], p["ffn2_w_down"], f"l{l}_ffn2")
        saved.append((s1, s2, s3))
    dcur, loss_part = _loss_grad(cur, loss_target)

    grads = [None] * L
    for l in reversed(range(L)):
        p = params[l]
        s1, s2, s3 = saved[l]
        g = {}
        dcur, g["ffn2_norm"], g["ffn2_w_gate"], g["ffn2_w_up"], g["ffn2_w_down"] = _ffn_backward(
            dcur, s3, p["ffn2_norm"], p["ffn2_w_gate"], p["ffn2_w_up"], p["ffn2_w_down"], f"l{l}_ffn2")
        dcur, gm = _mixer_backward(dcur, s2, p, consts, f"l{l}_mix")
        g.update(gm)
        dcur, g["ffn1_norm"], g["ffn1_w_gate"], g["ffn1_w_up"], g["ffn1_w_down"] = _ffn_backward(
            dcur, s1, p["ffn1_norm"], p["ffn1_w_gate"], p["ffn1_w_up"], p["ffn1_w_down"], f"l{l}_ffn1")
        grads[l] = g
    grad_x = dcur

    stacked = {n: jnp.stack([grads[l][n] for l in range(L)]) for n in big}
    received = _exchange([(stacked[n], "scatter", shards[n][1]) for n in big], "scatter_grads")
    gsum = {n: _sum8(r, f"sum_{n}") for n, r in zip(big, received)}
    gw = {}
    for n in ("ffn1_w_gate", "ffn1_w_up", "ffn2_w_gate", "ffn2_w_up"):
        gw[n] = gsum[n][:, :, :fs]
    for n in ("ffn1_w_down", "ffn2_w_down"):
        gw[n] = gsum[n][:, :fs, :]
    gw["w_out"] = gsum["w_out"]
    for i, n in enumerate(BRANCH_W):
        gw[n] = gsum["w_br"][:, i]
    gw["w_in"] = jnp.concatenate([gsum["w_conf"], gsum["w_sc"], gsum["w_swa"],
                                  gsum["w_fox"][..., :IN_FF[1] - IN_FOX[0]], gsum["w_zg"]], axis=-1)

    def small_partial(n):
        per_layer = [grads[l][n] for l in range(L)]
        if n == "rel_bias":
            return sum(pl_[:, :4] for pl_ in per_layer)
        if n in ("swa_sink", "b_forget"):
            return jnp.stack([a.reshape(-1)[:4] for a in per_layer])
        if n in ("swa_q_norm", "fox_q_norm", "fox_k_norm"):
            return jnp.stack([a.reshape(4, HEAD).sum(0) for a in per_layer])
        if n == "swa_k_norm":
            return jnp.stack([a.reshape(2, HEAD).sum(0) for a in per_layer])
        if n == "conf_dw":
            return jnp.stack([a[:CONV_K] for a in per_layer])
        if n == "sc_conv":
            return jnp.stack([a[:SC_K] for a in per_layer])
        return jnp.stack([a.reshape(-1) for a in per_layer])

    partial = [small_partial(n) for n in SMALL]
    small_shapes = [a.shape for a in partial]
    all_parts = _exchange([(_pack(partial), "gather", 0)], "gather_small_grads")[0]
    rows = all_parts.shape[0] // N_DEV
    small_sum = _unpack(_sum8(all_parts.reshape(N_DEV, rows, LANE), "sum_small"), small_shapes)
    for n, a in zip(SMALL, small_sum):
        if n in ("conf_dw", "sc_conv"):
            cs = w[n].shape[2]
            a = lax.dynamic_slice_in_dim(a, dev * cs, cs, axis=2)
        gw[n] = a

    delta, new_m, new_v = {}, {}, {}
    for n in W_NAMES:
        if n not in SMALL:
            delta[n], new_m[n], new_v[n] = _adamw(w[n], gw[n], m[n], v[n], f"adamw_{n}")
    shapes = [w[n].shape for n in SMALL]
    outs = _adamw(_pack([w[n] for n in SMALL]), _pack([gw[n] for n in SMALL]), _pack([m[n] for n in SMALL]),
                  _pack([v[n] for n in SMALL]), "adamw_small")
    for res, out in zip((delta, new_m, new_v), outs):
        for n, a in zip(SMALL, _unpack(out, shapes)):
            res[n] = a

    loss = lax.psum(loss_part[0, 0], ("x", "y", "c"))
    return loss, grad_x, gw, delta, new_m, new_v


def kernel(x, rel_bias, ffn1_norm, ffn1_w_gate, ffn1_w_up, ffn1_w_down, mix_norm, w_in, b_forget, conf_dw, conf_dw_b, conf_ln_g, conf_ln_b, conf_w_out, sc_conv, sc_w_out, swa_q_norm, swa_k_norm, swa_sink, swa_w_o, fox_q_norm, fox_k_norm, fox_w_o, w_out, ffn2_norm, ffn2_w_gate, ffn2_w_up, ffn2_w_down, loss_target, m_rel_bias, m_ffn1_norm, m_ffn1_w_gate, m_ffn1_w_up, m_ffn1_w_down, m_mix_norm, m_w_in, m_b_forget, m_conf_dw, m_conf_dw_b, m_conf_ln_g, m_conf_ln_b, m_conf_w_out, m_sc_conv, m_sc_w_out, m_swa_q_norm, m_swa_k_norm, m_swa_sink, m_swa_w_o, m_fox_q_norm, m_fox_k_norm, m_fox_w_o, m_w_out, m_ffn2_norm, m_ffn2_w_gate, m_ffn2_w_up, m_ffn2_w_down, v_rel_bias, v_ffn1_norm, v_ffn1_w_gate, v_ffn1_w_up, v_ffn1_w_down, v_mix_norm, v_w_in, v_b_forget, v_conf_dw, v_conf_dw_b, v_conf_ln_g, v_conf_ln_b, v_conf_w_out, v_sc_conv, v_sc_w_out, v_swa_q_norm, v_swa_k_norm, v_swa_sink, v_swa_w_o, v_fox_q_norm, v_fox_k_norm, v_fox_w_o, v_w_out, v_ffn2_norm, v_ffn2_w_gate, v_ffn2_w_up, v_ffn2_w_down):
    args = locals()
    w = {n: args[n] for n in W_NAMES}
    m = {n: args["m_" + n] for n in W_NAMES}
    v = {n: args["v_" + n] for n in W_NAMES}
    T, D = x.shape[-2:]
    loss, grad_x, gw, delta, new_m, new_v = _step(w, m, v, x.reshape(T, D), loss_target.reshape(T, D))
    return (loss, grad_x.reshape(x.shape), *[gw[n] for n in W_NAMES], *[delta[n] for n in W_NAMES],
            *[new_m[n] for n in W_NAMES], *[new_v[n] for n in W_NAMES])
```

```python
import math

import numpy as np
import jax
import jax.numpy as jnp
from jax import lax
from jax.experimental import pallas as pl
from jax.experimental.pallas import tpu as pltpu

F32 = jnp.float32
_MXU = jnp.bfloat16
EPS = 1e-6
NEG_INF = -1e30
HEAD = 64
CH = 256
WINDOW = 128
CONV_K = 31
SC_K = 3
CONV_HALO = 32
SC_HALO = 8
N_BUCKETS = 32
MAX_DISTANCE = 128
N_DEV = 8
LANE = 128
ROW_TILE = 512
VMEM_LIMIT = 48 * 1024 * 1024
ADAM_LR, ADAM_B1, ADAM_B2, ADAM_EPS, ADAM_WD, ADAM_STEP = 0.001, 0.9, 0.999, 1e-08, 0.01, 10

_NN = (((1,), (0,)), ((), ()))
_NT = (((1,), (1,)), ((), ()))
_TN = (((0,), (0,)), ((), ()))


def _cp(*sem):
    return pltpu.CompilerParams(dimension_semantics=sem, vmem_limit_bytes=VMEM_LIMIT)


def _tile(n, pref, align=LANE):
    t = (min(n, pref) // align) * align
    while t >= align:
        if n % t == 0:
            return t
        t -= align
    return n


def _dot(a, b, dims=_NN):
    return lax.dot_general(a.astype(_MXU), b.astype(_MXU), dims, preferred_element_type=F32)


def _split3(x):
    hi = x.astype(_MXU)
    r1 = x - hi.astype(F32)
    mid = r1.astype(_MXU)
    lo = (r1 - mid.astype(F32)).astype(_MXU)
    return hi, mid, lo


def _exact_dot(a, b, dims, data):
    if data == "a":
        return sum(lax.dot_general(p, b.astype(_MXU), dims, preferred_element_type=F32) for p in _split3(a))
    return sum(lax.dot_general(a.astype(_MXU), p, dims, preferred_element_type=F32) for p in _split3(b))


def _sigmoid(x):
    return jax.nn.sigmoid(x)


def _lane_mask(width, h):
    lane = lax.broadcasted_iota(jnp.int32, (1, width), 1)
    return (lane >= h * HEAD) & (lane < (h + 1) * HEAD)


def _head_rms(x, g, nh):
    xx = x * x
    ms = jnp.zeros_like(x)
    for h in range(nh):
        mk = _lane_mask(x.shape[-1], h)
        s = jnp.sum(jnp.where(mk, xx, 0.0), axis=-1, keepdims=True) * (1.0 / HEAD)
        ms = jnp.where(mk, s, ms)
    r = lax.rsqrt(ms + EPS)
    return x * r * g, r


def _head_rms_bwd(dy, x, r, g, nh):
    w = dy * g
    wx = w * x
    c = jnp.zeros_like(x)
    for h in range(nh):
        mk = _lane_mask(x.shape[-1], h)
        s = jnp.sum(jnp.where(mk, wx, 0.0), axis=-1, keepdims=True) * (1.0 / HEAD)
        c = jnp.where(mk, s, c)
    dx = r * w - x * (r * r * r) * c
    dg = jnp.sum(dy * x * r, axis=0, keepdims=True)
    return dx, dg


def _mm(pairs, mode, out_dtype, name, scale=None, res=None, tm=1024, tn=1024, tk=1024):
    a0, b0 = pairs[0]
    M = a0.shape[1] if mode == "tn" else a0.shape[0]
    N = b0.shape[0] if mode == "nt" else b0.shape[1]
    tm, tn = _tile(M, tm), _tile(N, tn)
    dims = {"nn": _NN, "nt": _NT, "tn": _TN}[mode]
    tks, nks, offs = [], [], []
    for a, _ in pairs:
        K = a.shape[0] if mode == "tn" else a.shape[1]
        t = _tile(K, tk)
        tks.append(t)
        nks.append(K // t)
        offs.append(sum(nks[:-1]))
    nk_tot = sum(nks)
    in_specs, args = [], []
    for (a, b), t, nk, off in zip(pairs, tks, nks, offs):
        def kk(k, off=off, nk=nk):
            return jnp.clip(k - off, 0, nk - 1)
        if mode == "tn":
            in_specs.append(pl.BlockSpec((t, tm), lambda i, j, k, kk=kk: (kk(k), i)))
        else:
            in_specs.append(pl.BlockSpec((tm, t), lambda i, j, k, kk=kk: (i, kk(k))))
        if mode == "nt":
            in_specs.append(pl.BlockSpec((tn, t), lambda i, j, k, kk=kk: (j, kk(k))))
        else:
            in_specs.append(pl.BlockSpec((t, tn), lambda i, j, k, kk=kk: (kk(k), j)))
        args += [a, b]
    if res is not None:
        in_specs.append(pl.BlockSpec((tm, tn), lambda i, j, k: (i, j)))
        args.append(res)
    npairs = len(pairs)

    def body(*refs):
        ab = refs[:2 * npairs]
        res_ref = refs[2 * npairs] if res is not None else None
        o_ref = refs[2 * npairs + (res is not None)]
        acc = refs[-1]
        k = pl.program_id(2)

        def finish(r):
            if scale is not None:
                r = r * scale
            if res_ref is not None:
                r = r + res_ref[...]
            o_ref[...] = r.astype(o_ref.dtype)

        if nk_tot == 1:
            finish(_dot(ab[0][...], ab[1][...], dims))
            return

        @pl.when(k == 0)
        def _():
            acc[...] = jnp.zeros_like(acc)

        for p in range(npairs):
            @pl.when(jnp.logical_and(k >= offs[p], k < offs[p] + nks[p]))
            def _(p=p):
                acc[...] += _dot(ab[2 * p][...], ab[2 * p + 1][...], dims)

        @pl.when(k == nk_tot - 1)
        def _():
            finish(acc[...])

    return pl.pallas_call(
        body, name=name, grid=(M // tm, N // tn, nk_tot), in_specs=in_specs,
        out_specs=pl.BlockSpec((tm, tn), lambda i, j, k: (i, j)),
        out_shape=jax.ShapeDtypeStruct((M, N), out_dtype),
        scratch_shapes=[pltpu.VMEM((tm, tn), F32)],
        compiler_params=_cp("parallel", "parallel", "arbitrary"))(*args)


def _rms_fwd(x, g, name):
    T, D = x.shape
    tm = _tile(T, ROW_TILE)

    def body(x_ref, g_ref, o_ref):
        xv = x_ref[...]
        r = lax.rsqrt(jnp.mean(xv * xv, axis=-1, keepdims=True) + EPS)
        o_ref[...] = (xv * r * g_ref[...]).astype(o_ref.dtype)

    return pl.pallas_call(
        body, name=name, grid=(T // tm,),
        in_specs=[pl.BlockSpec((tm, D), lambda i: (i, 0)), pl.BlockSpec((1, D), lambda i: (0, 0))],
        out_specs=pl.BlockSpec((tm, D), lambda i: (i, 0)),
        out_shape=jax.ShapeDtypeStruct((T, D), _MXU), compiler_params=_cp("parallel"))(x, g)


def _rms_bwd(da, x, g, dres, name):
    T, D = x.shape
    tm = _tile(T, ROW_TILE)

    def body(da_ref, x_ref, g_ref, dr_ref, dx_ref, dg_ref):
        @pl.when(pl.program_id(0) == 0)
        def _():
            dg_ref[...] = jnp.zeros_like(dg_ref)

        xv, dav = x_ref[...], da_ref[...]
        r = lax.rsqrt(jnp.mean(xv * xv, axis=-1, keepdims=True) + EPS)
        w = dav * g_ref[...]
        c = jnp.mean(w * xv, axis=-1, keepdims=True)
        dx_ref[...] = dr_ref[...] + (r * w - xv * (r * r * r) * c)
        dg_ref[...] += jnp.sum(dav * xv * r, axis=0, keepdims=True)

    row = pl.BlockSpec((tm, D), lambda i: (i, 0))
    vec = pl.BlockSpec((1, D), lambda i: (0, 0))
    return pl.pallas_call(
        body, name=name, grid=(T // tm,), in_specs=[row, row, vec, row], out_specs=[row, vec],
        out_shape=[jax.ShapeDtypeStruct((T, D), F32), jax.ShapeDtypeStruct((1, D), F32)],
        compiler_params=_cp("arbitrary"))(da, x, g, dres)


def _loss_grad(y, tgt):
    T, D = y.shape
    tm = _tile(T, ROW_TILE)

    def body(y_ref, t_ref, dy_ref, l_ref):
        @pl.when(pl.program_id(0) == 0)
        def _():
            l_ref[...] = jnp.zeros_like(l_ref)

        d = y_ref[...] - t_ref[...]
        dy_ref[...] = d * (1.0 / D)
        per_tok = jnp.mean(d * d, axis=-1, keepdims=True)
        l_ref[...] += 0.5 * jnp.sum(per_tok, axis=0, keepdims=True)

    row = pl.BlockSpec((tm, D), lambda i: (i, 0))
    return pl.pallas_call(
        body, name="loss_grad", grid=(T // tm,), in_specs=[row, row],
        out_specs=[row, pl.BlockSpec((1, 1), lambda i: (0, 0))],
        out_shape=[jax.ShapeDtypeStruct((T, D), F32), jax.ShapeDtypeStruct((1, 1), F32)],
        compiler_params=_cp("arbitrary"))(y, tgt)


def _ffn_up(a, wg, wu, name):
    T, D = a.shape
    Fp = wg.shape[1]
    tm, tn = _tile(T, ROW_TILE), _tile(Fp, 768)

    def body(a_ref, wg_ref, wu_ref, g_ref, u_ref, h_ref):
        av = a_ref[...]
        g = _dot(av, wg_ref[...])
        u = _dot(av, wu_ref[...])
        g_ref[...] = g.astype(g_ref.dtype)
        u_ref[...] = u.astype(u_ref.dtype)
        h_ref[...] = (g * _sigmoid(g) * u).astype(h_ref.dtype)

    wspec = pl.BlockSpec((D, tn), lambda j, i: (0, j))
    ospec = pl.BlockSpec((tm, tn), lambda j, i: (i, j))
    osh = jax.ShapeDtypeStruct((T, Fp), _MXU)
    return pl.pallas_call(
        body, name=name, grid=(Fp // tn, T // tm),
        in_specs=[pl.BlockSpec((tm, D), lambda j, i: (i, 0)), wspec, wspec],
        out_specs=[ospec, ospec, ospec], out_shape=[osh, osh, osh],
        compiler_params=_cp("parallel", "parallel"))(a, wg, wu)


def _ffn_bwd_hid(dxp, wd, gate, up, name):
    T, D = dxp.shape
    Fp = wd.shape[0]
    tm, tn = _tile(T, ROW_TILE), _tile(Fp, 768)

    def body(dx_ref, wd_ref, g_ref, u_ref, dg_ref, du_ref):
        dh = 0.5 * _dot(dx_ref[...], wd_ref[...], _NT)
        g = g_ref[...].astype(F32)
        u = u_ref[...].astype(F32)
        s = _sigmoid(g)
        du_ref[...] = (dh * (g * s)).astype(du_ref.dtype)
        dg_ref[...] = (dh * u * (s * (1.0 + g * (1.0 - s)))).astype(dg_ref.dtype)

    tspec = pl.BlockSpec((tm, tn), lambda j, i: (i, j))
    osh = jax.ShapeDtypeStruct((T, Fp), _MXU)
    return pl.pallas_call(
        body, name=name, grid=(Fp // tn, T // tm),
        in_specs=[pl.BlockSpec((tm, D), lambda j, i: (i, 0)), pl.BlockSpec((tn, D), lambda j, i: (j, 0)), tspec, tspec],
        out_specs=[tspec, tspec], out_shape=[osh, osh],
        compiler_params=_cp("parallel", "parallel"))(dxp, wd, gate, up)


def _conf_fwd(zc, dw, b, lng, lnb, name):
    T = zc.shape[0]
    tm = _tile(T, ROW_TILE)
    r = tm // CONV_HALO

    def body(z_ref, zh_ref, dw_ref, b_ref, g_ref, lb_ref, u1_ref, act_ref, ext):
        i = pl.program_id(0)
        cur = z_ref[...]
        ext[pl.ds(CONV_HALO, tm), :] = cur[:, :CH] * _sigmoid(cur[:, CH:])
        hal = zh_ref[...]
        ext[pl.ds(0, CONV_HALO), :] = jnp.where(i > 0, hal[:, :CH] * _sigmoid(hal[:, CH:]), 0.0)
        acc = jnp.zeros((tm, CH), F32)
        for k in range(CONV_K):
            acc = acc + dw_ref[pl.ds(k, 1), :] * ext[pl.ds(CONV_HALO - (CONV_K - 1) + k, tm), :]
        u1 = acc + b_ref[...]
        u1_ref[...] = u1
        mu = jnp.mean(u1, axis=-1, keepdims=True)
        var = jnp.mean(jnp.square(u1 - mu), axis=-1, keepdims=True)
        u2 = (u1 - mu) * lax.rsqrt(var + EPS) * g_ref[...] + lb_ref[...]
        act_ref[...] = u2 * _sigmoid(u2)

    vec = pl.BlockSpec((1, CH), lambda i: (0, 0))
    row = pl.BlockSpec((tm, CH), lambda i: (i, 0))
    osh = jax.ShapeDtypeStruct((T, CH), F32)
    return pl.pallas_call(
        body, name=name, grid=(T // tm,),
        in_specs=[pl.BlockSpec((tm, 2 * CH), lambda i: (i, 0)),
                  pl.BlockSpec((CONV_HALO, 2 * CH), lambda i: (jnp.maximum(i * r - 1, 0), 0)),
                  pl.BlockSpec((CONV_HALO, CH), lambda i: (0, 0)), vec, vec, vec],
        out_specs=[row, row], out_shape=[osh, osh],
        scratch_shapes=[pltpu.VMEM((tm + CONV_HALO, CH), F32)],
        compiler_params=_cp("parallel"))(zc, zc, dw, b, lng, lnb)


def _conf_bwd_ln(dact, u1, lng, lnb, name):
    T = u1.shape[0]
    tm = _tile(T, ROW_TILE)

    def body(da_ref, u_ref, g_ref, lb_ref, du_ref, sm_ref):
        @pl.when(pl.program_id(0) == 0)
        def _():
            sm_ref[...] = jnp.zeros_like(sm_ref)

        u1v = u_ref[...]
        mu = jnp.mean(u1v, axis=-1, keepdims=True)
        cen = u1v - mu
        rstd = lax.rsqrt(jnp.mean(cen * cen, axis=-1, keepdims=True) + EPS)
        y = cen * rstd
        u2 = y * g_ref[...] + lb_ref[...]
        s = _sigmoid(u2)
        du2 = da_ref[...] * (s * (1.0 + u2 * (1.0 - s)))
        dy = du2 * g_ref[...]
        du1 = rstd * (dy - jnp.mean(dy, axis=-1, keepdims=True) - y * jnp.mean(dy * y, axis=-1, keepdims=True))
        du_ref[...] = du1
        sm_ref[pl.ds(0, 1), :] += jnp.sum(du2 * y, axis=0, keepdims=True)
        sm_ref[pl.ds(1, 1), :] += jnp.sum(du2, axis=0, keepdims=True)
        sm_ref[pl.ds(2, 1), :] += jnp.sum(du1, axis=0, keepdims=True)

    vec = pl.BlockSpec((1, CH), lambda i: (0, 0))
    row = pl.BlockSpec((tm, CH), lambda i: (i, 0))
    return pl.pallas_call(
        body, name=name, grid=(T // tm,), in_specs=[row, row, vec, vec],
        out_specs=[row, pl.BlockSpec((8, CH), lambda i: (0, 0))],
        out_shape=[jax.ShapeDtypeStruct((T, CH), F32), jax.ShapeDtypeStruct((8, CH), F32)],
        compiler_params=_cp("arbitrary"))(dact, u1, lng, lnb)


def _conf_bwd_conv(zc, du1, dw, name):
    T = zc.shape[0]
    tm = _tile(T, ROW_TILE)
    r = tm // CONV_HALO
    nt = T // tm
    nh = T // CONV_HALO

    def body(z_ref, zh_ref, d_ref, dn_ref, dw_ref, dz_ref, ddw_ref, ext_u, ext_d):
        i = pl.program_id(0)

        @pl.when(i == 0)
        def _():
            ddw_ref[...] = jnp.zeros_like(ddw_ref)

        cur = z_ref[...]
        ca = cur[:, :CH]
        sg = _sigmoid(cur[:, CH:])
        ext_u[pl.ds(CONV_HALO, tm), :] = ca * sg
        hal = zh_ref[...]
        ext_u[pl.ds(0, CONV_HALO), :] = jnp.where(i > 0, hal[:, :CH] * _sigmoid(hal[:, CH:]), 0.0)
        d = d_ref[...]
        ext_d[pl.ds(0, tm), :] = d
        ext_d[pl.ds(tm, CONV_HALO), :] = jnp.where(i < nt - 1, dn_ref[...], 0.0)
        acc = jnp.zeros((tm, CH), F32)
        for k in range(CONV_K):
            acc = acc + dw_ref[pl.ds(k, 1), :] * ext_d[pl.ds(CONV_K - 1 - k, tm), :]
            ddw_ref[pl.ds(k, 1), :] += jnp.sum(
                d * ext_u[pl.ds(CONV_HALO - (CONV_K - 1) + k, tm), :], axis=0, keepdims=True)
        dz_ref[:, :CH] = (acc * sg).astype(dz_ref.dtype)
        dz_ref[:, CH:] = (acc * ca * sg * (1.0 - sg)).astype(dz_ref.dtype)

    return pl.pallas_call(
        body, name=name, grid=(nt,),
        in_specs=[pl.BlockSpec((tm, 2 * CH), lambda i: (i, 0)),
                  pl.BlockSpec((CONV_HALO, 2 * CH), lambda i: (jnp.maximum(i * r - 1, 0), 0)),
                  pl.BlockSpec((tm, CH), lambda i: (i, 0)),
                  pl.BlockSpec((CONV_HALO, CH), lambda i: (jnp.minimum((i + 1) * r, nh - 1), 0)),
                  pl.BlockSpec((CONV_HALO, CH), lambda i: (0, 0))],
        out_specs=[pl.BlockSpec((tm, 2 * CH), lambda i: (i, 0)), pl.BlockSpec((CONV_HALO, CH), lambda i: (0, 0))],
        out_shape=[jax.ShapeDtypeStruct((T, 2 * CH), _MXU), jax.ShapeDtypeStruct((CONV_HALO, CH), F32)],
        scratch_shapes=[pltpu.VMEM((tm + CONV_HALO, CH), F32), pltpu.VMEM((tm + CONV_HALO, CH), F32)],
        compiler_params=_cp("arbitrary"))(zc, zc, du1, du1, dw)


def _sc_fwd(zs, w, name):
    T = zs.shape[0]
    tm = _tile(T, ROW_TILE)
    r = tm // SC_HALO

    def body(z_ref, zh_ref, w_ref, act_ref, ext):
        i = pl.program_id(0)
        cur = z_ref[...]
        ext[pl.ds(SC_HALO, tm), :] = cur[:, CH:2 * CH] * cur[:, 2 * CH:]
        hal = zh_ref[...]
        ext[pl.ds(0, SC_HALO), :] = jnp.where(i > 0, hal[:, CH:2 * CH] * hal[:, 2 * CH:], 0.0)
        v1 = jnp.zeros((tm, CH), F32)
        for k in range(SC_K):
            v1 = v1 + w_ref[pl.ds(k, 1), :] * ext[pl.ds(SC_HALO - (SC_K - 1) + k, tm), :]
        act_ref[...] = cur[:, :CH] * v1

    return pl.pallas_call(
        body, name=name, grid=(T // tm,),
        in_specs=[pl.BlockSpec((tm, 3 * CH), lambda i: (i, 0)),
                  pl.BlockSpec((SC_HALO, 3 * CH), lambda i: (jnp.maximum(i * r - 1, 0), 0)),
                  pl.BlockSpec((SC_HALO, CH), lambda i: (0, 0))],
        out_specs=pl.BlockSpec((tm, CH), lambda i: (i, 0)),
        out_shape=jax.ShapeDtypeStruct((T, CH), F32),
        scratch_shapes=[pltpu.VMEM((tm + SC_HALO, CH), F32)],
        compiler_params=_cp("parallel"))(zs, zs, w)


def _sc_bwd(zs, dact, w, name):
    T = zs.shape[0]
    tm = _tile(T, ROW_TILE)
    r = tm // SC_HALO
    nt = T // tm
    nh = T // SC_HALO

    def body(z_ref, zh_ref, zn_ref, d_ref, dn_ref, w_ref, dz_ref, dw_ref, ext_v, ext_d):
        i = pl.program_id(0)

        @pl.when(i == 0)
        def _():
            dw_ref[...] = jnp.zeros_like(dw_ref)

        cur = z_ref[...]
        sb, sc, sx = cur[:, :CH], cur[:, CH:2 * CH], cur[:, 2 * CH:]
        ext_v[pl.ds(SC_HALO, tm), :] = sc * sx
        hal = zh_ref[...]
        ext_v[pl.ds(0, SC_HALO), :] = jnp.where(i > 0, hal[:, CH:2 * CH] * hal[:, 2 * CH:], 0.0)
        da = d_ref[...]
        dv1 = da * sb
        ext_d[pl.ds(0, tm), :] = dv1
        ext_d[pl.ds(tm, SC_HALO), :] = jnp.where(i < nt - 1, dn_ref[...] * zn_ref[...][:, :CH], 0.0)
        v1 = jnp.zeros((tm, CH), F32)
        dv0 = jnp.zeros((tm, CH), F32)
        for k in range(SC_K):
            shifted = ext_v[pl.ds(SC_HALO - (SC_K - 1) + k, tm), :]
            v1 = v1 + w_ref[pl.ds(k, 1), :] * shifted
            dv0 = dv0 + w_ref[pl.ds(k, 1), :] * ext_d[pl.ds(SC_K - 1 - k, tm), :]
            dw_ref[pl.ds(k, 1), :] += jnp.sum(dv1 * shifted, axis=0, keepdims=True)
        dz_ref[:, :CH] = (da * v1).astype(dz_ref.dtype)
        dz_ref[:, CH:2 * CH] = (dv0 * sx).astype(dz_ref.dtype)
        dz_ref[:, 2 * CH:] = (dv0 * sc).astype(dz_ref.dtype)

    return pl.pallas_call(
        body, name=name, grid=(nt,),
        in_specs=[pl.BlockSpec((tm, 3 * CH), lambda i: (i, 0)),
                  pl.BlockSpec((SC_HALO, 3 * CH), lambda i: (jnp.maximum(i * r - 1, 0), 0)),
                  pl.BlockSpec((SC_HALO, 3 * CH), lambda i: (jnp.minimum((i + 1) * r, nh - 1), 0)),
                  pl.BlockSpec((tm, CH), lambda i: (i, 0)),
                  pl.BlockSpec((SC_HALO, CH), lambda i: (jnp.minimum((i + 1) * r, nh - 1), 0)),
                  pl.BlockSpec((SC_HALO, CH), lambda i: (0, 0))],
        out_specs=[pl.BlockSpec((tm, 3 * CH), lambda i: (i, 0)), pl.BlockSpec((SC_HALO, CH), lambda i: (0, 0))],
        out_shape=[jax.ShapeDtypeStruct((T, 3 * CH), _MXU), jax.ShapeDtypeStruct((SC_HALO, CH), F32)],
        scratch_shapes=[pltpu.VMEM((tm + SC_HALO, CH), F32), pltpu.VMEM((tm + SC_HALO, CH), F32)],
        compiler_params=_cp("arbitrary"))(zs, zs, zs, dact, dact, w)


SWA_TQ = 256


def _t5_bucket_np(dist):
    max_exact = N_BUCKETS // 2
    d = np.maximum(dist, 1).astype(np.float32)
    large = max_exact + (np.log(d / np.float32(max_exact)) / np.float32(math.log(MAX_DISTANCE / max_exact))
                         * np.float32(N_BUCKETS - max_exact)).astype(np.int32)
    large = np.minimum(large, N_BUCKETS - 1)
    return np.where(dist < max_exact, dist, large).astype(np.int32)


def _swa_bucket_matrix(tq):
    dist = WINDOW + np.arange(tq)[:, None] - np.arange(tq + WINDOW)[None, :]
    ok = (dist >= 0) & (dist < WINDOW)
    return np.where(ok, _t5_bucket_np(np.maximum(dist, 0)), -1).astype(np.int32)


def _kv_expand_matrix():
    e = np.zeros((2 * HEAD, 4 * HEAD), np.float32)
    for h in range(4):
        for d in range(HEAD):
            e[(h // 2) * HEAD + d, h * HEAD + d] = 1.0
    return e


def _swa_bias(rel_bias, bucket, name):
    tq, tk = bucket.shape

    def body(rb_ref, bk_ref, o_ref):
        h = pl.program_id(0)
        bk = bk_ref[...]
        acc = jnp.full((tq, tk), NEG_INF, F32)
        for b in range(N_BUCKETS):
            acc = jnp.where(bk == b, rb_ref[b, h], acc)
        o_ref[0] = acc

    return pl.pallas_call(
        body, name=name, grid=(4,),
        in_specs=[pl.BlockSpec(memory_space=pltpu.SMEM), pl.BlockSpec((tq, tk), lambda h: (0, 0))],
        out_specs=pl.BlockSpec((1, tq, tk), lambda h: (h, 0, 0)),
        out_shape=jax.ShapeDtypeStruct((4, tq, tk), F32), compiler_params=_cp("parallel"))(rel_bias, bucket)


def _swa_probs(qh, kx, bm, first_col, sk):
    s = _dot(qh, kx, _NT) * (HEAD ** -0.5)
    col = lax.broadcasted_iota(jnp.int32, s.shape, 1)
    valid = (bm > 0.5 * NEG_INF) & (col >= first_col)
    s = jnp.where(valid, s + bm, NEG_INF)
    m = jnp.maximum(jnp.max(s, axis=-1, keepdims=True), sk)
    p = jnp.exp(s - m)
    den = jnp.sum(p, axis=-1, keepdims=True) + jnp.exp(sk - m)
    return p / den, m, den


def _swa_fwd(zw, gq, gk, sink, bias, expand, name):
    T = zw.shape[0]
    tq = bias.shape[1]
    r = tq // WINDOW

    def body(z_ref, zh_ref, gq_ref, gk_ref, sink_ref, b_ref, e_ref, o_ref, kext, vext):
        i = pl.program_id(0)
        cur = z_ref[...]
        qn, _ = _head_rms(cur[:, :4 * HEAD], gq_ref[...], 4)
        kc, _ = _head_rms(cur[:, 4 * HEAD:6 * HEAD], gk_ref[...], 2)
        hal = zh_ref[...]
        kp, _ = _head_rms(hal[:, :2 * HEAD], gk_ref[...], 2)
        kext[pl.ds(0, WINDOW), :] = kp
        kext[pl.ds(WINDOW, tq), :] = kc
        vext[pl.ds(0, WINDOW), :] = hal[:, 2 * HEAD:]
        vext[pl.ds(WINDOW, tq), :] = cur[:, 6 * HEAD:]
        kx = _dot(kext[...], e_ref[...]).astype(_MXU)
        vx = _dot(vext[...], e_ref[...]).astype(_MXU)
        first_col = jnp.where(i > 0, 0, WINDOW)
        out = jnp.zeros((tq, 4 * HEAD), F32)
        for h in range(4):
            mk = _lane_mask(4 * HEAD, h)
            qh = jnp.where(mk, qn, 0.0)
            pn, _, _ = _swa_probs(qh, kx, b_ref[h], first_col, sink_ref[0, h])
            out = jnp.where(mk, _dot(pn, vx), out)
        o_ref[...] = out

    return pl.pallas_call(
        body, name=name, grid=(T // tq,),
        in_specs=[pl.BlockSpec((tq, 8 * HEAD), lambda i: (i, 0)),
                  pl.BlockSpec((WINDOW, 4 * HEAD), lambda i: (jnp.maximum(i * r - 1, 0), 1)),
                  pl.BlockSpec((1, 4 * HEAD), lambda i: (0, 0)), pl.BlockSpec((1, 2 * HEAD), lambda i: (0, 0)),
                  pl.BlockSpec(memory_space=pltpu.SMEM),
                  pl.BlockSpec(bias.shape, lambda i: (0, 0, 0)),
                  pl.BlockSpec(expand.shape, lambda i: (0, 0))],
        out_specs=pl.BlockSpec((tq, 4 * HEAD), lambda i: (i, 0)),
        out_shape=jax.ShapeDtypeStruct((T, 4 * HEAD), F32),
        scratch_shapes=[pltpu.VMEM((tq + WINDOW, 2 * HEAD), F32), pltpu.VMEM((tq + WINDOW, 2 * HEAD), F32)],
        compiler_params=_cp("parallel"))(zw, zw, gq, gk, sink, bias, expand)


def _swa_bwd(zw, dact, gq, gk, sink, bias, bucket, expand, name):
    T = zw.shape[0]
    tq = bias.shape[1]
    tk = tq + WINDOW
    r = tq // WINDOW
    nt = T // tq
    nb = T // WINDOW
    scale = HEAD ** -0.5

    def body(z_ref, zh_ref, zn_ref, d_ref, dn_ref, gq_ref, gk_ref, sink_ref, b_ref, bk_ref, e_ref,
             dz_ref, dgq_ref, dgk_ref, dsk_ref, drb_ref, kext, vext, dk_s, dv_s, db_s):
        i = pl.program_id(0)

        @pl.when(i == 0)
        def _():
            dgq_ref[...] = jnp.zeros_like(dgq_ref)
            dgk_ref[...] = jnp.zeros_like(dgk_ref)
            dsk_ref[...] = jnp.zeros_like(dsk_ref)
            drb_ref[...] = jnp.zeros_like(drb_ref)
            db_s[...] = jnp.zeros_like(db_s)

        lane = lax.broadcasted_iota(jnp.int32, (1, LANE), 1)
        cur = z_ref[...]
        q_raw, k_raw = cur[:, :4 * HEAD], cur[:, 4 * HEAD:6 * HEAD]
        qn, q_r = _head_rms(q_raw, gq_ref[...], 4)
        kc, k_r = _head_rms(k_raw, gk_ref[...], 2)
        hal = zh_ref[...]
        kp, _ = _head_rms(hal[:, :2 * HEAD], gk_ref[...], 2)
        kext[pl.ds(0, WINDOW), :] = kp
        kext[pl.ds(WINDOW, tq), :] = kc
        vext[pl.ds(0, WINDOW), :] = hal[:, 2 * HEAD:]
        vext[pl.ds(WINDOW, tq), :] = cur[:, 6 * HEAD:]
        ev = e_ref[...]
        kx = _dot(kext[...], ev).astype(_MXU)
        vx = _dot(vext[...], ev).astype(_MXU)
        first_col = jnp.where(i > 0, 0, WINDOW)
        do = d_ref[...]
        dq = jnp.zeros((tq, 4 * HEAD), F32)
        dkx = jnp.zeros((tk, 4 * HEAD), F32)
        dvx = jnp.zeros((tk, 4 * HEAD), F32)
        dsk = jnp.zeros((1, LANE), F32)
        for h in range(4):
            mk = _lane_mask(4 * HEAD, h)
            qh = jnp.where(mk, qn, 0.0).astype(_MXU)
            sk = sink_ref[0, h]
            pn, m, den = _swa_probs(qh, kx, b_ref[h], first_col, sk)
            doh = jnp.where(mk, do, 0.0).astype(_MXU)
            dpn = _dot(doh, vx, _NT)
            delta = jnp.sum(pn * dpn, axis=-1, keepdims=True)
            ds = pn * (dpn - delta)
            psink = jnp.exp(sk - m) / den
            dsk = dsk + jnp.where(lane == h, jnp.sum(-psink * delta, axis=0, keepdims=True), 0.0)
            db_s[h] += ds
            dss = (ds * scale).astype(_MXU)
            dq = dq + jnp.where(mk, _dot(dss, kx), 0.0)
            dkx = dkx + _dot(dss, qh, _TN)
            dvx = dvx + _dot(pn, doh, _TN)
        dsk_ref[...] += dsk
        dk_ext = _exact_dot(dkx, ev, _NT, "a")
        dv_ext = _exact_dot(dvx, ev, _NT, "a")
        dk_s[...] = dk_ext[WINDOW:, :]
        dv_s[...] = dv_ext[WINDOW:, :]

        @pl.when(i < nt - 1)
        def _():
            nxt = zn_ref[...]
            q2, _ = _head_rms(nxt[:, :4 * HEAD], gq_ref[...], 4)
            k2n, _ = _head_rms(nxt[:, 4 * HEAD:6 * HEAD], gk_ref[...], 2)
            k2 = jnp.concatenate([kc[tq - WINDOW:, :], k2n], axis=0)
            v2 = jnp.concatenate([cur[tq - WINDOW:, 6 * HEAD:], nxt[:, 6 * HEAD:]], axis=0)
            k2x = _dot(k2, ev).astype(_MXU)
            v2x = _dot(v2, ev).astype(_MXU)
            do2 = dn_ref[...]
            dk2x = jnp.zeros((2 * WINDOW, 4 * HEAD), F32)
            dv2x = jnp.zeros((2 * WINDOW, 4 * HEAD), F32)
            for h in range(4):
                mk = _lane_mask(4 * HEAD, h)
                qh = jnp.where(mk, q2, 0.0).astype(_MXU)
                pn, _, _ = _swa_probs(qh, k2x, b_ref[h][:WINDOW, :2 * WINDOW], 0, sink_ref[0, h])
                doh = jnp.where(mk, do2, 0.0).astype(_MXU)
                dpn = _dot(doh, v2x, _NT)
                ds = pn * (dpn - jnp.sum(pn * dpn, axis=-1, keepdims=True))
                dk2x = dk2x + _dot((ds * scale).astype(_MXU), qh, _TN)
                dv2x = dv2x + _dot(pn, doh, _TN)
            dk_s[pl.ds(tq - WINDOW, WINDOW), :] += _exact_dot(dk2x, ev, _NT, "a")[:WINDOW, :]
            dv_s[pl.ds(tq - WINDOW, WINDOW), :] += _exact_dot(dv2x, ev, _NT, "a")[:WINDOW, :]

        dq_raw, dgq = _head_rms_bwd(dq, q_raw, q_r, gq_ref[...], 4)
        dk_raw, dgk = _head_rms_bwd(dk_s[...], k_raw, k_r, gk_ref[...], 2)
        dgq_ref[...] += dgq
        dgk_ref[...] += dgk
        dz_ref[:, :4 * HEAD] = dq_raw.astype(dz_ref.dtype)
        dz_ref[:, 4 * HEAD:6 * HEAD] = dk_raw.astype(dz_ref.dtype)
        dz_ref[:, 6 * HEAD:] = dv_s[...].astype(dz_ref.dtype)

        @pl.when(i == nt - 1)
        def _():
            bk = bk_ref[...]
            for b in range(N_BUCKETS):
                rowv = jnp.zeros((1, LANE), F32)
                for h in range(4):
                    s1 = jnp.sum(jnp.where(bk == b, db_s[h], 0.0), axis=0, keepdims=True)
                    rowv = jnp.where(lane == h, jnp.sum(s1, axis=1, keepdims=True), rowv)
                drb_ref[pl.ds(b, 1), :] = rowv

    const2 = lambda i: (0, 0)
    return pl.pallas_call(
        body, name=name, grid=(nt,),
        in_specs=[pl.BlockSpec((tq, 8 * HEAD), lambda i: (i, 0)),
                  pl.BlockSpec((WINDOW, 4 * HEAD), lambda i: (jnp.maximum(i * r - 1, 0), 1)),
                  pl.BlockSpec((WINDOW, 8 * HEAD), lambda i: (jnp.minimum((i + 1) * r, nb - 1), 0)),
                  pl.BlockSpec((tq, 4 * HEAD), lambda i: (i, 0)),
                  pl.BlockSpec((WINDOW, 4 * HEAD), lambda i: (jnp.minimum((i + 1) * r, nb - 1), 0)),
                  pl.BlockSpec((1, 4 * HEAD), const2), pl.BlockSpec((1, 2 * HEAD), const2),
                  pl.BlockSpec(memory_space=pltpu.SMEM),
                  pl.BlockSpec(bias.shape, lambda i: (0, 0, 0)),
                  pl.BlockSpec(bucket.shape, const2), pl.BlockSpec(expand.shape, const2)],
        out_specs=[pl.BlockSpec((tq, 8 * HEAD), lambda i: (i, 0)),
                   pl.BlockSpec((1, 4 * HEAD), const2), pl.BlockSpec((1, 2 * HEAD), const2),
                   pl.BlockSpec((1, LANE), const2), pl.BlockSpec((N_BUCKETS, LANE), const2)],
        out_shape=[jax.ShapeDtypeStruct((T, 8 * HEAD), _MXU), jax.ShapeDtypeStruct((1, 4 * HEAD), F32),
                   jax.ShapeDtypeStruct((1, 2 * HEAD), F32), jax.ShapeDtypeStruct((1, LANE), F32),
                   jax.ShapeDtypeStruct((N_BUCKETS, LANE), F32)],
        scratch_shapes=[pltpu.VMEM((tk, 2 * HEAD), F32), pltpu.VMEM((tk, 2 * HEAD), F32),
                        pltpu.VMEM((tq, 2 * HEAD), F32), pltpu.VMEM((tq, 2 * HEAD), F32),
                        pltpu.VMEM((4, tq, tk), F32)],
        compiler_params=_cp("arbitrary"))(zw, zw, zw, dact, dact, gq, gk, sink, bias, bucket, expand)


FOX_B = 512
FOX_TM = 256


def _tri(n, lower):
    m = np.tril(np.ones((n, n), np.float32)) if lower else np.triu(np.ones((n, n), np.float32))
    return m


def _log_sigmoid(x):
    return jnp.minimum(x, 0.0) - jnp.log1p(jnp.exp(-jnp.abs(x)))


def _fox_prep(zf, gq, gk, bf, name):
    T = zf.shape[0]
    tm = _tile(T, FOX_TM)
    lower = jnp.asarray(_tri(tm, True), _MXU)

    def body(z_ref, gq_ref, gk_ref, bf_ref, l_ref, q_ref, k_ref, v_ref, f_ref, ft_ref, carry):
        @pl.when(pl.program_id(0) == 0)
        def _():
            carry[...] = jnp.zeros_like(carry)

        z = z_ref[...]
        q, _ = _head_rms(z[:, :CH], gq_ref[...], 4)
        k, _ = _head_rms(z[:, CH:2 * CH], gk_ref[...], 4)
        q_ref[...] = q.astype(q_ref.dtype)
        k_ref[...] = k.astype(k_ref.dtype)
        v_ref[...] = z[:, 2 * CH:3 * CH].astype(v_ref.dtype)
        lane = lax.broadcasted_iota(jnp.int32, (1, LANE), 1)
        lf = jnp.where(lane < 4, _log_sigmoid(z[:, 3 * CH:] + bf_ref[...]), 0.0)
        fv = _exact_dot(l_ref[...], lf, _NN, "b") + carry[pl.ds(0, 1), :]
        f_ref[...] = fv
        ft_ref[...] = fv.T
        carry[pl.ds(0, 1), :] = f_ref[pl.ds(tm - 1, 1), :]

    row = pl.BlockSpec((tm, CH), lambda i: (i, 0))
    vec = pl.BlockSpec((1, CH), lambda i: (0, 0))
    qsh = jax.ShapeDtypeStruct((T, CH), _MXU)
    return pl.pallas_call(
        body, name=name, grid=(T // tm,),
        in_specs=[pl.BlockSpec((tm, 3 * CH + LANE), lambda i: (i, 0)), vec, vec,
                  pl.BlockSpec((1, LANE), lambda i: (0, 0)), pl.BlockSpec((tm, tm), lambda i: (0, 0))],
        out_specs=[row, row, row, pl.BlockSpec((tm, LANE), lambda i: (i, 0)), pl.BlockSpec((LANE, tm), lambda i: (0, i))],
        out_shape=[qsh, qsh, qsh, jax.ShapeDtypeStruct((T, LANE), F32), jax.ShapeDtypeStruct((LANE, T), F32)],
        scratch_shapes=[pltpu.VMEM((8, LANE), F32)],
        compiler_params=_cp("arbitrary"))(zf, gq, gk, bf, lower)


def _lane_col(x, h):
    lane = lax.broadcasted_iota(jnp.int32, (1, x.shape[-1]), 1)
    return jnp.sum(jnp.where(lane == h, x, 0.0), axis=-1, keepdims=True)


def _fox_scores(qh, k, fq, ft_ref, h, qi, ki, B):
    s = _dot(qh, k, _NT) * (HEAD ** -0.5)
    s = s + (fq - ft_ref[pl.ds(h, 1), :])
    row = qi * B + lax.broadcasted_iota(jnp.int32, s.shape, 0)
    col = ki * B + lax.broadcasted_iota(jnp.int32, s.shape, 1)
    return jnp.where(col <= row, s, NEG_INF)


def _fox_fwd(q, k, v, f, ft, name):
    T = q.shape[0]
    B = _tile(T, FOX_B)
    n = T // B

    def body(q_ref, k_ref, v_ref, f_ref, ft_ref, o_ref, lse_ref, m_s, l_s, acc):
        qi, ki = pl.program_id(0), pl.program_id(1)

        @pl.when(ki == 0)
        def _():
            m_s[...] = jnp.full_like(m_s, NEG_INF)
            l_s[...] = jnp.zeros_like(l_s)
            acc[...] = jnp.zeros_like(acc)

        @pl.when(ki <= qi)
        def _():
            qv, kv, vv, fv = q_ref[...], k_ref[...], v_ref[...], f_ref[...]
            for h in range(4):
                mk = _lane_mask(CH, h)
                qh = jnp.where(mk, qv, jnp.zeros_like(qv))
                s = _fox_scores(qh, kv, _lane_col(fv, h), ft_ref, h, qi, ki, B)
                m_old = m_s[h]
                m_new = jnp.maximum(m_old, jnp.max(s, axis=-1, keepdims=True))
                alpha = jnp.exp(m_old - m_new)
                p = jnp.exp(s - m_new)
                l_s[h] = alpha * l_s[h] + jnp.sum(p, axis=-1, keepdims=True)
                m_s[h] = m_new
                acc[...] = jnp.where(mk, acc[...] * alpha + _dot(p, vv), acc[...])

        @pl.when(ki == qi)
        def _():
            lane = lax.broadcasted_iota(jnp.int32, (1, LANE), 1)
            out = acc[...]
            lse = jnp.zeros((B, LANE), F32)
            for h in range(4):
                out = jnp.where(_lane_mask(CH, h), out / l_s[h], out)
                lse = jnp.where(lane == h, m_s[h] + jnp.log(l_s[h]), lse)
            o_ref[...] = out
            lse_ref[...] = lse

    qspec = pl.BlockSpec((B, CH), lambda qi, ki: (qi, 0))
    kspec = pl.BlockSpec((B, CH), lambda qi, ki: (jnp.minimum(ki, qi), 0))
    return pl.pallas_call(
        body, name=name, grid=(n, n),
        in_specs=[qspec, kspec, kspec, pl.BlockSpec((B, LANE), lambda qi, ki: (qi, 0)),
                  pl.BlockSpec((8, B), lambda qi, ki: (0, jnp.minimum(ki, qi)))],
        out_specs=[qspec, pl.BlockSpec((B, LANE), lambda qi, ki: (qi, 0))],
        out_shape=[jax.ShapeDtypeStruct((T, CH), F32), jax.ShapeDtypeStruct((T, LANE), F32)],
        scratch_shapes=[pltpu.VMEM((4, B, 1), F32), pltpu.VMEM((4, B, 1), F32), pltpu.VMEM((B, CH), F32)],
        compiler_params=_cp("parallel", "arbitrary"))(q, k, v, f, ft)


def _fox_delta(o, do, name):
    T = o.shape[0]
    tm = _tile(T, ROW_TILE)

    def body(o_ref, d_ref, out_ref):
        prod = o_ref[...] * d_ref[...]
        lane = lax.broadcasted_iota(jnp.int32, (1, LANE), 1)
        out = jnp.zeros((tm, LANE), F32)
        for h in range(4):
            s = jnp.sum(jnp.where(_lane_mask(CH, h), prod, 0.0), axis=-1, keepdims=True)
            out = jnp.where(lane == h, s, out)
        out_ref[...] = out

    row = pl.BlockSpec((tm, CH), lambda i: (i, 0))
    return pl.pallas_call(
        body, name=name, grid=(T // tm,), in_specs=[row, row],
        out_specs=pl.BlockSpec((tm, LANE), lambda i: (i, 0)),
        out_shape=jax.ShapeDtypeStruct((T, LANE), F32), compiler_params=_cp("parallel"))(o, do)


def _fox_bwd_dq(q, k, v, f, ft, lse, delta, do, name):
    T = q.shape[0]
    B = _tile(T, FOX_B)
    n = T // B

    def body(q_ref, k_ref, v_ref, f_ref, ft_ref, lse_ref, dl_ref, do_ref, dq_ref, dfq_ref, dq_s, df_s):
        qi, ki = pl.program_id(0), pl.program_id(1)

        @pl.when(ki == 0)
        def _():
            dq_s[...] = jnp.zeros_like(dq_s)
            df_s[...] = jnp.zeros_like(df_s)

        @pl.when(ki <= qi)
        def _():
            qv, kv, vv, fv = q_ref[...], k_ref[...], v_ref[...], f_ref[...]
            lsev, dlv, dov = lse_ref[...], dl_ref[...], do_ref[...]
            lane = lax.broadcasted_iota(jnp.int32, (1, LANE), 1)
            for h in range(4):
                mk = _lane_mask(CH, h)
                qh = jnp.where(mk, qv, jnp.zeros_like(qv))
                s = _fox_scores(qh, kv, _lane_col(fv, h), ft_ref, h, qi, ki, B)
                p = jnp.exp(s - _lane_col(lsev, h))
                doh = jnp.where(mk, dov, 0.0)
                ds = p * (_dot(doh, vv, _NT) - _lane_col(dlv, h))
                dq_s[...] += jnp.where(mk, _dot(ds * (HEAD ** -0.5), kv), 0.0)
                df_s[...] += jnp.where(lane == h, jnp.sum(ds, axis=-1, keepdims=True), 0.0)

        @pl.when(ki == qi)
        def _():
            dq_ref[...] = dq_s[...]
            dfq_ref[...] = df_s[...]

    qspec = pl.BlockSpec((B, CH), lambda qi, ki: (qi, 0))
    kspec = pl.BlockSpec((B, CH), lambda qi, ki: (jnp.minimum(ki, qi), 0))
    lspec = pl.BlockSpec((B, LANE), lambda qi, ki: (qi, 0))
    return pl.pallas_call(
        body, name=name, grid=(n, n),
        in_specs=[qspec, kspec, kspec, lspec, pl.BlockSpec((8, B), lambda qi, ki: (0, jnp.minimum(ki, qi))),
                  lspec, lspec, qspec],
        out_specs=[qspec, lspec],
        out_shape=[jax.ShapeDtypeStruct((T, CH), F32), jax.ShapeDtypeStruct((T, LANE), F32)],
        scratch_shapes=[pltpu.VMEM((B, CH), F32), pltpu.VMEM((B, LANE), F32)],
        compiler_params=_cp("parallel", "arbitrary"))(q, k, v, f, ft, lse, delta, do)


def _fox_bwd_dkv(q, k, v, f, ft, lse, delta, do, name):
    T = q.shape[0]
    B = _tile(T, FOX_B)
    n = T // B

    def body(q_ref, k_ref, v_ref, f_ref, ft_ref, lse_ref, dl_ref, do_ref, dk_ref, dv_ref, dft_ref, dk_s, dv_s, df_s):
        ki, qi = pl.program_id(0), pl.program_id(1)

        @pl.when(qi == 0)
        def _():
            dk_s[...] = jnp.zeros_like(dk_s)
            dv_s[...] = jnp.zeros_like(dv_s)
            df_s[...] = jnp.zeros_like(df_s)

        @pl.when(qi >= ki)
        def _():
            qv, kv, vv, fv = q_ref[...], k_ref[...], v_ref[...], f_ref[...]
            lsev, dlv, dov = lse_ref[...], dl_ref[...], do_ref[...]
            for h in range(4):
                mk = _lane_mask(CH, h)
                qh = jnp.where(mk, qv, jnp.zeros_like(qv))
                s = _fox_scores(qh, kv, _lane_col(fv, h), ft_ref, h, qi, ki, B)
                p = jnp.exp(s - _lane_col(lsev, h))
                doh = jnp.where(mk, dov, 0.0)
                ds = p * (_dot(doh, vv, _NT) - _lane_col(dlv, h))
                dv_s[...] += _dot(p, doh, _TN)
                dk_s[...] += _dot(ds * (HEAD ** -0.5), qh, _TN)
                df_s[pl.ds(h, 1), :] -= jnp.sum(ds, axis=0, keepdims=True)

        @pl.when(qi == n - 1)
        def _():
            dk_ref[...] = dk_s[...]
            dv_ref[...] = dv_s[...]
            dft_ref[...] = jnp.zeros_like(dft_ref)
            dft_ref[pl.ds(0, 8), :] = df_s[...]

    qspec = pl.BlockSpec((B, CH), lambda ki, qi: (jnp.maximum(qi, ki), 0))
    kspec = pl.BlockSpec((B, CH), lambda ki, qi: (ki, 0))
    lspec = pl.BlockSpec((B, LANE), lambda ki, qi: (jnp.maximum(qi, ki), 0))
    return pl.pallas_call(
        body, name=name, grid=(n, n),
        in_specs=[qspec, kspec, kspec, lspec, pl.BlockSpec((8, B), lambda ki, qi: (0, ki)), lspec, lspec, qspec],
        out_specs=[kspec, kspec, pl.BlockSpec((LANE, B), lambda ki, qi: (0, ki))],
        out_shape=[jax.ShapeDtypeStruct((T, CH), F32), jax.ShapeDtypeStruct((T, CH), F32),
                   jax.ShapeDtypeStruct((LANE, T), F32)],
        scratch_shapes=[pltpu.VMEM((B, CH), F32), pltpu.VMEM((B, CH), F32), pltpu.VMEM((8, B), F32)],
        compiler_params=_cp("parallel", "arbitrary"))(q, k, v, f, ft, lse, delta, do)


def _fox_post(zf, dqn, dkn, dv, dfq, dft, gq, gk, bf, name):
    T = zf.shape[0]
    tm = _tile(T, FOX_TM)
    nt = T // tm
    upper = jnp.asarray(_tri(tm, False), _MXU)

    def body(z_ref, dq_ref, dk_ref, dv_ref, dfq_ref, dft_ref, gq_ref, gk_ref, bf_ref, u_ref, dz_ref, sm_ref, carry, rc_s):
        @pl.when(pl.program_id(0) == 0)
        def _():
            carry[...] = jnp.zeros_like(carry)
            sm_ref[...] = jnp.zeros_like(sm_ref)

        z = z_ref[...]
        q_raw, k_raw = z[:, :CH], z[:, CH:2 * CH]
        _, q_r = _head_rms(q_raw, gq_ref[...], 4)
        _, k_r = _head_rms(k_raw, gk_ref[...], 4)
        dq, dgq = _head_rms_bwd(dq_ref[...], q_raw, q_r, gq_ref[...], 4)
        dk, dgk = _head_rms_bwd(dk_ref[...], k_raw, k_r, gk_ref[...], 4)
        df = dfq_ref[...] + dft_ref[...].T
        rc_s[...] = _exact_dot(u_ref[...], df, _NN, "b") + carry[pl.ds(0, 1), :]
        carry[pl.ds(0, 1), :] = rc_s[pl.ds(0, 1), :]
        lane = lax.broadcasted_iota(jnp.int32, (1, LANE), 1)
        x = z[:, 3 * CH:] + bf_ref[...]
        dff = jnp.where(lane < 4, rc_s[...] * _sigmoid(-x), 0.0)
        dz_ref[:, :CH] = dq.astype(dz_ref.dtype)
        dz_ref[:, CH:2 * CH] = dk.astype(dz_ref.dtype)
        dz_ref[:, 2 * CH:3 * CH] = dv_ref[...].astype(dz_ref.dtype)
        dz_ref[:, 3 * CH:] = dff.astype(dz_ref.dtype)
        sm_ref[pl.ds(0, 1), :] += dgq
        sm_ref[pl.ds(1, 1), :] += dgk
        sm_ref[pl.ds(2, 1), :LANE] += jnp.sum(dff, axis=0, keepdims=True)

    rev = lambda i: (nt - 1 - i, 0)
    row = pl.BlockSpec((tm, CH), rev)
    lrow = pl.BlockSpec((tm, LANE), rev)
    vec = pl.BlockSpec((1, CH), lambda i: (0, 0))
    return pl.pallas_call(
        body, name=name, grid=(nt,),
        in_specs=[pl.BlockSpec((tm, 3 * CH + LANE), rev), row, row, row, lrow,
                  pl.BlockSpec((LANE, tm), lambda i: (0, nt - 1 - i)), vec, vec,
                  pl.BlockSpec((1, LANE), lambda i: (0, 0)), pl.BlockSpec((tm, tm), lambda i: (0, 0))],
        out_specs=[pl.BlockSpec((tm, 3 * CH + LANE), rev), pl.BlockSpec((8, CH), lambda i: (0, 0))],
        out_shape=[jax.ShapeDtypeStruct((T, 3 * CH + LANE), _MXU), jax.ShapeDtypeStruct((8, CH), F32)],
        scratch_shapes=[pltpu.VMEM((8, LANE), F32), pltpu.VMEM((tm, LANE), F32)],
        compiler_params=_cp("arbitrary"))(zf, dqn, dkn, dv, dfq, dft, gq, gk, bf, upper)


AUG_F, AUG_ONE, AUG_LSE = HEAD, HEAD + 3, HEAD + 6


def _pieces(x):
    hi = x.astype(_MXU).astype(F32)
    r1 = x - hi
    mid = r1.astype(_MXU).astype(F32)
    lo = (r1 - mid).astype(_MXU).astype(F32)
    return hi, mid, lo


def _put_pieces(base, first_lane, x, sign):
    lane = lax.broadcasted_iota(jnp.int32, (1, LANE), 1)
    for j, piece in enumerate(_pieces(x)):
        base = jnp.where(lane == first_lane + j, sign * piece, base)
    return base


def _head_select_matrix():
    p = np.zeros((4, 4 * HEAD, LANE), np.float32)
    for h in range(4):
        for d in range(HEAD):
            p[h, h * HEAD + d, d] = 1.0
    return p


def _tri_steps(n, by_key):
    if by_key:
        pairs = [(q, k) for k in range(n) for q in range(k, n)]
    else:
        pairs = [(q, k) for q in range(n) for k in range(q + 1)]
    return (jnp.asarray([p[0] for p in pairs], jnp.int32), jnp.asarray([p[1] for p in pairs], jnp.int32))


def _fox2_prep(zf, gq, gk, bf, sel, name):
    T = zf.shape[0]
    tm = _tile(T, FOX_TM)
    lower = jnp.asarray(_tri(tm, True), _MXU)

    def body(z_ref, gq_ref, gk_ref, bf_ref, l_ref, p_ref, qa_ref, ka_ref, va_ref, carry, f_s):
        @pl.when(pl.program_id(0) == 0)
        def _():
            carry[...] = jnp.zeros_like(carry)

        z = z_ref[...]
        q, _ = _head_rms(z[:, :CH], gq_ref[...], 4)
        k, _ = _head_rms(z[:, CH:2 * CH], gk_ref[...], 4)
        q = (q * (HEAD ** -0.5)).astype(_MXU)
        k = k.astype(_MXU)
        v = z[:, 2 * CH:3 * CH].astype(_MXU)
        lane = lax.broadcasted_iota(jnp.int32, (1, LANE), 1)
        lf = jnp.where(lane < 4, _log_sigmoid(z[:, 3 * CH:] + bf_ref[...]), 0.0)
        f_s[...] = _exact_dot(l_ref[...], lf, _NN, "b") + carry[pl.ds(0, 1), :]
        carry[pl.ds(0, 1), :] = f_s[pl.ds(tm - 1, 1), :]
        fv = f_s[...]
        q_ones = (lane >= AUG_ONE) & (lane < AUG_ONE + 3)
        k_ones = ((lane >= AUG_F) & (lane < AUG_F + 3)) | ((lane >= AUG_LSE) & (lane < AUG_LSE + 3))
        v_ones = (lane >= AUG_F) & (lane < AUG_F + 3)
        for h in range(4):
            fh = _lane_col(fv, h)
            qa = jnp.where(q_ones, 1.0, _dot(q, p_ref[h]))
            qa_ref[h] = _put_pieces(qa, AUG_F, fh, 1.0).astype(qa_ref.dtype)
            ka = jnp.where(k_ones, 1.0, _dot(k, p_ref[h]))
            ka_ref[h] = _put_pieces(ka, AUG_ONE, fh, -1.0).astype(ka_ref.dtype)
            va_ref[h] = jnp.where(v_ones, 1.0, _dot(v, p_ref[h])).astype(va_ref.dtype)

    vec = pl.BlockSpec((1, CH), lambda i: (0, 0))
    hspec = pl.BlockSpec((4, tm, LANE), lambda i: (0, i, 0))
    hsh = jax.ShapeDtypeStruct((4, T, LANE), _MXU)
    return pl.pallas_call(
        body, name=name, grid=(T // tm,),
        in_specs=[pl.BlockSpec((tm, 3 * CH + LANE), lambda i: (i, 0)), vec, vec,
                  pl.BlockSpec((1, LANE), lambda i: (0, 0)), pl.BlockSpec((tm, tm), lambda i: (0, 0)),
                  pl.BlockSpec(sel.shape, lambda i: (0, 0, 0))],
        out_specs=[hspec, hspec, hspec], out_shape=[hsh, hsh, hsh],
        scratch_shapes=[pltpu.VMEM((8, LANE), F32), pltpu.VMEM((tm, LANE), F32)],
        compiler_params=_cp("arbitrary"))(zf, gq, gk, bf, lower, sel)


def _causal(s, transposed):
    row = lax.broadcasted_iota(jnp.int32, s.shape, 0)
    col = lax.broadcasted_iota(jnp.int32, s.shape, 1)
    return jnp.where((row <= col) if transposed else (col <= row), s, NEG_INF)


def _mxu_dot(a, b, dims):
    return lax.dot_general(a, b, dims, preferred_element_type=F32)


def _fox2_fwd(qa, ka, va, sel, name):
    T = qa.shape[1]
    B = _tile(T, FOX_B)
    n = T // B
    qt, kt = _tri_steps(n, False)

    def body(qt_ref, kt_ref, qa_ref, ka_ref, va_ref, p_ref, o_ref, qb_ref, m_s, acc):
        step = pl.program_id(0)
        qi, ki = qt_ref[step], kt_ref[step]

        @pl.when(ki == 0)
        def _():
            m_s[...] = jnp.full_like(m_s, NEG_INF)
            acc[...] = jnp.zeros_like(acc)

        def update(diag):
            for h in range(4):
                s = _mxu_dot(qa_ref[h], ka_ref[h], _NT)
                if diag:
                    s = _causal(s, False)
                m_old = m_s[h]
                m_new = jnp.maximum(m_old, jnp.max(s, axis=-1, keepdims=True))
                p = jnp.exp(s - m_new)
                acc[h] = acc[h] * jnp.exp(m_old - m_new) + _dot(p, va_ref[h])
                m_s[h] = m_new

        @pl.when(ki < qi)
        def _():
            update(False)

        @pl.when(ki == qi)
        def _():
            update(True)
            out = jnp.zeros((B, CH), F32)
            for h in range(4):
                a = acc[h]
                l = _lane_col(a, AUG_F)
                out = out + _exact_dot(a / l, p_ref[h], _NT, "a")
                lse = m_s[h] + jnp.log(l)
                qb_ref[h] = _put_pieces(qa_ref[h].astype(F32), AUG_LSE, lse, -1.0).astype(qb_ref.dtype)
            o_ref[...] = out

    qspec = pl.BlockSpec((4, B, LANE), lambda s, qt, kt: (0, qt[s], 0))
    kspec = pl.BlockSpec((4, B, LANE), lambda s, qt, kt: (0, kt[s], 0))
    grid_spec = pltpu.PrefetchScalarGridSpec(
        num_scalar_prefetch=2, grid=(qt.shape[0],),
        in_specs=[qspec, kspec, kspec, pl.BlockSpec(sel.shape, lambda s, qt, kt: (0, 0, 0))],
        out_specs=[pl.BlockSpec((B, CH), lambda s, qt, kt: (qt[s], 0)), qspec],
        scratch_shapes=[pltpu.VMEM((4, B, 1), F32), pltpu.VMEM((4, B, LANE), F32)])
    return pl.pallas_call(
        body, name=name, grid_spec=grid_spec,
        out_shape=[jax.ShapeDtypeStruct((T, CH), F32), jax.ShapeDtypeStruct((4, T, LANE), _MXU)],
        compiler_params=_cp("arbitrary"))(qt, kt, qa, ka, va, sel)


def _fox2_bwd_prep(o, do, sel, name):
    T = o.shape[0]
    tm = _tile(T, ROW_TILE)

    def body(o_ref, d_ref, p_ref, out_ref):
        dov = d_ref[...]
        prod = o_ref[...] * dov
        dob = dov.astype(_MXU)
        for h in range(4):
            delta = jnp.sum(jnp.where(_lane_mask(CH, h), prod, 0.0), axis=-1, keepdims=True)
            out_ref[h] = _put_pieces(_dot(dob, p_ref[h]), AUG_F, delta, -1.0).astype(out_ref.dtype)

    row = pl.BlockSpec((tm, CH), lambda i: (i, 0))
    return pl.pallas_call(
        body, name=name, grid=(T // tm,),
        in_specs=[row, row, pl.BlockSpec(sel.shape, lambda i: (0, 0, 0))],
        out_specs=pl.BlockSpec((4, tm, LANE), lambda i: (0, i, 0)),
        out_shape=jax.ShapeDtypeStruct((4, T, LANE), _MXU), compiler_params=_cp("parallel"))(o, do, sel)


def _fox2_bwd_dq(qb, ka, va, doa, sel, name):
    T = qb.shape[1]
    B = _tile(T, FOX_B)
    n = T // B
    qt, kt = _tri_steps(n, False)

    def body(qt_ref, kt_ref, qb_ref, ka_ref, va_ref, do_ref, p_ref, dq_ref, dfq_ref, dq_s):
        step = pl.program_id(0)
        qi, ki = qt_ref[step], kt_ref[step]

        @pl.when(ki == 0)
        def _():
            dq_s[...] = jnp.zeros_like(dq_s)

        def update(diag):
            for h in range(4):
                s = _mxu_dot(qb_ref[h], ka_ref[h], _NT)
                if diag:
                    s = _causal(s, False)
                ds = jnp.exp(s) * _mxu_dot(do_ref[h], va_ref[h], _NT)
                dq_s[h] += _dot(ds, ka_ref[h])

        @pl.when(ki < qi)
        def _():
            update(False)

        @pl.when(ki == qi)
        def _():
            update(True)
            lane = lax.broadcasted_iota(jnp.int32, (1, LANE), 1)
            out = jnp.zeros((B, CH), F32)
            dfq = jnp.zeros((B, LANE), F32)
            for h in range(4):
                out = out + _exact_dot(dq_s[h] * (HEAD ** -0.5), p_ref[h], _NT, "a")
                dfq = jnp.where(lane == h, _lane_col(dq_s[h], AUG_F), dfq)
            dq_ref[...] = out
            dfq_ref[...] = dfq

    qspec = pl.BlockSpec((4, B, LANE), lambda s, qt, kt: (0, qt[s], 0))
    kspec = pl.BlockSpec((4, B, LANE), lambda s, qt, kt: (0, kt[s], 0))
    grid_spec = pltpu.PrefetchScalarGridSpec(
        num_scalar_prefetch=2, grid=(qt.shape[0],),
        in_specs=[qspec, kspec, kspec, qspec, pl.BlockSpec(sel.shape, lambda s, qt, kt: (0, 0, 0))],
        out_specs=[pl.BlockSpec((B, CH), lambda s, qt, kt: (qt[s], 0)),
                   pl.BlockSpec((B, LANE), lambda s, qt, kt: (qt[s], 0))],
        scratch_shapes=[pltpu.VMEM((4, B, LANE), F32)])
    return pl.pallas_call(
        body, name=name, grid_spec=grid_spec,
        out_shape=[jax.ShapeDtypeStruct((T, CH), F32), jax.ShapeDtypeStruct((T, LANE), F32)],
        compiler_params=_cp("arbitrary"))(qt, kt, qb, ka, va, doa, sel)


def _fox2_bwd_dkv(qb, ka, va, doa, sel, name):
    T = qb.shape[1]
    B = _tile(T, FOX_B)
    n = T // B
    qt, kt = _tri_steps(n, True)

    def body(qt_ref, kt_ref, qb_ref, ka_ref, va_ref, do_ref, p_ref, dk_ref, dv_ref, df_ref, dk_s, dv_s):
        step = pl.program_id(0)
        qi, ki = qt_ref[step], kt_ref[step]

        @pl.when(qi == ki)
        def _():
            dk_s[...] = jnp.zeros_like(dk_s)
            dv_s[...] = jnp.zeros_like(dv_s)

        def update(diag):
            for h in range(4):
                st = _mxu_dot(ka_ref[h], qb_ref[h], _NT)
                if diag:
                    st = _causal(st, True)
                pt = jnp.exp(st)
                dst = pt * _mxu_dot(va_ref[h], do_ref[h], _NT)
                dv_s[h] += _dot(pt, do_ref[h])
                dk_s[h] += _dot(dst, qb_ref[h])

        @pl.when(qi == ki)
        def _():
            update(True)

        @pl.when(qi > ki)
        def _():
            update(False)

        @pl.when(qi == n - 1)
        def _():
            lane = lax.broadcasted_iota(jnp.int32, (1, LANE), 1)
            dk = jnp.zeros((B, CH), F32)
            dv = jnp.zeros((B, CH), F32)
            dfk = jnp.zeros((B, LANE), F32)
            for h in range(4):
                dk = dk + _exact_dot(dk_s[h], p_ref[h], _NT, "a")
                dv = dv + _exact_dot(dv_s[h], p_ref[h], _NT, "a")
                dfk = jnp.where(lane == h, -_lane_col(dk_s[h], AUG_ONE), dfk)
            dk_ref[...] = dk
            dv_ref[...] = dv
            df_ref[...] = dfk

    qspec = pl.BlockSpec((4, B, LANE), lambda s, qt, kt: (0, qt[s], 0))
    kspec = pl.BlockSpec((4, B, LANE), lambda s, qt, kt: (0, kt[s], 0))
    ospec = pl.BlockSpec((B, CH), lambda s, qt, kt: (kt[s], 0))
    grid_spec = pltpu.PrefetchScalarGridSpec(
        num_scalar_prefetch=2, grid=(qt.shape[0],),
        in_specs=[qspec, kspec, kspec, qspec, pl.BlockSpec(sel.shape, lambda s, qt, kt: (0, 0, 0))],
        out_specs=[ospec, ospec, pl.BlockSpec((B, LANE), lambda s, qt, kt: (kt[s], 0))],
        scratch_shapes=[pltpu.VMEM((4, B, LANE), F32), pltpu.VMEM((4, B, LANE), F32)])
    return pl.pallas_call(
        body, name=name, grid_spec=grid_spec,
        out_shape=[jax.ShapeDtypeStruct((T, CH), F32), jax.ShapeDtypeStruct((T, CH), F32),
                   jax.ShapeDtypeStruct((T, LANE), F32)],
        compiler_params=_cp("arbitrary"))(qt, kt, qb, ka, va, doa, sel)


def _fox2_post(zf, dqn, dkn, dv, dfq, dfk, gq, gk, bf, name):
    T = zf.shape[0]
    tm = _tile(T, FOX_TM)
    nt = T // tm
    upper = jnp.asarray(_tri(tm, False), _MXU)

    def body(z_ref, dq_ref, dk_ref, dv_ref, dfq_ref, df_ref, gq_ref, gk_ref, bf_ref, u_ref, dz_ref, sm_ref, carry, rc_s):
        @pl.when(pl.program_id(0) == 0)
        def _():
            carry[...] = jnp.zeros_like(carry)
            sm_ref[...] = jnp.zeros_like(sm_ref)

        z = z_ref[...]
        q_raw, k_raw = z[:, :CH], z[:, CH:2 * CH]
        _, q_r = _head_rms(q_raw, gq_ref[...], 4)
        _, k_r = _head_rms(k_raw, gk_ref[...], 4)
        dq, dgq = _head_rms_bwd(dq_ref[...], q_raw, q_r, gq_ref[...], 4)
        dk, dgk = _head_rms_bwd(dk_ref[...], k_raw, k_r, gk_ref[...], 4)
        rc_s[...] = _exact_dot(u_ref[...], dfq_ref[...] + df_ref[...], _NN, "b") + carry[pl.ds(0, 1), :]
        carry[pl.ds(0, 1), :] = rc_s[pl.ds(0, 1), :]
        lane = lax.broadcasted_iota(jnp.int32, (1, LANE), 1)
        x = z[:, 3 * CH:] + bf_ref[...]
        dff = jnp.where(lane < 4, rc_s[...] * _sigmoid(-x), 0.0)
        dz_ref[:, :CH] = dq.astype(dz_ref.dtype)
        dz_ref[:, CH:2 * CH] = dk.astype(dz_ref.dtype)
        dz_ref[:, 2 * CH:3 * CH] = dv_ref[...].astype(dz_ref.dtype)
        dz_ref[:, 3 * CH:] = dff.astype(dz_ref.dtype)
        sm_ref[pl.ds(0, 1), :] += dgq
        sm_ref[pl.ds(1, 1), :] += dgk
        sm_ref[pl.ds(2, 1), :LANE] += jnp.sum(dff, axis=0, keepdims=True)

    rev = lambda i: (nt - 1 - i, 0)
    row = pl.BlockSpec((tm, CH), rev)
    lrow = pl.BlockSpec((tm, LANE), rev)
    vec = pl.BlockSpec((1, CH), lambda i: (0, 0))
    return pl.pallas_call(
        body, name=name, grid=(nt,),
        in_specs=[pl.BlockSpec((tm, 3 * CH + LANE), rev), row, row, row, lrow, lrow, vec, vec,
                  pl.BlockSpec((1, LANE), lambda i: (0, 0)), pl.BlockSpec((tm, tm), lambda i: (0, 0))],
        out_specs=[pl.BlockSpec((tm, 3 * CH + LANE), rev), pl.BlockSpec((8, CH), lambda i: (0, 0))],
        out_shape=[jax.ShapeDtypeStruct((T, 3 * CH + LANE), _MXU), jax.ShapeDtypeStruct((8, CH), F32)],
        scratch_shapes=[pltpu.VMEM((8, LANE), F32), pltpu.VMEM((tm, LANE), F32)],
        compiler_params=_cp("arbitrary"))(zf, dqn, dkn, dv, dfq, dfk, gq, gk, bf, upper)


def _merge_fwd(acts, zg, wbr, wout, x1, name):
    T, D = x1.shape
    tm = _tile(T, 256)

    def body(a0, a1, a2, a3, zg_ref, wbr_ref, wout_ref, x_ref, o_ref, mg_ref):
        merged = None
        for i, a_ref in enumerate((a0, a1, a2, a3)):
            term = _sigmoid(zg_ref[:, i * D:(i + 1) * D]) * _dot(a_ref[...], wbr_ref[i])
            merged = term if merged is None else merged + term
        mg_ref[...] = merged.astype(mg_ref.dtype)
        o_ref[...] = x_ref[...] + _dot(merged, wout_ref[...])

    arow = pl.BlockSpec((tm, CH), lambda i: (i, 0))
    xrow = pl.BlockSpec((tm, D), lambda i: (i, 0))
    return pl.pallas_call(
        body, name=name, grid=(T // tm,),
        in_specs=[arow, arow, arow, arow, pl.BlockSpec((tm, 4 * D), lambda i: (i, 0)),
                  pl.BlockSpec((4, CH, D), lambda i: (0, 0, 0)), pl.BlockSpec((D, D), lambda i: (0, 0)), xrow],
        out_specs=[xrow, xrow],
        out_shape=[jax.ShapeDtypeStruct((T, D), F32), jax.ShapeDtypeStruct((T, D), _MXU)],
        compiler_params=_cp("parallel"))(*acts, zg, wbr, wout, x1)


def _merge_bwd(dx2, acts, zg, wbr, wout, name):
    T, D = dx2.shape
    tm = _tile(T, 256)
    nt = T // tm

    def body(dx_ref, a0, a1, a2, a3, zg_ref, wbr_ref, wout_ref, d0, d1, d2, d3, dzg_ref, dw_ref, dw_s):
        i = pl.program_id(0)

        @pl.when(i == 0)
        def _():
            dw_s[...] = jnp.zeros_like(dw_s)

        dm = _dot(dx_ref[...], wout_ref[...], _NT)
        for b, (a_ref, d_ref) in enumerate(((a0, d0), (a1, d1), (a2, d2), (a3, d3))):
            av = a_ref[...].astype(_MXU)
            g = _sigmoid(zg_ref[:, b * D:(b + 1) * D])
            p = _dot(av, wbr_ref[b])
            dzg_ref[:, b * D:(b + 1) * D] = (dm * p * (g * (1.0 - g))).astype(dzg_ref.dtype)
            dp = (dm * g).astype(_MXU)
            d_ref[...] = _dot(dp, wbr_ref[b], _NT)
            dw_s[b] += _dot(av, dp, _TN)

        @pl.when(i == nt - 1)
        def _():
            dw_ref[...] = dw_s[...].astype(dw_ref.dtype)

    arow = pl.BlockSpec((tm, CH), lambda i: (i, 0))
    xrow = pl.BlockSpec((tm, D), lambda i: (i, 0))
    grow = pl.BlockSpec((tm, 4 * D), lambda i: (i, 0))
    wspec = pl.BlockSpec((4, CH, D), lambda i: (0, 0, 0))
    ash = jax.ShapeDtypeStruct((T, CH), F32)
    return pl.pallas_call(
        body, name=name, grid=(nt,),
        in_specs=[xrow, arow, arow, arow, arow, grow, wspec, pl.BlockSpec((D, D), lambda i: (0, 0))],
        out_specs=[arow, arow, arow, arow, grow, wspec],
        out_shape=[ash, ash, ash, ash, jax.ShapeDtypeStruct((T, 4 * D), _MXU), jax.ShapeDtypeStruct((4, CH, D), _MXU)],
        scratch_shapes=[pltpu.VMEM((4, CH, D), F32)],
        compiler_params=_cp("arbitrary"))(dx2, *acts, zg, wbr, wout)


def _rows_2d(a):
    return a.reshape((-1, a.shape[-1])) if a.ndim > 1 else a.reshape((1, -1))


def _row_tile(rows, cols, n_bufs):
    padded = -(-cols // LANE) * LANE
    cap = max(8, (VMEM_LIMIT // 3) // (2 * n_bufs * 4 * padded))
    return _tile(rows, cap, 8)


def _sum8(recv, name):
    shape = recv.shape[1:]
    r2 = recv.reshape((N_DEV, -1, shape[-1]))
    rows, cols = r2.shape[1:]
    tr = _row_tile(rows, cols, N_DEV // 2 + 1)

    def body(r_ref, o_ref):
        acc = r_ref[0].astype(F32)
        for d in range(1, N_DEV):
            acc = acc + r_ref[d].astype(F32)
        o_ref[...] = acc

    out = pl.pallas_call(
        body, name=name, grid=(rows // tr,),
        in_specs=[pl.BlockSpec((N_DEV, tr, cols), lambda i: (0, i, 0))],
        out_specs=pl.BlockSpec((tr, cols), lambda i: (i, 0)),
        out_shape=jax.ShapeDtypeStruct((rows, cols), F32), compiler_params=_cp("parallel"))(r2)
    return out.reshape(shape)


def _adamw(w, g, m, v, name):
    shape = w.shape
    w2, g2, m2, v2 = (_rows_2d(a) for a in (w, g, m, v))
    rows, cols = w2.shape
    tr = _row_tile(rows, cols, 7)

    def body(w_ref, g_ref, m_ref, v_ref, d_ref, nm_ref, nv_ref):
        gv = g_ref[...]
        nm = ADAM_B1 * m_ref[...] + (1.0 - ADAM_B1) * gv
        nv = ADAM_B2 * v_ref[...] + (1.0 - ADAM_B2) * jnp.square(gv)
        m_hat = nm / (1.0 - ADAM_B1 ** ADAM_STEP)
        v_hat = nv / (1.0 - ADAM_B2 ** ADAM_STEP)
        d_ref[...] = -ADAM_LR * (m_hat / (jnp.sqrt(v_hat) + ADAM_EPS) + ADAM_WD * w_ref[...])
        nm_ref[...] = nm
        nv_ref[...] = nv

    spec = pl.BlockSpec((tr, cols), lambda i: (i, 0))
    osh = jax.ShapeDtypeStruct((rows, cols), F32)
    outs = pl.pallas_call(
        body, name=name, grid=(rows // tr,), in_specs=[spec] * 4, out_specs=[spec] * 3,
        out_shape=[osh] * 3, compiler_params=_cp("parallel"))(w2, g2, m2, v2)
    return tuple(o.reshape(shape) for o in outs)


def _exchange(items, name):
    n = len(items)
    widths, out_shapes = [], []
    for src, kind, ax in items:
        if kind == "gather":
            w = src.shape[ax]
            shp = list(src.shape)
            shp[ax] = N_DEV * w
        else:
            w = src.shape[ax] // N_DEV
            shp = list(src.shape)
            shp[ax] = w
            shp = [N_DEV] + shp
        widths.append(w)
        out_shapes.append(jax.ShapeDtypeStruct(tuple(shp), src.dtype))

    def body(*refs):
        srcs, outs = refs[:n], refs[n:2 * n]
        send, recv, lsem = refs[2 * n:]
        x, y, c = lax.axis_index("x"), lax.axis_index("y"), lax.axis_index("c")
        me = 4 * x + 2 * y + c

        def peer(k):
            b = k + 1
            px = 1 - x if b & 4 else x
            py = 1 - y if b & 2 else y
            pc = 1 - c if b & 1 else c
            return (px, py, pc), 4 * px + 2 * py + pc

        def win(ref, ax, idx, w):
            return ref.at[tuple([slice(None)] * ax + [pl.ds(idx * w, w)])]

        def ends(j, mine, theirs):
            _, kind, ax = items[j]
            if kind == "gather":
                return srcs[j], win(outs[j], ax, mine, widths[j])
            return win(srcs[j], ax, theirs, widths[j]), outs[j].at[mine]

        local, sent = [], []
        for j in range(n):
            s, d = ends(j, me, me)
            cp = pltpu.make_async_copy(s, d, lsem.at[j])
            cp.start()
            local.append(cp)
            for k in range(N_DEV - 1):
                dev, pid = peer(k)
                s, d = ends(j, me, pid)
                cp = pltpu.make_async_remote_copy(s, d, send.at[j, k], recv.at[j, k], device_id=dev,
                                                  device_id_type=pl.DeviceIdType.MESH)
                cp.start()
                sent.append(cp)
        for j in range(n):
            for k in range(N_DEV - 1):
                dev, pid = peer(k)
                s, d = ends(j, pid, me)
                pltpu.make_async_remote_copy(s, d, send.at[j, k], recv.at[j, k], device_id=dev,
                                             device_id_type=pl.DeviceIdType.MESH).wait_recv()
        for cp in sent:
            cp.wait_send()
        for cp in local:
            cp.wait()

    hbm = pl.BlockSpec(memory_space=pl.ANY)
    return pl.pallas_call(
        body, name=name, in_specs=[hbm] * n, out_specs=[hbm] * n, out_shape=out_shapes,
        scratch_shapes=[pltpu.SemaphoreType.DMA((n, N_DEV - 1)), pltpu.SemaphoreType.DMA((n, N_DEV - 1)),
                        pltpu.SemaphoreType.DMA((n,))],
        compiler_params=pltpu.CompilerParams(has_side_effects=True))(*[it[0] for it in items])


def _pack(arrs):
    flat = jnp.concatenate([a.reshape(-1).astype(F32) for a in arrs])
    n = flat.shape[0]
    rows = -(-n // (8 * LANE)) * 8
    return jnp.pad(flat, (0, rows * LANE - n)).reshape(rows, LANE)


def _unpack(buf, shapes):
    flat = buf.reshape(-1)
    out, off = [], 0
    for s in shapes:
        sz = int(np.prod(s))
        out.append(flat[off:off + sz].reshape(s))
        off += sz
    return out


def _pad_axis(a, axis, size):
    pad = [(0, 0)] * a.ndim
    pad[axis] = (0, size - a.shape[axis])
    return jnp.pad(a, pad)


def _ffn_forward(x, g, wg, wu, wd, tag):
    a = _rms_fwd(x, g, f"{tag}_rms")
    gate, up, hid = _ffn_up(a, wg, wu, f"{tag}_up")
    out = _mm([(hid, wd)], "nn", F32, f"{tag}_down", scale=0.5, res=x)
    return out, (x, a, gate, up, hid)


def _ffn_backward(dxp, saved, g, wg, wu, wd, tag):
    x, a, gate, up, hid = saved
    d_gate, d_up = _ffn_bwd_hid(dxp, wd, gate, up, f"{tag}_bwd_hid")
    d_wd = _mm([(hid, dxp)], "tn", _MXU, f"{tag}_dwd", scale=0.5, tk=2048)
    d_wg = _mm([(a, d_gate)], "tn", _MXU, f"{tag}_dwg", tk=2048)
    d_wu = _mm([(a, d_up)], "tn", _MXU, f"{tag}_dwu", tk=2048)
    d_a = _mm([(d_gate, wg), (d_up, wu)], "nt", F32, f"{tag}_da")
    dx, dg = _rms_bwd(d_a, x, g, dxp, f"{tag}_rms_bwd")
    return dx, dg, d_wg, d_wu, d_wd


def _tile_vec(v, reps):
    return jnp.tile(v.reshape(1, -1), (1, reps))


def _mixer_forward(x1, p, consts, tag):
    h = _rms_fwd(x1, p["mix_norm"], f"{tag}_rms")
    zg = _mm([(h, p["w_zg"])], "nn", F32, f"{tag}_in_g")
    zc = _mm([(h, p["w_conf"])], "nn", F32, f"{tag}_in_c")
    zs = _mm([(h, p["w_sc"])], "nn", F32, f"{tag}_in_s")
    zw = _mm([(h, p["w_swa"])], "nn", F32, f"{tag}_in_w")
    zf = _mm([(h, p["w_fox"])], "nn", F32, f"{tag}_in_f")
    u1, act_c = _conf_fwd(zc, p["conf_dw"], p["conf_dw_b"], p["conf_ln_g"], p["conf_ln_b"], f"{tag}_conf")
    act_s = _sc_fwd(zs, p["sc_conv"], f"{tag}_sc")
    act_w = _swa_fwd(zw, p["swa_q_norm"], p["swa_k_norm"], p["swa_sink"], consts["bias"], consts["expand"], f"{tag}_swa")
    qa, ka, va = _fox2_prep(zf, p["fox_q_norm"], p["fox_k_norm"], p["b_forget"], consts["sel"], f"{tag}_fox_prep")
    act_f, qb = _fox2_fwd(qa, ka, va, consts["sel"], f"{tag}_fox")
    acts = (act_c, act_s, act_w, act_f)
    x2, merged = _merge_fwd(acts, zg, p["w_br"], p["w_out"], x1, f"{tag}_merge")
    saved = (x1, h, zg, zc, zs, zw, zf, u1, acts, qb, ka, va, merged)
    return x2, saved


def _mixer_backward(dx2, saved, p, consts, tag):
    x1, h, zg, zc, zs, zw, zf, u1, acts, qb, ka, va, merged = saved
    g = {}
    g["w_out"] = _mm([(merged, dx2)], "tn", _MXU, f"{tag}_dwout", tk=2048)
    d_c, d_s, d_w, d_f, dzg, g["w_br"] = _merge_bwd(dx2, acts, zg, p["w_br"], p["w_out"], f"{tag}_merge_bwd")
    du1, sm_c = _conf_bwd_ln(d_c, u1, p["conf_ln_g"], p["conf_ln_b"], f"{tag}_conf_bwd_ln")
    dzc, g["conf_dw"] = _conf_bwd_conv(zc, du1, p["conf_dw"], f"{tag}_conf_bwd_conv")
    g["conf_ln_g"], g["conf_ln_b"], g["conf_dw_b"] = sm_c[0], sm_c[1], sm_c[2]
    dzs, g["sc_conv"] = _sc_bwd(zs, d_s, p["sc_conv"], f"{tag}_sc_bwd")
    dzw, dgq, dgk, g["swa_sink"], g["rel_bias"] = _swa_bwd(
        zw, d_w, p["swa_q_norm"], p["swa_k_norm"], p["swa_sink"], consts["bias"], consts["bucket"], consts["expand"],
        f"{tag}_swa_bwd")
    g["swa_q_norm"], g["swa_k_norm"] = dgq, dgk
    doa = _fox2_bwd_prep(acts[3], d_f, consts["sel"], f"{tag}_fox_bwd_prep")
    dqn, dfq = _fox2_bwd_dq(qb, ka, va, doa, consts["sel"], f"{tag}_fox_bwd_dq")
    dkn, dv, dfk = _fox2_bwd_dkv(qb, ka, va, doa, consts["sel"], f"{tag}_fox_bwd_dkv")
    dzf, sm_f = _fox2_post(zf, dqn, dkn, dv, dfq, dfk, p["fox_q_norm"], p["fox_k_norm"], p["b_forget"], f"{tag}_fox_post")
    g["fox_q_norm"], g["fox_k_norm"], g["b_forget"] = sm_f[0], sm_f[1], sm_f[2]
    parts = ((dzg, "w_zg"), (dzc, "w_conf"), (dzs, "w_sc"), (dzw, "w_swa"), (dzf, "w_fox"))
    for dz, wname in parts:
        g[wname] = _mm([(h, dz)], "tn", _MXU, f"{tag}_d{wname}", tk=2048)
    dh = _mm([(dz, p[wname]) for dz, wname in parts], "nt", F32, f"{tag}_dh", tm=512)
    dx1, g["mix_norm"] = _rms_bwd(dh, x1, p["mix_norm"], dx2, f"{tag}_rms_bwd")
    return dx1, g


W_NAMES = ['rel_bias', 'ffn1_norm', 'ffn1_w_gate', 'ffn1_w_up', 'ffn1_w_down', 'mix_norm', 'w_in', 'b_forget', 'conf_dw',
           'conf_dw_b', 'conf_ln_g', 'conf_ln_b', 'conf_w_out', 'sc_conv', 'sc_w_out', 'swa_q_norm', 'swa_k_norm',
           'swa_sink', 'swa_w_o', 'fox_q_norm', 'fox_k_norm', 'fox_w_o', 'w_out', 'ffn2_norm', 'ffn2_w_gate',
           'ffn2_w_up', 'ffn2_w_down']
SMALL = ['rel_bias', 'ffn1_norm', 'mix_norm', 'b_forget', 'conf_dw', 'conf_dw_b', 'conf_ln_g', 'conf_ln_b', 'sc_conv',
         'swa_q_norm', 'swa_k_norm', 'swa_sink', 'fox_q_norm', 'fox_k_norm', 'ffn2_norm']
BRANCH_W = ['conf_w_out', 'sc_w_out', 'swa_w_o', 'fox_w_o']
IN_CONF, IN_SC, IN_SWA, IN_FOX, IN_FF = (0, 512), (512, 1280), (1280, 1792), (1792, 2560), (2560, 2564)


def _step(w, m, v, x, loss_target):
    T, D = x.shape
    L = w["w_out"].shape[0]
    fs = w["ffn1_w_gate"].shape[2]
    fsp = -(-fs // LANE) * LANE
    dev = 4 * lax.axis_index("x") + 2 * lax.axis_index("y") + lax.axis_index("c")

    def cast(a):
        return a.astype(_MXU)

    win = w["w_in"]
    fox_cols = jnp.concatenate([win[..., IN_FOX[0]:IN_FF[1]],
                                jnp.zeros(win.shape[:2] + (LANE - (IN_FF[1] - IN_FF[0]),), win.dtype)], axis=-1)
    shards = {
        "ffn1_w_gate": (cast(_pad_axis(w["ffn1_w_gate"], 2, fsp)), 2),
        "ffn1_w_up": (cast(_pad_axis(w["ffn1_w_up"], 2, fsp)), 2),
        "ffn1_w_down": (cast(_pad_axis(w["ffn1_w_down"], 1, fsp)), 1),
        "ffn2_w_gate": (cast(_pad_axis(w["ffn2_w_gate"], 2, fsp)), 2),
        "ffn2_w_up": (cast(_pad_axis(w["ffn2_w_up"], 2, fsp)), 2),
        "ffn2_w_down": (cast(_pad_axis(w["ffn2_w_down"], 1, fsp)), 1),
        "w_zg": (cast(win[..., IN_FF[1]:]), 1),
        "w_conf": (cast(win[..., IN_CONF[0]:IN_CONF[1]]), 1),
        "w_sc": (cast(win[..., IN_SC[0]:IN_SC[1]]), 1),
        "w_swa": (cast(win[..., IN_SWA[0]:IN_SWA[1]]), 1),
        "w_fox": (cast(fox_cols), 1),
        "w_out": (cast(w["w_out"]), 1),
        "w_br": (cast(jnp.stack([w[n] for n in BRANCH_W], axis=1)), 3),
    }
    big = list(shards)
    conv_shard = jnp.concatenate([jnp.swapaxes(w["conf_dw"], 1, 2), jnp.swapaxes(w["sc_conv"], 1, 2)], axis=2)
    full = []
    for l in range(L):
        items = [(shards[n][0][l], "gather", shards[n][1] - 1) for n in big]
        if l == 0:
            items.append((conv_shard, "gather", 1))
        full.append(_exchange(items, f"gather_weights_l{l}"))
    conv_full = jnp.swapaxes(full[0][-1], 1, 2)
    conf_dw_full = _pad_axis(conv_full[:, :CONV_K], 1, CONV_HALO)
    sc_conv_full = _pad_axis(conv_full[:, CONV_K:], 1, SC_HALO)

    bucket = jnp.asarray(_swa_bucket_matrix(min(SWA_TQ, T)))
    consts = {"bucket": bucket, "expand": jnp.asarray(_kv_expand_matrix(), _MXU),
              "sel": jnp.asarray(_head_select_matrix(), _MXU),
              "bias": _swa_bias(w["rel_bias"], bucket, "swa_bias")}

    def layer_params(l):
        p = dict(zip(big, full[l]))
        for n in ("ffn1_norm", "mix_norm", "ffn2_norm", "conf_dw_b", "conf_ln_g", "conf_ln_b"):
            p[n] = w[n][l].reshape(1, -1)
        p["conf_dw"], p["sc_conv"] = conf_dw_full[l], sc_conv_full[l]
        p["swa_q_norm"], p["fox_q_norm"] = _tile_vec(w["swa_q_norm"][l], 4), _tile_vec(w["fox_q_norm"][l], 4)
        p["swa_k_norm"], p["fox_k_norm"] = _tile_vec(w["swa_k_norm"][l], 2), _tile_vec(w["fox_k_norm"][l], 4)
        p["swa_sink"] = w["swa_sink"][l].reshape(1, 4)
        p["b_forget"] = _pad_axis(w["b_forget"][l].reshape(1, 4), 1, LANE)
        return p

    params = [layer_params(l) for l in range(L)]
    saved = []
    cur = x
    for l, p in enumerate(params):
        x1, s1 = _ffn_forward(cur, p["ffn1_norm"], p["ffn1_w_gate"], p["ffn1_w_up"], p["ffn1_w_down"], f"l{l}_ffn1")
        x2, s2 = _mixer_forward(x1, p, consts, f"l{l}_mix")
        cur, s3 = _ffn_forward(x2, p["ffn2_norm"], p["ffn2_w_gate"], p["ffn2_w_up"], p["ffn2_w_down"], f"l{l}_ffn2")
        saved.append((s1, s2, s3))
    dcur, loss_part = _loss_grad(cur, loss_target)

    grads = [None] * L
    for l in reversed(range(L)):
        p = params[l]
        s1, s2, s3 = saved[l]
        g = {}
        dcur, g["ffn2_norm"], g["ffn2_w_gate"], g["ffn2_w_up"], g["ffn2_w_down"] = _ffn_backward(
            dcur, s3, p["ffn2_norm"], p["ffn2_w_gate"], p["ffn2_w_up"], p["ffn2_w_down"], f"l{l}_ffn2")
        dcur, gm = _mixer_backward(dcur, s2, p, consts, f"l{l}_mix")
        g.update(gm)
        dcur, g["ffn1_norm"], g["ffn1_w_gate"], g["ffn1_w_up"], g["ffn1_w_down"] = _ffn_backward(
            dcur, s1, p["ffn1_norm"], p["ffn1_w_gate"], p["ffn1_w_up"], p["ffn1_w_down"], f"l{l}_ffn1")
        grads[l] = g
    grad_x = dcur

    gsum = {n: [None] * L for n in big}
    for l in reversed(range(L)):
        received = _exchange([(grads[l][n], "scatter", shards[n][1] - 1) for n in big], f"scatter_grads_l{l}")
        for n, r in zip(big, received):
            gsum[n][l] = _sum8(r, f"sum_{n}_l{l}")
    gsum = {n: jnp.stack(parts) for n, parts in gsum.items()}
    gw = {}
    for n in ("ffn1_w_gate", "ffn1_w_up", "ffn2_w_gate", "ffn2_w_up"):
        gw[n] = gsum[n][:, :, :fs]
    for n in ("ffn1_w_down", "ffn2_w_down"):
        gw[n] = gsum[n][:, :fs, :]
    gw["w_out"] = gsum["w_out"]
    for i, n in enumerate(BRANCH_W):
        gw[n] = gsum["w_br"][:, i]
    gw["w_in"] = jnp.concatenate([gsum["w_conf"], gsum["w_sc"], gsum["w_swa"],
                                  gsum["w_fox"][..., :IN_FF[1] - IN_FOX[0]], gsum["w_zg"]], axis=-1)

    def small_partial(n):
        per_layer = [grads[l][n] for l in range(L)]
        if n == "rel_bias":
            return sum(pl_[:, :4] for pl_ in per_layer)
        if n in ("swa_sink", "b_forget"):
            return jnp.stack([a.reshape(-1)[:4] for a in per_layer])
        if n in ("swa_q_norm", "fox_q_norm", "fox_k_norm"):
            return jnp.stack([a.reshape(4, HEAD).sum(0) for a in per_layer])
        if n == "swa_k_norm":
            return jnp.stack([a.reshape(2, HEAD).sum(0) for a in per_layer])
        if n == "conf_dw":
            return jnp.stack([a[:CONV_K] for a in per_layer])
        if n == "sc_conv":
            return jnp.stack([a[:SC_K] for a in per_layer])
        return jnp.stack([a.reshape(-1) for a in per_layer])

    partial = [small_partial(n) for n in SMALL]
    small_shapes = [a.shape for a in partial]
    all_parts = _exchange([(_pack(partial), "gather", 0)], "gather_small_grads")[0]
    rows = all_parts.shape[0] // N_DEV
    small_sum = _unpack(_sum8(all_parts.reshape(N_DEV, rows, LANE), "sum_small"), small_shapes)
    for n, a in zip(SMALL, small_sum):
        if n in ("conf_dw", "sc_conv"):
            cs = w[n].shape[2]
            a = lax.dynamic_slice_in_dim(a, dev * cs, cs, axis=2)
        gw[n] = a

    delta, new_m, new_v = {}, {}, {}
    for n in W_NAMES:
        if n not in SMALL:
            delta[n], new_m[n], new_v[n] = _adamw(w[n], gw[n], m[n], v[n], f"adamw_{n}")
    shapes = [w[n].shape for n in SMALL]
    outs = _adamw(_pack([w[n] for n in SMALL]), _pack([gw[n] for n in SMALL]), _pack([m[n] for n in SMALL]),
                  _pack([v[n] for n in SMALL]), "adamw_small")
    for res, out in zip((delta, new_m, new_v), outs):
        for n, a in zip(SMALL, _unpack(out, shapes)):
            res[n] = a

    loss = lax.psum(loss_part[0, 0], ("x", "y", "c"))
    return loss, grad_x, gw, delta, new_m, new_v


def kernel(x, rel_bias, ffn1_norm, ffn1_w_gate, ffn1_w_up, ffn1_w_down, mix_norm, w_in, b_forget, conf_dw, conf_dw_b, conf_ln_g, conf_ln_b, conf_w_out, sc_conv, sc_w_out, swa_q_norm, swa_k_norm, swa_sink, swa_w_o, fox_q_norm, fox_k_norm, fox_w_o, w_out, ffn2_norm, ffn2_w_gate, ffn2_w_up, ffn2_w_down, loss_target, m_rel_bias, m_ffn1_norm, m_ffn1_w_gate, m_ffn1_w_up, m_ffn1_w_down, m_mix_norm, m_w_in, m_b_forget, m_conf_dw, m_conf_dw_b, m_conf_ln_g, m_conf_ln_b, m_conf_w_out, m_sc_conv, m_sc_w_out, m_swa_q_norm, m_swa_k_norm, m_swa_sink, m_swa_w_o, m_fox_q_norm, m_fox_k_norm, m_fox_w_o, m_w_out, m_ffn2_norm, m_ffn2_w_gate, m_ffn2_w_up, m_ffn2_w_down, v_rel_bias, v_ffn1_norm, v_ffn1_w_gate, v_ffn1_w_up, v_ffn1_w_down, v_mix_norm, v_w_in, v_b_forget, v_conf_dw, v_conf_dw_b, v_conf_ln_g, v_conf_ln_b, v_conf_w_out, v_sc_conv, v_sc_w_out, v_swa_q_norm, v_swa_k_norm, v_swa_sink, v_swa_w_o, v_fox_q_norm, v_fox_k_norm, v_fox_w_o, v_w_out, v_ffn2_norm, v_ffn2_w_gate, v_ffn2_w_up, v_ffn2_w_down):
    args = locals()
    w = {n: args[n] for n in W_NAMES}
    m = {n: args["m_" + n] for n in W_NAMES}
    v = {n: args["v_" + n] for n in W_NAMES}
    T, D = x.shape[-2:]
    loss, grad_x, gw, delta, new_m, new_v = _step(w, m, v, x.reshape(T, D), loss_target.reshape(T, D))
    return (loss, grad_x.reshape(x.shape), *[gw[n] for n in W_NAMES], *[delta[n] for n in W_NAMES],
            *[new_m[n] for n in W_NAMES], *[new_v[n] for n in W_NAMES])
```

```python
import math

import numpy as np
import jax
import jax.numpy as jnp
from jax import lax
from jax.experimental import pallas as pl
from jax.experimental.pallas import tpu as pltpu

F32 = jnp.float32
_MXU = jnp.bfloat16
EPS = 1e-6
NEG_INF = -1e30
HEAD = 64
CH = 256
WINDOW = 128
CONV_K = 31
SC_K = 3
CONV_HALO = 32
SC_HALO = 8
N_BUCKETS = 32
MAX_DISTANCE = 128
N_DEV = 8
LANE = 128
ROW_TILE = 512
VMEM_LIMIT = 48 * 1024 * 1024
ADAM_LR, ADAM_B1, ADAM_B2, ADAM_EPS, ADAM_WD, ADAM_STEP = 0.001, 0.9, 0.999, 1e-08, 0.01, 10

_NN = (((1,), (0,)), ((), ()))
_NT = (((1,), (1,)), ((), ()))
_TN = (((0,), (0,)), ((), ()))


def _cp(*sem):
    return pltpu.CompilerParams(dimension_semantics=sem, vmem_limit_bytes=VMEM_LIMIT)


def _tile(n, pref, align=LANE):
    t = (min(n, pref) // align) * align
    while t >= align:
        if n % t == 0:
            return t
        t -= align
    return n


def _dot(a, b, dims=_NN):
    return lax.dot_general(a.astype(_MXU), b.astype(_MXU), dims, preferred_element_type=F32)


def _split3(x):
    hi = x.astype(_MXU)
    r1 = x - hi.astype(F32)
    mid = r1.astype(_MXU)
    lo = (r1 - mid.astype(F32)).astype(_MXU)
    return hi, mid, lo


def _exact_dot(a, b, dims, data):
    if data == "a":
        return sum(lax.dot_general(p, b.astype(_MXU), dims, preferred_element_type=F32) for p in _split3(a))
    return sum(lax.dot_general(a.astype(_MXU), p, dims, preferred_element_type=F32) for p in _split3(b))


def _sigmoid(x):
    return jax.nn.sigmoid(x)


def _lane_mask(width, h):
    lane = lax.broadcasted_iota(jnp.int32, (1, width), 1)
    return (lane >= h * HEAD) & (lane < (h + 1) * HEAD)


def _head_rms(x, g, nh):
    xx = x * x
    ms = jnp.zeros_like(x)
    for h in range(nh):
        mk = _lane_mask(x.shape[-1], h)
        s = jnp.sum(jnp.where(mk, xx, 0.0), axis=-1, keepdims=True) * (1.0 / HEAD)
        ms = jnp.where(mk, s, ms)
    r = lax.rsqrt(ms + EPS)
    return x * r * g, r


def _head_rms_bwd(dy, x, r, g, nh):
    w = dy * g
    wx = w * x
    c = jnp.zeros_like(x)
    for h in range(nh):
        mk = _lane_mask(x.shape[-1], h)
        s = jnp.sum(jnp.where(mk, wx, 0.0), axis=-1, keepdims=True) * (1.0 / HEAD)
        c = jnp.where(mk, s, c)
    dx = r * w - x * (r * r * r) * c
    dg = jnp.sum(dy * x * r, axis=0, keepdims=True)
    return dx, dg


def _mm(pairs, mode, out_dtype, name, scale=None, res=None, tm=1024, tn=1024, tk=1024):
    a0, b0 = pairs[0]
    M = a0.shape[1] if mode == "tn" else a0.shape[0]
    N = b0.shape[0] if mode == "nt" else b0.shape[1]
    tm, tn = _tile(M, tm), _tile(N, tn)
    dims = {"nn": _NN, "nt": _NT, "tn": _TN}[mode]
    tks, nks, offs = [], [], []
    for a, _ in pairs:
        K = a.shape[0] if mode == "tn" else a.shape[1]
        t = _tile(K, tk)
        tks.append(t)
        nks.append(K // t)
        offs.append(sum(nks[:-1]))
    nk_tot = sum(nks)
    in_specs, args = [], []
    for (a, b), t, nk, off in zip(pairs, tks, nks, offs):
        def kk(k, off=off, nk=nk):
            return jnp.clip(k - off, 0, nk - 1)
        if mode == "tn":
            in_specs.append(pl.BlockSpec((t, tm), lambda i, j, k, kk=kk: (kk(k), i)))
        else:
            in_specs.append(pl.BlockSpec((tm, t), lambda i, j, k, kk=kk: (i, kk(k))))
        if mode == "nt":
            in_specs.append(pl.BlockSpec((tn, t), lambda i, j, k, kk=kk: (j, kk(k))))
        else:
            in_specs.append(pl.BlockSpec((t, tn), lambda i, j, k, kk=kk: (kk(k), j)))
        args += [a, b]
    if res is not None:
        in_specs.append(pl.BlockSpec((tm, tn), lambda i, j, k: (i, j)))
        args.append(res)
    npairs = len(pairs)

    def body(*refs):
        ab = refs[:2 * npairs]
        res_ref = refs[2 * npairs] if res is not None else None
        o_ref = refs[2 * npairs + (res is not None)]
        acc = refs[-1]
        k = pl.program_id(2)

        def finish(r):
            if scale is not None:
                r = r * scale
            if res_ref is not None:
                r = r + res_ref[...]
            o_ref[...] = r.astype(o_ref.dtype)

        if nk_tot == 1:
            finish(_dot(ab[0][...], ab[1][...], dims))
            return

        @pl.when(k == 0)
        def _():
            acc[...] = jnp.zeros_like(acc)

        for p in range(npairs):
            @pl.when(jnp.logical_and(k >= offs[p], k < offs[p] + nks[p]))
            def _(p=p):
                acc[...] += _dot(ab[2 * p][...], ab[2 * p + 1][...], dims)

        @pl.when(k == nk_tot - 1)
        def _():
            finish(acc[...])

    return pl.pallas_call(
        body, name=name, grid=(M // tm, N // tn, nk_tot), in_specs=in_specs,
        out_specs=pl.BlockSpec((tm, tn), lambda i, j, k: (i, j)),
        out_shape=jax.ShapeDtypeStruct((M, N), out_dtype),
        scratch_shapes=[pltpu.VMEM((tm, tn), F32)],
        compiler_params=_cp("parallel", "parallel", "arbitrary"))(*args)


def _rms_fwd(x, g, name):
    T, D = x.shape
    tm = _tile(T, ROW_TILE)

    def body(x_ref, g_ref, o_ref):
        xv = x_ref[...]
        r = lax.rsqrt(jnp.mean(xv * xv, axis=-1, keepdims=True) + EPS)
        o_ref[...] = (xv * r * g_ref[...]).astype(o_ref.dtype)

    return pl.pallas_call(
        body, name=name, grid=(T // tm,),
        in_specs=[pl.BlockSpec((tm, D), lambda i: (i, 0)), pl.BlockSpec((1, D), lambda i: (0, 0))],
        out_specs=pl.BlockSpec((tm, D), lambda i: (i, 0)),
        out_shape=jax.ShapeDtypeStruct((T, D), _MXU), compiler_params=_cp("parallel"))(x, g)


def _rms_bwd(da, x, g, dres, name):
    T, D = x.shape
    tm = _tile(T, ROW_TILE)

    def body(da_ref, x_ref, g_ref, dr_ref, dx_ref, dg_ref):
        @pl.when(pl.program_id(0) == 0)
        def _():
            dg_ref[...] = jnp.zeros_like(dg_ref)

        xv, dav = x_ref[...], da_ref[...]
        r = lax.rsqrt(jnp.mean(xv * xv, axis=-1, keepdims=True) + EPS)
        w = dav * g_ref[...]
        c = jnp.mean(w * xv, axis=-1, keepdims=True)
        dx_ref[...] = dr_ref[...] + (r * w - xv * (r * r * r) * c)
        dg_ref[...] += jnp.sum(dav * xv * r, axis=0, keepdims=True)

    row = pl.BlockSpec((tm, D), lambda i: (i, 0))
    vec = pl.BlockSpec((1, D), lambda i: (0, 0))
    return pl.pallas_call(
        body, name=name, grid=(T // tm,), in_specs=[row, row, vec, row], out_specs=[row, vec],
        out_shape=[jax.ShapeDtypeStruct((T, D), F32), jax.ShapeDtypeStruct((1, D), F32)],
        compiler_params=_cp("arbitrary"))(da, x, g, dres)


def _loss_grad(y, tgt):
    T, D = y.shape
    tm = _tile(T, ROW_TILE)

    def body(y_ref, t_ref, dy_ref, l_ref):
        @pl.when(pl.program_id(0) == 0)
        def _():
            l_ref[...] = jnp.zeros_like(l_ref)

        d = y_ref[...] - t_ref[...]
        dy_ref[...] = d * (1.0 / D)
        per_tok = jnp.mean(d * d, axis=-1, keepdims=True)
        l_ref[...] += 0.5 * jnp.sum(per_tok, axis=0, keepdims=True)

    row = pl.BlockSpec((tm, D), lambda i: (i, 0))
    return pl.pallas_call(
        body, name="loss_grad", grid=(T // tm,), in_specs=[row, row],
        out_specs=[row, pl.BlockSpec((1, 1), lambda i: (0, 0))],
        out_shape=[jax.ShapeDtypeStruct((T, D), F32), jax.ShapeDtypeStruct((1, 1), F32)],
        compiler_params=_cp("arbitrary"))(y, tgt)


def _ffn_up(a, wg, wu, name):
    T, D = a.shape
    Fp = wg.shape[1]
    tm, tn = _tile(T, ROW_TILE), _tile(Fp, 768)

    def body(a_ref, wg_ref, wu_ref, g_ref, u_ref, h_ref):
        av = a_ref[...]
        g = _dot(av, wg_ref[...])
        u = _dot(av, wu_ref[...])
        g_ref[...] = g.astype(g_ref.dtype)
        u_ref[...] = u.astype(u_ref.dtype)
        h_ref[...] = (g * _sigmoid(g) * u).astype(h_ref.dtype)

    wspec = pl.BlockSpec((D, tn), lambda j, i: (0, j))
    ospec = pl.BlockSpec((tm, tn), lambda j, i: (i, j))
    osh = jax.ShapeDtypeStruct((T, Fp), _MXU)
    return pl.pallas_call(
        body, name=name, grid=(Fp // tn, T // tm),
        in_specs=[pl.BlockSpec((tm, D), lambda j, i: (i, 0)), wspec, wspec],
        out_specs=[ospec, ospec, ospec], out_shape=[osh, osh, osh],
        compiler_params=_cp("parallel", "parallel"))(a, wg, wu)


def _ffn_bwd_hid(dxp, wd, gate, up, name):
    T, D = dxp.shape
    Fp = wd.shape[0]
    tm, tn = _tile(T, ROW_TILE), _tile(Fp, 768)

    def body(dx_ref, wd_ref, g_ref, u_ref, dg_ref, du_ref):
        dh = 0.5 * _dot(dx_ref[...], wd_ref[...], _NT)
        g = g_ref[...].astype(F32)
        u = u_ref[...].astype(F32)
        s = _sigmoid(g)
        du_ref[...] = (dh * (g * s)).astype(du_ref.dtype)
        dg_ref[...] = (dh * u * (s * (1.0 + g * (1.0 - s)))).astype(dg_ref.dtype)

    tspec = pl.BlockSpec((tm, tn), lambda j, i: (i, j))
    osh = jax.ShapeDtypeStruct((T, Fp), _MXU)
    return pl.pallas_call(
        body, name=name, grid=(Fp // tn, T // tm),
        in_specs=[pl.BlockSpec((tm, D), lambda j, i: (i, 0)), pl.BlockSpec((tn, D), lambda j, i: (j, 0)), tspec, tspec],
        out_specs=[tspec, tspec], out_shape=[osh, osh],
        compiler_params=_cp("parallel", "parallel"))(dxp, wd, gate, up)


def _conf_fwd(zc, dw, b, lng, lnb, name):
    T = zc.shape[0]
    tm = _tile(T, ROW_TILE)
    r = tm // CONV_HALO

    def body(z_ref, zh_ref, dw_ref, b_ref, g_ref, lb_ref, u1_ref, act_ref, ext):
        i = pl.program_id(0)
        cur = z_ref[...]
        ext[pl.ds(CONV_HALO, tm), :] = cur[:, :CH] * _sigmoid(cur[:, CH:])
        hal = zh_ref[...]
        ext[pl.ds(0, CONV_HALO), :] = jnp.where(i > 0, hal[:, :CH] * _sigmoid(hal[:, CH:]), 0.0)
        acc = jnp.zeros((tm, CH), F32)
        for k in range(CONV_K):
            acc = acc + dw_ref[pl.ds(k, 1), :] * ext[pl.ds(CONV_HALO - (CONV_K - 1) + k, tm), :]
        u1 = acc + b_ref[...]
        u1_ref[...] = u1
        mu = jnp.mean(u1, axis=-1, keepdims=True)
        var = jnp.mean(jnp.square(u1 - mu), axis=-1, keepdims=True)
        u2 = (u1 - mu) * lax.rsqrt(var + EPS) * g_ref[...] + lb_ref[...]
        act_ref[...] = u2 * _sigmoid(u2)

    vec = pl.BlockSpec((1, CH), lambda i: (0, 0))
    row = pl.BlockSpec((tm, CH), lambda i: (i, 0))
    osh = jax.ShapeDtypeStruct((T, CH), F32)
    return pl.pallas_call(
        body, name=name, grid=(T // tm,),
        in_specs=[pl.BlockSpec((tm, 2 * CH), lambda i: (i, 0)),
                  pl.BlockSpec((CONV_HALO, 2 * CH), lambda i: (jnp.maximum(i * r - 1, 0), 0)),
                  pl.BlockSpec((CONV_HALO, CH), lambda i: (0, 0)), vec, vec, vec],
        out_specs=[row, row], out_shape=[osh, osh],
        scratch_shapes=[pltpu.VMEM((tm + CONV_HALO, CH), F32)],
        compiler_params=_cp("parallel"))(zc, zc, dw, b, lng, lnb)


def _conf_bwd_ln(dact, u1, lng, lnb, name):
    T = u1.shape[0]
    tm = _tile(T, ROW_TILE)

    def body(da_ref, u_ref, g_ref, lb_ref, du_ref, sm_ref):
        @pl.when(pl.program_id(0) == 0)
        def _():
            sm_ref[...] = jnp.zeros_like(sm_ref)

        u1v = u_ref[...]
        mu = jnp.mean(u1v, axis=-1, keepdims=True)
        cen = u1v - mu
        rstd = lax.rsqrt(jnp.mean(cen * cen, axis=-1, keepdims=True) + EPS)
        y = cen * rstd
        u2 = y * g_ref[...] + lb_ref[...]
        s = _sigmoid(u2)
        du2 = da_ref[...] * (s * (1.0 + u2 * (1.0 - s)))
        dy = du2 * g_ref[...]
        du1 = rstd * (dy - jnp.mean(dy, axis=-1, keepdims=True) - y * jnp.mean(dy * y, axis=-1, keepdims=True))
        du_ref[...] = du1
        sm_ref[pl.ds(0, 1), :] += jnp.sum(du2 * y, axis=0, keepdims=True)
        sm_ref[pl.ds(1, 1), :] += jnp.sum(du2, axis=0, keepdims=True)
        sm_ref[pl.ds(2, 1), :] += jnp.sum(du1, axis=0, keepdims=True)

    vec = pl.BlockSpec((1, CH), lambda i: (0, 0))
    row = pl.BlockSpec((tm, CH), lambda i: (i, 0))
    return pl.pallas_call(
        body, name=name, grid=(T // tm,), in_specs=[row, row, vec, vec],
        out_specs=[row, pl.BlockSpec((8, CH), lambda i: (0, 0))],
        out_shape=[jax.ShapeDtypeStruct((T, CH), F32), jax.ShapeDtypeStruct((8, CH), F32)],
        compiler_params=_cp("arbitrary"))(dact, u1, lng, lnb)


def _conf_bwd_conv(zc, du1, dw, name):
    T = zc.shape[0]
    tm = _tile(T, ROW_TILE)
    r = tm // CONV_HALO
    nt = T // tm
    nh = T // CONV_HALO

    def body(z_ref, zh_ref, d_ref, dn_ref, dw_ref, dz_ref, ddw_ref, ext_u, ext_d):
        i = pl.program_id(0)

        @pl.when(i == 0)
        def _():
            ddw_ref[...] = jnp.zeros_like(ddw_ref)

        cur = z_ref[...]
        ca = cur[:, :CH]
        sg = _sigmoid(cur[:, CH:])
        ext_u[pl.ds(CONV_HALO, tm), :] = ca * sg
        hal = zh_ref[...]
        ext_u[pl.ds(0, CONV_HALO), :] = jnp.where(i > 0, hal[:, :CH] * _sigmoid(hal[:, CH:]), 0.0)
        d = d_ref[...]
        ext_d[pl.ds(0, tm), :] = d
        ext_d[pl.ds(tm, CONV_HALO), :] = jnp.where(i < nt - 1, dn_ref[...], 0.0)
        acc = jnp.zeros((tm, CH), F32)
        for k in range(CONV_K):
            acc = acc + dw_ref[pl.ds(k, 1), :] * ext_d[pl.ds(CONV_K - 1 - k, tm), :]
            ddw_ref[pl.ds(k, 1), :] += jnp.sum(
                d * ext_u[pl.ds(CONV_HALO - (CONV_K - 1) + k, tm), :], axis=0, keepdims=True)
        dz_ref[:, :CH] = (acc * sg).astype(dz_ref.dtype)
        dz_ref[:, CH:] = (acc * ca * sg * (1.0 - sg)).astype(dz_ref.dtype)

    return pl.pallas_call(
        body, name=name, grid=(nt,),
        in_specs=[pl.BlockSpec((tm, 2 * CH), lambda i: (i, 0)),
                  pl.BlockSpec((CONV_HALO, 2 * CH), lambda i: (jnp.maximum(i * r - 1, 0), 0)),
                  pl.BlockSpec((tm, CH), lambda i: (i, 0)),
                  pl.BlockSpec((CONV_HALO, CH), lambda i: (jnp.minimum((i + 1) * r, nh - 1), 0)),
                  pl.BlockSpec((CONV_HALO, CH), lambda i: (0, 0))],
        out_specs=[pl.BlockSpec((tm, 2 * CH), lambda i: (i, 0)), pl.BlockSpec((CONV_HALO, CH), lambda i: (0, 0))],
        out_shape=[jax.ShapeDtypeStruct((T, 2 * CH), _MXU), jax.ShapeDtypeStruct((CONV_HALO, CH), F32)],
        scratch_shapes=[pltpu.VMEM((tm + CONV_HALO, CH), F32), pltpu.VMEM((tm + CONV_HALO, CH), F32)],
        compiler_params=_cp("arbitrary"))(zc, zc, du1, du1, dw)


def _sc_fwd(zs, w, name):
    T = zs.shape[0]
    tm = _tile(T, ROW_TILE)
    r = tm // SC_HALO

    def body(z_ref, zh_ref, w_ref, act_ref, ext):
        i = pl.program_id(0)
        cur = z_ref[...]
        ext[pl.ds(SC_HALO, tm), :] = cur[:, CH:2 * CH] * cur[:, 2 * CH:]
        hal = zh_ref[...]
        ext[pl.ds(0, SC_HALO), :] = jnp.where(i > 0, hal[:, CH:2 * CH] * hal[:, 2 * CH:], 0.0)
        v1 = jnp.zeros((tm, CH), F32)
        for k in range(SC_K):
            v1 = v1 + w_ref[pl.ds(k, 1), :] * ext[pl.ds(SC_HALO - (SC_K - 1) + k, tm), :]
        act_ref[...] = cur[:, :CH] * v1

    return pl.pallas_call(
        body, name=name, grid=(T // tm,),
        in_specs=[pl.BlockSpec((tm, 3 * CH), lambda i: (i, 0)),
                  pl.BlockSpec((SC_HALO, 3 * CH), lambda i: (jnp.maximum(i * r - 1, 0), 0)),
                  pl.BlockSpec((SC_HALO, CH), lambda i: (0, 0))],
        out_specs=pl.BlockSpec((tm, CH), lambda i: (i, 0)),
        out_shape=jax.ShapeDtypeStruct((T, CH), F32),
        scratch_shapes=[pltpu.VMEM((tm + SC_HALO, CH), F32)],
        compiler_params=_cp("parallel"))(zs, zs, w)


def _sc_bwd(zs, dact, w, name):
    T = zs.shape[0]
    tm = _tile(T, ROW_TILE)
    r = tm // SC_HALO
    nt = T // tm
    nh = T // SC_HALO

    def body(z_ref, zh_ref, zn_ref, d_ref, dn_ref, w_ref, dz_ref, dw_ref, ext_v, ext_d):
        i = pl.program_id(0)

        @pl.when(i == 0)
        def _():
            dw_ref[...] = jnp.zeros_like(dw_ref)

        cur = z_ref[...]
        sb, sc, sx = cur[:, :CH], cur[:, CH:2 * CH], cur[:, 2 * CH:]
        ext_v[pl.ds(SC_HALO, tm), :] = sc * sx
        hal = zh_ref[...]
        ext_v[pl.ds(0, SC_HALO), :] = jnp.where(i > 0, hal[:, CH:2 * CH] * hal[:, 2 * CH:], 0.0)
        da = d_ref[...]
        dv1 = da * sb
        ext_d[pl.ds(0, tm), :] = dv1
        ext_d[pl.ds(tm, SC_HALO), :] = jnp.where(i < nt - 1, dn_ref[...] * zn_ref[...][:, :CH], 0.0)
        v1 = jnp.zeros((tm, CH), F32)
        dv0 = jnp.zeros((tm, CH), F32)
        for k in range(SC_K):
            shifted = ext_v[pl.ds(SC_HALO - (SC_K - 1) + k, tm), :]
            v1 = v1 + w_ref[pl.ds(k, 1), :] * shifted
            dv0 = dv0 + w_ref[pl.ds(k, 1), :] * ext_d[pl.ds(SC_K - 1 - k, tm), :]
            dw_ref[pl.ds(k, 1), :] += jnp.sum(dv1 * shifted, axis=0, keepdims=True)
        dz_ref[:, :CH] = (da * v1).astype(dz_ref.dtype)
        dz_ref[:, CH:2 * CH] = (dv0 * sx).astype(dz_ref.dtype)
        dz_ref[:, 2 * CH:] = (dv0 * sc).astype(dz_ref.dtype)

    return pl.pallas_call(
        body, name=name, grid=(nt,),
        in_specs=[pl.BlockSpec((tm, 3 * CH), lambda i: (i, 0)),
                  pl.BlockSpec((SC_HALO, 3 * CH), lambda i: (jnp.maximum(i * r - 1, 0), 0)),
                  pl.BlockSpec((SC_HALO, 3 * CH), lambda i: (jnp.minimum((i + 1) * r, nh - 1), 0)),
                  pl.BlockSpec((tm, CH), lambda i: (i, 0)),
                  pl.BlockSpec((SC_HALO, CH), lambda i: (jnp.minimum((i + 1) * r, nh - 1), 0)),
                  pl.BlockSpec((SC_HALO, CH), lambda i: (0, 0))],
        out_specs=[pl.BlockSpec((tm, 3 * CH), lambda i: (i, 0)), pl.BlockSpec((SC_HALO, CH), lambda i: (0, 0))],
        out_shape=[jax.ShapeDtypeStruct((T, 3 * CH), _MXU), jax.ShapeDtypeStruct((SC_HALO, CH), F32)],
        scratch_shapes=[pltpu.VMEM((tm + SC_HALO, CH), F32), pltpu.VMEM((tm + SC_HALO, CH), F32)],
        compiler_params=_cp("arbitrary"))(zs, zs, zs, dact, dact, w)


SWA_TQ = 256


def _t5_bucket_np(dist):
    max_exact = N_BUCKETS // 2
    d = np.maximum(dist, 1).astype(np.float32)
    large = max_exact + (np.log(d / np.float32(max_exact)) / np.float32(math.log(MAX_DISTANCE / max_exact))
                         * np.float32(N_BUCKETS - max_exact)).astype(np.int32)
    large = np.minimum(large, N_BUCKETS - 1)
    return np.where(dist < max_exact, dist, large).astype(np.int32)


def _swa_bucket_matrix(tq):
    dist = WINDOW + np.arange(tq)[:, None] - np.arange(tq + WINDOW)[None, :]
    ok = (dist >= 0) & (dist < WINDOW)
    return np.where(ok, _t5_bucket_np(np.maximum(dist, 0)), -1).astype(np.int32)


def _kv_expand_matrix():
    e = np.zeros((2 * HEAD, 4 * HEAD), np.float32)
    for h in range(4):
        for d in range(HEAD):
            e[(h // 2) * HEAD + d, h * HEAD + d] = 1.0
    return e


def _swa_bias(rel_bias, bucket, name):
    tq, tk = bucket.shape

    def body(rb_ref, bk_ref, o_ref):
        h = pl.program_id(0)
        bk = bk_ref[...]
        acc = jnp.full((tq, tk), NEG_INF, F32)
        for b in range(N_BUCKETS):
            acc = jnp.where(bk == b, rb_ref[b, h], acc)
        o_ref[0] = acc

    return pl.pallas_call(
        body, name=name, grid=(4,),
        in_specs=[pl.BlockSpec(memory_space=pltpu.SMEM), pl.BlockSpec((tq, tk), lambda h: (0, 0))],
        out_specs=pl.BlockSpec((1, tq, tk), lambda h: (h, 0, 0)),
        out_shape=jax.ShapeDtypeStruct((4, tq, tk), F32), compiler_params=_cp("parallel"))(rel_bias, bucket)


def _swa_probs(qh, kx, bm, first_col, sk):
    s = _dot(qh, kx, _NT) * (HEAD ** -0.5)
    col = lax.broadcasted_iota(jnp.int32, s.shape, 1)
    valid = (bm > 0.5 * NEG_INF) & (col >= first_col)
    s = jnp.where(valid, s + bm, NEG_INF)
    m = jnp.maximum(jnp.max(s, axis=-1, keepdims=True), sk)
    p = jnp.exp(s - m)
    den = jnp.sum(p, axis=-1, keepdims=True) + jnp.exp(sk - m)
    return p / den, m, den


def _swa_fwd(zw, gq, gk, sink, bias, expand, name):
    T = zw.shape[0]
    tq = bias.shape[1]
    r = tq // WINDOW

    def body(z_ref, zh_ref, gq_ref, gk_ref, sink_ref, b_ref, e_ref, o_ref, kext, vext):
        i = pl.program_id(0)
        cur = z_ref[...]
        qn, _ = _head_rms(cur[:, :4 * HEAD], gq_ref[...], 4)
        kc, _ = _head_rms(cur[:, 4 * HEAD:6 * HEAD], gk_ref[...], 2)
        hal = zh_ref[...]
        kp, _ = _head_rms(hal[:, :2 * HEAD], gk_ref[...], 2)
        kext[pl.ds(0, WINDOW), :] = kp
        kext[pl.ds(WINDOW, tq), :] = kc
        vext[pl.ds(0, WINDOW), :] = hal[:, 2 * HEAD:]
        vext[pl.ds(WINDOW, tq), :] = cur[:, 6 * HEAD:]
        kx = _dot(kext[...], e_ref[...]).astype(_MXU)
        vx = _dot(vext[...], e_ref[...]).astype(_MXU)
        first_col = jnp.where(i > 0, 0, WINDOW)
        out = jnp.zeros((tq, 4 * HEAD), F32)
        for h in range(4):
            mk = _lane_mask(4 * HEAD, h)
            qh = jnp.where(mk, qn, 0.0)
            pn, _, _ = _swa_probs(qh, kx, b_ref[h], first_col, sink_ref[0, h])
            out = jnp.where(mk, _dot(pn, vx), out)
        o_ref[...] = out

    return pl.pallas_call(
        body, name=name, grid=(T // tq,),
        in_specs=[pl.BlockSpec((tq, 8 * HEAD), lambda i: (i, 0)),
                  pl.BlockSpec((WINDOW, 4 * HEAD), lambda i: (jnp.maximum(i * r - 1, 0), 1)),
                  pl.BlockSpec((1, 4 * HEAD), lambda i: (0, 0)), pl.BlockSpec((1, 2 * HEAD), lambda i: (0, 0)),
                  pl.BlockSpec(memory_space=pltpu.SMEM),
                  pl.BlockSpec(bias.shape, lambda i: (0, 0, 0)),
                  pl.BlockSpec(expand.shape, lambda i: (0, 0))],
        out_specs=pl.BlockSpec((tq, 4 * HEAD), lambda i: (i, 0)),
        out_shape=jax.ShapeDtypeStruct((T, 4 * HEAD), F32),
        scratch_shapes=[pltpu.VMEM((tq + WINDOW, 2 * HEAD), F32), pltpu.VMEM((tq + WINDOW, 2 * HEAD), F32)],
        compiler_params=_cp("parallel"))(zw, zw, gq, gk, sink, bias, expand)


def _swa_bwd(zw, dact, gq, gk, sink, bias, bucket, expand, name):
    T = zw.shape[0]
    tq = bias.shape[1]
    tk = tq + WINDOW
    r = tq // WINDOW
    nt = T // tq
    nb = T // WINDOW
    scale = HEAD ** -0.5

    def body(z_ref, zh_ref, zn_ref, d_ref, dn_ref, gq_ref, gk_ref, sink_ref, b_ref, bk_ref, e_ref,
             dz_ref, dgq_ref, dgk_ref, dsk_ref, drb_ref, kext, vext, dk_s, dv_s, db_s):
        i = pl.program_id(0)

        @pl.when(i == 0)
        def _():
            dgq_ref[...] = jnp.zeros_like(dgq_ref)
            dgk_ref[...] = jnp.zeros_like(dgk_ref)
            dsk_ref[...] = jnp.zeros_like(dsk_ref)
            drb_ref[...] = jnp.zeros_like(drb_ref)
            db_s[...] = jnp.zeros_like(db_s)

        lane = lax.broadcasted_iota(jnp.int32, (1, LANE), 1)
        cur = z_ref[...]
        q_raw, k_raw = cur[:, :4 * HEAD], cur[:, 4 * HEAD:6 * HEAD]
        qn, q_r = _head_rms(q_raw, gq_ref[...], 4)
        kc, k_r = _head_rms(k_raw, gk_ref[...], 2)
        hal = zh_ref[...]
        kp, _ = _head_rms(hal[:, :2 * HEAD], gk_ref[...], 2)
        kext[pl.ds(0, WINDOW), :] = kp
        kext[pl.ds(WINDOW, tq), :] = kc
        vext[pl.ds(0, WINDOW), :] = hal[:, 2 * HEAD:]
        vext[pl.ds(WINDOW, tq), :] = cur[:, 6 * HEAD:]
        ev = e_ref[...]
        kx = _dot(kext[...], ev).astype(_MXU)
        vx = _dot(vext[...], ev).astype(_MXU)
        first_col = jnp.where(i > 0, 0, WINDOW)
        do = d_ref[...]
        dq = jnp.zeros((tq, 4 * HEAD), F32)
        dkx = jnp.zeros((tk, 4 * HEAD), F32)
        dvx = jnp.zeros((tk, 4 * HEAD), F32)
        dsk = jnp.zeros((1, LANE), F32)
        for h in range(4):
            mk = _lane_mask(4 * HEAD, h)
            qh = jnp.where(mk, qn, 0.0).astype(_MXU)
            sk = sink_ref[0, h]
            pn, m, den = _swa_probs(qh, kx, b_ref[h], first_col, sk)
            doh = jnp.where(mk, do, 0.0).astype(_MXU)
            dpn = _dot(doh, vx, _NT)
            delta = jnp.sum(pn * dpn, axis=-1, keepdims=True)
            ds = pn * (dpn - delta)
            psink = jnp.exp(sk - m) / den
            dsk = dsk + jnp.where(lane == h, jnp.sum(-psink * delta, axis=0, keepdims=True), 0.0)
            db_s[h] += ds
            dss = (ds * scale).astype(_MXU)
            dq = dq + jnp.where(mk, _dot(dss, kx), 0.0)
            dkx = dkx + _dot(dss, qh, _TN)
            dvx = dvx + _dot(pn, doh, _TN)
        dsk_ref[...] += dsk
        dk_ext = _exact_dot(dkx, ev, _NT, "a")
        dv_ext = _exact_dot(dvx, ev, _NT, "a")
        dk_s[...] = dk_ext[WINDOW:, :]
        dv_s[...] = dv_ext[WINDOW:, :]

        @pl.when(i < nt - 1)
        def _():
            nxt = zn_ref[...]
            q2, _ = _head_rms(nxt[:, :4 * HEAD], gq_ref[...], 4)
            k2n, _ = _head_rms(nxt[:, 4 * HEAD:6 * HEAD], gk_ref[...], 2)
            k2 = jnp.concatenate([kc[tq - WINDOW:, :], k2n], axis=0)
            v2 = jnp.concatenate([cur[tq - WINDOW:, 6 * HEAD:], nxt[:, 6 * HEAD:]], axis=0)
            k2x = _dot(k2, ev).astype(_MXU)
            v2x = _dot(v2, ev).astype(_MXU)
            do2 = dn_ref[...]
            dk2x = jnp.zeros((2 * WINDOW, 4 * HEAD), F32)
            dv2x = jnp.zeros((2 * WINDOW, 4 * HEAD), F32)
            for h in range(4):
                mk = _lane_mask(4 * HEAD, h)
                qh = jnp.where(mk, q2, 0.0).astype(_MXU)
                pn, _, _ = _swa_probs(qh, k2x, b_ref[h][:WINDOW, :2 * WINDOW], 0, sink_ref[0, h])
                doh = jnp.where(mk, do2, 0.0).astype(_MXU)
                dpn = _dot(doh, v2x, _NT)
                ds = pn * (dpn - jnp.sum(pn * dpn, axis=-1, keepdims=True))
                dk2x = dk2x + _dot((ds * scale).astype(_MXU), qh, _TN)
                dv2x = dv2x + _dot(pn, doh, _TN)
            dk_s[pl.ds(tq - WINDOW, WINDOW), :] += _exact_dot(dk2x, ev, _NT, "a")[:WINDOW, :]
            dv_s[pl.ds(tq - WINDOW, WINDOW), :] += _exact_dot(dv2x, ev, _NT, "a")[:WINDOW, :]

        dq_raw, dgq = _head_rms_bwd(dq, q_raw, q_r, gq_ref[...], 4)
        dk_raw, dgk = _head_rms_bwd(dk_s[...], k_raw, k_r, gk_ref[...], 2)
        dgq_ref[...] += dgq
        dgk_ref[...] += dgk
        dz_ref[:, :4 * HEAD] = dq_raw.astype(dz_ref.dtype)
        dz_ref[:, 4 * HEAD:6 * HEAD] = dk_raw.astype(dz_ref.dtype)
        dz_ref[:, 6 * HEAD:] = dv_s[...].astype(dz_ref.dtype)

        @pl.when(i == nt - 1)
        def _():
            bk = bk_ref[...]
            for b in range(N_BUCKETS):
                rowv = jnp.zeros((1, LANE), F32)
                for h in range(4):
                    s1 = jnp.sum(jnp.where(bk == b, db_s[h], 0.0), axis=0, keepdims=True)
                    rowv = jnp.where(lane == h, jnp.sum(s1, axis=1, keepdims=True), rowv)
                drb_ref[pl.ds(b, 1), :] = rowv

    const2 = lambda i: (0, 0)
    return pl.pallas_call(
        body, name=name, grid=(nt,),
        in_specs=[pl.BlockSpec((tq, 8 * HEAD), lambda i: (i, 0)),
                  pl.BlockSpec((WINDOW, 4 * HEAD), lambda i: (jnp.maximum(i * r - 1, 0), 1)),
                  pl.BlockSpec((WINDOW, 8 * HEAD), lambda i: (jnp.minimum((i + 1) * r, nb - 1), 0)),
                  pl.BlockSpec((tq, 4 * HEAD), lambda i: (i, 0)),
                  pl.BlockSpec((WINDOW, 4 * HEAD), lambda i: (jnp.minimum((i + 1) * r, nb - 1), 0)),
                  pl.BlockSpec((1, 4 * HEAD), const2), pl.BlockSpec((1, 2 * HEAD), const2),
                  pl.BlockSpec(memory_space=pltpu.SMEM),
                  pl.BlockSpec(bias.shape, lambda i: (0, 0, 0)),
                  pl.BlockSpec(bucket.shape, const2), pl.BlockSpec(expand.shape, const2)],
        out_specs=[pl.BlockSpec((tq, 8 * HEAD), lambda i: (i, 0)),
                   pl.BlockSpec((1, 4 * HEAD), const2), pl.BlockSpec((1, 2 * HEAD), const2),
                   pl.BlockSpec((1, LANE), const2), pl.BlockSpec((N_BUCKETS, LANE), const2)],
        out_shape=[jax.ShapeDtypeStruct((T, 8 * HEAD), _MXU), jax.ShapeDtypeStruct((1, 4 * HEAD), F32),
                   jax.ShapeDtypeStruct((1, 2 * HEAD), F32), jax.ShapeDtypeStruct((1, LANE), F32),
                   jax.ShapeDtypeStruct((N_BUCKETS, LANE), F32)],
        scratch_shapes=[pltpu.VMEM((tk, 2 * HEAD), F32), pltpu.VMEM((tk, 2 * HEAD), F32),
                        pltpu.VMEM((tq, 2 * HEAD), F32), pltpu.VMEM((tq, 2 * HEAD), F32),
                        pltpu.VMEM((4, tq, tk), F32)],
        compiler_params=_cp("arbitrary"))(zw, zw, zw, dact, dact, gq, gk, sink, bias, bucket, expand)


FOX_B = 512
FOX_TM = 256


def _tri(n, lower):
    m = np.tril(np.ones((n, n), np.float32)) if lower else np.triu(np.ones((n, n), np.float32))
    return m


def _log_sigmoid(x):
    return jnp.minimum(x, 0.0) - jnp.log1p(jnp.exp(-jnp.abs(x)))


def _fox_prep(zf, gq, gk, bf, name):
    T = zf.shape[0]
    tm = _tile(T, FOX_TM)
    lower = jnp.asarray(_tri(tm, True), _MXU)

    def body(z_ref, gq_ref, gk_ref, bf_ref, l_ref, q_ref, k_ref, v_ref, f_ref, ft_ref, carry):
        @pl.when(pl.program_id(0) == 0)
        def _():
            carry[...] = jnp.zeros_like(carry)

        z = z_ref[...]
        q, _ = _head_rms(z[:, :CH], gq_ref[...], 4)
        k, _ = _head_rms(z[:, CH:2 * CH], gk_ref[...], 4)
        q_ref[...] = q.astype(q_ref.dtype)
        k_ref[...] = k.astype(k_ref.dtype)
        v_ref[...] = z[:, 2 * CH:3 * CH].astype(v_ref.dtype)
        lane = lax.broadcasted_iota(jnp.int32, (1, LANE), 1)
        lf = jnp.where(lane < 4, _log_sigmoid(z[:, 3 * CH:] + bf_ref[...]), 0.0)
        fv = _exact_dot(l_ref[...], lf, _NN, "b") + carry[pl.ds(0, 1), :]
        f_ref[...] = fv
        ft_ref[...] = fv.T
        carry[pl.ds(0, 1), :] = f_ref[pl.ds(tm - 1, 1), :]

    row = pl.BlockSpec((tm, CH), lambda i: (i, 0))
    vec = pl.BlockSpec((1, CH), lambda i: (0, 0))
    qsh = jax.ShapeDtypeStruct((T, CH), _MXU)
    return pl.pallas_call(
        body, name=name, grid=(T // tm,),
        in_specs=[pl.BlockSpec((tm, 3 * CH + LANE), lambda i: (i, 0)), vec, vec,
                  pl.BlockSpec((1, LANE), lambda i: (0, 0)), pl.BlockSpec((tm, tm), lambda i: (0, 0))],
        out_specs=[row, row, row, pl.BlockSpec((tm, LANE), lambda i: (i, 0)), pl.BlockSpec((LANE, tm), lambda i: (0, i))],
        out_shape=[qsh, qsh, qsh, jax.ShapeDtypeStruct((T, LANE), F32), jax.ShapeDtypeStruct((LANE, T), F32)],
        scratch_shapes=[pltpu.VMEM((8, LANE), F32)],
        compiler_params=_cp("arbitrary"))(zf, gq, gk, bf, lower)


def _lane_col(x, h):
    lane = lax.broadcasted_iota(jnp.int32, (1, x.shape[-1]), 1)
    return jnp.sum(jnp.where(lane == h, x, 0.0), axis=-1, keepdims=True)


def _fox_scores(qh, k, fq, ft_ref, h, qi, ki, B):
    s = _dot(qh, k, _NT) * (HEAD ** -0.5)
    s = s + (fq - ft_ref[pl.ds(h, 1), :])
    row = qi * B + lax.broadcasted_iota(jnp.int32, s.shape, 0)
    col = ki * B + lax.broadcasted_iota(jnp.int32, s.shape, 1)
    return jnp.where(col <= row, s, NEG_INF)


def _fox_fwd(q, k, v, f, ft, name):
    T = q.shape[0]
    B = _tile(T, FOX_B)
    n = T // B

    def body(q_ref, k_ref, v_ref, f_ref, ft_ref, o_ref, lse_ref, m_s, l_s, acc):
        qi, ki = pl.program_id(0), pl.program_id(1)

        @pl.when(ki == 0)
        def _():
            m_s[...] = jnp.full_like(m_s, NEG_INF)
            l_s[...] = jnp.zeros_like(l_s)
            acc[...] = jnp.zeros_like(acc)

        @pl.when(ki <= qi)
        def _():
            qv, kv, vv, fv = q_ref[...], k_ref[...], v_ref[...], f_ref[...]
            for h in range(4):
                mk = _lane_mask(CH, h)
                qh = jnp.where(mk, qv, jnp.zeros_like(qv))
                s = _fox_scores(qh, kv, _lane_col(fv, h), ft_ref, h, qi, ki, B)
                m_old = m_s[h]
                m_new = jnp.maximum(m_old, jnp.max(s, axis=-1, keepdims=True))
                alpha = jnp.exp(m_old - m_new)
                p = jnp.exp(s - m_new)
                l_s[h] = alpha * l_s[h] + jnp.sum(p, axis=-1, keepdims=True)
                m_s[h] = m_new
                acc[...] = jnp.where(mk, acc[...] * alpha + _dot(p, vv), acc[...])

        @pl.when(ki == qi)
        def _():
            lane = lax.broadcasted_iota(jnp.int32, (1, LANE), 1)
            out = acc[...]
            lse = jnp.zeros((B, LANE), F32)
            for h in range(4):
                out = jnp.where(_lane_mask(CH, h), out / l_s[h], out)
                lse = jnp.where(lane == h, m_s[h] + jnp.log(l_s[h]), lse)
            o_ref[...] = out
            lse_ref[...] = lse

    qspec = pl.BlockSpec((B, CH), lambda qi, ki: (qi, 0))
    kspec = pl.BlockSpec((B, CH), lambda qi, ki: (jnp.minimum(ki, qi), 0))
    return pl.pallas_call(
        body, name=name, grid=(n, n),
        in_specs=[qspec, kspec, kspec, pl.BlockSpec((B, LANE), lambda qi, ki: (qi, 0)),
                  pl.BlockSpec((8, B), lambda qi, ki: (0, jnp.minimum(ki, qi)))],
        out_specs=[qspec, pl.BlockSpec((B, LANE), lambda qi, ki: (qi, 0))],
        out_shape=[jax.ShapeDtypeStruct((T, CH), F32), jax.ShapeDtypeStruct((T, LANE), F32)],
        scratch_shapes=[pltpu.VMEM((4, B, 1), F32), pltpu.VMEM((4, B, 1), F32), pltpu.VMEM((B, CH), F32)],
        compiler_params=_cp("parallel", "arbitrary"))(q, k, v, f, ft)


def _fox_delta(o, do, name):
    T = o.shape[0]
    tm = _tile(T, ROW_TILE)

    def body(o_ref, d_ref, out_ref):
        prod = o_ref[...] * d_ref[...]
        lane = lax.broadcasted_iota(jnp.int32, (1, LANE), 1)
        out = jnp.zeros((tm, LANE), F32)
        for h in range(4):
            s = jnp.sum(jnp.where(_lane_mask(CH, h), prod, 0.0), axis=-1, keepdims=True)
            out = jnp.where(lane == h, s, out)
        out_ref[...] = out

    row = pl.BlockSpec((tm, CH), lambda i: (i, 0))
    return pl.pallas_call(
        body, name=name, grid=(T // tm,), in_specs=[row, row],
        out_specs=pl.BlockSpec((tm, LANE), lambda i: (i, 0)),
        out_shape=jax.ShapeDtypeStruct((T, LANE), F32), compiler_params=_cp("parallel"))(o, do)


def _fox_bwd_dq(q, k, v, f, ft, lse, delta, do, name):
    T = q.shape[0]
    B = _tile(T, FOX_B)
    n = T // B

    def body(q_ref, k_ref, v_ref, f_ref, ft_ref, lse_ref, dl_ref, do_ref, dq_ref, dfq_ref, dq_s, df_s):
        qi, ki = pl.program_id(0), pl.program_id(1)

        @pl.when(ki == 0)
        def _():
            dq_s[...] = jnp.zeros_like(dq_s)
            df_s[...] = jnp.zeros_like(df_s)

        @pl.when(ki <= qi)
        def _():
            qv, kv, vv, fv = q_ref[...], k_ref[...], v_ref[...], f_ref[...]
            lsev, dlv, dov = lse_ref[...], dl_ref[...], do_ref[...]
            lane = lax.broadcasted_iota(jnp.int32, (1, LANE), 1)
            for h in range(4):
                mk = _lane_mask(CH, h)
                qh = jnp.where(mk, qv, jnp.zeros_like(qv))
                s = _fox_scores(qh, kv, _lane_col(fv, h), ft_ref, h, qi, ki, B)
                p = jnp.exp(s - _lane_col(lsev, h))
                doh = jnp.where(mk, dov, 0.0)
                ds = p * (_dot(doh, vv, _NT) - _lane_col(dlv, h))
                dq_s[...] += jnp.where(mk, _dot(ds * (HEAD ** -0.5), kv), 0.0)
                df_s[...] += jnp.where(lane == h, jnp.sum(ds, axis=-1, keepdims=True), 0.0)

        @pl.when(ki == qi)
        def _():
            dq_ref[...] = dq_s[...]
            dfq_ref[...] = df_s[...]

    qspec = pl.BlockSpec((B, CH), lambda qi, ki: (qi, 0))
    kspec = pl.BlockSpec((B, CH), lambda qi, ki: (jnp.minimum(ki, qi), 0))
    lspec = pl.BlockSpec((B, LANE), lambda qi, ki: (qi, 0))
    return pl.pallas_call(
        body, name=name, grid=(n, n),
        in_specs=[qspec, kspec, kspec, lspec, pl.BlockSpec((8, B), lambda qi, ki: (0, jnp.minimum(ki, qi))),
                  lspec, lspec, qspec],
        out_specs=[qspec, lspec],
        out_shape=[jax.ShapeDtypeStruct((T, CH), F32), jax.ShapeDtypeStruct((T, LANE), F32)],
        scratch_shapes=[pltpu.VMEM((B, CH), F32), pltpu.VMEM((B, LANE), F32)],
        compiler_params=_cp("parallel", "arbitrary"))(q, k, v, f, ft, lse, delta, do)


def _fox_bwd_dkv(q, k, v, f, ft, lse, delta, do, name):
    T = q.shape[0]
    B = _tile(T, FOX_B)
    n = T // B

    def body(q_ref, k_ref, v_ref, f_ref, ft_ref, lse_ref, dl_ref, do_ref, dk_ref, dv_ref, dft_ref, dk_s, dv_s, df_s):
        ki, qi = pl.program_id(0), pl.program_id(1)

        @pl.when(qi == 0)
        def _():
            dk_s[...] = jnp.zeros_like(dk_s)
            dv_s[...] = jnp.zeros_like(dv_s)
            df_s[...] = jnp.zeros_like(df_s)

        @pl.when(qi >= ki)
        def _():
            qv, kv, vv, fv = q_ref[...], k_ref[...], v_ref[...], f_ref[...]
            lsev, dlv, dov = lse_ref[...], dl_ref[...], do_ref[...]
            for h in range(4):
                mk = _lane_mask(CH, h)
                qh = jnp.where(mk, qv, jnp.zeros_like(qv))
                s = _fox_scores(qh, kv, _lane_col(fv, h), ft_ref, h, qi, ki, B)
                p = jnp.exp(s - _lane_col(lsev, h))
                doh = jnp.where(mk, dov, 0.0)
                ds = p * (_dot(doh, vv, _NT) - _lane_col(dlv, h))
                dv_s[...] += _dot(p, doh, _TN)
                dk_s[...] += _dot(ds * (HEAD ** -0.5), qh, _TN)
                df_s[pl.ds(h, 1), :] -= jnp.sum(ds, axis=0, keepdims=True)

        @pl.when(qi == n - 1)
        def _():
            dk_ref[...] = dk_s[...]
            dv_ref[...] = dv_s[...]
            dft_ref[...] = jnp.zeros_like(dft_ref)
            dft_ref[pl.ds(0, 8), :] = df_s[...]

    qspec = pl.BlockSpec((B, CH), lambda ki, qi: (jnp.maximum(qi, ki), 0))
    kspec = pl.BlockSpec((B, CH), lambda ki, qi: (ki, 0))
    lspec = pl.BlockSpec((B, LANE), lambda ki, qi: (jnp.maximum(qi, ki), 0))
    return pl.pallas_call(
        body, name=name, grid=(n, n),
        in_specs=[qspec, kspec, kspec, lspec, pl.BlockSpec((8, B), lambda ki, qi: (0, ki)), lspec, lspec, qspec],
        out_specs=[kspec, kspec, pl.BlockSpec((LANE, B), lambda ki, qi: (0, ki))],
        out_shape=[jax.ShapeDtypeStruct((T, CH), F32), jax.ShapeDtypeStruct((T, CH), F32),
                   jax.ShapeDtypeStruct((LANE, T), F32)],
        scratch_shapes=[pltpu.VMEM((B, CH), F32), pltpu.VMEM((B, CH), F32), pltpu.VMEM((8, B), F32)],
        compiler_params=_cp("parallel", "arbitrary"))(q, k, v, f, ft, lse, delta, do)


def _fox_post(zf, dqn, dkn, dv, dfq, dft, gq, gk, bf, name):
    T = zf.shape[0]
    tm = _tile(T, FOX_TM)
    nt = T // tm
    upper = jnp.asarray(_tri(tm, False), _MXU)

    def body(z_ref, dq_ref, dk_ref, dv_ref, dfq_ref, dft_ref, gq_ref, gk_ref, bf_ref, u_ref, dz_ref, sm_ref, carry, rc_s):
        @pl.when(pl.program_id(0) == 0)
        def _():
            carry[...] = jnp.zeros_like(carry)
            sm_ref[...] = jnp.zeros_like(sm_ref)

        z = z_ref[...]
        q_raw, k_raw = z[:, :CH], z[:, CH:2 * CH]
        _, q_r = _head_rms(q_raw, gq_ref[...], 4)
        _, k_r = _head_rms(k_raw, gk_ref[...], 4)
        dq, dgq = _head_rms_bwd(dq_ref[...], q_raw, q_r, gq_ref[...], 4)
        dk, dgk = _head_rms_bwd(dk_ref[...], k_raw, k_r, gk_ref[...], 4)
        df = dfq_ref[...] + dft_ref[...].T
        rc_s[...] = _exact_dot(u_ref[...], df, _NN, "b") + carry[pl.ds(0, 1), :]
        carry[pl.ds(0, 1), :] = rc_s[pl.ds(0, 1), :]
        lane = lax.broadcasted_iota(jnp.int32, (1, LANE), 1)
        x = z[:, 3 * CH:] + bf_ref[...]
        dff = jnp.where(lane < 4, rc_s[...] * _sigmoid(-x), 0.0)
        dz_ref[:, :CH] = dq.astype(dz_ref.dtype)
        dz_ref[:, CH:2 * CH] = dk.astype(dz_ref.dtype)
        dz_ref[:, 2 * CH:3 * CH] = dv_ref[...].astype(dz_ref.dtype)
        dz_ref[:, 3 * CH:] = dff.astype(dz_ref.dtype)
        sm_ref[pl.ds(0, 1), :] += dgq
        sm_ref[pl.ds(1, 1), :] += dgk
        sm_ref[pl.ds(2, 1), :LANE] += jnp.sum(dff, axis=0, keepdims=True)

    rev = lambda i: (nt - 1 - i, 0)
    row = pl.BlockSpec((tm, CH), rev)
    lrow = pl.BlockSpec((tm, LANE), rev)
    vec = pl.BlockSpec((1, CH), lambda i: (0, 0))
    return pl.pallas_call(
        body, name=name, grid=(nt,),
        in_specs=[pl.BlockSpec((tm, 3 * CH + LANE), rev), row, row, row, lrow,
                  pl.BlockSpec((LANE, tm), lambda i: (0, nt - 1 - i)), vec, vec,
                  pl.BlockSpec((1, LANE), lambda i: (0, 0)), pl.BlockSpec((tm, tm), lambda i: (0, 0))],
        out_specs=[pl.BlockSpec((tm, 3 * CH + LANE), rev), pl.BlockSpec((8, CH), lambda i: (0, 0))],
        out_shape=[jax.ShapeDtypeStruct((T, 3 * CH + LANE), _MXU), jax.ShapeDtypeStruct((8, CH), F32)],
        scratch_shapes=[pltpu.VMEM((8, LANE), F32), pltpu.VMEM((tm, LANE), F32)],
        compiler_params=_cp("arbitrary"))(zf, dqn, dkn, dv, dfq, dft, gq, gk, bf, upper)


AUG_F, AUG_ONE, AUG_LSE = HEAD, HEAD + 3, HEAD + 6


def _pieces(x):
    hi = x.astype(_MXU).astype(F32)
    r1 = x - hi
    mid = r1.astype(_MXU).astype(F32)
    lo = (r1 - mid).astype(_MXU).astype(F32)
    return hi, mid, lo


def _put_pieces(base, first_lane, x, sign):
    lane = lax.broadcasted_iota(jnp.int32, (1, LANE), 1)
    for j, piece in enumerate(_pieces(x)):
        base = jnp.where(lane == first_lane + j, sign * piece, base)
    return base


def _head_select_matrix():
    p = np.zeros((4, 4 * HEAD, LANE), np.float32)
    for h in range(4):
        for d in range(HEAD):
            p[h, h * HEAD + d, d] = 1.0
    return p


def _tri_steps(n, by_key):
    if by_key:
        pairs = [(q, k) for k in range(n) for q in range(k, n)]
    else:
        pairs = [(q, k) for q in range(n) for k in range(q + 1)]
    return (jnp.asarray([p[0] for p in pairs], jnp.int32), jnp.asarray([p[1] for p in pairs], jnp.int32))


def _fox2_prep(zf, gq, gk, bf, sel, name):
    T = zf.shape[0]
    tm = _tile(T, FOX_TM)
    lower = jnp.asarray(_tri(tm, True), _MXU)

    def body(z_ref, gq_ref, gk_ref, bf_ref, l_ref, p_ref, qa_ref, ka_ref, va_ref, carry, f_s):
        @pl.when(pl.program_id(0) == 0)
        def _():
            carry[...] = jnp.zeros_like(carry)

        z = z_ref[...]
        q, _ = _head_rms(z[:, :CH], gq_ref[...], 4)
        k, _ = _head_rms(z[:, CH:2 * CH], gk_ref[...], 4)
        q = (q * (HEAD ** -0.5)).astype(_MXU)
        k = k.astype(_MXU)
        v = z[:, 2 * CH:3 * CH].astype(_MXU)
        lane = lax.broadcasted_iota(jnp.int32, (1, LANE), 1)
        lf = jnp.where(lane < 4, _log_sigmoid(z[:, 3 * CH:] + bf_ref[...]), 0.0)
        f_s[...] = _exact_dot(l_ref[...], lf, _NN, "b") + carry[pl.ds(0, 1), :]
        carry[pl.ds(0, 1), :] = f_s[pl.ds(tm - 1, 1), :]
        fv = f_s[...]
        q_ones = (lane >= AUG_ONE) & (lane < AUG_ONE + 3)
        k_ones = ((lane >= AUG_F) & (lane < AUG_F + 3)) | ((lane >= AUG_LSE) & (lane < AUG_LSE + 3))
        v_ones = (lane >= AUG_F) & (lane < AUG_F + 3)
        for h in range(4):
            fh = _lane_col(fv, h)
            qa = jnp.where(q_ones, 1.0, _dot(q, p_ref[h]))
            qa_ref[h] = _put_pieces(qa, AUG_F, fh, 1.0).astype(qa_ref.dtype)
            ka = jnp.where(k_ones, 1.0, _dot(k, p_ref[h]))
            ka_ref[h] = _put_pieces(ka, AUG_ONE, fh, -1.0).astype(ka_ref.dtype)
            va_ref[h] = jnp.where(v_ones, 1.0, _dot(v, p_ref[h])).astype(va_ref.dtype)

    vec = pl.BlockSpec((1, CH), lambda i: (0, 0))
    hspec = pl.BlockSpec((4, tm, LANE), lambda i: (0, i, 0))
    hsh = jax.ShapeDtypeStruct((4, T, LANE), _MXU)
    return pl.pallas_call(
        body, name=name, grid=(T // tm,),
        in_specs=[pl.BlockSpec((tm, 3 * CH + LANE), lambda i: (i, 0)), vec, vec,
                  pl.BlockSpec((1, LANE), lambda i: (0, 0)), pl.BlockSpec((tm, tm), lambda i: (0, 0)),
                  pl.BlockSpec(sel.shape, lambda i: (0, 0, 0))],
        out_specs=[hspec, hspec, hspec], out_shape=[hsh, hsh, hsh],
        scratch_shapes=[pltpu.VMEM((8, LANE), F32), pltpu.VMEM((tm, LANE), F32)],
        compiler_params=_cp("arbitrary"))(zf, gq, gk, bf, lower, sel)


def _causal(s, transposed):
    row = lax.broadcasted_iota(jnp.int32, s.shape, 0)
    col = lax.broadcasted_iota(jnp.int32, s.shape, 1)
    return jnp.where((row <= col) if transposed else (col <= row), s, NEG_INF)


def _mxu_dot(a, b, dims):
    return lax.dot_general(a, b, dims, preferred_element_type=F32)


def _fox2_fwd(qa, ka, va, sel, name):
    T = qa.shape[1]
    B = _tile(T, FOX_B)
    n = T // B
    qt, kt = _tri_steps(n, False)

    def body(qt_ref, kt_ref, qa_ref, ka_ref, va_ref, p_ref, o_ref, qb_ref, m_s, acc):
        step = pl.program_id(0)
        qi, ki = qt_ref[step], kt_ref[step]

        @pl.when(ki == 0)
        def _():
            m_s[...] = jnp.full_like(m_s, NEG_INF)
            acc[...] = jnp.zeros_like(acc)

        def update(diag):
            for h in range(4):
                s = _mxu_dot(qa_ref[h], ka_ref[h], _NT)
                if diag:
                    s = _causal(s, False)
                m_old = m_s[h]
                m_new = jnp.maximum(m_old, jnp.max(s, axis=-1, keepdims=True))
                p = jnp.exp(s - m_new)
                acc[h] = acc[h] * jnp.exp(m_old - m_new) + _dot(p, va_ref[h])
                m_s[h] = m_new

        @pl.when(ki < qi)
        def _():
            update(False)

        @pl.when(ki == qi)
        def _():
            update(True)
            out = jnp.zeros((B, CH), F32)
            for h in range(4):
                a = acc[h]
                l = _lane_col(a, AUG_F)
                out = out + _exact_dot(a / l, p_ref[h], _NT, "a")
                lse = m_s[h] + jnp.log(l)
                qb_ref[h] = _put_pieces(qa_ref[h].astype(F32), AUG_LSE, lse, -1.0).astype(qb_ref.dtype)
            o_ref[...] = out

    qspec = pl.BlockSpec((4, B, LANE), lambda s, qt, kt: (0, qt[s], 0))
    kspec = pl.BlockSpec((4, B, LANE), lambda s, qt, kt: (0, kt[s], 0))
    grid_spec = pltpu.PrefetchScalarGridSpec(
        num_scalar_prefetch=2, grid=(qt.shape[0],),
        in_specs=[qspec, kspec, kspec, pl.BlockSpec(sel.shape, lambda s, qt, kt: (0, 0, 0))],
        out_specs=[pl.BlockSpec((B, CH), lambda s, qt, kt: (qt[s], 0)), qspec],
        scratch_shapes=[pltpu.VMEM((4, B, 1), F32), pltpu.VMEM((4, B, LANE), F32)])
    return pl.pallas_call(
        body, name=name, grid_spec=grid_spec,
        out_shape=[jax.ShapeDtypeStruct((T, CH), F32), jax.ShapeDtypeStruct((4, T, LANE), _MXU)],
        compiler_params=_cp("arbitrary"))(qt, kt, qa, ka, va, sel)


def _fox2_bwd_prep(o, do, sel, name):
    T = o.shape[0]
    tm = _tile(T, ROW_TILE)

    def body(o_ref, d_ref, p_ref, out_ref):
        dov = d_ref[...]
        prod = o_ref[...] * dov
        dob = dov.astype(_MXU)
        for h in range(4):
            delta = jnp.sum(jnp.where(_lane_mask(CH, h), prod, 0.0), axis=-1, keepdims=True)
            out_ref[h] = _put_pieces(_dot(dob, p_ref[h]), AUG_F, delta, -1.0).astype(out_ref.dtype)

    row = pl.BlockSpec((tm, CH), lambda i: (i, 0))
    return pl.pallas_call(
        body, name=name, grid=(T // tm,),
        in_specs=[row, row, pl.BlockSpec(sel.shape, lambda i: (0, 0, 0))],
        out_specs=pl.BlockSpec((4, tm, LANE), lambda i: (0, i, 0)),
        out_shape=jax.ShapeDtypeStruct((4, T, LANE), _MXU), compiler_params=_cp("parallel"))(o, do, sel)


def _fox2_bwd_dq(qb, ka, va, doa, sel, name):
    T = qb.shape[1]
    B = _tile(T, FOX_B)
    n = T // B
    qt, kt = _tri_steps(n, False)

    def body(qt_ref, kt_ref, qb_ref, ka_ref, va_ref, do_ref, p_ref, dq_ref, dfq_ref, dq_s):
        step = pl.program_id(0)
        qi, ki = qt_ref[step], kt_ref[step]

        @pl.when(ki == 0)
        def _():
            dq_s[...] = jnp.zeros_like(dq_s)

        def update(diag):
            for h in range(4):
                s = _mxu_dot(qb_ref[h], ka_ref[h], _NT)
                if diag:
                    s = _causal(s, False)
                ds = jnp.exp(s) * _mxu_dot(do_ref[h], va_ref[h], _NT)
                dq_s[h] += _dot(ds, ka_ref[h])

        @pl.when(ki < qi)
        def _():
            update(False)

        @pl.when(ki == qi)
        def _():
            update(True)
            lane = lax.broadcasted_iota(jnp.int32, (1, LANE), 1)
            out = jnp.zeros((B, CH), F32)
            dfq = jnp.zeros((B, LANE), F32)
            for h in range(4):
                out = out + _exact_dot(dq_s[h] * (HEAD ** -0.5), p_ref[h], _NT, "a")
                dfq = jnp.where(lane == h, _lane_col(dq_s[h], AUG_F), dfq)
            dq_ref[...] = out
            dfq_ref[...] = dfq

    qspec = pl.BlockSpec((4, B, LANE), lambda s, qt, kt: (0, qt[s], 0))
    kspec = pl.BlockSpec((4, B, LANE), lambda s, qt, kt: (0, kt[s], 0))
    grid_spec = pltpu.PrefetchScalarGridSpec(
        num_scalar_prefetch=2, grid=(qt.shape[0],),
        in_specs=[qspec, kspec, kspec, qspec, pl.BlockSpec(sel.shape, lambda s, qt, kt: (0, 0, 0))],
        out_specs=[pl.BlockSpec((B, CH), lambda s, qt, kt: (qt[s], 0)),
                   pl.BlockSpec((B, LANE), lambda s, qt, kt: (qt[s], 0))],
        scratch_shapes=[pltpu.VMEM((4, B, LANE), F32)])
    return pl.pallas_call(
        body, name=name, grid_spec=grid_spec,
        out_shape=[jax.ShapeDtypeStruct((T, CH), F32), jax.ShapeDtypeStruct((T, LANE), F32)],
        compiler_params=_cp("arbitrary"))(qt, kt, qb, ka, va, doa, sel)


def _fox2_bwd_dkv(qb, ka, va, doa, sel, name):
    T = qb.shape[1]
    B = _tile(T, FOX_B)
    n = T // B
    qt, kt = _tri_steps(n, True)

    def body(qt_ref, kt_ref, qb_ref, ka_ref, va_ref, do_ref, p_ref, dk_ref, dv_ref, df_ref, dk_s, dv_s):
        step = pl.program_id(0)
        qi, ki = qt_ref[step], kt_ref[step]

        @pl.when(qi == ki)
        def _():
            dk_s[...] = jnp.zeros_like(dk_s)
            dv_s[...] = jnp.zeros_like(dv_s)

        def update(diag):
            for h in range(4):
                st = _mxu_dot(ka_ref[h], qb_ref[h], _NT)
                if diag:
                    st = _causal(st, True)
                pt = jnp.exp(st)
                dst = pt * _mxu_dot(va_ref[h], do_ref[h], _NT)
                dv_s[h] += _dot(pt, do_ref[h])
                dk_s[h] += _dot(dst, qb_ref[h])

        @pl.when(qi == ki)
        def _():
            update(True)

        @pl.when(qi > ki)
        def _():
            update(False)

        @pl.when(qi == n - 1)
        def _():
            lane = lax.broadcasted_iota(jnp.int32, (1, LANE), 1)
            dk = jnp.zeros((B, CH), F32)
            dv = jnp.zeros((B, CH), F32)
            dfk = jnp.zeros((B, LANE), F32)
            for h in range(4):
                dk = dk + _exact_dot(dk_s[h], p_ref[h], _NT, "a")
                dv = dv + _exact_dot(dv_s[h], p_ref[h], _NT, "a")
                dfk = jnp.where(lane == h, -_lane_col(dk_s[h], AUG_ONE), dfk)
            dk_ref[...] = dk
            dv_ref[...] = dv
            df_ref[...] = dfk

    qspec = pl.BlockSpec((4, B, LANE), lambda s, qt, kt: (0, qt[s], 0))
    kspec = pl.BlockSpec((4, B, LANE), lambda s, qt, kt: (0, kt[s], 0))
    ospec = pl.BlockSpec((B, CH), lambda s, qt, kt: (kt[s], 0))
    grid_spec = pltpu.PrefetchScalarGridSpec(
        num_scalar_prefetch=2, grid=(qt.shape[0],),
        in_specs=[qspec, kspec, kspec, qspec, pl.BlockSpec(sel.shape, lambda s, qt, kt: (0, 0, 0))],
        out_specs=[ospec, ospec, pl.BlockSpec((B, LANE), lambda s, qt, kt: (kt[s], 0))],
        scratch_shapes=[pltpu.VMEM((4, B, LANE), F32), pltpu.VMEM((4, B, LANE), F32)])
    return pl.pallas_call(
        body, name=name, grid_spec=grid_spec,
        out_shape=[jax.ShapeDtypeStruct((T, CH), F32), jax.ShapeDtypeStruct((T, CH), F32),
                   jax.ShapeDtypeStruct((T, LANE), F32)],
        compiler_params=_cp("arbitrary"))(qt, kt, qb, ka, va, doa, sel)


def _fox2_post(zf, dqn, dkn, dv, dfq, dfk, gq, gk, bf, name):
    T = zf.shape[0]
    tm = _tile(T, FOX_TM)
    nt = T // tm
    upper = jnp.asarray(_tri(tm, False), _MXU)

    def body(z_ref, dq_ref, dk_ref, dv_ref, dfq_ref, df_ref, gq_ref, gk_ref, bf_ref, u_ref, dz_ref, sm_ref, carry, rc_s):
        @pl.when(pl.program_id(0) == 0)
        def _():
            carry[...] = jnp.zeros_like(carry)
            sm_ref[...] = jnp.zeros_like(sm_ref)

        z = z_ref[...]
        q_raw, k_raw = z[:, :CH], z[:, CH:2 * CH]
        _, q_r = _head_rms(q_raw, gq_ref[...], 4)
        _, k_r = _head_rms(k_raw, gk_ref[...], 4)
        dq, dgq = _head_rms_bwd(dq_ref[...], q_raw, q_r, gq_ref[...], 4)
        dk, dgk = _head_rms_bwd(dk_ref[...], k_raw, k_r, gk_ref[...], 4)
        rc_s[...] = _exact_dot(u_ref[...], dfq_ref[...] + df_ref[...], _NN, "b") + carry[pl.ds(0, 1), :]
        carry[pl.ds(0, 1), :] = rc_s[pl.ds(0, 1), :]
        lane = lax.broadcasted_iota(jnp.int32, (1, LANE), 1)
        x = z[:, 3 * CH:] + bf_ref[...]
        dff = jnp.where(lane < 4, rc_s[...] * _sigmoid(-x), 0.0)
        dz_ref[:, :CH] = dq.astype(dz_ref.dtype)
        dz_ref[:, CH:2 * CH] = dk.astype(dz_ref.dtype)
        dz_ref[:, 2 * CH:3 * CH] = dv_ref[...].astype(dz_ref.dtype)
        dz_ref[:, 3 * CH:] = dff.astype(dz_ref.dtype)
        sm_ref[pl.ds(0, 1), :] += dgq
        sm_ref[pl.ds(1, 1), :] += dgk
        sm_ref[pl.ds(2, 1), :LANE] += jnp.sum(dff, axis=0, keepdims=True)

    rev = lambda i: (nt - 1 - i, 0)
    row = pl.BlockSpec((tm, CH), rev)
    lrow = pl.BlockSpec((tm, LANE), rev)
    vec = pl.BlockSpec((1, CH), lambda i: (0, 0))
    return pl.pallas_call(
        body, name=name, grid=(nt,),
        in_specs=[pl.BlockSpec((tm, 3 * CH + LANE), rev), row, row, row, lrow, lrow, vec, vec,
                  pl.BlockSpec((1, LANE), lambda i: (0, 0)), pl.BlockSpec((tm, tm), lambda i: (0, 0))],
        out_specs=[pl.BlockSpec((tm, 3 * CH + LANE), rev), pl.BlockSpec((8, CH), lambda i: (0, 0))],
        out_shape=[jax.ShapeDtypeStruct((T, 3 * CH + LANE), _MXU), jax.ShapeDtypeStruct((8, CH), F32)],
        scratch_shapes=[pltpu.VMEM((8, LANE), F32), pltpu.VMEM((tm, LANE), F32)],
        compiler_params=_cp("arbitrary"))(zf, dqn, dkn, dv, dfq, dfk, gq, gk, bf, upper)


def _merge_fwd(acts, zg, wbr, wout, x1, name):
    T, D = x1.shape
    tm = _tile(T, 256)

    def body(a0, a1, a2, a3, zg_ref, wbr_ref, wout_ref, x_ref, o_ref, mg_ref):
        merged = None
        for i, a_ref in enumerate((a0, a1, a2, a3)):
            term = _sigmoid(zg_ref[:, i * D:(i + 1) * D]) * _dot(a_ref[...], wbr_ref[i])
            merged = term if merged is None else merged + term
        mg_ref[...] = merged.astype(mg_ref.dtype)
        o_ref[...] = x_ref[...] + _dot(merged, wout_ref[...])

    arow = pl.BlockSpec((tm, CH), lambda i: (i, 0))
    xrow = pl.BlockSpec((tm, D), lambda i: (i, 0))
    return pl.pallas_call(
        body, name=name, grid=(T // tm,),
        in_specs=[arow, arow, arow, arow, pl.BlockSpec((tm, 4 * D), lambda i: (i, 0)),
                  pl.BlockSpec((4, CH, D), lambda i: (0, 0, 0)), pl.BlockSpec((D, D), lambda i: (0, 0)), xrow],
        out_specs=[xrow, xrow],
        out_shape=[jax.ShapeDtypeStruct((T, D), F32), jax.ShapeDtypeStruct((T, D), _MXU)],
        compiler_params=_cp("parallel"))(*acts, zg, wbr, wout, x1)


def _merge_bwd(dx2, acts, zg, wbr, wout, name):
    T, D = dx2.shape
    tm = _tile(T, 256)
    nt = T // tm

    def body(dx_ref, a0, a1, a2, a3, zg_ref, wbr_ref, wout_ref, d0, d1, d2, d3, dzg_ref, dw_ref, dw_s):
        i = pl.program_id(0)

        @pl.when(i == 0)
        def _():
            dw_s[...] = jnp.zeros_like(dw_s)

        dm = _dot(dx_ref[...], wout_ref[...], _NT)
        for b, (a_ref, d_ref) in enumerate(((a0, d0), (a1, d1), (a2, d2), (a3, d3))):
            av = a_ref[...].astype(_MXU)
            g = _sigmoid(zg_ref[:, b * D:(b + 1) * D])
            p = _dot(av, wbr_ref[b])
            dzg_ref[:, b * D:(b + 1) * D] = (dm * p * (g * (1.0 - g))).astype(dzg_ref.dtype)
            dp = (dm * g).astype(_MXU)
            d_ref[...] = _dot(dp, wbr_ref[b], _NT)
            dw_s[b] += _dot(av, dp, _TN)

        @pl.when(i == nt - 1)
        def _():
            dw_ref[...] = dw_s[...].astype(dw_ref.dtype)

    arow = pl.BlockSpec((tm, CH), lambda i: (i, 0))
    xrow = pl.BlockSpec((tm, D), lambda i: (i, 0))
    grow = pl.BlockSpec((tm, 4 * D), lambda i: (i, 0))
    wspec = pl.BlockSpec((4, CH, D), lambda i: (0, 0, 0))
    ash = jax.ShapeDtypeStruct((T, CH), F32)
    return pl.pallas_call(
        body, name=name, grid=(nt,),
        in_specs=[xrow, arow, arow, arow, arow, grow, wspec, pl.BlockSpec((D, D), lambda i: (0, 0))],
        out_specs=[arow, arow, arow, arow, grow, wspec],
        out_shape=[ash, ash, ash, ash, jax.ShapeDtypeStruct((T, 4 * D), _MXU), jax.ShapeDtypeStruct((4, CH, D), _MXU)],
        scratch_shapes=[pltpu.VMEM((4, CH, D), F32)],
        compiler_params=_cp("arbitrary"))(dx2, *acts, zg, wbr, wout)


def _rows_2d(a):
    return a.reshape((-1, a.shape[-1])) if a.ndim > 1 else a.reshape((1, -1))


def _row_tile(rows, cols, n_bufs):
    padded = -(-cols // LANE) * LANE
    cap = max(8, (VMEM_LIMIT // 3) // (2 * n_bufs * 4 * padded))
    return _tile(rows, cap, 8)


def _sum8(recv, name):
    shape = recv.shape[1:]
    r2 = recv.reshape((N_DEV, -1, shape[-1]))
    rows, cols = r2.shape[1:]
    tr = _row_tile(rows, cols, N_DEV // 2 + 1)

    def body(r_ref, o_ref):
        acc = r_ref[0].astype(F32)
        for d in range(1, N_DEV):
            acc = acc + r_ref[d].astype(F32)
        o_ref[...] = acc

    out = pl.pallas_call(
        body, name=name, grid=(rows // tr,),
        in_specs=[pl.BlockSpec((N_DEV, tr, cols), lambda i: (0, i, 0))],
        out_specs=pl.BlockSpec((tr, cols), lambda i: (i, 0)),
        out_shape=jax.ShapeDtypeStruct((rows, cols), F32), compiler_params=_cp("parallel"))(r2)
    return out.reshape(shape)


def _adamw(w, g, m, v, name):
    shape = w.shape
    w2, g2, m2, v2 = (_rows_2d(a) for a in (w, g, m, v))
    rows, cols = w2.shape
    tr = _row_tile(rows, cols, 7)

    def body(w_ref, g_ref, m_ref, v_ref, d_ref, nm_ref, nv_ref):
        gv = g_ref[...]
        nm = ADAM_B1 * m_ref[...] + (1.0 - ADAM_B1) * gv
        nv = ADAM_B2 * v_ref[...] + (1.0 - ADAM_B2) * jnp.square(gv)
        m_hat = nm / (1.0 - ADAM_B1 ** ADAM_STEP)
        v_hat = nv / (1.0 - ADAM_B2 ** ADAM_STEP)
        d_ref[...] = -ADAM_LR * (m_hat / (jnp.sqrt(v_hat) + ADAM_EPS) + ADAM_WD * w_ref[...])
        nm_ref[...] = nm
        nv_ref[...] = nv

    spec = pl.BlockSpec((tr, cols), lambda i: (i, 0))
    osh = jax.ShapeDtypeStruct((rows, cols), F32)
    outs = pl.pallas_call(
        body, name=name, grid=(rows // tr,), in_specs=[spec] * 4, out_specs=[spec] * 3,
        out_shape=[osh] * 3, compiler_params=_cp("parallel"))(w2, g2, m2, v2)
    return tuple(o.reshape(shape) for o in outs)


def _exchange(items, name):
    n = len(items)
    widths, out_shapes = [], []
    for src, kind, ax in items:
        if kind == "gather":
            w = src.shape[ax]
            shp = list(src.shape)
            shp[ax] = N_DEV * w
        else:
            w = src.shape[ax] // N_DEV
            shp = list(src.shape)
            shp[ax] = w
            shp = [N_DEV] + shp
        widths.append(w)
        out_shapes.append(jax.ShapeDtypeStruct(tuple(shp), src.dtype))

    def body(*refs):
        srcs, outs = refs[:n], refs[n:2 * n]
        send, recv, lsem = refs[2 * n:]
        x, y, c = lax.axis_index("x"), lax.axis_index("y"), lax.axis_index("c")
        me = 4 * x + 2 * y + c

        def peer(k):
            b = k + 1
            px = 1 - x if b & 4 else x
            py = 1 - y if b & 2 else y
            pc = 1 - c if b & 1 else c
            return (px, py, pc), 4 * px + 2 * py + pc

        def win(ref, ax, idx, w):
            return ref.at[tuple([slice(None)] * ax + [pl.ds(idx * w, w)])]

        def ends(j, mine, theirs):
            _, kind, ax = items[j]
            if kind == "gather":
                return srcs[j], win(outs[j], ax, mine, widths[j])
            return win(srcs[j], ax, theirs, widths[j]), outs[j].at[mine]

        local, sent = [], []
        for j in range(n):
            s, d = ends(j, me, me)
            cp = pltpu.make_async_copy(s, d, lsem.at[j])
            cp.start()
            local.append(cp)
            for k in range(N_DEV - 1):
                dev, pid = peer(k)
                s, d = ends(j, me, pid)
                cp = pltpu.make_async_remote_copy(s, d, send.at[j, k], recv.at[j, k], device_id=dev,
                                                  device_id_type=pl.DeviceIdType.MESH)
                cp.start()
                sent.append(cp)
        for j in range(n):
            for k in range(N_DEV - 1):
                dev, pid = peer(k)
                s, d = ends(j, pid, me)
                pltpu.make_async_remote_copy(s, d, send.at[j, k], recv.at[j, k], device_id=dev,
                                             device_id_type=pl.DeviceIdType.MESH).wait_recv()
        for cp in sent:
            cp.wait_send()
        for cp in local:
            cp.wait()

    hbm = pl.BlockSpec(memory_space=pl.ANY)
    return pl.pallas_call(
        body, name=name, in_specs=[hbm] * n, out_specs=[hbm] * n, out_shape=out_shapes,
        scratch_shapes=[pltpu.SemaphoreType.DMA((n, N_DEV - 1)), pltpu.SemaphoreType.DMA((n, N_DEV - 1)),
                        pltpu.SemaphoreType.DMA((n,))],
        compiler_params=pltpu.CompilerParams(has_side_effects=True))(*[it[0] for it in items])


def _exchange_plan(items):
    widths, out_shapes = [], []
    for src, kind, ax in items:
        shp = list(src.shape)
        if kind == "gather":
            w = src.shape[ax]
            shp[ax] = N_DEV * w
        else:
            w = src.shape[ax] // N_DEV
            shp[ax] = w
            shp = [N_DEV] + shp
        widths.append(w)
        out_shapes.append((tuple(shp), src.dtype))
    return widths, out_shapes


def _exchange_refs(items, widths, srcs, outs):
    x, y, c = lax.axis_index("x"), lax.axis_index("y"), lax.axis_index("c")
    me = 4 * x + 2 * y + c

    def peer(k):
        b = k + 1
        px = 1 - x if b & 4 else x
        py = 1 - y if b & 2 else y
        pc = 1 - c if b & 1 else c
        return (px, py, pc), 4 * px + 2 * py + pc

    def win(ref, ax, idx, w):
        return ref.at[tuple([slice(None)] * ax + [pl.ds(idx * w, w)])]

    def ends(j, mine, theirs):
        _, kind, ax = items[j]
        if kind == "gather":
            return srcs[j], win(outs[j], ax, mine, widths[j])
        return win(srcs[j], ax, theirs, widths[j]), outs[j].at[mine]

    return me, peer, ends


_HBM = pl.BlockSpec(memory_space=pltpu.HBM)
_SEM = pl.BlockSpec(memory_space=pltpu.SEMAPHORE)


def _exchange_start(items, name):
    n = len(items)
    widths, out_shapes = _exchange_plan(items)
    meta = [(None, kind, ax) for _, kind, ax in items]

    def body(*refs):
        srcs, lands = refs[:n], refs[n:2 * n]
        send, recv = refs[2 * n], refs[2 * n + 1]
        token, lsem = refs[-2], refs[-1]
        me, peer, ends = _exchange_refs(meta, widths, srcs, lands)
        local = []
        for j in range(n):
            s, d = ends(j, me, me)
            cp = pltpu.make_async_copy(s, d, lsem.at[j])
            cp.start()
            local.append(cp)
            for k in range(N_DEV - 1):
                dev, pid = peer(k)
                s, d = ends(j, me, pid)
                q = j * (N_DEV - 1) + k
                pltpu.make_async_remote_copy(s, d, send.at[q], recv.at[q], device_id=dev,
                                             device_id_type=pl.DeviceIdType.MESH).start()
        for cp in local:
            cp.wait()
        token[...] = jnp.zeros_like(token)

    srcs = [pltpu.with_memory_space_constraint(it[0], pltpu.HBM) for it in items]
    lands = [pltpu.with_memory_space_constraint(lax.empty(shp, dt), pltpu.HBM) for shp, dt in out_shapes]
    outs = pl.pallas_call(
        body, name=name,
        out_shape=(pltpu.SemaphoreType.DMA((n * (N_DEV - 1),)), pltpu.SemaphoreType.DMA((n * (N_DEV - 1),)),
                   *[pltpu.HBM(s.shape, s.dtype) for s in srcs], *[pltpu.HBM(shp, dt) for shp, dt in out_shapes],
                   jax.ShapeDtypeStruct((8, LANE), F32)),
        in_specs=[_HBM] * (2 * n),
        out_specs=(_SEM, _SEM, *([_HBM] * (2 * n)), pl.BlockSpec(memory_space=pltpu.VMEM)),
        input_output_aliases={i: 2 + i for i in range(2 * n)},
        scratch_shapes=[pltpu.SemaphoreType.DMA((n,))],
        compiler_params=pltpu.CompilerParams(has_side_effects=pltpu.SideEffectType.DATAFLOW_SIDE_EFFECTING),
    )(*srcs, *lands)
    handle = (meta, widths, outs[0], outs[1], outs[2:2 + n], outs[2 + n:2 + 2 * n])
    return handle, outs[-1]


def _exchange_wait(handle, after, name):
    meta, widths, send_sem, recv_sem, src_thru, land_thru = handle
    n = len(meta)

    def body(*refs):
        srcs, lands = refs[:n], refs[n:2 * n]
        send, recv = refs[2 * n], refs[2 * n + 1]
        me, peer, ends = _exchange_refs(meta, widths, srcs, lands)
        for j in range(n):
            for k in range(N_DEV - 1):
                dev, pid = peer(k)
                q = j * (N_DEV - 1) + k
                s, d = ends(j, me, pid)
                pltpu.make_async_remote_copy(s, d, send.at[q], recv.at[q], device_id=dev,
                                             device_id_type=pl.DeviceIdType.MESH).wait_send()
                s, d = ends(j, pid, me)
                pltpu.make_async_remote_copy(s, d, send.at[q], recv.at[q], device_id=dev,
                                             device_id_type=pl.DeviceIdType.MESH).wait_recv()

    outs = pl.pallas_call(
        body, name=name,
        out_shape=tuple(pltpu.HBM(a.shape, a.dtype) for a in (*src_thru, *land_thru)),
        in_specs=[_HBM] * (2 * n) + [_SEM, _SEM, pl.BlockSpec(memory_space=pl.ANY)],
        out_specs=tuple([_HBM] * (2 * n)),
        input_output_aliases={i: i for i in range(2 * n)},
        compiler_params=pltpu.CompilerParams(has_side_effects=pltpu.SideEffectType.DATAFLOW_SIDE_EFFECTING),
    )(*src_thru, *land_thru, send_sem, recv_sem, after)
    return list(outs[n:])


def _pack(arrs):
    flat = jnp.concatenate([a.reshape(-1).astype(F32) for a in arrs])
    n = flat.shape[0]
    rows = -(-n // (8 * LANE)) * 8
    return jnp.pad(flat, (0, rows * LANE - n)).reshape(rows, LANE)


def _unpack(buf, shapes):
    flat = buf.reshape(-1)
    out, off = [], 0
    for s in shapes:
        sz = int(np.prod(s))
        out.append(flat[off:off + sz].reshape(s))
        off += sz
    return out


def _pad_axis(a, axis, size):
    pad = [(0, 0)] * a.ndim
    pad[axis] = (0, size - a.shape[axis])
    return jnp.pad(a, pad)


def _ffn_forward(x, g, wg, wu, wd, tag):
    a = _rms_fwd(x, g, f"{tag}_rms")
    gate, up, hid = _ffn_up(a, wg, wu, f"{tag}_up")
    out = _mm([(hid, wd)], "nn", F32, f"{tag}_down", scale=0.5, res=x)
    return out, (x, a, gate, up, hid)


def _ffn_backward(dxp, saved, g, wg, wu, wd, tag):
    x, a, gate, up, hid = saved
    d_gate, d_up = _ffn_bwd_hid(dxp, wd, gate, up, f"{tag}_bwd_hid")
    d_wd = _mm([(hid, dxp)], "tn", _MXU, f"{tag}_dwd", scale=0.5, tk=2048)
    d_wg = _mm([(a, d_gate)], "tn", _MXU, f"{tag}_dwg", tk=2048)
    d_wu = _mm([(a, d_up)], "tn", _MXU, f"{tag}_dwu", tk=2048)
    d_a = _mm([(d_gate, wg), (d_up, wu)], "nt", F32, f"{tag}_da")
    dx, dg = _rms_bwd(d_a, x, g, dxp, f"{tag}_rms_bwd")
    return dx, dg, d_wg, d_wu, d_wd


def _tile_vec(v, reps):
    return jnp.tile(v.reshape(1, -1), (1, reps))


def _mixer_forward(x1, p, consts, tag):
    h = _rms_fwd(x1, p["mix_norm"], f"{tag}_rms")
    zg = _mm([(h, p["w_zg"])], "nn", F32, f"{tag}_in_g")
    zc = _mm([(h, p["w_conf"])], "nn", F32, f"{tag}_in_c")
    zs = _mm([(h, p["w_sc"])], "nn", F32, f"{tag}_in_s")
    zw = _mm([(h, p["w_swa"])], "nn", F32, f"{tag}_in_w")
    zf = _mm([(h, p["w_fox"])], "nn", F32, f"{tag}_in_f")
    u1, act_c = _conf_fwd(zc, p["conf_dw"], p["conf_dw_b"], p["conf_ln_g"], p["conf_ln_b"], f"{tag}_conf")
    act_s = _sc_fwd(zs, p["sc_conv"], f"{tag}_sc")
    act_w = _swa_fwd(zw, p["swa_q_norm"], p["swa_k_norm"], p["swa_sink"], consts["bias"], consts["expand"], f"{tag}_swa")
    qa, ka, va = _fox2_prep(zf, p["fox_q_norm"], p["fox_k_norm"], p["b_forget"], consts["sel"], f"{tag}_fox_prep")
    act_f, qb = _fox2_fwd(qa, ka, va, consts["sel"], f"{tag}_fox")
    acts = (act_c, act_s, act_w, act_f)
    x2, merged = _merge_fwd(acts, zg, p["w_br"], p["w_out"], x1, f"{tag}_merge")
    saved = (x1, h, zg, zc, zs, zw, zf, u1, acts, qb, ka, va, merged)
    return x2, saved


def _mixer_backward(dx2, saved, p, consts, tag):
    x1, h, zg, zc, zs, zw, zf, u1, acts, qb, ka, va, merged = saved
    g = {}
    g["w_out"] = _mm([(merged, dx2)], "tn", _MXU, f"{tag}_dwout", tk=2048)
    d_c, d_s, d_w, d_f, dzg, g["w_br"] = _merge_bwd(dx2, acts, zg, p["w_br"], p["w_out"], f"{tag}_merge_bwd")
    du1, sm_c = _conf_bwd_ln(d_c, u1, p["conf_ln_g"], p["conf_ln_b"], f"{tag}_conf_bwd_ln")
    dzc, g["conf_dw"] = _conf_bwd_conv(zc, du1, p["conf_dw"], f"{tag}_conf_bwd_conv")
    g["conf_ln_g"], g["conf_ln_b"], g["conf_dw_b"] = sm_c[0], sm_c[1], sm_c[2]
    dzs, g["sc_conv"] = _sc_bwd(zs, d_s, p["sc_conv"], f"{tag}_sc_bwd")
    dzw, dgq, dgk, g["swa_sink"], g["rel_bias"] = _swa_bwd(
        zw, d_w, p["swa_q_norm"], p["swa_k_norm"], p["swa_sink"], consts["bias"], consts["bucket"], consts["expand"],
        f"{tag}_swa_bwd")
    g["swa_q_norm"], g["swa_k_norm"] = dgq, dgk
    doa = _fox2_bwd_prep(acts[3], d_f, consts["sel"], f"{tag}_fox_bwd_prep")
    dqn, dfq = _fox2_bwd_dq(qb, ka, va, doa, consts["sel"], f"{tag}_fox_bwd_dq")
    dkn, dv, dfk = _fox2_bwd_dkv(qb, ka, va, doa, consts["sel"], f"{tag}_fox_bwd_dkv")
    dzf, sm_f = _fox2_post(zf, dqn, dkn, dv, dfq, dfk, p["fox_q_norm"], p["fox_k_norm"], p["b_forget"], f"{tag}_fox_post")
    g["fox_q_norm"], g["fox_k_norm"], g["b_forget"] = sm_f[0], sm_f[1], sm_f[2]
    parts = ((dzg, "w_zg"), (dzc, "w_conf"), (dzs, "w_sc"), (dzw, "w_swa"), (dzf, "w_fox"))
    for dz, wname in parts:
        g[wname] = _mm([(h, dz)], "tn", _MXU, f"{tag}_d{wname}", tk=2048)
    dh = _mm([(dz, p[wname]) for dz, wname in parts], "nt", F32, f"{tag}_dh", tm=512)
    dx1, g["mix_norm"] = _rms_bwd(dh, x1, p["mix_norm"], dx2, f"{tag}_rms_bwd")
    return dx1, g


W_NAMES = ['rel_bias', 'ffn1_norm', 'ffn1_w_gate', 'ffn1_w_up', 'ffn1_w_down', 'mix_norm', 'w_in', 'b_forget', 'conf_dw',
           'conf_dw_b', 'conf_ln_g', 'conf_ln_b', 'conf_w_out', 'sc_conv', 'sc_w_out', 'swa_q_norm', 'swa_k_norm',
           'swa_sink', 'swa_w_o', 'fox_q_norm', 'fox_k_norm', 'fox_w_o', 'w_out', 'ffn2_norm', 'ffn2_w_gate',
           'ffn2_w_up', 'ffn2_w_down']
SMALL = ['rel_bias', 'ffn1_norm', 'mix_norm', 'b_forget', 'conf_dw', 'conf_dw_b', 'conf_ln_g', 'conf_ln_b', 'sc_conv',
         'swa_q_norm', 'swa_k_norm', 'swa_sink', 'fox_q_norm', 'fox_k_norm', 'ffn2_norm']
BRANCH_W = ['conf_w_out', 'sc_w_out', 'swa_w_o', 'fox_w_o']
IN_CONF, IN_SC, IN_SWA, IN_FOX, IN_FF = (0, 512), (512, 1280), (1280, 1792), (1792, 2560), (2560, 2564)


def _step(w, m, v, x, loss_target):
    T, D = x.shape
    L = w["w_out"].shape[0]
    fs = w["ffn1_w_gate"].shape[2]
    fsp = -(-fs // LANE) * LANE
    dev = 4 * lax.axis_index("x") + 2 * lax.axis_index("y") + lax.axis_index("c")

    def cast(a):
        return a.astype(_MXU)

    win = w["w_in"]
    fox_cols = jnp.concatenate([win[..., IN_FOX[0]:IN_FF[1]],
                                jnp.zeros(win.shape[:2] + (LANE - (IN_FF[1] - IN_FF[0]),), win.dtype)], axis=-1)
    shards = {
        "ffn1_w_gate": (cast(_pad_axis(w["ffn1_w_gate"], 2, fsp)), 2),
        "ffn1_w_up": (cast(_pad_axis(w["ffn1_w_up"], 2, fsp)), 2),
        "ffn1_w_down": (cast(_pad_axis(w["ffn1_w_down"], 1, fsp)), 1),
        "ffn2_w_gate": (cast(_pad_axis(w["ffn2_w_gate"], 2, fsp)), 2),
        "ffn2_w_up": (cast(_pad_axis(w["ffn2_w_up"], 2, fsp)), 2),
        "ffn2_w_down": (cast(_pad_axis(w["ffn2_w_down"], 1, fsp)), 1),
        "w_zg": (cast(win[..., IN_FF[1]:]), 1),
        "w_conf": (cast(win[..., IN_CONF[0]:IN_CONF[1]]), 1),
        "w_sc": (cast(win[..., IN_SC[0]:IN_SC[1]]), 1),
        "w_swa": (cast(win[..., IN_SWA[0]:IN_SWA[1]]), 1),
        "w_fox": (cast(fox_cols), 1),
        "w_out": (cast(w["w_out"]), 1),
        "w_br": (cast(jnp.stack([w[n] for n in BRANCH_W], axis=1)), 3),
    }
    big = list(shards)
    conv_shard = jnp.concatenate([jnp.swapaxes(w["conf_dw"], 1, 2), jnp.swapaxes(w["sc_conv"], 1, 2)], axis=2)
    conv_full = jnp.swapaxes(_exchange([(conv_shard, "gather", 1)], "gather_conv")[0], 1, 2)
    conf_dw_full = _pad_axis(conv_full[:, :CONV_K], 1, CONV_HALO)
    sc_conv_full = _pad_axis(conv_full[:, CONV_K:], 1, SC_HALO)

    first = ["ffn1_w_gate", "ffn1_w_up", "ffn1_w_down"]
    rest = [n for n in big if n not in first]
    groups = [(l, names) for l in range(L) for names in (first, rest)]
    pending, tokens = [], []
    for l, names in groups:
        h, tok = _exchange_start([(shards[n][0][l], "gather", shards[n][1] - 1) for n in names],
                                 f"gather_start_l{l}_{names[0]}")
        pending.append(h)
        tokens.append(tok)
    full = [dict() for _ in range(L)]

    def arrive(gi, after):
        l, names = groups[gi]
        full[l].update(zip(names, _exchange_wait(pending[gi], after, f"gather_wait_l{l}_{names[0]}")))

    bucket = jnp.asarray(_swa_bucket_matrix(min(SWA_TQ, T)))
    consts = {"bucket": bucket, "expand": jnp.asarray(_kv_expand_matrix(), _MXU),
              "sel": jnp.asarray(_head_select_matrix(), _MXU),
              "bias": _swa_bias(w["rel_bias"], bucket, "swa_bias")}

    def layer_params(l):
        p = {}
        for n in ("ffn1_norm", "mix_norm", "ffn2_norm", "conf_dw_b", "conf_ln_g", "conf_ln_b"):
            p[n] = w[n][l].reshape(1, -1)
        p["conf_dw"], p["sc_conv"] = conf_dw_full[l], sc_conv_full[l]
        p["swa_q_norm"], p["fox_q_norm"] = _tile_vec(w["swa_q_norm"][l], 4), _tile_vec(w["fox_q_norm"][l], 4)
        p["swa_k_norm"], p["fox_k_norm"] = _tile_vec(w["swa_k_norm"][l], 2), _tile_vec(w["fox_k_norm"][l], 4)
        p["swa_sink"] = w["swa_sink"][l].reshape(1, 4)
        p["b_forget"] = _pad_axis(w["b_forget"][l].reshape(1, 4), 1, LANE)
        return p

    params = [layer_params(l) for l in range(L)]
    saved = []
    cur = x
    for l, p in enumerate(params):
        arrive(2 * l, cur if l else sum(tokens[1:]))
        p.update(full[l])
        x1, s1 = _ffn_forward(cur, p["ffn1_norm"], p["ffn1_w_gate"], p["ffn1_w_up"], p["ffn1_w_down"], f"l{l}_ffn1")
        arrive(2 * l + 1, x1)
        p.update(full[l])
        x2, s2 = _mixer_forward(x1, p, consts, f"l{l}_mix")
        cur, s3 = _ffn_forward(x2, p["ffn2_norm"], p["ffn2_w_gate"], p["ffn2_w_up"], p["ffn2_w_down"], f"l{l}_ffn2")
        saved.append((s1, s2, s3))
    dcur, loss_part = _loss_grad(cur, loss_target)

    grads = [None] * L
    leaving = []

    def leave(l, names, g):
        h, tok = _exchange_start([(g[n], "scatter", shards[n][1] - 1) for n in names],
                                 f"scatter_start_l{l}_{names[0]}")
        leaving.append((l, names, h))
        return tok[0:1, 0:1]

    carry_tok = None
    for l in reversed(range(L)):
        p = params[l]
        s1, s2, s3 = saved[l]
        g = {}
        g2 = p["ffn2_norm"] if carry_tok is None else p["ffn2_norm"] + carry_tok
        dcur, g["ffn2_norm"], g["ffn2_w_gate"], g["ffn2_w_up"], g["ffn2_w_down"] = _ffn_backward(
            dcur, s3, g2, p["ffn2_w_gate"], p["ffn2_w_up"], p["ffn2_w_down"], f"l{l}_ffn2")
        dcur, gm = _mixer_backward(dcur, s2, p, consts, f"l{l}_mix")
        g.update(gm)
        tok = leave(l, rest, g)
        dcur, g["ffn1_norm"], g["ffn1_w_gate"], g["ffn1_w_up"], g["ffn1_w_down"] = _ffn_backward(
            dcur, s1, p["ffn1_norm"] + tok, p["ffn1_w_gate"], p["ffn1_w_up"], p["ffn1_w_down"], f"l{l}_ffn1")
        carry_tok = leave(l, first, g)
        grads[l] = g
    grad_x = dcur

    gsum = {n: [None] * L for n in big}
    for l, names, h in leaving:
        for n, r in zip(names, _exchange_wait(h, grad_x, f"scatter_wait_l{l}_{names[0]}")):
            gsum[n][l] = _sum8(r, f"sum_{n}_l{l}")
    gsum = {n: jnp.stack(parts) for n, parts in gsum.items()}
    gw = {}
    for n in ("ffn1_w_gate", "ffn1_w_up", "ffn2_w_gate", "ffn2_w_up"):
        gw[n] = gsum[n][:, :, :fs]
    for n in ("ffn1_w_down", "ffn2_w_down"):
        gw[n] = gsum[n][:, :fs, :]
    gw["w_out"] = gsum["w_out"]
    for i, n in enumerate(BRANCH_W):
        gw[n] = gsum["w_br"][:, i]
    gw["w_in"] = jnp.concatenate([gsum["w_conf"], gsum["w_sc"], gsum["w_swa"],
                                  gsum["w_fox"][..., :IN_FF[1] - IN_FOX[0]], gsum["w_zg"]], axis=-1)

    def small_partial(n):
        per_layer = [grads[l][n] for l in range(L)]
        if n == "rel_bias":
            return sum(pl_[:, :4] for pl_ in per_layer)
        if n in ("swa_sink", "b_forget"):
            return jnp.stack([a.reshape(-1)[:4] for a in per_layer])
        if n in ("swa_q_norm", "fox_q_norm", "fox_k_norm"):
            return jnp.stack([a.reshape(4, HEAD).sum(0) for a in per_layer])
        if n == "swa_k_norm":
            return jnp.stack([a.reshape(2, HEAD).sum(0) for a in per_layer])
        if n == "conf_dw":
            return jnp.stack([a[:CONV_K] for a in per_layer])
        if n == "sc_conv":
            return jnp.stack([a[:SC_K] for a in per_layer])
        return jnp.stack([a.reshape(-1) for a in per_layer])

    partial = [small_partial(n) for n in SMALL]
    small_shapes = [a.shape for a in partial]
    all_parts = _exchange([(_pack(partial), "gather", 0)], "gather_small_grads")[0]
    rows = all_parts.shape[0] // N_DEV
    small_sum = _unpack(_sum8(all_parts.reshape(N_DEV, rows, LANE), "sum_small"), small_shapes)
    for n, a in zip(SMALL, small_sum):
        if n in ("conf_dw", "sc_conv"):
            cs = w[n].shape[2]
            a = lax.dynamic_slice_in_dim(a, dev * cs, cs, axis=2)
        gw[n] = a

    delta, new_m, new_v = {}, {}, {}
    for n in W_NAMES:
        if n not in SMALL:
            delta[n], new_m[n], new_v[n] = _adamw(w[n], gw[n], m[n], v[n], f"adamw_{n}")
    shapes = [w[n].shape for n in SMALL]
    outs = _adamw(_pack([w[n] for n in SMALL]), _pack([gw[n] for n in SMALL]), _pack([m[n] for n in SMALL]),
                  _pack([v[n] for n in SMALL]), "adamw_small")
    for res, out in zip((delta, new_m, new_v), outs):
        for n, a in zip(SMALL, _unpack(out, shapes)):
            res[n] = a

    loss = lax.psum(loss_part[0, 0], ("x", "y", "c"))
    return loss, grad_x, gw, delta, new_m, new_v


def kernel(x, rel_bias, ffn1_norm, ffn1_w_gate, ffn1_w_up, ffn1_w_down, mix_norm, w_in, b_forget, conf_dw, conf_dw_b, conf_ln_g, conf_ln_b, conf_w_out, sc_conv, sc_w_out, swa_q_norm, swa_k_norm, swa_sink, swa_w_o, fox_q_norm, fox_k_norm, fox_w_o, w_out, ffn2_norm, ffn2_w_gate, ffn2_w_up, ffn2_w_down, loss_target, m_rel_bias, m_ffn1_norm, m_ffn1_w_gate, m_ffn1_w_up, m_ffn1_w_down, m_mix_norm, m_w_in, m_b_forget, m_conf_dw, m_conf_dw_b, m_conf_ln_g, m_conf_ln_b, m_conf_w_out, m_sc_conv, m_sc_w_out, m_swa_q_norm, m_swa_k_norm, m_swa_sink, m_swa_w_o, m_fox_q_norm, m_fox_k_norm, m_fox_w_o, m_w_out, m_ffn2_norm, m_ffn2_w_gate, m_ffn2_w_up, m_ffn2_w_down, v_rel_bias, v_ffn1_norm, v_ffn1_w_gate, v_ffn1_w_up, v_ffn1_w_down, v_mix_norm, v_w_in, v_b_forget, v_conf_dw, v_conf_dw_b, v_conf_ln_g, v_conf_ln_b, v_conf_w_out, v_sc_conv, v_sc_w_out, v_swa_q_norm, v_swa_k_norm, v_swa_sink, v_swa_w_o, v_fox_q_norm, v_fox_k_norm, v_fox_w_o, v_w_out, v_ffn2_norm, v_ffn2_w_gate, v_ffn2_w_up, v_ffn2_w_down):
    args = locals()
    w = {n: args[n] for n in W_NAMES}
    m = {n: args["m_" + n] for n in W_NAMES}
    v = {n: args["v_" + n] for n in W_NAMES}
    T, D = x.shape[-2:]
    loss, grad_x, gw, delta, new_m, new_v = _step(w, m, v, x.reshape(T, D), loss_target.reshape(T, D))
    return (loss, grad_x.reshape(x.shape), *[gw[n] for n in W_NAMES], *[delta[n] for n in W_NAMES],
            *[new_m[n] for n in W_NAMES], *[new_v[n] for n in W_NAMES])
```

```python
import math

import numpy as np
import jax
import jax.numpy as jnp
from jax import lax
from jax.experimental import pallas as pl
from jax.experimental.pallas import tpu as pltpu

F32 = jnp.float32
_MXU = jnp.bfloat16
EPS = 1e-6
NEG_INF = -1e30
HEAD = 64
CH = 256
WINDOW = 128
CONV_K = 31
SC_K = 3
CONV_HALO = 32
SC_HALO = 8
N_BUCKETS = 32
MAX_DISTANCE = 128
N_DEV = 8
LANE = 128
ROW_TILE = 512
VMEM_LIMIT = 48 * 1024 * 1024
ADAM_LR, ADAM_B1, ADAM_B2, ADAM_EPS, ADAM_WD, ADAM_STEP = 0.001, 0.9, 0.999, 1e-08, 0.01, 10

_NN = (((1,), (0,)), ((), ()))
_NT = (((1,), (1,)), ((), ()))
_TN = (((0,), (0,)), ((), ()))


def _cp(*sem):
    return pltpu.CompilerParams(dimension_semantics=sem, vmem_limit_bytes=VMEM_LIMIT)


def _tile(n, pref, align=LANE):
    t = (min(n, pref) // align) * align
    while t >= align:
        if n % t == 0:
            return t
        t -= align
    return n


def _dot(a, b, dims=_NN):
    return lax.dot_general(a.astype(_MXU), b.astype(_MXU), dims, preferred_element_type=F32)


def _split3(x):
    hi = x.astype(_MXU)
    r1 = x - hi.astype(F32)
    mid = r1.astype(_MXU)
    lo = (r1 - mid.astype(F32)).astype(_MXU)
    return hi, mid, lo


def _exact_dot(a, b, dims, data):
    if data == "a":
        return sum(lax.dot_general(p, b.astype(_MXU), dims, preferred_element_type=F32) for p in _split3(a))
    return sum(lax.dot_general(a.astype(_MXU), p, dims, preferred_element_type=F32) for p in _split3(b))


def _sigmoid(x):
    return jax.nn.sigmoid(x)


def _lane_mask(width, h):
    lane = lax.broadcasted_iota(jnp.int32, (1, width), 1)
    return (lane >= h * HEAD) & (lane < (h + 1) * HEAD)


def _head_rms(x, g, nh):
    xx = x * x
    ms = jnp.zeros_like(x)
    for h in range(nh):
        mk = _lane_mask(x.shape[-1], h)
        s = jnp.sum(jnp.where(mk, xx, 0.0), axis=-1, keepdims=True) * (1.0 / HEAD)
        ms = jnp.where(mk, s, ms)
    r = lax.rsqrt(ms + EPS)
    return x * r * g, r


def _head_rms_bwd(dy, x, r, g, nh):
    w = dy * g
    wx = w * x
    c = jnp.zeros_like(x)
    for h in range(nh):
        mk = _lane_mask(x.shape[-1], h)
        s = jnp.sum(jnp.where(mk, wx, 0.0), axis=-1, keepdims=True) * (1.0 / HEAD)
        c = jnp.where(mk, s, c)
    dx = r * w - x * (r * r * r) * c
    dg = jnp.sum(dy * x * r, axis=0, keepdims=True)
    return dx, dg


def _mm(pairs, mode, out_dtype, name, scale=None, res=None, tm=1024, tn=1024, tk=1024):
    a0, b0 = pairs[0]
    M = a0.shape[1] if mode == "tn" else a0.shape[0]
    N = b0.shape[0] if mode == "nt" else b0.shape[1]
    tm, tn = _tile(M, tm), _tile(N, tn)
    dims = {"nn": _NN, "nt": _NT, "tn": _TN}[mode]
    tks, nks, offs = [], [], []
    for a, _ in pairs:
        K = a.shape[0] if mode == "tn" else a.shape[1]
        t = _tile(K, tk)
        tks.append(t)
        nks.append(K // t)
        offs.append(sum(nks[:-1]))
    nk_tot = sum(nks)
    in_specs, args = [], []
    for (a, b), t, nk, off in zip(pairs, tks, nks, offs):
        def kk(k, off=off, nk=nk):
            return jnp.clip(k - off, 0, nk - 1)
        if mode == "tn":
            in_specs.append(pl.BlockSpec((t, tm), lambda i, j, k, kk=kk: (kk(k), i)))
        else:
            in_specs.append(pl.BlockSpec((tm, t), lambda i, j, k, kk=kk: (i, kk(k))))
        if mode == "nt":
            in_specs.append(pl.BlockSpec((tn, t), lambda i, j, k, kk=kk: (j, kk(k))))
        else:
            in_specs.append(pl.BlockSpec((t, tn), lambda i, j, k, kk=kk: (kk(k), j)))
        args += [a, b]
    if res is not None:
        in_specs.append(pl.BlockSpec((tm, tn), lambda i, j, k: (i, j)))
        args.append(res)
    npairs = len(pairs)

    def body(*refs):
        ab = refs[:2 * npairs]
        res_ref = refs[2 * npairs] if res is not None else None
        o_ref = refs[2 * npairs + (res is not None)]
        acc = refs[-1]
        k = pl.program_id(2)

        def finish(r):
            if scale is not None:
                r = r * scale
            if res_ref is not None:
                r = r + res_ref[...]
            o_ref[...] = r.astype(o_ref.dtype)

        if nk_tot == 1:
            finish(_dot(ab[0][...], ab[1][...], dims))
            return

        @pl.when(k == 0)
        def _():
            acc[...] = jnp.zeros_like(acc)

        for p in range(npairs):
            @pl.when(jnp.logical_and(k >= offs[p], k < offs[p] + nks[p]))
            def _(p=p):
                acc[...] += _dot(ab[2 * p][...], ab[2 * p + 1][...], dims)

        @pl.when(k == nk_tot - 1)
        def _():
            finish(acc[...])

    return pl.pallas_call(
        body, name=name, grid=(M // tm, N // tn, nk_tot), in_specs=in_specs,
        out_specs=pl.BlockSpec((tm, tn), lambda i, j, k: (i, j)),
        out_shape=jax.ShapeDtypeStruct((M, N), out_dtype),
        scratch_shapes=[pltpu.VMEM((tm, tn), F32)],
        compiler_params=_cp("parallel", "parallel", "arbitrary"))(*args)


def _rms_fwd(x, g, name):
    T, D = x.shape
    tm = _tile(T, ROW_TILE)

    def body(x_ref, g_ref, o_ref):
        xv = x_ref[...]
        r = lax.rsqrt(jnp.mean(xv * xv, axis=-1, keepdims=True) + EPS)
        o_ref[...] = (xv * r * g_ref[...]).astype(o_ref.dtype)

    return pl.pallas_call(
        body, name=name, grid=(T // tm,),
        in_specs=[pl.BlockSpec((tm, D), lambda i: (i, 0)), pl.BlockSpec((1, D), lambda i: (0, 0))],
        out_specs=pl.BlockSpec((tm, D), lambda i: (i, 0)),
        out_shape=jax.ShapeDtypeStruct((T, D), _MXU), compiler_params=_cp("parallel"))(x, g)


def _rms_bwd(da, x, g, dres, name):
    T, D = x.shape
    tm = _tile(T, ROW_TILE)

    def body(da_ref, x_ref, g_ref, dr_ref, dx_ref, dg_ref):
        @pl.when(pl.program_id(0) == 0)
        def _():
            dg_ref[...] = jnp.zeros_like(dg_ref)

        xv, dav = x_ref[...], da_ref[...]
        r = lax.rsqrt(jnp.mean(xv * xv, axis=-1, keepdims=True) + EPS)
        w = dav * g_ref[...]
        c = jnp.mean(w * xv, axis=-1, keepdims=True)
        dx_ref[...] = dr_ref[...] + (r * w - xv * (r * r * r) * c)
        dg_ref[...] += jnp.sum(dav * xv * r, axis=0, keepdims=True)

    row = pl.BlockSpec((tm, D), lambda i: (i, 0))
    vec = pl.BlockSpec((1, D), lambda i: (0, 0))
    return pl.pallas_call(
        body, name=name, grid=(T // tm,), in_specs=[row, row, vec, row], out_specs=[row, vec],
        out_shape=[jax.ShapeDtypeStruct((T, D), F32), jax.ShapeDtypeStruct((1, D), F32)],
        compiler_params=_cp("arbitrary"))(da, x, g, dres)


def _loss_grad(y, tgt):
    T, D = y.shape
    tm = _tile(T, ROW_TILE)

    def body(y_ref, t_ref, dy_ref, l_ref):
        @pl.when(pl.program_id(0) == 0)
        def _():
            l_ref[...] = jnp.zeros_like(l_ref)

        d = y_ref[...] - t_ref[...]
        dy_ref[...] = d * (1.0 / D)
        per_tok = jnp.mean(d * d, axis=-1, keepdims=True)
        l_ref[...] += 0.5 * jnp.sum(per_tok, axis=0, keepdims=True)

    row = pl.BlockSpec((tm, D), lambda i: (i, 0))
    return pl.pallas_call(
        body, name="loss_grad", grid=(T // tm,), in_specs=[row, row],
        out_specs=[row, pl.BlockSpec((1, 1), lambda i: (0, 0))],
        out_shape=[jax.ShapeDtypeStruct((T, D), F32), jax.ShapeDtypeStruct((1, 1), F32)],
        compiler_params=_cp("arbitrary"))(y, tgt)


def _ffn_up(a, wg, wu, name):
    T, D = a.shape
    Fp = wg.shape[1]
    tm, tn = _tile(T, ROW_TILE), _tile(Fp, 768)

    def body(a_ref, wg_ref, wu_ref, g_ref, u_ref, h_ref):
        av = a_ref[...]
        g = _dot(av, wg_ref[...])
        u = _dot(av, wu_ref[...])
        g_ref[...] = g.astype(g_ref.dtype)
        u_ref[...] = u.astype(u_ref.dtype)
        h_ref[...] = (g * _sigmoid(g) * u).astype(h_ref.dtype)

    wspec = pl.BlockSpec((D, tn), lambda j, i: (0, j))
    ospec = pl.BlockSpec((tm, tn), lambda j, i: (i, j))
    osh = jax.ShapeDtypeStruct((T, Fp), _MXU)
    return pl.pallas_call(
        body, name=name, grid=(Fp // tn, T // tm),
        in_specs=[pl.BlockSpec((tm, D), lambda j, i: (i, 0)), wspec, wspec],
        out_specs=[ospec, ospec, ospec], out_shape=[osh, osh, osh],
        compiler_params=_cp("parallel", "parallel"))(a, wg, wu)


def _ffn_bwd_hid(dxp, wd, gate, up, name):
    T, D = dxp.shape
    Fp = wd.shape[0]
    tm, tn = _tile(T, ROW_TILE), _tile(Fp, 768)

    def body(dx_ref, wd_ref, g_ref, u_ref, dg_ref, du_ref):
        dh = 0.5 * _dot(dx_ref[...], wd_ref[...], _NT)
        g = g_ref[...].astype(F32)
        u = u_ref[...].astype(F32)
        s = _sigmoid(g)
        du_ref[...] = (dh * (g * s)).astype(du_ref.dtype)
        dg_ref[...] = (dh * u * (s * (1.0 + g * (1.0 - s)))).astype(dg_ref.dtype)

    tspec = pl.BlockSpec((tm, tn), lambda j, i: (i, j))
    osh = jax.ShapeDtypeStruct((T, Fp), _MXU)
    return pl.pallas_call(
        body, name=name, grid=(Fp // tn, T // tm),
        in_specs=[pl.BlockSpec((tm, D), lambda j, i: (i, 0)), pl.BlockSpec((tn, D), lambda j, i: (j, 0)), tspec, tspec],
        out_specs=[tspec, tspec], out_shape=[osh, osh],
        compiler_params=_cp("parallel", "parallel"))(dxp, wd, gate, up)


def _conf_fwd(zc, dw, b, lng, lnb, name):
    T = zc.shape[0]
    tm = _tile(T, ROW_TILE)
    r = tm // CONV_HALO

    def body(z_ref, zh_ref, dw_ref, b_ref, g_ref, lb_ref, u1_ref, act_ref, ext):
        i = pl.program_id(0)
        cur = z_ref[...]
        ext[pl.ds(CONV_HALO, tm), :] = cur[:, :CH] * _sigmoid(cur[:, CH:])
        hal = zh_ref[...]
        ext[pl.ds(0, CONV_HALO), :] = jnp.where(i > 0, hal[:, :CH] * _sigmoid(hal[:, CH:]), 0.0)
        acc = jnp.zeros((tm, CH), F32)
        for k in range(CONV_K):
            acc = acc + dw_ref[pl.ds(k, 1), :] * ext[pl.ds(CONV_HALO - (CONV_K - 1) + k, tm), :]
        u1 = acc + b_ref[...]
        u1_ref[...] = u1
        mu = jnp.mean(u1, axis=-1, keepdims=True)
        var = jnp.mean(jnp.square(u1 - mu), axis=-1, keepdims=True)
        u2 = (u1 - mu) * lax.rsqrt(var + EPS) * g_ref[...] + lb_ref[...]
        act_ref[...] = u2 * _sigmoid(u2)

    vec = pl.BlockSpec((1, CH), lambda i: (0, 0))
    row = pl.BlockSpec((tm, CH), lambda i: (i, 0))
    osh = jax.ShapeDtypeStruct((T, CH), F32)
    return pl.pallas_call(
        body, name=name, grid=(T // tm,),
        in_specs=[pl.BlockSpec((tm, 2 * CH), lambda i: (i, 0)),
                  pl.BlockSpec((CONV_HALO, 2 * CH), lambda i: (jnp.maximum(i * r - 1, 0), 0)),
                  pl.BlockSpec((CONV_HALO, CH), lambda i: (0, 0)), vec, vec, vec],
        out_specs=[row, row], out_shape=[osh, osh],
        scratch_shapes=[pltpu.VMEM((tm + CONV_HALO, CH), F32)],
        compiler_params=_cp("parallel"))(zc, zc, dw, b, lng, lnb)


def _conf_bwd_ln(dact, u1, lng, lnb, name):
    T = u1.shape[0]
    tm = _tile(T, ROW_TILE)

    def body(da_ref, u_ref, g_ref, lb_ref, du_ref, sm_ref):
        @pl.when(pl.program_id(0) == 0)
        def _():
            sm_ref[...] = jnp.zeros_like(sm_ref)

        u1v = u_ref[...]
        mu = jnp.mean(u1v, axis=-1, keepdims=True)
        cen = u1v - mu
        rstd = lax.rsqrt(jnp.mean(cen * cen, axis=-1, keepdims=True) + EPS)
        y = cen * rstd
        u2 = y * g_ref[...] + lb_ref[...]
        s = _sigmoid(u2)
        du2 = da_ref[...] * (s * (1.0 + u2 * (1.0 - s)))
        dy = du2 * g_ref[...]
        du1 = rstd * (dy - jnp.mean(dy, axis=-1, keepdims=True) - y * jnp.mean(dy * y, axis=-1, keepdims=True))
        du_ref[...] = du1
        sm_ref[pl.ds(0, 1), :] += jnp.sum(du2 * y, axis=0, keepdims=True)
        sm_ref[pl.ds(1, 1), :] += jnp.sum(du2, axis=0, keepdims=True)
        sm_ref[pl.ds(2, 1), :] += jnp.sum(du1, axis=0, keepdims=True)

    vec = pl.BlockSpec((1, CH), lambda i: (0, 0))
    row = pl.BlockSpec((tm, CH), lambda i: (i, 0))
    return pl.pallas_call(
        body, name=name, grid=(T // tm,), in_specs=[row, row, vec, vec],
        out_specs=[row, pl.BlockSpec((8, CH), lambda i: (0, 0))],
        out_shape=[jax.ShapeDtypeStruct((T, CH), F32), jax.ShapeDtypeStruct((8, CH), F32)],
        compiler_params=_cp("arbitrary"))(dact, u1, lng, lnb)


def _conf_bwd_conv(zc, du1, dw, name):
    T = zc.shape[0]
    tm = _tile(T, ROW_TILE)
    r = tm // CONV_HALO
    nt = T // tm
    nh = T // CONV_HALO

    def body(z_ref, zh_ref, d_ref, dn_ref, dw_ref, dz_ref, ddw_ref, ext_u, ext_d):
        i = pl.program_id(0)

        @pl.when(i == 0)
        def _():
            ddw_ref[...] = jnp.zeros_like(ddw_ref)

        cur = z_ref[...]
        ca = cur[:, :CH]
        sg = _sigmoid(cur[:, CH:])
        ext_u[pl.ds(CONV_HALO, tm), :] = ca * sg
        hal = zh_ref[...]
        ext_u[pl.ds(0, CONV_HALO), :] = jnp.where(i > 0, hal[:, :CH] * _sigmoid(hal[:, CH:]), 0.0)
        d = d_ref[...]
        ext_d[pl.ds(0, tm), :] = d
        ext_d[pl.ds(tm, CONV_HALO), :] = jnp.where(i < nt - 1, dn_ref[...], 0.0)
        acc = jnp.zeros((tm, CH), F32)
        for k in range(CONV_K):
            acc = acc + dw_ref[pl.ds(k, 1), :] * ext_d[pl.ds(CONV_K - 1 - k, tm), :]
            ddw_ref[pl.ds(k, 1), :] += jnp.sum(
                d * ext_u[pl.ds(CONV_HALO - (CONV_K - 1) + k, tm), :], axis=0, keepdims=True)
        dz_ref[:, :CH] = (acc * sg).astype(dz_ref.dtype)
        dz_ref[:, CH:] = (acc * ca * sg * (1.0 - sg)).astype(dz_ref.dtype)

    return pl.pallas_call(
        body, name=name, grid=(nt,),
        in_specs=[pl.BlockSpec((tm, 2 * CH), lambda i: (i, 0)),
                  pl.BlockSpec((CONV_HALO, 2 * CH), lambda i: (jnp.maximum(i * r - 1, 0), 0)),
                  pl.BlockSpec((tm, CH), lambda i: (i, 0)),
                  pl.BlockSpec((CONV_HALO, CH), lambda i: (jnp.minimum((i + 1) * r, nh - 1), 0)),
                  pl.BlockSpec((CONV_HALO, CH), lambda i: (0, 0))],
        out_specs=[pl.BlockSpec((tm, 2 * CH), lambda i: (i, 0)), pl.BlockSpec((CONV_HALO, CH), lambda i: (0, 0))],
        out_shape=[jax.ShapeDtypeStruct((T, 2 * CH), _MXU), jax.ShapeDtypeStruct((CONV_HALO, CH), F32)],
        scratch_shapes=[pltpu.VMEM((tm + CONV_HALO, CH), F32), pltpu.VMEM((tm + CONV_HALO, CH), F32)],
        compiler_params=_cp("arbitrary"))(zc, zc, du1, du1, dw)


def _sc_fwd(zs, w, name):
    T = zs.shape[0]
    tm = _tile(T, ROW_TILE)
    r = tm // SC_HALO

    def body(z_ref, zh_ref, w_ref, act_ref, ext):
        i = pl.program_id(0)
        cur = z_ref[...]
        ext[pl.ds(SC_HALO, tm), :] = cur[:, CH:2 * CH] * cur[:, 2 * CH:]
        hal = zh_ref[...]
        ext[pl.ds(0, SC_HALO), :] = jnp.where(i > 0, hal[:, CH:2 * CH] * hal[:, 2 * CH:], 0.0)
        v1 = jnp.zeros((tm, CH), F32)
        for k in range(SC_K):
            v1 = v1 + w_ref[pl.ds(k, 1), :] * ext[pl.ds(SC_HALO - (SC_K - 1) + k, tm), :]
        act_ref[...] = cur[:, :CH] * v1

    return pl.pallas_call(
        body, name=name, grid=(T // tm,),
        in_specs=[pl.BlockSpec((tm, 3 * CH), lambda i: (i, 0)),
                  pl.BlockSpec((SC_HALO, 3 * CH), lambda i: (jnp.maximum(i * r - 1, 0), 0)),
                  pl.BlockSpec((SC_HALO, CH), lambda i: (0, 0))],
        out_specs=pl.BlockSpec((tm, CH), lambda i: (i, 0)),
        out_shape=jax.ShapeDtypeStruct((T, CH), F32),
        scratch_shapes=[pltpu.VMEM((tm + SC_HALO, CH), F32)],
        compiler_params=_cp("parallel"))(zs, zs, w)


def _sc_bwd(zs, dact, w, name):
    T = zs.shape[0]
    tm = _tile(T, ROW_TILE)
    r = tm // SC_HALO
    nt = T // tm
    nh = T // SC_HALO

    def body(z_ref, zh_ref, zn_ref, d_ref, dn_ref, w_ref, dz_ref, dw_ref, ext_v, ext_d):
        i = pl.program_id(0)

        @pl.when(i == 0)
        def _():
            dw_ref[...] = jnp.zeros_like(dw_ref)

        cur = z_ref[...]
        sb, sc, sx = cur[:, :CH], cur[:, CH:2 * CH], cur[:, 2 * CH:]
        ext_v[pl.ds(SC_HALO, tm), :] = sc * sx
        hal = zh_ref[...]
        ext_v[pl.ds(0, SC_HALO), :] = jnp.where(i > 0, hal[:, CH:2 * CH] * hal[:, 2 * CH:], 0.0)
        da = d_ref[...]
        dv1 = da * sb
        ext_d[pl.ds(0, tm), :] = dv1
        ext_d[pl.ds(tm, SC_HALO), :] = jnp.where(i < nt - 1, dn_ref[...] * zn_ref[...][:, :CH], 0.0)
        v1 = jnp.zeros((tm, CH), F32)
        dv0 = jnp.zeros((tm, CH), F32)
        for k in range(SC_K):
            shifted = ext_v[pl.ds(SC_HALO - (SC_K - 1) + k, tm), :]
            v1 = v1 + w_ref[pl.ds(k, 1), :] * shifted
            dv0 = dv0 + w_ref[pl.ds(k, 1), :] * ext_d[pl.ds(SC_K - 1 - k, tm), :]
            dw_ref[pl.ds(k, 1), :] += jnp.sum(dv1 * shifted, axis=0, keepdims=True)
        dz_ref[:, :CH] = (da * v1).astype(dz_ref.dtype)
        dz_ref[:, CH:2 * CH] = (dv0 * sx).astype(dz_ref.dtype)
        dz_ref[:, 2 * CH:] = (dv0 * sc).astype(dz_ref.dtype)

    return pl.pallas_call(
        body, name=name, grid=(nt,),
        in_specs=[pl.BlockSpec((tm, 3 * CH), lambda i: (i, 0)),
                  pl.BlockSpec((SC_HALO, 3 * CH), lambda i: (jnp.maximum(i * r - 1, 0), 0)),
                  pl.BlockSpec((SC_HALO, 3 * CH), lambda i: (jnp.minimum((i + 1) * r, nh - 1), 0)),
                  pl.BlockSpec((tm, CH), lambda i: (i, 0)),
                  pl.BlockSpec((SC_HALO, CH), lambda i: (jnp.minimum((i + 1) * r, nh - 1), 0)),
                  pl.BlockSpec((SC_HALO, CH), lambda i: (0, 0))],
        out_specs=[pl.BlockSpec((tm, 3 * CH), lambda i: (i, 0)), pl.BlockSpec((SC_HALO, CH), lambda i: (0, 0))],
        out_shape=[jax.ShapeDtypeStruct((T, 3 * CH), _MXU), jax.ShapeDtypeStruct((SC_HALO, CH), F32)],
        scratch_shapes=[pltpu.VMEM((tm + SC_HALO, CH), F32), pltpu.VMEM((tm + SC_HALO, CH), F32)],
        compiler_params=_cp("arbitrary"))(zs, zs, zs, dact, dact, w)


SWA_TQ = 256


def _t5_bucket_np(dist):
    max_exact = N_BUCKETS // 2
    d = np.maximum(dist, 1).astype(np.float32)
    large = max_exact + (np.log(d / np.float32(max_exact)) / np.float32(math.log(MAX_DISTANCE / max_exact))
                         * np.float32(N_BUCKETS - max_exact)).astype(np.int32)
    large = np.minimum(large, N_BUCKETS - 1)
    return np.where(dist < max_exact, dist, large).astype(np.int32)


def _swa_bucket_matrix(tq):
    dist = WINDOW + np.arange(tq)[:, None] - np.arange(tq + WINDOW)[None, :]
    ok = (dist >= 0) & (dist < WINDOW)
    return np.where(ok, _t5_bucket_np(np.maximum(dist, 0)), -1).astype(np.int32)


def _kv_expand_matrix():
    e = np.zeros((2 * HEAD, 4 * HEAD), np.float32)
    for h in range(4):
        for d in range(HEAD):
            e[(h // 2) * HEAD + d, h * HEAD + d] = 1.0
    return e


def _swa_bias(rel_bias, bucket, name):
    tq, tk = bucket.shape

    def body(rb_ref, bk_ref, o_ref):
        h = pl.program_id(0)
        bk = bk_ref[...]
        acc = jnp.full((tq, tk), NEG_INF, F32)
        for b in range(N_BUCKETS):
            acc = jnp.where(bk == b, rb_ref[b, h], acc)
        o_ref[0] = acc

    return pl.pallas_call(
        body, name=name, grid=(4,),
        in_specs=[pl.BlockSpec(memory_space=pltpu.SMEM), pl.BlockSpec((tq, tk), lambda h: (0, 0))],
        out_specs=pl.BlockSpec((1, tq, tk), lambda h: (h, 0, 0)),
        out_shape=jax.ShapeDtypeStruct((4, tq, tk), F32), compiler_params=_cp("parallel"))(rel_bias, bucket)


def _swa_probs(qh, kx, bm, first_col, sk):
    s = _dot(qh, kx, _NT) * (HEAD ** -0.5)
    col = lax.broadcasted_iota(jnp.int32, s.shape, 1)
    valid = (bm > 0.5 * NEG_INF) & (col >= first_col)
    s = jnp.where(valid, s + bm, NEG_INF)
    m = jnp.maximum(jnp.max(s, axis=-1, keepdims=True), sk)
    p = jnp.exp(s - m)
    den = jnp.sum(p, axis=-1, keepdims=True) + jnp.exp(sk - m)
    return p / den, m, den


def _swa_fwd(zw, gq, gk, sink, bias, expand, name):
    T = zw.shape[0]
    tq = bias.shape[1]
    r = tq // WINDOW

    def body(z_ref, zh_ref, gq_ref, gk_ref, sink_ref, b_ref, e_ref, o_ref, kext, vext):
        i = pl.program_id(0)
        cur = z_ref[...]
        qn, _ = _head_rms(cur[:, :4 * HEAD], gq_ref[...], 4)
        kc, _ = _head_rms(cur[:, 4 * HEAD:6 * HEAD], gk_ref[...], 2)
        hal = zh_ref[...]
        kp, _ = _head_rms(hal[:, :2 * HEAD], gk_ref[...], 2)
        kext[pl.ds(0, WINDOW), :] = kp
        kext[pl.ds(WINDOW, tq), :] = kc
        vext[pl.ds(0, WINDOW), :] = hal[:, 2 * HEAD:]
        vext[pl.ds(WINDOW, tq), :] = cur[:, 6 * HEAD:]
        kx = _dot(kext[...], e_ref[...]).astype(_MXU)
        vx = _dot(vext[...], e_ref[...]).astype(_MXU)
        first_col = jnp.where(i > 0, 0, WINDOW)
        out = jnp.zeros((tq, 4 * HEAD), F32)
        for h in range(4):
            mk = _lane_mask(4 * HEAD, h)
            qh = jnp.where(mk, qn, 0.0)
            pn, _, _ = _swa_probs(qh, kx, b_ref[h], first_col, sink_ref[0, h])
            out = jnp.where(mk, _dot(pn, vx), out)
        o_ref[...] = out

    return pl.pallas_call(
        body, name=name, grid=(T // tq,),
        in_specs=[pl.BlockSpec((tq, 8 * HEAD), lambda i: (i, 0)),
                  pl.BlockSpec((WINDOW, 4 * HEAD), lambda i: (jnp.maximum(i * r - 1, 0), 1)),
                  pl.BlockSpec((1, 4 * HEAD), lambda i: (0, 0)), pl.BlockSpec((1, 2 * HEAD), lambda i: (0, 0)),
                  pl.BlockSpec(memory_space=pltpu.SMEM),
                  pl.BlockSpec(bias.shape, lambda i: (0, 0, 0)),
                  pl.BlockSpec(expand.shape, lambda i: (0, 0))],
        out_specs=pl.BlockSpec((tq, 4 * HEAD), lambda i: (i, 0)),
        out_shape=jax.ShapeDtypeStruct((T, 4 * HEAD), F32),
        scratch_shapes=[pltpu.VMEM((tq + WINDOW, 2 * HEAD), F32), pltpu.VMEM((tq + WINDOW, 2 * HEAD), F32)],
        compiler_params=_cp("parallel"))(zw, zw, gq, gk, sink, bias, expand)


def _swa_bwd(zw, dact, gq, gk, sink, bias, bucket, expand, name):
    T = zw.shape[0]
    tq = bias.shape[1]
    tk = tq + WINDOW
    r = tq // WINDOW
    nt = T // tq
    nb = T // WINDOW
    scale = HEAD ** -0.5

    def body(z_ref, zh_ref, zn_ref, d_ref, dn_ref, gq_ref, gk_ref, sink_ref, b_ref, bk_ref, e_ref,
             dz_ref, dgq_ref, dgk_ref, dsk_ref, drb_ref, kext, vext, dk_s, dv_s, db_s):
        i = pl.program_id(0)

        @pl.when(i == 0)
        def _():
            dgq_ref[...] = jnp.zeros_like(dgq_ref)
            dgk_ref[...] = jnp.zeros_like(dgk_ref)
            dsk_ref[...] = jnp.zeros_like(dsk_ref)
            drb_ref[...] = jnp.zeros_like(drb_ref)
            db_s[...] = jnp.zeros_like(db_s)

        lane = lax.broadcasted_iota(jnp.int32, (1, LANE), 1)
        cur = z_ref[...]
        q_raw, k_raw = cur[:, :4 * HEAD], cur[:, 4 * HEAD:6 * HEAD]
        qn, q_r = _head_rms(q_raw, gq_ref[...], 4)
        kc, k_r = _head_rms(k_raw, gk_ref[...], 2)
        hal = zh_ref[...]
        kp, _ = _head_rms(hal[:, :2 * HEAD], gk_ref[...], 2)
        kext[pl.ds(0, WINDOW), :] = kp
        kext[pl.ds(WINDOW, tq), :] = kc
        vext[pl.ds(0, WINDOW), :] = hal[:, 2 * HEAD:]
        vext[pl.ds(WINDOW, tq), :] = cur[:, 6 * HEAD:]
        ev = e_ref[...]
        kx = _dot(kext[...], ev).astype(_MXU)
        vx = _dot(vext[...], ev).astype(_MXU)
        first_col = jnp.where(i > 0, 0, WINDOW)
        do = d_ref[...]
        dq = jnp.zeros((tq, 4 * HEAD), F32)
        dkx = jnp.zeros((tk, 4 * HEAD), F32)
        dvx = jnp.zeros((tk, 4 * HEAD), F32)
        dsk = jnp.zeros((1, LANE), F32)
        for h in range(4):
            mk = _lane_mask(4 * HEAD, h)
            qh = jnp.where(mk, qn, 0.0).astype(_MXU)
            sk = sink_ref[0, h]
            pn, m, den = _swa_probs(qh, kx, b_ref[h], first_col, sk)
            doh = jnp.where(mk, do, 0.0).astype(_MXU)
            dpn = _dot(doh, vx, _NT)
            delta = jnp.sum(pn * dpn, axis=-1, keepdims=True)
            ds = pn * (dpn - delta)
            psink = jnp.exp(sk - m) / den
            dsk = dsk + jnp.where(lane == h, jnp.sum(-psink * delta, axis=0, keepdims=True), 0.0)
            db_s[h] += ds
            dss = (ds * scale).astype(_MXU)
            dq = dq + jnp.where(mk, _dot(dss, kx), 0.0)
            dkx = dkx + _dot(dss, qh, _TN)
            dvx = dvx + _dot(pn, doh, _TN)
        dsk_ref[...] += dsk
        dk_ext = _exact_dot(dkx, ev, _NT, "a")
        dv_ext = _exact_dot(dvx, ev, _NT, "a")
        dk_s[...] = dk_ext[WINDOW:, :]
        dv_s[...] = dv_ext[WINDOW:, :]

        @pl.when(i < nt - 1)
        def _():
            nxt = zn_ref[...]
            q2, _ = _head_rms(nxt[:, :4 * HEAD], gq_ref[...], 4)
            k2n, _ = _head_rms(nxt[:, 4 * HEAD:6 * HEAD], gk_ref[...], 2)
            k2 = jnp.concatenate([kc[tq - WINDOW:, :], k2n], axis=0)
            v2 = jnp.concatenate([cur[tq - WINDOW:, 6 * HEAD:], nxt[:, 6 * HEAD:]], axis=0)
            k2x = _dot(k2, ev).astype(_MXU)
            v2x = _dot(v2, ev).astype(_MXU)
            do2 = dn_ref[...]
            dk2x = jnp.zeros((2 * WINDOW, 4 * HEAD), F32)
            dv2x = jnp.zeros((2 * WINDOW, 4 * HEAD), F32)
            for h in range(4):
                mk = _lane_mask(4 * HEAD, h)
                qh = jnp.where(mk, q2, 0.0).astype(_MXU)
                pn, _, _ = _swa_probs(qh, k2x, b_ref[h][:WINDOW, :2 * WINDOW], 0, sink_ref[0, h])
                doh = jnp.where(mk, do2, 0.0).astype(_MXU)
                dpn = _dot(doh, v2x, _NT)
                ds = pn * (dpn - jnp.sum(pn * dpn, axis=-1, keepdims=True))
                dk2x = dk2x + _dot((ds * scale).astype(_MXU), qh, _TN)
                dv2x = dv2x + _dot(pn, doh, _TN)
            dk_s[pl.ds(tq - WINDOW, WINDOW), :] += _exact_dot(dk2x, ev, _NT, "a")[:WINDOW, :]
            dv_s[pl.ds(tq - WINDOW, WINDOW), :] += _exact_dot(dv2x, ev, _NT, "a")[:WINDOW, :]

        dq_raw, dgq = _head_rms_bwd(dq, q_raw, q_r, gq_ref[...], 4)
        dk_raw, dgk = _head_rms_bwd(dk_s[...], k_raw, k_r, gk_ref[...], 2)
        dgq_ref[...] += dgq
        dgk_ref[...] += dgk
        dz_ref[:, :4 * HEAD] = dq_raw.astype(dz_ref.dtype)
        dz_ref[:, 4 * HEAD:6 * HEAD] = dk_raw.astype(dz_ref.dtype)
        dz_ref[:, 6 * HEAD:] = dv_s[...].astype(dz_ref.dtype)

        @pl.when(i == nt - 1)
        def _():
            bk = bk_ref[...]
            for b in range(N_BUCKETS):
                rowv = jnp.zeros((1, LANE), F32)
                for h in range(4):
                    s1 = jnp.sum(jnp.where(bk == b, db_s[h], 0.0), axis=0, keepdims=True)
                    rowv = jnp.where(lane == h, jnp.sum(s1, axis=1, keepdims=True), rowv)
                drb_ref[pl.ds(b, 1), :] = rowv

    const2 = lambda i: (0, 0)
    return pl.pallas_call(
        body, name=name, grid=(nt,),
        in_specs=[pl.BlockSpec((tq, 8 * HEAD), lambda i: (i, 0)),
                  pl.BlockSpec((WINDOW, 4 * HEAD), lambda i: (jnp.maximum(i * r - 1, 0), 1)),
                  pl.BlockSpec((WINDOW, 8 * HEAD), lambda i: (jnp.minimum((i + 1) * r, nb - 1), 0)),
                  pl.BlockSpec((tq, 4 * HEAD), lambda i: (i, 0)),
                  pl.BlockSpec((WINDOW, 4 * HEAD), lambda i: (jnp.minimum((i + 1) * r, nb - 1), 0)),
                  pl.BlockSpec((1, 4 * HEAD), const2), pl.BlockSpec((1, 2 * HEAD), const2),
                  pl.BlockSpec(memory_space=pltpu.SMEM),
                  pl.BlockSpec(bias.shape, lambda i: (0, 0, 0)),
                  pl.BlockSpec(bucket.shape, const2), pl.BlockSpec(expand.shape, const2)],
        out_specs=[pl.BlockSpec((tq, 8 * HEAD), lambda i: (i, 0)),
                   pl.BlockSpec((1, 4 * HEAD), const2), pl.BlockSpec((1, 2 * HEAD), const2),
                   pl.BlockSpec((1, LANE), const2), pl.BlockSpec((N_BUCKETS, LANE), const2)],
        out_shape=[jax.ShapeDtypeStruct((T, 8 * HEAD), _MXU), jax.ShapeDtypeStruct((1, 4 * HEAD), F32),
                   jax.ShapeDtypeStruct((1, 2 * HEAD), F32), jax.ShapeDtypeStruct((1, LANE), F32),
                   jax.ShapeDtypeStruct((N_BUCKETS, LANE), F32)],
        scratch_shapes=[pltpu.VMEM((tk, 2 * HEAD), F32), pltpu.VMEM((tk, 2 * HEAD), F32),
                        pltpu.VMEM((tq, 2 * HEAD), F32), pltpu.VMEM((tq, 2 * HEAD), F32),
                        pltpu.VMEM((4, tq, tk), F32)],
        compiler_params=_cp("arbitrary"))(zw, zw, zw, dact, dact, gq, gk, sink, bias, bucket, expand)


FOX_B = 512
FOX_TM = 256


def _tri(n, lower):
    m = np.tril(np.ones((n, n), np.float32)) if lower else np.triu(np.ones((n, n), np.float32))
    return m


def _log_sigmoid(x):
    return jnp.minimum(x, 0.0) - jnp.log1p(jnp.exp(-jnp.abs(x)))


def _fox_prep(zf, gq, gk, bf, name):
    T = zf.shape[0]
    tm = _tile(T, FOX_TM)
    lower = jnp.asarray(_tri(tm, True), _MXU)

    def body(z_ref, gq_ref, gk_ref, bf_ref, l_ref, q_ref, k_ref, v_ref, f_ref, ft_ref, carry):
        @pl.when(pl.program_id(0) == 0)
        def _():
            carry[...] = jnp.zeros_like(carry)

        z = z_ref[...]
        q, _ = _head_rms(z[:, :CH], gq_ref[...], 4)
        k, _ = _head_rms(z[:, CH:2 * CH], gk_ref[...], 4)
        q_ref[...] = q.astype(q_ref.dtype)
        k_ref[...] = k.astype(k_ref.dtype)
        v_ref[...] = z[:, 2 * CH:3 * CH].astype(v_ref.dtype)
        lane = lax.broadcasted_iota(jnp.int32, (1, LANE), 1)
        lf = jnp.where(lane < 4, _log_sigmoid(z[:, 3 * CH:] + bf_ref[...]), 0.0)
        fv = _exact_dot(l_ref[...], lf, _NN, "b") + carry[pl.ds(0, 1), :]
        f_ref[...] = fv
        ft_ref[...] = fv.T
        carry[pl.ds(0, 1), :] = f_ref[pl.ds(tm - 1, 1), :]

    row = pl.BlockSpec((tm, CH), lambda i: (i, 0))
    vec = pl.BlockSpec((1, CH), lambda i: (0, 0))
    qsh = jax.ShapeDtypeStruct((T, CH), _MXU)
    return pl.pallas_call(
        body, name=name, grid=(T // tm,),
        in_specs=[pl.BlockSpec((tm, 3 * CH + LANE), lambda i: (i, 0)), vec, vec,
                  pl.BlockSpec((1, LANE), lambda i: (0, 0)), pl.BlockSpec((tm, tm), lambda i: (0, 0))],
        out_specs=[row, row, row, pl.BlockSpec((tm, LANE), lambda i: (i, 0)), pl.BlockSpec((LANE, tm), lambda i: (0, i))],
        out_shape=[qsh, qsh, qsh, jax.ShapeDtypeStruct((T, LANE), F32), jax.ShapeDtypeStruct((LANE, T), F32)],
        scratch_shapes=[pltpu.VMEM((8, LANE), F32)],
        compiler_params=_cp("arbitrary"))(zf, gq, gk, bf, lower)


def _lane_col(x, h):
    lane = lax.broadcasted_iota(jnp.int32, (1, x.shape[-1]), 1)
    return jnp.sum(jnp.where(lane == h, x, 0.0), axis=-1, keepdims=True)


def _fox_scores(qh, k, fq, ft_ref, h, qi, ki, B):
    s = _dot(qh, k, _NT) * (HEAD ** -0.5)
    s = s + (fq - ft_ref[pl.ds(h, 1), :])
    row = qi * B + lax.broadcasted_iota(jnp.int32, s.shape, 0)
    col = ki * B + lax.broadcasted_iota(jnp.int32, s.shape, 1)
    return jnp.where(col <= row, s, NEG_INF)


def _fox_fwd(q, k, v, f, ft, name):
    T = q.shape[0]
    B = _tile(T, FOX_B)
    n = T // B

    def body(q_ref, k_ref, v_ref, f_ref, ft_ref, o_ref, lse_ref, m_s, l_s, acc):
        qi, ki = pl.program_id(0), pl.program_id(1)

        @pl.when(ki == 0)
        def _():
            m_s[...] = jnp.full_like(m_s, NEG_INF)
            l_s[...] = jnp.zeros_like(l_s)
            acc[...] = jnp.zeros_like(acc)

        @pl.when(ki <= qi)
        def _():
            qv, kv, vv, fv = q_ref[...], k_ref[...], v_ref[...], f_ref[...]
            for h in range(4):
                mk = _lane_mask(CH, h)
                qh = jnp.where(mk, qv, jnp.zeros_like(qv))
                s = _fox_scores(qh, kv, _lane_col(fv, h), ft_ref, h, qi, ki, B)
                m_old = m_s[h]
                m_new = jnp.maximum(m_old, jnp.max(s, axis=-1, keepdims=True))
                alpha = jnp.exp(m_old - m_new)
                p = jnp.exp(s - m_new)
                l_s[h] = alpha * l_s[h] + jnp.sum(p, axis=-1, keepdims=True)
                m_s[h] = m_new
                acc[...] = jnp.where(mk, acc[...] * alpha + _dot(p, vv), acc[...])

        @pl.when(ki == qi)
        def _():
            lane = lax.broadcasted_iota(jnp.int32, (1, LANE), 1)
            out = acc[...]
            lse = jnp.zeros((B, LANE), F32)
            for h in range(4):
                out = jnp.where(_lane_mask(CH, h), out / l_s[h], out)
                lse = jnp.where(lane == h, m_s[h] + jnp.log(l_s[h]), lse)
            o_ref[...] = out
            lse_ref[...] = lse

    qspec = pl.BlockSpec((B, CH), lambda qi, ki: (qi, 0))
    kspec = pl.BlockSpec((B, CH), lambda qi, ki: (jnp.minimum(ki, qi), 0))
    return pl.pallas_call(
        body, name=name, grid=(n, n),
        in_specs=[qspec, kspec, kspec, pl.BlockSpec((B, LANE), lambda qi, ki: (qi, 0)),
                  pl.BlockSpec((8, B), lambda qi, ki: (0, jnp.minimum(ki, qi)))],
        out_specs=[qspec, pl.BlockSpec((B, LANE), lambda qi, ki: (qi, 0))],
        out_shape=[jax.ShapeDtypeStruct((T, CH), F32), jax.ShapeDtypeStruct((T, LANE), F32)],
        scratch_shapes=[pltpu.VMEM((4, B, 1), F32), pltpu.VMEM((4, B, 1), F32), pltpu.VMEM((B, CH), F32)],
        compiler_params=_cp("parallel", "arbitrary"))(q, k, v, f, ft)


def _fox_delta(o, do, name):
    T = o.shape[0]
    tm = _tile(T, ROW_TILE)

    def body(o_ref, d_ref, out_ref):
        prod = o_ref[...] * d_ref[...]
        lane = lax.broadcasted_iota(jnp.int32, (1, LANE), 1)
        out = jnp.zeros((tm, LANE), F32)
        for h in range(4):
            s = jnp.sum(jnp.where(_lane_mask(CH, h), prod, 0.0), axis=-1, keepdims=True)
            out = jnp.where(lane == h, s, out)
        out_ref[...] = out

    row = pl.BlockSpec((tm, CH), lambda i: (i, 0))
    return pl.pallas_call(
        body, name=name, grid=(T // tm,), in_specs=[row, row],
        out_specs=pl.BlockSpec((tm, LANE), lambda i: (i, 0)),
        out_shape=jax.ShapeDtypeStruct((T, LANE), F32), compiler_params=_cp("parallel"))(o, do)


def _fox_bwd_dq(q, k, v, f, ft, lse, delta, do, name):
    T = q.shape[0]
    B = _tile(T, FOX_B)
    n = T // B

    def body(q_ref, k_ref, v_ref, f_ref, ft_ref, lse_ref, dl_ref, do_ref, dq_ref, dfq_ref, dq_s, df_s):
        qi, ki = pl.program_id(0), pl.program_id(1)

        @pl.when(ki == 0)
        def _():
            dq_s[...] = jnp.zeros_like(dq_s)
            df_s[...] = jnp.zeros_like(df_s)

        @pl.when(ki <= qi)
        def _():
            qv, kv, vv, fv = q_ref[...], k_ref[...], v_ref[...], f_ref[...]
            lsev, dlv, dov = lse_ref[...], dl_ref[...], do_ref[...]
            lane = lax.broadcasted_iota(jnp.int32, (1, LANE), 1)
            for h in range(4):
                mk = _lane_mask(CH, h)
                qh = jnp.where(mk, qv, jnp.zeros_like(qv))
                s = _fox_scores(qh, kv, _lane_col(fv, h), ft_ref, h, qi, ki, B)
                p = jnp.exp(s - _lane_col(lsev, h))
                doh = jnp.where(mk, dov, 0.0)
                ds = p * (_dot(doh, vv, _NT) - _lane_col(dlv, h))
                dq_s[...] += jnp.where(mk, _dot(ds * (HEAD ** -0.5), kv), 0.0)
                df_s[...] += jnp.where(lane == h, jnp.sum(ds, axis=-1, keepdims=True), 0.0)

        @pl.when(ki == qi)
        def _():
            dq_ref[...] = dq_s[...]
            dfq_ref[...] = df_s[...]

    qspec = pl.BlockSpec((B, CH), lambda qi, ki: (qi, 0))
    kspec = pl.BlockSpec((B, CH), lambda qi, ki: (jnp.minimum(ki, qi), 0))
    lspec = pl.BlockSpec((B, LANE), lambda qi, ki: (qi, 0))
    return pl.pallas_call(
        body, name=name, grid=(n, n),
        in_specs=[qspec, kspec, kspec, lspec, pl.BlockSpec((8, B), lambda qi, ki: (0, jnp.minimum(ki, qi))),
                  lspec, lspec, qspec],
        out_specs=[qspec, lspec],
        out_shape=[jax.ShapeDtypeStruct((T, CH), F32), jax.ShapeDtypeStruct((T, LANE), F32)],
        scratch_shapes=[pltpu.VMEM((B, CH), F32), pltpu.VMEM((B, LANE), F32)],
        compiler_params=_cp("parallel", "arbitrary"))(q, k, v, f, ft, lse, delta, do)


def _fox_bwd_dkv(q, k, v, f, ft, lse, delta, do, name):
    T = q.shape[0]
    B = _tile(T, FOX_B)
    n = T // B

    def body(q_ref, k_ref, v_ref, f_ref, ft_ref, lse_ref, dl_ref, do_ref, dk_ref, dv_ref, dft_ref, dk_s, dv_s, df_s):
        ki, qi = pl.program_id(0), pl.program_id(1)

        @pl.when(qi == 0)
        def _():
            dk_s[...] = jnp.zeros_like(dk_s)
            dv_s[...] = jnp.zeros_like(dv_s)
            df_s[...] = jnp.zeros_like(df_s)

        @pl.when(qi >= ki)
        def _():
            qv, kv, vv, fv = q_ref[...], k_ref[...], v_ref[...], f_ref[...]
            lsev, dlv, dov = lse_ref[...], dl_ref[...], do_ref[...]
            for h in range(4):
                mk = _lane_mask(CH, h)
                qh = jnp.where(mk, qv, jnp.zeros_like(qv))
                s = _fox_scores(qh, kv, _lane_col(fv, h), ft_ref, h, qi, ki, B)
                p = jnp.exp(s - _lane_col(lsev, h))
                doh = jnp.where(mk, dov, 0.0)
                ds = p * (_dot(doh, vv, _NT) - _lane_col(dlv, h))
                dv_s[...] += _dot(p, doh, _TN)
                dk_s[...] += _dot(ds * (HEAD ** -0.5), qh, _TN)
                df_s[pl.ds(h, 1), :] -= jnp.sum(ds, axis=0, keepdims=True)

        @pl.when(qi == n - 1)
        def _():
            dk_ref[...] = dk_s[...]
            dv_ref[...] = dv_s[...]
            dft_ref[...] = jnp.zeros_like(dft_ref)
            dft_ref[pl.ds(0, 8), :] = df_s[...]

    qspec = pl.BlockSpec((B, CH), lambda ki, qi: (jnp.maximum(qi, ki), 0))
    kspec = pl.BlockSpec((B, CH), lambda ki, qi: (ki, 0))
    lspec = pl.BlockSpec((B, LANE), lambda ki, qi: (jnp.maximum(qi, ki), 0))
    return pl.pallas_call(
        body, name=name, grid=(n, n),
        in_specs=[qspec, kspec, kspec, lspec, pl.BlockSpec((8, B), lambda ki, qi: (0, ki)), lspec, lspec, qspec],
        out_specs=[kspec, kspec, pl.BlockSpec((LANE, B), lambda ki, qi: (0, ki))],
        out_shape=[jax.ShapeDtypeStruct((T, CH), F32), jax.ShapeDtypeStruct((T, CH), F32),
                   jax.ShapeDtypeStruct((LANE, T), F32)],
        scratch_shapes=[pltpu.VMEM((B, CH), F32), pltpu.VMEM((B, CH), F32), pltpu.VMEM((8, B), F32)],
        compiler_params=_cp("parallel", "arbitrary"))(q, k, v, f, ft, lse, delta, do)


def _fox_post(zf, dqn, dkn, dv, dfq, dft, gq, gk, bf, name):
    T = zf.shape[0]
    tm = _tile(T, FOX_TM)
    nt = T // tm
    upper = jnp.asarray(_tri(tm, False), _MXU)

    def body(z_ref, dq_ref, dk_ref, dv_ref, dfq_ref, dft_ref, gq_ref, gk_ref, bf_ref, u_ref, dz_ref, sm_ref, carry, rc_s):
        @pl.when(pl.program_id(0) == 0)
        def _():
            carry[...] = jnp.zeros_like(carry)
            sm_ref[...] = jnp.zeros_like(sm_ref)

        z = z_ref[...]
        q_raw, k_raw = z[:, :CH], z[:, CH:2 * CH]
        _, q_r = _head_rms(q_raw, gq_ref[...], 4)
        _, k_r = _head_rms(k_raw, gk_ref[...], 4)
        dq, dgq = _head_rms_bwd(dq_ref[...], q_raw, q_r, gq_ref[...], 4)
        dk, dgk = _head_rms_bwd(dk_ref[...], k_raw, k_r, gk_ref[...], 4)
        df = dfq_ref[...] + dft_ref[...].T
        rc_s[...] = _exact_dot(u_ref[...], df, _NN, "b") + carry[pl.ds(0, 1), :]
        carry[pl.ds(0, 1), :] = rc_s[pl.ds(0, 1), :]
        lane = lax.broadcasted_iota(jnp.int32, (1, LANE), 1)
        x = z[:, 3 * CH:] + bf_ref[...]
        dff = jnp.where(lane < 4, rc_s[...] * _sigmoid(-x), 0.0)
        dz_ref[:, :CH] = dq.astype(dz_ref.dtype)
        dz_ref[:, CH:2 * CH] = dk.astype(dz_ref.dtype)
        dz_ref[:, 2 * CH:3 * CH] = dv_ref[...].astype(dz_ref.dtype)
        dz_ref[:, 3 * CH:] = dff.astype(dz_ref.dtype)
        sm_ref[pl.ds(0, 1), :] += dgq
        sm_ref[pl.ds(1, 1), :] += dgk
        sm_ref[pl.ds(2, 1), :LANE] += jnp.sum(dff, axis=0, keepdims=True)

    rev = lambda i: (nt - 1 - i, 0)
    row = pl.BlockSpec((tm, CH), rev)
    lrow = pl.BlockSpec((tm, LANE), rev)
    vec = pl.BlockSpec((1, CH), lambda i: (0, 0))
    return pl.pallas_call(
        body, name=name, grid=(nt,),
        in_specs=[pl.BlockSpec((tm, 3 * CH + LANE), rev), row, row, row, lrow,
                  pl.BlockSpec((LANE, tm), lambda i: (0, nt - 1 - i)), vec, vec,
                  pl.BlockSpec((1, LANE), lambda i: (0, 0)), pl.BlockSpec((tm, tm), lambda i: (0, 0))],
        out_specs=[pl.BlockSpec((tm, 3 * CH + LANE), rev), pl.BlockSpec((8, CH), lambda i: (0, 0))],
        out_shape=[jax.ShapeDtypeStruct((T, 3 * CH + LANE), _MXU), jax.ShapeDtypeStruct((8, CH), F32)],
        scratch_shapes=[pltpu.VMEM((8, LANE), F32), pltpu.VMEM((tm, LANE), F32)],
        compiler_params=_cp("arbitrary"))(zf, dqn, dkn, dv, dfq, dft, gq, gk, bf, upper)


AUG_F, AUG_ONE, AUG_LSE = HEAD, HEAD + 3, HEAD + 6


def _pieces(x):
    hi = x.astype(_MXU).astype(F32)
    r1 = x - hi
    mid = r1.astype(_MXU).astype(F32)
    lo = (r1 - mid).astype(_MXU).astype(F32)
    return hi, mid, lo


def _put_pieces(base, first_lane, x, sign):
    lane = lax.broadcasted_iota(jnp.int32, (1, LANE), 1)
    for j, piece in enumerate(_pieces(x)):
        base = jnp.where(lane == first_lane + j, sign * piece, base)
    return base


def _head_select_matrix():
    p = np.zeros((4, 4 * HEAD, LANE), np.float32)
    for h in range(4):
        for d in range(HEAD):
            p[h, h * HEAD + d, d] = 1.0
    return p


def _tri_steps(n, by_key):
    if by_key:
        pairs = [(q, k) for k in range(n) for q in range(k, n)]
    else:
        pairs = [(q, k) for q in range(n) for k in range(q + 1)]
    return (jnp.asarray([p[0] for p in pairs], jnp.int32), jnp.asarray([p[1] for p in pairs], jnp.int32))


def _fox2_prep(zf, gq, gk, bf, sel, name):
    T = zf.shape[0]
    tm = _tile(T, FOX_TM)
    lower = jnp.asarray(_tri(tm, True), _MXU)

    def body(z_ref, gq_ref, gk_ref, bf_ref, l_ref, p_ref, qa_ref, ka_ref, va_ref, carry, f_s):
        @pl.when(pl.program_id(0) == 0)
        def _():
            carry[...] = jnp.zeros_like(carry)

        z = z_ref[...]
        q, _ = _head_rms(z[:, :CH], gq_ref[...], 4)
        k, _ = _head_rms(z[:, CH:2 * CH], gk_ref[...], 4)
        q = (q * (HEAD ** -0.5)).astype(_MXU)
        k = k.astype(_MXU)
        v = z[:, 2 * CH:3 * CH].astype(_MXU)
        lane = lax.broadcasted_iota(jnp.int32, (1, LANE), 1)
        lf = jnp.where(lane < 4, _log_sigmoid(z[:, 3 * CH:] + bf_ref[...]), 0.0)
        f_s[...] = _exact_dot(l_ref[...], lf, _NN, "b") + carry[pl.ds(0, 1), :]
        carry[pl.ds(0, 1), :] = f_s[pl.ds(tm - 1, 1), :]
        fv = f_s[...]
        q_ones = (lane >= AUG_ONE) & (lane < AUG_ONE + 3)
        k_ones = ((lane >= AUG_F) & (lane < AUG_F + 3)) | ((lane >= AUG_LSE) & (lane < AUG_LSE + 3))
        v_ones = (lane >= AUG_F) & (lane < AUG_F + 3)
        for h in range(4):
            fh = _lane_col(fv, h)
            qa = jnp.where(q_ones, 1.0, _dot(q, p_ref[h]))
            qa_ref[h] = _put_pieces(qa, AUG_F, fh, 1.0).astype(qa_ref.dtype)
            ka = jnp.where(k_ones, 1.0, _dot(k, p_ref[h]))
            ka_ref[h] = _put_pieces(ka, AUG_ONE, fh, -1.0).astype(ka_ref.dtype)
            va_ref[h] = jnp.where(v_ones, 1.0, _dot(v, p_ref[h])).astype(va_ref.dtype)

    vec = pl.BlockSpec((1, CH), lambda i: (0, 0))
    hspec = pl.BlockSpec((4, tm, LANE), lambda i: (0, i, 0))
    hsh = jax.ShapeDtypeStruct((4, T, LANE), _MXU)
    return pl.pallas_call(
        body, name=name, grid=(T // tm,),
        in_specs=[pl.BlockSpec((tm, 3 * CH + LANE), lambda i: (i, 0)), vec, vec,
                  pl.BlockSpec((1, LANE), lambda i: (0, 0)), pl.BlockSpec((tm, tm), lambda i: (0, 0)),
                  pl.BlockSpec(sel.shape, lambda i: (0, 0, 0))],
        out_specs=[hspec, hspec, hspec], out_shape=[hsh, hsh, hsh],
        scratch_shapes=[pltpu.VMEM((8, LANE), F32), pltpu.VMEM((tm, LANE), F32)],
        compiler_params=_cp("arbitrary"))(zf, gq, gk, bf, lower, sel)


def _causal(s, transposed):
    row = lax.broadcasted_iota(jnp.int32, s.shape, 0)
    col = lax.broadcasted_iota(jnp.int32, s.shape, 1)
    return jnp.where((row <= col) if transposed else (col <= row), s, NEG_INF)


def _mxu_dot(a, b, dims):
    return lax.dot_general(a, b, dims, preferred_element_type=F32)


def _fox2_fwd(qa, ka, va, sel, name):
    T = qa.shape[1]
    B = _tile(T, FOX_B)
    n = T // B
    qt, kt = _tri_steps(n, False)

    def body(qt_ref, kt_ref, qa_ref, ka_ref, va_ref, p_ref, o_ref, qb_ref, m_s, acc):
        step = pl.program_id(0)
        qi, ki = qt_ref[step], kt_ref[step]

        @pl.when(ki == 0)
        def _():
            m_s[...] = jnp.full_like(m_s, NEG_INF)
            acc[...] = jnp.zeros_like(acc)

        def update(diag):
            for h in range(4):
                s = _mxu_dot(qa_ref[h], ka_ref[h], _NT)
                if diag:
                    s = _causal(s, False)
                m_old = m_s[h]
                m_new = jnp.maximum(m_old, jnp.max(s, axis=-1, keepdims=True))
                p = jnp.exp(s - m_new)
                acc[h] = acc[h] * jnp.exp(m_old - m_new) + _dot(p, va_ref[h])
                m_s[h] = m_new

        @pl.when(ki < qi)
        def _():
            update(False)

        @pl.when(ki == qi)
        def _():
            update(True)
            out = jnp.zeros((B, CH), F32)
            for h in range(4):
                a = acc[h]
                l = _lane_col(a, AUG_F)
                out = out + _exact_dot(a / l, p_ref[h], _NT, "a")
                lse = m_s[h] + jnp.log(l)
                qb_ref[h] = _put_pieces(qa_ref[h].astype(F32), AUG_LSE, lse, -1.0).astype(qb_ref.dtype)
            o_ref[...] = out

    qspec = pl.BlockSpec((4, B, LANE), lambda s, qt, kt: (0, qt[s], 0))
    kspec = pl.BlockSpec((4, B, LANE), lambda s, qt, kt: (0, kt[s], 0))
    grid_spec = pltpu.PrefetchScalarGridSpec(
        num_scalar_prefetch=2, grid=(qt.shape[0],),
        in_specs=[qspec, kspec, kspec, pl.BlockSpec(sel.shape, lambda s, qt, kt: (0, 0, 0))],
        out_specs=[pl.BlockSpec((B, CH), lambda s, qt, kt: (qt[s], 0)), qspec],
        scratch_shapes=[pltpu.VMEM((4, B, 1), F32), pltpu.VMEM((4, B, LANE), F32)])
    return pl.pallas_call(
        body, name=name, grid_spec=grid_spec,
        out_shape=[jax.ShapeDtypeStruct((T, CH), F32), jax.ShapeDtypeStruct((4, T, LANE), _MXU)],
        compiler_params=_cp("arbitrary"))(qt, kt, qa, ka, va, sel)


def _fox2_bwd_prep(o, do, sel, name):
    T = o.shape[0]
    tm = _tile(T, ROW_TILE)

    def body(o_ref, d_ref, p_ref, out_ref):
        dov = d_ref[...]
        prod = o_ref[...] * dov
        dob = dov.astype(_MXU)
        for h in range(4):
            delta = jnp.sum(jnp.where(_lane_mask(CH, h), prod, 0.0), axis=-1, keepdims=True)
            out_ref[h] = _put_pieces(_dot(dob, p_ref[h]), AUG_F, delta, -1.0).astype(out_ref.dtype)

    row = pl.BlockSpec((tm, CH), lambda i: (i, 0))
    return pl.pallas_call(
        body, name=name, grid=(T // tm,),
        in_specs=[row, row, pl.BlockSpec(sel.shape, lambda i: (0, 0, 0))],
        out_specs=pl.BlockSpec((4, tm, LANE), lambda i: (0, i, 0)),
        out_shape=jax.ShapeDtypeStruct((4, T, LANE), _MXU), compiler_params=_cp("parallel"))(o, do, sel)


def _fox2_bwd_dq(qb, ka, va, doa, sel, name):
    T = qb.shape[1]
    B = _tile(T, FOX_B)
    n = T // B
    qt, kt = _tri_steps(n, False)

    def body(qt_ref, kt_ref, qb_ref, ka_ref, va_ref, do_ref, p_ref, dq_ref, dfq_ref, dq_s):
        step = pl.program_id(0)
        qi, ki = qt_ref[step], kt_ref[step]

        @pl.when(ki == 0)
        def _():
            dq_s[...] = jnp.zeros_like(dq_s)

        def update(diag):
            for h in range(4):
                s = _mxu_dot(qb_ref[h], ka_ref[h], _NT)
                if diag:
                    s = _causal(s, False)
                ds = jnp.exp(s) * _mxu_dot(do_ref[h], va_ref[h], _NT)
                dq_s[h] += _dot(ds, ka_ref[h])

        @pl.when(ki < qi)
        def _():
            update(False)

        @pl.when(ki == qi)
        def _():
            update(True)
            lane = lax.broadcasted_iota(jnp.int32, (1, LANE), 1)
            out = jnp.zeros((B, CH), F32)
            dfq = jnp.zeros((B, LANE), F32)
            for h in range(4):
                out = out + _exact_dot(dq_s[h] * (HEAD ** -0.5), p_ref[h], _NT, "a")
                dfq = jnp.where(lane == h, _lane_col(dq_s[h], AUG_F), dfq)
            dq_ref[...] = out
            dfq_ref[...] = dfq

    qspec = pl.BlockSpec((4, B, LANE), lambda s, qt, kt: (0, qt[s], 0))
    kspec = pl.BlockSpec((4, B, LANE), lambda s, qt, kt: (0, kt[s], 0))
    grid_spec = pltpu.PrefetchScalarGridSpec(
        num_scalar_prefetch=2, grid=(qt.shape[0],),
        in_specs=[qspec, kspec, kspec, qspec, pl.BlockSpec(sel.shape, lambda s, qt, kt: (0, 0, 0))],
        out_specs=[pl.BlockSpec((B, CH), lambda s, qt, kt: (qt[s], 0)),
                   pl.BlockSpec((B, LANE), lambda s, qt, kt: (qt[s], 0))],
        scratch_shapes=[pltpu.VMEM((4, B, LANE), F32)])
    return pl.pallas_call(
        body, name=name, grid_spec=grid_spec,
        out_shape=[jax.ShapeDtypeStruct((T, CH), F32), jax.ShapeDtypeStruct((T, LANE), F32)],
        compiler_params=_cp("arbitrary"))(qt, kt, qb, ka, va, doa, sel)


def _fox2_bwd_dkv(qb, ka, va, doa, sel, name):
    T = qb.shape[1]
    B = _tile(T, FOX_B)
    n = T // B
    qt, kt = _tri_steps(n, True)

    def body(qt_ref, kt_ref, qb_ref, ka_ref, va_ref, do_ref, p_ref, dk_ref, dv_ref, df_ref, dk_s, dv_s):
        step = pl.program_id(0)
        qi, ki = qt_ref[step], kt_ref[step]

        @pl.when(qi == ki)
        def _():
            dk_s[...] = jnp.zeros_like(dk_s)
            dv_s[...] = jnp.zeros_like(dv_s)

        def update(diag):
            for h in range(4):
                st = _mxu_dot(ka_ref[h], qb_ref[h], _NT)
                if diag:
                    st = _causal(st, True)
                pt = jnp.exp(st)
                dst = pt * _mxu_dot(va_ref[h], do_ref[h], _NT)
                dv_s[h] += _dot(pt, do_ref[h])
                dk_s[h] += _dot(dst, qb_ref[h])

        @pl.when(qi == ki)
        def _():
            update(True)

        @pl.when(qi > ki)
        def _():
            update(False)

        @pl.when(qi == n - 1)
        def _():
            lane = lax.broadcasted_iota(jnp.int32, (1, LANE), 1)
            dk = jnp.zeros((B, CH), F32)
            dv = jnp.zeros((B, CH), F32)
            dfk = jnp.zeros((B, LANE), F32)
            for h in range(4):
                dk = dk + _exact_dot(dk_s[h], p_ref[h], _NT, "a")
                dv = dv + _exact_dot(dv_s[h], p_ref[h], _NT, "a")
                dfk = jnp.where(lane == h, -_lane_col(dk_s[h], AUG_ONE), dfk)
            dk_ref[...] = dk
            dv_ref[...] = dv
            df_ref[...] = dfk

    qspec = pl.BlockSpec((4, B, LANE), lambda s, qt, kt: (0, qt[s], 0))
    kspec = pl.BlockSpec((4, B, LANE), lambda s, qt, kt: (0, kt[s], 0))
    ospec = pl.BlockSpec((B, CH), lambda s, qt, kt: (kt[s], 0))
    grid_spec = pltpu.PrefetchScalarGridSpec(
        num_scalar_prefetch=2, grid=(qt.shape[0],),
        in_specs=[qspec, kspec, kspec, qspec, pl.BlockSpec(sel.shape, lambda s, qt, kt: (0, 0, 0))],
        out_specs=[ospec, ospec, pl.BlockSpec((B, LANE), lambda s, qt, kt: (kt[s], 0))],
        scratch_shapes=[pltpu.VMEM((4, B, LANE), F32), pltpu.VMEM((4, B, LANE), F32)])
    return pl.pallas_call(
        body, name=name, grid_spec=grid_spec,
        out_shape=[jax.ShapeDtypeStruct((T, CH), F32), jax.ShapeDtypeStruct((T, CH), F32),
                   jax.ShapeDtypeStruct((T, LANE), F32)],
        compiler_params=_cp("arbitrary"))(qt, kt, qb, ka, va, doa, sel)


def _fox2_post(zf, dqn, dkn, dv, dfq, dfk, gq, gk, bf, name):
    T = zf.shape[0]
    tm = _tile(T, FOX_TM)
    nt = T // tm
    upper = jnp.asarray(_tri(tm, False), _MXU)

    def body(z_ref, dq_ref, dk_ref, dv_ref, dfq_ref, df_ref, gq_ref, gk_ref, bf_ref, u_ref, dz_ref, sm_ref, carry, rc_s):
        @pl.when(pl.program_id(0) == 0)
        def _():
            carry[...] = jnp.zeros_like(carry)
            sm_ref[...] = jnp.zeros_like(sm_ref)

        z = z_ref[...]
        q_raw, k_raw = z[:, :CH], z[:, CH:2 * CH]
        _, q_r = _head_rms(q_raw, gq_ref[...], 4)
        _, k_r = _head_rms(k_raw, gk_ref[...], 4)
        dq, dgq = _head_rms_bwd(dq_ref[...], q_raw, q_r, gq_ref[...], 4)
        dk, dgk = _head_rms_bwd(dk_ref[...], k_raw, k_r, gk_ref[...], 4)
        rc_s[...] = _exact_dot(u_ref[...], dfq_ref[...] + df_ref[...], _NN, "b") + carry[pl.ds(0, 1), :]
        carry[pl.ds(0, 1), :] = rc_s[pl.ds(0, 1), :]
        lane = lax.broadcasted_iota(jnp.int32, (1, LANE), 1)
        x = z[:, 3 * CH:] + bf_ref[...]
        dff = jnp.where(lane < 4, rc_s[...] * _sigmoid(-x), 0.0)
        dz_ref[:, :CH] = dq.astype(dz_ref.dtype)
        dz_ref[:, CH:2 * CH] = dk.astype(dz_ref.dtype)
        dz_ref[:, 2 * CH:3 * CH] = dv_ref[...].astype(dz_ref.dtype)
        dz_ref[:, 3 * CH:] = dff.astype(dz_ref.dtype)
        sm_ref[pl.ds(0, 1), :] += dgq
        sm_ref[pl.ds(1, 1), :] += dgk
        sm_ref[pl.ds(2, 1), :LANE] += jnp.sum(dff, axis=0, keepdims=True)

    rev = lambda i: (nt - 1 - i, 0)
    row = pl.BlockSpec((tm, CH), rev)
    lrow = pl.BlockSpec((tm, LANE), rev)
    vec = pl.BlockSpec((1, CH), lambda i: (0, 0))
    return pl.pallas_call(
        body, name=name, grid=(nt,),
        in_specs=[pl.BlockSpec((tm, 3 * CH + LANE), rev), row, row, row, lrow, lrow, vec, vec,
                  pl.BlockSpec((1, LANE), lambda i: (0, 0)), pl.BlockSpec((tm, tm), lambda i: (0, 0))],
        out_specs=[pl.BlockSpec((tm, 3 * CH + LANE), rev), pl.BlockSpec((8, CH), lambda i: (0, 0))],
        out_shape=[jax.ShapeDtypeStruct((T, 3 * CH + LANE), _MXU), jax.ShapeDtypeStruct((8, CH), F32)],
        scratch_shapes=[pltpu.VMEM((8, LANE), F32), pltpu.VMEM((tm, LANE), F32)],
        compiler_params=_cp("arbitrary"))(zf, dqn, dkn, dv, dfq, dfk, gq, gk, bf, upper)


def _merge_fwd(acts, zg, wbr, wout, x1, name):
    T, D = x1.shape
    tm = _tile(T, 256)

    def body(a0, a1, a2, a3, zg_ref, wbr_ref, wout_ref, x_ref, o_ref, mg_ref):
        merged = None
        for i, a_ref in enumerate((a0, a1, a2, a3)):
            term = _sigmoid(zg_ref[:, i * D:(i + 1) * D]) * _dot(a_ref[...], wbr_ref[i])
            merged = term if merged is None else merged + term
        mg_ref[...] = merged.astype(mg_ref.dtype)
        o_ref[...] = x_ref[...] + _dot(merged, wout_ref[...])

    arow = pl.BlockSpec((tm, CH), lambda i: (i, 0))
    xrow = pl.BlockSpec((tm, D), lambda i: (i, 0))
    return pl.pallas_call(
        body, name=name, grid=(T // tm,),
        in_specs=[arow, arow, arow, arow, pl.BlockSpec((tm, 4 * D), lambda i: (i, 0)),
                  pl.BlockSpec((4, CH, D), lambda i: (0, 0, 0)), pl.BlockSpec((D, D), lambda i: (0, 0)), xrow],
        out_specs=[xrow, xrow],
        out_shape=[jax.ShapeDtypeStruct((T, D), F32), jax.ShapeDtypeStruct((T, D), _MXU)],
        compiler_params=_cp("parallel"))(*acts, zg, wbr, wout, x1)


def _merge_bwd(dx2, acts, zg, wbr, wout, name):
    T, D = dx2.shape
    tm = _tile(T, 256)
    nt = T // tm

    def body(dx_ref, a0, a1, a2, a3, zg_ref, wbr_ref, wout_ref, d0, d1, d2, d3, dzg_ref, dw_ref, dw_s):
        i = pl.program_id(0)

        @pl.when(i == 0)
        def _():
            dw_s[...] = jnp.zeros_like(dw_s)

        dm = _dot(dx_ref[...], wout_ref[...], _NT)
        for b, (a_ref, d_ref) in enumerate(((a0, d0), (a1, d1), (a2, d2), (a3, d3))):
            av = a_ref[...].astype(_MXU)
            g = _sigmoid(zg_ref[:, b * D:(b + 1) * D])
            p = _dot(av, wbr_ref[b])
            dzg_ref[:, b * D:(b + 1) * D] = (dm * p * (g * (1.0 - g))).astype(dzg_ref.dtype)
            dp = (dm * g).astype(_MXU)
            d_ref[...] = _dot(dp, wbr_ref[b], _NT)
            dw_s[b] += _dot(av, dp, _TN)

        @pl.when(i == nt - 1)
        def _():
            dw_ref[...] = dw_s[...].astype(dw_ref.dtype)

    arow = pl.BlockSpec((tm, CH), lambda i: (i, 0))
    xrow = pl.BlockSpec((tm, D), lambda i: (i, 0))
    grow = pl.BlockSpec((tm, 4 * D), lambda i: (i, 0))
    wspec = pl.BlockSpec((4, CH, D), lambda i: (0, 0, 0))
    ash = jax.ShapeDtypeStruct((T, CH), F32)
    return pl.pallas_call(
        body, name=name, grid=(nt,),
        in_specs=[xrow, arow, arow, arow, arow, grow, wspec, pl.BlockSpec((D, D), lambda i: (0, 0))],
        out_specs=[arow, arow, arow, arow, grow, wspec],
        out_shape=[ash, ash, ash, ash, jax.ShapeDtypeStruct((T, 4 * D), _MXU), jax.ShapeDtypeStruct((4, CH, D), _MXU)],
        scratch_shapes=[pltpu.VMEM((4, CH, D), F32)],
        compiler_params=_cp("arbitrary"))(dx2, *acts, zg, wbr, wout)


def _rows_2d(a):
    return a.reshape((-1, a.shape[-1])) if a.ndim > 1 else a.reshape((1, -1))


def _row_tile(rows, cols, n_bufs):
    padded = -(-cols // LANE) * LANE
    cap = max(8, (VMEM_LIMIT // 3) // (2 * n_bufs * 4 * padded))
    return _tile(rows, cap, 8)


def _sum8(recv, name):
    shape = recv.shape[1:]
    r2 = recv.reshape((N_DEV, -1, shape[-1]))
    rows, cols = r2.shape[1:]
    tr = _row_tile(rows, cols, N_DEV // 2 + 1)

    def body(r_ref, o_ref):
        acc = r_ref[0].astype(F32)
        for d in range(1, N_DEV):
            acc = acc + r_ref[d].astype(F32)
        o_ref[...] = acc

    out = pl.pallas_call(
        body, name=name, grid=(rows // tr,),
        in_specs=[pl.BlockSpec((N_DEV, tr, cols), lambda i: (0, i, 0))],
        out_specs=pl.BlockSpec((tr, cols), lambda i: (i, 0)),
        out_shape=jax.ShapeDtypeStruct((rows, cols), F32), compiler_params=_cp("parallel"))(r2)
    return out.reshape(shape)


def _adamw(w, g, m, v, name):
    shape = w.shape
    w2, g2, m2, v2 = (_rows_2d(a) for a in (w, g, m, v))
    rows, cols = w2.shape
    tr = _row_tile(rows, cols, 7)

    def body(w_ref, g_ref, m_ref, v_ref, d_ref, nm_ref, nv_ref):
        gv = g_ref[...]
        nm = ADAM_B1 * m_ref[...] + (1.0 - ADAM_B1) * gv
        nv = ADAM_B2 * v_ref[...] + (1.0 - ADAM_B2) * jnp.square(gv)
        m_hat = nm / (1.0 - ADAM_B1 ** ADAM_STEP)
        v_hat = nv / (1.0 - ADAM_B2 ** ADAM_STEP)
        d_ref[...] = -ADAM_LR * (m_hat / (jnp.sqrt(v_hat) + ADAM_EPS) + ADAM_WD * w_ref[...])
        nm_ref[...] = nm
        nv_ref[...] = nv

    spec = pl.BlockSpec((tr, cols), lambda i: (i, 0))
    osh = jax.ShapeDtypeStruct((rows, cols), F32)
    outs = pl.pallas_call(
        body, name=name, grid=(rows // tr,), in_specs=[spec] * 4, out_specs=[spec] * 3,
        out_shape=[osh] * 3, compiler_params=_cp("parallel"))(w2, g2, m2, v2)
    return tuple(o.reshape(shape) for o in outs)


def _exchange(items, name):
    n = len(items)
    widths, out_shapes = [], []
    for src, kind, ax in items:
        if kind == "gather":
            w = src.shape[ax]
            shp = list(src.shape)
            shp[ax] = N_DEV * w
        else:
            w = src.shape[ax] // N_DEV
            shp = list(src.shape)
            shp[ax] = w
            shp = [N_DEV] + shp
        widths.append(w)
        out_shapes.append(jax.ShapeDtypeStruct(tuple(shp), src.dtype))

    def body(*refs):
        srcs, outs = refs[:n], refs[n:2 * n]
        send, recv, lsem = refs[2 * n:]
        x, y, c = lax.axis_index("x"), lax.axis_index("y"), lax.axis_index("c")
        me = 4 * x + 2 * y + c

        def peer(k):
            b = k + 1
            px = 1 - x if b & 4 else x
            py = 1 - y if b & 2 else y
            pc = 1 - c if b & 1 else c
            return (px, py, pc), 4 * px + 2 * py + pc

        def win(ref, ax, idx, w):
            return ref.at[tuple([slice(None)] * ax + [pl.ds(idx * w, w)])]

        def ends(j, mine, theirs):
            _, kind, ax = items[j]
            if kind == "gather":
                return srcs[j], win(outs[j], ax, mine, widths[j])
            return win(srcs[j], ax, theirs, widths[j]), outs[j].at[mine]

        local, sent = [], []
        for j in range(n):
            s, d = ends(j, me, me)
            cp = pltpu.make_async_copy(s, d, lsem.at[j])
            cp.start()
            local.append(cp)
            for k in range(N_DEV - 1):
                dev, pid = peer(k)
                s, d = ends(j, me, pid)
                cp = pltpu.make_async_remote_copy(s, d, send.at[j, k], recv.at[j, k], device_id=dev,
                                                  device_id_type=pl.DeviceIdType.MESH)
                cp.start()
                sent.append(cp)
        for j in range(n):
            for k in range(N_DEV - 1):
                dev, pid = peer(k)
                s, d = ends(j, pid, me)
                pltpu.make_async_remote_copy(s, d, send.at[j, k], recv.at[j, k], device_id=dev,
                                             device_id_type=pl.DeviceIdType.MESH).wait_recv()
        for cp in sent:
            cp.wait_send()
        for cp in local:
            cp.wait()

    hbm = pl.BlockSpec(memory_space=pl.ANY)
    return pl.pallas_call(
        body, name=name, in_specs=[hbm] * n, out_specs=[hbm] * n, out_shape=out_shapes,
        scratch_shapes=[pltpu.SemaphoreType.DMA((n, N_DEV - 1)), pltpu.SemaphoreType.DMA((n, N_DEV - 1)),
                        pltpu.SemaphoreType.DMA((n,))],
        compiler_params=pltpu.CompilerParams(has_side_effects=True))(*[it[0] for it in items])


def _exchange_plan(items):
    widths, out_shapes = [], []
    for src, kind, ax in items:
        shp = list(src.shape)
        if kind == "gather":
            w = src.shape[ax]
            shp[ax] = N_DEV * w
        else:
            w = src.shape[ax] // N_DEV
            shp[ax] = w
            shp = [N_DEV] + shp
        widths.append(w)
        out_shapes.append((tuple(shp), src.dtype))
    return widths, out_shapes


def _exchange_refs(items, widths, srcs, outs):
    x, y, c = lax.axis_index("x"), lax.axis_index("y"), lax.axis_index("c")
    me = 4 * x + 2 * y + c

    def peer(k):
        b = k + 1
        px = 1 - x if b & 4 else x
        py = 1 - y if b & 2 else y
        pc = 1 - c if b & 1 else c
        return (px, py, pc), 4 * px + 2 * py + pc

    def win(ref, ax, idx, w):
        return ref.at[tuple([slice(None)] * ax + [pl.ds(idx * w, w)])]

    def ends(j, mine, theirs):
        _, kind, ax = items[j]
        if kind == "gather":
            return srcs[j], win(outs[j], ax, mine, widths[j])
        return win(srcs[j], ax, theirs, widths[j]), outs[j].at[mine]

    return me, peer, ends


_HBM = pl.BlockSpec(memory_space=pltpu.HBM)
_SEM = pl.BlockSpec(memory_space=pltpu.SEMAPHORE)


def _exchange_start(items, name):
    n = len(items)
    widths, out_shapes = _exchange_plan(items)
    meta = [(None, kind, ax) for _, kind, ax in items]

    def body(*refs):
        srcs, lands = refs[:n], refs[n:2 * n]
        send, recv = refs[2 * n], refs[2 * n + 1]
        token = refs[-1]
        me, peer, ends = _exchange_refs(meta, widths, srcs, lands)
        for j in range(n):
            for k in range(N_DEV - 1):
                dev, pid = peer(k)
                s, d = ends(j, me, pid)
                q = j * (N_DEV - 1) + k
                pltpu.make_async_remote_copy(s, d, send.at[q], recv.at[q], device_id=dev,
                                             device_id_type=pl.DeviceIdType.MESH).start()
        token[...] = jnp.zeros_like(token)

    srcs = [pltpu.with_memory_space_constraint(it[0], pltpu.HBM) for it in items]
    lands = [pltpu.with_memory_space_constraint(lax.empty(shp, dt), pltpu.HBM) for shp, dt in out_shapes]
    outs = pl.pallas_call(
        body, name=name,
        out_shape=(pltpu.SemaphoreType.DMA((n * (N_DEV - 1),)), pltpu.SemaphoreType.DMA((n * (N_DEV - 1),)),
                   *[pltpu.HBM(s.shape, s.dtype) for s in srcs], *[pltpu.HBM(shp, dt) for shp, dt in out_shapes],
                   jax.ShapeDtypeStruct((8, LANE), F32)),
        in_specs=[_HBM] * (2 * n),
        out_specs=(_SEM, _SEM, *([_HBM] * (2 * n)), pl.BlockSpec(memory_space=pltpu.VMEM)),
        input_output_aliases={i: 2 + i for i in range(2 * n)},
        compiler_params=pltpu.CompilerParams(has_side_effects=pltpu.SideEffectType.DATAFLOW_SIDE_EFFECTING),
    )(*srcs, *lands)
    handle = (meta, widths, outs[0], outs[1], outs[2:2 + n], outs[2 + n:2 + 2 * n])
    return handle, outs[-1]


def _exchange_wait(handle, after, name):
    meta, widths, send_sem, recv_sem, src_thru, land_thru = handle
    n = len(meta)

    def body(*refs):
        srcs, lands = refs[:n], refs[n:2 * n]
        send, recv = refs[2 * n], refs[2 * n + 1]
        lsem = refs[-1]
        me, peer, ends = _exchange_refs(meta, widths, srcs, lands)
        _, _, out_ends = _exchange_refs(meta, widths, srcs, refs[3 * n + 3:4 * n + 3])
        local = []
        for j in range(n):
            s, d = out_ends(j, me, me)
            cp = pltpu.make_async_copy(s, d, lsem.at[j])
            cp.start()
            local.append(cp)
        for j in range(n):
            for k in range(N_DEV - 1):
                dev, pid = peer(k)
                q = j * (N_DEV - 1) + k
                s, d = ends(j, me, pid)
                pltpu.make_async_remote_copy(s, d, send.at[q], recv.at[q], device_id=dev,
                                             device_id_type=pl.DeviceIdType.MESH).wait_send()
                s, d = ends(j, pid, me)
                pltpu.make_async_remote_copy(s, d, send.at[q], recv.at[q], device_id=dev,
                                             device_id_type=pl.DeviceIdType.MESH).wait_recv()
        for cp in local:
            cp.wait()

    outs = pl.pallas_call(
        body, name=name,
        out_shape=tuple(pltpu.HBM(a.shape, a.dtype) for a in (*src_thru, *land_thru)),
        in_specs=[_HBM] * (2 * n) + [_SEM, _SEM, pl.BlockSpec(memory_space=pl.ANY)],
        out_specs=tuple([_HBM] * (2 * n)),
        input_output_aliases={i: i for i in range(2 * n)},
        scratch_shapes=[pltpu.SemaphoreType.DMA((n,))],
        compiler_params=pltpu.CompilerParams(has_side_effects=pltpu.SideEffectType.DATAFLOW_SIDE_EFFECTING),
    )(*src_thru, *land_thru, send_sem, recv_sem, after)
    return list(outs[n:])


def _pack(arrs):
    flat = jnp.concatenate([a.reshape(-1).astype(F32) for a in arrs])
    n = flat.shape[0]
    rows = -(-n // (8 * LANE)) * 8
    return jnp.pad(flat, (0, rows * LANE - n)).reshape(rows, LANE)


def _unpack(buf, shapes):
    flat = buf.reshape(-1)
    out, off = [], 0
    for s in shapes:
        sz = int(np.prod(s))
        out.append(flat[off:off + sz].reshape(s))
        off += sz
    return out


def _pad_axis(a, axis, size):
    pad = [(0, 0)] * a.ndim
    pad[axis] = (0, size - a.shape[axis])
    return jnp.pad(a, pad)


def _ffn_forward(x, g, wg, wu, wd, tag):
    a = _rms_fwd(x, g, f"{tag}_rms")
    gate, up, hid = _ffn_up(a, wg, wu, f"{tag}_up")
    out = _mm([(hid, wd)], "nn", F32, f"{tag}_down", scale=0.5, res=x)
    return out, (x, a, gate, up, hid)


def _ffn_backward(dxp, saved, g, wg, wu, wd, tag):
    x, a, gate, up, hid = saved
    d_gate, d_up = _ffn_bwd_hid(dxp, wd, gate, up, f"{tag}_bwd_hid")
    d_wd = _mm([(hid, dxp)], "tn", _MXU, f"{tag}_dwd", scale=0.5, tk=2048)
    d_wg = _mm([(a, d_gate)], "tn", _MXU, f"{tag}_dwg", tk=2048)
    d_wu = _mm([(a, d_up)], "tn", _MXU, f"{tag}_dwu", tk=2048)
    d_a = _mm([(d_gate, wg), (d_up, wu)], "nt", F32, f"{tag}_da")
    dx, dg = _rms_bwd(d_a, x, g, dxp, f"{tag}_rms_bwd")
    return dx, dg, d_wg, d_wu, d_wd


def _tile_vec(v, reps):
    return jnp.tile(v.reshape(1, -1), (1, reps))


def _mixer_forward(x1, p, consts, tag):
    h = _rms_fwd(x1, p["mix_norm"], f"{tag}_rms")
    zg = _mm([(h, p["w_zg"])], "nn", F32, f"{tag}_in_g")
    zc = _mm([(h, p["w_conf"])], "nn", F32, f"{tag}_in_c")
    zs = _mm([(h, p["w_sc"])], "nn", F32, f"{tag}_in_s")
    zw = _mm([(h, p["w_swa"])], "nn", F32, f"{tag}_in_w")
    zf = _mm([(h, p["w_fox"])], "nn", F32, f"{tag}_in_f")
    u1, act_c = _conf_fwd(zc, p["conf_dw"], p["conf_dw_b"], p["conf_ln_g"], p["conf_ln_b"], f"{tag}_conf")
    act_s = _sc_fwd(zs, p["sc_conv"], f"{tag}_sc")
    act_w = _swa_fwd(zw, p["swa_q_norm"], p["swa_k_norm"], p["swa_sink"], consts["bias"], consts["expand"], f"{tag}_swa")
    qa, ka, va = _fox2_prep(zf, p["fox_q_norm"], p["fox_k_norm"], p["b_forget"], consts["sel"], f"{tag}_fox_prep")
    act_f, qb = _fox2_fwd(qa, ka, va, consts["sel"], f"{tag}_fox")
    acts = (act_c, act_s, act_w, act_f)
    x2, merged = _merge_fwd(acts, zg, p["w_br"], p["w_out"], x1, f"{tag}_merge")
    saved = (x1, h, zg, zc, zs, zw, zf, u1, acts, qb, ka, va, merged)
    return x2, saved


def _mixer_backward(dx2, saved, p, consts, tag):
    x1, h, zg, zc, zs, zw, zf, u1, acts, qb, ka, va, merged = saved
    g = {}
    g["w_out"] = _mm([(merged, dx2)], "tn", _MXU, f"{tag}_dwout", tk=2048)
    d_c, d_s, d_w, d_f, dzg, g["w_br"] = _merge_bwd(dx2, acts, zg, p["w_br"], p["w_out"], f"{tag}_merge_bwd")
    du1, sm_c = _conf_bwd_ln(d_c, u1, p["conf_ln_g"], p["conf_ln_b"], f"{tag}_conf_bwd_ln")
    dzc, g["conf_dw"] = _conf_bwd_conv(zc, du1, p["conf_dw"], f"{tag}_conf_bwd_conv")
    g["conf_ln_g"], g["conf_ln_b"], g["conf_dw_b"] = sm_c[0], sm_c[1], sm_c[2]
    dzs, g["sc_conv"] = _sc_bwd(zs, d_s, p["sc_conv"], f"{tag}_sc_bwd")
    dzw, dgq, dgk, g["swa_sink"], g["rel_bias"] = _swa_bwd(
        zw, d_w, p["swa_q_norm"], p["swa_k_norm"], p["swa_sink"], consts["bias"], consts["bucket"], consts["expand"],
        f"{tag}_swa_bwd")
    g["swa_q_norm"], g["swa_k_norm"] = dgq, dgk
    doa = _fox2_bwd_prep(acts[3], d_f, consts["sel"], f"{tag}_fox_bwd_prep")
    dqn, dfq = _fox2_bwd_dq(qb, ka, va, doa, consts["sel"], f"{tag}_fox_bwd_dq")
    dkn, dv, dfk = _fox2_bwd_dkv(qb, ka, va, doa, consts["sel"], f"{tag}_fox_bwd_dkv")
    dzf, sm_f = _fox2_post(zf, dqn, dkn, dv, dfq, dfk, p["fox_q_norm"], p["fox_k_norm"], p["b_forget"], f"{tag}_fox_post")
    g["fox_q_norm"], g["fox_k_norm"], g["b_forget"] = sm_f[0], sm_f[1], sm_f[2]
    parts = ((dzg, "w_zg"), (dzc, "w_conf"), (dzs, "w_sc"), (dzw, "w_swa"), (dzf, "w_fox"))
    for dz, wname in parts:
        g[wname] = _mm([(h, dz)], "tn", _MXU, f"{tag}_d{wname}", tk=2048)
    dh = _mm([(dz, p[wname]) for dz, wname in parts], "nt", F32, f"{tag}_dh", tm=512)
    dx1, g["mix_norm"] = _rms_bwd(dh, x1, p["mix_norm"], dx2, f"{tag}_rms_bwd")
    return dx1, g


W_NAMES = ['rel_bias', 'ffn1_norm', 'ffn1_w_gate', 'ffn1_w_up', 'ffn1_w_down', 'mix_norm', 'w_in', 'b_forget', 'conf_dw',
           'conf_dw_b', 'conf_ln_g', 'conf_ln_b', 'conf_w_out', 'sc_conv', 'sc_w_out', 'swa_q_norm', 'swa_k_norm',
           'swa_sink', 'swa_w_o', 'fox_q_norm', 'fox_k_norm', 'fox_w_o', 'w_out', 'ffn2_norm', 'ffn2_w_gate',
           'ffn2_w_up', 'ffn2_w_down']
SMALL = ['rel_bias', 'ffn1_norm', 'mix_norm', 'b_forget', 'conf_dw', 'conf_dw_b', 'conf_ln_g', 'conf_ln_b', 'sc_conv',
         'swa_q_norm', 'swa_k_norm', 'swa_sink', 'fox_q_norm', 'fox_k_norm', 'ffn2_norm']
BRANCH_W = ['conf_w_out', 'sc_w_out', 'swa_w_o', 'fox_w_o']
IN_CONF, IN_SC, IN_SWA, IN_FOX, IN_FF = (0, 512), (512, 1280), (1280, 1792), (1792, 2560), (2560, 2564)


def _step(w, m, v, x, loss_target):
    T, D = x.shape
    L = w["w_out"].shape[0]
    fs = w["ffn1_w_gate"].shape[2]
    fsp = -(-fs // LANE) * LANE
    dev = 4 * lax.axis_index("x") + 2 * lax.axis_index("y") + lax.axis_index("c")

    def cast(a):
        return a.astype(_MXU)

    win = w["w_in"]
    fox_cols = jnp.concatenate([win[..., IN_FOX[0]:IN_FF[1]],
                                jnp.zeros(win.shape[:2] + (LANE - (IN_FF[1] - IN_FF[0]),), win.dtype)], axis=-1)
    shards = {
        "ffn1_w_gate": (cast(_pad_axis(w["ffn1_w_gate"], 2, fsp)), 2),
        "ffn1_w_up": (cast(_pad_axis(w["ffn1_w_up"], 2, fsp)), 2),
        "ffn1_w_down": (cast(_pad_axis(w["ffn1_w_down"], 1, fsp)), 1),
        "ffn2_w_gate": (cast(_pad_axis(w["ffn2_w_gate"], 2, fsp)), 2),
        "ffn2_w_up": (cast(_pad_axis(w["ffn2_w_up"], 2, fsp)), 2),
        "ffn2_w_down": (cast(_pad_axis(w["ffn2_w_down"], 1, fsp)), 1),
        "w_zg": (cast(win[..., IN_FF[1]:]), 1),
        "w_conf": (cast(win[..., IN_CONF[0]:IN_CONF[1]]), 1),
        "w_sc": (cast(win[..., IN_SC[0]:IN_SC[1]]), 1),
        "w_swa": (cast(win[..., IN_SWA[0]:IN_SWA[1]]), 1),
        "w_fox": (cast(fox_cols), 1),
        "w_out": (cast(w["w_out"]), 1),
        "w_br": (cast(jnp.stack([w[n] for n in BRANCH_W], axis=1)), 3),
    }
    big = list(shards)
    conv_shard = jnp.concatenate([jnp.swapaxes(w["conf_dw"], 1, 2), jnp.swapaxes(w["sc_conv"], 1, 2)], axis=2)
    conv_full = jnp.swapaxes(_exchange([(conv_shard, "gather", 1)], "gather_conv")[0], 1, 2)
    conf_dw_full = _pad_axis(conv_full[:, :CONV_K], 1, CONV_HALO)
    sc_conv_full = _pad_axis(conv_full[:, CONV_K:], 1, SC_HALO)

    stages = (("ffn1", ["ffn1_w_gate", "ffn1_w_up", "ffn1_w_down"]),
              ("mix", ["w_zg", "w_conf", "w_sc", "w_swa", "w_fox", "w_out", "w_br"]),
              ("ffn2", ["ffn2_w_gate", "ffn2_w_up", "ffn2_w_down"]))
    first, rest = stages[0][1], stages[1][1] + stages[2][1]
    groups = [(l, st, names) for l in range(L) for st, names in stages]

    def depart(gi, dep):
        l, st, names = groups[gi]
        items = [(shards[n][0][l], "gather", shards[n][1] - 1) for n in names]
        if dep is not None:
            src0 = items[0][0]
            zero = (dep[(0,) * dep.ndim].astype(F32) * 0.0).astype(src0.dtype)
            items[0] = (src0 + zero,) + items[0][1:]
        return _exchange_start(items, f"gather_start_l{l}_{st}")

    bucket = jnp.asarray(_swa_bucket_matrix(min(SWA_TQ, T)))
    consts = {"bucket": bucket, "expand": jnp.asarray(_kv_expand_matrix(), _MXU),
              "sel": jnp.asarray(_head_select_matrix(), _MXU),
              "bias": _swa_bias(w["rel_bias"], bucket, "swa_bias")}

    def layer_params(l):
        p = {}
        for n in ("ffn1_norm", "mix_norm", "ffn2_norm", "conf_dw_b", "conf_ln_g", "conf_ln_b"):
            p[n] = w[n][l].reshape(1, -1)
        p["conf_dw"], p["sc_conv"] = conf_dw_full[l], sc_conv_full[l]
        p["swa_q_norm"], p["fox_q_norm"] = _tile_vec(w["swa_q_norm"][l], 4), _tile_vec(w["fox_q_norm"][l], 4)
        p["swa_k_norm"], p["fox_k_norm"] = _tile_vec(w["swa_k_norm"][l], 2), _tile_vec(w["fox_k_norm"][l], 4)
        p["swa_sink"] = w["swa_sink"][l].reshape(1, 4)
        p["b_forget"] = _pad_axis(w["b_forget"][l].reshape(1, 4), 1, LANE)
        return p

    params = [layer_params(l) for l in range(L)]
    saved = [[] for _ in range(L)]
    cur = x
    handle, tok = depart(0, None)
    for gi, (l, st, names) in enumerate(groups):
        p = params[l]
        got = _exchange_wait(handle, cur if gi else tok, f"gather_wait_l{l}_{st}")
        p.update(zip(names, got))
        zero = 0.0
        if gi + 1 < len(groups):
            handle, tok = depart(gi + 1, got[0])
            zero = tok[0:1, 0:1]
        if st == "ffn1":
            cur, s = _ffn_forward(cur, p["ffn1_norm"] + zero, p["ffn1_w_gate"], p["ffn1_w_up"], p["ffn1_w_down"], f"l{l}_ffn1")
        elif st == "mix":
            cur, s = _mixer_forward(cur, dict(p, mix_norm=p["mix_norm"] + zero), consts, f"l{l}_mix")
        else:
            cur, s = _ffn_forward(cur, p["ffn2_norm"] + zero, p["ffn2_w_gate"], p["ffn2_w_up"], p["ffn2_w_down"], f"l{l}_ffn2")
        saved[l].append(s)
    dcur, loss_part = _loss_grad(cur, loss_target)

    grads = [None] * L
    leaving = []

    def leave(l, st, g):
        names = dict(stages)[st]
        h, tok = _exchange_start([(g[n], "scatter", shards[n][1] - 1) for n in names], f"scatter_start_l{l}_{st}")
        leaving.append((l, st, names, h))
        return tok[0:1, 0:1]

    zero = 0.0
    for l in reversed(range(L)):
        p = params[l]
        s1, s2, s3 = saved[l]
        g = {}
        dcur, g["ffn2_norm"], g["ffn2_w_gate"], g["ffn2_w_up"], g["ffn2_w_down"] = _ffn_backward(
            dcur, s3, p["ffn2_norm"] + zero, p["ffn2_w_gate"], p["ffn2_w_up"], p["ffn2_w_down"], f"l{l}_ffn2")
        zero = leave(l, "ffn2", g)
        dcur, gm = _mixer_backward(dcur, s2, dict(p, mix_norm=p["mix_norm"] + zero), consts, f"l{l}_mix")
        g.update(gm)
        zero = leave(l, "mix", g)
        dcur, g["ffn1_norm"], g["ffn1_w_gate"], g["ffn1_w_up"], g["ffn1_w_down"] = _ffn_backward(
            dcur, s1, p["ffn1_norm"] + zero, p["ffn1_w_gate"], p["ffn1_w_up"], p["ffn1_w_down"], f"l{l}_ffn1")
        zero = leave(l, "ffn1", g)
        grads[l] = g
    grad_x = dcur

    gsum = {n: [None] * L for n in big}
    for l, st, names, h in leaving:
        for n, r in zip(names, _exchange_wait(h, grad_x, f"scatter_wait_l{l}_{st}")):
            gsum[n][l] = _sum8(r, f"sum_{n}_l{l}")
    gsum = {n: jnp.stack(parts) for n, parts in gsum.items()}
    gw = {}
    for n in ("ffn1_w_gate", "ffn1_w_up", "ffn2_w_gate", "ffn2_w_up"):
        gw[n] = gsum[n][:, :, :fs]
    for n in ("ffn1_w_down", "ffn2_w_down"):
        gw[n] = gsum[n][:, :fs, :]
    gw["w_out"] = gsum["w_out"]
    for i, n in enumerate(BRANCH_W):
        gw[n] = gsum["w_br"][:, i]
    gw["w_in"] = jnp.concatenate([gsum["w_conf"], gsum["w_sc"], gsum["w_swa"],
                                  gsum["w_fox"][..., :IN_FF[1] - IN_FOX[0]], gsum["w_zg"]], axis=-1)

    def small_partial(n):
        per_layer = [grads[l][n] for l in range(L)]
        if n == "rel_bias":
            return sum(pl_[:, :4] for pl_ in per_layer)
        if n in ("swa_sink", "b_forget"):
            return jnp.stack([a.reshape(-1)[:4] for a in per_layer])
        if n in ("swa_q_norm", "fox_q_norm", "fox_k_norm"):
            return jnp.stack([a.reshape(4, HEAD).sum(0) for a in per_layer])
        if n == "swa_k_norm":
            return jnp.stack([a.reshape(2, HEAD).sum(0) for a in per_layer])
        if n == "conf_dw":
            return jnp.stack([a[:CONV_K] for a in per_layer])
        if n == "sc_conv":
            return jnp.stack([a[:SC_K] for a in per_layer])
        return jnp.stack([a.reshape(-1) for a in per_layer])

    partial = [small_partial(n) for n in SMALL]
    small_shapes = [a.shape for a in partial]
    all_parts = _exchange([(_pack(partial), "gather", 0)], "gather_small_grads")[0]
    rows = all_parts.shape[0] // N_DEV
    small_sum = _unpack(_sum8(all_parts.reshape(N_DEV, rows, LANE), "sum_small"), small_shapes)
    for n, a in zip(SMALL, small_sum):
        if n in ("conf_dw", "sc_conv"):
            cs = w[n].shape[2]
            a = lax.dynamic_slice_in_dim(a, dev * cs, cs, axis=2)
        gw[n] = a

    delta, new_m, new_v = {}, {}, {}
    for n in W_NAMES:
        if n not in SMALL:
            delta[n], new_m[n], new_v[n] = _adamw(w[n], gw[n], m[n], v[n], f"adamw_{n}")
    shapes = [w[n].shape for n in SMALL]
    outs = _adamw(_pack([w[n] for n in SMALL]), _pack([gw[n] for n in SMALL]), _pack([m[n] for n in SMALL]),
                  _pack([v[n] for n in SMALL]), "adamw_small")
    for res, out in zip((delta, new_m, new_v), outs):
        for n, a in zip(SMALL, _unpack(out, shapes)):
            res[n] = a

    loss = lax.psum(loss_part[0, 0], ("x", "y", "c"))
    return loss, grad_x, gw, delta, new_m, new_v


def kernel(x, rel_bias, ffn1_norm, ffn1_w_gate, ffn1_w_up, ffn1_w_down, mix_norm, w_in, b_forget, conf_dw, conf_dw_b, conf_ln_g, conf_ln_b, conf_w_out, sc_conv, sc_w_out, swa_q_norm, swa_k_norm, swa_sink, swa_w_o, fox_q_norm, fox_k_norm, fox_w_o, w_out, ffn2_norm, ffn2_w_gate, ffn2_w_up, ffn2_w_down, loss_target, m_rel_bias, m_ffn1_norm, m_ffn1_w_gate, m_ffn1_w_up, m_ffn1_w_down, m_mix_norm, m_w_in, m_b_forget, m_conf_dw, m_conf_dw_b, m_conf_ln_g, m_conf_ln_b, m_conf_w_out, m_sc_conv, m_sc_w_out, m_swa_q_norm, m_swa_k_norm, m_swa_sink, m_swa_w_o, m_fox_q_norm, m_fox_k_norm, m_fox_w_o, m_w_out, m_ffn2_norm, m_ffn2_w_gate, m_ffn2_w_up, m_ffn2_w_down, v_rel_bias, v_ffn1_norm, v_ffn1_w_gate, v_ffn1_w_up, v_ffn1_w_down, v_mix_norm, v_w_in, v_b_forget, v_conf_dw, v_conf_dw_b, v_conf_ln_g, v_conf_ln_b, v_conf_w_out, v_sc_conv, v_sc_w_out, v_swa_q_norm, v_swa_k_norm, v_swa_sink, v_swa_w_o, v_fox_q_norm, v_fox_k_norm, v_fox_w_o, v_w_out, v_ffn2_norm, v_ffn2_w_gate, v_ffn2_w_up, v_ffn2_w_down):
    args = locals()
    w = {n: args[n] for n in W_NAMES}
    m = {n: args["m_" + n] for n in W_NAMES}
    v = {n: args["v_" + n] for n in W_NAMES}
    T, D = x.shape[-2:]
    loss, grad_x, gw, delta, new_m, new_v = _step(w, m, v, x.reshape(T, D), loss_target.reshape(T, D))
    return (loss, grad_x.reshape(x.shape), *[gw[n] for n in W_NAMES], *[delta[n] for n in W_NAMES],
            *[new_m[n] for n in W_NAMES], *[new_v[n] for n in W_NAMES])
```

```python
import math

import numpy as np
import jax
import jax.numpy as jnp
from jax import lax
from jax.experimental import pallas as pl
from jax.experimental.pallas import tpu as pltpu

F32 = jnp.float32
_MXU = jnp.bfloat16
EPS = 1e-6
NEG_INF = -1e30
HEAD = 64
CH = 256
WINDOW = 128
CONV_K = 31
SC_K = 3
CONV_HALO = 32
SC_HALO = 8
N_BUCKETS = 32
MAX_DISTANCE = 128
N_DEV = 8
LANE = 128
ROW_TILE = 512
VMEM_LIMIT = 48 * 1024 * 1024
ADAM_LR, ADAM_B1, ADAM_B2, ADAM_EPS, ADAM_WD, ADAM_STEP = 0.001, 0.9, 0.999, 1e-08, 0.01, 10

_NN = (((1,), (0,)), ((), ()))
_NT = (((1,), (1,)), ((), ()))
_TN = (((0,), (0,)), ((), ()))


def _cp(*sem):
    return pltpu.CompilerParams(dimension_semantics=sem, vmem_limit_bytes=VMEM_LIMIT)


def _tile(n, pref, align=LANE):
    t = (min(n, pref) // align) * align
    while t >= align:
        if n % t == 0:
            return t
        t -= align
    return n


def _dot(a, b, dims=_NN):
    return lax.dot_general(a.astype(_MXU), b.astype(_MXU), dims, preferred_element_type=F32)


def _split3(x):
    hi = x.astype(_MXU)
    r1 = x - hi.astype(F32)
    mid = r1.astype(_MXU)
    lo = (r1 - mid.astype(F32)).astype(_MXU)
    return hi, mid, lo


def _exact_dot(a, b, dims, data):
    if data == "a":
        return sum(lax.dot_general(p, b.astype(_MXU), dims, preferred_element_type=F32) for p in _split3(a))
    return sum(lax.dot_general(a.astype(_MXU), p, dims, preferred_element_type=F32) for p in _split3(b))


def _sigmoid(x):
    return jax.nn.sigmoid(x)


def _lane_mask(width, h):
    lane = lax.broadcasted_iota(jnp.int32, (1, width), 1)
    return (lane >= h * HEAD) & (lane < (h + 1) * HEAD)


def _head_rms(x, g, nh):
    xx = x * x
    ms = jnp.zeros_like(x)
    for h in range(nh):
        mk = _lane_mask(x.shape[-1], h)
        s = jnp.sum(jnp.where(mk, xx, 0.0), axis=-1, keepdims=True) * (1.0 / HEAD)
        ms = jnp.where(mk, s, ms)
    r = lax.rsqrt(ms + EPS)
    return x * r * g, r


def _head_rms_bwd(dy, x, r, g, nh):
    w = dy * g
    wx = w * x
    c = jnp.zeros_like(x)
    for h in range(nh):
        mk = _lane_mask(x.shape[-1], h)
        s = jnp.sum(jnp.where(mk, wx, 0.0), axis=-1, keepdims=True) * (1.0 / HEAD)
        c = jnp.where(mk, s, c)
    dx = r * w - x * (r * r * r) * c
    dg = jnp.sum(dy * x * r, axis=0, keepdims=True)
    return dx, dg


def _mm(pairs, mode, out_dtype, name, scale=None, res=None, tm=1024, tn=1024, tk=1024):
    a0, b0 = pairs[0]
    M = a0.shape[1] if mode == "tn" else a0.shape[0]
    N = b0.shape[0] if mode == "nt" else b0.shape[1]
    tm, tn = _tile(M, tm), _tile(N, tn)
    dims = {"nn": _NN, "nt": _NT, "tn": _TN}[mode]
    tks, nks, offs = [], [], []
    for a, _ in pairs:
        K = a.shape[0] if mode == "tn" else a.shape[1]
        t = _tile(K, tk)
        tks.append(t)
        nks.append(K // t)
        offs.append(sum(nks[:-1]))
    nk_tot = sum(nks)
    in_specs, args = [], []
    for (a, b), t, nk, off in zip(pairs, tks, nks, offs):
        def kk(k, off=off, nk=nk):
            return jnp.clip(k - off, 0, nk - 1)
        if mode == "tn":
            in_specs.append(pl.BlockSpec((t, tm), lambda i, j, k, kk=kk: (kk(k), i)))
        else:
            in_specs.append(pl.BlockSpec((tm, t), lambda i, j, k, kk=kk: (i, kk(k))))
        if mode == "nt":
            in_specs.append(pl.BlockSpec((tn, t), lambda i, j, k, kk=kk: (j, kk(k))))
        else:
            in_specs.append(pl.BlockSpec((t, tn), lambda i, j, k, kk=kk: (kk(k), j)))
        args += [a, b]
    if res is not None:
        in_specs.append(pl.BlockSpec((tm, tn), lambda i, j, k: (i, j)))
        args.append(res)
    npairs = len(pairs)

    def body(*refs):
        ab = refs[:2 * npairs]
        res_ref = refs[2 * npairs] if res is not None else None
        o_ref = refs[2 * npairs + (res is not None)]
        acc = refs[-1]
        k = pl.program_id(2)

        def finish(r):
            if scale is not None:
                r = r * scale
            if res_ref is not None:
                r = r + res_ref[...]
            o_ref[...] = r.astype(o_ref.dtype)

        if nk_tot == 1:
            finish(_dot(ab[0][...], ab[1][...], dims))
            return

        @pl.when(k == 0)
        def _():
            acc[...] = jnp.zeros_like(acc)

        for p in range(npairs):
            @pl.when(jnp.logical_and(k >= offs[p], k < offs[p] + nks[p]))
            def _(p=p):
                acc[...] += _dot(ab[2 * p][...], ab[2 * p + 1][...], dims)

        @pl.when(k == nk_tot - 1)
        def _():
            finish(acc[...])

    return pl.pallas_call(
        body, name=name, grid=(M // tm, N // tn, nk_tot), in_specs=in_specs,
        out_specs=pl.BlockSpec((tm, tn), lambda i, j, k: (i, j)),
        out_shape=jax.ShapeDtypeStruct((M, N), out_dtype),
        scratch_shapes=[pltpu.VMEM((tm, tn), F32)],
        compiler_params=_cp("parallel", "parallel", "arbitrary"))(*args)


def _rms_fwd(x, g, name):
    T, D = x.shape
    tm = _tile(T, ROW_TILE)

    def body(x_ref, g_ref, o_ref):
        xv = x_ref[...]
        r = lax.rsqrt(jnp.mean(xv * xv, axis=-1, keepdims=True) + EPS)
        o_ref[...] = (xv * r * g_ref[...]).astype(o_ref.dtype)

    return pl.pallas_call(
        body, name=name, grid=(T // tm,),
        in_specs=[pl.BlockSpec((tm, D), lambda i: (i, 0)), pl.BlockSpec((1, D), lambda i: (0, 0))],
        out_specs=pl.BlockSpec((tm, D), lambda i: (i, 0)),
        out_shape=jax.ShapeDtypeStruct((T, D), _MXU), compiler_params=_cp("parallel"))(x, g)


def _rms_bwd(da, x, g, dres, name):
    T, D = x.shape
    tm = _tile(T, ROW_TILE)

    def body(da_ref, x_ref, g_ref, dr_ref, dx_ref, dg_ref):
        @pl.when(pl.program_id(0) == 0)
        def _():
            dg_ref[...] = jnp.zeros_like(dg_ref)

        xv, dav = x_ref[...], da_ref[...]
        r = lax.rsqrt(jnp.mean(xv * xv, axis=-1, keepdims=True) + EPS)
        w = dav * g_ref[...]
        c = jnp.mean(w * xv, axis=-1, keepdims=True)
        dx_ref[...] = dr_ref[...] + (r * w - xv * (r * r * r) * c)
        dg_ref[...] += jnp.sum(dav * xv * r, axis=0, keepdims=True)

    row = pl.BlockSpec((tm, D), lambda i: (i, 0))
    vec = pl.BlockSpec((1, D), lambda i: (0, 0))
    return pl.pallas_call(
        body, name=name, grid=(T // tm,), in_specs=[row, row, vec, row], out_specs=[row, vec],
        out_shape=[jax.ShapeDtypeStruct((T, D), F32), jax.ShapeDtypeStruct((1, D), F32)],
        compiler_params=_cp("arbitrary"))(da, x, g, dres)


def _loss_grad(y, tgt):
    T, D = y.shape
    tm = _tile(T, ROW_TILE)

    def body(y_ref, t_ref, dy_ref, l_ref):
        @pl.when(pl.program_id(0) == 0)
        def _():
            l_ref[...] = jnp.zeros_like(l_ref)

        d = y_ref[...] - t_ref[...]
        dy_ref[...] = d * (1.0 / D)
        per_tok = jnp.mean(d * d, axis=-1, keepdims=True)
        l_ref[...] += 0.5 * jnp.sum(per_tok, axis=0, keepdims=True)

    row = pl.BlockSpec((tm, D), lambda i: (i, 0))
    return pl.pallas_call(
        body, name="loss_grad", grid=(T // tm,), in_specs=[row, row],
        out_specs=[row, pl.BlockSpec((1, 1), lambda i: (0, 0))],
        out_shape=[jax.ShapeDtypeStruct((T, D), F32), jax.ShapeDtypeStruct((1, 1), F32)],
        compiler_params=_cp("arbitrary"))(y, tgt)


def _ffn_up(a, wg, wu, name):
    T, D = a.shape
    Fp = wg.shape[1]
    tm, tn = _tile(T, ROW_TILE), _tile(Fp, 768)

    def body(a_ref, wg_ref, wu_ref, g_ref, u_ref, h_ref):
        av = a_ref[...]
        g = _dot(av, wg_ref[...])
        u = _dot(av, wu_ref[...])
        g_ref[...] = g.astype(g_ref.dtype)
        u_ref[...] = u.astype(u_ref.dtype)
        h_ref[...] = (g * _sigmoid(g) * u).astype(h_ref.dtype)

    wspec = pl.BlockSpec((D, tn), lambda j, i: (0, j))
    ospec = pl.BlockSpec((tm, tn), lambda j, i: (i, j))
    osh = jax.ShapeDtypeStruct((T, Fp), _MXU)
    return pl.pallas_call(
        body, name=name, grid=(Fp // tn, T // tm),
        in_specs=[pl.BlockSpec((tm, D), lambda j, i: (i, 0)), wspec, wspec],
        out_specs=[ospec, ospec, ospec], out_shape=[osh, osh, osh],
        compiler_params=_cp("parallel", "parallel"))(a, wg, wu)


def _ffn_bwd_hid(dxp, wd, gate, up, name):
    T, D = dxp.shape
    Fp = wd.shape[0]
    tm, tn = _tile(T, ROW_TILE), _tile(Fp, 768)

    def body(dx_ref, wd_ref, g_ref, u_ref, dg_ref, du_ref):
        dh = 0.5 * _dot(dx_ref[...], wd_ref[...], _NT)
        g = g_ref[...].astype(F32)
        u = u_ref[...].astype(F32)
        s = _sigmoid(g)
        du_ref[...] = (dh * (g * s)).astype(du_ref.dtype)
        dg_ref[...] = (dh * u * (s * (1.0 + g * (1.0 - s)))).astype(dg_ref.dtype)

    tspec = pl.BlockSpec((tm, tn), lambda j, i: (i, j))
    osh = jax.ShapeDtypeStruct((T, Fp), _MXU)
    return pl.pallas_call(
        body, name=name, grid=(Fp // tn, T // tm),
        in_specs=[pl.BlockSpec((tm, D), lambda j, i: (i, 0)), pl.BlockSpec((tn, D), lambda j, i: (j, 0)), tspec, tspec],
        out_specs=[tspec, tspec], out_shape=[osh, osh],
        compiler_params=_cp("parallel", "parallel"))(dxp, wd, gate, up)


def _conf_fwd(zc, dw, b, lng, lnb, name):
    T = zc.shape[0]
    tm = _tile(T, ROW_TILE)
    r = tm // CONV_HALO

    def body(z_ref, zh_ref, dw_ref, b_ref, g_ref, lb_ref, u1_ref, act_ref, ext):
        i = pl.program_id(0)
        cur = z_ref[...]
        ext[pl.ds(CONV_HALO, tm), :] = cur[:, :CH] * _sigmoid(cur[:, CH:])
        hal = zh_ref[...]
        ext[pl.ds(0, CONV_HALO), :] = jnp.where(i > 0, hal[:, :CH] * _sigmoid(hal[:, CH:]), 0.0)
        acc = jnp.zeros((tm, CH), F32)
        for k in range(CONV_K):
            acc = acc + dw_ref[pl.ds(k, 1), :] * ext[pl.ds(CONV_HALO - (CONV_K - 1) + k, tm), :]
        u1 = acc + b_ref[...]
        u1_ref[...] = u1
        mu = jnp.mean(u1, axis=-1, keepdims=True)
        var = jnp.mean(jnp.square(u1 - mu), axis=-1, keepdims=True)
        u2 = (u1 - mu) * lax.rsqrt(var + EPS) * g_ref[...] + lb_ref[...]
        act_ref[...] = u2 * _sigmoid(u2)

    vec = pl.BlockSpec((1, CH), lambda i: (0, 0))
    row = pl.BlockSpec((tm, CH), lambda i: (i, 0))
    osh = jax.ShapeDtypeStruct((T, CH), F32)
    return pl.pallas_call(
        body, name=name, grid=(T // tm,),
        in_specs=[pl.BlockSpec((tm, 2 * CH), lambda i: (i, 0)),
                  pl.BlockSpec((CONV_HALO, 2 * CH), lambda i: (jnp.maximum(i * r - 1, 0), 0)),
                  pl.BlockSpec((CONV_HALO, CH), lambda i: (0, 0)), vec, vec, vec],
        out_specs=[row, row], out_shape=[osh, osh],
        scratch_shapes=[pltpu.VMEM((tm + CONV_HALO, CH), F32)],
        compiler_params=_cp("parallel"))(zc, zc, dw, b, lng, lnb)


def _conf_bwd_ln(dact, u1, lng, lnb, name):
    T = u1.shape[0]
    tm = _tile(T, ROW_TILE)

    def body(da_ref, u_ref, g_ref, lb_ref, du_ref, sm_ref):
        @pl.when(pl.program_id(0) == 0)
        def _():
            sm_ref[...] = jnp.zeros_like(sm_ref)

        u1v = u_ref[...]
        mu = jnp.mean(u1v, axis=-1, keepdims=True)
        cen = u1v - mu
        rstd = lax.rsqrt(jnp.mean(cen * cen, axis=-1, keepdims=True) + EPS)
        y = cen * rstd
        u2 = y * g_ref[...] + lb_ref[...]
        s = _sigmoid(u2)
        du2 = da_ref[...] * (s * (1.0 + u2 * (1.0 - s)))
        dy = du2 * g_ref[...]
        du1 = rstd * (dy - jnp.mean(dy, axis=-1, keepdims=True) - y * jnp.mean(dy * y, axis=-1, keepdims=True))
        du_ref[...] = du1
        sm_ref[pl.ds(0, 1), :] += jnp.sum(du2 * y, axis=0, keepdims=True)
        sm_ref[pl.ds(1, 1), :] += jnp.sum(du2, axis=0, keepdims=True)
        sm_ref[pl.ds(2, 1), :] += jnp.sum(du1, axis=0, keepdims=True)

    vec = pl.BlockSpec((1, CH), lambda i: (0, 0))
    row = pl.BlockSpec((tm, CH), lambda i: (i, 0))
    return pl.pallas_call(
        body, name=name, grid=(T // tm,), in_specs=[row, row, vec, vec],
        out_specs=[row, pl.BlockSpec((8, CH), lambda i: (0, 0))],
        out_shape=[jax.ShapeDtypeStruct((T, CH), F32), jax.ShapeDtypeStruct((8, CH), F32)],
        compiler_params=_cp("arbitrary"))(dact, u1, lng, lnb)


def _conf_bwd_conv(zc, du1, dw, name):
    T = zc.shape[0]
    tm = _tile(T, ROW_TILE)
    r = tm // CONV_HALO
    nt = T // tm
    nh = T // CONV_HALO

    def body(z_ref, zh_ref, d_ref, dn_ref, dw_ref, dz_ref, ddw_ref, ext_u, ext_d):
        i = pl.program_id(0)

        @pl.when(i == 0)
        def _():
            ddw_ref[...] = jnp.zeros_like(ddw_ref)

        cur = z_ref[...]
        ca = cur[:, :CH]
        sg = _sigmoid(cur[:, CH:])
        ext_u[pl.ds(CONV_HALO, tm), :] = ca * sg
        hal = zh_ref[...]
        ext_u[pl.ds(0, CONV_HALO), :] = jnp.where(i > 0, hal[:, :CH] * _sigmoid(hal[:, CH:]), 0.0)
        d = d_ref[...]
        ext_d[pl.ds(0, tm), :] = d
        ext_d[pl.ds(tm, CONV_HALO), :] = jnp.where(i < nt - 1, dn_ref[...], 0.0)
        acc = jnp.zeros((tm, CH), F32)
        for k in range(CONV_K):
            acc = acc + dw_ref[pl.ds(k, 1), :] * ext_d[pl.ds(CONV_K - 1 - k, tm), :]
            ddw_ref[pl.ds(k, 1), :] += jnp.sum(
                d * ext_u[pl.ds(CONV_HALO - (CONV_K - 1) + k, tm), :], axis=0, keepdims=True)
        dz_ref[:, :CH] = (acc * sg).astype(dz_ref.dtype)
        dz_ref[:, CH:] = (acc * ca * sg * (1.0 - sg)).astype(dz_ref.dtype)

    return pl.pallas_call(
        body, name=name, grid=(nt,),
        in_specs=[pl.BlockSpec((tm, 2 * CH), lambda i: (i, 0)),
                  pl.BlockSpec((CONV_HALO, 2 * CH), lambda i: (jnp.maximum(i * r - 1, 0), 0)),
                  pl.BlockSpec((tm, CH), lambda i: (i, 0)),
                  pl.BlockSpec((CONV_HALO, CH), lambda i: (jnp.minimum((i + 1) * r, nh - 1), 0)),
                  pl.BlockSpec((CONV_HALO, CH), lambda i: (0, 0))],
        out_specs=[pl.BlockSpec((tm, 2 * CH), lambda i: (i, 0)), pl.BlockSpec((CONV_HALO, CH), lambda i: (0, 0))],
        out_shape=[jax.ShapeDtypeStruct((T, 2 * CH), _MXU), jax.ShapeDtypeStruct((CONV_HALO, CH), F32)],
        scratch_shapes=[pltpu.VMEM((tm + CONV_HALO, CH), F32), pltpu.VMEM((tm + CONV_HALO, CH), F32)],
        compiler_params=_cp("arbitrary"))(zc, zc, du1, du1, dw)


def _sc_fwd(zs, w, name):
    T = zs.shape[0]
    tm = _tile(T, ROW_TILE)
    r = tm // SC_HALO

    def body(z_ref, zh_ref, w_ref, act_ref, ext):
        i = pl.program_id(0)
        cur = z_ref[...]
        ext[pl.ds(SC_HALO, tm), :] = cur[:, CH:2 * CH] * cur[:, 2 * CH:]
        hal = zh_ref[...]
        ext[pl.ds(0, SC_HALO), :] = jnp.where(i > 0, hal[:, CH:2 * CH] * hal[:, 2 * CH:], 0.0)
        v1 = jnp.zeros((tm, CH), F32)
        for k in range(SC_K):
            v1 = v1 + w_ref[pl.ds(k, 1), :] * ext[pl.ds(SC_HALO - (SC_K - 1) + k, tm), :]
        act_ref[...] = cur[:, :CH] * v1

    return pl.pallas_call(
        body, name=name, grid=(T // tm,),
        in_specs=[pl.BlockSpec((tm, 3 * CH), lambda i: (i, 0)),
                  pl.BlockSpec((SC_HALO, 3 * CH), lambda i: (jnp.maximum(i * r - 1, 0), 0)),
                  pl.BlockSpec((SC_HALO, CH), lambda i: (0, 0))],
        out_specs=pl.BlockSpec((tm, CH), lambda i: (i, 0)),
        out_shape=jax.ShapeDtypeStruct((T, CH), F32),
        scratch_shapes=[pltpu.VMEM((tm + SC_HALO, CH), F32)],
        compiler_params=_cp("parallel"))(zs, zs, w)


def _sc_bwd(zs, dact, w, name):
    T = zs.shape[0]
    tm = _tile(T, ROW_TILE)
    r = tm // SC_HALO
    nt = T // tm
    nh = T // SC_HALO

    def body(z_ref, zh_ref, zn_ref, d_ref, dn_ref, w_ref, dz_ref, dw_ref, ext_v, ext_d):
        i = pl.program_id(0)

        @pl.when(i == 0)
        def _():
            dw_ref[...] = jnp.zeros_like(dw_ref)

        cur = z_ref[...]
        sb, sc, sx = cur[:, :CH], cur[:, CH:2 * CH], cur[:, 2 * CH:]
        ext_v[pl.ds(SC_HALO, tm), :] = sc * sx
        hal = zh_ref[...]
        ext_v[pl.ds(0, SC_HALO), :] = jnp.where(i > 0, hal[:, CH:2 * CH] * hal[:, 2 * CH:], 0.0)
        da = d_ref[...]
        dv1 = da * sb
        ext_d[pl.ds(0, tm), :] = dv1
        ext_d[pl.ds(tm, SC_HALO), :] = jnp.where(i < nt - 1, dn_ref[...] * zn_ref[...][:, :CH], 0.0)
        v1 = jnp.zeros((tm, CH), F32)
        dv0 = jnp.zeros((tm, CH), F32)
        for k in range(SC_K):
            shifted = ext_v[pl.ds(SC_HALO - (SC_K - 1) + k, tm), :]
            v1 = v1 + w_ref[pl.ds(k, 1), :] * shifted
            dv0 = dv0 + w_ref[pl.ds(k, 1), :] * ext_d[pl.ds(SC_K - 1 - k, tm), :]
            dw_ref[pl.ds(k, 1), :] += jnp.sum(dv1 * shifted, axis=0, keepdims=True)
        dz_ref[:, :CH] = (da * v1).astype(dz_ref.dtype)
        dz_ref[:, CH:2 * CH] = (dv0 * sx).astype(dz_ref.dtype)
        dz_ref[:, 2 * CH:] = (dv0 * sc).astype(dz_ref.dtype)

    return pl.pallas_call(
        body, name=name, grid=(nt,),
        in_specs=[pl.BlockSpec((tm, 3 * CH), lambda i: (i, 0)),
                  pl.BlockSpec((SC_HALO, 3 * CH), lambda i: (jnp.maximum(i * r - 1, 0), 0)),
                  pl.BlockSpec((SC_HALO, 3 * CH), lambda i: (jnp.minimum((i + 1) * r, nh - 1), 0)),
                  pl.BlockSpec((tm, CH), lambda i: (i, 0)),
                  pl.BlockSpec((SC_HALO, CH), lambda i: (jnp.minimum((i + 1) * r, nh - 1), 0)),
                  pl.BlockSpec((SC_HALO, CH), lambda i: (0, 0))],
        out_specs=[pl.BlockSpec((tm, 3 * CH), lambda i: (i, 0)), pl.BlockSpec((SC_HALO, CH), lambda i: (0, 0))],
        out_shape=[jax.ShapeDtypeStruct((T, 3 * CH), _MXU), jax.ShapeDtypeStruct((SC_HALO, CH), F32)],
        scratch_shapes=[pltpu.VMEM((tm + SC_HALO, CH), F32), pltpu.VMEM((tm + SC_HALO, CH), F32)],
        compiler_params=_cp("arbitrary"))(zs, zs, zs, dact, dact, w)


SWA_TQ = 256


def _t5_bucket_np(dist):
    max_exact = N_BUCKETS // 2
    d = np.maximum(dist, 1).astype(np.float32)
    large = max_exact + (np.log(d / np.float32(max_exact)) / np.float32(math.log(MAX_DISTANCE / max_exact))
                         * np.float32(N_BUCKETS - max_exact)).astype(np.int32)
    large = np.minimum(large, N_BUCKETS - 1)
    return np.where(dist < max_exact, dist, large).astype(np.int32)


def _swa_bucket_matrix(tq):
    dist = WINDOW + np.arange(tq)[:, None] - np.arange(tq + WINDOW)[None, :]
    ok = (dist >= 0) & (dist < WINDOW)
    return np.where(ok, _t5_bucket_np(np.maximum(dist, 0)), -1).astype(np.int32)


def _kv_expand_matrix():
    e = np.zeros((2 * HEAD, 4 * HEAD), np.float32)
    for h in range(4):
        for d in range(HEAD):
            e[(h // 2) * HEAD + d, h * HEAD + d] = 1.0
    return e


def _swa_bias(rel_bias, bucket, name):
    tq, tk = bucket.shape

    def body(rb_ref, bk_ref, o_ref):
        h = pl.program_id(0)
        bk = bk_ref[...]
        acc = jnp.full((tq, tk), NEG_INF, F32)
        for b in range(N_BUCKETS):
            acc = jnp.where(bk == b, rb_ref[b, h], acc)
        o_ref[0] = acc

    return pl.pallas_call(
        body, name=name, grid=(4,),
        in_specs=[pl.BlockSpec(memory_space=pltpu.SMEM), pl.BlockSpec((tq, tk), lambda h: (0, 0))],
        out_specs=pl.BlockSpec((1, tq, tk), lambda h: (h, 0, 0)),
        out_shape=jax.ShapeDtypeStruct((4, tq, tk), F32), compiler_params=_cp("parallel"))(rel_bias, bucket)


def _swa_probs(qh, kx, bm, first_col, sk):
    s = _dot(qh, kx, _NT) * (HEAD ** -0.5)
    col = lax.broadcasted_iota(jnp.int32, s.shape, 1)
    valid = (bm > 0.5 * NEG_INF) & (col >= first_col)
    s = jnp.where(valid, s + bm, NEG_INF)
    m = jnp.maximum(jnp.max(s, axis=-1, keepdims=True), sk)
    p = jnp.exp(s - m)
    den = jnp.sum(p, axis=-1, keepdims=True) + jnp.exp(sk - m)
    return p / den, m, den


def _swa_fwd(zw, gq, gk, sink, bias, expand, name):
    T = zw.shape[0]
    tq = bias.shape[1]
    r = tq // WINDOW

    def body(z_ref, zh_ref, gq_ref, gk_ref, sink_ref, b_ref, e_ref, o_ref, kext, vext):
        i = pl.program_id(0)
        cur = z_ref[...]
        qn, _ = _head_rms(cur[:, :4 * HEAD], gq_ref[...], 4)
        kc, _ = _head_rms(cur[:, 4 * HEAD:6 * HEAD], gk_ref[...], 2)
        hal = zh_ref[...]
        kp, _ = _head_rms(hal[:, :2 * HEAD], gk_ref[...], 2)
        kext[pl.ds(0, WINDOW), :] = kp
        kext[pl.ds(WINDOW, tq), :] = kc
        vext[pl.ds(0, WINDOW), :] = hal[:, 2 * HEAD:]
        vext[pl.ds(WINDOW, tq), :] = cur[:, 6 * HEAD:]
        kx = _dot(kext[...], e_ref[...]).astype(_MXU)
        vx = _dot(vext[...], e_ref[...]).astype(_MXU)
        first_col = jnp.where(i > 0, 0, WINDOW)
        out = jnp.zeros((tq, 4 * HEAD), F32)
        for h in range(4):
            mk = _lane_mask(4 * HEAD, h)
            qh = jnp.where(mk, qn, 0.0)
            pn, _, _ = _swa_probs(qh, kx, b_ref[h], first_col, sink_ref[0, h])
            out = jnp.where(mk, _dot(pn, vx), out)
        o_ref[...] = out

    return pl.pallas_call(
        body, name=name, grid=(T // tq,),
        in_specs=[pl.BlockSpec((tq, 8 * HEAD), lambda i: (i, 0)),
                  pl.BlockSpec((WINDOW, 4 * HEAD), lambda i: (jnp.maximum(i * r - 1, 0), 1)),
                  pl.BlockSpec((1, 4 * HEAD), lambda i: (0, 0)), pl.BlockSpec((1, 2 * HEAD), lambda i: (0, 0)),
                  pl.BlockSpec(memory_space=pltpu.SMEM),
                  pl.BlockSpec(bias.shape, lambda i: (0, 0, 0)),
                  pl.BlockSpec(expand.shape, lambda i: (0, 0))],
        out_specs=pl.BlockSpec((tq, 4 * HEAD), lambda i: (i, 0)),
        out_shape=jax.ShapeDtypeStruct((T, 4 * HEAD), F32),
        scratch_shapes=[pltpu.VMEM((tq + WINDOW, 2 * HEAD), F32), pltpu.VMEM((tq + WINDOW, 2 * HEAD), F32)],
        compiler_params=_cp("parallel"))(zw, zw, gq, gk, sink, bias, expand)


def _swa_bwd(zw, dact, gq, gk, sink, bias, bucket, expand, name):
    T = zw.shape[0]
    tq = bias.shape[1]
    tk = tq + WINDOW
    r = tq // WINDOW
    nt = T // tq
    nb = T // WINDOW
    scale = HEAD ** -0.5

    def body(z_ref, zh_ref, zn_ref, d_ref, dn_ref, gq_ref, gk_ref, sink_ref, b_ref, bk_ref, e_ref,
             dz_ref, dgq_ref, dgk_ref, dsk_ref, drb_ref, kext, vext, dk_s, dv_s, db_s):
        i = pl.program_id(0)

        @pl.when(i == 0)
        def _():
            dgq_ref[...] = jnp.zeros_like(dgq_ref)
            dgk_ref[...] = jnp.zeros_like(dgk_ref)
            dsk_ref[...] = jnp.zeros_like(dsk_ref)
            drb_ref[...] = jnp.zeros_like(drb_ref)
            db_s[...] = jnp.zeros_like(db_s)

        lane = lax.broadcasted_iota(jnp.int32, (1, LANE), 1)
        cur = z_ref[...]
        q_raw, k_raw = cur[:, :4 * HEAD], cur[:, 4 * HEAD:6 * HEAD]
        qn, q_r = _head_rms(q_raw, gq_ref[...], 4)
        kc, k_r = _head_rms(k_raw, gk_ref[...], 2)
        hal = zh_ref[...]
        kp, _ = _head_rms(hal[:, :2 * HEAD], gk_ref[...], 2)
        kext[pl.ds(0, WINDOW), :] = kp
        kext[pl.ds(WINDOW, tq), :] = kc
        vext[pl.ds(0, WINDOW), :] = hal[:, 2 * HEAD:]
        vext[pl.ds(WINDOW, tq), :] = cur[:, 6 * HEAD:]
        ev = e_ref[...]
        kx = _dot(kext[...], ev).astype(_MXU)
        vx = _dot(vext[...], ev).astype(_MXU)
        first_col = jnp.where(i > 0, 0, WINDOW)
        do = d_ref[...]
        dq = jnp.zeros((tq, 4 * HEAD), F32)
        dkx = jnp.zeros((tk, 4 * HEAD), F32)
        dvx = jnp.zeros((tk, 4 * HEAD), F32)
        dsk = jnp.zeros((1, LANE), F32)
        for h in range(4):
            mk = _lane_mask(4 * HEAD, h)
            qh = jnp.where(mk, qn, 0.0).astype(_MXU)
            sk = sink_ref[0, h]
            pn, m, den = _swa_probs(qh, kx, b_ref[h], first_col, sk)
            doh = jnp.where(mk, do, 0.0).astype(_MXU)
            dpn = _dot(doh, vx, _NT)
            delta = jnp.sum(pn * dpn, axis=-1, keepdims=True)
            ds = pn * (dpn - delta)
            psink = jnp.exp(sk - m) / den
            dsk = dsk + jnp.where(lane == h, jnp.sum(-psink * delta, axis=0, keepdims=True), 0.0)
            db_s[h] += ds
            dss = (ds * scale).astype(_MXU)
            dq = dq + jnp.where(mk, _dot(dss, kx), 0.0)
            dkx = dkx + _dot(dss, qh, _TN)
            dvx = dvx + _dot(pn, doh, _TN)
        dsk_ref[...] += dsk
        dk_ext = _exact_dot(dkx, ev, _NT, "a")
        dv_ext = _exact_dot(dvx, ev, _NT, "a")
        dk_s[...] = dk_ext[WINDOW:, :]
        dv_s[...] = dv_ext[WINDOW:, :]

        @pl.when(i < nt - 1)
        def _():
            nxt = zn_ref[...]
            q2, _ = _head_rms(nxt[:, :4 * HEAD], gq_ref[...], 4)
            k2n, _ = _head_rms(nxt[:, 4 * HEAD:6 * HEAD], gk_ref[...], 2)
            k2 = jnp.concatenate([kc[tq - WINDOW:, :], k2n], axis=0)
            v2 = jnp.concatenate([cur[tq - WINDOW:, 6 * HEAD:], nxt[:, 6 * HEAD:]], axis=0)
            k2x = _dot(k2, ev).astype(_MXU)
            v2x = _dot(v2, ev).astype(_MXU)
            do2 = dn_ref[...]
            dk2x = jnp.zeros((2 * WINDOW, 4 * HEAD), F32)
            dv2x = jnp.zeros((2 * WINDOW, 4 * HEAD), F32)
            for h in range(4):
                mk = _lane_mask(4 * HEAD, h)
                qh = jnp.where(mk, q2, 0.0).astype(_MXU)
                pn, _, _ = _swa_probs(qh, k2x, b_ref[h][:WINDOW, :2 * WINDOW], 0, sink_ref[0, h])
                doh = jnp.where(mk, do2, 0.0).astype(_MXU)
                dpn = _dot(doh, v2x, _NT)
                ds = pn * (dpn - jnp.sum(pn * dpn, axis=-1, keepdims=True))
                dk2x = dk2x + _dot((ds * scale).astype(_MXU), qh, _TN)
                dv2x = dv2x + _dot(pn, doh, _TN)
            dk_s[pl.ds(tq - WINDOW, WINDOW), :] += _exact_dot(dk2x, ev, _NT, "a")[:WINDOW, :]
            dv_s[pl.ds(tq - WINDOW, WINDOW), :] += _exact_dot(dv2x, ev, _NT, "a")[:WINDOW, :]

        dq_raw, dgq = _head_rms_bwd(dq, q_raw, q_r, gq_ref[...], 4)
        dk_raw, dgk = _head_rms_bwd(dk_s[...], k_raw, k_r, gk_ref[...], 2)
        dgq_ref[...] += dgq
        dgk_ref[...] += dgk
        dz_ref[:, :4 * HEAD] = dq_raw.astype(dz_ref.dtype)
        dz_ref[:, 4 * HEAD:6 * HEAD] = dk_raw.astype(dz_ref.dtype)
        dz_ref[:, 6 * HEAD:] = dv_s[...].astype(dz_ref.dtype)

        @pl.when(i == nt - 1)
        def _():
            bk = bk_ref[...]
            for b in range(N_BUCKETS):
                rowv = jnp.zeros((1, LANE), F32)
                for h in range(4):
                    s1 = jnp.sum(jnp.where(bk == b, db_s[h], 0.0), axis=0, keepdims=True)
                    rowv = jnp.where(lane == h, jnp.sum(s1, axis=1, keepdims=True), rowv)
                drb_ref[pl.ds(b, 1), :] = rowv

    const2 = lambda i: (0, 0)
    return pl.pallas_call(
        body, name=name, grid=(nt,),
        in_specs=[pl.BlockSpec((tq, 8 * HEAD), lambda i: (i, 0)),
                  pl.BlockSpec((WINDOW, 4 * HEAD), lambda i: (jnp.maximum(i * r - 1, 0), 1)),
                  pl.BlockSpec((WINDOW, 8 * HEAD), lambda i: (jnp.minimum((i + 1) * r, nb - 1), 0)),
                  pl.BlockSpec((tq, 4 * HEAD), lambda i: (i, 0)),
                  pl.BlockSpec((WINDOW, 4 * HEAD), lambda i: (jnp.minimum((i + 1) * r, nb - 1), 0)),
                  pl.BlockSpec((1, 4 * HEAD), const2), pl.BlockSpec((1, 2 * HEAD), const2),
                  pl.BlockSpec(memory_space=pltpu.SMEM),
                  pl.BlockSpec(bias.shape, lambda i: (0, 0, 0)),
                  pl.BlockSpec(bucket.shape, const2), pl.BlockSpec(expand.shape, const2)],
        out_specs=[pl.BlockSpec((tq, 8 * HEAD), lambda i: (i, 0)),
                   pl.BlockSpec((1, 4 * HEAD), const2), pl.BlockSpec((1, 2 * HEAD), const2),
                   pl.BlockSpec((1, LANE), const2), pl.BlockSpec((N_BUCKETS, LANE), const2)],
        out_shape=[jax.ShapeDtypeStruct((T, 8 * HEAD), _MXU), jax.ShapeDtypeStruct((1, 4 * HEAD), F32),
                   jax.ShapeDtypeStruct((1, 2 * HEAD), F32), jax.ShapeDtypeStruct((1, LANE), F32),
                   jax.ShapeDtypeStruct((N_BUCKETS, LANE), F32)],
        scratch_shapes=[pltpu.VMEM((tk, 2 * HEAD), F32), pltpu.VMEM((tk, 2 * HEAD), F32),
                        pltpu.VMEM((tq, 2 * HEAD), F32), pltpu.VMEM((tq, 2 * HEAD), F32),
                        pltpu.VMEM((4, tq, tk), F32)],
        compiler_params=_cp("arbitrary"))(zw, zw, zw, dact, dact, gq, gk, sink, bias, bucket, expand)


FOX_B = 512
FOX_TM = 256


def _tri(n, lower):
    m = np.tril(np.ones((n, n), np.float32)) if lower else np.triu(np.ones((n, n), np.float32))
    return m


def _log_sigmoid(x):
    return jnp.minimum(x, 0.0) - jnp.log1p(jnp.exp(-jnp.abs(x)))


def _fox_prep(zf, gq, gk, bf, name):
    T = zf.shape[0]
    tm = _tile(T, FOX_TM)
    lower = jnp.asarray(_tri(tm, True), _MXU)

    def body(z_ref, gq_ref, gk_ref, bf_ref, l_ref, q_ref, k_ref, v_ref, f_ref, ft_ref, carry):
        @pl.when(pl.program_id(0) == 0)
        def _():
            carry[...] = jnp.zeros_like(carry)

        z = z_ref[...]
        q, _ = _head_rms(z[:, :CH], gq_ref[...], 4)
        k, _ = _head_rms(z[:, CH:2 * CH], gk_ref[...], 4)
        q_ref[...] = q.astype(q_ref.dtype)
        k_ref[...] = k.astype(k_ref.dtype)
        v_ref[...] = z[:, 2 * CH:3 * CH].astype(v_ref.dtype)
        lane = lax.broadcasted_iota(jnp.int32, (1, LANE), 1)
        lf = jnp.where(lane < 4, _log_sigmoid(z[:, 3 * CH:] + bf_ref[...]), 0.0)
        fv = _exact_dot(l_ref[...], lf, _NN, "b") + carry[pl.ds(0, 1), :]
        f_ref[...] = fv
        ft_ref[...] = fv.T
        carry[pl.ds(0, 1), :] = f_ref[pl.ds(tm - 1, 1), :]

    row = pl.BlockSpec((tm, CH), lambda i: (i, 0))
    vec = pl.BlockSpec((1, CH), lambda i: (0, 0))
    qsh = jax.ShapeDtypeStruct((T, CH), _MXU)
    return pl.pallas_call(
        body, name=name, grid=(T // tm,),
        in_specs=[pl.BlockSpec((tm, 3 * CH + LANE), lambda i: (i, 0)), vec, vec,
                  pl.BlockSpec((1, LANE), lambda i: (0, 0)), pl.BlockSpec((tm, tm), lambda i: (0, 0))],
        out_specs=[row, row, row, pl.BlockSpec((tm, LANE), lambda i: (i, 0)), pl.BlockSpec((LANE, tm), lambda i: (0, i))],
        out_shape=[qsh, qsh, qsh, jax.ShapeDtypeStruct((T, LANE), F32), jax.ShapeDtypeStruct((LANE, T), F32)],
        scratch_shapes=[pltpu.VMEM((8, LANE), F32)],
        compiler_params=_cp("arbitrary"))(zf, gq, gk, bf, lower)


def _lane_col(x, h):
    lane = lax.broadcasted_iota(jnp.int32, (1, x.shape[-1]), 1)
    return jnp.sum(jnp.where(lane == h, x, 0.0), axis=-1, keepdims=True)


def _fox_scores(qh, k, fq, ft_ref, h, qi, ki, B):
    s = _dot(qh, k, _NT) * (HEAD ** -0.5)
    s = s + (fq - ft_ref[pl.ds(h, 1), :])
    row = qi * B + lax.broadcasted_iota(jnp.int32, s.shape, 0)
    col = ki * B + lax.broadcasted_iota(jnp.int32, s.shape, 1)
    return jnp.where(col <= row, s, NEG_INF)


def _fox_fwd(q, k, v, f, ft, name):
    T = q.shape[0]
    B = _tile(T, FOX_B)
    n = T // B

    def body(q_ref, k_ref, v_ref, f_ref, ft_ref, o_ref, lse_ref, m_s, l_s, acc):
        qi, ki = pl.program_id(0), pl.program_id(1)

        @pl.when(ki == 0)
        def _():
            m_s[...] = jnp.full_like(m_s, NEG_INF)
            l_s[...] = jnp.zeros_like(l_s)
            acc[...] = jnp.zeros_like(acc)

        @pl.when(ki <= qi)
        def _():
            qv, kv, vv, fv = q_ref[...], k_ref[...], v_ref[...], f_ref[...]
            for h in range(4):
                mk = _lane_mask(CH, h)
                qh = jnp.where(mk, qv, jnp.zeros_like(qv))
                s = _fox_scores(qh, kv, _lane_col(fv, h), ft_ref, h, qi, ki, B)
                m_old = m_s[h]
                m_new = jnp.maximum(m_old, jnp.max(s, axis=-1, keepdims=True))
                alpha = jnp.exp(m_old - m_new)
                p = jnp.exp(s - m_new)
                l_s[h] = alpha * l_s[h] + jnp.sum(p, axis=-1, keepdims=True)
                m_s[h] = m_new
                acc[...] = jnp.where(mk, acc[...] * alpha + _dot(p, vv), acc[...])

        @pl.when(ki == qi)
        def _():
            lane = lax.broadcasted_iota(jnp.int32, (1, LANE), 1)
            out = acc[...]
            lse = jnp.zeros((B, LANE), F32)
            for h in range(4):
                out = jnp.where(_lane_mask(CH, h), out / l_s[h], out)
                lse = jnp.where(lane == h, m_s[h] + jnp.log(l_s[h]), lse)
            o_ref[...] = out
            lse_ref[...] = lse

    qspec = pl.BlockSpec((B, CH), lambda qi, ki: (qi, 0))
    kspec = pl.BlockSpec((B, CH), lambda qi, ki: (jnp.minimum(ki, qi), 0))
    return pl.pallas_call(
        body, name=name, grid=(n, n),
        in_specs=[qspec, kspec, kspec, pl.BlockSpec((B, LANE), lambda qi, ki: (qi, 0)),
                  pl.BlockSpec((8, B), lambda qi, ki: (0, jnp.minimum(ki, qi)))],
        out_specs=[qspec, pl.BlockSpec((B, LANE), lambda qi, ki: (qi, 0))],
        out_shape=[jax.ShapeDtypeStruct((T, CH), F32), jax.ShapeDtypeStruct((T, LANE), F32)],
        scratch_shapes=[pltpu.VMEM((4, B, 1), F32), pltpu.VMEM((4, B, 1), F32), pltpu.VMEM((B, CH), F32)],
        compiler_params=_cp("parallel", "arbitrary"))(q, k, v, f, ft)


def _fox_delta(o, do, name):
    T = o.shape[0]
    tm = _tile(T, ROW_TILE)

    def body(o_ref, d_ref, out_ref):
        prod = o_ref[...] * d_ref[...]
        lane = lax.broadcasted_iota(jnp.int32, (1, LANE), 1)
        out = jnp.zeros((tm, LANE), F32)
        for h in range(4):
            s = jnp.sum(jnp.where(_lane_mask(CH, h), prod, 0.0), axis=-1, keepdims=True)
            out = jnp.where(lane == h, s, out)
        out_ref[...] = out

    row = pl.BlockSpec((tm, CH), lambda i: (i, 0))
    return pl.pallas_call(
        body, name=name, grid=(T // tm,), in_specs=[row, row],
        out_specs=pl.BlockSpec((tm, LANE), lambda i: (i, 0)),
        out_shape=jax.ShapeDtypeStruct((T, LANE), F32), compiler_params=_cp("parallel"))(o, do)


def _fox_bwd_dq(q, k, v, f, ft, lse, delta, do, name):
    T = q.shape[0]
    B = _tile(T, FOX_B)
    n = T // B

    def body(q_ref, k_ref, v_ref, f_ref, ft_ref, lse_ref, dl_ref, do_ref, dq_ref, dfq_ref, dq_s, df_s):
        qi, ki = pl.program_id(0), pl.program_id(1)

        @pl.when(ki == 0)
        def _():
            dq_s[...] = jnp.zeros_like(dq_s)
            df_s[...] = jnp.zeros_like(df_s)

        @pl.when(ki <= qi)
        def _():
            qv, kv, vv, fv = q_ref[...], k_ref[...], v_ref[...], f_ref[...]
            lsev, dlv, dov = lse_ref[...], dl_ref[...], do_ref[...]
            lane = lax.broadcasted_iota(jnp.int32, (1, LANE), 1)
            for h in range(4):
                mk = _lane_mask(CH, h)
                qh = jnp.where(mk, qv, jnp.zeros_like(qv))
                s = _fox_scores(qh, kv, _lane_col(fv, h), ft_ref, h, qi, ki, B)
                p = jnp.exp(s - _lane_col(lsev, h))
                doh = jnp.where(mk, dov, 0.0)
                ds = p * (_dot(doh, vv, _NT) - _lane_col(dlv, h))
                dq_s[...] += jnp.where(mk, _dot(ds * (HEAD ** -0.5), kv), 0.0)
                df_s[...] += jnp.where(lane == h, jnp.sum(ds, axis=-1, keepdims=True), 0.0)

        @pl.when(ki == qi)
        def _():
            dq_ref[...] = dq_s[...]
            dfq_ref[...] = df_s[...]

    qspec = pl.BlockSpec((B, CH), lambda qi, ki: (qi, 0))
    kspec = pl.BlockSpec((B, CH), lambda qi, ki: (jnp.minimum(ki, qi), 0))
    lspec = pl.BlockSpec((B, LANE), lambda qi, ki: (qi, 0))
    return pl.pallas_call(
        body, name=name, grid=(n, n),
        in_specs=[qspec, kspec, kspec, lspec, pl.BlockSpec((8, B), lambda qi, ki: (0, jnp.minimum(ki, qi))),
                  lspec, lspec, qspec],
        out_specs=[qspec, lspec],
        out_shape=[jax.ShapeDtypeStruct((T, CH), F32), jax.ShapeDtypeStruct((T, LANE), F32)],
        scratch_shapes=[pltpu.VMEM((B, CH), F32), pltpu.VMEM((B, LANE), F32)],
        compiler_params=_cp("parallel", "arbitrary"))(q, k, v, f, ft, lse, delta, do)


def _fox_bwd_dkv(q, k, v, f, ft, lse, delta, do, name):
    T = q.shape[0]
    B = _tile(T, FOX_B)
    n = T // B

    def body(q_ref, k_ref, v_ref, f_ref, ft_ref, lse_ref, dl_ref, do_ref, dk_ref, dv_ref, dft_ref, dk_s, dv_s, df_s):
        ki, qi = pl.program_id(0), pl.program_id(1)

        @pl.when(qi == 0)
        def _():
            dk_s[...] = jnp.zeros_like(dk_s)
            dv_s[...] = jnp.zeros_like(dv_s)
            df_s[...] = jnp.zeros_like(df_s)

        @pl.when(qi >= ki)
        def _():
            qv, kv, vv, fv = q_ref[...], k_ref[...], v_ref[...], f_ref[...]
            lsev, dlv, dov = lse_ref[...], dl_ref[...], do_ref[...]
            for h in range(4):
                mk = _lane_mask(CH, h)
                qh = jnp.where(mk, qv, jnp.zeros_like(qv))
                s = _fox_scores(qh, kv, _lane_col(fv, h), ft_ref, h, qi, ki, B)
                p = jnp.exp(s - _lane_col(lsev, h))
                doh = jnp.where(mk, dov, 0.0)
                ds = p * (_dot(doh, vv, _NT) - _lane_col(dlv, h))
                dv_s[...] += _dot(p, doh, _TN)
                dk_s[...] += _dot(ds * (HEAD ** -0.5), qh, _TN)
                df_s[pl.ds(h, 1), :] -= jnp.sum(ds, axis=0, keepdims=True)

        @pl.when(qi == n - 1)
        def _():
            dk_ref[...] = dk_s[...]
            dv_ref[...] = dv_s[...]
            dft_ref[...] = jnp.zeros_like(dft_ref)
            dft_ref[pl.ds(0, 8), :] = df_s[...]

    qspec = pl.BlockSpec((B, CH), lambda ki, qi: (jnp.maximum(qi, ki), 0))
    kspec = pl.BlockSpec((B, CH), lambda ki, qi: (ki, 0))
    lspec = pl.BlockSpec((B, LANE), lambda ki, qi: (jnp.maximum(qi, ki), 0))
    return pl.pallas_call(
        body, name=name, grid=(n, n),
        in_specs=[qspec, kspec, kspec, lspec, pl.BlockSpec((8, B), lambda ki, qi: (0, ki)), lspec, lspec, qspec],
        out_specs=[kspec, kspec, pl.BlockSpec((LANE, B), lambda ki, qi: (0, ki))],
        out_shape=[jax.ShapeDtypeStruct((T, CH), F32), jax.ShapeDtypeStruct((T, CH), F32),
                   jax.ShapeDtypeStruct((LANE, T), F32)],
        scratch_shapes=[pltpu.VMEM((B, CH), F32), pltpu.VMEM((B, CH), F32), pltpu.VMEM((8, B), F32)],
        compiler_params=_cp("parallel", "arbitrary"))(q, k, v, f, ft, lse, delta, do)


def _fox_post(zf, dqn, dkn, dv, dfq, dft, gq, gk, bf, name):
    T = zf.shape[0]
    tm = _tile(T, FOX_TM)
    nt = T // tm
    upper = jnp.asarray(_tri(tm, False), _MXU)

    def body(z_ref, dq_ref, dk_ref, dv_ref, dfq_ref, dft_ref, gq_ref, gk_ref, bf_ref, u_ref, dz_ref, sm_ref, carry, rc_s):
        @pl.when(pl.program_id(0) == 0)
        def _():
            carry[...] = jnp.zeros_like(carry)
            sm_ref[...] = jnp.zeros_like(sm_ref)

        z = z_ref[...]
        q_raw, k_raw = z[:, :CH], z[:, CH:2 * CH]
        _, q_r = _head_rms(q_raw, gq_ref[...], 4)
        _, k_r = _head_rms(k_raw, gk_ref[...], 4)
        dq, dgq = _head_rms_bwd(dq_ref[...], q_raw, q_r, gq_ref[...], 4)
        dk, dgk = _head_rms_bwd(dk_ref[...], k_raw, k_r, gk_ref[...], 4)
        df = dfq_ref[...] + dft_ref[...].T
        rc_s[...] = _exact_dot(u_ref[...], df, _NN, "b") + carry[pl.ds(0, 1), :]
        carry[pl.ds(0, 1), :] = rc_s[pl.ds(0, 1), :]
        lane = lax.broadcasted_iota(jnp.int32, (1, LANE), 1)
        x = z[:, 3 * CH:] + bf_ref[...]
        dff = jnp.where(lane < 4, rc_s[...] * _sigmoid(-x), 0.0)
        dz_ref[:, :CH] = dq.astype(dz_ref.dtype)
        dz_ref[:, CH:2 * CH] = dk.astype(dz_ref.dtype)
        dz_ref[:, 2 * CH:3 * CH] = dv_ref[...].astype(dz_ref.dtype)
        dz_ref[:, 3 * CH:] = dff.astype(dz_ref.dtype)
        sm_ref[pl.ds(0, 1), :] += dgq
        sm_ref[pl.ds(1, 1), :] += dgk
        sm_ref[pl.ds(2, 1), :LANE] += jnp.sum(dff, axis=0, keepdims=True)

    rev = lambda i: (nt - 1 - i, 0)
    row = pl.BlockSpec((tm, CH), rev)
    lrow = pl.BlockSpec((tm, LANE), rev)
    vec = pl.BlockSpec((1, CH), lambda i: (0, 0))
    return pl.pallas_call(
        body, name=name, grid=(nt,),
        in_specs=[pl.BlockSpec((tm, 3 * CH + LANE), rev), row, row, row, lrow,
                  pl.BlockSpec((LANE, tm), lambda i: (0, nt - 1 - i)), vec, vec,
                  pl.BlockSpec((1, LANE), lambda i: (0, 0)), pl.BlockSpec((tm, tm), lambda i: (0, 0))],
        out_specs=[pl.BlockSpec((tm, 3 * CH + LANE), rev), pl.BlockSpec((8, CH), lambda i: (0, 0))],
        out_shape=[jax.ShapeDtypeStruct((T, 3 * CH + LANE), _MXU), jax.ShapeDtypeStruct((8, CH), F32)],
        scratch_shapes=[pltpu.VMEM((8, LANE), F32), pltpu.VMEM((tm, LANE), F32)],
        compiler_params=_cp("arbitrary"))(zf, dqn, dkn, dv, dfq, dft, gq, gk, bf, upper)


AUG_F, AUG_ONE, AUG_LSE = HEAD, HEAD + 3, HEAD + 6


def _pieces(x):
    hi = x.astype(_MXU).astype(F32)
    r1 = x - hi
    mid = r1.astype(_MXU).astype(F32)
    lo = (r1 - mid).astype(_MXU).astype(F32)
    return hi, mid, lo


def _put_pieces(base, first_lane, x, sign):
    lane = lax.broadcasted_iota(jnp.int32, (1, LANE), 1)
    for j, piece in enumerate(_pieces(x)):
        base = jnp.where(lane == first_lane + j, sign * piece, base)
    return base


def _head_select_matrix():
    p = np.zeros((4, 4 * HEAD, LANE), np.float32)
    for h in range(4):
        for d in range(HEAD):
            p[h, h * HEAD + d, d] = 1.0
    return p


def _tri_steps(n, by_key):
    if by_key:
        pairs = [(q, k) for k in range(n) for q in range(k, n)]
    else:
        pairs = [(q, k) for q in range(n) for k in range(q + 1)]
    return (jnp.asarray([p[0] for p in pairs], jnp.int32), jnp.asarray([p[1] for p in pairs], jnp.int32))


def _fox2_prep(zf, gq, gk, bf, sel, name):
    T = zf.shape[0]
    tm = _tile(T, FOX_TM)
    lower = jnp.asarray(_tri(tm, True), _MXU)

    def body(z_ref, gq_ref, gk_ref, bf_ref, l_ref, p_ref, qa_ref, ka_ref, va_ref, carry, f_s):
        @pl.when(pl.program_id(0) == 0)
        def _():
            carry[...] = jnp.zeros_like(carry)

        z = z_ref[...]
        q, _ = _head_rms(z[:, :CH], gq_ref[...], 4)
        k, _ = _head_rms(z[:, CH:2 * CH], gk_ref[...], 4)
        q = (q * (HEAD ** -0.5)).astype(_MXU)
        k = k.astype(_MXU)
        v = z[:, 2 * CH:3 * CH].astype(_MXU)
        lane = lax.broadcasted_iota(jnp.int32, (1, LANE), 1)
        lf = jnp.where(lane < 4, _log_sigmoid(z[:, 3 * CH:] + bf_ref[...]), 0.0)
        f_s[...] = _exact_dot(l_ref[...], lf, _NN, "b") + carry[pl.ds(0, 1), :]
        carry[pl.ds(0, 1), :] = f_s[pl.ds(tm - 1, 1), :]
        fv = f_s[...]
        q_ones = (lane >= AUG_ONE) & (lane < AUG_ONE + 3)
        k_ones = ((lane >= AUG_F) & (lane < AUG_F + 3)) | ((lane >= AUG_LSE) & (lane < AUG_LSE + 3))
        v_ones = (lane >= AUG_F) & (lane < AUG_F + 3)
        for h in range(4):
            fh = _lane_col(fv, h)
            qa = jnp.where(q_ones, 1.0, _dot(q, p_ref[h]))
            qa_ref[h] = _put_pieces(qa, AUG_F, fh, 1.0).astype(qa_ref.dtype)
            ka = jnp.where(k_ones, 1.0, _dot(k, p_ref[h]))
            ka_ref[h] = _put_pieces(ka, AUG_ONE, fh, -1.0).astype(ka_ref.dtype)
            va_ref[h] = jnp.where(v_ones, 1.0, _dot(v, p_ref[h])).astype(va_ref.dtype)

    vec = pl.BlockSpec((1, CH), lambda i: (0, 0))
    hspec = pl.BlockSpec((4, tm, LANE), lambda i: (0, i, 0))
    hsh = jax.ShapeDtypeStruct((4, T, LANE), _MXU)
    return pl.pallas_call(
        body, name=name, grid=(T // tm,),
        in_specs=[pl.BlockSpec((tm, 3 * CH + LANE), lambda i: (i, 0)), vec, vec,
                  pl.BlockSpec((1, LANE), lambda i: (0, 0)), pl.BlockSpec((tm, tm), lambda i: (0, 0)),
                  pl.BlockSpec(sel.shape, lambda i: (0, 0, 0))],
        out_specs=[hspec, hspec, hspec], out_shape=[hsh, hsh, hsh],
        scratch_shapes=[pltpu.VMEM((8, LANE), F32), pltpu.VMEM((tm, LANE), F32)],
        compiler_params=_cp("arbitrary"))(zf, gq, gk, bf, lower, sel)


def _causal(s, transposed):
    row = lax.broadcasted_iota(jnp.int32, s.shape, 0)
    col = lax.broadcasted_iota(jnp.int32, s.shape, 1)
    return jnp.where((row <= col) if transposed else (col <= row), s, NEG_INF)


def _mxu_dot(a, b, dims):
    return lax.dot_general(a, b, dims, preferred_element_type=F32)


def _fox2_fwd(qa, ka, va, sel, name):
    T = qa.shape[1]
    B = _tile(T, FOX_B)
    n = T // B
    qt, kt = _tri_steps(n, False)

    def body(qt_ref, kt_ref, qa_ref, ka_ref, va_ref, p_ref, o_ref, qb_ref, m_s, acc):
        step = pl.program_id(0)
        qi, ki = qt_ref[step], kt_ref[step]

        @pl.when(ki == 0)
        def _():
            m_s[...] = jnp.full_like(m_s, NEG_INF)
            acc[...] = jnp.zeros_like(acc)

        def update(diag):
            for h in range(4):
                s = _mxu_dot(qa_ref[h], ka_ref[h], _NT)
                if diag:
                    s = _causal(s, False)
                m_old = m_s[h]
                m_new = jnp.maximum(m_old, jnp.max(s, axis=-1, keepdims=True))
                p = jnp.exp(s - m_new)
                acc[h] = acc[h] * jnp.exp(m_old - m_new) + _dot(p, va_ref[h])
                m_s[h] = m_new

        @pl.when(ki < qi)
        def _():
            update(False)

        @pl.when(ki == qi)
        def _():
            update(True)
            out = jnp.zeros((B, CH), F32)
            for h in range(4):
                a = acc[h]
                l = _lane_col(a, AUG_F)
                out = out + _exact_dot(a / l, p_ref[h], _NT, "a")
                lse = m_s[h] + jnp.log(l)
                qb_ref[h] = _put_pieces(qa_ref[h].astype(F32), AUG_LSE, lse, -1.0).astype(qb_ref.dtype)
            o_ref[...] = out

    qspec = pl.BlockSpec((4, B, LANE), lambda s, qt, kt: (0, qt[s], 0))
    kspec = pl.BlockSpec((4, B, LANE), lambda s, qt, kt: (0, kt[s], 0))
    grid_spec = pltpu.PrefetchScalarGridSpec(
        num_scalar_prefetch=2, grid=(qt.shape[0],),
        in_specs=[qspec, kspec, kspec, pl.BlockSpec(sel.shape, lambda s, qt, kt: (0, 0, 0))],
        out_specs=[pl.BlockSpec((B, CH), lambda s, qt, kt: (qt[s], 0)), qspec],
        scratch_shapes=[pltpu.VMEM((4, B, 1), F32), pltpu.VMEM((4, B, LANE), F32)])
    return pl.pallas_call(
        body, name=name, grid_spec=grid_spec,
        out_shape=[jax.ShapeDtypeStruct((T, CH), F32), jax.ShapeDtypeStruct((4, T, LANE), _MXU)],
        compiler_params=_cp("arbitrary"))(qt, kt, qa, ka, va, sel)


def _fox2_bwd_prep(o, do, sel, name):
    T = o.shape[0]
    tm = _tile(T, ROW_TILE)

    def body(o_ref, d_ref, p_ref, out_ref):
        dov = d_ref[...]
        prod = o_ref[...] * dov
        dob = dov.astype(_MXU)
        for h in range(4):
            delta = jnp.sum(jnp.where(_lane_mask(CH, h), prod, 0.0), axis=-1, keepdims=True)
            out_ref[h] = _put_pieces(_dot(dob, p_ref[h]), AUG_F, delta, -1.0).astype(out_ref.dtype)

    row = pl.BlockSpec((tm, CH), lambda i: (i, 0))
    return pl.pallas_call(
        body, name=name, grid=(T // tm,),
        in_specs=[row, row, pl.BlockSpec(sel.shape, lambda i: (0, 0, 0))],
        out_specs=pl.BlockSpec((4, tm, LANE), lambda i: (0, i, 0)),
        out_shape=jax.ShapeDtypeStruct((4, T, LANE), _MXU), compiler_params=_cp("parallel"))(o, do, sel)


def _fox2_bwd_dq(qb, ka, va, doa, sel, name):
    T = qb.shape[1]
    B = _tile(T, FOX_B)
    n = T // B
    qt, kt = _tri_steps(n, False)

    def body(qt_ref, kt_ref, qb_ref, ka_ref, va_ref, do_ref, p_ref, dq_ref, dfq_ref, dq_s):
        step = pl.program_id(0)
        qi, ki = qt_ref[step], kt_ref[step]

        @pl.when(ki == 0)
        def _():
            dq_s[...] = jnp.zeros_like(dq_s)

        def update(diag):
            for h in range(4):
                s = _mxu_dot(qb_ref[h], ka_ref[h], _NT)
                if diag:
                    s = _causal(s, False)
                ds = jnp.exp(s) * _mxu_dot(do_ref[h], va_ref[h], _NT)
                dq_s[h] += _dot(ds, ka_ref[h])

        @pl.when(ki < qi)
        def _():
            update(False)

        @pl.when(ki == qi)
        def _():
            update(True)
            lane = lax.broadcasted_iota(jnp.int32, (1, LANE), 1)
            out = jnp.zeros((B, CH), F32)
            dfq = jnp.zeros((B, LANE), F32)
            for h in range(4):
                out = out + _exact_dot(dq_s[h] * (HEAD ** -0.5), p_ref[h], _NT, "a")
                dfq = jnp.where(lane == h, _lane_col(dq_s[h], AUG_F), dfq)
            dq_ref[...] = out
            dfq_ref[...] = dfq

    qspec = pl.BlockSpec((4, B, LANE), lambda s, qt, kt: (0, qt[s], 0))
    kspec = pl.BlockSpec((4, B, LANE), lambda s, qt, kt: (0, kt[s], 0))
    grid_spec = pltpu.PrefetchScalarGridSpec(
        num_scalar_prefetch=2, grid=(qt.shape[0],),
        in_specs=[qspec, kspec, kspec, qspec, pl.BlockSpec(sel.shape, lambda s, qt, kt: (0, 0, 0))],
        out_specs=[pl.BlockSpec((B, CH), lambda s, qt, kt: (qt[s], 0)),
                   pl.BlockSpec((B, LANE), lambda s, qt, kt: (qt[s], 0))],
        scratch_shapes=[pltpu.VMEM((4, B, LANE), F32)])
    return pl.pallas_call(
        body, name=name, grid_spec=grid_spec,
        out_shape=[jax.ShapeDtypeStruct((T, CH), F32), jax.ShapeDtypeStruct((T, LANE), F32)],
        compiler_params=_cp("arbitrary"))(qt, kt, qb, ka, va, doa, sel)


def _fox2_bwd_dkv(qb, ka, va, doa, sel, name):
    T = qb.shape[1]
    B = _tile(T, FOX_B)
    n = T // B
    qt, kt = _tri_steps(n, True)

    def body(qt_ref, kt_ref, qb_ref, ka_ref, va_ref, do_ref, p_ref, dk_ref, dv_ref, df_ref, dk_s, dv_s):
        step = pl.program_id(0)
        qi, ki = qt_ref[step], kt_ref[step]

        @pl.when(qi == ki)
        def _():
            dk_s[...] = jnp.zeros_like(dk_s)
            dv_s[...] = jnp.zeros_like(dv_s)

        def update(diag):
            for h in range(4):
                st = _mxu_dot(ka_ref[h], qb_ref[h], _NT)
                if diag:
                    st = _causal(st, True)
                pt = jnp.exp(st)
                dst = pt * _mxu_dot(va_ref[h], do_ref[h], _NT)
                dv_s[h] += _dot(pt, do_ref[h])
                dk_s[h] += _dot(dst, qb_ref[h])

        @pl.when(qi == ki)
        def _():
            update(True)

        @pl.when(qi > ki)
        def _():
            update(False)

        @pl.when(qi == n - 1)
        def _():
            lane = lax.broadcasted_iota(jnp.int32, (1, LANE), 1)
            dk = jnp.zeros((B, CH), F32)
            dv = jnp.zeros((B, CH), F32)
            dfk = jnp.zeros((B, LANE), F32)
            for h in range(4):
                dk = dk + _exact_dot(dk_s[h], p_ref[h], _NT, "a")
                dv = dv + _exact_dot(dv_s[h], p_ref[h], _NT, "a")
                dfk = jnp.where(lane == h, -_lane_col(dk_s[h], AUG_ONE), dfk)
            dk_ref[...] = dk
            dv_ref[...] = dv
            df_ref[...] = dfk

    qspec = pl.BlockSpec((4, B, LANE), lambda s, qt, kt: (0, qt[s], 0))
    kspec = pl.BlockSpec((4, B, LANE), lambda s, qt, kt: (0, kt[s], 0))
    ospec = pl.BlockSpec((B, CH), lambda s, qt, kt: (kt[s], 0))
    grid_spec = pltpu.PrefetchScalarGridSpec(
        num_scalar_prefetch=2, grid=(qt.shape[0],),
        in_specs=[qspec, kspec, kspec, qspec, pl.BlockSpec(sel.shape, lambda s, qt, kt: (0, 0, 0))],
        out_specs=[ospec, ospec, pl.BlockSpec((B, LANE), lambda s, qt, kt: (kt[s], 0))],
        scratch_shapes=[pltpu.VMEM((4, B, LANE), F32), pltpu.VMEM((4, B, LANE), F32)])
    return pl.pallas_call(
        body, name=name, grid_spec=grid_spec,
        out_shape=[jax.ShapeDtypeStruct((T, CH), F32), jax.ShapeDtypeStruct((T, CH), F32),
                   jax.ShapeDtypeStruct((T, LANE), F32)],
        compiler_params=_cp("arbitrary"))(qt, kt, qb, ka, va, doa, sel)


def _fox2_post(zf, dqn, dkn, dv, dfq, dfk, gq, gk, bf, name):
    T = zf.shape[0]
    tm = _tile(T, FOX_TM)
    nt = T // tm
    upper = jnp.asarray(_tri(tm, False), _MXU)

    def body(z_ref, dq_ref, dk_ref, dv_ref, dfq_ref, df_ref, gq_ref, gk_ref, bf_ref, u_ref, dz_ref, sm_ref, carry, rc_s):
        @pl.when(pl.program_id(0) == 0)
        def _():
            carry[...] = jnp.zeros_like(carry)
            sm_ref[...] = jnp.zeros_like(sm_ref)

        z = z_ref[...]
        q_raw, k_raw = z[:, :CH], z[:, CH:2 * CH]
        _, q_r = _head_rms(q_raw, gq_ref[...], 4)
        _, k_r = _head_rms(k_raw, gk_ref[...], 4)
        dq, dgq = _head_rms_bwd(dq_ref[...], q_raw, q_r, gq_ref[...], 4)
        dk, dgk = _head_rms_bwd(dk_ref[...], k_raw, k_r, gk_ref[...], 4)
        rc_s[...] = _exact_dot(u_ref[...], dfq_ref[...] + df_ref[...], _NN, "b") + carry[pl.ds(0, 1), :]
        carry[pl.ds(0, 1), :] = rc_s[pl.ds(0, 1), :]
        lane = lax.broadcasted_iota(jnp.int32, (1, LANE), 1)
        x = z[:, 3 * CH:] + bf_ref[...]
        dff = jnp.where(lane < 4, rc_s[...] * _sigmoid(-x), 0.0)
        dz_ref[:, :CH] = dq.astype(dz_ref.dtype)
        dz_ref[:, CH:2 * CH] = dk.astype(dz_ref.dtype)
        dz_ref[:, 2 * CH:3 * CH] = dv_ref[...].astype(dz_ref.dtype)
        dz_ref[:, 3 * CH:] = dff.astype(dz_ref.dtype)
        sm_ref[pl.ds(0, 1), :] += dgq
        sm_ref[pl.ds(1, 1), :] += dgk
        sm_ref[pl.ds(2, 1), :LANE] += jnp.sum(dff, axis=0, keepdims=True)

    rev = lambda i: (nt - 1 - i, 0)
    row = pl.BlockSpec((tm, CH), rev)
    lrow = pl.BlockSpec((tm, LANE), rev)
    vec = pl.BlockSpec((1, CH), lambda i: (0, 0))
    return pl.pallas_call(
        body, name=name, grid=(nt,),
        in_specs=[pl.BlockSpec((tm, 3 * CH + LANE), rev), row, row, row, lrow, lrow, vec, vec,
                  pl.BlockSpec((1, LANE), lambda i: (0, 0)), pl.BlockSpec((tm, tm), lambda i: (0, 0))],
        out_specs=[pl.BlockSpec((tm, 3 * CH + LANE), rev), pl.BlockSpec((8, CH), lambda i: (0, 0))],
        out_shape=[jax.ShapeDtypeStruct((T, 3 * CH + LANE), _MXU), jax.ShapeDtypeStruct((8, CH), F32)],
        scratch_shapes=[pltpu.VMEM((8, LANE), F32), pltpu.VMEM((tm, LANE), F32)],
        compiler_params=_cp("arbitrary"))(zf, dqn, dkn, dv, dfq, dfk, gq, gk, bf, upper)


def _merge_fwd(acts, zg, wbr, wout, x1, name):
    T, D = x1.shape
    tm = _tile(T, 256)

    def body(a0, a1, a2, a3, zg_ref, wbr_ref, wout_ref, x_ref, o_ref, mg_ref):
        merged = None
        for i, a_ref in enumerate((a0, a1, a2, a3)):
            term = _sigmoid(zg_ref[:, i * D:(i + 1) * D]) * _dot(a_ref[...], wbr_ref[i])
            merged = term if merged is None else merged + term
        mg_ref[...] = merged.astype(mg_ref.dtype)
        o_ref[...] = x_ref[...] + _dot(merged, wout_ref[...])

    arow = pl.BlockSpec((tm, CH), lambda i: (i, 0))
    xrow = pl.BlockSpec((tm, D), lambda i: (i, 0))
    return pl.pallas_call(
        body, name=name, grid=(T // tm,),
        in_specs=[arow, arow, arow, arow, pl.BlockSpec((tm, 4 * D), lambda i: (i, 0)),
                  pl.BlockSpec((4, CH, D), lambda i: (0, 0, 0)), pl.BlockSpec((D, D), lambda i: (0, 0)), xrow],
        out_specs=[xrow, xrow],
        out_shape=[jax.ShapeDtypeStruct((T, D), F32), jax.ShapeDtypeStruct((T, D), _MXU)],
        compiler_params=_cp("parallel"))(*acts, zg, wbr, wout, x1)


def _merge_bwd(dx2, acts, zg, wbr, wout, name):
    T, D = dx2.shape
    tm = _tile(T, 256)
    nt = T // tm

    def body(dx_ref, a0, a1, a2, a3, zg_ref, wbr_ref, wout_ref, d0, d1, d2, d3, dzg_ref, dw_ref, dw_s):
        i = pl.program_id(0)

        @pl.when(i == 0)
        def _():
            dw_s[...] = jnp.zeros_like(dw_s)

        dm = _dot(dx_ref[...], wout_ref[...], _NT)
        for b, (a_ref, d_ref) in enumerate(((a0, d0), (a1, d1), (a2, d2), (a3, d3))):
            av = a_ref[...].astype(_MXU)
            g = _sigmoid(zg_ref[:, b * D:(b + 1) * D])
            p = _dot(av, wbr_ref[b])
            dzg_ref[:, b * D:(b + 1) * D] = (dm * p * (g * (1.0 - g))).astype(dzg_ref.dtype)
            dp = (dm * g).astype(_MXU)
            d_ref[...] = _dot(dp, wbr_ref[b], _NT)
            dw_s[b] += _dot(av, dp, _TN)

        @pl.when(i == nt - 1)
        def _():
            dw_ref[...] = dw_s[...].astype(dw_ref.dtype)

    arow = pl.BlockSpec((tm, CH), lambda i: (i, 0))
    xrow = pl.BlockSpec((tm, D), lambda i: (i, 0))
    grow = pl.BlockSpec((tm, 4 * D), lambda i: (i, 0))
    wspec = pl.BlockSpec((4, CH, D), lambda i: (0, 0, 0))
    ash = jax.ShapeDtypeStruct((T, CH), F32)
    return pl.pallas_call(
        body, name=name, grid=(nt,),
        in_specs=[xrow, arow, arow, arow, arow, grow, wspec, pl.BlockSpec((D, D), lambda i: (0, 0))],
        out_specs=[arow, arow, arow, arow, grow, wspec],
        out_shape=[ash, ash, ash, ash, jax.ShapeDtypeStruct((T, 4 * D), _MXU), jax.ShapeDtypeStruct((4, CH, D), _MXU)],
        scratch_shapes=[pltpu.VMEM((4, CH, D), F32)],
        compiler_params=_cp("arbitrary"))(dx2, *acts, zg, wbr, wout)


def _rows_2d(a):
    return a.reshape((-1, a.shape[-1])) if a.ndim > 1 else a.reshape((1, -1))


def _row_tile(rows, cols, n_bufs):
    padded = -(-cols // LANE) * LANE
    cap = max(8, (VMEM_LIMIT // 3) // (2 * n_bufs * 4 * padded))
    return _tile(rows, cap, 8)


def _sum8(recv, name):
    shape = recv.shape[1:]
    r2 = recv.reshape((N_DEV, -1, shape[-1]))
    rows, cols = r2.shape[1:]
    tr = _row_tile(rows, cols, N_DEV // 2 + 1)

    def body(r_ref, o_ref):
        acc = r_ref[0].astype(F32)
        for d in range(1, N_DEV):
            acc = acc + r_ref[d].astype(F32)
        o_ref[...] = acc

    out = pl.pallas_call(
        body, name=name, grid=(rows // tr,),
        in_specs=[pl.BlockSpec((N_DEV, tr, cols), lambda i: (0, i, 0))],
        out_specs=pl.BlockSpec((tr, cols), lambda i: (i, 0)),
        out_shape=jax.ShapeDtypeStruct((rows, cols), F32), compiler_params=_cp("parallel"))(r2)
    return out.reshape(shape)


def _adamw(w, g, m, v, name):
    shape = w.shape
    w2, g2, m2, v2 = (_rows_2d(a) for a in (w, g, m, v))
    rows, cols = w2.shape
    tr = _row_tile(rows, cols, 7)

    def body(w_ref, g_ref, m_ref, v_ref, d_ref, nm_ref, nv_ref):
        gv = g_ref[...]
        nm = ADAM_B1 * m_ref[...] + (1.0 - ADAM_B1) * gv
        nv = ADAM_B2 * v_ref[...] + (1.0 - ADAM_B2) * jnp.square(gv)
        m_hat = nm / (1.0 - ADAM_B1 ** ADAM_STEP)
        v_hat = nv / (1.0 - ADAM_B2 ** ADAM_STEP)
        d_ref[...] = -ADAM_LR * (m_hat / (jnp.sqrt(v_hat) + ADAM_EPS) + ADAM_WD * w_ref[...])
        nm_ref[...] = nm
        nv_ref[...] = nv

    spec = pl.BlockSpec((tr, cols), lambda i: (i, 0))
    osh = jax.ShapeDtypeStruct((rows, cols), F32)
    outs = pl.pallas_call(
        body, name=name, grid=(rows // tr,), in_specs=[spec] * 4, out_specs=[spec] * 3,
        out_shape=[osh] * 3, compiler_params=_cp("parallel"))(w2, g2, m2, v2)
    return tuple(o.reshape(shape) for o in outs)


def _exchange(items, name):
    n = len(items)
    widths, out_shapes = [], []
    for src, kind, ax in items:
        if kind == "gather":
            w = src.shape[ax]
            shp = list(src.shape)
            shp[ax] = N_DEV * w
        else:
            w = src.shape[ax] // N_DEV
            shp = list(src.shape)
            shp[ax] = w
            shp = [N_DEV] + shp
        widths.append(w)
        out_shapes.append(jax.ShapeDtypeStruct(tuple(shp), src.dtype))

    def body(*refs):
        srcs, outs = refs[:n], refs[n:2 * n]
        send, recv, lsem = refs[2 * n:]
        x, y, c = lax.axis_index("x"), lax.axis_index("y"), lax.axis_index("c")
        me = 4 * x + 2 * y + c

        def peer(k):
            b = k + 1
            px = 1 - x if b & 4 else x
            py = 1 - y if b & 2 else y
            pc = 1 - c if b & 1 else c
            return (px, py, pc), 4 * px + 2 * py + pc

        def win(ref, ax, idx, w):
            return ref.at[tuple([slice(None)] * ax + [pl.ds(idx * w, w)])]

        def ends(j, mine, theirs):
            _, kind, ax = items[j]
            if kind == "gather":
                return srcs[j], win(outs[j], ax, mine, widths[j])
            return win(srcs[j], ax, theirs, widths[j]), outs[j].at[mine]

        local, sent = [], []
        for j in range(n):
            s, d = ends(j, me, me)
            cp = pltpu.make_async_copy(s, d, lsem.at[j])
            cp.start()
            local.append(cp)
            for k in range(N_DEV - 1):
                dev, pid = peer(k)
                s, d = ends(j, me, pid)
                cp = pltpu.make_async_remote_copy(s, d, send.at[j, k], recv.at[j, k], device_id=dev,
                                                  device_id_type=pl.DeviceIdType.MESH)
                cp.start()
                sent.append(cp)
        for j in range(n):
            for k in range(N_DEV - 1):
                dev, pid = peer(k)
                s, d = ends(j, pid, me)
                pltpu.make_async_remote_copy(s, d, send.at[j, k], recv.at[j, k], device_id=dev,
                                             device_id_type=pl.DeviceIdType.MESH).wait_recv()
        for cp in sent:
            cp.wait_send()
        for cp in local:
            cp.wait()

    hbm = pl.BlockSpec(memory_space=pl.ANY)
    return pl.pallas_call(
        body, name=name, in_specs=[hbm] * n, out_specs=[hbm] * n, out_shape=out_shapes,
        scratch_shapes=[pltpu.SemaphoreType.DMA((n, N_DEV - 1)), pltpu.SemaphoreType.DMA((n, N_DEV - 1)),
                        pltpu.SemaphoreType.DMA((n,))],
        compiler_params=pltpu.CompilerParams(has_side_effects=True))(*[it[0] for it in items])


def _exchange_plan(items):
    widths, out_shapes = [], []
    for src, kind, ax in items:
        shp = list(src.shape)
        if kind == "gather":
            w = src.shape[ax]
            shp[ax] = N_DEV * w
        else:
            w = src.shape[ax] // N_DEV
            shp[ax] = w
            shp = [N_DEV] + shp
        widths.append(w)
        out_shapes.append((tuple(shp), src.dtype))
    return widths, out_shapes


def _exchange_refs(items, widths, srcs, outs):
    x, y, c = lax.axis_index("x"), lax.axis_index("y"), lax.axis_index("c")
    me = 4 * x + 2 * y + c

    def peer(k):
        b = k + 1
        px = 1 - x if b & 4 else x
        py = 1 - y if b & 2 else y
        pc = 1 - c if b & 1 else c
        return (px, py, pc), 4 * px + 2 * py + pc

    def win(ref, ax, idx, w):
        return ref.at[tuple([slice(None)] * ax + [pl.ds(idx * w, w)])]

    def ends(j, mine, theirs):
        _, kind, ax = items[j]
        if kind == "gather":
            return srcs[j], win(outs[j], ax, mine, widths[j])
        return win(srcs[j], ax, theirs, widths[j]), outs[j].at[mine]

    return me, peer, ends


_HBM = pl.BlockSpec(memory_space=pltpu.HBM)
_SEM = pl.BlockSpec(memory_space=pltpu.SEMAPHORE)


def _exchange_start(items, name):
    n = len(items)
    widths, out_shapes = _exchange_plan(items)
    meta = [(None, kind, ax) for _, kind, ax in items]

    def body(*refs):
        srcs, lands = refs[:n], refs[n:2 * n]
        send, recv, lsem = refs[2 * n], refs[2 * n + 1], refs[2 * n + 2]
        token = refs[-1]
        me, peer, ends = _exchange_refs(meta, widths, srcs, lands)
        for j in range(n):
            for k in range(N_DEV - 1):
                dev, pid = peer(k)
                s, d = ends(j, me, pid)
                q = j * (N_DEV - 1) + k
                pltpu.make_async_remote_copy(s, d, send.at[q], recv.at[q], device_id=dev,
                                             device_id_type=pl.DeviceIdType.MESH).start()
        for j in range(n):
            s, d = ends(j, me, me)
            pltpu.make_async_copy(s, d, lsem.at[j]).start()
        token[...] = jnp.zeros_like(token)

    srcs = [pltpu.with_memory_space_constraint(it[0], pltpu.HBM) for it in items]
    lands = [pltpu.with_memory_space_constraint(lax.empty(shp, dt), pltpu.HBM) for shp, dt in out_shapes]
    outs = pl.pallas_call(
        body, name=name,
        out_shape=(pltpu.SemaphoreType.DMA((n * (N_DEV - 1),)), pltpu.SemaphoreType.DMA((n * (N_DEV - 1),)),
                   pltpu.SemaphoreType.DMA((n,)),
                   *[pltpu.HBM(s.shape, s.dtype) for s in srcs], *[pltpu.HBM(shp, dt) for shp, dt in out_shapes],
                   jax.ShapeDtypeStruct((8, LANE), F32)),
        in_specs=[_HBM] * (2 * n),
        out_specs=(_SEM, _SEM, _SEM, *([_HBM] * (2 * n)), pl.BlockSpec(memory_space=pltpu.VMEM)),
        input_output_aliases={i: 3 + i for i in range(2 * n)},
        compiler_params=pltpu.CompilerParams(has_side_effects=pltpu.SideEffectType.DATAFLOW_SIDE_EFFECTING),
    )(*srcs, *lands)
    handle = (meta, widths, outs[0], outs[1], outs[2], outs[3:3 + n], outs[3 + n:3 + 2 * n])
    return handle, outs[-1]


def _exchange_wait(handle, after, name):
    meta, widths, send_sem, recv_sem, local_sem, src_thru, land_thru = handle
    n = len(meta)

    def body(*refs):
        srcs, lands = refs[:n], refs[n:2 * n]
        send, recv, lsem = refs[2 * n], refs[2 * n + 1], refs[2 * n + 2]
        me, peer, ends = _exchange_refs(meta, widths, srcs, lands)
        for j in range(n):
            for k in range(N_DEV - 1):
                dev, pid = peer(k)
                q = j * (N_DEV - 1) + k
                s, d = ends(j, me, pid)
                pltpu.make_async_remote_copy(s, d, send.at[q], recv.at[q], device_id=dev,
                                             device_id_type=pl.DeviceIdType.MESH).wait_send()
                s, d = ends(j, pid, me)
                pltpu.make_async_remote_copy(s, d, send.at[q], recv.at[q], device_id=dev,
                                             device_id_type=pl.DeviceIdType.MESH).wait_recv()
        for j in range(n):
            s, d = ends(j, me, me)
            pltpu.make_async_copy(s, d, lsem.at[j]).wait()

    outs = pl.pallas_call(
        body, name=name,
        out_shape=tuple(pltpu.HBM(a.shape, a.dtype) for a in (*src_thru, *land_thru)),
        in_specs=[_HBM] * (2 * n) + [_SEM, _SEM, _SEM, pl.BlockSpec(memory_space=pl.ANY)],
        out_specs=tuple([_HBM] * (2 * n)),
        input_output_aliases={i: i for i in range(2 * n)},
        compiler_params=pltpu.CompilerParams(has_side_effects=pltpu.SideEffectType.DATAFLOW_SIDE_EFFECTING),
    )(*src_thru, *land_thru, send_sem, recv_sem, local_sem, after)
    return list(outs[n:])


def _pack(arrs):
    flat = jnp.concatenate([a.reshape(-1).astype(F32) for a in arrs])
    n = flat.shape[0]
    rows = -(-n // (8 * LANE)) * 8
    return jnp.pad(flat, (0, rows * LANE - n)).reshape(rows, LANE)


def _unpack(buf, shapes):
    flat = buf.reshape(-1)
    out, off = [], 0
    for s in shapes:
        sz = int(np.prod(s))
        out.append(flat[off:off + sz].reshape(s))
        off += sz
    return out


def _pad_axis(a, axis, size):
    pad = [(0, 0)] * a.ndim
    pad[axis] = (0, size - a.shape[axis])
    return jnp.pad(a, pad)


def _ffn_forward(x, g, wg, wu, wd, tag):
    a = _rms_fwd(x, g, f"{tag}_rms")
    gate, up, hid = _ffn_up(a, wg, wu, f"{tag}_up")
    out = _mm([(hid, wd)], "nn", F32, f"{tag}_down", scale=0.5, res=x)
    return out, (x, a, gate, up, hid)


def _ffn_backward(dxp, saved, g, wg, wu, wd, tag):
    x, a, gate, up, hid = saved
    d_gate, d_up = _ffn_bwd_hid(dxp, wd, gate, up, f"{tag}_bwd_hid")
    d_wd = _mm([(hid, dxp)], "tn", _MXU, f"{tag}_dwd", scale=0.5, tk=2048)
    d_wg = _mm([(a, d_gate)], "tn", _MXU, f"{tag}_dwg", tk=2048)
    d_wu = _mm([(a, d_up)], "tn", _MXU, f"{tag}_dwu", tk=2048)
    d_a = _mm([(d_gate, wg), (d_up, wu)], "nt", F32, f"{tag}_da")
    dx, dg = _rms_bwd(d_a, x, g, dxp, f"{tag}_rms_bwd")
    return dx, dg, d_wg, d_wu, d_wd


def _tile_vec(v, reps):
    return jnp.tile(v.reshape(1, -1), (1, reps))


def _mixer_forward(x1, p, consts, tag):
    h = _rms_fwd(x1, p["mix_norm"], f"{tag}_rms")
    zg = _mm([(h, p["w_zg"])], "nn", F32, f"{tag}_in_g")
    zc = _mm([(h, p["w_conf"])], "nn", F32, f"{tag}_in_c")
    zs = _mm([(h, p["w_sc"])], "nn", F32, f"{tag}_in_s")
    zw = _mm([(h, p["w_swa"])], "nn", F32, f"{tag}_in_w")
    zf = _mm([(h, p["w_fox"])], "nn", F32, f"{tag}_in_f")
    u1, act_c = _conf_fwd(zc, p["conf_dw"], p["conf_dw_b"], p["conf_ln_g"], p["conf_ln_b"], f"{tag}_conf")
    act_s = _sc_fwd(zs, p["sc_conv"], f"{tag}_sc")
    act_w = _swa_fwd(zw, p["swa_q_norm"], p["swa_k_norm"], p["swa_sink"], consts["bias"], consts["expand"], f"{tag}_swa")
    qa, ka, va = _fox2_prep(zf, p["fox_q_norm"], p["fox_k_norm"], p["b_forget"], consts["sel"], f"{tag}_fox_prep")
    act_f, qb = _fox2_fwd(qa, ka, va, consts["sel"], f"{tag}_fox")
    acts = (act_c, act_s, act_w, act_f)
    x2, merged = _merge_fwd(acts, zg, p["w_br"], p["w_out"], x1, f"{tag}_merge")
    saved = (x1, h, zg, zc, zs, zw, zf, u1, acts, qb, ka, va, merged)
    return x2, saved


def _mixer_backward(dx2, saved, p, consts, tag):
    x1, h, zg, zc, zs, zw, zf, u1, acts, qb, ka, va, merged = saved
    g = {}
    g["w_out"] = _mm([(merged, dx2)], "tn", _MXU, f"{tag}_dwout", tk=2048)
    d_c, d_s, d_w, d_f, dzg, g["w_br"] = _merge_bwd(dx2, acts, zg, p["w_br"], p["w_out"], f"{tag}_merge_bwd")
    du1, sm_c = _conf_bwd_ln(d_c, u1, p["conf_ln_g"], p["conf_ln_b"], f"{tag}_conf_bwd_ln")
    dzc, g["conf_dw"] = _conf_bwd_conv(zc, du1, p["conf_dw"], f"{tag}_conf_bwd_conv")
    g["conf_ln_g"], g["conf_ln_b"], g["conf_dw_b"] = sm_c[0], sm_c[1], sm_c[2]
    dzs, g["sc_conv"] = _sc_bwd(zs, d_s, p["sc_conv"], f"{tag}_sc_bwd")
    dzw, dgq, dgk, g["swa_sink"], g["rel_bias"] = _swa_bwd(
        zw, d_w, p["swa_q_norm"], p["swa_k_norm"], p["swa_sink"], consts["bias"], consts["bucket"], consts["expand"],
        f"{tag}_swa_bwd")
    g["swa_q_norm"], g["swa_k_norm"] = dgq, dgk
    doa = _fox2_bwd_prep(acts[3], d_f, consts["sel"], f"{tag}_fox_bwd_prep")
    dqn, dfq = _fox2_bwd_dq(qb, ka, va, doa, consts["sel"], f"{tag}_fox_bwd_dq")
    dkn, dv, dfk = _fox2_bwd_dkv(qb, ka, va, doa, consts["sel"], f"{tag}_fox_bwd_dkv")
    dzf, sm_f = _fox2_post(zf, dqn, dkn, dv, dfq, dfk, p["fox_q_norm"], p["fox_k_norm"], p["b_forget"], f"{tag}_fox_post")
    g["fox_q_norm"], g["fox_k_norm"], g["b_forget"] = sm_f[0], sm_f[1], sm_f[2]
    parts = ((dzg, "w_zg"), (dzc, "w_conf"), (dzs, "w_sc"), (dzw, "w_swa"), (dzf, "w_fox"))
    for dz, wname in parts:
        g[wname] = _mm([(h, dz)], "tn", _MXU, f"{tag}_d{wname}", tk=2048)
    dh = _mm([(dz, p[wname]) for dz, wname in parts], "nt", F32, f"{tag}_dh", tm=512)
    dx1, g["mix_norm"] = _rms_bwd(dh, x1, p["mix_norm"], dx2, f"{tag}_rms_bwd")
    return dx1, g


W_NAMES = ['rel_bias', 'ffn1_norm', 'ffn1_w_gate', 'ffn1_w_up', 'ffn1_w_down', 'mix_norm', 'w_in', 'b_forget', 'conf_dw',
           'conf_dw_b', 'conf_ln_g', 'conf_ln_b', 'conf_w_out', 'sc_conv', 'sc_w_out', 'swa_q_norm', 'swa_k_norm',
           'swa_sink', 'swa_w_o', 'fox_q_norm', 'fox_k_norm', 'fox_w_o', 'w_out', 'ffn2_norm', 'ffn2_w_gate',
           'ffn2_w_up', 'ffn2_w_down']
SMALL = ['rel_bias', 'ffn1_norm', 'mix_norm', 'b_forget', 'conf_dw', 'conf_dw_b', 'conf_ln_g', 'conf_ln_b', 'sc_conv',
         'swa_q_norm', 'swa_k_norm', 'swa_sink', 'fox_q_norm', 'fox_k_norm', 'ffn2_norm']
BRANCH_W = ['conf_w_out', 'sc_w_out', 'swa_w_o', 'fox_w_o']
IN_CONF, IN_SC, IN_SWA, IN_FOX, IN_FF = (0, 512), (512, 1280), (1280, 1792), (1792, 2560), (2560, 2564)


def _step(w, m, v, x, loss_target):
    T, D = x.shape
    L = w["w_out"].shape[0]
    fs = w["ffn1_w_gate"].shape[2]
    fsp = -(-fs // LANE) * LANE
    dev = 4 * lax.axis_index("x") + 2 * lax.axis_index("y") + lax.axis_index("c")

    def cast(a):
        return a.astype(_MXU)

    win = w["w_in"]
    fox_cols = jnp.concatenate([win[..., IN_FOX[0]:IN_FF[1]],
                                jnp.zeros(win.shape[:2] + (LANE - (IN_FF[1] - IN_FF[0]),), win.dtype)], axis=-1)
    shards = {
        "ffn1_w_gate": (cast(_pad_axis(w["ffn1_w_gate"], 2, fsp)), 2),
        "ffn1_w_up": (cast(_pad_axis(w["ffn1_w_up"], 2, fsp)), 2),
        "ffn1_w_down": (cast(_pad_axis(w["ffn1_w_down"], 1, fsp)), 1),
        "ffn2_w_gate": (cast(_pad_axis(w["ffn2_w_gate"], 2, fsp)), 2),
        "ffn2_w_up": (cast(_pad_axis(w["ffn2_w_up"], 2, fsp)), 2),
        "ffn2_w_down": (cast(_pad_axis(w["ffn2_w_down"], 1, fsp)), 1),
        "w_zg": (cast(win[..., IN_FF[1]:]), 1),
        "w_conf": (cast(win[..., IN_CONF[0]:IN_CONF[1]]), 1),
        "w_sc": (cast(win[..., IN_SC[0]:IN_SC[1]]), 1),
        "w_swa": (cast(win[..., IN_SWA[0]:IN_SWA[1]]), 1),
        "w_fox": (cast(fox_cols), 1),
        "w_out": (cast(w["w_out"]), 1),
        "w_br": (cast(jnp.stack([w[n] for n in BRANCH_W], axis=1)), 3),
    }
    big = list(shards)
    conv_shard = jnp.concatenate([jnp.swapaxes(w["conf_dw"], 1, 2), jnp.swapaxes(w["sc_conv"], 1, 2)], axis=2)
    conv_full = jnp.swapaxes(_exchange([(conv_shard, "gather", 1)], "gather_conv")[0], 1, 2)
    conf_dw_full = _pad_axis(conv_full[:, :CONV_K], 1, CONV_HALO)
    sc_conv_full = _pad_axis(conv_full[:, CONV_K:], 1, SC_HALO)

    stages = (("ffn1", ["ffn1_w_gate", "ffn1_w_up", "ffn1_w_down"]),
              ("mix", ["w_zg", "w_conf", "w_sc", "w_swa", "w_fox", "w_out", "w_br"]),
              ("ffn2", ["ffn2_w_gate", "ffn2_w_up", "ffn2_w_down"]))
    first, rest = stages[0][1], stages[1][1] + stages[2][1]
    groups = [(l, st, names) for l in range(L) for st, names in stages]

    def depart(gi, dep):
        l, st, names = groups[gi]
        items = [(shards[n][0][l], "gather", shards[n][1] - 1) for n in names]
        if dep is not None:
            src0 = items[0][0]
            zero = (dep[(0,) * dep.ndim].astype(F32) * 0.0).astype(src0.dtype)
            items[0] = (src0 + zero,) + items[0][1:]
        return _exchange_start(items, f"gather_start_l{l}_{st}")

    bucket = jnp.asarray(_swa_bucket_matrix(min(SWA_TQ, T)))
    consts = {"bucket": bucket, "expand": jnp.asarray(_kv_expand_matrix(), _MXU),
              "sel": jnp.asarray(_head_select_matrix(), _MXU),
              "bias": _swa_bias(w["rel_bias"], bucket, "swa_bias")}

    def layer_params(l):
        p = {}
        for n in ("ffn1_norm", "mix_norm", "ffn2_norm", "conf_dw_b", "conf_ln_g", "conf_ln_b"):
            p[n] = w[n][l].reshape(1, -1)
        p["conf_dw"], p["sc_conv"] = conf_dw_full[l], sc_conv_full[l]
        p["swa_q_norm"], p["fox_q_norm"] = _tile_vec(w["swa_q_norm"][l], 4), _tile_vec(w["fox_q_norm"][l], 4)
        p["swa_k_norm"], p["fox_k_norm"] = _tile_vec(w["swa_k_norm"][l], 2), _tile_vec(w["fox_k_norm"][l], 4)
        p["swa_sink"] = w["swa_sink"][l].reshape(1, 4)
        p["b_forget"] = _pad_axis(w["b_forget"][l].reshape(1, 4), 1, LANE)
        return p

    params = [layer_params(l) for l in range(L)]
    saved = [[] for _ in range(L)]
    cur = x
    handle, tok = depart(0, None)
    for gi, (l, st, names) in enumerate(groups):
        p = params[l]
        got = _exchange_wait(handle, cur if gi else tok, f"gather_wait_l{l}_{st}")
        p.update(zip(names, got))
        zero = 0.0
        if gi + 1 < len(groups):
            handle, tok = depart(gi + 1, got[0])
            zero = tok[0:1, 0:1]
        if st == "ffn1":
            cur, s = _ffn_forward(cur, p["ffn1_norm"] + zero, p["ffn1_w_gate"], p["ffn1_w_up"], p["ffn1_w_down"], f"l{l}_ffn1")
        elif st == "mix":
            cur, s = _mixer_forward(cur, dict(p, mix_norm=p["mix_norm"] + zero), consts, f"l{l}_mix")
        else:
            cur, s = _ffn_forward(cur, p["ffn2_norm"] + zero, p["ffn2_w_gate"], p["ffn2_w_up"], p["ffn2_w_down"], f"l{l}_ffn2")
        saved[l].append(s)
    dcur, loss_part = _loss_grad(cur, loss_target)

    grads = [None] * L
    leaving = []

    def leave(l, st, g):
        names = dict(stages)[st]
        h, tok = _exchange_start([(g[n], "scatter", shards[n][1] - 1) for n in names], f"scatter_start_l{l}_{st}")
        leaving.append((l, st, names, h))
        return tok[0:1, 0:1]

    zero = 0.0
    for l in reversed(range(L)):
        p = params[l]
        s1, s2, s3 = saved[l]
        g = {}
        dcur, g["ffn2_norm"], g["ffn2_w_gate"], g["ffn2_w_up"], g["ffn2_w_down"] = _ffn_backward(
            dcur, s3, p["ffn2_norm"] + zero, p["ffn2_w_gate"], p["ffn2_w_up"], p["ffn2_w_down"], f"l{l}_ffn2")
        zero = leave(l, "ffn2", g)
        dcur, gm = _mixer_backward(dcur, s2, dict(p, mix_norm=p["mix_norm"] + zero), consts, f"l{l}_mix")
        g.update(gm)
        zero = leave(l, "mix", g)
        dcur, g["ffn1_norm"], g["ffn1_w_gate"], g["ffn1_w_up"], g["ffn1_w_down"] = _ffn_backward(
            dcur, s1, p["ffn1_norm"] + zero, p["ffn1_w_gate"], p["ffn1_w_up"], p["ffn1_w_down"], f"l{l}_ffn1")
        zero = leave(l, "ffn1", g)
        grads[l] = g
    grad_x = dcur

    gsum = {n: [None] * L for n in big}
    for l, st, names, h in leaving:
        for n, r in zip(names, _exchange_wait(h, grad_x, f"scatter_wait_l{l}_{st}")):
            gsum[n][l] = _sum8(r, f"sum_{n}_l{l}")
    gsum = {n: jnp.stack(parts) for n, parts in gsum.items()}
    gw = {}
    for n in ("ffn1_w_gate", "ffn1_w_up", "ffn2_w_gate", "ffn2_w_up"):
        gw[n] = gsum[n][:, :, :fs]
    for n in ("ffn1_w_down", "ffn2_w_down"):
        gw[n] = gsum[n][:, :fs, :]
    gw["w_out"] = gsum["w_out"]
    for i, n in enumerate(BRANCH_W):
        gw[n] = gsum["w_br"][:, i]
    gw["w_in"] = jnp.concatenate([gsum["w_conf"], gsum["w_sc"], gsum["w_swa"],
                                  gsum["w_fox"][..., :IN_FF[1] - IN_FOX[0]], gsum["w_zg"]], axis=-1)

    def small_partial(n):
        per_layer = [grads[l][n] for l in range(L)]
        if n == "rel_bias":
            return sum(pl_[:, :4] for pl_ in per_layer)
        if n in ("swa_sink", "b_forget"):
            return jnp.stack([a.reshape(-1)[:4] for a in per_layer])
        if n in ("swa_q_norm", "fox_q_norm", "fox_k_norm"):
            return jnp.stack([a.reshape(4, HEAD).sum(0) for a in per_layer])
        if n == "swa_k_norm":
            return jnp.stack([a.reshape(2, HEAD).sum(0) for a in per_layer])
        if n == "conf_dw":
            return jnp.stack([a[:CONV_K] for a in per_layer])
        if n == "sc_conv":
            return jnp.stack([a[:SC_K] for a in per_layer])
        return jnp.stack([a.reshape(-1) for a in per_layer])

    partial = [small_partial(n) for n in SMALL]
    small_shapes = [a.shape for a in partial]
    all_parts = _exchange([(_pack(partial), "gather", 0)], "gather_small_grads")[0]
    rows = all_parts.shape[0] // N_DEV
    small_sum = _unpack(_sum8(all_parts.reshape(N_DEV, rows, LANE), "sum_small"), small_shapes)
    for n, a in zip(SMALL, small_sum):
        if n in ("conf_dw", "sc_conv"):
            cs = w[n].shape[2]
            a = lax.dynamic_slice_in_dim(a, dev * cs, cs, axis=2)
        gw[n] = a

    delta, new_m, new_v = {}, {}, {}
    for n in W_NAMES:
        if n not in SMALL:
            delta[n], new_m[n], new_v[n] = _adamw(w[n], gw[n], m[n], v[n], f"adamw_{n}")
    shapes = [w[n].shape for n in SMALL]
    outs = _adamw(_pack([w[n] for n in SMALL]), _pack([gw[n] for n in SMALL]), _pack([m[n] for n in SMALL]),
                  _pack([v[n] for n in SMALL]), "adamw_small")
    for res, out in zip((delta, new_m, new_v), outs):
        for n, a in zip(SMALL, _unpack(out, shapes)):
            res[n] = a

    loss = lax.psum(loss_part[0, 0], ("x", "y", "c"))
    return loss, grad_x, gw, delta, new_m, new_v


def kernel(x, rel_bias, ffn1_norm, ffn1_w_gate, ffn1_w_up, ffn1_w_down, mix_norm, w_in, b_forget, conf_dw, conf_dw_b, conf_ln_g, conf_ln_b, conf_w_out, sc_conv, sc_w_out, swa_q_norm, swa_k_norm, swa_sink, swa_w_o, fox_q_norm, fox_k_norm, fox_w_o, w_out, ffn2_norm, ffn2_w_gate, ffn2_w_up, ffn2_w_down, loss_target, m_rel_bias, m_ffn1_norm, m_ffn1_w_gate, m_ffn1_w_up, m_ffn1_w_down, m_mix_norm, m_w_in, m_b_forget, m_conf_dw, m_conf_dw_b, m_conf_ln_g, m_conf_ln_b, m_conf_w_out, m_sc_conv, m_sc_w_out, m_swa_q_norm, m_swa_k_norm, m_swa_sink, m_swa_w_o, m_fox_q_norm, m_fox_k_norm, m_fox_w_o, m_w_out, m_ffn2_norm, m_ffn2_w_gate, m_ffn2_w_up, m_ffn2_w_down, v_rel_bias, v_ffn1_norm, v_ffn1_w_gate, v_ffn1_w_up, v_ffn1_w_down, v_mix_norm, v_w_in, v_b_forget, v_conf_dw, v_conf_dw_b, v_conf_ln_g, v_conf_ln_b, v_conf_w_out, v_sc_conv, v_sc_w_out, v_swa_q_norm, v_swa_k_norm, v_swa_sink, v_swa_w_o, v_fox_q_norm, v_fox_k_norm, v_fox_w_o, v_w_out, v_ffn2_norm, v_ffn2_w_gate, v_ffn2_w_up, v_ffn2_w_down):
    args = locals()
    w = {n: args[n] for n in W_NAMES}
    m = {n: args["m_" + n] for n in W_NAMES}
    v = {n: args["v_" + n] for n in W_NAMES}
    T, D = x.shape[-2:]
    loss, grad_x, gw, delta, new_m, new_v = _step(w, m, v, x.reshape(T, D), loss_target.reshape(T, D))
    return (loss, grad_x.reshape(x.shape), *[gw[n] for n in W_NAMES], *[delta[n] for n in W_NAMES],
            *[new_m[n] for n in W_NAMES], *[new_v[n] for n in W_NAMES])
```

```python
import math

import numpy as np
import jax
import jax.numpy as jnp
from jax import lax
from jax.experimental import pallas as pl
from jax.experimental.pallas import tpu as pltpu

F32 = jnp.float32
_MXU = jnp.bfloat16
EPS = 1e-6
NEG_INF = -1e30
HEAD = 64
CH = 256
WINDOW = 128
CONV_K = 31
SC_K = 3
CONV_HALO = 32
SC_HALO = 8
N_BUCKETS = 32
MAX_DISTANCE = 128
N_DEV = 8
LANE = 128
ROW_TILE = 512
VMEM_LIMIT = 48 * 1024 * 1024
ADAM_LR, ADAM_B1, ADAM_B2, ADAM_EPS, ADAM_WD, ADAM_STEP = 0.001, 0.9, 0.999, 1e-08, 0.01, 10

_NN = (((1,), (0,)), ((), ()))
_NT = (((1,), (1,)), ((), ()))
_TN = (((0,), (0,)), ((), ()))


def _cp(*sem):
    return pltpu.CompilerParams(dimension_semantics=sem, vmem_limit_bytes=VMEM_LIMIT)


def _tile(n, pref, align=LANE):
    t = (min(n, pref) // align) * align
    while t >= align:
        if n % t == 0:
            return t
        t -= align
    return n


def _dot(a, b, dims=_NN):
    return lax.dot_general(a.astype(_MXU), b.astype(_MXU), dims, preferred_element_type=F32)


def _split3(x):
    hi = x.astype(_MXU)
    r1 = x - hi.astype(F32)
    mid = r1.astype(_MXU)
    lo = (r1 - mid.astype(F32)).astype(_MXU)
    return hi, mid, lo


def _exact_dot(a, b, dims, data):
    if data == "a":
        return sum(lax.dot_general(p, b.astype(_MXU), dims, preferred_element_type=F32) for p in _split3(a))
    return sum(lax.dot_general(a.astype(_MXU), p, dims, preferred_element_type=F32) for p in _split3(b))


def _sigmoid(x):
    return jax.nn.sigmoid(x)


def _lane_mask(width, h):
    lane = lax.broadcasted_iota(jnp.int32, (1, width), 1)
    return (lane >= h * HEAD) & (lane < (h + 1) * HEAD)


def _head_rms(x, g, nh):
    xx = x * x
    ms = jnp.zeros_like(x)
    for h in range(nh):
        mk = _lane_mask(x.shape[-1], h)
        s = jnp.sum(jnp.where(mk, xx, 0.0), axis=-1, keepdims=True) * (1.0 / HEAD)
        ms = jnp.where(mk, s, ms)
    r = lax.rsqrt(ms + EPS)
    return x * r * g, r


def _head_rms_bwd(dy, x, r, g, nh):
    w = dy * g
    wx = w * x
    c = jnp.zeros_like(x)
    for h in range(nh):
        mk = _lane_mask(x.shape[-1], h)
        s = jnp.sum(jnp.where(mk, wx, 0.0), axis=-1, keepdims=True) * (1.0 / HEAD)
        c = jnp.where(mk, s, c)
    dx = r * w - x * (r * r * r) * c
    dg = jnp.sum(dy * x * r, axis=0, keepdims=True)
    return dx, dg


def _mm(pairs, mode, out_dtype, name, scale=None, res=None, tm=1024, tn=1024, tk=1024):
    a0, b0 = pairs[0]
    M = a0.shape[1] if mode == "tn" else a0.shape[0]
    N = b0.shape[0] if mode == "nt" else b0.shape[1]
    tm, tn = _tile(M, tm), _tile(N, tn)
    dims = {"nn": _NN, "nt": _NT, "tn": _TN}[mode]
    tks, nks, offs = [], [], []
    for a, _ in pairs:
        K = a.shape[0] if mode == "tn" else a.shape[1]
        t = _tile(K, tk)
        tks.append(t)
        nks.append(K // t)
        offs.append(sum(nks[:-1]))
    nk_tot = sum(nks)
    in_specs, args = [], []
    for (a, b), t, nk, off in zip(pairs, tks, nks, offs):
        def kk(k, off=off, nk=nk):
            return jnp.clip(k - off, 0, nk - 1)
        if mode == "tn":
            in_specs.append(pl.BlockSpec((t, tm), lambda i, j, k, kk=kk: (kk(k), i)))
        else:
            in_specs.append(pl.BlockSpec((tm, t), lambda i, j, k, kk=kk: (i, kk(k))))
        if mode == "nt":
            in_specs.append(pl.BlockSpec((tn, t), lambda i, j, k, kk=kk: (j, kk(k))))
        else:
            in_specs.append(pl.BlockSpec((t, tn), lambda i, j, k, kk=kk: (kk(k), j)))
        args += [a, b]
    if res is not None:
        in_specs.append(pl.BlockSpec((tm, tn), lambda i, j, k: (i, j)))
        args.append(res)
    npairs = len(pairs)

    def body(*refs):
        ab = refs[:2 * npairs]
        res_ref = refs[2 * npairs] if res is not None else None
        o_ref = refs[2 * npairs + (res is not None)]
        acc = refs[-1]
        k = pl.program_id(2)

        def finish(r):
            if scale is not None:
                r = r * scale
            if res_ref is not None:
                r = r + res_ref[...]
            o_ref[...] = r.astype(o_ref.dtype)

        if nk_tot == 1:
            finish(_dot(ab[0][...], ab[1][...], dims))
            return

        @pl.when(k == 0)
        def _():
            acc[...] = jnp.zeros_like(acc)

        for p in range(npairs):
            @pl.when(jnp.logical_and(k >= offs[p], k < offs[p] + nks[p]))
            def _(p=p):
                acc[...] += _dot(ab[2 * p][...], ab[2 * p + 1][...], dims)

        @pl.when(k == nk_tot - 1)
        def _():
            finish(acc[...])

    return pl.pallas_call(
        body, name=name, grid=(M // tm, N // tn, nk_tot), in_specs=in_specs,
        out_specs=pl.BlockSpec((tm, tn), lambda i, j, k: (i, j)),
        out_shape=jax.ShapeDtypeStruct((M, N), out_dtype),
        scratch_shapes=[pltpu.VMEM((tm, tn), F32)],
        compiler_params=_cp("parallel", "parallel", "arbitrary"))(*args)


def _rms_fwd(x, g, name):
    T, D = x.shape
    tm = _tile(T, ROW_TILE)

    def body(x_ref, g_ref, o_ref):
        xv = x_ref[...]
        r = lax.rsqrt(jnp.mean(xv * xv, axis=-1, keepdims=True) + EPS)
        o_ref[...] = (xv * r * g_ref[...]).astype(o_ref.dtype)

    return pl.pallas_call(
        body, name=name, grid=(T // tm,),
        in_specs=[pl.BlockSpec((tm, D), lambda i: (i, 0)), pl.BlockSpec((1, D), lambda i: (0, 0))],
        out_specs=pl.BlockSpec((tm, D), lambda i: (i, 0)),
        out_shape=jax.ShapeDtypeStruct((T, D), _MXU), compiler_params=_cp("parallel"))(x, g)


def _rms_bwd(da, x, g, dres, name):
    T, D = x.shape
    tm = _tile(T, ROW_TILE)

    def body(da_ref, x_ref, g_ref, dr_ref, dx_ref, dg_ref):
        @pl.when(pl.program_id(0) == 0)
        def _():
            dg_ref[...] = jnp.zeros_like(dg_ref)

        xv, dav = x_ref[...], da_ref[...]
        r = lax.rsqrt(jnp.mean(xv * xv, axis=-1, keepdims=True) + EPS)
        w = dav * g_ref[...]
        c = jnp.mean(w * xv, axis=-1, keepdims=True)
        dx_ref[...] = dr_ref[...] + (r * w - xv * (r * r * r) * c)
        dg_ref[...] += jnp.sum(dav * xv * r, axis=0, keepdims=True)

    row = pl.BlockSpec((tm, D), lambda i: (i, 0))
    vec = pl.BlockSpec((1, D), lambda i: (0, 0))
    return pl.pallas_call(
        body, name=name, grid=(T // tm,), in_specs=[row, row, vec, row], out_specs=[row, vec],
        out_shape=[jax.ShapeDtypeStruct((T, D), F32), jax.ShapeDtypeStruct((1, D), F32)],
        compiler_params=_cp("arbitrary"))(da, x, g, dres)


def _loss_grad(y, tgt):
    T, D = y.shape
    tm = _tile(T, ROW_TILE)

    def body(y_ref, t_ref, dy_ref, l_ref):
        @pl.when(pl.program_id(0) == 0)
        def _():
            l_ref[...] = jnp.zeros_like(l_ref)

        d = y_ref[...] - t_ref[...]
        dy_ref[...] = d * (1.0 / D)
        per_tok = jnp.mean(d * d, axis=-1, keepdims=True)
        l_ref[...] += 0.5 * jnp.sum(per_tok, axis=0, keepdims=True)

    row = pl.BlockSpec((tm, D), lambda i: (i, 0))
    return pl.pallas_call(
        body, name="loss_grad", grid=(T // tm,), in_specs=[row, row],
        out_specs=[row, pl.BlockSpec((1, 1), lambda i: (0, 0))],
        out_shape=[jax.ShapeDtypeStruct((T, D), F32), jax.ShapeDtypeStruct((1, 1), F32)],
        compiler_params=_cp("arbitrary"))(y, tgt)


def _ffn_up(a, wg, wu, name):
    T, D = a.shape
    Fp = wg.shape[1]
    tm, tn = _tile(T, ROW_TILE), _tile(Fp, 768)

    def body(a_ref, wg_ref, wu_ref, g_ref, u_ref, h_ref):
        av = a_ref[...]
        g = _dot(av, wg_ref[...])
        u = _dot(av, wu_ref[...])
        g_ref[...] = g.astype(g_ref.dtype)
        u_ref[...] = u.astype(u_ref.dtype)
        h_ref[...] = (g * _sigmoid(g) * u).astype(h_ref.dtype)

    wspec = pl.BlockSpec((D, tn), lambda j, i: (0, j))
    ospec = pl.BlockSpec((tm, tn), lambda j, i: (i, j))
    osh = jax.ShapeDtypeStruct((T, Fp), _MXU)
    return pl.pallas_call(
        body, name=name, grid=(Fp // tn, T // tm),
        in_specs=[pl.BlockSpec((tm, D), lambda j, i: (i, 0)), wspec, wspec],
        out_specs=[ospec, ospec, ospec], out_shape=[osh, osh, osh],
        compiler_params=_cp("parallel", "parallel"))(a, wg, wu)


def _ffn_bwd_hid(dxp, wd, gate, up, name):
    T, D = dxp.shape
    Fp = wd.shape[0]
    tm, tn = _tile(T, ROW_TILE), _tile(Fp, 768)

    def body(dx_ref, wd_ref, g_ref, u_ref, dg_ref, du_ref):
        dh = 0.5 * _dot(dx_ref[...], wd_ref[...], _NT)
        g = g_ref[...].astype(F32)
        u = u_ref[...].astype(F32)
        s = _sigmoid(g)
        du_ref[...] = (dh * (g * s)).astype(du_ref.dtype)
        dg_ref[...] = (dh * u * (s * (1.0 + g * (1.0 - s)))).astype(dg_ref.dtype)

    tspec = pl.BlockSpec((tm, tn), lambda j, i: (i, j))
    osh = jax.ShapeDtypeStruct((T, Fp), _MXU)
    return pl.pallas_call(
        body, name=name, grid=(Fp // tn, T // tm),
        in_specs=[pl.BlockSpec((tm, D), lambda j, i: (i, 0)), pl.BlockSpec((tn, D), lambda j, i: (j, 0)), tspec, tspec],
        out_specs=[tspec, tspec], out_shape=[osh, osh],
        compiler_params=_cp("parallel", "parallel"))(dxp, wd, gate, up)


def _conf_fwd(zc, dw, b, lng, lnb, name):
    T = zc.shape[0]
    tm = _tile(T, ROW_TILE)
    r = tm // CONV_HALO

    def body(z_ref, zh_ref, dw_ref, b_ref, g_ref, lb_ref, u1_ref, act_ref, ext):
        i = pl.program_id(0)
        cur = z_ref[...]
        ext[pl.ds(CONV_HALO, tm), :] = cur[:, :CH] * _sigmoid(cur[:, CH:])
        hal = zh_ref[...]
        ext[pl.ds(0, CONV_HALO), :] = jnp.where(i > 0, hal[:, :CH] * _sigmoid(hal[:, CH:]), 0.0)
        acc = jnp.zeros((tm, CH), F32)
        for k in range(CONV_K):
            acc = acc + dw_ref[pl.ds(k, 1), :] * ext[pl.ds(CONV_HALO - (CONV_K - 1) + k, tm), :]
        u1 = acc + b_ref[...]
        u1_ref[...] = u1
        mu = jnp.mean(u1, axis=-1, keepdims=True)
        var = jnp.mean(jnp.square(u1 - mu), axis=-1, keepdims=True)
        u2 = (u1 - mu) * lax.rsqrt(var + EPS) * g_ref[...] + lb_ref[...]
        act_ref[...] = u2 * _sigmoid(u2)

    vec = pl.BlockSpec((1, CH), lambda i: (0, 0))
    row = pl.BlockSpec((tm, CH), lambda i: (i, 0))
    osh = jax.ShapeDtypeStruct((T, CH), F32)
    return pl.pallas_call(
        body, name=name, grid=(T // tm,),
        in_specs=[pl.BlockSpec((tm, 2 * CH), lambda i: (i, 0)),
                  pl.BlockSpec((CONV_HALO, 2 * CH), lambda i: (jnp.maximum(i * r - 1, 0), 0)),
                  pl.BlockSpec((CONV_HALO, CH), lambda i: (0, 0)), vec, vec, vec],
        out_specs=[row, row], out_shape=[osh, osh],
        scratch_shapes=[pltpu.VMEM((tm + CONV_HALO, CH), F32)],
        compiler_params=_cp("parallel"))(zc, zc, dw, b, lng, lnb)


def _conf_bwd_ln(dact, u1, lng, lnb, name):
    T = u1.shape[0]
    tm = _tile(T, ROW_TILE)

    def body(da_ref, u_ref, g_ref, lb_ref, du_ref, sm_ref):
        @pl.when(pl.program_id(0) == 0)
        def _():
            sm_ref[...] = jnp.zeros_like(sm_ref)

        u1v = u_ref[...]
        mu = jnp.mean(u1v, axis=-1, keepdims=True)
        cen = u1v - mu
        rstd = lax.rsqrt(jnp.mean(cen * cen, axis=-1, keepdims=True) + EPS)
        y = cen * rstd
        u2 = y * g_ref[...] + lb_ref[...]
        s = _sigmoid(u2)
        du2 = da_ref[...] * (s * (1.0 + u2 * (1.0 - s)))
        dy = du2 * g_ref[...]
        du1 = rstd * (dy - jnp.mean(dy, axis=-1, keepdims=True) - y * jnp.mean(dy * y, axis=-1, keepdims=True))
        du_ref[...] = du1
        sm_ref[pl.ds(0, 1), :] += jnp.sum(du2 * y, axis=0, keepdims=True)
        sm_ref[pl.ds(1, 1), :] += jnp.sum(du2, axis=0, keepdims=True)
        sm_ref[pl.ds(2, 1), :] += jnp.sum(du1, axis=0, keepdims=True)

    vec = pl.BlockSpec((1, CH), lambda i: (0, 0))
    row = pl.BlockSpec((tm, CH), lambda i: (i, 0))
    return pl.pallas_call(
        body, name=name, grid=(T // tm,), in_specs=[row, row, vec, vec],
        out_specs=[row, pl.BlockSpec((8, CH), lambda i: (0, 0))],
        out_shape=[jax.ShapeDtypeStruct((T, CH), F32), jax.ShapeDtypeStruct((8, CH), F32)],
        compiler_params=_cp("arbitrary"))(dact, u1, lng, lnb)


def _conf_bwd_conv(zc, du1, dw, name):
    T = zc.shape[0]
    tm = _tile(T, ROW_TILE)
    r = tm // CONV_HALO
    nt = T // tm
    nh = T // CONV_HALO

    def body(z_ref, zh_ref, d_ref, dn_ref, dw_ref, dz_ref, ddw_ref, ext_u, ext_d):
        i = pl.program_id(0)

        @pl.when(i == 0)
        def _():
            ddw_ref[...] = jnp.zeros_like(ddw_ref)

        cur = z_ref[...]
        ca = cur[:, :CH]
        sg = _sigmoid(cur[:, CH:])
        ext_u[pl.ds(CONV_HALO, tm), :] = ca * sg
        hal = zh_ref[...]
        ext_u[pl.ds(0, CONV_HALO), :] = jnp.where(i > 0, hal[:, :CH] * _sigmoid(hal[:, CH:]), 0.0)
        d = d_ref[...]
        ext_d[pl.ds(0, tm), :] = d
        ext_d[pl.ds(tm, CONV_HALO), :] = jnp.where(i < nt - 1, dn_ref[...], 0.0)
        acc = jnp.zeros((tm, CH), F32)
        for k in range(CONV_K):
            acc = acc + dw_ref[pl.ds(k, 1), :] * ext_d[pl.ds(CONV_K - 1 - k, tm), :]
            ddw_ref[pl.ds(k, 1), :] += jnp.sum(
                d * ext_u[pl.ds(CONV_HALO - (CONV_K - 1) + k, tm), :], axis=0, keepdims=True)
        dz_ref[:, :CH] = (acc * sg).astype(dz_ref.dtype)
        dz_ref[:, CH:] = (acc * ca * sg * (1.0 - sg)).astype(dz_ref.dtype)

    return pl.pallas_call(
        body, name=name, grid=(nt,),
        in_specs=[pl.BlockSpec((tm, 2 * CH), lambda i: (i, 0)),
                  pl.BlockSpec((CONV_HALO, 2 * CH), lambda i: (jnp.maximum(i * r - 1, 0), 0)),
                  pl.BlockSpec((tm, CH), lambda i: (i, 0)),
                  pl.BlockSpec((CONV_HALO, CH), lambda i: (jnp.minimum((i + 1) * r, nh - 1), 0)),
                  pl.BlockSpec((CONV_HALO, CH), lambda i: (0, 0))],
        out_specs=[pl.BlockSpec((tm, 2 * CH), lambda i: (i, 0)), pl.BlockSpec((CONV_HALO, CH), lambda i: (0, 0))],
        out_shape=[jax.ShapeDtypeStruct((T, 2 * CH), _MXU), jax.ShapeDtypeStruct((CONV_HALO, CH), F32)],
        scratch_shapes=[pltpu.VMEM((tm + CONV_HALO, CH), F32), pltpu.VMEM((tm + CONV_HALO, CH), F32)],
        compiler_params=_cp("arbitrary"))(zc, zc, du1, du1, dw)


def _sc_fwd(zs, w, name):
    T = zs.shape[0]
    tm = _tile(T, ROW_TILE)
    r = tm // SC_HALO

    def body(z_ref, zh_ref, w_ref, act_ref, ext):
        i = pl.program_id(0)
        cur = z_ref[...]
        ext[pl.ds(SC_HALO, tm), :] = cur[:, CH:2 * CH] * cur[:, 2 * CH:]
        hal = zh_ref[...]
        ext[pl.ds(0, SC_HALO), :] = jnp.where(i > 0, hal[:, CH:2 * CH] * hal[:, 2 * CH:], 0.0)
        v1 = jnp.zeros((tm, CH), F32)
        for k in range(SC_K):
            v1 = v1 + w_ref[pl.ds(k, 1), :] * ext[pl.ds(SC_HALO - (SC_K - 1) + k, tm), :]
        act_ref[...] = cur[:, :CH] * v1

    return pl.pallas_call(
        body, name=name, grid=(T // tm,),
        in_specs=[pl.BlockSpec((tm, 3 * CH), lambda i: (i, 0)),
                  pl.BlockSpec((SC_HALO, 3 * CH), lambda i: (jnp.maximum(i * r - 1, 0), 0)),
                  pl.BlockSpec((SC_HALO, CH), lambda i: (0, 0))],
        out_specs=pl.BlockSpec((tm, CH), lambda i: (i, 0)),
        out_shape=jax.ShapeDtypeStruct((T, CH), F32),
        scratch_shapes=[pltpu.VMEM((tm + SC_HALO, CH), F32)],
        compiler_params=_cp("parallel"))(zs, zs, w)


def _sc_bwd(zs, dact, w, name):
    T = zs.shape[0]
    tm = _tile(T, ROW_TILE)
    r = tm // SC_HALO
    nt = T // tm
    nh = T // SC_HALO

    def body(z_ref, zh_ref, zn_ref, d_ref, dn_ref, w_ref, dz_ref, dw_ref, ext_v, ext_d):
        i = pl.program_id(0)

        @pl.when(i == 0)
        def _():
            dw_ref[...] = jnp.zeros_like(dw_ref)

        cur = z_ref[...]
        sb, sc, sx = cur[:, :CH], cur[:, CH:2 * CH], cur[:, 2 * CH:]
        ext_v[pl.ds(SC_HALO, tm), :] = sc * sx
        hal = zh_ref[...]
        ext_v[pl.ds(0, SC_HALO), :] = jnp.where(i > 0, hal[:, CH:2 * CH] * hal[:, 2 * CH:], 0.0)
        da = d_ref[...]
        dv1 = da * sb
        ext_d[pl.ds(0, tm), :] = dv1
        ext_d[pl.ds(tm, SC_HALO), :] = jnp.where(i < nt - 1, dn_ref[...] * zn_ref[...][:, :CH], 0.0)
        v1 = jnp.zeros((tm, CH), F32)
        dv0 = jnp.zeros((tm, CH), F32)
        for k in range(SC_K):
            shifted = ext_v[pl.ds(SC_HALO - (SC_K - 1) + k, tm), :]
            v1 = v1 + w_ref[pl.ds(k, 1), :] * shifted
            dv0 = dv0 + w_ref[pl.ds(k, 1), :] * ext_d[pl.ds(SC_K - 1 - k, tm), :]
            dw_ref[pl.ds(k, 1), :] += jnp.sum(dv1 * shifted, axis=0, keepdims=True)
        dz_ref[:, :CH] = (da * v1).astype(dz_ref.dtype)
        dz_ref[:, CH:2 * CH] = (dv0 * sx).astype(dz_ref.dtype)
        dz_ref[:, 2 * CH:] = (dv0 * sc).astype(dz_ref.dtype)

    return pl.pallas_call(
        body, name=name, grid=(nt,),
        in_specs=[pl.BlockSpec((tm, 3 * CH), lambda i: (i, 0)),
                  pl.BlockSpec((SC_HALO, 3 * CH), lambda i: (jnp.maximum(i * r - 1, 0), 0)),
                  pl.BlockSpec((SC_HALO, 3 * CH), lambda i: (jnp.minimum((i + 1) * r, nh - 1), 0)),
                  pl.BlockSpec((tm, CH), lambda i: (i, 0)),
                  pl.BlockSpec((SC_HALO, CH), lambda i: (jnp.minimum((i + 1) * r, nh - 1), 0)),
                  pl.BlockSpec((SC_HALO, CH), lambda i: (0, 0))],
        out_specs=[pl.BlockSpec((tm, 3 * CH), lambda i: (i, 0)), pl.BlockSpec((SC_HALO, CH), lambda i: (0, 0))],
        out_shape=[jax.ShapeDtypeStruct((T, 3 * CH), _MXU), jax.ShapeDtypeStruct((SC_HALO, CH), F32)],
        scratch_shapes=[pltpu.VMEM((tm + SC_HALO, CH), F32), pltpu.VMEM((tm + SC_HALO, CH), F32)],
        compiler_params=_cp("arbitrary"))(zs, zs, zs, dact, dact, w)


SWA_TQ = 512


def _t5_bucket_np(dist):
    max_exact = N_BUCKETS // 2
    d = np.maximum(dist, 1).astype(np.float32)
    large = max_exact + (np.log(d / np.float32(max_exact)) / np.float32(math.log(MAX_DISTANCE / max_exact))
                         * np.float32(N_BUCKETS - max_exact)).astype(np.int32)
    large = np.minimum(large, N_BUCKETS - 1)
    return np.where(dist < max_exact, dist, large).astype(np.int32)


def _swa_bucket_matrix(tq):
    dist = WINDOW + np.arange(tq)[:, None] - np.arange(tq + WINDOW)[None, :]
    ok = (dist >= 0) & (dist < WINDOW)
    return np.where(ok, _t5_bucket_np(np.maximum(dist, 0)), -1).astype(np.int32)


def _kv_expand_matrix():
    e = np.zeros((2 * HEAD, 4 * HEAD), np.float32)
    for h in range(4):
        for d in range(HEAD):
            e[(h // 2) * HEAD + d, h * HEAD + d] = 1.0
    return e


def _swa_bias(rel_bias, bucket, name):
    tq, tk = bucket.shape

    def body(rb_ref, bk_ref, o_ref):
        h = pl.program_id(0)
        bk = bk_ref[...]
        acc = jnp.full((tq, tk), NEG_INF, F32)
        for b in range(N_BUCKETS):
            acc = jnp.where(bk == b, rb_ref[b, h], acc)
        o_ref[0] = acc

    return pl.pallas_call(
        body, name=name, grid=(4,),
        in_specs=[pl.BlockSpec(memory_space=pltpu.SMEM), pl.BlockSpec((tq, tk), lambda h: (0, 0))],
        out_specs=pl.BlockSpec((1, tq, tk), lambda h: (h, 0, 0)),
        out_shape=jax.ShapeDtypeStruct((4, tq, tk), F32), compiler_params=_cp("parallel"))(rel_bias, bucket)


def _swa_probs(qh, kx, bm, first_col, sk):
    s = _dot(qh, kx, _NT) * (HEAD ** -0.5)
    col = lax.broadcasted_iota(jnp.int32, s.shape, 1)
    valid = (bm > 0.5 * NEG_INF) & (col >= first_col)
    s = jnp.where(valid, s + bm, NEG_INF)
    m = jnp.maximum(jnp.max(s, axis=-1, keepdims=True), sk)
    p = jnp.exp(s - m)
    den = jnp.sum(p, axis=-1, keepdims=True) + jnp.exp(sk - m)
    return p / den, m, den


def _swa_fwd(zw, gq, gk, sink, bias, expand, name):
    T = zw.shape[0]
    tq = bias.shape[1]
    r = tq // WINDOW

    def body(z_ref, zh_ref, gq_ref, gk_ref, sink_ref, b_ref, e_ref, o_ref, kext, vext):
        i = pl.program_id(0)
        cur = z_ref[...]
        qn, _ = _head_rms(cur[:, :4 * HEAD], gq_ref[...], 4)
        kc, _ = _head_rms(cur[:, 4 * HEAD:6 * HEAD], gk_ref[...], 2)
        hal = zh_ref[...]
        kp, _ = _head_rms(hal[:, :2 * HEAD], gk_ref[...], 2)
        kext[pl.ds(0, WINDOW), :] = kp
        kext[pl.ds(WINDOW, tq), :] = kc
        vext[pl.ds(0, WINDOW), :] = hal[:, 2 * HEAD:]
        vext[pl.ds(WINDOW, tq), :] = cur[:, 6 * HEAD:]
        kx = _dot(kext[...], e_ref[...]).astype(_MXU)
        vx = _dot(vext[...], e_ref[...]).astype(_MXU)
        first_col = jnp.where(i > 0, 0, WINDOW)
        out = jnp.zeros((tq, 4 * HEAD), F32)
        for h in range(4):
            mk = _lane_mask(4 * HEAD, h)
            qh = jnp.where(mk, qn, 0.0)
            pn, _, _ = _swa_probs(qh, kx, b_ref[h], first_col, sink_ref[0, h])
            out = jnp.where(mk, _dot(pn, vx), out)
        o_ref[...] = out

    return pl.pallas_call(
        body, name=name, grid=(T // tq,),
        in_specs=[pl.BlockSpec((tq, 8 * HEAD), lambda i: (i, 0)),
                  pl.BlockSpec((WINDOW, 4 * HEAD), lambda i: (jnp.maximum(i * r - 1, 0), 1)),
                  pl.BlockSpec((1, 4 * HEAD), lambda i: (0, 0)), pl.BlockSpec((1, 2 * HEAD), lambda i: (0, 0)),
                  pl.BlockSpec(memory_space=pltpu.SMEM),
                  pl.BlockSpec(bias.shape, lambda i: (0, 0, 0)),
                  pl.BlockSpec(expand.shape, lambda i: (0, 0))],
        out_specs=pl.BlockSpec((tq, 4 * HEAD), lambda i: (i, 0)),
        out_shape=jax.ShapeDtypeStruct((T, 4 * HEAD), F32),
        scratch_shapes=[pltpu.VMEM((tq + WINDOW, 2 * HEAD), F32), pltpu.VMEM((tq + WINDOW, 2 * HEAD), F32)],
        compiler_params=_cp("parallel"))(zw, zw, gq, gk, sink, bias, expand)


def _swa_bwd(zw, dact, gq, gk, sink, bias, bucket, expand, name):
    T = zw.shape[0]
    tq = bias.shape[1]
    tk = tq + WINDOW
    r = tq // WINDOW
    nt = T // tq
    nb = T // WINDOW
    scale = HEAD ** -0.5

    def body(z_ref, zh_ref, zn_ref, d_ref, dn_ref, gq_ref, gk_ref, sink_ref, b_ref, bk_ref, e_ref,
             dz_ref, dgq_ref, dgk_ref, dsk_ref, drb_ref, kext, vext, dk_s, dv_s, db_s):
        i = pl.program_id(0)

        @pl.when(i == 0)
        def _():
            dgq_ref[...] = jnp.zeros_like(dgq_ref)
            dgk_ref[...] = jnp.zeros_like(dgk_ref)
            dsk_ref[...] = jnp.zeros_like(dsk_ref)
            drb_ref[...] = jnp.zeros_like(drb_ref)
            db_s[...] = jnp.zeros_like(db_s)

        lane = lax.broadcasted_iota(jnp.int32, (1, LANE), 1)
        cur = z_ref[...]
        q_raw, k_raw = cur[:, :4 * HEAD], cur[:, 4 * HEAD:6 * HEAD]
        qn, q_r = _head_rms(q_raw, gq_ref[...], 4)
        kc, k_r = _head_rms(k_raw, gk_ref[...], 2)
        hal = zh_ref[...]
        kp, _ = _head_rms(hal[:, :2 * HEAD], gk_ref[...], 2)
        kext[pl.ds(0, WINDOW), :] = kp
        kext[pl.ds(WINDOW, tq), :] = kc
        vext[pl.ds(0, WINDOW), :] = hal[:, 2 * HEAD:]
        vext[pl.ds(WINDOW, tq), :] = cur[:, 6 * HEAD:]
        ev = e_ref[...]
        kx = _dot(kext[...], ev).astype(_MXU)
        vx = _dot(vext[...], ev).astype(_MXU)
        first_col = jnp.where(i > 0, 0, WINDOW)
        do = d_ref[...]
        dq = jnp.zeros((tq, 4 * HEAD), F32)
        dkx = jnp.zeros((tk, 4 * HEAD), F32)
        dvx = jnp.zeros((tk, 4 * HEAD), F32)
        dsk = jnp.zeros((1, LANE), F32)
        for h in range(4):
            mk = _lane_mask(4 * HEAD, h)
            qh = jnp.where(mk, qn, 0.0).astype(_MXU)
            sk = sink_ref[0, h]
            pn, m, den = _swa_probs(qh, kx, b_ref[h], first_col, sk)
            doh = jnp.where(mk, do, 0.0).astype(_MXU)
            dpn = _dot(doh, vx, _NT)
            delta = jnp.sum(pn * dpn, axis=-1, keepdims=True)
            ds = pn * (dpn - delta)
            psink = jnp.exp(sk - m) / den
            dsk = dsk + jnp.where(lane == h, jnp.sum(-psink * delta, axis=0, keepdims=True), 0.0)
            db_s[h] += ds
            dss = (ds * scale).astype(_MXU)
            dq = dq + jnp.where(mk, _dot(dss, kx), 0.0)
            dkx = dkx + _dot(dss, qh, _TN)
            dvx = dvx + _dot(pn, doh, _TN)
        dsk_ref[...] += dsk
        dk_ext = _exact_dot(dkx, ev, _NT, "a")
        dv_ext = _exact_dot(dvx, ev, _NT, "a")
        dk_s[...] = dk_ext[WINDOW:, :]
        dv_s[...] = dv_ext[WINDOW:, :]

        @pl.when(i < nt - 1)
        def _():
            nxt = zn_ref[...]
            q2, _ = _head_rms(nxt[:, :4 * HEAD], gq_ref[...], 4)
            k2n, _ = _head_rms(nxt[:, 4 * HEAD:6 * HEAD], gk_ref[...], 2)
            k2 = jnp.concatenate([kc[tq - WINDOW:, :], k2n], axis=0)
            v2 = jnp.concatenate([cur[tq - WINDOW:, 6 * HEAD:], nxt[:, 6 * HEAD:]], axis=0)
            k2x = _dot(k2, ev).astype(_MXU)
            v2x = _dot(v2, ev).astype(_MXU)
            do2 = dn_ref[...]
            dk2x = jnp.zeros((2 * WINDOW, 4 * HEAD), F32)
            dv2x = jnp.zeros((2 * WINDOW, 4 * HEAD), F32)
            for h in range(4):
                mk = _lane_mask(4 * HEAD, h)
                qh = jnp.where(mk, q2, 0.0).astype(_MXU)
                pn, _, _ = _swa_probs(qh, k2x, b_ref[h][:WINDOW, :2 * WINDOW], 0, sink_ref[0, h])
                doh = jnp.where(mk, do2, 0.0).astype(_MXU)
                dpn = _dot(doh, v2x, _NT)
                ds = pn * (dpn - jnp.sum(pn * dpn, axis=-1, keepdims=True))
                dk2x = dk2x + _dot((ds * scale).astype(_MXU), qh, _TN)
                dv2x = dv2x + _dot(pn, doh, _TN)
            dk_s[pl.ds(tq - WINDOW, WINDOW), :] += _exact_dot(dk2x, ev, _NT, "a")[:WINDOW, :]
            dv_s[pl.ds(tq - WINDOW, WINDOW), :] += _exact_dot(dv2x, ev, _NT, "a")[:WINDOW, :]

        dq_raw, dgq = _head_rms_bwd(dq, q_raw, q_r, gq_ref[...], 4)
        dk_raw, dgk = _head_rms_bwd(dk_s[...], k_raw, k_r, gk_ref[...], 2)
        dgq_ref[...] += dgq
        dgk_ref[...] += dgk
        dz_ref[:, :4 * HEAD] = dq_raw.astype(dz_ref.dtype)
        dz_ref[:, 4 * HEAD:6 * HEAD] = dk_raw.astype(dz_ref.dtype)
        dz_ref[:, 6 * HEAD:] = dv_s[...].astype(dz_ref.dtype)

        @pl.when(i == nt - 1)
        def _():
            bk = bk_ref[...]
            for b in range(N_BUCKETS):
                rowv = jnp.zeros((1, LANE), F32)
                for h in range(4):
                    s1 = jnp.sum(jnp.where(bk == b, db_s[h], 0.0), axis=0, keepdims=True)
                    rowv = jnp.where(lane == h, jnp.sum(s1, axis=1, keepdims=True), rowv)
                drb_ref[pl.ds(b, 1), :] = rowv

    const2 = lambda i: (0, 0)
    return pl.pallas_call(
        body, name=name, grid=(nt,),
        in_specs=[pl.BlockSpec((tq, 8 * HEAD), lambda i: (i, 0)),
                  pl.BlockSpec((WINDOW, 4 * HEAD), lambda i: (jnp.maximum(i * r - 1, 0), 1)),
                  pl.BlockSpec((WINDOW, 8 * HEAD), lambda i: (jnp.minimum((i + 1) * r, nb - 1), 0)),
                  pl.BlockSpec((tq, 4 * HEAD), lambda i: (i, 0)),
                  pl.BlockSpec((WINDOW, 4 * HEAD), lambda i: (jnp.minimum((i + 1) * r, nb - 1), 0)),
                  pl.BlockSpec((1, 4 * HEAD), const2), pl.BlockSpec((1, 2 * HEAD), const2),
                  pl.BlockSpec(memory_space=pltpu.SMEM),
                  pl.BlockSpec(bias.shape, lambda i: (0, 0, 0)),
                  pl.BlockSpec(bucket.shape, const2), pl.BlockSpec(expand.shape, const2)],
        out_specs=[pl.BlockSpec((tq, 8 * HEAD), lambda i: (i, 0)),
                   pl.BlockSpec((1, 4 * HEAD), const2), pl.BlockSpec((1, 2 * HEAD), const2),
                   pl.BlockSpec((1, LANE), const2), pl.BlockSpec((N_BUCKETS, LANE), const2)],
        out_shape=[jax.ShapeDtypeStruct((T, 8 * HEAD), _MXU), jax.ShapeDtypeStruct((1, 4 * HEAD), F32),
                   jax.ShapeDtypeStruct((1, 2 * HEAD), F32), jax.ShapeDtypeStruct((1, LANE), F32),
                   jax.ShapeDtypeStruct((N_BUCKETS, LANE), F32)],
        scratch_shapes=[pltpu.VMEM((tk, 2 * HEAD), F32), pltpu.VMEM((tk, 2 * HEAD), F32),
                        pltpu.VMEM((tq, 2 * HEAD), F32), pltpu.VMEM((tq, 2 * HEAD), F32),
                        pltpu.VMEM((4, tq, tk), F32)],
        compiler_params=_cp("arbitrary"))(zw, zw, zw, dact, dact, gq, gk, sink, bias, bucket, expand)


FOX_B = 512
FOX_TM = 256


def _tri(n, lower):
    m = np.tril(np.ones((n, n), np.float32)) if lower else np.triu(np.ones((n, n), np.float32))
    return m


def _log_sigmoid(x):
    return jnp.minimum(x, 0.0) - jnp.log1p(jnp.exp(-jnp.abs(x)))


def _fox_prep(zf, gq, gk, bf, name):
    T = zf.shape[0]
    tm = _tile(T, FOX_TM)
    lower = jnp.asarray(_tri(tm, True), _MXU)

    def body(z_ref, gq_ref, gk_ref, bf_ref, l_ref, q_ref, k_ref, v_ref, f_ref, ft_ref, carry):
        @pl.when(pl.program_id(0) == 0)
        def _():
            carry[...] = jnp.zeros_like(carry)

        z = z_ref[...]
        q, _ = _head_rms(z[:, :CH], gq_ref[...], 4)
        k, _ = _head_rms(z[:, CH:2 * CH], gk_ref[...], 4)
        q_ref[...] = q.astype(q_ref.dtype)
        k_ref[...] = k.astype(k_ref.dtype)
        v_ref[...] = z[:, 2 * CH:3 * CH].astype(v_ref.dtype)
        lane = lax.broadcasted_iota(jnp.int32, (1, LANE), 1)
        lf = jnp.where(lane < 4, _log_sigmoid(z[:, 3 * CH:] + bf_ref[...]), 0.0)
        fv = _exact_dot(l_ref[...], lf, _NN, "b") + carry[pl.ds(0, 1), :]
        f_ref[...] = fv
        ft_ref[...] = fv.T
        carry[pl.ds(0, 1), :] = f_ref[pl.ds(tm - 1, 1), :]

    row = pl.BlockSpec((tm, CH), lambda i: (i, 0))
    vec = pl.BlockSpec((1, CH), lambda i: (0, 0))
    qsh = jax.ShapeDtypeStruct((T, CH), _MXU)
    return pl.pallas_call(
        body, name=name, grid=(T // tm,),
        in_specs=[pl.BlockSpec((tm, 3 * CH + LANE), lambda i: (i, 0)), vec, vec,
                  pl.BlockSpec((1, LANE), lambda i: (0, 0)), pl.BlockSpec((tm, tm), lambda i: (0, 0))],
        out_specs=[row, row, row, pl.BlockSpec((tm, LANE), lambda i: (i, 0)), pl.BlockSpec((LANE, tm), lambda i: (0, i))],
        out_shape=[qsh, qsh, qsh, jax.ShapeDtypeStruct((T, LANE), F32), jax.ShapeDtypeStruct((LANE, T), F32)],
        scratch_shapes=[pltpu.VMEM((8, LANE), F32)],
        compiler_params=_cp("arbitrary"))(zf, gq, gk, bf, lower)


def _lane_col(x, h):
    lane = lax.broadcasted_iota(jnp.int32, (1, x.shape[-1]), 1)
    return jnp.sum(jnp.where(lane == h, x, 0.0), axis=-1, keepdims=True)


def _fox_scores(qh, k, fq, ft_ref, h, qi, ki, B):
    s = _dot(qh, k, _NT) * (HEAD ** -0.5)
    s = s + (fq - ft_ref[pl.ds(h, 1), :])
    row = qi * B + lax.broadcasted_iota(jnp.int32, s.shape, 0)
    col = ki * B + lax.broadcasted_iota(jnp.int32, s.shape, 1)
    return jnp.where(col <= row, s, NEG_INF)


def _fox_fwd(q, k, v, f, ft, name):
    T = q.shape[0]
    B = _tile(T, FOX_B)
    n = T // B

    def body(q_ref, k_ref, v_ref, f_ref, ft_ref, o_ref, lse_ref, m_s, l_s, acc):
        qi, ki = pl.program_id(0), pl.program_id(1)

        @pl.when(ki == 0)
        def _():
            m_s[...] = jnp.full_like(m_s, NEG_INF)
            l_s[...] = jnp.zeros_like(l_s)
            acc[...] = jnp.zeros_like(acc)

        @pl.when(ki <= qi)
        def _():
            qv, kv, vv, fv = q_ref[...], k_ref[...], v_ref[...], f_ref[...]
            for h in range(4):
                mk = _lane_mask(CH, h)
                qh = jnp.where(mk, qv, jnp.zeros_like(qv))
                s = _fox_scores(qh, kv, _lane_col(fv, h), ft_ref, h, qi, ki, B)
                m_old = m_s[h]
                m_new = jnp.maximum(m_old, jnp.max(s, axis=-1, keepdims=True))
                alpha = jnp.exp(m_old - m_new)
                p = jnp.exp(s - m_new)
                l_s[h] = alpha * l_s[h] + jnp.sum(p, axis=-1, keepdims=True)
                m_s[h] = m_new
                acc[...] = jnp.where(mk, acc[...] * alpha + _dot(p, vv), acc[...])

        @pl.when(ki == qi)
        def _():
            lane = lax.broadcasted_iota(jnp.int32, (1, LANE), 1)
            out = acc[...]
            lse = jnp.zeros((B, LANE), F32)
            for h in range(4):
                out = jnp.where(_lane_mask(CH, h), out / l_s[h], out)
                lse = jnp.where(lane == h, m_s[h] + jnp.log(l_s[h]), lse)
            o_ref[...] = out
            lse_ref[...] = lse

    qspec = pl.BlockSpec((B, CH), lambda qi, ki: (qi, 0))
    kspec = pl.BlockSpec((B, CH), lambda qi, ki: (jnp.minimum(ki, qi), 0))
    return pl.pallas_call(
        body, name=name, grid=(n, n),
        in_specs=[qspec, kspec, kspec, pl.BlockSpec((B, LANE), lambda qi, ki: (qi, 0)),
                  pl.BlockSpec((8, B), lambda qi, ki: (0, jnp.minimum(ki, qi)))],
        out_specs=[qspec, pl.BlockSpec((B, LANE), lambda qi, ki: (qi, 0))],
        out_shape=[jax.ShapeDtypeStruct((T, CH), F32), jax.ShapeDtypeStruct((T, LANE), F32)],
        scratch_shapes=[pltpu.VMEM((4, B, 1), F32), pltpu.VMEM((4, B, 1), F32), pltpu.VMEM((B, CH), F32)],
        compiler_params=_cp("parallel", "arbitrary"))(q, k, v, f, ft)


def _fox_delta(o, do, name):
    T = o.shape[0]
    tm = _tile(T, ROW_TILE)

    def body(o_ref, d_ref, out_ref):
        prod = o_ref[...] * d_ref[...]
        lane = lax.broadcasted_iota(jnp.int32, (1, LANE), 1)
        out = jnp.zeros((tm, LANE), F32)
        for h in range(4):
            s = jnp.sum(jnp.where(_lane_mask(CH, h), prod, 0.0), axis=-1, keepdims=True)
            out = jnp.where(lane == h, s, out)
        out_ref[...] = out

    row = pl.BlockSpec((tm, CH), lambda i: (i, 0))
    return pl.pallas_call(
        body, name=name, grid=(T // tm,), in_specs=[row, row],
        out_specs=pl.BlockSpec((tm, LANE), lambda i: (i, 0)),
        out_shape=jax.ShapeDtypeStruct((T, LANE), F32), compiler_params=_cp("parallel"))(o, do)


def _fox_bwd_dq(q, k, v, f, ft, lse, delta, do, name):
    T = q.shape[0]
    B = _tile(T, FOX_B)
    n = T // B

    def body(q_ref, k_ref, v_ref, f_ref, ft_ref, lse_ref, dl_ref, do_ref, dq_ref, dfq_ref, dq_s, df_s):
        qi, ki = pl.program_id(0), pl.program_id(1)

        @pl.when(ki == 0)
        def _():
            dq_s[...] = jnp.zeros_like(dq_s)
            df_s[...] = jnp.zeros_like(df_s)

        @pl.when(ki <= qi)
        def _():
            qv, kv, vv, fv = q_ref[...], k_ref[...], v_ref[...], f_ref[...]
            lsev, dlv, dov = lse_ref[...], dl_ref[...], do_ref[...]
            lane = lax.broadcasted_iota(jnp.int32, (1, LANE), 1)
            for h in range(4):
                mk = _lane_mask(CH, h)
                qh = jnp.where(mk, qv, jnp.zeros_like(qv))
                s = _fox_scores(qh, kv, _lane_col(fv, h), ft_ref, h, qi, ki, B)
                p = jnp.exp(s - _lane_col(lsev, h))
                doh = jnp.where(mk, dov, 0.0)
                ds = p * (_dot(doh, vv, _NT) - _lane_col(dlv, h))
                dq_s[...] += jnp.where(mk, _dot(ds * (HEAD ** -0.5), kv), 0.0)
                df_s[...] += jnp.where(lane == h, jnp.sum(ds, axis=-1, keepdims=True), 0.0)

        @pl.when(ki == qi)
        def _():
            dq_ref[...] = dq_s[...]
            dfq_ref[...] = df_s[...]

    qspec = pl.BlockSpec((B, CH), lambda qi, ki: (qi, 0))
    kspec = pl.BlockSpec((B, CH), lambda qi, ki: (jnp.minimum(ki, qi), 0))
    lspec = pl.BlockSpec((B, LANE), lambda qi, ki: (qi, 0))
    return pl.pallas_call(
        body, name=name, grid=(n, n),
        in_specs=[qspec, kspec, kspec, lspec, pl.BlockSpec((8, B), lambda qi, ki: (0, jnp.minimum(ki, qi))),
                  lspec, lspec, qspec],
        out_specs=[qspec, lspec],
        out_shape=[jax.ShapeDtypeStruct((T, CH), F32), jax.ShapeDtypeStruct((T, LANE), F32)],
        scratch_shapes=[pltpu.VMEM((B, CH), F32), pltpu.VMEM((B, LANE), F32)],
        compiler_params=_cp("parallel", "arbitrary"))(q, k, v, f, ft, lse, delta, do)


def _fox_bwd_dkv(q, k, v, f, ft, lse, delta, do, name):
    T = q.shape[0]
    B = _tile(T, FOX_B)
    n = T // B

    def body(q_ref, k_ref, v_ref, f_ref, ft_ref, lse_ref, dl_ref, do_ref, dk_ref, dv_ref, dft_ref, dk_s, dv_s, df_s):
        ki, qi = pl.program_id(0), pl.program_id(1)

        @pl.when(qi == 0)
        def _():
            dk_s[...] = jnp.zeros_like(dk_s)
            dv_s[...] = jnp.zeros_like(dv_s)
            df_s[...] = jnp.zeros_like(df_s)

        @pl.when(qi >= ki)
        def _():
            qv, kv, vv, fv = q_ref[...], k_ref[...], v_ref[...], f_ref[...]
            lsev, dlv, dov = lse_ref[...], dl_ref[...], do_ref[...]
            for h in range(4):
                mk = _lane_mask(CH, h)
                qh = jnp.where(mk, qv, jnp.zeros_like(qv))
                s = _fox_scores(qh, kv, _lane_col(fv, h), ft_ref, h, qi, ki, B)
                p = jnp.exp(s - _lane_col(lsev, h))
                doh = jnp.where(mk, dov, 0.0)
                ds = p * (_dot(doh, vv, _NT) - _lane_col(dlv, h))
                dv_s[...] += _dot(p, doh, _TN)
                dk_s[...] += _dot(ds * (HEAD ** -0.5), qh, _TN)
                df_s[pl.ds(h, 1), :] -= jnp.sum(ds, axis=0, keepdims=True)

        @pl.when(qi == n - 1)
        def _():
            dk_ref[...] = dk_s[...]
            dv_ref[...] = dv_s[...]
            dft_ref[...] = jnp.zeros_like(dft_ref)
            dft_ref[pl.ds(0, 8), :] = df_s[...]

    qspec = pl.BlockSpec((B, CH), lambda ki, qi: (jnp.maximum(qi, ki), 0))
    kspec = pl.BlockSpec((B, CH), lambda ki, qi: (ki, 0))
    lspec = pl.BlockSpec((B, LANE), lambda ki, qi: (jnp.maximum(qi, ki), 0))
    return pl.pallas_call(
        body, name=name, grid=(n, n),
        in_specs=[qspec, kspec, kspec, lspec, pl.BlockSpec((8, B), lambda ki, qi: (0, ki)), lspec, lspec, qspec],
        out_specs=[kspec, kspec, pl.BlockSpec((LANE, B), lambda ki, qi: (0, ki))],
        out_shape=[jax.ShapeDtypeStruct((T, CH), F32), jax.ShapeDtypeStruct((T, CH), F32),
                   jax.ShapeDtypeStruct((LANE, T), F32)],
        scratch_shapes=[pltpu.VMEM((B, CH), F32), pltpu.VMEM((B, CH), F32), pltpu.VMEM((8, B), F32)],
        compiler_params=_cp("parallel", "arbitrary"))(q, k, v, f, ft, lse, delta, do)


def _fox_post(zf, dqn, dkn, dv, dfq, dft, gq, gk, bf, name):
    T = zf.shape[0]
    tm = _tile(T, FOX_TM)
    nt = T // tm
    upper = jnp.asarray(_tri(tm, False), _MXU)

    def body(z_ref, dq_ref, dk_ref, dv_ref, dfq_ref, dft_ref, gq_ref, gk_ref, bf_ref, u_ref, dz_ref, sm_ref, carry, rc_s):
        @pl.when(pl.program_id(0) == 0)
        def _():
            carry[...] = jnp.zeros_like(carry)
            sm_ref[...] = jnp.zeros_like(sm_ref)

        z = z_ref[...]
        q_raw, k_raw = z[:, :CH], z[:, CH:2 * CH]
        _, q_r = _head_rms(q_raw, gq_ref[...], 4)
        _, k_r = _head_rms(k_raw, gk_ref[...], 4)
        dq, dgq = _head_rms_bwd(dq_ref[...], q_raw, q_r, gq_ref[...], 4)
        dk, dgk = _head_rms_bwd(dk_ref[...], k_raw, k_r, gk_ref[...], 4)
        df = dfq_ref[...] + dft_ref[...].T
        rc_s[...] = _exact_dot(u_ref[...], df, _NN, "b") + carry[pl.ds(0, 1), :]
        carry[pl.ds(0, 1), :] = rc_s[pl.ds(0, 1), :]
        lane = lax.broadcasted_iota(jnp.int32, (1, LANE), 1)
        x = z[:, 3 * CH:] + bf_ref[...]
        dff = jnp.where(lane < 4, rc_s[...] * _sigmoid(-x), 0.0)
        dz_ref[:, :CH] = dq.astype(dz_ref.dtype)
        dz_ref[:, CH:2 * CH] = dk.astype(dz_ref.dtype)
        dz_ref[:, 2 * CH:3 * CH] = dv_ref[...].astype(dz_ref.dtype)
        dz_ref[:, 3 * CH:] = dff.astype(dz_ref.dtype)
        sm_ref[pl.ds(0, 1), :] += dgq
        sm_ref[pl.ds(1, 1), :] += dgk
        sm_ref[pl.ds(2, 1), :LANE] += jnp.sum(dff, axis=0, keepdims=True)

    rev = lambda i: (nt - 1 - i, 0)
    row = pl.BlockSpec((tm, CH), rev)
    lrow = pl.BlockSpec((tm, LANE), rev)
    vec = pl.BlockSpec((1, CH), lambda i: (0, 0))
    return pl.pallas_call(
        body, name=name, grid=(nt,),
        in_specs=[pl.BlockSpec((tm, 3 * CH + LANE), rev), row, row, row, lrow,
                  pl.BlockSpec((LANE, tm), lambda i: (0, nt - 1 - i)), vec, vec,
                  pl.BlockSpec((1, LANE), lambda i: (0, 0)), pl.BlockSpec((tm, tm), lambda i: (0, 0))],
        out_specs=[pl.BlockSpec((tm, 3 * CH + LANE), rev), pl.BlockSpec((8, CH), lambda i: (0, 0))],
        out_shape=[jax.ShapeDtypeStruct((T, 3 * CH + LANE), _MXU), jax.ShapeDtypeStruct((8, CH), F32)],
        scratch_shapes=[pltpu.VMEM((8, LANE), F32), pltpu.VMEM((tm, LANE), F32)],
        compiler_params=_cp("arbitrary"))(zf, dqn, dkn, dv, dfq, dft, gq, gk, bf, upper)


AUG_F, AUG_ONE, AUG_LSE = HEAD, HEAD + 3, HEAD + 6


def _pieces(x):
    hi = x.astype(_MXU).astype(F32)
    r1 = x - hi
    mid = r1.astype(_MXU).astype(F32)
    lo = (r1 - mid).astype(_MXU).astype(F32)
    return hi, mid, lo


def _put_pieces(base, first_lane, x, sign):
    lane = lax.broadcasted_iota(jnp.int32, (1, LANE), 1)
    for j, piece in enumerate(_pieces(x)):
        base = jnp.where(lane == first_lane + j, sign * piece, base)
    return base


def _head_select_matrix():
    p = np.zeros((4, 4 * HEAD, LANE), np.float32)
    for h in range(4):
        for d in range(HEAD):
            p[h, h * HEAD + d, d] = 1.0
    return p


def _tri_steps(n, by_key):
    if by_key:
        pairs = [(q, k) for k in range(n) for q in range(k, n)]
    else:
        pairs = [(q, k) for q in range(n) for k in range(q + 1)]
    return (jnp.asarray([p[0] for p in pairs], jnp.int32), jnp.asarray([p[1] for p in pairs], jnp.int32))


def _fox2_prep(zf, gq, gk, bf, sel, name):
    T = zf.shape[0]
    tm = _tile(T, FOX_TM)
    lower = jnp.asarray(_tri(tm, True), _MXU)

    def body(z_ref, gq_ref, gk_ref, bf_ref, l_ref, p_ref, qa_ref, ka_ref, va_ref, carry, f_s):
        @pl.when(pl.program_id(0) == 0)
        def _():
            carry[...] = jnp.zeros_like(carry)

        z = z_ref[...]
        q, _ = _head_rms(z[:, :CH], gq_ref[...], 4)
        k, _ = _head_rms(z[:, CH:2 * CH], gk_ref[...], 4)
        q = (q * (HEAD ** -0.5)).astype(_MXU)
        k = k.astype(_MXU)
        v = z[:, 2 * CH:3 * CH].astype(_MXU)
        lane = lax.broadcasted_iota(jnp.int32, (1, LANE), 1)
        lf = jnp.where(lane < 4, _log_sigmoid(z[:, 3 * CH:] + bf_ref[...]), 0.0)
        f_s[...] = _exact_dot(l_ref[...], lf, _NN, "b") + carry[pl.ds(0, 1), :]
        carry[pl.ds(0, 1), :] = f_s[pl.ds(tm - 1, 1), :]
        fv = f_s[...]
        q_ones = (lane >= AUG_ONE) & (lane < AUG_ONE + 3)
        k_ones = ((lane >= AUG_F) & (lane < AUG_F + 3)) | ((lane >= AUG_LSE) & (lane < AUG_LSE + 3))
        v_ones = (lane >= AUG_F) & (lane < AUG_F + 3)
        for h in range(4):
            fh = _lane_col(fv, h)
            qa = jnp.where(q_ones, 1.0, _dot(q, p_ref[h]))
            qa_ref[h] = _put_pieces(qa, AUG_F, fh, 1.0).astype(qa_ref.dtype)
            ka = jnp.where(k_ones, 1.0, _dot(k, p_ref[h]))
            ka_ref[h] = _put_pieces(ka, AUG_ONE, fh, -1.0).astype(ka_ref.dtype)
            va_ref[h] = jnp.where(v_ones, 1.0, _dot(v, p_ref[h])).astype(va_ref.dtype)

    vec = pl.BlockSpec((1, CH), lambda i: (0, 0))
    hspec = pl.BlockSpec((4, tm, LANE), lambda i: (0, i, 0))
    hsh = jax.ShapeDtypeStruct((4, T, LANE), _MXU)
    return pl.pallas_call(
        body, name=name, grid=(T // tm,),
        in_specs=[pl.BlockSpec((tm, 3 * CH + LANE), lambda i: (i, 0)), vec, vec,
                  pl.BlockSpec((1, LANE), lambda i: (0, 0)), pl.BlockSpec((tm, tm), lambda i: (0, 0)),
                  pl.BlockSpec(sel.shape, lambda i: (0, 0, 0))],
        out_specs=[hspec, hspec, hspec], out_shape=[hsh, hsh, hsh],
        scratch_shapes=[pltpu.VMEM((8, LANE), F32), pltpu.VMEM((tm, LANE), F32)],
        compiler_params=_cp("arbitrary"))(zf, gq, gk, bf, lower, sel)


def _causal(s, transposed):
    row = lax.broadcasted_iota(jnp.int32, s.shape, 0)
    col = lax.broadcasted_iota(jnp.int32, s.shape, 1)
    return jnp.where((row <= col) if transposed else (col <= row), s, NEG_INF)


def _mxu_dot(a, b, dims):
    return lax.dot_general(a, b, dims, preferred_element_type=F32)


def _fox2_fwd(qa, ka, va, sel, name):
    T = qa.shape[1]
    B = _tile(T, FOX_B)
    n = T // B
    qt, kt = _tri_steps(n, False)

    def body(qt_ref, kt_ref, qa_ref, ka_ref, va_ref, p_ref, o_ref, qb_ref, m_s, acc):
        step = pl.program_id(0)
        qi, ki = qt_ref[step], kt_ref[step]

        @pl.when(ki == 0)
        def _():
            m_s[...] = jnp.full_like(m_s, NEG_INF)
            acc[...] = jnp.zeros_like(acc)

        def update(diag):
            for h in range(4):
                s = _mxu_dot(qa_ref[h], ka_ref[h], _NT)
                if diag:
                    s = _causal(s, False)
                m_old = m_s[h]
                m_new = jnp.maximum(m_old, jnp.max(s, axis=-1, keepdims=True))
                p = jnp.exp(s - m_new)
                acc[h] = acc[h] * jnp.exp(m_old - m_new) + _dot(p, va_ref[h])
                m_s[h] = m_new

        @pl.when(ki < qi)
        def _():
            update(False)

        @pl.when(ki == qi)
        def _():
            update(True)
            out = jnp.zeros((B, CH), F32)
            for h in range(4):
                a = acc[h]
                l = _lane_col(a, AUG_F)
                out = out + _exact_dot(a / l, p_ref[h], _NT, "a")
                lse = m_s[h] + jnp.log(l)
                qb_ref[h] = _put_pieces(qa_ref[h].astype(F32), AUG_LSE, lse, -1.0).astype(qb_ref.dtype)
            o_ref[...] = out

    qspec = pl.BlockSpec((4, B, LANE), lambda s, qt, kt: (0, qt[s], 0))
    kspec = pl.BlockSpec((4, B, LANE), lambda s, qt, kt: (0, kt[s], 0))
    grid_spec = pltpu.PrefetchScalarGridSpec(
        num_scalar_prefetch=2, grid=(qt.shape[0],),
        in_specs=[qspec, kspec, kspec, pl.BlockSpec(sel.shape, lambda s, qt, kt: (0, 0, 0))],
        out_specs=[pl.BlockSpec((B, CH), lambda s, qt, kt: (qt[s], 0)), qspec],
        scratch_shapes=[pltpu.VMEM((4, B, 1), F32), pltpu.VMEM((4, B, LANE), F32)])
    return pl.pallas_call(
        body, name=name, grid_spec=grid_spec,
        out_shape=[jax.ShapeDtypeStruct((T, CH), F32), jax.ShapeDtypeStruct((4, T, LANE), _MXU)],
        compiler_params=_cp("arbitrary"))(qt, kt, qa, ka, va, sel)


def _fox2_bwd_prep(o, do, sel, name):
    T = o.shape[0]
    tm = _tile(T, ROW_TILE)

    def body(o_ref, d_ref, p_ref, out_ref):
        dov = d_ref[...]
        prod = o_ref[...] * dov
        dob = dov.astype(_MXU)
        for h in range(4):
            delta = jnp.sum(jnp.where(_lane_mask(CH, h), prod, 0.0), axis=-1, keepdims=True)
            out_ref[h] = _put_pieces(_dot(dob, p_ref[h]), AUG_F, delta, -1.0).astype(out_ref.dtype)

    row = pl.BlockSpec((tm, CH), lambda i: (i, 0))
    return pl.pallas_call(
        body, name=name, grid=(T // tm,),
        in_specs=[row, row, pl.BlockSpec(sel.shape, lambda i: (0, 0, 0))],
        out_specs=pl.BlockSpec((4, tm, LANE), lambda i: (0, i, 0)),
        out_shape=jax.ShapeDtypeStruct((4, T, LANE), _MXU), compiler_params=_cp("parallel"))(o, do, sel)


def _fox2_bwd_dq(qb, ka, va, doa, sel, name):
    T = qb.shape[1]
    B = _tile(T, FOX_B)
    n = T // B
    qt, kt = _tri_steps(n, False)

    def body(qt_ref, kt_ref, qb_ref, ka_ref, va_ref, do_ref, p_ref, dq_ref, dfq_ref, dq_s):
        step = pl.program_id(0)
        qi, ki = qt_ref[step], kt_ref[step]

        @pl.when(ki == 0)
        def _():
            dq_s[...] = jnp.zeros_like(dq_s)

        def update(diag):
            for h in range(4):
                s = _mxu_dot(qb_ref[h], ka_ref[h], _NT)
                if diag:
                    s = _causal(s, False)
                ds = jnp.exp(s) * _mxu_dot(do_ref[h], va_ref[h], _NT)
                dq_s[h] += _dot(ds, ka_ref[h])

        @pl.when(ki < qi)
        def _():
            update(False)

        @pl.when(ki == qi)
        def _():
            update(True)
            lane = lax.broadcasted_iota(jnp.int32, (1, LANE), 1)
            out = jnp.zeros((B, CH), F32)
            dfq = jnp.zeros((B, LANE), F32)
            for h in range(4):
                out = out + _exact_dot(dq_s[h] * (HEAD ** -0.5), p_ref[h], _NT, "a")
                dfq = jnp.where(lane == h, _lane_col(dq_s[h], AUG_F), dfq)
            dq_ref[...] = out
            dfq_ref[...] = dfq

    qspec = pl.BlockSpec((4, B, LANE), lambda s, qt, kt: (0, qt[s], 0))
    kspec = pl.BlockSpec((4, B, LANE), lambda s, qt, kt: (0, kt[s], 0))
    grid_spec = pltpu.PrefetchScalarGridSpec(
        num_scalar_prefetch=2, grid=(qt.shape[0],),
        in_specs=[qspec, kspec, kspec, qspec, pl.BlockSpec(sel.shape, lambda s, qt, kt: (0, 0, 0))],
        out_specs=[pl.BlockSpec((B, CH), lambda s, qt, kt: (qt[s], 0)),
                   pl.BlockSpec((B, LANE), lambda s, qt, kt: (qt[s], 0))],
        scratch_shapes=[pltpu.VMEM((4, B, LANE), F32)])
    return pl.pallas_call(
        body, name=name, grid_spec=grid_spec,
        out_shape=[jax.ShapeDtypeStruct((T, CH), F32), jax.ShapeDtypeStruct((T, LANE), F32)],
        compiler_params=_cp("arbitrary"))(qt, kt, qb, ka, va, doa, sel)


def _fox2_bwd_dkv(qb, ka, va, doa, sel, name):
    T = qb.shape[1]
    B = _tile(T, FOX_B)
    n = T // B
    qt, kt = _tri_steps(n, True)

    def body(qt_ref, kt_ref, qb_ref, ka_ref, va_ref, do_ref, p_ref, dk_ref, dv_ref, df_ref, dk_s, dv_s):
        step = pl.program_id(0)
        qi, ki = qt_ref[step], kt_ref[step]

        @pl.when(qi == ki)
        def _():
            dk_s[...] = jnp.zeros_like(dk_s)
            dv_s[...] = jnp.zeros_like(dv_s)

        def update(diag):
            for h in range(4):
                st = _mxu_dot(ka_ref[h], qb_ref[h], _NT)
                if diag:
                    st = _causal(st, True)
                pt = jnp.exp(st)
                dst = pt * _mxu_dot(va_ref[h], do_ref[h], _NT)
                dv_s[h] += _dot(pt, do_ref[h])
                dk_s[h] += _dot(dst, qb_ref[h])

        @pl.when(qi == ki)
        def _():
            update(True)

        @pl.when(qi > ki)
        def _():
            update(False)

        @pl.when(qi == n - 1)
        def _():
            lane = lax.broadcasted_iota(jnp.int32, (1, LANE), 1)
            dk = jnp.zeros((B, CH), F32)
            dv = jnp.zeros((B, CH), F32)
            dfk = jnp.zeros((B, LANE), F32)
            for h in range(4):
                dk = dk + _exact_dot(dk_s[h], p_ref[h], _NT, "a")
                dv = dv + _exact_dot(dv_s[h], p_ref[h], _NT, "a")
                dfk = jnp.where(lane == h, -_lane_col(dk_s[h], AUG_ONE), dfk)
            dk_ref[...] = dk
            dv_ref[...] = dv
            df_ref[...] = dfk

    qspec = pl.BlockSpec((4, B, LANE), lambda s, qt, kt: (0, qt[s], 0))
    kspec = pl.BlockSpec((4, B, LANE), lambda s, qt, kt: (0, kt[s], 0))
    ospec = pl.BlockSpec((B, CH), lambda s, qt, kt: (kt[s], 0))
    grid_spec = pltpu.PrefetchScalarGridSpec(
        num_scalar_prefetch=2, grid=(qt.shape[0],),
        in_specs=[qspec, kspec, kspec, qspec, pl.BlockSpec(sel.shape, lambda s, qt, kt: (0, 0, 0))],
        out_specs=[ospec, ospec, pl.BlockSpec((B, LANE), lambda s, qt, kt: (kt[s], 0))],
        scratch_shapes=[pltpu.VMEM((4, B, LANE), F32), pltpu.VMEM((4, B, LANE), F32)])
    return pl.pallas_call(
        body, name=name, grid_spec=grid_spec,
        out_shape=[jax.ShapeDtypeStruct((T, CH), F32), jax.ShapeDtypeStruct((T, CH), F32),
                   jax.ShapeDtypeStruct((T, LANE), F32)],
        compiler_params=_cp("arbitrary"))(qt, kt, qb, ka, va, doa, sel)


def _fox2_bwd(qb, ka, va, doa, sel, name):
    T = qb.shape[1]
    B = _tile(T, FOX_B)
    n = T // B
    qt, kt = _tri_steps(n, True)
    nsteps = qt.shape[0]

    def body(qt_ref, kt_ref, qb_ref, ka_ref, va_ref, do_ref, p_ref, dk_ref, dv_ref, df_ref, dq_hbm,
             dk_s, dv_s, kat_s, dq_s, sem):
        step = pl.program_id(0)
        qi, ki = qt_ref[step], kt_ref[step]

        @pl.when(step == 0)
        def _():
            dq_s[...] = jnp.zeros_like(dq_s)

        @pl.when(qi == ki)
        def _():
            dk_s[...] = jnp.zeros_like(dk_s)
            dv_s[...] = jnp.zeros_like(dv_s)
            for h in range(4):
                kat_s[h] = ka_ref[h].astype(F32).T.astype(kat_s.dtype)

        def update(diag):
            for h in range(4):
                st = _mxu_dot(ka_ref[h], qb_ref[h], _NT)
                if diag:
                    st = _causal(st, True)
                pt = jnp.exp(st)
                dst = (pt * _mxu_dot(va_ref[h], do_ref[h], _NT)).astype(_MXU)
                dv_s[h] += _dot(pt, do_ref[h])
                dk_s[h] += _mxu_dot(dst, qb_ref[h], _NN)
                dq_s[qi, h] += _mxu_dot(kat_s[h], dst, _NN)

        @pl.when(qi == ki)
        def _():
            update(True)

        @pl.when(qi > ki)
        def _():
            update(False)

        @pl.when(qi == n - 1)
        def _():
            lane = lax.broadcasted_iota(jnp.int32, (1, LANE), 1)
            dk = jnp.zeros((B, CH), F32)
            dv = jnp.zeros((B, CH), F32)
            dfk = jnp.zeros((B, LANE), F32)
            for h in range(4):
                dk = dk + _exact_dot(dk_s[h], p_ref[h], _NT, "a")
                dv = dv + _exact_dot(dv_s[h], p_ref[h], _NT, "a")
                dfk = jnp.where(lane == h, -_lane_col(dk_s[h], AUG_ONE), dfk)
            dk_ref[...] = dk
            dv_ref[...] = dv
            df_ref[...] = dfk

        @pl.when(step == nsteps - 1)
        def _():
            cp = pltpu.make_async_copy(dq_s, dq_hbm, sem)
            cp.start()
            cp.wait()

    qspec = pl.BlockSpec((4, B, LANE), lambda s, qt, kt: (0, qt[s], 0))
    kspec = pl.BlockSpec((4, B, LANE), lambda s, qt, kt: (0, kt[s], 0))
    ospec = pl.BlockSpec((B, CH), lambda s, qt, kt: (kt[s], 0))
    grid_spec = pltpu.PrefetchScalarGridSpec(
        num_scalar_prefetch=2, grid=(nsteps,),
        in_specs=[qspec, kspec, kspec, qspec, pl.BlockSpec(sel.shape, lambda s, qt, kt: (0, 0, 0))],
        out_specs=[ospec, ospec, pl.BlockSpec((B, LANE), lambda s, qt, kt: (kt[s], 0)),
                   pl.BlockSpec(memory_space=pl.ANY)],
        scratch_shapes=[pltpu.VMEM((4, B, LANE), F32), pltpu.VMEM((4, B, LANE), F32), pltpu.VMEM((4, LANE, B), _MXU),
                        pltpu.VMEM((n, 4, LANE, B), F32), pltpu.SemaphoreType.DMA])
    return pl.pallas_call(
        body, name=name, grid_spec=grid_spec,
        out_shape=[jax.ShapeDtypeStruct((T, CH), F32), jax.ShapeDtypeStruct((T, CH), F32),
                   jax.ShapeDtypeStruct((T, LANE), F32), jax.ShapeDtypeStruct((n, 4, LANE, B), F32)],
        compiler_params=_cp("arbitrary"))(qt, kt, qb, ka, va, doa, sel)


def _fox2_post(zf, dqt, dkn, dv, dfk, sel, gq, gk, bf, name):
    T = zf.shape[0]
    tm = _tile(T, FOX_TM)
    nt = T // tm
    B = dqt.shape[3]
    per = B // tm
    upper = jnp.asarray(_tri(tm, False), _MXU)

    def body(z_ref, dqt_ref, dk_ref, dv_ref, df_ref, p_ref, gq_ref, gk_ref, bf_ref, u_ref, dz_ref, sm_ref, carry, rc_s):
        @pl.when(pl.program_id(0) == 0)
        def _():
            carry[...] = jnp.zeros_like(carry)
            sm_ref[...] = jnp.zeros_like(sm_ref)

        lane = lax.broadcasted_iota(jnp.int32, (1, LANE), 1)
        dqn = jnp.zeros((tm, CH), F32)
        dfq = jnp.zeros((tm, LANE), F32)
        for h in range(4):
            blk = dqt_ref[0, h].T
            dqn = dqn + _exact_dot(blk * (HEAD ** -0.5), p_ref[h], _NT, "a")
            dfq = jnp.where(lane == h, _lane_col(blk, AUG_F), dfq)
        z = z_ref[...]
        q_raw, k_raw = z[:, :CH], z[:, CH:2 * CH]
        _, q_r = _head_rms(q_raw, gq_ref[...], 4)
        _, k_r = _head_rms(k_raw, gk_ref[...], 4)
        dq, dgq = _head_rms_bwd(dqn, q_raw, q_r, gq_ref[...], 4)
        dk, dgk = _head_rms_bwd(dk_ref[...], k_raw, k_r, gk_ref[...], 4)
        rc_s[...] = _exact_dot(u_ref[...], dfq + df_ref[...], _NN, "b") + carry[pl.ds(0, 1), :]
        carry[pl.ds(0, 1), :] = rc_s[pl.ds(0, 1), :]
        x = z[:, 3 * CH:] + bf_ref[...]
        dff = jnp.where(lane < 4, rc_s[...] * _sigmoid(-x), 0.0)
        dz_ref[:, :CH] = dq.astype(dz_ref.dtype)
        dz_ref[:, CH:2 * CH] = dk.astype(dz_ref.dtype)
        dz_ref[:, 2 * CH:3 * CH] = dv_ref[...].astype(dz_ref.dtype)
        dz_ref[:, 3 * CH:] = dff.astype(dz_ref.dtype)
        sm_ref[pl.ds(0, 1), :] += dgq
        sm_ref[pl.ds(1, 1), :] += dgk
        sm_ref[pl.ds(2, 1), :LANE] += jnp.sum(dff, axis=0, keepdims=True)

    rev = lambda i: (nt - 1 - i, 0)
    row = pl.BlockSpec((tm, CH), rev)
    lrow = pl.BlockSpec((tm, LANE), rev)
    vec = pl.BlockSpec((1, CH), lambda i: (0, 0))
    return pl.pallas_call(
        body, name=name, grid=(nt,),
        in_specs=[pl.BlockSpec((tm, 3 * CH + LANE), rev),
                  pl.BlockSpec((1, 4, LANE, tm), lambda i: ((nt - 1 - i) // per, 0, 0, (nt - 1 - i) % per)),
                  row, row, lrow, pl.BlockSpec(sel.shape, lambda i: (0, 0, 0)), vec, vec,
                  pl.BlockSpec((1, LANE), lambda i: (0, 0)), pl.BlockSpec((tm, tm), lambda i: (0, 0))],
        out_specs=[pl.BlockSpec((tm, 3 * CH + LANE), rev), pl.BlockSpec((8, CH), lambda i: (0, 0))],
        out_shape=[jax.ShapeDtypeStruct((T, 3 * CH + LANE), _MXU), jax.ShapeDtypeStruct((8, CH), F32)],
        scratch_shapes=[pltpu.VMEM((8, LANE), F32), pltpu.VMEM((tm, LANE), F32)],
        compiler_params=_cp("arbitrary"))(zf, dqt, dkn, dv, dfk, sel, gq, gk, bf, upper)


def _merge_fwd(acts, zg, wbr, wout, x1, name):
    T, D = x1.shape
    tm = _tile(T, 256)

    def body(a0, a1, a2, a3, zg_ref, wbr_ref, wout_ref, x_ref, o_ref, mg_ref):
        merged = None
        for i, a_ref in enumerate((a0, a1, a2, a3)):
            term = _sigmoid(zg_ref[:, i * D:(i + 1) * D]) * _dot(a_ref[...], wbr_ref[i])
            merged = term if merged is None else merged + term
        mg_ref[...] = merged.astype(mg_ref.dtype)
        o_ref[...] = x_ref[...] + _dot(merged, wout_ref[...])

    arow = pl.BlockSpec((tm, CH), lambda i: (i, 0))
    xrow = pl.BlockSpec((tm, D), lambda i: (i, 0))
    return pl.pallas_call(
        body, name=name, grid=(T // tm,),
        in_specs=[arow, arow, arow, arow, pl.BlockSpec((tm, 4 * D), lambda i: (i, 0)),
                  pl.BlockSpec((4, CH, D), lambda i: (0, 0, 0)), pl.BlockSpec((D, D), lambda i: (0, 0)), xrow],
        out_specs=[xrow, xrow],
        out_shape=[jax.ShapeDtypeStruct((T, D), F32), jax.ShapeDtypeStruct((T, D), _MXU)],
        compiler_params=_cp("parallel"))(*acts, zg, wbr, wout, x1)


def _merge_bwd(dx2, acts, zg, wbr, wout, name):
    T, D = dx2.shape
    tm = _tile(T, 256)
    nt = T // tm

    def body(dx_ref, a0, a1, a2, a3, zg_ref, wbr_ref, wout_ref, d0, d1, d2, d3, dzg_ref, dw_ref, dw_s):
        i = pl.program_id(0)

        @pl.when(i == 0)
        def _():
            dw_s[...] = jnp.zeros_like(dw_s)

        dm = _dot(dx_ref[...], wout_ref[...], _NT)
        for b, (a_ref, d_ref) in enumerate(((a0, d0), (a1, d1), (a2, d2), (a3, d3))):
            av = a_ref[...].astype(_MXU)
            g = _sigmoid(zg_ref[:, b * D:(b + 1) * D])
            p = _dot(av, wbr_ref[b])
            dzg_ref[:, b * D:(b + 1) * D] = (dm * p * (g * (1.0 - g))).astype(dzg_ref.dtype)
            dp = (dm * g).astype(_MXU)
            d_ref[...] = _dot(dp, wbr_ref[b], _NT)
            dw_s[b] += _dot(av, dp, _TN)

        @pl.when(i == nt - 1)
        def _():
            dw_ref[...] = dw_s[...].astype(dw_ref.dtype)

    arow = pl.BlockSpec((tm, CH), lambda i: (i, 0))
    xrow = pl.BlockSpec((tm, D), lambda i: (i, 0))
    grow = pl.BlockSpec((tm, 4 * D), lambda i: (i, 0))
    wspec = pl.BlockSpec((4, CH, D), lambda i: (0, 0, 0))
    ash = jax.ShapeDtypeStruct((T, CH), F32)
    return pl.pallas_call(
        body, name=name, grid=(nt,),
        in_specs=[xrow, arow, arow, arow, arow, grow, wspec, pl.BlockSpec((D, D), lambda i: (0, 0))],
        out_specs=[arow, arow, arow, arow, grow, wspec],
        out_shape=[ash, ash, ash, ash, jax.ShapeDtypeStruct((T, 4 * D), _MXU), jax.ShapeDtypeStruct((4, CH, D), _MXU)],
        scratch_shapes=[pltpu.VMEM((4, CH, D), F32)],
        compiler_params=_cp("arbitrary"))(dx2, *acts, zg, wbr, wout)


def _rows_2d(a):
    return a.reshape((-1, a.shape[-1])) if a.ndim > 1 else a.reshape((1, -1))


def _row_tile(rows, cols, n_bufs):
    padded = -(-cols // LANE) * LANE
    cap = max(8, (VMEM_LIMIT // 3) // (2 * n_bufs * 4 * padded))
    return _tile(rows, cap, 8)


def _sum8(recv, name):
    shape = recv.shape[1:]
    r2 = recv.reshape((N_DEV, -1, shape[-1]))
    rows, cols = r2.shape[1:]
    tr = _row_tile(rows, cols, N_DEV // 2 + 1)

    def body(r_ref, o_ref):
        acc = r_ref[0].astype(F32)
        for d in range(1, N_DEV):
            acc = acc + r_ref[d].astype(F32)
        o_ref[...] = acc

    out = pl.pallas_call(
        body, name=name, grid=(rows // tr,),
        in_specs=[pl.BlockSpec((N_DEV, tr, cols), lambda i: (0, i, 0))],
        out_specs=pl.BlockSpec((tr, cols), lambda i: (i, 0)),
        out_shape=jax.ShapeDtypeStruct((rows, cols), F32), compiler_params=_cp("parallel"))(r2)
    return out.reshape(shape)


def _adamw(w, g, m, v, name):
    shape = w.shape
    w2, g2, m2, v2 = (_rows_2d(a) for a in (w, g, m, v))
    rows, cols = w2.shape
    tr = _row_tile(rows, cols, 7)

    def body(w_ref, g_ref, m_ref, v_ref, d_ref, nm_ref, nv_ref):
        gv = g_ref[...]
        nm = ADAM_B1 * m_ref[...] + (1.0 - ADAM_B1) * gv
        nv = ADAM_B2 * v_ref[...] + (1.0 - ADAM_B2) * jnp.square(gv)
        m_hat = nm / (1.0 - ADAM_B1 ** ADAM_STEP)
        v_hat = nv / (1.0 - ADAM_B2 ** ADAM_STEP)
        d_ref[...] = -ADAM_LR * (m_hat / (jnp.sqrt(v_hat) + ADAM_EPS) + ADAM_WD * w_ref[...])
        nm_ref[...] = nm
        nv_ref[...] = nv

    spec = pl.BlockSpec((tr, cols), lambda i: (i, 0))
    osh = jax.ShapeDtypeStruct((rows, cols), F32)
    outs = pl.pallas_call(
        body, name=name, grid=(rows // tr,), in_specs=[spec] * 4, out_specs=[spec] * 3,
        out_shape=[osh] * 3, compiler_params=_cp("parallel"))(w2, g2, m2, v2)
    return tuple(o.reshape(shape) for o in outs)


def _exchange(items, name):
    n = len(items)
    widths, out_shapes = [], []
    for src, kind, ax in items:
        if kind == "gather":
            w = src.shape[ax]
            shp = list(src.shape)
            shp[ax] = N_DEV * w
        else:
            w = src.shape[ax] // N_DEV
            shp = list(src.shape)
            shp[ax] = w
            shp = [N_DEV] + shp
        widths.append(w)
        out_shapes.append(jax.ShapeDtypeStruct(tuple(shp), src.dtype))

    def body(*refs):
        srcs, outs = refs[:n], refs[n:2 * n]
        send, recv, lsem = refs[2 * n:]
        x, y, c = lax.axis_index("x"), lax.axis_index("y"), lax.axis_index("c")
        me = 4 * x + 2 * y + c

        def peer(k):
            b = k + 1
            px = 1 - x if b & 4 else x
            py = 1 - y if b & 2 else y
            pc = 1 - c if b & 1 else c
            return (px, py, pc), 4 * px + 2 * py + pc

        def win(ref, ax, idx, w):
            return ref.at[tuple([slice(None)] * ax + [pl.ds(idx * w, w)])]

        def ends(j, mine, theirs):
            _, kind, ax = items[j]
            if kind == "gather":
                return srcs[j], win(outs[j], ax, mine, widths[j])
            return win(srcs[j], ax, theirs, widths[j]), outs[j].at[mine]

        local, sent = [], []
        for j in range(n):
            s, d = ends(j, me, me)
            cp = pltpu.make_async_copy(s, d, lsem.at[j])
            cp.start()
            local.append(cp)
            for k in range(N_DEV - 1):
                dev, pid = peer(k)
                s, d = ends(j, me, pid)
                cp = pltpu.make_async_remote_copy(s, d, send.at[j, k], recv.at[j, k], device_id=dev,
                                                  device_id_type=pl.DeviceIdType.MESH)
                cp.start()
                sent.append(cp)
        for j in range(n):
            for k in range(N_DEV - 1):
                dev, pid = peer(k)
                s, d = ends(j, pid, me)
                pltpu.make_async_remote_copy(s, d, send.at[j, k], recv.at[j, k], device_id=dev,
                                             device_id_type=pl.DeviceIdType.MESH).wait_recv()
        for cp in sent:
            cp.wait_send()
        for cp in local:
            cp.wait()

    hbm = pl.BlockSpec(memory_space=pl.ANY)
    return pl.pallas_call(
        body, name=name, in_specs=[hbm] * n, out_specs=[hbm] * n, out_shape=out_shapes,
        scratch_shapes=[pltpu.SemaphoreType.DMA((n, N_DEV - 1)), pltpu.SemaphoreType.DMA((n, N_DEV - 1)),
                        pltpu.SemaphoreType.DMA((n,))],
        compiler_params=pltpu.CompilerParams(has_side_effects=True))(*[it[0] for it in items])


def _exchange_plan(items):
    widths, out_shapes = [], []
    for src, kind, ax in items:
        shp = list(src.shape)
        if kind == "gather":
            w = src.shape[ax]
            shp[ax] = N_DEV * w
        else:
            w = src.shape[ax] // N_DEV
            shp[ax] = w
            shp = [N_DEV] + shp
        widths.append(w)
        out_shapes.append((tuple(shp), src.dtype))
    return widths, out_shapes


def _exchange_refs(items, widths, srcs, outs):
    x, y, c = lax.axis_index("x"), lax.axis_index("y"), lax.axis_index("c")
    me = 4 * x + 2 * y + c

    def peer(k):
        b = k + 1
        px = 1 - x if b & 4 else x
        py = 1 - y if b & 2 else y
        pc = 1 - c if b & 1 else c
        return (px, py, pc), 4 * px + 2 * py + pc

    def win(ref, ax, idx, w):
        return ref.at[tuple([slice(None)] * ax + [pl.ds(idx * w, w)])]

    def ends(j, mine, theirs):
        _, kind, ax = items[j]
        if kind == "gather":
            return srcs[j], win(outs[j], ax, mine, widths[j])
        return win(srcs[j], ax, theirs, widths[j]), outs[j].at[mine]

    return me, peer, ends


_HBM = pl.BlockSpec(memory_space=pltpu.HBM)
_SEM = pl.BlockSpec(memory_space=pltpu.SEMAPHORE)


def _exchange_start(items, name):
    n = len(items)
    widths, out_shapes = _exchange_plan(items)
    meta = [(None, kind, ax) for _, kind, ax in items]

    def body(*refs):
        srcs, lands = refs[:n], refs[n:2 * n]
        send, recv, lsem = refs[2 * n], refs[2 * n + 1], refs[2 * n + 2]
        token = refs[-1]
        me, peer, ends = _exchange_refs(meta, widths, srcs, lands)
        for j in range(n):
            for k in range(N_DEV - 1):
                dev, pid = peer(k)
                s, d = ends(j, me, pid)
                q = j * (N_DEV - 1) + k
                pltpu.make_async_remote_copy(s, d, send.at[q], recv.at[q], device_id=dev,
                                             device_id_type=pl.DeviceIdType.MESH).start()
        for j in range(n):
            s, d = ends(j, me, me)
            pltpu.make_async_copy(s, d, lsem.at[j]).start()
        token[...] = jnp.zeros_like(token)

    srcs = [pltpu.with_memory_space_constraint(it[0], pltpu.HBM) for it in items]
    lands = [pltpu.with_memory_space_constraint(lax.empty(shp, dt), pltpu.HBM) for shp, dt in out_shapes]
    outs = pl.pallas_call(
        body, name=name,
        out_shape=(pltpu.SemaphoreType.DMA((n * (N_DEV - 1),)), pltpu.SemaphoreType.DMA((n * (N_DEV - 1),)),
                   pltpu.SemaphoreType.DMA((n,)),
                   *[pltpu.HBM(s.shape, s.dtype) for s in srcs], *[pltpu.HBM(shp, dt) for shp, dt in out_shapes],
                   jax.ShapeDtypeStruct((8, LANE), F32)),
        in_specs=[_HBM] * (2 * n),
        out_specs=(_SEM, _SEM, _SEM, *([_HBM] * (2 * n)), pl.BlockSpec(memory_space=pltpu.VMEM)),
        input_output_aliases={i: 3 + i for i in range(2 * n)},
        compiler_params=pltpu.CompilerParams(has_side_effects=pltpu.SideEffectType.DATAFLOW_SIDE_EFFECTING),
    )(*srcs, *lands)
    handle = (meta, widths, outs[0], outs[1], outs[2], outs[3:3 + n], outs[3 + n:3 + 2 * n])
    return handle, outs[-1]


def _exchange_wait(handle, after, name):
    meta, widths, send_sem, recv_sem, local_sem, src_thru, land_thru = handle
    n = len(meta)

    def body(*refs):
        srcs, lands = refs[:n], refs[n:2 * n]
        send, recv, lsem = refs[2 * n], refs[2 * n + 1], refs[2 * n + 2]
        me, peer, ends = _exchange_refs(meta, widths, srcs, lands)
        for j in range(n):
            for k in range(N_DEV - 1):
                dev, pid = peer(k)
                q = j * (N_DEV - 1) + k
                s, d = ends(j, me, pid)
                pltpu.make_async_remote_copy(s, d, send.at[q], recv.at[q], device_id=dev,
                                             device_id_type=pl.DeviceIdType.MESH).wait_send()
                s, d = ends(j, pid, me)
                pltpu.make_async_remote_copy(s, d, send.at[q], recv.at[q], device_id=dev,
                                             device_id_type=pl.DeviceIdType.MESH).wait_recv()
        for j in range(n):
            s, d = ends(j, me, me)
            pltpu.make_async_copy(s, d, lsem.at[j]).wait()

    outs = pl.pallas_call(
        body, name=name,
        out_shape=tuple(pltpu.HBM(a.shape, a.dtype) for a in (*src_thru, *land_thru)),
        in_specs=[_HBM] * (2 * n) + [_SEM, _SEM, _SEM, pl.BlockSpec(memory_space=pl.ANY)],
        out_specs=tuple([_HBM] * (2 * n)),
        input_output_aliases={i: i for i in range(2 * n)},
        compiler_params=pltpu.CompilerParams(has_side_effects=pltpu.SideEffectType.DATAFLOW_SIDE_EFFECTING),
    )(*src_thru, *land_thru, send_sem, recv_sem, local_sem, after)
    return list(outs[n:])


def _pack(arrs):
    flat = jnp.concatenate([a.reshape(-1).astype(F32) for a in arrs])
    n = flat.shape[0]
    rows = -(-n // (8 * LANE)) * 8
    return jnp.pad(flat, (0, rows * LANE - n)).reshape(rows, LANE)


def _unpack(buf, shapes):
    flat = buf.reshape(-1)
    out, off = [], 0
    for s in shapes:
        sz = int(np.prod(s))
        out.append(flat[off:off + sz].reshape(s))
        off += sz
    return out


def _pad_axis(a, axis, size):
    pad = [(0, 0)] * a.ndim
    pad[axis] = (0, size - a.shape[axis])
    return jnp.pad(a, pad)


def _ffn_forward(x, g, wg, wu, wd, tag):
    a = _rms_fwd(x, g, f"{tag}_rms")
    gate, up, hid = _ffn_up(a, wg, wu, f"{tag}_up")
    out = _mm([(hid, wd)], "nn", F32, f"{tag}_down", scale=0.5, res=x)
    return out, (x, a, gate, up, hid)


def _ffn_backward(dxp, saved, g, wg, wu, wd, tag, emit=None):
    x, a, gate, up, hid = saved
    d_gate, d_up = _ffn_bwd_hid(dxp, wd, gate, up, f"{tag}_bwd_hid")
    d_wd = _mm([(hid, dxp)], "tn", _MXU, f"{tag}_dwd", scale=0.5, tk=2048)
    if emit is not None:
        emit("down", d_wd)
    d_wg = _mm([(a, d_gate)], "tn", _MXU, f"{tag}_dwg", tk=2048)
    if emit is not None:
        emit("gate", d_wg)
    d_wu = _mm([(a, d_up)], "tn", _MXU, f"{tag}_dwu", tk=2048)
    if emit is not None:
        emit("up", d_wu)
    d_a = _mm([(d_gate, wg), (d_up, wu)], "nt", F32, f"{tag}_da")
    dx, dg = _rms_bwd(d_a, x, g, dxp, f"{tag}_rms_bwd")
    return dx, dg, d_wg, d_wu, d_wd


def _tile_vec(v, reps):
    return jnp.tile(v.reshape(1, -1), (1, reps))


def _mixer_forward(x1, p, consts, tag):
    h = _rms_fwd(x1, p["mix_norm"], f"{tag}_rms")
    zg = _mm([(h, p["w_zg"])], "nn", F32, f"{tag}_in_g")
    zc = _mm([(h, p["w_conf"])], "nn", F32, f"{tag}_in_c")
    zs = _mm([(h, p["w_sc"])], "nn", F32, f"{tag}_in_s")
    zw = _mm([(h, p["w_swa"])], "nn", F32, f"{tag}_in_w")
    zf = _mm([(h, p["w_fox"])], "nn", F32, f"{tag}_in_f")
    u1, act_c = _conf_fwd(zc, p["conf_dw"], p["conf_dw_b"], p["conf_ln_g"], p["conf_ln_b"], f"{tag}_conf")
    act_s = _sc_fwd(zs, p["sc_conv"], f"{tag}_sc")
    act_w = _swa_fwd(zw, p["swa_q_norm"], p["swa_k_norm"], p["swa_sink"], consts["bias"], consts["expand"], f"{tag}_swa")
    qa, ka, va = _fox2_prep(zf, p["fox_q_norm"], p["fox_k_norm"], p["b_forget"], consts["sel"], f"{tag}_fox_prep")
    act_f, qb = _fox2_fwd(qa, ka, va, consts["sel"], f"{tag}_fox")
    acts = (act_c, act_s, act_w, act_f)
    x2, merged = _merge_fwd(acts, zg, p["w_br"], p["w_out"], x1, f"{tag}_merge")
    saved = (x1, h, zg, zc, zs, zw, zf, u1, acts, qb, ka, va, merged)
    return x2, saved


def _mixer_backward(dx2, saved, p, consts, tag):
    x1, h, zg, zc, zs, zw, zf, u1, acts, qb, ka, va, merged = saved
    g = {}
    g["w_out"] = _mm([(merged, dx2)], "tn", _MXU, f"{tag}_dwout", tk=2048)
    d_c, d_s, d_w, d_f, dzg, g["w_br"] = _merge_bwd(dx2, acts, zg, p["w_br"], p["w_out"], f"{tag}_merge_bwd")
    du1, sm_c = _conf_bwd_ln(d_c, u1, p["conf_ln_g"], p["conf_ln_b"], f"{tag}_conf_bwd_ln")
    dzc, g["conf_dw"] = _conf_bwd_conv(zc, du1, p["conf_dw"], f"{tag}_conf_bwd_conv")
    g["conf_ln_g"], g["conf_ln_b"], g["conf_dw_b"] = sm_c[0], sm_c[1], sm_c[2]
    dzs, g["sc_conv"] = _sc_bwd(zs, d_s, p["sc_conv"], f"{tag}_sc_bwd")
    dzw, dgq, dgk, g["swa_sink"], g["rel_bias"] = _swa_bwd(
        zw, d_w, p["swa_q_norm"], p["swa_k_norm"], p["swa_sink"], consts["bias"], consts["bucket"], consts["expand"],
        f"{tag}_swa_bwd")
    g["swa_q_norm"], g["swa_k_norm"] = dgq, dgk
    doa = _fox2_bwd_prep(acts[3], d_f, consts["sel"], f"{tag}_fox_bwd_prep")
    dkn, dv, dfk, dqt = _fox2_bwd(qb, ka, va, doa, consts["sel"], f"{tag}_fox_bwd")
    dzf, sm_f = _fox2_post(zf, dqt, dkn, dv, dfk, consts["sel"], p["fox_q_norm"], p["fox_k_norm"], p["b_forget"], f"{tag}_fox_post")
    g["fox_q_norm"], g["fox_k_norm"], g["b_forget"] = sm_f[0], sm_f[1], sm_f[2]
    parts = ((dzg, "w_zg"), (dzc, "w_conf"), (dzs, "w_sc"), (dzw, "w_swa"), (dzf, "w_fox"))
    for dz, wname in parts:
        g[wname] = _mm([(h, dz)], "tn", _MXU, f"{tag}_d{wname}", tk=2048)
    dh = _mm([(dz, p[wname]) for dz, wname in parts], "nt", F32, f"{tag}_dh", tm=512)
    dx1, g["mix_norm"] = _rms_bwd(dh, x1, p["mix_norm"], dx2, f"{tag}_rms_bwd")
    return dx1, g


W_NAMES = ['rel_bias', 'ffn1_norm', 'ffn1_w_gate', 'ffn1_w_up', 'ffn1_w_down', 'mix_norm', 'w_in', 'b_forget', 'conf_dw',
           'conf_dw_b', 'conf_ln_g', 'conf_ln_b', 'conf_w_out', 'sc_conv', 'sc_w_out', 'swa_q_norm', 'swa_k_norm',
           'swa_sink', 'swa_w_o', 'fox_q_norm', 'fox_k_norm', 'fox_w_o', 'w_out', 'ffn2_norm', 'ffn2_w_gate',
           'ffn2_w_up', 'ffn2_w_down']
SMALL = ['rel_bias', 'ffn1_norm', 'mix_norm', 'b_forget', 'conf_dw', 'conf_dw_b', 'conf_ln_g', 'conf_ln_b', 'sc_conv',
         'swa_q_norm', 'swa_k_norm', 'swa_sink', 'fox_q_norm', 'fox_k_norm', 'ffn2_norm']
BRANCH_W = ['conf_w_out', 'sc_w_out', 'swa_w_o', 'fox_w_o']
IN_CONF, IN_SC, IN_SWA, IN_FOX, IN_FF = (0, 512), (512, 1280), (1280, 1792), (1792, 2560), (2560, 2564)


def _step(w, m, v, x, loss_target):
    T, D = x.shape
    L = w["w_out"].shape[0]
    fs = w["ffn1_w_gate"].shape[2]
    fsp = -(-fs // LANE) * LANE
    dev = 4 * lax.axis_index("x") + 2 * lax.axis_index("y") + lax.axis_index("c")

    def cast(a):
        return a.astype(_MXU)

    win = w["w_in"]
    fox_cols = jnp.concatenate([win[..., IN_FOX[0]:IN_FF[1]],
                                jnp.zeros(win.shape[:2] + (LANE - (IN_FF[1] - IN_FF[0]),), win.dtype)], axis=-1)
    shards = {
        "ffn1_w_gate": (cast(_pad_axis(w["ffn1_w_gate"], 2, fsp)), 2),
        "ffn1_w_up": (cast(_pad_axis(w["ffn1_w_up"], 2, fsp)), 2),
        "ffn1_w_down": (cast(_pad_axis(w["ffn1_w_down"], 1, fsp)), 1),
        "ffn2_w_gate": (cast(_pad_axis(w["ffn2_w_gate"], 2, fsp)), 2),
        "ffn2_w_up": (cast(_pad_axis(w["ffn2_w_up"], 2, fsp)), 2),
        "ffn2_w_down": (cast(_pad_axis(w["ffn2_w_down"], 1, fsp)), 1),
        "w_zg": (cast(win[..., IN_FF[1]:]), 1),
        "w_conf": (cast(win[..., IN_CONF[0]:IN_CONF[1]]), 1),
        "w_sc": (cast(win[..., IN_SC[0]:IN_SC[1]]), 1),
        "w_swa": (cast(win[..., IN_SWA[0]:IN_SWA[1]]), 1),
        "w_fox": (cast(fox_cols), 1),
        "w_out": (cast(w["w_out"]), 1),
        "w_br": (cast(jnp.stack([w[n] for n in BRANCH_W], axis=1)), 3),
    }
    big = list(shards)
    conv_shard = jnp.concatenate([jnp.swapaxes(w["conf_dw"], 1, 2), jnp.swapaxes(w["sc_conv"], 1, 2)], axis=2)
    conv_full = jnp.swapaxes(_exchange([(conv_shard, "gather", 1)], "gather_conv")[0], 1, 2)
    conf_dw_full = _pad_axis(conv_full[:, :CONV_K], 1, CONV_HALO)
    sc_conv_full = _pad_axis(conv_full[:, CONV_K:], 1, SC_HALO)

    stages = (("ffn1", ["ffn1_w_gate", "ffn1_w_up"]), ("ffn1d", ["ffn1_w_down"]),
              ("mix", ["w_zg", "w_conf", "w_sc", "w_swa", "w_fox", "w_out", "w_br"]),
              ("ffn2", ["ffn2_w_gate", "ffn2_w_up", "ffn2_w_down"]))
    groups = [(l, st, names) for l in range(L) for st, names in stages]

    def depart(gi, dep):
        l, st, names = groups[gi]
        items = [(shards[n][0][l], "gather", shards[n][1] - 1) for n in names]
        if dep is not None:
            src0 = items[0][0]
            zero = (dep[(0,) * dep.ndim].astype(F32) * 0.0).astype(src0.dtype)
            items[0] = (src0 + zero,) + items[0][1:]
        return _exchange_start(items, f"gather_start_l{l}_{st}")

    bucket = jnp.asarray(_swa_bucket_matrix(min(SWA_TQ, T)))
    consts = {"bucket": bucket, "expand": jnp.asarray(_kv_expand_matrix(), _MXU),
              "sel": jnp.asarray(_head_select_matrix(), _MXU),
              "bias": _swa_bias(w["rel_bias"], bucket, "swa_bias")}

    def layer_params(l):
        p = {}
        for n in ("ffn1_norm", "mix_norm", "ffn2_norm", "conf_dw_b", "conf_ln_g", "conf_ln_b"):
            p[n] = w[n][l].reshape(1, -1)
        p["conf_dw"], p["sc_conv"] = conf_dw_full[l], sc_conv_full[l]
        p["swa_q_norm"], p["fox_q_norm"] = _tile_vec(w["swa_q_norm"][l], 4), _tile_vec(w["fox_q_norm"][l], 4)
        p["swa_k_norm"], p["fox_k_norm"] = _tile_vec(w["swa_k_norm"][l], 2), _tile_vec(w["fox_k_norm"][l], 4)
        p["swa_sink"] = w["swa_sink"][l].reshape(1, 4)
        p["b_forget"] = _pad_axis(w["b_forget"][l].reshape(1, 4), 1, LANE)
        return p

    params = [layer_params(l) for l in range(L)]
    saved = [None] * L
    flight = {}
    flight["handle"], first_tok = depart(0, None)

    def advance(gi, after):
        l, st, names = groups[gi]
        got = _exchange_wait(flight["handle"], after, f"gather_wait_l{l}_{st}")
        params[l].update(zip(names, got))
        if gi + 1 == len(groups):
            return 0.0
        flight["handle"], tok = depart(gi + 1, got[0])
        return tok[0:1, 0:1]

    cur = x
    for l, p in enumerate(params):
        gi = len(stages) * l
        zero = advance(gi, cur if l else first_tok)
        a = _rms_fwd(cur, p["ffn1_norm"] + zero, f"l{l}_ffn1_rms")
        gate, up, hid = _ffn_up(a, p["ffn1_w_gate"], p["ffn1_w_up"], f"l{l}_ffn1_up")
        advance(gi + 1, hid)
        x1 = _mm([(hid, p["ffn1_w_down"])], "nn", F32, f"l{l}_ffn1_down", scale=0.5, res=cur)
        s1 = (cur, a, gate, up, hid)
        zero = advance(gi + 2, x1)
        x2, s2 = _mixer_forward(x1, dict(p, mix_norm=p["mix_norm"] + zero), consts, f"l{l}_mix")
        zero = advance(gi + 3, x2)
        cur, s3 = _ffn_forward(x2, p["ffn2_norm"] + zero, p["ffn2_w_gate"], p["ffn2_w_up"], p["ffn2_w_down"], f"l{l}_ffn2")
        saved[l] = (s1, s2, s3)
    dcur, loss_part = _loss_grad(cur, loss_target)

    grads = [None] * L
    leaving = []

    def leave(l, st, names, g):
        h, tok = _exchange_start([(g[n], "scatter", shards[n][1] - 1) for n in names], f"scatter_start_l{l}_{st}")
        leaving.append((l, st, names, h))
        return tok[0:1, 0:1]

    zero = 0.0
    for l in reversed(range(L)):
        p = params[l]
        s1, s2, s3 = saved[l]
        g = {}
        dcur, g["ffn2_norm"], g["ffn2_w_gate"], g["ffn2_w_up"], g["ffn2_w_down"] = _ffn_backward(
            dcur, s3, p["ffn2_norm"] + zero, p["ffn2_w_gate"], p["ffn2_w_up"], p["ffn2_w_down"], f"l{l}_ffn2")
        zero = leave(l, "ffn2", stages[3][1], g)
        dcur, gm = _mixer_backward(dcur, s2, dict(p, mix_norm=p["mix_norm"] + zero), consts, f"l{l}_mix")
        g.update(gm)
        zero = leave(l, "mix", stages[2][1], g)
        ffn1 = ["ffn1_w_gate", "ffn1_w_up", "ffn1_w_down"]
        emit = (lambda which, arr, l=l: leave(l, which, [f"ffn1_w_{which}"], {f"ffn1_w_{which}": arr})) if l == 0 else None
        dcur, g["ffn1_norm"], g["ffn1_w_gate"], g["ffn1_w_up"], g["ffn1_w_down"] = _ffn_backward(
            dcur, s1, p["ffn1_norm"] + zero, p["ffn1_w_gate"], p["ffn1_w_up"], p["ffn1_w_down"], f"l{l}_ffn1", emit)
        if emit is None:
            zero = leave(l, "ffn1", ffn1, g)
        grads[l] = g
    grad_x = dcur

    gsum = {n: [None] * L for n in big}
    for l, st, names, h in leaving:
        for n, r in zip(names, _exchange_wait(h, grad_x, f"scatter_wait_l{l}_{st}")):
            gsum[n][l] = _sum8(r, f"sum_{n}_l{l}")
    gsum = {n: jnp.stack(parts) for n, parts in gsum.items()}
    gw = {}
    for n in ("ffn1_w_gate", "ffn1_w_up", "ffn2_w_gate", "ffn2_w_up"):
        gw[n] = gsum[n][:, :, :fs]
    for n in ("ffn1_w_down", "ffn2_w_down"):
        gw[n] = gsum[n][:, :fs, :]
    gw["w_out"] = gsum["w_out"]
    for i, n in enumerate(BRANCH_W):
        gw[n] = gsum["w_br"][:, i]
    gw["w_in"] = jnp.concatenate([gsum["w_conf"], gsum["w_sc"], gsum["w_swa"],
                                  gsum["w_fox"][..., :IN_FF[1] - IN_FOX[0]], gsum["w_zg"]], axis=-1)

    def small_partial(n):
        per_layer = [grads[l][n] for l in range(L)]
        if n == "rel_bias":
            return sum(pl_[:, :4] for pl_ in per_layer)
        if n in ("swa_sink", "b_forget"):
            return jnp.stack([a.reshape(-1)[:4] for a in per_layer])
        if n in ("swa_q_norm", "fox_q_norm", "fox_k_norm"):
            return jnp.stack([a.reshape(4, HEAD).sum(0) for a in per_layer])
        if n == "swa_k_norm":
            return jnp.stack([a.reshape(2, HEAD).sum(0) for a in per_layer])
        if n == "conf_dw":
            return jnp.stack([a[:CONV_K] for a in per_layer])
        if n == "sc_conv":
            return jnp.stack([a[:SC_K] for a in per_layer])
        return jnp.stack([a.reshape(-1) for a in per_layer])

    partial = [small_partial(n) for n in SMALL]
    small_shapes = [a.shape for a in partial]
    all_parts = _exchange([(_pack(partial), "gather", 0)], "gather_small_grads")[0]
    rows = all_parts.shape[0] // N_DEV
    small_sum = _unpack(_sum8(all_parts.reshape(N_DEV, rows, LANE), "sum_small"), small_shapes)
    for n, a in zip(SMALL, small_sum):
        if n in ("conf_dw", "sc_conv"):
            cs = w[n].shape[2]
            a = lax.dynamic_slice_in_dim(a, dev * cs, cs, axis=2)
        gw[n] = a

    delta, new_m, new_v = {}, {}, {}
    for n in W_NAMES:
        if n not in SMALL:
            delta[n], new_m[n], new_v[n] = _adamw(w[n], gw[n], m[n], v[n], f"adamw_{n}")
    shapes = [w[n].shape for n in SMALL]
    outs = _adamw(_pack([w[n] for n in SMALL]), _pack([gw[n] for n in SMALL]), _pack([m[n] for n in SMALL]),
                  _pack([v[n] for n in SMALL]), "adamw_small")
    for res, out in zip((delta, new_m, new_v), outs):
        for n, a in zip(SMALL, _unpack(out, shapes)):
            res[n] = a

    loss = lax.psum(loss_part[0, 0], ("x", "y", "c"))
    return loss, grad_x, gw, delta, new_m, new_v


def kernel(x, rel_bias, ffn1_norm, ffn1_w_gate, ffn1_w_up, ffn1_w_down, mix_norm, w_in, b_forget, conf_dw, conf_dw_b, conf_ln_g, conf_ln_b, conf_w_out, sc_conv, sc_w_out, swa_q_norm, swa_k_norm, swa_sink, swa_w_o, fox_q_norm, fox_k_norm, fox_w_o, w_out, ffn2_norm, ffn2_w_gate, ffn2_w_up, ffn2_w_down, loss_target, m_rel_bias, m_ffn1_norm, m_ffn1_w_gate, m_ffn1_w_up, m_ffn1_w_down, m_mix_norm, m_w_in, m_b_forget, m_conf_dw, m_conf_dw_b, m_conf_ln_g, m_conf_ln_b, m_conf_w_out, m_sc_conv, m_sc_w_out, m_swa_q_norm, m_swa_k_norm, m_swa_sink, m_swa_w_o, m_fox_q_norm, m_fox_k_norm, m_fox_w_o, m_w_out, m_ffn2_norm, m_ffn2_w_gate, m_ffn2_w_up, m_ffn2_w_down, v_rel_bias, v_ffn1_norm, v_ffn1_w_gate, v_ffn1_w_up, v_ffn1_w_down, v_mix_norm, v_w_in, v_b_forget, v_conf_dw, v_conf_dw_b, v_conf_ln_g, v_conf_ln_b, v_conf_w_out, v_sc_conv, v_sc_w_out, v_swa_q_norm, v_swa_k_norm, v_swa_sink, v_swa_w_o, v_fox_q_norm, v_fox_k_norm, v_fox_w_o, v_w_out, v_ffn2_norm, v_ffn2_w_gate, v_ffn2_w_up, v_ffn2_w_down):
    args = locals()
    w = {n: args[n] for n in W_NAMES}
    m = {n: args["m_" + n] for n in W_NAMES}
    v = {n: args["v_" + n] for n in W_NAMES}
    T, D = x.shape[-2:]
    loss, grad_x, gw, delta, new_m, new_v = _step(w, m, v, x.reshape(T, D), loss_target.reshape(T, D))
    return (loss, grad_x.reshape(x.shape), *[gw[n] for n in W_NAMES], *[delta[n] for n in W_NAMES],
            *[new_m[n] for n in W_NAMES], *[new_v[n] for n in W_NAMES])
```

```python
import math

import numpy as np
import jax
import jax.numpy as jnp
from jax import lax
from jax.experimental import pallas as pl
from jax.experimental.pallas import tpu as pltpu

F32 = jnp.float32
_MXU = jnp.bfloat16
EPS = 1e-6
NEG_INF = -1e30
HEAD = 64
CH = 256
WINDOW = 128
CONV_K = 31
SC_K = 3
CONV_HALO = 32
SC_HALO = 8
N_BUCKETS = 32
MAX_DISTANCE = 128
N_DEV = 8
LANE = 128
ROW_TILE = 512
VMEM_LIMIT = 48 * 1024 * 1024
ADAM_LR, ADAM_B1, ADAM_B2, ADAM_EPS, ADAM_WD, ADAM_STEP = 0.001, 0.9, 0.999, 1e-08, 0.01, 10

_NN = (((1,), (0,)), ((), ()))
_NT = (((1,), (1,)), ((), ()))
_TN = (((0,), (0,)), ((), ()))


def _cp(*sem):
    return pltpu.CompilerParams(dimension_semantics=sem, vmem_limit_bytes=VMEM_LIMIT)


def _tile(n, pref, align=LANE):
    t = (min(n, pref) // align) * align
    while t >= align:
        if n % t == 0:
            return t
        t -= align
    return n


def _dot(a, b, dims=_NN):
    return lax.dot_general(a.astype(_MXU), b.astype(_MXU), dims, preferred_element_type=F32)


def _split3(x):
    hi = x.astype(_MXU)
    r1 = x - hi.astype(F32)
    mid = r1.astype(_MXU)
    lo = (r1 - mid.astype(F32)).astype(_MXU)
    return hi, mid, lo


def _exact_dot(a, b, dims, data):
    if data == "a":
        return sum(lax.dot_general(p, b.astype(_MXU), dims, preferred_element_type=F32) for p in _split3(a))
    return sum(lax.dot_general(a.astype(_MXU), p, dims, preferred_element_type=F32) for p in _split3(b))


def _sigmoid(x):
    return jax.nn.sigmoid(x)


def _lane_mask(width, h):
    lane = lax.broadcasted_iota(jnp.int32, (1, width), 1)
    return (lane >= h * HEAD) & (lane < (h + 1) * HEAD)


def _head_rms(x, g, nh):
    xx = x * x
    ms = jnp.zeros_like(x)
    for h in range(nh):
        mk = _lane_mask(x.shape[-1], h)
        s = jnp.sum(jnp.where(mk, xx, 0.0), axis=-1, keepdims=True) * (1.0 / HEAD)
        ms = jnp.where(mk, s, ms)
    r = lax.rsqrt(ms + EPS)
    return x * r * g, r


def _head_rms_bwd(dy, x, r, g, nh):
    w = dy * g
    wx = w * x
    c = jnp.zeros_like(x)
    for h in range(nh):
        mk = _lane_mask(x.shape[-1], h)
        s = jnp.sum(jnp.where(mk, wx, 0.0), axis=-1, keepdims=True) * (1.0 / HEAD)
        c = jnp.where(mk, s, c)
    dx = r * w - x * (r * r * r) * c
    dg = jnp.sum(dy * x * r, axis=0, keepdims=True)
    return dx, dg


def _mm(pairs, mode, out_dtype, name, scale=None, res=None, tm=1024, tn=1024, tk=1024, after=None):
    a0, b0 = pairs[0]
    M = a0.shape[1] if mode == "tn" else a0.shape[0]
    N = b0.shape[0] if mode == "nt" else b0.shape[1]
    tm, tn = _tile(M, tm), _tile(N, tn)
    dims = {"nn": _NN, "nt": _NT, "tn": _TN}[mode]
    tks, nks, offs = [], [], []
    for a, _ in pairs:
        K = a.shape[0] if mode == "tn" else a.shape[1]
        t = _tile(K, tk)
        tks.append(t)
        nks.append(K // t)
        offs.append(sum(nks[:-1]))
    nk_tot = sum(nks)
    in_specs, args = [], []
    for (a, b), t, nk, off in zip(pairs, tks, nks, offs):
        def kk(k, off=off, nk=nk):
            return jnp.clip(k - off, 0, nk - 1)
        if mode == "tn":
            in_specs.append(pl.BlockSpec((t, tm), lambda i, j, k, kk=kk: (kk(k), i)))
        else:
            in_specs.append(pl.BlockSpec((tm, t), lambda i, j, k, kk=kk: (i, kk(k))))
        if mode == "nt":
            in_specs.append(pl.BlockSpec((tn, t), lambda i, j, k, kk=kk: (j, kk(k))))
        else:
            in_specs.append(pl.BlockSpec((t, tn), lambda i, j, k, kk=kk: (kk(k), j)))
        args += [a, b]
    if res is not None:
        in_specs.append(pl.BlockSpec((tm, tn), lambda i, j, k: (i, j)))
        args.append(res)
    if after is not None:
        in_specs.append(pl.BlockSpec(memory_space=pl.ANY))
        args.append(after)
    npairs = len(pairs)

    def body(*refs):
        ab = refs[:2 * npairs]
        res_ref = refs[2 * npairs] if res is not None else None
        o_ref = refs[2 * npairs + (res is not None) + (after is not None)]
        acc = refs[-1]
        k = pl.program_id(2)

        def finish(r):
            if scale is not None:
                r = r * scale
            if res_ref is not None:
                r = r + res_ref[...]
            o_ref[...] = r.astype(o_ref.dtype)

        if nk_tot == 1:
            finish(_dot(ab[0][...], ab[1][...], dims))
            return

        @pl.when(k == 0)
        def _():
            acc[...] = jnp.zeros_like(acc)

        for p in range(npairs):
            @pl.when(jnp.logical_and(k >= offs[p], k < offs[p] + nks[p]))
            def _(p=p):
                acc[...] += _dot(ab[2 * p][...], ab[2 * p + 1][...], dims)

        @pl.when(k == nk_tot - 1)
        def _():
            finish(acc[...])

    return pl.pallas_call(
        body, name=name, grid=(M // tm, N // tn, nk_tot), in_specs=in_specs,
        out_specs=pl.BlockSpec((tm, tn), lambda i, j, k: (i, j)),
        out_shape=jax.ShapeDtypeStruct((M, N), out_dtype),
        scratch_shapes=[pltpu.VMEM((tm, tn), F32)],
        compiler_params=_cp("parallel", "parallel", "arbitrary"))(*args)


def _rms_fwd(x, g, name):
    T, D = x.shape
    tm = _tile(T, ROW_TILE)

    def body(x_ref, g_ref, o_ref):
        xv = x_ref[...]
        r = lax.rsqrt(jnp.mean(xv * xv, axis=-1, keepdims=True) + EPS)
        o_ref[...] = (xv * r * g_ref[...]).astype(o_ref.dtype)

    return pl.pallas_call(
        body, name=name, grid=(T // tm,),
        in_specs=[pl.BlockSpec((tm, D), lambda i: (i, 0)), pl.BlockSpec((1, D), lambda i: (0, 0))],
        out_specs=pl.BlockSpec((tm, D), lambda i: (i, 0)),
        out_shape=jax.ShapeDtypeStruct((T, D), _MXU), compiler_params=_cp("parallel"))(x, g)


def _rms_bwd(da, x, g, dres, name):
    T, D = x.shape
    tm = _tile(T, ROW_TILE)

    def body(da_ref, x_ref, g_ref, dr_ref, dx_ref, dg_ref):
        @pl.when(pl.program_id(0) == 0)
        def _():
            dg_ref[...] = jnp.zeros_like(dg_ref)

        xv, dav = x_ref[...], da_ref[...]
        r = lax.rsqrt(jnp.mean(xv * xv, axis=-1, keepdims=True) + EPS)
        w = dav * g_ref[...]
        c = jnp.mean(w * xv, axis=-1, keepdims=True)
        dx_ref[...] = dr_ref[...] + (r * w - xv * (r * r * r) * c)
        dg_ref[...] += jnp.sum(dav * xv * r, axis=0, keepdims=True)

    row = pl.BlockSpec((tm, D), lambda i: (i, 0))
    vec = pl.BlockSpec((1, D), lambda i: (0, 0))
    return pl.pallas_call(
        body, name=name, grid=(T // tm,), in_specs=[row, row, vec, row], out_specs=[row, vec],
        out_shape=[jax.ShapeDtypeStruct((T, D), F32), jax.ShapeDtypeStruct((1, D), F32)],
        compiler_params=_cp("arbitrary"))(da, x, g, dres)


def _loss_grad(y, tgt):
    T, D = y.shape
    tm = _tile(T, ROW_TILE)

    def body(y_ref, t_ref, dy_ref, l_ref):
        @pl.when(pl.program_id(0) == 0)
        def _():
            l_ref[...] = jnp.zeros_like(l_ref)

        d = y_ref[...] - t_ref[...]
        dy_ref[...] = d * (1.0 / D)
        per_tok = jnp.mean(d * d, axis=-1, keepdims=True)
        l_ref[...] += 0.5 * jnp.sum(per_tok, axis=0, keepdims=True)

    row = pl.BlockSpec((tm, D), lambda i: (i, 0))
    return pl.pallas_call(
        body, name="loss_grad", grid=(T // tm,), in_specs=[row, row],
        out_specs=[row, pl.BlockSpec((1, 1), lambda i: (0, 0))],
        out_shape=[jax.ShapeDtypeStruct((T, D), F32), jax.ShapeDtypeStruct((1, 1), F32)],
        compiler_params=_cp("arbitrary"))(y, tgt)


def _ffn_up(a, wg, wu, name):
    T, D = a.shape
    Fp = wg.shape[1]
    tm, tn = _tile(T, ROW_TILE), _tile(Fp, 768)

    def body(a_ref, wg_ref, wu_ref, g_ref, u_ref, h_ref):
        av = a_ref[...]
        g = _dot(av, wg_ref[...])
        u = _dot(av, wu_ref[...])
        g_ref[...] = g.astype(g_ref.dtype)
        u_ref[...] = u.astype(u_ref.dtype)
        h_ref[...] = (g * _sigmoid(g) * u).astype(h_ref.dtype)

    wspec = pl.BlockSpec((D, tn), lambda j, i: (0, j))
    ospec = pl.BlockSpec((tm, tn), lambda j, i: (i, j))
    osh = jax.ShapeDtypeStruct((T, Fp), _MXU)
    return pl.pallas_call(
        body, name=name, grid=(Fp // tn, T // tm),
        in_specs=[pl.BlockSpec((tm, D), lambda j, i: (i, 0)), wspec, wspec],
        out_specs=[ospec, ospec, ospec], out_shape=[osh, osh, osh],
        compiler_params=_cp("parallel", "parallel"))(a, wg, wu)


def _ffn_bwd_hid(dxp, wd, gate, up, name, after=None):
    T, D = dxp.shape
    Fp = wd.shape[0]
    tm, tn = _tile(T, ROW_TILE), _tile(Fp, 768)
    extra = [] if after is None else [after]

    def body(*refs):
        dx_ref, wd_ref, g_ref, u_ref = refs[:4]
        dg_ref, du_ref = refs[-2:]
        dh = 0.5 * _dot(dx_ref[...], wd_ref[...], _NT)
        g = g_ref[...].astype(F32)
        u = u_ref[...].astype(F32)
        s = _sigmoid(g)
        du_ref[...] = (dh * (g * s)).astype(du_ref.dtype)
        dg_ref[...] = (dh * u * (s * (1.0 + g * (1.0 - s)))).astype(dg_ref.dtype)

    tspec = pl.BlockSpec((tm, tn), lambda j, i: (i, j))
    osh = jax.ShapeDtypeStruct((T, Fp), _MXU)
    return pl.pallas_call(
        body, name=name, grid=(Fp // tn, T // tm),
        in_specs=[pl.BlockSpec((tm, D), lambda j, i: (i, 0)), pl.BlockSpec((tn, D), lambda j, i: (j, 0)), tspec, tspec]
        + [pl.BlockSpec(memory_space=pl.ANY)] * len(extra),
        out_specs=[tspec, tspec], out_shape=[osh, osh],
        compiler_params=_cp("parallel", "parallel"))(dxp, wd, gate, up, *extra)


def _conf_fwd(zc, dw, b, lng, lnb, name):
    T = zc.shape[0]
    tm = _tile(T, ROW_TILE)
    r = tm // CONV_HALO

    def body(z_ref, zh_ref, dw_ref, b_ref, g_ref, lb_ref, u1_ref, act_ref, ext):
        i = pl.program_id(0)
        cur = z_ref[...]
        ext[pl.ds(CONV_HALO, tm), :] = cur[:, :CH] * _sigmoid(cur[:, CH:])
        hal = zh_ref[...]
        ext[pl.ds(0, CONV_HALO), :] = jnp.where(i > 0, hal[:, :CH] * _sigmoid(hal[:, CH:]), 0.0)
        acc = jnp.zeros((tm, CH), F32)
        for k in range(CONV_K):
            acc = acc + dw_ref[pl.ds(k, 1), :] * ext[pl.ds(CONV_HALO - (CONV_K - 1) + k, tm), :]
        u1 = acc + b_ref[...]
        u1_ref[...] = u1
        mu = jnp.mean(u1, axis=-1, keepdims=True)
        var = jnp.mean(jnp.square(u1 - mu), axis=-1, keepdims=True)
        u2 = (u1 - mu) * lax.rsqrt(var + EPS) * g_ref[...] + lb_ref[...]
        act_ref[...] = u2 * _sigmoid(u2)

    vec = pl.BlockSpec((1, CH), lambda i: (0, 0))
    row = pl.BlockSpec((tm, CH), lambda i: (i, 0))
    osh = jax.ShapeDtypeStruct((T, CH), F32)
    return pl.pallas_call(
        body, name=name, grid=(T // tm,),
        in_specs=[pl.BlockSpec((tm, 2 * CH), lambda i: (i, 0)),
                  pl.BlockSpec((CONV_HALO, 2 * CH), lambda i: (jnp.maximum(i * r - 1, 0), 0)),
                  pl.BlockSpec((CONV_HALO, CH), lambda i: (0, 0)), vec, vec, vec],
        out_specs=[row, row], out_shape=[osh, osh],
        scratch_shapes=[pltpu.VMEM((tm + CONV_HALO, CH), F32)],
        compiler_params=_cp("parallel"))(zc, zc, dw, b, lng, lnb)


def _conf_bwd_ln(dact, u1, lng, lnb, name):
    T = u1.shape[0]
    tm = _tile(T, ROW_TILE)

    def body(da_ref, u_ref, g_ref, lb_ref, du_ref, sm_ref):
        @pl.when(pl.program_id(0) == 0)
        def _():
            sm_ref[...] = jnp.zeros_like(sm_ref)

        u1v = u_ref[...]
        mu = jnp.mean(u1v, axis=-1, keepdims=True)
        cen = u1v - mu
        rstd = lax.rsqrt(jnp.mean(cen * cen, axis=-1, keepdims=True) + EPS)
        y = cen * rstd
        u2 = y * g_ref[...] + lb_ref[...]
        s = _sigmoid(u2)
        du2 = da_ref[...] * (s * (1.0 + u2 * (1.0 - s)))
        dy = du2 * g_ref[...]
        du1 = rstd * (dy - jnp.mean(dy, axis=-1, keepdims=True) - y * jnp.mean(dy * y, axis=-1, keepdims=True))
        du_ref[...] = du1
        sm_ref[pl.ds(0, 1), :] += jnp.sum(du2 * y, axis=0, keepdims=True)
        sm_ref[pl.ds(1, 1), :] += jnp.sum(du2, axis=0, keepdims=True)
        sm_ref[pl.ds(2, 1), :] += jnp.sum(du1, axis=0, keepdims=True)

    vec = pl.BlockSpec((1, CH), lambda i: (0, 0))
    row = pl.BlockSpec((tm, CH), lambda i: (i, 0))
    return pl.pallas_call(
        body, name=name, grid=(T // tm,), in_specs=[row, row, vec, vec],
        out_specs=[row, pl.BlockSpec((8, CH), lambda i: (0, 0))],
        out_shape=[jax.ShapeDtypeStruct((T, CH), F32), jax.ShapeDtypeStruct((8, CH), F32)],
        compiler_params=_cp("arbitrary"))(dact, u1, lng, lnb)


def _conf_bwd_conv(zc, du1, dw, name):
    T = zc.shape[0]
    tm = _tile(T, ROW_TILE)
    r = tm // CONV_HALO
    nt = T // tm
    nh = T // CONV_HALO

    def body(z_ref, zh_ref, d_ref, dn_ref, dw_ref, dz_ref, ddw_ref, ext_u, ext_d):
        i = pl.program_id(0)

        @pl.when(i == 0)
        def _():
            ddw_ref[...] = jnp.zeros_like(ddw_ref)

        cur = z_ref[...]
        ca = cur[:, :CH]
        sg = _sigmoid(cur[:, CH:])
        ext_u[pl.ds(CONV_HALO, tm), :] = ca * sg
        hal = zh_ref[...]
        ext_u[pl.ds(0, CONV_HALO), :] = jnp.where(i > 0, hal[:, :CH] * _sigmoid(hal[:, CH:]), 0.0)
        d = d_ref[...]
        ext_d[pl.ds(0, tm), :] = d
        ext_d[pl.ds(tm, CONV_HALO), :] = jnp.where(i < nt - 1, dn_ref[...], 0.0)
        acc = jnp.zeros((tm, CH), F32)
        for k in range(CONV_K):
            acc = acc + dw_ref[pl.ds(k, 1), :] * ext_d[pl.ds(CONV_K - 1 - k, tm), :]
            ddw_ref[pl.ds(k, 1), :] += jnp.sum(
                d * ext_u[pl.ds(CONV_HALO - (CONV_K - 1) + k, tm), :], axis=0, keepdims=True)
        dz_ref[:, :CH] = (acc * sg).astype(dz_ref.dtype)
        dz_ref[:, CH:] = (acc * ca * sg * (1.0 - sg)).astype(dz_ref.dtype)

    return pl.pallas_call(
        body, name=name, grid=(nt,),
        in_specs=[pl.BlockSpec((tm, 2 * CH), lambda i: (i, 0)),
                  pl.BlockSpec((CONV_HALO, 2 * CH), lambda i: (jnp.maximum(i * r - 1, 0), 0)),
                  pl.BlockSpec((tm, CH), lambda i: (i, 0)),
                  pl.BlockSpec((CONV_HALO, CH), lambda i: (jnp.minimum((i + 1) * r, nh - 1), 0)),
                  pl.BlockSpec((CONV_HALO, CH), lambda i: (0, 0))],
        out_specs=[pl.BlockSpec((tm, 2 * CH), lambda i: (i, 0)), pl.BlockSpec((CONV_HALO, CH), lambda i: (0, 0))],
        out_shape=[jax.ShapeDtypeStruct((T, 2 * CH), _MXU), jax.ShapeDtypeStruct((CONV_HALO, CH), F32)],
        scratch_shapes=[pltpu.VMEM((tm + CONV_HALO, CH), F32), pltpu.VMEM((tm + CONV_HALO, CH), F32)],
        compiler_params=_cp("arbitrary"))(zc, zc, du1, du1, dw)


def _sc_fwd(zs, w, name):
    T = zs.shape[0]
    tm = _tile(T, ROW_TILE)
    r = tm // SC_HALO

    def body(z_ref, zh_ref, w_ref, act_ref, ext):
        i = pl.program_id(0)
        cur = z_ref[...]
        ext[pl.ds(SC_HALO, tm), :] = cur[:, CH:2 * CH] * cur[:, 2 * CH:]
        hal = zh_ref[...]
        ext[pl.ds(0, SC_HALO), :] = jnp.where(i > 0, hal[:, CH:2 * CH] * hal[:, 2 * CH:], 0.0)
        v1 = jnp.zeros((tm, CH), F32)
        for k in range(SC_K):
            v1 = v1 + w_ref[pl.ds(k, 1), :] * ext[pl.ds(SC_HALO - (SC_K - 1) + k, tm), :]
        act_ref[...] = cur[:, :CH] * v1

    return pl.pallas_call(
        body, name=name, grid=(T // tm,),
        in_specs=[pl.BlockSpec((tm, 3 * CH), lambda i: (i, 0)),
                  pl.BlockSpec((SC_HALO, 3 * CH), lambda i: (jnp.maximum(i * r - 1, 0), 0)),
                  pl.BlockSpec((SC_HALO, CH), lambda i: (0, 0))],
        out_specs=pl.BlockSpec((tm, CH), lambda i: (i, 0)),
        out_shape=jax.ShapeDtypeStruct((T, CH), F32),
        scratch_shapes=[pltpu.VMEM((tm + SC_HALO, CH), F32)],
        compiler_params=_cp("parallel"))(zs, zs, w)


def _sc_bwd(zs, dact, w, name):
    T = zs.shape[0]
    tm = _tile(T, ROW_TILE)
    r = tm // SC_HALO
    nt = T // tm
    nh = T // SC_HALO

    def body(z_ref, zh_ref, zn_ref, d_ref, dn_ref, w_ref, dz_ref, dw_ref, ext_v, ext_d):
        i = pl.program_id(0)

        @pl.when(i == 0)
        def _():
            dw_ref[...] = jnp.zeros_like(dw_ref)

        cur = z_ref[...]
        sb, sc, sx = cur[:, :CH], cur[:, CH:2 * CH], cur[:, 2 * CH:]
        ext_v[pl.ds(SC_HALO, tm), :] = sc * sx
        hal = zh_ref[...]
        ext_v[pl.ds(0, SC_HALO), :] = jnp.where(i > 0, hal[:, CH:2 * CH] * hal[:, 2 * CH:], 0.0)
        da = d_ref[...]
        dv1 = da * sb
        ext_d[pl.ds(0, tm), :] = dv1
        ext_d[pl.ds(tm, SC_HALO), :] = jnp.where(i < nt - 1, dn_ref[...] * zn_ref[...][:, :CH], 0.0)
        v1 = jnp.zeros((tm, CH), F32)
        dv0 = jnp.zeros((tm, CH), F32)
        for k in range(SC_K):
            shifted = ext_v[pl.ds(SC_HALO - (SC_K - 1) + k, tm), :]
            v1 = v1 + w_ref[pl.ds(k, 1), :] * shifted
            dv0 = dv0 + w_ref[pl.ds(k, 1), :] * ext_d[pl.ds(SC_K - 1 - k, tm), :]
            dw_ref[pl.ds(k, 1), :] += jnp.sum(dv1 * shifted, axis=0, keepdims=True)
        dz_ref[:, :CH] = (da * v1).astype(dz_ref.dtype)
        dz_ref[:, CH:2 * CH] = (dv0 * sx).astype(dz_ref.dtype)
        dz_ref[:, 2 * CH:] = (dv0 * sc).astype(dz_ref.dtype)

    return pl.pallas_call(
        body, name=name, grid=(nt,),
        in_specs=[pl.BlockSpec((tm, 3 * CH), lambda i: (i, 0)),
                  pl.BlockSpec((SC_HALO, 3 * CH), lambda i: (jnp.maximum(i * r - 1, 0), 0)),
                  pl.BlockSpec((SC_HALO, 3 * CH), lambda i: (jnp.minimum((i + 1) * r, nh - 1), 0)),
                  pl.BlockSpec((tm, CH), lambda i: (i, 0)),
                  pl.BlockSpec((SC_HALO, CH), lambda i: (jnp.minimum((i + 1) * r, nh - 1), 0)),
                  pl.BlockSpec((SC_HALO, CH), lambda i: (0, 0))],
        out_specs=[pl.BlockSpec((tm, 3 * CH), lambda i: (i, 0)), pl.BlockSpec((SC_HALO, CH), lambda i: (0, 0))],
        out_shape=[jax.ShapeDtypeStruct((T, 3 * CH), _MXU), jax.ShapeDtypeStruct((SC_HALO, CH), F32)],
        scratch_shapes=[pltpu.VMEM((tm + SC_HALO, CH), F32), pltpu.VMEM((tm + SC_HALO, CH), F32)],
        compiler_params=_cp("arbitrary"))(zs, zs, zs, dact, dact, w)


SWA_TQ = 256


def _t5_bucket_np(dist):
    max_exact = N_BUCKETS // 2
    d = np.maximum(dist, 1).astype(np.float32)
    large = max_exact + (np.log(d / np.float32(max_exact)) / np.float32(math.log(MAX_DISTANCE / max_exact))
                         * np.float32(N_BUCKETS - max_exact)).astype(np.int32)
    large = np.minimum(large, N_BUCKETS - 1)
    return np.where(dist < max_exact, dist, large).astype(np.int32)


def _swa_bucket_matrix(tq):
    dist = WINDOW + np.arange(tq)[:, None] - np.arange(tq + WINDOW)[None, :]
    ok = (dist >= 0) & (dist < WINDOW)
    return np.where(ok, _t5_bucket_np(np.maximum(dist, 0)), -1).astype(np.int32)


def _kv_expand_matrix():
    e = np.zeros((2 * HEAD, 4 * HEAD), np.float32)
    for h in range(4):
        for d in range(HEAD):
            e[(h // 2) * HEAD + d, h * HEAD + d] = 1.0
    return e


def _swa_bias(rel_bias, bucket, name):
    tq, tk = bucket.shape

    def body(rb_ref, bk_ref, o_ref):
        h = pl.program_id(0)
        bk = bk_ref[...]
        acc = jnp.full((tq, tk), NEG_INF, F32)
        for b in range(N_BUCKETS):
            acc = jnp.where(bk == b, rb_ref[b, h], acc)
        o_ref[0] = acc

    return pl.pallas_call(
        body, name=name, grid=(4,),
        in_specs=[pl.BlockSpec(memory_space=pltpu.SMEM), pl.BlockSpec((tq, tk), lambda h: (0, 0))],
        out_specs=pl.BlockSpec((1, tq, tk), lambda h: (h, 0, 0)),
        out_shape=jax.ShapeDtypeStruct((4, tq, tk), F32), compiler_params=_cp("parallel"))(rel_bias, bucket)


def _swa_probs(qh, kx, bm, first_col, sk):
    s = _dot(qh, kx, _NT) * (HEAD ** -0.5)
    col = lax.broadcasted_iota(jnp.int32, s.shape, 1)
    valid = (bm > 0.5 * NEG_INF) & (col >= first_col)
    s = jnp.where(valid, s + bm, NEG_INF)
    m = jnp.maximum(jnp.max(s, axis=-1, keepdims=True), sk)
    p = jnp.exp(s - m)
    den = jnp.sum(p, axis=-1, keepdims=True) + jnp.exp(sk - m)
    return p / den, m, den


def _swa_fwd(zw, gq, gk, sink, bias, expand, name):
    T = zw.shape[0]
    tq = bias.shape[1]
    r = tq // WINDOW

    def body(z_ref, zh_ref, gq_ref, gk_ref, sink_ref, b_ref, e_ref, o_ref, kext, vext):
        i = pl.program_id(0)
        cur = z_ref[...]
        qn, _ = _head_rms(cur[:, :4 * HEAD], gq_ref[...], 4)
        kc, _ = _head_rms(cur[:, 4 * HEAD:6 * HEAD], gk_ref[...], 2)
        hal = zh_ref[...]
        kp, _ = _head_rms(hal[:, :2 * HEAD], gk_ref[...], 2)
        kext[pl.ds(0, WINDOW), :] = kp
        kext[pl.ds(WINDOW, tq), :] = kc
        vext[pl.ds(0, WINDOW), :] = hal[:, 2 * HEAD:]
        vext[pl.ds(WINDOW, tq), :] = cur[:, 6 * HEAD:]
        kx = _dot(kext[...], e_ref[...]).astype(_MXU)
        vx = _dot(vext[...], e_ref[...]).astype(_MXU)
        first_col = jnp.where(i > 0, 0, WINDOW)
        out = jnp.zeros((tq, 4 * HEAD), F32)
        for h in range(4):
            mk = _lane_mask(4 * HEAD, h)
            qh = jnp.where(mk, qn, 0.0)
            pn, _, _ = _swa_probs(qh, kx, b_ref[h], first_col, sink_ref[0, h])
            out = jnp.where(mk, _dot(pn, vx), out)
        o_ref[...] = out

    return pl.pallas_call(
        body, name=name, grid=(T // tq,),
        in_specs=[pl.BlockSpec((tq, 8 * HEAD), lambda i: (i, 0)),
                  pl.BlockSpec((WINDOW, 4 * HEAD), lambda i: (jnp.maximum(i * r - 1, 0), 1)),
                  pl.BlockSpec((1, 4 * HEAD), lambda i: (0, 0)), pl.BlockSpec((1, 2 * HEAD), lambda i: (0, 0)),
                  pl.BlockSpec(memory_space=pltpu.SMEM),
                  pl.BlockSpec(bias.shape, lambda i: (0, 0, 0)),
                  pl.BlockSpec(expand.shape, lambda i: (0, 0))],
        out_specs=pl.BlockSpec((tq, 4 * HEAD), lambda i: (i, 0)),
        out_shape=jax.ShapeDtypeStruct((T, 4 * HEAD), F32),
        scratch_shapes=[pltpu.VMEM((tq + WINDOW, 2 * HEAD), F32), pltpu.VMEM((tq + WINDOW, 2 * HEAD), F32)],
        compiler_params=_cp("parallel"))(zw, zw, gq, gk, sink, bias, expand)


def _swa_bwd(zw, dact, gq, gk, sink, bias, bucket, expand, name):
    T = zw.shape[0]
    tq = bias.shape[1]
    tk = tq + WINDOW
    r = tq // WINDOW
    nt = T // tq
    nb = T // WINDOW
    scale = HEAD ** -0.5

    def body(z_ref, zh_ref, zn_ref, d_ref, dn_ref, gq_ref, gk_ref, sink_ref, b_ref, bk_ref, e_ref,
             dz_ref, dgq_ref, dgk_ref, dsk_ref, drb_ref, kext, vext, dk_s, dv_s, db_s):
        i = pl.program_id(0)

        @pl.when(i == 0)
        def _():
            dgq_ref[...] = jnp.zeros_like(dgq_ref)
            dgk_ref[...] = jnp.zeros_like(dgk_ref)
            dsk_ref[...] = jnp.zeros_like(dsk_ref)
            drb_ref[...] = jnp.zeros_like(drb_ref)
            db_s[...] = jnp.zeros_like(db_s)

        lane = lax.broadcasted_iota(jnp.int32, (1, LANE), 1)
        cur = z_ref[...]
        q_raw, k_raw = cur[:, :4 * HEAD], cur[:, 4 * HEAD:6 * HEAD]
        qn, q_r = _head_rms(q_raw, gq_ref[...], 4)
        kc, k_r = _head_rms(k_raw, gk_ref[...], 2)
        hal = zh_ref[...]
        kp, _ = _head_rms(hal[:, :2 * HEAD], gk_ref[...], 2)
        kext[pl.ds(0, WINDOW), :] = kp
        kext[pl.ds(WINDOW, tq), :] = kc
        vext[pl.ds(0, WINDOW), :] = hal[:, 2 * HEAD:]
        vext[pl.ds(WINDOW, tq), :] = cur[:, 6 * HEAD:]
        ev = e_ref[...]
        kx = _dot(kext[...], ev).astype(_MXU)
        vx = _dot(vext[...], ev).astype(_MXU)
        first_col = jnp.where(i > 0, 0, WINDOW)
        do = d_ref[...]
        dq = jnp.zeros((tq, 4 * HEAD), F32)
        dkx = jnp.zeros((tk, 4 * HEAD), F32)
        dvx = jnp.zeros((tk, 4 * HEAD), F32)
        dsk = jnp.zeros((1, LANE), F32)
        for h in range(4):
            mk = _lane_mask(4 * HEAD, h)
            qh = jnp.where(mk, qn, 0.0).astype(_MXU)
            sk = sink_ref[0, h]
            pn, m, den = _swa_probs(qh, kx, b_ref[h], first_col, sk)
            doh = jnp.where(mk, do, 0.0).astype(_MXU)
            dpn = _dot(doh, vx, _NT)
            delta = jnp.sum(pn * dpn, axis=-1, keepdims=True)
            ds = pn * (dpn - delta)
            psink = jnp.exp(sk - m) / den
            dsk = dsk + jnp.where(lane == h, jnp.sum(-psink * delta, axis=0, keepdims=True), 0.0)
            db_s[h] += ds
            dss = (ds * scale).astype(_MXU)
            dq = dq + jnp.where(mk, _dot(dss, kx), 0.0)
            dkx = dkx + _dot(dss, qh, _TN)
            dvx = dvx + _dot(pn, doh, _TN)
        dsk_ref[...] += dsk
        dk_ext = _exact_dot(dkx, ev, _NT, "a")
        dv_ext = _exact_dot(dvx, ev, _NT, "a")
        dk_s[...] = dk_ext[WINDOW:, :]
        dv_s[...] = dv_ext[WINDOW:, :]

        @pl.when(i < nt - 1)
        def _():
            nxt = zn_ref[...]
            q2, _ = _head_rms(nxt[:, :4 * HEAD], gq_ref[...], 4)
            k2n, _ = _head_rms(nxt[:, 4 * HEAD:6 * HEAD], gk_ref[...], 2)
            k2 = jnp.concatenate([kc[tq - WINDOW:, :], k2n], axis=0)
            v2 = jnp.concatenate([cur[tq - WINDOW:, 6 * HEAD:], nxt[:, 6 * HEAD:]], axis=0)
            k2x = _dot(k2, ev).astype(_MXU)
            v2x = _dot(v2, ev).astype(_MXU)
            do2 = dn_ref[...]
            dk2x = jnp.zeros((2 * WINDOW, 4 * HEAD), F32)
            dv2x = jnp.zeros((2 * WINDOW, 4 * HEAD), F32)
            for h in range(4):
                mk = _lane_mask(4 * HEAD, h)
                qh = jnp.where(mk, q2, 0.0).astype(_MXU)
                pn, _, _ = _swa_probs(qh, k2x, b_ref[h][:WINDOW, :2 * WINDOW], 0, sink_ref[0, h])
                doh = jnp.where(mk, do2, 0.0).astype(_MXU)
                dpn = _dot(doh, v2x, _NT)
                ds = pn * (dpn - jnp.sum(pn * dpn, axis=-1, keepdims=True))
                dk2x = dk2x + _dot((ds * scale).astype(_MXU), qh, _TN)
                dv2x = dv2x + _dot(pn, doh, _TN)
            dk_s[pl.ds(tq - WINDOW, WINDOW), :] += _exact_dot(dk2x, ev, _NT, "a")[:WINDOW, :]
            dv_s[pl.ds(tq - WINDOW, WINDOW), :] += _exact_dot(dv2x, ev, _NT, "a")[:WINDOW, :]

        dq_raw, dgq = _head_rms_bwd(dq, q_raw, q_r, gq_ref[...], 4)
        dk_raw, dgk = _head_rms_bwd(dk_s[...], k_raw, k_r, gk_ref[...], 2)
        dgq_ref[...] += dgq
        dgk_ref[...] += dgk
        dz_ref[:, :4 * HEAD] = dq_raw.astype(dz_ref.dtype)
        dz_ref[:, 4 * HEAD:6 * HEAD] = dk_raw.astype(dz_ref.dtype)
        dz_ref[:, 6 * HEAD:] = dv_s[...].astype(dz_ref.dtype)

        @pl.when(i == nt - 1)
        def _():
            bk = bk_ref[...]
            for b in range(N_BUCKETS):
                rowv = jnp.zeros((1, LANE), F32)
                for h in range(4):
                    s1 = jnp.sum(jnp.where(bk == b, db_s[h], 0.0), axis=0, keepdims=True)
                    rowv = jnp.where(lane == h, jnp.sum(s1, axis=1, keepdims=True), rowv)
                drb_ref[pl.ds(b, 1), :] = rowv

    const2 = lambda i: (0, 0)
    return pl.pallas_call(
        body, name=name, grid=(nt,),
        in_specs=[pl.BlockSpec((tq, 8 * HEAD), lambda i: (i, 0)),
                  pl.BlockSpec((WINDOW, 4 * HEAD), lambda i: (jnp.maximum(i * r - 1, 0), 1)),
                  pl.BlockSpec((WINDOW, 8 * HEAD), lambda i: (jnp.minimum((i + 1) * r, nb - 1), 0)),
                  pl.BlockSpec((tq, 4 * HEAD), lambda i: (i, 0)),
                  pl.BlockSpec((WINDOW, 4 * HEAD), lambda i: (jnp.minimum((i + 1) * r, nb - 1), 0)),
                  pl.BlockSpec((1, 4 * HEAD), const2), pl.BlockSpec((1, 2 * HEAD), const2),
                  pl.BlockSpec(memory_space=pltpu.SMEM),
                  pl.BlockSpec(bias.shape, lambda i: (0, 0, 0)),
                  pl.BlockSpec(bucket.shape, const2), pl.BlockSpec(expand.shape, const2)],
        out_specs=[pl.BlockSpec((tq, 8 * HEAD), lambda i: (i, 0)),
                   pl.BlockSpec((1, 4 * HEAD), const2), pl.BlockSpec((1, 2 * HEAD), const2),
                   pl.BlockSpec((1, LANE), const2), pl.BlockSpec((N_BUCKETS, LANE), const2)],
        out_shape=[jax.ShapeDtypeStruct((T, 8 * HEAD), _MXU), jax.ShapeDtypeStruct((1, 4 * HEAD), F32),
                   jax.ShapeDtypeStruct((1, 2 * HEAD), F32), jax.ShapeDtypeStruct((1, LANE), F32),
                   jax.ShapeDtypeStruct((N_BUCKETS, LANE), F32)],
        scratch_shapes=[pltpu.VMEM((tk, 2 * HEAD), F32), pltpu.VMEM((tk, 2 * HEAD), F32),
                        pltpu.VMEM((tq, 2 * HEAD), F32), pltpu.VMEM((tq, 2 * HEAD), F32),
                        pltpu.VMEM((4, tq, tk), F32)],
        compiler_params=_cp("arbitrary"))(zw, zw, zw, dact, dact, gq, gk, sink, bias, bucket, expand)


FOX_B = 512
FOX_TM = 256


def _tri(n, lower):
    m = np.tril(np.ones((n, n), np.float32)) if lower else np.triu(np.ones((n, n), np.float32))
    return m


def _log_sigmoid(x):
    return jnp.minimum(x, 0.0) - jnp.log1p(jnp.exp(-jnp.abs(x)))


def _fox_prep(zf, gq, gk, bf, name):
    T = zf.shape[0]
    tm = _tile(T, FOX_TM)
    lower = jnp.asarray(_tri(tm, True), _MXU)

    def body(z_ref, gq_ref, gk_ref, bf_ref, l_ref, q_ref, k_ref, v_ref, f_ref, ft_ref, carry):
        @pl.when(pl.program_id(0) == 0)
        def _():
            carry[...] = jnp.zeros_like(carry)

        z = z_ref[...]
        q, _ = _head_rms(z[:, :CH], gq_ref[...], 4)
        k, _ = _head_rms(z[:, CH:2 * CH], gk_ref[...], 4)
        q_ref[...] = q.astype(q_ref.dtype)
        k_ref[...] = k.astype(k_ref.dtype)
        v_ref[...] = z[:, 2 * CH:3 * CH].astype(v_ref.dtype)
        lane = lax.broadcasted_iota(jnp.int32, (1, LANE), 1)
        lf = jnp.where(lane < 4, _log_sigmoid(z[:, 3 * CH:] + bf_ref[...]), 0.0)
        fv = _exact_dot(l_ref[...], lf, _NN, "b") + carry[pl.ds(0, 1), :]
        f_ref[...] = fv
        ft_ref[...] = fv.T
        carry[pl.ds(0, 1), :] = f_ref[pl.ds(tm - 1, 1), :]

    row = pl.BlockSpec((tm, CH), lambda i: (i, 0))
    vec = pl.BlockSpec((1, CH), lambda i: (0, 0))
    qsh = jax.ShapeDtypeStruct((T, CH), _MXU)
    return pl.pallas_call(
        body, name=name, grid=(T // tm,),
        in_specs=[pl.BlockSpec((tm, 3 * CH + LANE), lambda i: (i, 0)), vec, vec,
                  pl.BlockSpec((1, LANE), lambda i: (0, 0)), pl.BlockSpec((tm, tm), lambda i: (0, 0))],
        out_specs=[row, row, row, pl.BlockSpec((tm, LANE), lambda i: (i, 0)), pl.BlockSpec((LANE, tm), lambda i: (0, i))],
        out_shape=[qsh, qsh, qsh, jax.ShapeDtypeStruct((T, LANE), F32), jax.ShapeDtypeStruct((LANE, T), F32)],
        scratch_shapes=[pltpu.VMEM((8, LANE), F32)],
        compiler_params=_cp("arbitrary"))(zf, gq, gk, bf, lower)


def _lane_col(x, h):
    lane = lax.broadcasted_iota(jnp.int32, (1, x.shape[-1]), 1)
    return jnp.sum(jnp.where(lane == h, x, 0.0), axis=-1, keepdims=True)


def _fox_scores(qh, k, fq, ft_ref, h, qi, ki, B):
    s = _dot(qh, k, _NT) * (HEAD ** -0.5)
    s = s + (fq - ft_ref[pl.ds(h, 1), :])
    row = qi * B + lax.broadcasted_iota(jnp.int32, s.shape, 0)
    col = ki * B + lax.broadcasted_iota(jnp.int32, s.shape, 1)
    return jnp.where(col <= row, s, NEG_INF)


def _fox_fwd(q, k, v, f, ft, name):
    T = q.shape[0]
    B = _tile(T, FOX_B)
    n = T // B

    def body(q_ref, k_ref, v_ref, f_ref, ft_ref, o_ref, lse_ref, m_s, l_s, acc):
        qi, ki = pl.program_id(0), pl.program_id(1)

        @pl.when(ki == 0)
        def _():
            m_s[...] = jnp.full_like(m_s, NEG_INF)
            l_s[...] = jnp.zeros_like(l_s)
            acc[...] = jnp.zeros_like(acc)

        @pl.when(ki <= qi)
        def _():
            qv, kv, vv, fv = q_ref[...], k_ref[...], v_ref[...], f_ref[...]
            for h in range(4):
                mk = _lane_mask(CH, h)
                qh = jnp.where(mk, qv, jnp.zeros_like(qv))
                s = _fox_scores(qh, kv, _lane_col(fv, h), ft_ref, h, qi, ki, B)
                m_old = m_s[h]
                m_new = jnp.maximum(m_old, jnp.max(s, axis=-1, keepdims=True))
                alpha = jnp.exp(m_old - m_new)
                p = jnp.exp(s - m_new)
                l_s[h] = alpha * l_s[h] + jnp.sum(p, axis=-1, keepdims=True)
                m_s[h] = m_new
                acc[...] = jnp.where(mk, acc[...] * alpha + _dot(p, vv), acc[...])

        @pl.when(ki == qi)
        def _():
            lane = lax.broadcasted_iota(jnp.int32, (1, LANE), 1)
            out = acc[...]
            lse = jnp.zeros((B, LANE), F32)
            for h in range(4):
                out = jnp.where(_lane_mask(CH, h), out / l_s[h], out)
                lse = jnp.where(lane == h, m_s[h] + jnp.log(l_s[h]), lse)
            o_ref[...] = out
            lse_ref[...] = lse

    qspec = pl.BlockSpec((B, CH), lambda qi, ki: (qi, 0))
    kspec = pl.BlockSpec((B, CH), lambda qi, ki: (jnp.minimum(ki, qi), 0))
    return pl.pallas_call(
        body, name=name, grid=(n, n),
        in_specs=[qspec, kspec, kspec, pl.BlockSpec((B, LANE), lambda qi, ki: (qi, 0)),
                  pl.BlockSpec((8, B), lambda qi, ki: (0, jnp.minimum(ki, qi)))],
        out_specs=[qspec, pl.BlockSpec((B, LANE), lambda qi, ki: (qi, 0))],
        out_shape=[jax.ShapeDtypeStruct((T, CH), F32), jax.ShapeDtypeStruct((T, LANE), F32)],
        scratch_shapes=[pltpu.VMEM((4, B, 1), F32), pltpu.VMEM((4, B, 1), F32), pltpu.VMEM((B, CH), F32)],
        compiler_params=_cp("parallel", "arbitrary"))(q, k, v, f, ft)


def _fox_delta(o, do, name):
    T = o.shape[0]
    tm = _tile(T, ROW_TILE)

    def body(o_ref, d_ref, out_ref):
        prod = o_ref[...] * d_ref[...]
        lane = lax.broadcasted_iota(jnp.int32, (1, LANE), 1)
        out = jnp.zeros((tm, LANE), F32)
        for h in range(4):
            s = jnp.sum(jnp.where(_lane_mask(CH, h), prod, 0.0), axis=-1, keepdims=True)
            out = jnp.where(lane == h, s, out)
        out_ref[...] = out

    row = pl.BlockSpec((tm, CH), lambda i: (i, 0))
    return pl.pallas_call(
        body, name=name, grid=(T // tm,), in_specs=[row, row],
        out_specs=pl.BlockSpec((tm, LANE), lambda i: (i, 0)),
        out_shape=jax.ShapeDtypeStruct((T, LANE), F32), compiler_params=_cp("parallel"))(o, do)


def _fox_bwd_dq(q, k, v, f, ft, lse, delta, do, name):
    T = q.shape[0]
    B = _tile(T, FOX_B)
    n = T // B

    def body(q_ref, k_ref, v_ref, f_ref, ft_ref, lse_ref, dl_ref, do_ref, dq_ref, dfq_ref, dq_s, df_s):
        qi, ki = pl.program_id(0), pl.program_id(1)

        @pl.when(ki == 0)
        def _():
            dq_s[...] = jnp.zeros_like(dq_s)
            df_s[...] = jnp.zeros_like(df_s)

        @pl.when(ki <= qi)
        def _():
            qv, kv, vv, fv = q_ref[...], k_ref[...], v_ref[...], f_ref[...]
            lsev, dlv, dov = lse_ref[...], dl_ref[...], do_ref[...]
            lane = lax.broadcasted_iota(jnp.int32, (1, LANE), 1)
            for h in range(4):
                mk = _lane_mask(CH, h)
                qh = jnp.where(mk, qv, jnp.zeros_like(qv))
                s = _fox_scores(qh, kv, _lane_col(fv, h), ft_ref, h, qi, ki, B)
                p = jnp.exp(s - _lane_col(lsev, h))
                doh = jnp.where(mk, dov, 0.0)
                ds = p * (_dot(doh, vv, _NT) - _lane_col(dlv, h))
                dq_s[...] += jnp.where(mk, _dot(ds * (HEAD ** -0.5), kv), 0.0)
                df_s[...] += jnp.where(lane == h, jnp.sum(ds, axis=-1, keepdims=True), 0.0)

        @pl.when(ki == qi)
        def _():
            dq_ref[...] = dq_s[...]
            dfq_ref[...] = df_s[...]

    qspec = pl.BlockSpec((B, CH), lambda qi, ki: (qi, 0))
    kspec = pl.BlockSpec((B, CH), lambda qi, ki: (jnp.minimum(ki, qi), 0))
    lspec = pl.BlockSpec((B, LANE), lambda qi, ki: (qi, 0))
    return pl.pallas_call(
        body, name=name, grid=(n, n),
        in_specs=[qspec, kspec, kspec, lspec, pl.BlockSpec((8, B), lambda qi, ki: (0, jnp.minimum(ki, qi))),
                  lspec, lspec, qspec],
        out_specs=[qspec, lspec],
        out_shape=[jax.ShapeDtypeStruct((T, CH), F32), jax.ShapeDtypeStruct((T, LANE), F32)],
        scratch_shapes=[pltpu.VMEM((B, CH), F32), pltpu.VMEM((B, LANE), F32)],
        compiler_params=_cp("parallel", "arbitrary"))(q, k, v, f, ft, lse, delta, do)


def _fox_bwd_dkv(q, k, v, f, ft, lse, delta, do, name):
    T = q.shape[0]
    B = _tile(T, FOX_B)
    n = T // B

    def body(q_ref, k_ref, v_ref, f_ref, ft_ref, lse_ref, dl_ref, do_ref, dk_ref, dv_ref, dft_ref, dk_s, dv_s, df_s):
        ki, qi = pl.program_id(0), pl.program_id(1)

        @pl.when(qi == 0)
        def _():
            dk_s[...] = jnp.zeros_like(dk_s)
            dv_s[...] = jnp.zeros_like(dv_s)
            df_s[...] = jnp.zeros_like(df_s)

        @pl.when(qi >= ki)
        def _():
            qv, kv, vv, fv = q_ref[...], k_ref[...], v_ref[...], f_ref[...]
            lsev, dlv, dov = lse_ref[...], dl_ref[...], do_ref[...]
            for h in range(4):
                mk = _lane_mask(CH, h)
                qh = jnp.where(mk, qv, jnp.zeros_like(qv))
                s = _fox_scores(qh, kv, _lane_col(fv, h), ft_ref, h, qi, ki, B)
                p = jnp.exp(s - _lane_col(lsev, h))
                doh = jnp.where(mk, dov, 0.0)
                ds = p * (_dot(doh, vv, _NT) - _lane_col(dlv, h))
                dv_s[...] += _dot(p, doh, _TN)
                dk_s[...] += _dot(ds * (HEAD ** -0.5), qh, _TN)
                df_s[pl.ds(h, 1), :] -= jnp.sum(ds, axis=0, keepdims=True)

        @pl.when(qi == n - 1)
        def _():
            dk_ref[...] = dk_s[...]
            dv_ref[...] = dv_s[...]
            dft_ref[...] = jnp.zeros_like(dft_ref)
            dft_ref[pl.ds(0, 8), :] = df_s[...]

    qspec = pl.BlockSpec((B, CH), lambda ki, qi: (jnp.maximum(qi, ki), 0))
    kspec = pl.BlockSpec((B, CH), lambda ki, qi: (ki, 0))
    lspec = pl.BlockSpec((B, LANE), lambda ki, qi: (jnp.maximum(qi, ki), 0))
    return pl.pallas_call(
        body, name=name, grid=(n, n),
        in_specs=[qspec, kspec, kspec, lspec, pl.BlockSpec((8, B), lambda ki, qi: (0, ki)), lspec, lspec, qspec],
        out_specs=[kspec, kspec, pl.BlockSpec((LANE, B), lambda ki, qi: (0, ki))],
        out_shape=[jax.ShapeDtypeStruct((T, CH), F32), jax.ShapeDtypeStruct((T, CH), F32),
                   jax.ShapeDtypeStruct((LANE, T), F32)],
        scratch_shapes=[pltpu.VMEM((B, CH), F32), pltpu.VMEM((B, CH), F32), pltpu.VMEM((8, B), F32)],
        compiler_params=_cp("parallel", "arbitrary"))(q, k, v, f, ft, lse, delta, do)


def _fox_post(zf, dqn, dkn, dv, dfq, dft, gq, gk, bf, name):
    T = zf.shape[0]
    tm = _tile(T, FOX_TM)
    nt = T // tm
    upper = jnp.asarray(_tri(tm, False), _MXU)

    def body(z_ref, dq_ref, dk_ref, dv_ref, dfq_ref, dft_ref, gq_ref, gk_ref, bf_ref, u_ref, dz_ref, sm_ref, carry, rc_s):
        @pl.when(pl.program_id(0) == 0)
        def _():
            carry[...] = jnp.zeros_like(carry)
            sm_ref[...] = jnp.zeros_like(sm_ref)

        z = z_ref[...]
        q_raw, k_raw = z[:, :CH], z[:, CH:2 * CH]
        _, q_r = _head_rms(q_raw, gq_ref[...], 4)
        _, k_r = _head_rms(k_raw, gk_ref[...], 4)
        dq, dgq = _head_rms_bwd(dq_ref[...], q_raw, q_r, gq_ref[...], 4)
        dk, dgk = _head_rms_bwd(dk_ref[...], k_raw, k_r, gk_ref[...], 4)
        df = dfq_ref[...] + dft_ref[...].T
        rc_s[...] = _exact_dot(u_ref[...], df, _NN, "b") + carry[pl.ds(0, 1), :]
        carry[pl.ds(0, 1), :] = rc_s[pl.ds(0, 1), :]
        lane = lax.broadcasted_iota(jnp.int32, (1, LANE), 1)
        x = z[:, 3 * CH:] + bf_ref[...]
        dff = jnp.where(lane < 4, rc_s[...] * _sigmoid(-x), 0.0)
        dz_ref[:, :CH] = dq.astype(dz_ref.dtype)
        dz_ref[:, CH:2 * CH] = dk.astype(dz_ref.dtype)
        dz_ref[:, 2 * CH:3 * CH] = dv_ref[...].astype(dz_ref.dtype)
        dz_ref[:, 3 * CH:] = dff.astype(dz_ref.dtype)
        sm_ref[pl.ds(0, 1), :] += dgq
        sm_ref[pl.ds(1, 1), :] += dgk
        sm_ref[pl.ds(2, 1), :LANE] += jnp.sum(dff, axis=0, keepdims=True)

    rev = lambda i: (nt - 1 - i, 0)
    row = pl.BlockSpec((tm, CH), rev)
    lrow = pl.BlockSpec((tm, LANE), rev)
    vec = pl.BlockSpec((1, CH), lambda i: (0, 0))
    return pl.pallas_call(
        body, name=name, grid=(nt,),
        in_specs=[pl.BlockSpec((tm, 3 * CH + LANE), rev), row, row, row, lrow,
                  pl.BlockSpec((LANE, tm), lambda i: (0, nt - 1 - i)), vec, vec,
                  pl.BlockSpec((1, LANE), lambda i: (0, 0)), pl.BlockSpec((tm, tm), lambda i: (0, 0))],
        out_specs=[pl.BlockSpec((tm, 3 * CH + LANE), rev), pl.BlockSpec((8, CH), lambda i: (0, 0))],
        out_shape=[jax.ShapeDtypeStruct((T, 3 * CH + LANE), _MXU), jax.ShapeDtypeStruct((8, CH), F32)],
        scratch_shapes=[pltpu.VMEM((8, LANE), F32), pltpu.VMEM((tm, LANE), F32)],
        compiler_params=_cp("arbitrary"))(zf, dqn, dkn, dv, dfq, dft, gq, gk, bf, upper)


AUG_F, AUG_ONE, AUG_LSE = HEAD, HEAD + 3, HEAD + 6


def _pieces(x):
    hi = x.astype(_MXU).astype(F32)
    r1 = x - hi
    mid = r1.astype(_MXU).astype(F32)
    lo = (r1 - mid).astype(_MXU).astype(F32)
    return hi, mid, lo


def _put_pieces(base, first_lane, x, sign):
    lane = lax.broadcasted_iota(jnp.int32, (1, LANE), 1)
    for j, piece in enumerate(_pieces(x)):
        base = jnp.where(lane == first_lane + j, sign * piece, base)
    return base


def _head_select_matrix():
    p = np.zeros((4, 4 * HEAD, LANE), np.float32)
    for h in range(4):
        for d in range(HEAD):
            p[h, h * HEAD + d, d] = 1.0
    return p


def _tri_steps(n, by_key):
    if by_key:
        pairs = [(q, k) for k in range(n) for q in range(k, n)]
    else:
        pairs = [(q, k) for q in range(n) for k in range(q + 1)]
    return (jnp.asarray([p[0] for p in pairs], jnp.int32), jnp.asarray([p[1] for p in pairs], jnp.int32))


def _fox2_prep(zf, gq, gk, bf, sel, name):
    T = zf.shape[0]
    tm = _tile(T, FOX_TM)
    lower = jnp.asarray(_tri(tm, True), _MXU)

    def body(z_ref, gq_ref, gk_ref, bf_ref, l_ref, p_ref, qa_ref, ka_ref, va_ref, carry, f_s):
        @pl.when(pl.program_id(0) == 0)
        def _():
            carry[...] = jnp.zeros_like(carry)

        z = z_ref[...]
        q, _ = _head_rms(z[:, :CH], gq_ref[...], 4)
        k, _ = _head_rms(z[:, CH:2 * CH], gk_ref[...], 4)
        q = (q * (HEAD ** -0.5)).astype(_MXU)
        k = k.astype(_MXU)
        v = z[:, 2 * CH:3 * CH].astype(_MXU)
        lane = lax.broadcasted_iota(jnp.int32, (1, LANE), 1)
        lf = jnp.where(lane < 4, _log_sigmoid(z[:, 3 * CH:] + bf_ref[...]), 0.0)
        f_s[...] = _exact_dot(l_ref[...], lf, _NN, "b") + carry[pl.ds(0, 1), :]
        carry[pl.ds(0, 1), :] = f_s[pl.ds(tm - 1, 1), :]
        fv = f_s[...]
        q_ones = (lane >= AUG_ONE) & (lane < AUG_ONE + 3)
        k_ones = ((lane >= AUG_F) & (lane < AUG_F + 3)) | ((lane >= AUG_LSE) & (lane < AUG_LSE + 3))
        v_ones = (lane >= AUG_F) & (lane < AUG_F + 3)
        for h in range(4):
            fh = _lane_col(fv, h)
            qa = jnp.where(q_ones, 1.0, _dot(q, p_ref[h]))
            qa_ref[h] = _put_pieces(qa, AUG_F, fh, 1.0).astype(qa_ref.dtype)
            ka = jnp.where(k_ones, 1.0, _dot(k, p_ref[h]))
            ka_ref[h] = _put_pieces(ka, AUG_ONE, fh, -1.0).astype(ka_ref.dtype)
            va_ref[h] = jnp.where(v_ones, 1.0, _dot(v, p_ref[h])).astype(va_ref.dtype)

    vec = pl.BlockSpec((1, CH), lambda i: (0, 0))
    hspec = pl.BlockSpec((4, tm, LANE), lambda i: (0, i, 0))
    hsh = jax.ShapeDtypeStruct((4, T, LANE), _MXU)
    return pl.pallas_call(
        body, name=name, grid=(T // tm,),
        in_specs=[pl.BlockSpec((tm, 3 * CH + LANE), lambda i: (i, 0)), vec, vec,
                  pl.BlockSpec((1, LANE), lambda i: (0, 0)), pl.BlockSpec((tm, tm), lambda i: (0, 0)),
                  pl.BlockSpec(sel.shape, lambda i: (0, 0, 0))],
        out_specs=[hspec, hspec, hspec], out_shape=[hsh, hsh, hsh],
        scratch_shapes=[pltpu.VMEM((8, LANE), F32), pltpu.VMEM((tm, LANE), F32)],
        compiler_params=_cp("arbitrary"))(zf, gq, gk, bf, lower, sel)


def _causal(s, transposed):
    row = lax.broadcasted_iota(jnp.int32, s.shape, 0)
    col = lax.broadcasted_iota(jnp.int32, s.shape, 1)
    return jnp.where((row <= col) if transposed else (col <= row), s, NEG_INF)


def _mxu_dot(a, b, dims):
    return lax.dot_general(a, b, dims, preferred_element_type=F32)


def _fox2_fwd(qa, ka, va, sel, name):
    T = qa.shape[1]
    B = _tile(T, FOX_B)
    n = T // B
    qt, kt = _tri_steps(n, False)

    def body(qt_ref, kt_ref, qa_ref, ka_ref, va_ref, p_ref, o_ref, qb_ref, m_s, acc):
        step = pl.program_id(0)
        qi, ki = qt_ref[step], kt_ref[step]

        @pl.when(ki == 0)
        def _():
            m_s[...] = jnp.full_like(m_s, NEG_INF)
            acc[...] = jnp.zeros_like(acc)

        def update(diag):
            for h in range(4):
                s = _mxu_dot(qa_ref[h], ka_ref[h], _NT)
                if diag:
                    s = _causal(s, False)
                m_old = m_s[h]
                m_new = jnp.maximum(m_old, jnp.max(s, axis=-1, keepdims=True))
                p = jnp.exp(s - m_new)
                acc[h] = acc[h] * jnp.exp(m_old - m_new) + _dot(p, va_ref[h])
                m_s[h] = m_new

        @pl.when(ki < qi)
        def _():
            update(False)

        @pl.when(ki == qi)
        def _():
            update(True)
            out = jnp.zeros((B, CH), F32)
            for h in range(4):
                a = acc[h]
                l = _lane_col(a, AUG_F)
                out = out + _exact_dot(a / l, p_ref[h], _NT, "a")
                lse = m_s[h] + jnp.log(l)
                qb_ref[h] = _put_pieces(qa_ref[h].astype(F32), AUG_LSE, lse, -1.0).astype(qb_ref.dtype)
            o_ref[...] = out

    qspec = pl.BlockSpec((4, B, LANE), lambda s, qt, kt: (0, qt[s], 0))
    kspec = pl.BlockSpec((4, B, LANE), lambda s, qt, kt: (0, kt[s], 0))
    grid_spec = pltpu.PrefetchScalarGridSpec(
        num_scalar_prefetch=2, grid=(qt.shape[0],),
        in_specs=[qspec, kspec, kspec, pl.BlockSpec(sel.shape, lambda s, qt, kt: (0, 0, 0))],
        out_specs=[pl.BlockSpec((B, CH), lambda s, qt, kt: (qt[s], 0)), qspec],
        scratch_shapes=[pltpu.VMEM((4, B, 1), F32), pltpu.VMEM((4, B, LANE), F32)])
    return pl.pallas_call(
        body, name=name, grid_spec=grid_spec,
        out_shape=[jax.ShapeDtypeStruct((T, CH), F32), jax.ShapeDtypeStruct((4, T, LANE), _MXU)],
        compiler_params=_cp("arbitrary"))(qt, kt, qa, ka, va, sel)


def _fox2_bwd_prep(o, do, sel, name):
    T = o.shape[0]
    tm = _tile(T, ROW_TILE)

    def body(o_ref, d_ref, p_ref, out_ref):
        dov = d_ref[...]
        prod = o_ref[...] * dov
        dob = dov.astype(_MXU)
        for h in range(4):
            delta = jnp.sum(jnp.where(_lane_mask(CH, h), prod, 0.0), axis=-1, keepdims=True)
            out_ref[h] = _put_pieces(_dot(dob, p_ref[h]), AUG_F, delta, -1.0).astype(out_ref.dtype)

    row = pl.BlockSpec((tm, CH), lambda i: (i, 0))
    return pl.pallas_call(
        body, name=name, grid=(T // tm,),
        in_specs=[row, row, pl.BlockSpec(sel.shape, lambda i: (0, 0, 0))],
        out_specs=pl.BlockSpec((4, tm, LANE), lambda i: (0, i, 0)),
        out_shape=jax.ShapeDtypeStruct((4, T, LANE), _MXU), compiler_params=_cp("parallel"))(o, do, sel)


def _fox2_bwd_dq(qb, ka, va, doa, sel, name):
    T = qb.shape[1]
    B = _tile(T, FOX_B)
    n = T // B
    qt, kt = _tri_steps(n, False)

    def body(qt_ref, kt_ref, qb_ref, ka_ref, va_ref, do_ref, p_ref, dq_ref, dfq_ref, dq_s):
        step = pl.program_id(0)
        qi, ki = qt_ref[step], kt_ref[step]

        @pl.when(ki == 0)
        def _():
            dq_s[...] = jnp.zeros_like(dq_s)

        def update(diag):
            for h in range(4):
                s = _mxu_dot(qb_ref[h], ka_ref[h], _NT)
                if diag:
                    s = _causal(s, False)
                ds = jnp.exp(s) * _mxu_dot(do_ref[h], va_ref[h], _NT)
                dq_s[h] += _dot(ds, ka_ref[h])

        @pl.when(ki < qi)
        def _():
            update(False)

        @pl.when(ki == qi)
        def _():
            update(True)
            lane = lax.broadcasted_iota(jnp.int32, (1, LANE), 1)
            out = jnp.zeros((B, CH), F32)
            dfq = jnp.zeros((B, LANE), F32)
            for h in range(4):
                out = out + _exact_dot(dq_s[h] * (HEAD ** -0.5), p_ref[h], _NT, "a")
                dfq = jnp.where(lane == h, _lane_col(dq_s[h], AUG_F), dfq)
            dq_ref[...] = out
            dfq_ref[...] = dfq

    qspec = pl.BlockSpec((4, B, LANE), lambda s, qt, kt: (0, qt[s], 0))
    kspec = pl.BlockSpec((4, B, LANE), lambda s, qt, kt: (0, kt[s], 0))
    grid_spec = pltpu.PrefetchScalarGridSpec(
        num_scalar_prefetch=2, grid=(qt.shape[0],),
        in_specs=[qspec, kspec, kspec, qspec, pl.BlockSpec(sel.shape, lambda s, qt, kt: (0, 0, 0))],
        out_specs=[pl.BlockSpec((B, CH), lambda s, qt, kt: (qt[s], 0)),
                   pl.BlockSpec((B, LANE), lambda s, qt, kt: (qt[s], 0))],
        scratch_shapes=[pltpu.VMEM((4, B, LANE), F32)])
    return pl.pallas_call(
        body, name=name, grid_spec=grid_spec,
        out_shape=[jax.ShapeDtypeStruct((T, CH), F32), jax.ShapeDtypeStruct((T, LANE), F32)],
        compiler_params=_cp("arbitrary"))(qt, kt, qb, ka, va, doa, sel)


def _fox2_bwd_dkv(qb, ka, va, doa, sel, name):
    T = qb.shape[1]
    B = _tile(T, FOX_B)
    n = T // B
    qt, kt = _tri_steps(n, True)

    def body(qt_ref, kt_ref, qb_ref, ka_ref, va_ref, do_ref, p_ref, dk_ref, dv_ref, df_ref, dk_s, dv_s):
        step = pl.program_id(0)
        qi, ki = qt_ref[step], kt_ref[step]

        @pl.when(qi == ki)
        def _():
            dk_s[...] = jnp.zeros_like(dk_s)
            dv_s[...] = jnp.zeros_like(dv_s)

        def update(diag):
            for h in range(4):
                st = _mxu_dot(ka_ref[h], qb_ref[h], _NT)
                if diag:
                    st = _causal(st, True)
                pt = jnp.exp(st)
                dst = pt * _mxu_dot(va_ref[h], do_ref[h], _NT)
                dv_s[h] += _dot(pt, do_ref[h])
                dk_s[h] += _dot(dst, qb_ref[h])

        @pl.when(qi == ki)
        def _():
            update(True)

        @pl.when(qi > ki)
        def _():
            update(False)

        @pl.when(qi == n - 1)
        def _():
            lane = lax.broadcasted_iota(jnp.int32, (1, LANE), 1)
            dk = jnp.zeros((B, CH), F32)
            dv = jnp.zeros((B, CH), F32)
            dfk = jnp.zeros((B, LANE), F32)
            for h in range(4):
                dk = dk + _exact_dot(dk_s[h], p_ref[h], _NT, "a")
                dv = dv + _exact_dot(dv_s[h], p_ref[h], _NT, "a")
                dfk = jnp.where(lane == h, -_lane_col(dk_s[h], AUG_ONE), dfk)
            dk_ref[...] = dk
            dv_ref[...] = dv
            df_ref[...] = dfk

    qspec = pl.BlockSpec((4, B, LANE), lambda s, qt, kt: (0, qt[s], 0))
    kspec = pl.BlockSpec((4, B, LANE), lambda s, qt, kt: (0, kt[s], 0))
    ospec = pl.BlockSpec((B, CH), lambda s, qt, kt: (kt[s], 0))
    grid_spec = pltpu.PrefetchScalarGridSpec(
        num_scalar_prefetch=2, grid=(qt.shape[0],),
        in_specs=[qspec, kspec, kspec, qspec, pl.BlockSpec(sel.shape, lambda s, qt, kt: (0, 0, 0))],
        out_specs=[ospec, ospec, pl.BlockSpec((B, LANE), lambda s, qt, kt: (kt[s], 0))],
        scratch_shapes=[pltpu.VMEM((4, B, LANE), F32), pltpu.VMEM((4, B, LANE), F32)])
    return pl.pallas_call(
        body, name=name, grid_spec=grid_spec,
        out_shape=[jax.ShapeDtypeStruct((T, CH), F32), jax.ShapeDtypeStruct((T, CH), F32),
                   jax.ShapeDtypeStruct((T, LANE), F32)],
        compiler_params=_cp("arbitrary"))(qt, kt, qb, ka, va, doa, sel)


def _fox2_bwd(qb, ka, va, doa, sel, name):
    T = qb.shape[1]
    B = _tile(T, FOX_B)
    n = T // B
    qt, kt = _tri_steps(n, True)
    nsteps = qt.shape[0]

    def body(qt_ref, kt_ref, qb_ref, ka_ref, va_ref, do_ref, p_ref, dk_ref, dv_ref, df_ref, dq_hbm,
             dk_s, dv_s, kat_s, dq_s, sem):
        step = pl.program_id(0)
        qi, ki = qt_ref[step], kt_ref[step]

        @pl.when(step == 0)
        def _():
            dq_s[...] = jnp.zeros_like(dq_s)

        @pl.when(qi == ki)
        def _():
            dk_s[...] = jnp.zeros_like(dk_s)
            dv_s[...] = jnp.zeros_like(dv_s)
            for h in range(4):
                kat_s[h] = ka_ref[h].astype(F32).T.astype(kat_s.dtype)

        def update(diag):
            for h in range(4):
                st = _mxu_dot(ka_ref[h], qb_ref[h], _NT)
                if diag:
                    st = _causal(st, True)
                pt = jnp.exp(st)
                dst = (pt * _mxu_dot(va_ref[h], do_ref[h], _NT)).astype(_MXU)
                dv_s[h] += _dot(pt, do_ref[h])
                dk_s[h] += _mxu_dot(dst, qb_ref[h], _NN)
                dq_s[qi, h] += _mxu_dot(kat_s[h], dst, _NN)

        @pl.when(qi == ki)
        def _():
            update(True)

        @pl.when(qi > ki)
        def _():
            update(False)

        @pl.when(qi == n - 1)
        def _():
            lane = lax.broadcasted_iota(jnp.int32, (1, LANE), 1)
            dk = jnp.zeros((B, CH), F32)
            dv = jnp.zeros((B, CH), F32)
            dfk = jnp.zeros((B, LANE), F32)
            for h in range(4):
                dk = dk + _exact_dot(dk_s[h], p_ref[h], _NT, "a")
                dv = dv + _exact_dot(dv_s[h], p_ref[h], _NT, "a")
                dfk = jnp.where(lane == h, -_lane_col(dk_s[h], AUG_ONE), dfk)
            dk_ref[...] = dk
            dv_ref[...] = dv
            df_ref[...] = dfk

        @pl.when(step == nsteps - 1)
        def _():
            cp = pltpu.make_async_copy(dq_s, dq_hbm, sem)
            cp.start()
            cp.wait()

    qspec = pl.BlockSpec((4, B, LANE), lambda s, qt, kt: (0, qt[s], 0))
    kspec = pl.BlockSpec((4, B, LANE), lambda s, qt, kt: (0, kt[s], 0))
    ospec = pl.BlockSpec((B, CH), lambda s, qt, kt: (kt[s], 0))
    grid_spec = pltpu.PrefetchScalarGridSpec(
        num_scalar_prefetch=2, grid=(nsteps,),
        in_specs=[qspec, kspec, kspec, qspec, pl.BlockSpec(sel.shape, lambda s, qt, kt: (0, 0, 0))],
        out_specs=[ospec, ospec, pl.BlockSpec((B, LANE), lambda s, qt, kt: (kt[s], 0)),
                   pl.BlockSpec(memory_space=pl.ANY)],
        scratch_shapes=[pltpu.VMEM((4, B, LANE), F32), pltpu.VMEM((4, B, LANE), F32), pltpu.VMEM((4, LANE, B), _MXU),
                        pltpu.VMEM((n, 4, LANE, B), F32), pltpu.SemaphoreType.DMA])
    return pl.pallas_call(
        body, name=name, grid_spec=grid_spec,
        out_shape=[jax.ShapeDtypeStruct((T, CH), F32), jax.ShapeDtypeStruct((T, CH), F32),
                   jax.ShapeDtypeStruct((T, LANE), F32), jax.ShapeDtypeStruct((n, 4, LANE, B), F32)],
        compiler_params=_cp("arbitrary"))(qt, kt, qb, ka, va, doa, sel)


def _fox2_post(zf, dqt, dkn, dv, dfk, sel, gq, gk, bf, name):
    T = zf.shape[0]
    tm = _tile(T, FOX_TM)
    nt = T // tm
    B = dqt.shape[3]
    per = B // tm
    upper = jnp.asarray(_tri(tm, False), _MXU)

    def body(z_ref, dqt_ref, dk_ref, dv_ref, df_ref, p_ref, gq_ref, gk_ref, bf_ref, u_ref, dz_ref, sm_ref, carry, rc_s):
        @pl.when(pl.program_id(0) == 0)
        def _():
            carry[...] = jnp.zeros_like(carry)
            sm_ref[...] = jnp.zeros_like(sm_ref)

        lane = lax.broadcasted_iota(jnp.int32, (1, LANE), 1)
        dqn = jnp.zeros((tm, CH), F32)
        dfq = jnp.zeros((tm, LANE), F32)
        for h in range(4):
            blk = dqt_ref[0, h].T
            dqn = dqn + _exact_dot(blk * (HEAD ** -0.5), p_ref[h], _NT, "a")
            dfq = jnp.where(lane == h, _lane_col(blk, AUG_F), dfq)
        z = z_ref[...]
        q_raw, k_raw = z[:, :CH], z[:, CH:2 * CH]
        _, q_r = _head_rms(q_raw, gq_ref[...], 4)
        _, k_r = _head_rms(k_raw, gk_ref[...], 4)
        dq, dgq = _head_rms_bwd(dqn, q_raw, q_r, gq_ref[...], 4)
        dk, dgk = _head_rms_bwd(dk_ref[...], k_raw, k_r, gk_ref[...], 4)
        rc_s[...] = _exact_dot(u_ref[...], dfq + df_ref[...], _NN, "b") + carry[pl.ds(0, 1), :]
        carry[pl.ds(0, 1), :] = rc_s[pl.ds(0, 1), :]
        x = z[:, 3 * CH:] + bf_ref[...]
        dff = jnp.where(lane < 4, rc_s[...] * _sigmoid(-x), 0.0)
        dz_ref[:, :CH] = dq.astype(dz_ref.dtype)
        dz_ref[:, CH:2 * CH] = dk.astype(dz_ref.dtype)
        dz_ref[:, 2 * CH:3 * CH] = dv_ref[...].astype(dz_ref.dtype)
        dz_ref[:, 3 * CH:] = dff.astype(dz_ref.dtype)
        sm_ref[pl.ds(0, 1), :] += dgq
        sm_ref[pl.ds(1, 1), :] += dgk
        sm_ref[pl.ds(2, 1), :LANE] += jnp.sum(dff, axis=0, keepdims=True)

    rev = lambda i: (nt - 1 - i, 0)
    row = pl.BlockSpec((tm, CH), rev)
    lrow = pl.BlockSpec((tm, LANE), rev)
    vec = pl.BlockSpec((1, CH), lambda i: (0, 0))
    return pl.pallas_call(
        body, name=name, grid=(nt,),
        in_specs=[pl.BlockSpec((tm, 3 * CH + LANE), rev),
                  pl.BlockSpec((1, 4, LANE, tm), lambda i: ((nt - 1 - i) // per, 0, 0, (nt - 1 - i) % per)),
                  row, row, lrow, pl.BlockSpec(sel.shape, lambda i: (0, 0, 0)), vec, vec,
                  pl.BlockSpec((1, LANE), lambda i: (0, 0)), pl.BlockSpec((tm, tm), lambda i: (0, 0))],
        out_specs=[pl.BlockSpec((tm, 3 * CH + LANE), rev), pl.BlockSpec((8, CH), lambda i: (0, 0))],
        out_shape=[jax.ShapeDtypeStruct((T, 3 * CH + LANE), _MXU), jax.ShapeDtypeStruct((8, CH), F32)],
        scratch_shapes=[pltpu.VMEM((8, LANE), F32), pltpu.VMEM((tm, LANE), F32)],
        compiler_params=_cp("arbitrary"))(zf, dqt, dkn, dv, dfk, sel, gq, gk, bf, upper)


def _merge_fwd(acts, zg, wbr, wout, x1, name):
    T, D = x1.shape
    tm = _tile(T, 256)

    def body(a0, a1, a2, a3, zg_ref, wbr_ref, wout_ref, x_ref, o_ref, mg_ref):
        merged = None
        for i, a_ref in enumerate((a0, a1, a2, a3)):
            term = _sigmoid(zg_ref[:, i * D:(i + 1) * D]) * _dot(a_ref[...], wbr_ref[i])
            merged = term if merged is None else merged + term
        mg_ref[...] = merged.astype(mg_ref.dtype)
        o_ref[...] = x_ref[...] + _dot(merged, wout_ref[...])

    arow = pl.BlockSpec((tm, CH), lambda i: (i, 0))
    xrow = pl.BlockSpec((tm, D), lambda i: (i, 0))
    return pl.pallas_call(
        body, name=name, grid=(T // tm,),
        in_specs=[arow, arow, arow, arow, pl.BlockSpec((tm, 4 * D), lambda i: (i, 0)),
                  pl.BlockSpec((4, CH, D), lambda i: (0, 0, 0)), pl.BlockSpec((D, D), lambda i: (0, 0)), xrow],
        out_specs=[xrow, xrow],
        out_shape=[jax.ShapeDtypeStruct((T, D), F32), jax.ShapeDtypeStruct((T, D), _MXU)],
        compiler_params=_cp("parallel"))(*acts, zg, wbr, wout, x1)


def _merge_bwd(dx2, acts, zg, wbr, wout, name):
    T, D = dx2.shape
    tm = _tile(T, 256)
    nt = T // tm

    def body(dx_ref, a0, a1, a2, a3, zg_ref, wbr_ref, wout_ref, d0, d1, d2, d3, dzg_ref, dw_ref, dw_s):
        i = pl.program_id(0)

        @pl.when(i == 0)
        def _():
            dw_s[...] = jnp.zeros_like(dw_s)

        dm = _dot(dx_ref[...], wout_ref[...], _NT)
        for b, (a_ref, d_ref) in enumerate(((a0, d0), (a1, d1), (a2, d2), (a3, d3))):
            av = a_ref[...].astype(_MXU)
            g = _sigmoid(zg_ref[:, b * D:(b + 1) * D])
            p = _dot(av, wbr_ref[b])
            dzg_ref[:, b * D:(b + 1) * D] = (dm * p * (g * (1.0 - g))).astype(dzg_ref.dtype)
            dp = (dm * g).astype(_MXU)
            d_ref[...] = _dot(dp, wbr_ref[b], _NT)
            dw_s[b] += _dot(av, dp, _TN)

        @pl.when(i == nt - 1)
        def _():
            dw_ref[...] = dw_s[...].astype(dw_ref.dtype)

    arow = pl.BlockSpec((tm, CH), lambda i: (i, 0))
    xrow = pl.BlockSpec((tm, D), lambda i: (i, 0))
    grow = pl.BlockSpec((tm, 4 * D), lambda i: (i, 0))
    wspec = pl.BlockSpec((4, CH, D), lambda i: (0, 0, 0))
    ash = jax.ShapeDtypeStruct((T, CH), F32)
    return pl.pallas_call(
        body, name=name, grid=(nt,),
        in_specs=[xrow, arow, arow, arow, arow, grow, wspec, pl.BlockSpec((D, D), lambda i: (0, 0))],
        out_specs=[arow, arow, arow, arow, grow, wspec],
        out_shape=[ash, ash, ash, ash, jax.ShapeDtypeStruct((T, 4 * D), _MXU), jax.ShapeDtypeStruct((4, CH, D), _MXU)],
        scratch_shapes=[pltpu.VMEM((4, CH, D), F32)],
        compiler_params=_cp("arbitrary"))(dx2, *acts, zg, wbr, wout)


def _rows_2d(a):
    return a.reshape((-1, a.shape[-1])) if a.ndim > 1 else a.reshape((1, -1))


def _row_tile(rows, cols, n_bufs):
    padded = -(-cols // LANE) * LANE
    cap = max(8, (VMEM_LIMIT // 3) // (2 * n_bufs * 4 * padded))
    return _tile(rows, cap, 8)


def _sum8(recv, name):
    shape = recv.shape[1:]
    r2 = recv.reshape((N_DEV, -1, shape[-1]))
    rows, cols = r2.shape[1:]
    tr = _row_tile(rows, cols, N_DEV // 2 + 1)

    def body(r_ref, o_ref):
        acc = r_ref[0].astype(F32)
        for d in range(1, N_DEV):
            acc = acc + r_ref[d].astype(F32)
        o_ref[...] = acc

    out = pl.pallas_call(
        body, name=name, grid=(rows // tr,),
        in_specs=[pl.BlockSpec((N_DEV, tr, cols), lambda i: (0, i, 0))],
        out_specs=pl.BlockSpec((tr, cols), lambda i: (i, 0)),
        out_shape=jax.ShapeDtypeStruct((rows, cols), F32), compiler_params=_cp("parallel"))(r2)
    return out.reshape(shape)


def _adamw(w, g, m, v, name):
    shape = w.shape
    w2, g2, m2, v2 = (_rows_2d(a) for a in (w, g, m, v))
    rows, cols = w2.shape
    tr = _row_tile(rows, cols, 7)

    def body(w_ref, g_ref, m_ref, v_ref, d_ref, nm_ref, nv_ref):
        gv = g_ref[...]
        nm = ADAM_B1 * m_ref[...] + (1.0 - ADAM_B1) * gv
        nv = ADAM_B2 * v_ref[...] + (1.0 - ADAM_B2) * jnp.square(gv)
        m_hat = nm / (1.0 - ADAM_B1 ** ADAM_STEP)
        v_hat = nv / (1.0 - ADAM_B2 ** ADAM_STEP)
        d_ref[...] = -ADAM_LR * (m_hat / (jnp.sqrt(v_hat) + ADAM_EPS) + ADAM_WD * w_ref[...])
        nm_ref[...] = nm
        nv_ref[...] = nv

    spec = pl.BlockSpec((tr, cols), lambda i: (i, 0))
    osh = jax.ShapeDtypeStruct((rows, cols), F32)
    outs = pl.pallas_call(
        body, name=name, grid=(rows // tr,), in_specs=[spec] * 4, out_specs=[spec] * 3,
        out_shape=[osh] * 3, compiler_params=_cp("parallel"))(w2, g2, m2, v2)
    return tuple(o.reshape(shape) for o in outs)


def _exchange(items, name):
    n = len(items)
    widths, out_shapes = [], []
    for src, kind, ax in items:
        if kind == "gather":
            w = src.shape[ax]
            shp = list(src.shape)
            shp[ax] = N_DEV * w
        else:
            w = src.shape[ax] // N_DEV
            shp = list(src.shape)
            shp[ax] = w
            shp = [N_DEV] + shp
        widths.append(w)
        out_shapes.append(jax.ShapeDtypeStruct(tuple(shp), src.dtype))

    def body(*refs):
        srcs, outs = refs[:n], refs[n:2 * n]
        send, recv, lsem = refs[2 * n:]
        x, y, c = lax.axis_index("x"), lax.axis_index("y"), lax.axis_index("c")
        me = 4 * x + 2 * y + c

        def peer(k):
            b = k + 1
            px = 1 - x if b & 4 else x
            py = 1 - y if b & 2 else y
            pc = 1 - c if b & 1 else c
            return (px, py, pc), 4 * px + 2 * py + pc

        def win(ref, ax, idx, w):
            return ref.at[tuple([slice(None)] * ax + [pl.ds(idx * w, w)])]

        def ends(j, mine, theirs):
            _, kind, ax = items[j]
            if kind == "gather":
                return srcs[j], win(outs[j], ax, mine, widths[j])
            return win(srcs[j], ax, theirs, widths[j]), outs[j].at[mine]

        local, sent = [], []
        for j in range(n):
            s, d = ends(j, me, me)
            cp = pltpu.make_async_copy(s, d, lsem.at[j])
            cp.start()
            local.append(cp)
            for k in range(N_DEV - 1):
                dev, pid = peer(k)
                s, d = ends(j, me, pid)
                cp = pltpu.make_async_remote_copy(s, d, send.at[j, k], recv.at[j, k], device_id=dev,
                                                  device_id_type=pl.DeviceIdType.MESH)
                cp.start()
                sent.append(cp)
        for j in range(n):
            for k in range(N_DEV - 1):
                dev, pid = peer(k)
                s, d = ends(j, pid, me)
                pltpu.make_async_remote_copy(s, d, send.at[j, k], recv.at[j, k], device_id=dev,
                                             device_id_type=pl.DeviceIdType.MESH).wait_recv()
        for cp in sent:
            cp.wait_send()
        for cp in local:
            cp.wait()

    hbm = pl.BlockSpec(memory_space=pl.ANY)
    return pl.pallas_call(
        body, name=name, in_specs=[hbm] * n, out_specs=[hbm] * n, out_shape=out_shapes,
        scratch_shapes=[pltpu.SemaphoreType.DMA((n, N_DEV - 1)), pltpu.SemaphoreType.DMA((n, N_DEV - 1)),
                        pltpu.SemaphoreType.DMA((n,))],
        compiler_params=pltpu.CompilerParams(has_side_effects=True))(*[it[0] for it in items])


def _exchange_plan(items):
    widths, out_shapes = [], []
    for src, kind, ax in items:
        shp = list(src.shape)
        if kind == "gather":
            w = src.shape[ax]
            shp[ax] = N_DEV * w
        else:
            w = src.shape[ax] // N_DEV
            shp[ax] = w
            shp = [N_DEV] + shp
        widths.append(w)
        out_shapes.append((tuple(shp), src.dtype))
    return widths, out_shapes


def _exchange_refs(items, widths, srcs, outs):
    x, y, c = lax.axis_index("x"), lax.axis_index("y"), lax.axis_index("c")
    me = 4 * x + 2 * y + c

    def peer(k):
        b = k + 1
        px = 1 - x if b & 4 else x
        py = 1 - y if b & 2 else y
        pc = 1 - c if b & 1 else c
        return (px, py, pc), 4 * px + 2 * py + pc

    def win(ref, ax, idx, w):
        return ref.at[tuple([slice(None)] * ax + [pl.ds(idx * w, w)])]

    def ends(j, mine, theirs):
        _, kind, ax = items[j]
        if kind == "gather":
            return srcs[j], win(outs[j], ax, mine, widths[j])
        return win(srcs[j], ax, theirs, widths[j]), outs[j].at[mine]

    return me, peer, ends


_HBM = pl.BlockSpec(memory_space=pltpu.HBM)
_SEM = pl.BlockSpec(memory_space=pltpu.SEMAPHORE)


def _exchange_start(items, name):
    n = len(items)
    widths, out_shapes = _exchange_plan(items)
    meta = [(None, kind, ax) for _, kind, ax in items]

    def body(*refs):
        srcs, lands = refs[:n], refs[n:2 * n]
        send, recv, lsem = refs[2 * n], refs[2 * n + 1], refs[2 * n + 2]
        token = refs[-1]
        me, peer, ends = _exchange_refs(meta, widths, srcs, lands)
        for j in range(n):
            for k in range(N_DEV - 1):
                dev, pid = peer(k)
                s, d = ends(j, me, pid)
                q = j * (N_DEV - 1) + k
                pltpu.make_async_remote_copy(s, d, send.at[q], recv.at[q], device_id=dev,
                                             device_id_type=pl.DeviceIdType.MESH).start()
        for j in range(n):
            s, d = ends(j, me, me)
            pltpu.make_async_copy(s, d, lsem.at[j]).start()
        token[...] = jnp.zeros_like(token)

    srcs = [pltpu.with_memory_space_constraint(it[0], pltpu.HBM) for it in items]
    lands = [pltpu.with_memory_space_constraint(lax.empty(shp, dt), pltpu.HBM) for shp, dt in out_shapes]
    outs = pl.pallas_call(
        body, name=name,
        out_shape=(pltpu.SemaphoreType.DMA((n * (N_DEV - 1),)), pltpu.SemaphoreType.DMA((n * (N_DEV - 1),)),
                   pltpu.SemaphoreType.DMA((n,)),
                   *[pltpu.HBM(s.shape, s.dtype) for s in srcs], *[pltpu.HBM(shp, dt) for shp, dt in out_shapes],
                   jax.ShapeDtypeStruct((8, LANE), F32)),
        in_specs=[_HBM] * (2 * n),
        out_specs=(_SEM, _SEM, _SEM, *([_HBM] * (2 * n)), pl.BlockSpec(memory_space=pltpu.VMEM)),
        input_output_aliases={i: 3 + i for i in range(2 * n)},
        compiler_params=pltpu.CompilerParams(has_side_effects=pltpu.SideEffectType.DATAFLOW_SIDE_EFFECTING),
    )(*srcs, *lands)
    handle = (meta, widths, outs[0], outs[1], outs[2], outs[3:3 + n], outs[3 + n:3 + 2 * n])
    return handle, outs[-1]


def _exchange_wait(handle, after, name):
    meta, widths, send_sem, recv_sem, local_sem, src_thru, land_thru = handle
    n = len(meta)

    def body(*refs):
        srcs, lands = refs[:n], refs[n:2 * n]
        send, recv, lsem = refs[2 * n], refs[2 * n + 1], refs[2 * n + 2]
        me, peer, ends = _exchange_refs(meta, widths, srcs, lands)
        for j in range(n):
            for k in range(N_DEV - 1):
                dev, pid = peer(k)
                q = j * (N_DEV - 1) + k
                s, d = ends(j, me, pid)
                pltpu.make_async_remote_copy(s, d, send.at[q], recv.at[q], device_id=dev,
                                             device_id_type=pl.DeviceIdType.MESH).wait_send()
                s, d = ends(j, pid, me)
                pltpu.make_async_remote_copy(s, d, send.at[q], recv.at[q], device_id=dev,
                                             device_id_type=pl.DeviceIdType.MESH).wait_recv()
        for j in range(n):
            s, d = ends(j, me, me)
            pltpu.make_async_copy(s, d, lsem.at[j]).wait()

    outs = pl.pallas_call(
        body, name=name,
        out_shape=tuple(pltpu.HBM(a.shape, a.dtype) for a in (*src_thru, *land_thru)),
        in_specs=[_HBM] * (2 * n) + [_SEM, _SEM, _SEM, pl.BlockSpec(memory_space=pl.ANY)],
        out_specs=tuple([_HBM] * (2 * n)),
        input_output_aliases={i: i for i in range(2 * n)},
        compiler_params=pltpu.CompilerParams(has_side_effects=pltpu.SideEffectType.DATAFLOW_SIDE_EFFECTING),
    )(*src_thru, *land_thru, send_sem, recv_sem, local_sem, after)
    return list(outs[n:])


def _pack(arrs):
    flat = jnp.concatenate([a.reshape(-1).astype(F32) for a in arrs])
    n = flat.shape[0]
    rows = -(-n // (8 * LANE)) * 8
    return jnp.pad(flat, (0, rows * LANE - n)).reshape(rows, LANE)


def _unpack(buf, shapes):
    flat = buf.reshape(-1)
    out, off = [], 0
    for s in shapes:
        sz = int(np.prod(s))
        out.append(flat[off:off + sz].reshape(s))
        off += sz
    return out


def _pad_axis(a, axis, size):
    pad = [(0, 0)] * a.ndim
    pad[axis] = (0, size - a.shape[axis])
    return jnp.pad(a, pad)


def _ffn_forward(x, g, wg, wu, wd, tag):
    a = _rms_fwd(x, g, f"{tag}_rms")
    gate, up, hid = _ffn_up(a, wg, wu, f"{tag}_up")
    out = _mm([(hid, wd)], "nn", F32, f"{tag}_down", scale=0.5, res=x)
    return out, (x, a, gate, up, hid)


def _ffn_backward(dxp, saved, g, wg, wu, wd, tag, emit=None, after=None):
    x, a, gate, up, hid = saved
    d_gate, d_up = _ffn_bwd_hid(dxp, wd, gate, up, f"{tag}_bwd_hid", after)
    d_wd = _mm([(hid, dxp)], "tn", _MXU, f"{tag}_dwd", scale=0.5, tk=2048)
    tok = emit("down", d_wd) if emit is not None else None
    d_wg = _mm([(a, d_gate)], "tn", _MXU, f"{tag}_dwg", tk=2048, after=tok)
    tok = emit("gate", d_wg) if emit is not None else None
    d_wu = _mm([(a, d_up)], "tn", _MXU, f"{tag}_dwu", tk=2048, after=tok)
    tok = emit("up", d_wu) if emit is not None else None
    d_a = _mm([(d_gate, wg), (d_up, wu)], "nt", F32, f"{tag}_da", after=tok)
    dx, dg = _rms_bwd(d_a, x, g, dxp, f"{tag}_rms_bwd")
    return dx, dg, d_wg, d_wu, d_wd


def _tile_vec(v, reps):
    return jnp.tile(v.reshape(1, -1), (1, reps))


def _mixer_forward(x1, p, consts, tag):
    h = _rms_fwd(x1, p["mix_norm"], f"{tag}_rms")
    zg = _mm([(h, p["w_zg"])], "nn", F32, f"{tag}_in_g")
    zc = _mm([(h, p["w_conf"])], "nn", F32, f"{tag}_in_c")
    zs = _mm([(h, p["w_sc"])], "nn", F32, f"{tag}_in_s")
    zw = _mm([(h, p["w_swa"])], "nn", F32, f"{tag}_in_w")
    zf = _mm([(h, p["w_fox"])], "nn", F32, f"{tag}_in_f")
    u1, act_c = _conf_fwd(zc, p["conf_dw"], p["conf_dw_b"], p["conf_ln_g"], p["conf_ln_b"], f"{tag}_conf")
    act_s = _sc_fwd(zs, p["sc_conv"], f"{tag}_sc")
    act_w = _swa_fwd(zw, p["swa_q_norm"], p["swa_k_norm"], p["swa_sink"], consts["bias"], consts["expand"], f"{tag}_swa")
    qa, ka, va = _fox2_prep(zf, p["fox_q_norm"], p["fox_k_norm"], p["b_forget"], consts["sel"], f"{tag}_fox_prep")
    act_f, qb = _fox2_fwd(qa, ka, va, consts["sel"], f"{tag}_fox")
    acts = (act_c, act_s, act_w, act_f)
    x2, merged = _merge_fwd(acts, zg, p["w_br"], p["w_out"], x1, f"{tag}_merge")
    saved = (x1, h, zg, zc, zs, zw, zf, u1, acts, qb, ka, va, merged)
    return x2, saved


def _mixer_backward(dx2, saved, p, consts, tag, after=None):
    x1, h, zg, zc, zs, zw, zf, u1, acts, qb, ka, va, merged = saved
    g = {}
    g["w_out"] = _mm([(merged, dx2)], "tn", _MXU, f"{tag}_dwout", tk=2048, after=after)
    d_c, d_s, d_w, d_f, dzg, g["w_br"] = _merge_bwd(dx2, acts, zg, p["w_br"], p["w_out"], f"{tag}_merge_bwd")
    du1, sm_c = _conf_bwd_ln(d_c, u1, p["conf_ln_g"], p["conf_ln_b"], f"{tag}_conf_bwd_ln")
    dzc, g["conf_dw"] = _conf_bwd_conv(zc, du1, p["conf_dw"], f"{tag}_conf_bwd_conv")
    g["conf_ln_g"], g["conf_ln_b"], g["conf_dw_b"] = sm_c[0], sm_c[1], sm_c[2]
    dzs, g["sc_conv"] = _sc_bwd(zs, d_s, p["sc_conv"], f"{tag}_sc_bwd")
    dzw, dgq, dgk, g["swa_sink"], g["rel_bias"] = _swa_bwd(
        zw, d_w, p["swa_q_norm"], p["swa_k_norm"], p["swa_sink"], consts["bias"], consts["bucket"], consts["expand"],
        f"{tag}_swa_bwd")
    g["swa_q_norm"], g["swa_k_norm"] = dgq, dgk
    doa = _fox2_bwd_prep(acts[3], d_f, consts["sel"], f"{tag}_fox_bwd_prep")
    dkn, dv, dfk, dqt = _fox2_bwd(qb, ka, va, doa, consts["sel"], f"{tag}_fox_bwd")
    dzf, sm_f = _fox2_post(zf, dqt, dkn, dv, dfk, consts["sel"], p["fox_q_norm"], p["fox_k_norm"], p["b_forget"], f"{tag}_fox_post")
    g["fox_q_norm"], g["fox_k_norm"], g["b_forget"] = sm_f[0], sm_f[1], sm_f[2]
    parts = ((dzg, "w_zg"), (dzc, "w_conf"), (dzs, "w_sc"), (dzw, "w_swa"), (dzf, "w_fox"))
    for dz, wname in parts:
        g[wname] = _mm([(h, dz)], "tn", _MXU, f"{tag}_d{wname}", tk=2048)
    dh = _mm([(dz, p[wname]) for dz, wname in parts], "nt", F32, f"{tag}_dh", tm=512)
    dx1, g["mix_norm"] = _rms_bwd(dh, x1, p["mix_norm"], dx2, f"{tag}_rms_bwd")
    return dx1, g


W_NAMES = ['rel_bias', 'ffn1_norm', 'ffn1_w_gate', 'ffn1_w_up', 'ffn1_w_down', 'mix_norm', 'w_in', 'b_forget', 'conf_dw',
           'conf_dw_b', 'conf_ln_g', 'conf_ln_b', 'conf_w_out', 'sc_conv', 'sc_w_out', 'swa_q_norm', 'swa_k_norm',
           'swa_sink', 'swa_w_o', 'fox_q_norm', 'fox_k_norm', 'fox_w_o', 'w_out', 'ffn2_norm', 'ffn2_w_gate',
           'ffn2_w_up', 'ffn2_w_down']
SMALL = ['rel_bias', 'ffn1_norm', 'mix_norm', 'b_forget', 'conf_dw', 'conf_dw_b', 'conf_ln_g', 'conf_ln_b', 'sc_conv',
         'swa_q_norm', 'swa_k_norm', 'swa_sink', 'fox_q_norm', 'fox_k_norm', 'ffn2_norm']
BRANCH_W = ['conf_w_out', 'sc_w_out', 'swa_w_o', 'fox_w_o']
IN_CONF, IN_SC, IN_SWA, IN_FOX, IN_FF = (0, 512), (512, 1280), (1280, 1792), (1792, 2560), (2560, 2564)


def _step(w, m, v, x, loss_target):
    T, D = x.shape
    L = w["w_out"].shape[0]
    fs = w["ffn1_w_gate"].shape[2]
    fsp = -(-fs // LANE) * LANE
    dev = 4 * lax.axis_index("x") + 2 * lax.axis_index("y") + lax.axis_index("c")

    def cast(a):
        return a.astype(_MXU)

    win = w["w_in"]
    fox_cols = jnp.concatenate([win[..., IN_FOX[0]:IN_FF[1]],
                                jnp.zeros(win.shape[:2] + (LANE - (IN_FF[1] - IN_FF[0]),), win.dtype)], axis=-1)
    shards = {
        "ffn1_w_gate": (cast(_pad_axis(w["ffn1_w_gate"], 2, fsp)), 2),
        "ffn1_w_up": (cast(_pad_axis(w["ffn1_w_up"], 2, fsp)), 2),
        "ffn1_w_down": (cast(_pad_axis(w["ffn1_w_down"], 1, fsp)), 1),
        "ffn2_w_gate": (cast(_pad_axis(w["ffn2_w_gate"], 2, fsp)), 2),
        "ffn2_w_up": (cast(_pad_axis(w["ffn2_w_up"], 2, fsp)), 2),
        "ffn2_w_down": (cast(_pad_axis(w["ffn2_w_down"], 1, fsp)), 1),
        "w_zg": (cast(win[..., IN_FF[1]:]), 1),
        "w_conf": (cast(win[..., IN_CONF[0]:IN_CONF[1]]), 1),
        "w_sc": (cast(win[..., IN_SC[0]:IN_SC[1]]), 1),
        "w_swa": (cast(win[..., IN_SWA[0]:IN_SWA[1]]), 1),
        "w_fox": (cast(fox_cols), 1),
        "w_out": (cast(w["w_out"]), 1),
        "w_br": (cast(jnp.stack([w[n] for n in BRANCH_W], axis=1)), 3),
    }
    big = list(shards)
    conv_shard = jnp.concatenate([jnp.swapaxes(w["conf_dw"], 1, 2), jnp.swapaxes(w["sc_conv"], 1, 2)], axis=2)
    conv_full = jnp.swapaxes(_exchange([(conv_shard, "gather", 1)], "gather_conv")[0], 1, 2)
    conf_dw_full = _pad_axis(conv_full[:, :CONV_K], 1, CONV_HALO)
    sc_conv_full = _pad_axis(conv_full[:, CONV_K:], 1, SC_HALO)

    stages = (("ffn1", ["ffn1_w_gate", "ffn1_w_up"]), ("ffn1d", ["ffn1_w_down"]),
              ("mix", ["w_zg", "w_conf", "w_sc", "w_swa", "w_fox", "w_out", "w_br"]),
              ("ffn2", ["ffn2_w_gate", "ffn2_w_up", "ffn2_w_down"]))
    stage_names = dict(stages)
    flight = {}

    def depart(l, st, dep):
        items = [(shards[n][0][l], "gather", shards[n][1] - 1) for n in stage_names[st]]
        if dep is not None:
            src0 = items[0][0]
            zero = (dep[(0,) * dep.ndim].astype(F32) * 0.0).astype(src0.dtype)
            items[0] = (src0 + zero,) + items[0][1:]
        flight[l, st], tok = _exchange_start(items, f"gather_start_l{l}_{st}")
        return tok

    def arrive(l, st, after):
        got = _exchange_wait(flight.pop((l, st)), after, f"gather_wait_l{l}_{st}")
        params[l].update(zip(stage_names[st], got))
        return got[0]

    bucket = jnp.asarray(_swa_bucket_matrix(min(SWA_TQ, T)))
    consts = {"bucket": bucket, "expand": jnp.asarray(_kv_expand_matrix(), _MXU),
              "sel": jnp.asarray(_head_select_matrix(), _MXU),
              "bias": _swa_bias(w["rel_bias"], bucket, "swa_bias")}

    def layer_params(l):
        p = {}
        for n in ("ffn1_norm", "mix_norm", "ffn2_norm", "conf_dw_b", "conf_ln_g", "conf_ln_b"):
            p[n] = w[n][l].reshape(1, -1)
        p["conf_dw"], p["sc_conv"] = conf_dw_full[l], sc_conv_full[l]
        p["swa_q_norm"], p["fox_q_norm"] = _tile_vec(w["swa_q_norm"][l], 4), _tile_vec(w["fox_q_norm"][l], 4)
        p["swa_k_norm"], p["fox_k_norm"] = _tile_vec(w["swa_k_norm"][l], 2), _tile_vec(w["fox_k_norm"][l], 4)
        p["swa_sink"] = w["swa_sink"][l].reshape(1, 4)
        p["b_forget"] = _pad_axis(w["b_forget"][l].reshape(1, 4), 1, LANE)
        return p

    params = [layer_params(l) for l in range(L)]
    saved = [None] * L
    cur = x
    first = depart(0, "ffn1", None)
    for l, p in enumerate(params):
        got = arrive(l, "ffn1", cur if l else first)
        zero = (depart(l, "ffn1d", got) + depart(l, "mix", got))[0:1, 0:1]
        a = _rms_fwd(cur, p["ffn1_norm"] + zero, f"l{l}_ffn1_rms")
        gate, up, hid = _ffn_up(a, p["ffn1_w_gate"], p["ffn1_w_up"], f"l{l}_ffn1_up")
        got = arrive(l, "ffn1d", hid)
        tok = depart(l, "ffn2", got)
        x1 = _mm([(hid, p["ffn1_w_down"])], "nn", F32, f"l{l}_ffn1_down", scale=0.5, res=cur, after=tok)
        s1 = (cur, a, gate, up, hid)
        got = arrive(l, "mix", x1)
        zero = depart(l + 1, "ffn1", got)[0:1, 0:1] if l + 1 < L else 0.0
        x2, s2 = _mixer_forward(x1, dict(p, mix_norm=p["mix_norm"] + zero), consts, f"l{l}_mix")
        arrive(l, "ffn2", x2)
        cur, s3 = _ffn_forward(x2, p["ffn2_norm"], p["ffn2_w_gate"], p["ffn2_w_up"], p["ffn2_w_down"], f"l{l}_ffn2")
        saved[l] = (s1, s2, s3)
    dcur, loss_part = _loss_grad(cur, loss_target)

    grads = [None] * L
    leaving = []

    def leave(l, st, names, g):
        h, tok = _exchange_start([(g[n], "scatter", shards[n][1] - 1) for n in names], f"scatter_start_l{l}_{st}")
        leaving.append((l, st, names, h))
        return tok

    tok = None
    for l in reversed(range(L)):
        p = params[l]
        s1, s2, s3 = saved[l]
        g = {}
        dcur, g["ffn2_norm"], g["ffn2_w_gate"], g["ffn2_w_up"], g["ffn2_w_down"] = _ffn_backward(
            dcur, s3, p["ffn2_norm"], p["ffn2_w_gate"], p["ffn2_w_up"], p["ffn2_w_down"], f"l{l}_ffn2", after=tok)
        tok = leave(l, "ffn2", stage_names["ffn2"], g)
        dcur, gm = _mixer_backward(dcur, s2, p, consts, f"l{l}_mix", after=tok)
        g.update(gm)
        tok = leave(l, "mix", stage_names["mix"], g)
        ffn1 = ["ffn1_w_gate", "ffn1_w_up", "ffn1_w_down"]
        emit = (lambda which, arr, l=l: leave(l, which, [f"ffn1_w_{which}"], {f"ffn1_w_{which}": arr})) if l == 0 else None
        dcur, g["ffn1_norm"], g["ffn1_w_gate"], g["ffn1_w_up"], g["ffn1_w_down"] = _ffn_backward(
            dcur, s1, p["ffn1_norm"], p["ffn1_w_gate"], p["ffn1_w_up"], p["ffn1_w_down"], f"l{l}_ffn1", emit, after=tok)
        if emit is None:
            tok = leave(l, "ffn1", ffn1, g)
        grads[l] = g
    grad_x = dcur

    gsum = {n: [None] * L for n in big}
    for l, st, names, h in leaving:
        for n, r in zip(names, _exchange_wait(h, grad_x, f"scatter_wait_l{l}_{st}")):
            gsum[n][l] = _sum8(r, f"sum_{n}_l{l}")
    gsum = {n: jnp.stack(parts) for n, parts in gsum.items()}
    gw = {}
    for n in ("ffn1_w_gate", "ffn1_w_up", "ffn2_w_gate", "ffn2_w_up"):
        gw[n] = gsum[n][:, :, :fs]
    for n in ("ffn1_w_down", "ffn2_w_down"):
        gw[n] = gsum[n][:, :fs, :]
    gw["w_out"] = gsum["w_out"]
    for i, n in enumerate(BRANCH_W):
        gw[n] = gsum["w_br"][:, i]
    gw["w_in"] = jnp.concatenate([gsum["w_conf"], gsum["w_sc"], gsum["w_swa"],
                                  gsum["w_fox"][..., :IN_FF[1] - IN_FOX[0]], gsum["w_zg"]], axis=-1)

    def small_partial(n):
        per_layer = [grads[l][n] for l in range(L)]
        if n == "rel_bias":
            return sum(pl_[:, :4] for pl_ in per_layer)
        if n in ("swa_sink", "b_forget"):
            return jnp.stack([a.reshape(-1)[:4] for a in per_layer])
        if n in ("swa_q_norm", "fox_q_norm", "fox_k_norm"):
            return jnp.stack([a.reshape(4, HEAD).sum(0) for a in per_layer])
        if n == "swa_k_norm":
            return jnp.stack([a.reshape(2, HEAD).sum(0) for a in per_layer])
        if n == "conf_dw":
            return jnp.stack([a[:CONV_K] for a in per_layer])
        if n == "sc_conv":
            return jnp.stack([a[:SC_K] for a in per_layer])
        return jnp.stack([a.reshape(-1) for a in per_layer])

    partial = [small_partial(n) for n in SMALL]
    small_shapes = [a.shape for a in partial]
    all_parts = _exchange([(_pack(partial), "gather", 0)], "gather_small_grads")[0]
    rows = all_parts.shape[0] // N_DEV
    small_sum = _unpack(_sum8(all_parts.reshape(N_DEV, rows, LANE), "sum_small"), small_shapes)
    for n, a in zip(SMALL, small_sum):
        if n in ("conf_dw", "sc_conv"):
            cs = w[n].shape[2]
            a = lax.dynamic_slice_in_dim(a, dev * cs, cs, axis=2)
        gw[n] = a

    delta, new_m, new_v = {}, {}, {}
    for n in W_NAMES:
        if n not in SMALL:
            delta[n], new_m[n], new_v[n] = _adamw(w[n], gw[n], m[n], v[n], f"adamw_{n}")
    shapes = [w[n].shape for n in SMALL]
    outs = _adamw(_pack([w[n] for n in SMALL]), _pack([gw[n] for n in SMALL]), _pack([m[n] for n in SMALL]),
                  _pack([v[n] for n in SMALL]), "adamw_small")
    for res, out in zip((delta, new_m, new_v), outs):
        for n, a in zip(SMALL, _unpack(out, shapes)):
            res[n] = a

    loss = lax.psum(loss_part[0, 0], ("x", "y", "c"))
    return loss, grad_x, gw, delta, new_m, new_v


def kernel(x, rel_bias, ffn1_norm, ffn1_w_gate, ffn1_w_up, ffn1_w_down, mix_norm, w_in, b_forget, conf_dw, conf_dw_b, conf_ln_g, conf_ln_b, conf_w_out, sc_conv, sc_w_out, swa_q_norm, swa_k_norm, swa_sink, swa_w_o, fox_q_norm, fox_k_norm, fox_w_o, w_out, ffn2_norm, ffn2_w_gate, ffn2_w_up, ffn2_w_down, loss_target, m_rel_bias, m_ffn1_norm, m_ffn1_w_gate, m_ffn1_w_up, m_ffn1_w_down, m_mix_norm, m_w_in, m_b_forget, m_conf_dw, m_conf_dw_b, m_conf_ln_g, m_conf_ln_b, m_conf_w_out, m_sc_conv, m_sc_w_out, m_swa_q_norm, m_swa_k_norm, m_swa_sink, m_swa_w_o, m_fox_q_norm, m_fox_k_norm, m_fox_w_o, m_w_out, m_ffn2_norm, m_ffn2_w_gate, m_ffn2_w_up, m_ffn2_w_down, v_rel_bias, v_ffn1_norm, v_ffn1_w_gate, v_ffn1_w_up, v_ffn1_w_down, v_mix_norm, v_w_in, v_b_forget, v_conf_dw, v_conf_dw_b, v_conf_ln_g, v_conf_ln_b, v_conf_w_out, v_sc_conv, v_sc_w_out, v_swa_q_norm, v_swa_k_norm, v_swa_sink, v_swa_w_o, v_fox_q_norm, v_fox_k_norm, v_fox_w_o, v_w_out, v_ffn2_norm, v_ffn2_w_gate, v_ffn2_w_up, v_ffn2_w_down):
    args = locals()
    w = {n: args[n] for n in W_NAMES}
    m = {n: args["m_" + n] for n in W_NAMES}
    v = {n: args["v_" + n] for n in W_NAMES}
    T, D = x.shape[-2:]
    loss, grad_x, gw, delta, new_m, new_v = _step(w, m, v, x.reshape(T, D), loss_target.reshape(T, D))
    return (loss, grad_x.reshape(x.shape), *[gw[n] for n in W_NAMES], *[delta[n] for n in W_NAMES],
            *[new_m[n] for n in W_NAMES], *[new_v[n] for n in W_NAMES])
```

```python
import math

import numpy as np
import jax
import jax.numpy as jnp
from jax import lax
from jax.experimental import pallas as pl
from jax.experimental.pallas import tpu as pltpu

F32 = jnp.float32
_MXU = jnp.bfloat16
EPS = 1e-6
NEG_INF = -1e30
HEAD = 64
CH = 256
WINDOW = 128
CONV_K = 31
SC_K = 3
CONV_HALO = 32
SC_HALO = 8
N_BUCKETS = 32
MAX_DISTANCE = 128
N_DEV = 8
LANE = 128
ROW_TILE = 512
VMEM_LIMIT = 48 * 1024 * 1024
ADAM_LR, ADAM_B1, ADAM_B2, ADAM_EPS, ADAM_WD, ADAM_STEP = 0.001, 0.9, 0.999, 1e-08, 0.01, 10

_NN = (((1,), (0,)), ((), ()))
_NT = (((1,), (1,)), ((), ()))
_TN = (((0,), (0,)), ((), ()))


def _cp(*sem):
    return pltpu.CompilerParams(dimension_semantics=sem, vmem_limit_bytes=VMEM_LIMIT)


def _tile(n, pref, align=LANE):
    t = (min(n, pref) // align) * align
    while t >= align:
        if n % t == 0:
            return t
        t -= align
    return n


def _dot(a, b, dims=_NN):
    return lax.dot_general(a.astype(_MXU), b.astype(_MXU), dims, preferred_element_type=F32)


def _split3(x):
    hi = x.astype(_MXU)
    r1 = x - hi.astype(F32)
    mid = r1.astype(_MXU)
    lo = (r1 - mid.astype(F32)).astype(_MXU)
    return hi, mid, lo


def _exact_dot(a, b, dims, data):
    if data == "a":
        return sum(lax.dot_general(p, b.astype(_MXU), dims, preferred_element_type=F32) for p in _split3(a))
    return sum(lax.dot_general(a.astype(_MXU), p, dims, preferred_element_type=F32) for p in _split3(b))


def _sigmoid(x):
    return jax.nn.sigmoid(x)


def _lane_mask(width, h):
    lane = lax.broadcasted_iota(jnp.int32, (1, width), 1)
    return (lane >= h * HEAD) & (lane < (h + 1) * HEAD)


def _head_rms(x, g, nh):
    xx = x * x
    ms = jnp.zeros_like(x)
    for h in range(nh):
        mk = _lane_mask(x.shape[-1], h)
        s = jnp.sum(jnp.where(mk, xx, 0.0), axis=-1, keepdims=True) * (1.0 / HEAD)
        ms = jnp.where(mk, s, ms)
    r = lax.rsqrt(ms + EPS)
    return x * r * g, r


def _head_rms_bwd(dy, x, r, g, nh):
    w = dy * g
    wx = w * x
    c = jnp.zeros_like(x)
    for h in range(nh):
        mk = _lane_mask(x.shape[-1], h)
        s = jnp.sum(jnp.where(mk, wx, 0.0), axis=-1, keepdims=True) * (1.0 / HEAD)
        c = jnp.where(mk, s, c)
    dx = r * w - x * (r * r * r) * c
    dg = jnp.sum(dy * x * r, axis=0, keepdims=True)
    return dx, dg


def _mm(pairs, mode, out_dtype, name, scale=None, res=None, tm=1024, tn=1024, tk=1024, after=None):
    a0, b0 = pairs[0]
    M = a0.shape[1] if mode == "tn" else a0.shape[0]
    N = b0.shape[0] if mode == "nt" else b0.shape[1]
    tm, tn = _tile(M, tm), _tile(N, tn)
    dims = {"nn": _NN, "nt": _NT, "tn": _TN}[mode]
    tks, nks, offs = [], [], []
    for a, _ in pairs:
        K = a.shape[0] if mode == "tn" else a.shape[1]
        t = _tile(K, tk)
        tks.append(t)
        nks.append(K // t)
        offs.append(sum(nks[:-1]))
    nk_tot = sum(nks)
    in_specs, args = [], []
    for (a, b), t, nk, off in zip(pairs, tks, nks, offs):
        def kk(k, off=off, nk=nk):
            return jnp.clip(k - off, 0, nk - 1)
        if mode == "tn":
            in_specs.append(pl.BlockSpec((t, tm), lambda i, j, k, kk=kk: (kk(k), i)))
        else:
            in_specs.append(pl.BlockSpec((tm, t), lambda i, j, k, kk=kk: (i, kk(k))))
        if mode == "nt":
            in_specs.append(pl.BlockSpec((tn, t), lambda i, j, k, kk=kk: (j, kk(k))))
        else:
            in_specs.append(pl.BlockSpec((t, tn), lambda i, j, k, kk=kk: (kk(k), j)))
        args += [a, b]
    if res is not None:
        in_specs.append(pl.BlockSpec((tm, tn), lambda i, j, k: (i, j)))
        args.append(res)
    if after is not None:
        in_specs.append(pl.BlockSpec(memory_space=pl.ANY))
        args.append(after)
    npairs = len(pairs)

    def body(*refs):
        ab = refs[:2 * npairs]
        res_ref = refs[2 * npairs] if res is not None else None
        o_ref = refs[2 * npairs + (res is not None) + (after is not None)]
        acc = refs[-1]
        k = pl.program_id(2)

        def finish(r):
            if scale is not None:
                r = r * scale
            if res_ref is not None:
                r = r + res_ref[...]
            o_ref[...] = r.astype(o_ref.dtype)

        if nk_tot == 1:
            finish(_dot(ab[0][...], ab[1][...], dims))
            return

        @pl.when(k == 0)
        def _():
            acc[...] = jnp.zeros_like(acc)

        for p in range(npairs):
            @pl.when(jnp.logical_and(k >= offs[p], k < offs[p] + nks[p]))
            def _(p=p):
                acc[...] += _dot(ab[2 * p][...], ab[2 * p + 1][...], dims)

        @pl.when(k == nk_tot - 1)
        def _():
            finish(acc[...])

    return pl.pallas_call(
        body, name=name, grid=(M // tm, N // tn, nk_tot), in_specs=in_specs,
        out_specs=pl.BlockSpec((tm, tn), lambda i, j, k: (i, j)),
        out_shape=jax.ShapeDtypeStruct((M, N), out_dtype),
        scratch_shapes=[pltpu.VMEM((tm, tn), F32)],
        compiler_params=_cp("parallel", "parallel", "arbitrary"))(*args)


def _rms_fwd(x, g, name):
    T, D = x.shape
    tm = _tile(T, ROW_TILE)

    def body(x_ref, g_ref, o_ref):
        xv = x_ref[...]
        r = lax.rsqrt(jnp.mean(xv * xv, axis=-1, keepdims=True) + EPS)
        o_ref[...] = (xv * r * g_ref[...]).astype(o_ref.dtype)

    return pl.pallas_call(
        body, name=name, grid=(T // tm,),
        in_specs=[pl.BlockSpec((tm, D), lambda i: (i, 0)), pl.BlockSpec((1, D), lambda i: (0, 0))],
        out_specs=pl.BlockSpec((tm, D), lambda i: (i, 0)),
        out_shape=jax.ShapeDtypeStruct((T, D), _MXU), compiler_params=_cp("parallel"))(x, g)


def _rms_bwd(da, x, g, dres, name):
    T, D = x.shape
    tm = _tile(T, ROW_TILE)

    def body(da_ref, x_ref, g_ref, dr_ref, dx_ref, dg_ref):
        @pl.when(pl.program_id(0) == 0)
        def _():
            dg_ref[...] = jnp.zeros_like(dg_ref)

        xv, dav = x_ref[...], da_ref[...]
        r = lax.rsqrt(jnp.mean(xv * xv, axis=-1, keepdims=True) + EPS)
        w = dav * g_ref[...]
        c = jnp.mean(w * xv, axis=-1, keepdims=True)
        dx_ref[...] = dr_ref[...] + (r * w - xv * (r * r * r) * c)
        dg_ref[...] += jnp.sum(dav * xv * r, axis=0, keepdims=True)

    row = pl.BlockSpec((tm, D), lambda i: (i, 0))
    vec = pl.BlockSpec((1, D), lambda i: (0, 0))
    return pl.pallas_call(
        body, name=name, grid=(T // tm,), in_specs=[row, row, vec, row], out_specs=[row, vec],
        out_shape=[jax.ShapeDtypeStruct((T, D), F32), jax.ShapeDtypeStruct((1, D), F32)],
        compiler_params=_cp("arbitrary"))(da, x, g, dres)


def _loss_grad(y, tgt):
    T, D = y.shape
    tm = _tile(T, ROW_TILE)

    def body(y_ref, t_ref, dy_ref, l_ref):
        @pl.when(pl.program_id(0) == 0)
        def _():
            l_ref[...] = jnp.zeros_like(l_ref)

        d = y_ref[...] - t_ref[...]
        dy_ref[...] = d * (1.0 / D)
        per_tok = jnp.mean(d * d, axis=-1, keepdims=True)
        l_ref[...] += 0.5 * jnp.sum(per_tok, axis=0, keepdims=True)

    row = pl.BlockSpec((tm, D), lambda i: (i, 0))
    return pl.pallas_call(
        body, name="loss_grad", grid=(T // tm,), in_specs=[row, row],
        out_specs=[row, pl.BlockSpec((1, 1), lambda i: (0, 0))],
        out_shape=[jax.ShapeDtypeStruct((T, D), F32), jax.ShapeDtypeStruct((1, 1), F32)],
        compiler_params=_cp("arbitrary"))(y, tgt)


def _ffn_up(a, wg, wu, name):
    T, D = a.shape
    Fp = wg.shape[1]
    tm, tn = _tile(T, ROW_TILE), _tile(Fp, 768)

    def body(a_ref, wg_ref, wu_ref, g_ref, u_ref, h_ref):
        av = a_ref[...]
        g = _dot(av, wg_ref[...])
        u = _dot(av, wu_ref[...])
        g_ref[...] = g.astype(g_ref.dtype)
        u_ref[...] = u.astype(u_ref.dtype)
        h_ref[...] = (g * _sigmoid(g) * u).astype(h_ref.dtype)

    wspec = pl.BlockSpec((D, tn), lambda j, i: (0, j))
    ospec = pl.BlockSpec((tm, tn), lambda j, i: (i, j))
    osh = jax.ShapeDtypeStruct((T, Fp), _MXU)
    return pl.pallas_call(
        body, name=name, grid=(Fp // tn, T // tm),
        in_specs=[pl.BlockSpec((tm, D), lambda j, i: (i, 0)), wspec, wspec],
        out_specs=[ospec, ospec, ospec], out_shape=[osh, osh, osh],
        compiler_params=_cp("parallel", "parallel"))(a, wg, wu)


def _ffn_bwd_hid(dxp, wd, gate, up, name, after=None):
    T, D = dxp.shape
    Fp = wd.shape[0]
    tm, tn = _tile(T, ROW_TILE), _tile(Fp, 768)
    extra = [] if after is None else [after]

    def body(*refs):
        dx_ref, wd_ref, g_ref, u_ref = refs[:4]
        dg_ref, du_ref = refs[-2:]
        dh = 0.5 * _dot(dx_ref[...], wd_ref[...], _NT)
        g = g_ref[...].astype(F32)
        u = u_ref[...].astype(F32)
        s = _sigmoid(g)
        du_ref[...] = (dh * (g * s)).astype(du_ref.dtype)
        dg_ref[...] = (dh * u * (s * (1.0 + g * (1.0 - s)))).astype(dg_ref.dtype)

    tspec = pl.BlockSpec((tm, tn), lambda j, i: (i, j))
    osh = jax.ShapeDtypeStruct((T, Fp), _MXU)
    return pl.pallas_call(
        body, name=name, grid=(Fp // tn, T // tm),
        in_specs=[pl.BlockSpec((tm, D), lambda j, i: (i, 0)), pl.BlockSpec((tn, D), lambda j, i: (j, 0)), tspec, tspec]
        + [pl.BlockSpec(memory_space=pl.ANY)] * len(extra),
        out_specs=[tspec, tspec], out_shape=[osh, osh],
        compiler_params=_cp("parallel", "parallel"))(dxp, wd, gate, up, *extra)


def _conf_fwd(zc, dw, b, lng, lnb, name):
    T = zc.shape[0]
    tm = _tile(T, ROW_TILE)
    r = tm // CONV_HALO

    def body(z_ref, zh_ref, dw_ref, b_ref, g_ref, lb_ref, u1_ref, act_ref, ext):
        i = pl.program_id(0)
        cur = z_ref[...]
        ext[pl.ds(CONV_HALO, tm), :] = cur[:, :CH] * _sigmoid(cur[:, CH:])
        hal = zh_ref[...]
        ext[pl.ds(0, CONV_HALO), :] = jnp.where(i > 0, hal[:, :CH] * _sigmoid(hal[:, CH:]), 0.0)
        acc = jnp.zeros((tm, CH), F32)
        for k in range(CONV_K):
            acc = acc + dw_ref[pl.ds(k, 1), :] * ext[pl.ds(CONV_HALO - (CONV_K - 1) + k, tm), :]
        u1 = acc + b_ref[...]
        u1_ref[...] = u1
        mu = jnp.mean(u1, axis=-1, keepdims=True)
        var = jnp.mean(jnp.square(u1 - mu), axis=-1, keepdims=True)
        u2 = (u1 - mu) * lax.rsqrt(var + EPS) * g_ref[...] + lb_ref[...]
        act_ref[...] = u2 * _sigmoid(u2)

    vec = pl.BlockSpec((1, CH), lambda i: (0, 0))
    row = pl.BlockSpec((tm, CH), lambda i: (i, 0))
    osh = jax.ShapeDtypeStruct((T, CH), F32)
    return pl.pallas_call(
        body, name=name, grid=(T // tm,),
        in_specs=[pl.BlockSpec((tm, 2 * CH), lambda i: (i, 0)),
                  pl.BlockSpec((CONV_HALO, 2 * CH), lambda i: (jnp.maximum(i * r - 1, 0), 0)),
                  pl.BlockSpec((CONV_HALO, CH), lambda i: (0, 0)), vec, vec, vec],
        out_specs=[row, row], out_shape=[osh, osh],
        scratch_shapes=[pltpu.VMEM((tm + CONV_HALO, CH), F32)],
        compiler_params=_cp("parallel"))(zc, zc, dw, b, lng, lnb)


def _conf_bwd_ln(dact, u1, lng, lnb, name):
    T = u1.shape[0]
    tm = _tile(T, ROW_TILE)

    def body(da_ref, u_ref, g_ref, lb_ref, du_ref, sm_ref):
        @pl.when(pl.program_id(0) == 0)
        def _():
            sm_ref[...] = jnp.zeros_like(sm_ref)

        u1v = u_ref[...]
        mu = jnp.mean(u1v, axis=-1, keepdims=True)
        cen = u1v - mu
        rstd = lax.rsqrt(jnp.mean(cen * cen, axis=-1, keepdims=True) + EPS)
        y = cen * rstd
        u2 = y * g_ref[...] + lb_ref[...]
        s = _sigmoid(u2)
        du2 = da_ref[...] * (s * (1.0 + u2 * (1.0 - s)))
        dy = du2 * g_ref[...]
        du1 = rstd * (dy - jnp.mean(dy, axis=-1, keepdims=True) - y * jnp.mean(dy * y, axis=-1, keepdims=True))
        du_ref[...] = du1
        sm_ref[pl.ds(0, 1), :] += jnp.sum(du2 * y, axis=0, keepdims=True)
        sm_ref[pl.ds(1, 1), :] += jnp.sum(du2, axis=0, keepdims=True)
        sm_ref[pl.ds(2, 1), :] += jnp.sum(du1, axis=0, keepdims=True)

    vec = pl.BlockSpec((1, CH), lambda i: (0, 0))
    row = pl.BlockSpec((tm, CH), lambda i: (i, 0))
    return pl.pallas_call(
        body, name=name, grid=(T // tm,), in_specs=[row, row, vec, vec],
        out_specs=[row, pl.BlockSpec((8, CH), lambda i: (0, 0))],
        out_shape=[jax.ShapeDtypeStruct((T, CH), F32), jax.ShapeDtypeStruct((8, CH), F32)],
        compiler_params=_cp("arbitrary"))(dact, u1, lng, lnb)


def _conf_bwd_conv(zc, du1, dw, name):
    T = zc.shape[0]
    tm = _tile(T, ROW_TILE)
    r = tm // CONV_HALO
    nt = T // tm
    nh = T // CONV_HALO

    def body(z_ref, zh_ref, d_ref, dn_ref, dw_ref, dz_ref, ddw_ref, ext_u, ext_d):
        i = pl.program_id(0)

        @pl.when(i == 0)
        def _():
            ddw_ref[...] = jnp.zeros_like(ddw_ref)

        cur = z_ref[...]
        ca = cur[:, :CH]
        sg = _sigmoid(cur[:, CH:])
        ext_u[pl.ds(CONV_HALO, tm), :] = ca * sg
        hal = zh_ref[...]
        ext_u[pl.ds(0, CONV_HALO), :] = jnp.where(i > 0, hal[:, :CH] * _sigmoid(hal[:, CH:]), 0.0)
        d = d_ref[...]
        ext_d[pl.ds(0, tm), :] = d
        ext_d[pl.ds(tm, CONV_HALO), :] = jnp.where(i < nt - 1, dn_ref[...], 0.0)
        acc = jnp.zeros((tm, CH), F32)
        for k in range(CONV_K):
            acc = acc + dw_ref[pl.ds(k, 1), :] * ext_d[pl.ds(CONV_K - 1 - k, tm), :]
            ddw_ref[pl.ds(k, 1), :] += jnp.sum(
                d * ext_u[pl.ds(CONV_HALO - (CONV_K - 1) + k, tm), :], axis=0, keepdims=True)
        dz_ref[:, :CH] = (acc * sg).astype(dz_ref.dtype)
        dz_ref[:, CH:] = (acc * ca * sg * (1.0 - sg)).astype(dz_ref.dtype)

    return pl.pallas_call(
        body, name=name, grid=(nt,),
        in_specs=[pl.BlockSpec((tm, 2 * CH), lambda i: (i, 0)),
                  pl.BlockSpec((CONV_HALO, 2 * CH), lambda i: (jnp.maximum(i * r - 1, 0), 0)),
                  pl.BlockSpec((tm, CH), lambda i: (i, 0)),
                  pl.BlockSpec((CONV_HALO, CH), lambda i: (jnp.minimum((i + 1) * r, nh - 1), 0)),
                  pl.BlockSpec((CONV_HALO, CH), lambda i: (0, 0))],
        out_specs=[pl.BlockSpec((tm, 2 * CH), lambda i: (i, 0)), pl.BlockSpec((CONV_HALO, CH), lambda i: (0, 0))],
        out_shape=[jax.ShapeDtypeStruct((T, 2 * CH), _MXU), jax.ShapeDtypeStruct((CONV_HALO, CH), F32)],
        scratch_shapes=[pltpu.VMEM((tm + CONV_HALO, CH), F32), pltpu.VMEM((tm + CONV_HALO, CH), F32)],
        compiler_params=_cp("arbitrary"))(zc, zc, du1, du1, dw)


def _sc_fwd(zs, w, name):
    T = zs.shape[0]
    tm = _tile(T, ROW_TILE)
    r = tm // SC_HALO

    def body(z_ref, zh_ref, w_ref, act_ref, ext):
        i = pl.program_id(0)
        cur = z_ref[...]
        ext[pl.ds(SC_HALO, tm), :] = cur[:, CH:2 * CH] * cur[:, 2 * CH:]
        hal = zh_ref[...]
        ext[pl.ds(0, SC_HALO), :] = jnp.where(i > 0, hal[:, CH:2 * CH] * hal[:, 2 * CH:], 0.0)
        v1 = jnp.zeros((tm, CH), F32)
        for k in range(SC_K):
            v1 = v1 + w_ref[pl.ds(k, 1), :] * ext[pl.ds(SC_HALO - (SC_K - 1) + k, tm), :]
        act_ref[...] = cur[:, :CH] * v1

    return pl.pallas_call(
        body, name=name, grid=(T // tm,),
        in_specs=[pl.BlockSpec((tm, 3 * CH), lambda i: (i, 0)),
                  pl.BlockSpec((SC_HALO, 3 * CH), lambda i: (jnp.maximum(i * r - 1, 0), 0)),
                  pl.BlockSpec((SC_HALO, CH), lambda i: (0, 0))],
        out_specs=pl.BlockSpec((tm, CH), lambda i: (i, 0)),
        out_shape=jax.ShapeDtypeStruct((T, CH), F32),
        scratch_shapes=[pltpu.VMEM((tm + SC_HALO, CH), F32)],
        compiler_params=_cp("parallel"))(zs, zs, w)


def _sc_bwd(zs, dact, w, name):
    T = zs.shape[0]
    tm = _tile(T, ROW_TILE)
    r = tm // SC_HALO
    nt = T // tm
    nh = T // SC_HALO

    def body(z_ref, zh_ref, zn_ref, d_ref, dn_ref, w_ref, dz_ref, dw_ref, ext_v, ext_d):
        i = pl.program_id(0)

        @pl.when(i == 0)
        def _():
            dw_ref[...] = jnp.zeros_like(dw_ref)

        cur = z_ref[...]
        sb, sc, sx = cur[:, :CH], cur[:, CH:2 * CH], cur[:, 2 * CH:]
        ext_v[pl.ds(SC_HALO, tm), :] = sc * sx
        hal = zh_ref[...]
        ext_v[pl.ds(0, SC_HALO), :] = jnp.where(i > 0, hal[:, CH:2 * CH] * hal[:, 2 * CH:], 0.0)
        da = d_ref[...]
        dv1 = da * sb
        ext_d[pl.ds(0, tm), :] = dv1
        ext_d[pl.ds(tm, SC_HALO), :] = jnp.where(i < nt - 1, dn_ref[...] * zn_ref[...][:, :CH], 0.0)
        v1 = jnp.zeros((tm, CH), F32)
        dv0 = jnp.zeros((tm, CH), F32)
        for k in range(SC_K):
            shifted = ext_v[pl.ds(SC_HALO - (SC_K - 1) + k, tm), :]
            v1 = v1 + w_ref[pl.ds(k, 1), :] * shifted
            dv0 = dv0 + w_ref[pl.ds(k, 1), :] * ext_d[pl.ds(SC_K - 1 - k, tm), :]
            dw_ref[pl.ds(k, 1), :] += jnp.sum(dv1 * shifted, axis=0, keepdims=True)
        dz_ref[:, :CH] = (da * v1).astype(dz_ref.dtype)
        dz_ref[:, CH:2 * CH] = (dv0 * sx).astype(dz_ref.dtype)
        dz_ref[:, 2 * CH:] = (dv0 * sc).astype(dz_ref.dtype)

    return pl.pallas_call(
        body, name=name, grid=(nt,),
        in_specs=[pl.BlockSpec((tm, 3 * CH), lambda i: (i, 0)),
                  pl.BlockSpec((SC_HALO, 3 * CH), lambda i: (jnp.maximum(i * r - 1, 0), 0)),
                  pl.BlockSpec((SC_HALO, 3 * CH), lambda i: (jnp.minimum((i + 1) * r, nh - 1), 0)),
                  pl.BlockSpec((tm, CH), lambda i: (i, 0)),
                  pl.BlockSpec((SC_HALO, CH), lambda i: (jnp.minimum((i + 1) * r, nh - 1), 0)),
                  pl.BlockSpec((SC_HALO, CH), lambda i: (0, 0))],
        out_specs=[pl.BlockSpec((tm, 3 * CH), lambda i: (i, 0)), pl.BlockSpec((SC_HALO, CH), lambda i: (0, 0))],
        out_shape=[jax.ShapeDtypeStruct((T, 3 * CH), _MXU), jax.ShapeDtypeStruct((SC_HALO, CH), F32)],
        scratch_shapes=[pltpu.VMEM((tm + SC_HALO, CH), F32), pltpu.VMEM((tm + SC_HALO, CH), F32)],
        compiler_params=_cp("arbitrary"))(zs, zs, zs, dact, dact, w)


SWA_TQ = 256


def _t5_bucket_np(dist):
    max_exact = N_BUCKETS // 2
    d = np.maximum(dist, 1).astype(np.float32)
    large = max_exact + (np.log(d / np.float32(max_exact)) / np.float32(math.log(MAX_DISTANCE / max_exact))
                         * np.float32(N_BUCKETS - max_exact)).astype(np.int32)
    large = np.minimum(large, N_BUCKETS - 1)
    return np.where(dist < max_exact, dist, large).astype(np.int32)


def _swa_bucket_matrix(tq):
    dist = WINDOW + np.arange(tq)[:, None] - np.arange(tq + WINDOW)[None, :]
    ok = (dist >= 0) & (dist < WINDOW)
    return np.where(ok, _t5_bucket_np(np.maximum(dist, 0)), -1).astype(np.int32)


def _kv_expand_matrix():
    e = np.zeros((2 * HEAD, 4 * HEAD), np.float32)
    for h in range(4):
        for d in range(HEAD):
            e[(h // 2) * HEAD + d, h * HEAD + d] = 1.0
    return e


def _swa_bias(rel_bias, bucket, name):
    tq, tk = bucket.shape

    def body(rb_ref, bk_ref, o_ref):
        h = pl.program_id(0)
        bk = bk_ref[...]
        acc = jnp.full((tq, tk), NEG_INF, F32)
        for b in range(N_BUCKETS):
            acc = jnp.where(bk == b, rb_ref[b, h], acc)
        o_ref[0] = acc

    return pl.pallas_call(
        body, name=name, grid=(4,),
        in_specs=[pl.BlockSpec(memory_space=pltpu.SMEM), pl.BlockSpec((tq, tk), lambda h: (0, 0))],
        out_specs=pl.BlockSpec((1, tq, tk), lambda h: (h, 0, 0)),
        out_shape=jax.ShapeDtypeStruct((4, tq, tk), F32), compiler_params=_cp("parallel"))(rel_bias, bucket)


def _swa_probs(qh, kx, bm, first_col, sk):
    s = _dot(qh, kx, _NT) * (HEAD ** -0.5)
    col = lax.broadcasted_iota(jnp.int32, s.shape, 1)
    valid = (bm > 0.5 * NEG_INF) & (col >= first_col)
    s = jnp.where(valid, s + bm, NEG_INF)
    m = jnp.maximum(jnp.max(s, axis=-1, keepdims=True), sk)
    p = jnp.exp(s - m)
    den = jnp.sum(p, axis=-1, keepdims=True) + jnp.exp(sk - m)
    return p / den, m, den


def _swa_fwd(zw, gq, gk, sink, bias, expand, name):
    T = zw.shape[0]
    tq = bias.shape[1]
    r = tq // WINDOW

    def body(z_ref, zh_ref, gq_ref, gk_ref, sink_ref, b_ref, e_ref, o_ref, kext, vext):
        i = pl.program_id(0)
        cur = z_ref[...]
        qn, _ = _head_rms(cur[:, :4 * HEAD], gq_ref[...], 4)
        kc, _ = _head_rms(cur[:, 4 * HEAD:6 * HEAD], gk_ref[...], 2)
        hal = zh_ref[...]
        kp, _ = _head_rms(hal[:, :2 * HEAD], gk_ref[...], 2)
        kext[pl.ds(0, WINDOW), :] = kp
        kext[pl.ds(WINDOW, tq), :] = kc
        vext[pl.ds(0, WINDOW), :] = hal[:, 2 * HEAD:]
        vext[pl.ds(WINDOW, tq), :] = cur[:, 6 * HEAD:]
        kx = _dot(kext[...], e_ref[...]).astype(_MXU)
        vx = _dot(vext[...], e_ref[...]).astype(_MXU)
        first_col = jnp.where(i > 0, 0, WINDOW)
        out = jnp.zeros((tq, 4 * HEAD), F32)
        for h in range(4):
            mk = _lane_mask(4 * HEAD, h)
            qh = jnp.where(mk, qn, 0.0)
            pn, _, _ = _swa_probs(qh, kx, b_ref[h], first_col, sink_ref[0, h])
            out = jnp.where(mk, _dot(pn, vx), out)
        o_ref[...] = out

    return pl.pallas_call(
        body, name=name, grid=(T // tq,),
        in_specs=[pl.BlockSpec((tq, 8 * HEAD), lambda i: (i, 0)),
                  pl.BlockSpec((WINDOW, 4 * HEAD), lambda i: (jnp.maximum(i * r - 1, 0), 1)),
                  pl.BlockSpec((1, 4 * HEAD), lambda i: (0, 0)), pl.BlockSpec((1, 2 * HEAD), lambda i: (0, 0)),
                  pl.BlockSpec(memory_space=pltpu.SMEM),
                  pl.BlockSpec(bias.shape, lambda i: (0, 0, 0)),
                  pl.BlockSpec(expand.shape, lambda i: (0, 0))],
        out_specs=pl.BlockSpec((tq, 4 * HEAD), lambda i: (i, 0)),
        out_shape=jax.ShapeDtypeStruct((T, 4 * HEAD), F32),
        scratch_shapes=[pltpu.VMEM((tq + WINDOW, 2 * HEAD), F32), pltpu.VMEM((tq + WINDOW, 2 * HEAD), F32)],
        compiler_params=_cp("parallel"))(zw, zw, gq, gk, sink, bias, expand)


def _swa_bwd(zw, dact, gq, gk, sink, bias, bucket, expand, name):
    T = zw.shape[0]
    tq = bias.shape[1]
    tk = tq + WINDOW
    r = tq // WINDOW
    nt = T // tq
    nb = T // WINDOW
    scale = HEAD ** -0.5

    def body(z_ref, zh_ref, zn_ref, d_ref, dn_ref, gq_ref, gk_ref, sink_ref, b_ref, bk_ref, e_ref,
             dz_ref, dgq_ref, dgk_ref, dsk_ref, drb_ref, kext, vext, dk_s, dv_s, db_s):
        i = pl.program_id(0)

        @pl.when(i == 0)
        def _():
            dgq_ref[...] = jnp.zeros_like(dgq_ref)
            dgk_ref[...] = jnp.zeros_like(dgk_ref)
            dsk_ref[...] = jnp.zeros_like(dsk_ref)
            drb_ref[...] = jnp.zeros_like(drb_ref)
            db_s[...] = jnp.zeros_like(db_s)

        lane = lax.broadcasted_iota(jnp.int32, (1, LANE), 1)
        cur = z_ref[...]
        q_raw, k_raw = cur[:, :4 * HEAD], cur[:, 4 * HEAD:6 * HEAD]
        qn, q_r = _head_rms(q_raw, gq_ref[...], 4)
        kc, k_r = _head_rms(k_raw, gk_ref[...], 2)
        hal = zh_ref[...]
        kp, _ = _head_rms(hal[:, :2 * HEAD], gk_ref[...], 2)
        kext[pl.ds(0, WINDOW), :] = kp
        kext[pl.ds(WINDOW, tq), :] = kc
        vext[pl.ds(0, WINDOW), :] = hal[:, 2 * HEAD:]
        vext[pl.ds(WINDOW, tq), :] = cur[:, 6 * HEAD:]
        ev = e_ref[...]
        kx = _dot(kext[...], ev).astype(_MXU)
        vx = _dot(vext[...], ev).astype(_MXU)
        first_col = jnp.where(i > 0, 0, WINDOW)
        do = d_ref[...]
        dq = jnp.zeros((tq, 4 * HEAD), F32)
        dkx = jnp.zeros((tk, 4 * HEAD), F32)
        dvx = jnp.zeros((tk, 4 * HEAD), F32)
        dsk = jnp.zeros((1, LANE), F32)
        for h in range(4):
            mk = _lane_mask(4 * HEAD, h)
            qh = jnp.where(mk, qn, 0.0).astype(_MXU)
            sk = sink_ref[0, h]
            pn, m, den = _swa_probs(qh, kx, b_ref[h], first_col, sk)
            doh = jnp.where(mk, do, 0.0).astype(_MXU)
            dpn = _dot(doh, vx, _NT)
            delta = jnp.sum(pn * dpn, axis=-1, keepdims=True)
            ds = pn * (dpn - delta)
            psink = jnp.exp(sk - m) / den
            dsk = dsk + jnp.where(lane == h, jnp.sum(-psink * delta, axis=0, keepdims=True), 0.0)
            db_s[h] += ds
            dss = (ds * scale).astype(_MXU)
            dq = dq + jnp.where(mk, _dot(dss, kx), 0.0)
            dkx = dkx + _dot(dss, qh, _TN)
            dvx = dvx + _dot(pn, doh, _TN)
        dsk_ref[...] += dsk
        dk_ext = _exact_dot(dkx, ev, _NT, "a")
        dv_ext = _exact_dot(dvx, ev, _NT, "a")
        dk_s[...] = dk_ext[WINDOW:, :]
        dv_s[...] = dv_ext[WINDOW:, :]

        @pl.when(i < nt - 1)
        def _():
            nxt = zn_ref[...]
            q2, _ = _head_rms(nxt[:, :4 * HEAD], gq_ref[...], 4)
            k2n, _ = _head_rms(nxt[:, 4 * HEAD:6 * HEAD], gk_ref[...], 2)
            k2 = jnp.concatenate([kc[tq - WINDOW:, :], k2n], axis=0)
            v2 = jnp.concatenate([cur[tq - WINDOW:, 6 * HEAD:], nxt[:, 6 * HEAD:]], axis=0)
            k2x = _dot(k2, ev).astype(_MXU)
            v2x = _dot(v2, ev).astype(_MXU)
            do2 = dn_ref[...]
            dk2x = jnp.zeros((2 * WINDOW, 4 * HEAD), F32)
            dv2x = jnp.zeros((2 * WINDOW, 4 * HEAD), F32)
            for h in range(4):
                mk = _lane_mask(4 * HEAD, h)
                qh = jnp.where(mk, q2, 0.0).astype(_MXU)
                pn, _, _ = _swa_probs(qh, k2x, b_ref[h][:WINDOW, :2 * WINDOW], 0, sink_ref[0, h])
                doh = jnp.where(mk, do2, 0.0).astype(_MXU)
                dpn = _dot(doh, v2x, _NT)
                ds = pn * (dpn - jnp.sum(pn * dpn, axis=-1, keepdims=True))
                dk2x = dk2x + _dot((ds * scale).astype(_MXU), qh, _TN)
                dv2x = dv2x + _dot(pn, doh, _TN)
            dk_s[pl.ds(tq - WINDOW, WINDOW), :] += _exact_dot(dk2x, ev, _NT, "a")[:WINDOW, :]
            dv_s[pl.ds(tq - WINDOW, WINDOW), :] += _exact_dot(dv2x, ev, _NT, "a")[:WINDOW, :]

        dq_raw, dgq = _head_rms_bwd(dq, q_raw, q_r, gq_ref[...], 4)
        dk_raw, dgk = _head_rms_bwd(dk_s[...], k_raw, k_r, gk_ref[...], 2)
        dgq_ref[...] += dgq
        dgk_ref[...] += dgk
        dz_ref[:, :4 * HEAD] = dq_raw.astype(dz_ref.dtype)
        dz_ref[:, 4 * HEAD:6 * HEAD] = dk_raw.astype(dz_ref.dtype)
        dz_ref[:, 6 * HEAD:] = dv_s[...].astype(dz_ref.dtype)

        @pl.when(i == nt - 1)
        def _():
            bk = bk_ref[...]
            for b in range(N_BUCKETS):
                rowv = jnp.zeros((1, LANE), F32)
                for h in range(4):
                    s1 = jnp.sum(jnp.where(bk == b, db_s[h], 0.0), axis=0, keepdims=True)
                    rowv = jnp.where(lane == h, jnp.sum(s1, axis=1, keepdims=True), rowv)
                drb_ref[pl.ds(b, 1), :] = rowv

    const2 = lambda i: (0, 0)
    return pl.pallas_call(
        body, name=name, grid=(nt,),
        in_specs=[pl.BlockSpec((tq, 8 * HEAD), lambda i: (i, 0)),
                  pl.BlockSpec((WINDOW, 4 * HEAD), lambda i: (jnp.maximum(i * r - 1, 0), 1)),
                  pl.BlockSpec((WINDOW, 8 * HEAD), lambda i: (jnp.minimum((i + 1) * r, nb - 1), 0)),
                  pl.BlockSpec((tq, 4 * HEAD), lambda i: (i, 0)),
                  pl.BlockSpec((WINDOW, 4 * HEAD), lambda i: (jnp.minimum((i + 1) * r, nb - 1), 0)),
                  pl.BlockSpec((1, 4 * HEAD), const2), pl.BlockSpec((1, 2 * HEAD), const2),
                  pl.BlockSpec(memory_space=pltpu.SMEM),
                  pl.BlockSpec(bias.shape, lambda i: (0, 0, 0)),
                  pl.BlockSpec(bucket.shape, const2), pl.BlockSpec(expand.shape, const2)],
        out_specs=[pl.BlockSpec((tq, 8 * HEAD), lambda i: (i, 0)),
                   pl.BlockSpec((1, 4 * HEAD), const2), pl.BlockSpec((1, 2 * HEAD), const2),
                   pl.BlockSpec((1, LANE), const2), pl.BlockSpec((N_BUCKETS, LANE), const2)],
        out_shape=[jax.ShapeDtypeStruct((T, 8 * HEAD), _MXU), jax.ShapeDtypeStruct((1, 4 * HEAD), F32),
                   jax.ShapeDtypeStruct((1, 2 * HEAD), F32), jax.ShapeDtypeStruct((1, LANE), F32),
                   jax.ShapeDtypeStruct((N_BUCKETS, LANE), F32)],
        scratch_shapes=[pltpu.VMEM((tk, 2 * HEAD), F32), pltpu.VMEM((tk, 2 * HEAD), F32),
                        pltpu.VMEM((tq, 2 * HEAD), F32), pltpu.VMEM((tq, 2 * HEAD), F32),
                        pltpu.VMEM((4, tq, tk), F32)],
        compiler_params=_cp("arbitrary"))(zw, zw, zw, dact, dact, gq, gk, sink, bias, bucket, expand)


FOX_B = 512
FOX_TM = 256


def _tri(n, lower):
    m = np.tril(np.ones((n, n), np.float32)) if lower else np.triu(np.ones((n, n), np.float32))
    return m


def _log_sigmoid(x):
    return jnp.minimum(x, 0.0) - jnp.log1p(jnp.exp(-jnp.abs(x)))


def _fox_prep(zf, gq, gk, bf, name):
    T = zf.shape[0]
    tm = _tile(T, FOX_TM)
    lower = jnp.asarray(_tri(tm, True), _MXU)

    def body(z_ref, gq_ref, gk_ref, bf_ref, l_ref, q_ref, k_ref, v_ref, f_ref, ft_ref, carry):
        @pl.when(pl.program_id(0) == 0)
        def _():
            carry[...] = jnp.zeros_like(carry)

        z = z_ref[...]
        q, _ = _head_rms(z[:, :CH], gq_ref[...], 4)
        k, _ = _head_rms(z[:, CH:2 * CH], gk_ref[...], 4)
        q_ref[...] = q.astype(q_ref.dtype)
        k_ref[...] = k.astype(k_ref.dtype)
        v_ref[...] = z[:, 2 * CH:3 * CH].astype(v_ref.dtype)
        lane = lax.broadcasted_iota(jnp.int32, (1, LANE), 1)
        lf = jnp.where(lane < 4, _log_sigmoid(z[:, 3 * CH:] + bf_ref[...]), 0.0)
        fv = _exact_dot(l_ref[...], lf, _NN, "b") + carry[pl.ds(0, 1), :]
        f_ref[...] = fv
        ft_ref[...] = fv.T
        carry[pl.ds(0, 1), :] = f_ref[pl.ds(tm - 1, 1), :]

    row = pl.BlockSpec((tm, CH), lambda i: (i, 0))
    vec = pl.BlockSpec((1, CH), lambda i: (0, 0))
    qsh = jax.ShapeDtypeStruct((T, CH), _MXU)
    return pl.pallas_call(
        body, name=name, grid=(T // tm,),
        in_specs=[pl.BlockSpec((tm, 3 * CH + LANE), lambda i: (i, 0)), vec, vec,
                  pl.BlockSpec((1, LANE), lambda i: (0, 0)), pl.BlockSpec((tm, tm), lambda i: (0, 0))],
        out_specs=[row, row, row, pl.BlockSpec((tm, LANE), lambda i: (i, 0)), pl.BlockSpec((LANE, tm), lambda i: (0, i))],
        out_shape=[qsh, qsh, qsh, jax.ShapeDtypeStruct((T, LANE), F32), jax.ShapeDtypeStruct((LANE, T), F32)],
        scratch_shapes=[pltpu.VMEM((8, LANE), F32)],
        compiler_params=_cp("arbitrary"))(zf, gq, gk, bf, lower)


def _lane_col(x, h):
    lane = lax.broadcasted_iota(jnp.int32, (1, x.shape[-1]), 1)
    return jnp.sum(jnp.where(lane == h, x, 0.0), axis=-1, keepdims=True)


def _fox_scores(qh, k, fq, ft_ref, h, qi, ki, B):
    s = _dot(qh, k, _NT) * (HEAD ** -0.5)
    s = s + (fq - ft_ref[pl.ds(h, 1), :])
    row = qi * B + lax.broadcasted_iota(jnp.int32, s.shape, 0)
    col = ki * B + lax.broadcasted_iota(jnp.int32, s.shape, 1)
    return jnp.where(col <= row, s, NEG_INF)


def _fox_fwd(q, k, v, f, ft, name):
    T = q.shape[0]
    B = _tile(T, FOX_B)
    n = T // B

    def body(q_ref, k_ref, v_ref, f_ref, ft_ref, o_ref, lse_ref, m_s, l_s, acc):
        qi, ki = pl.program_id(0), pl.program_id(1)

        @pl.when(ki == 0)
        def _():
            m_s[...] = jnp.full_like(m_s, NEG_INF)
            l_s[...] = jnp.zeros_like(l_s)
            acc[...] = jnp.zeros_like(acc)

        @pl.when(ki <= qi)
        def _():
            qv, kv, vv, fv = q_ref[...], k_ref[...], v_ref[...], f_ref[...]
            for h in range(4):
                mk = _lane_mask(CH, h)
                qh = jnp.where(mk, qv, jnp.zeros_like(qv))
                s = _fox_scores(qh, kv, _lane_col(fv, h), ft_ref, h, qi, ki, B)
                m_old = m_s[h]
                m_new = jnp.maximum(m_old, jnp.max(s, axis=-1, keepdims=True))
                alpha = jnp.exp(m_old - m_new)
                p = jnp.exp(s - m_new)
                l_s[h] = alpha * l_s[h] + jnp.sum(p, axis=-1, keepdims=True)
                m_s[h] = m_new
                acc[...] = jnp.where(mk, acc[...] * alpha + _dot(p, vv), acc[...])

        @pl.when(ki == qi)
        def _():
            lane = lax.broadcasted_iota(jnp.int32, (1, LANE), 1)
            out = acc[...]
            lse = jnp.zeros((B, LANE), F32)
            for h in range(4):
                out = jnp.where(_lane_mask(CH, h), out / l_s[h], out)
                lse = jnp.where(lane == h, m_s[h] + jnp.log(l_s[h]), lse)
            o_ref[...] = out
            lse_ref[...] = lse

    qspec = pl.BlockSpec((B, CH), lambda qi, ki: (qi, 0))
    kspec = pl.BlockSpec((B, CH), lambda qi, ki: (jnp.minimum(ki, qi), 0))
    return pl.pallas_call(
        body, name=name, grid=(n, n),
        in_specs=[qspec, kspec, kspec, pl.BlockSpec((B, LANE), lambda qi, ki: (qi, 0)),
                  pl.BlockSpec((8, B), lambda qi, ki: (0, jnp.minimum(ki, qi)))],
        out_specs=[qspec, pl.BlockSpec((B, LANE), lambda qi, ki: (qi, 0))],
        out_shape=[jax.ShapeDtypeStruct((T, CH), F32), jax.ShapeDtypeStruct((T, LANE), F32)],
        scratch_shapes=[pltpu.VMEM((4, B, 1), F32), pltpu.VMEM((4, B, 1), F32), pltpu.VMEM((B, CH), F32)],
        compiler_params=_cp("parallel", "arbitrary"))(q, k, v, f, ft)


def _fox_delta(o, do, name):
    T = o.shape[0]
    tm = _tile(T, ROW_TILE)

    def body(o_ref, d_ref, out_ref):
        prod = o_ref[...] * d_ref[...]
        lane = lax.broadcasted_iota(jnp.int32, (1, LANE), 1)
        out = jnp.zeros((tm, LANE), F32)
        for h in range(4):
            s = jnp.sum(jnp.where(_lane_mask(CH, h), prod, 0.0), axis=-1, keepdims=True)
            out = jnp.where(lane == h, s, out)
        out_ref[...] = out

    row = pl.BlockSpec((tm, CH), lambda i: (i, 0))
    return pl.pallas_call(
        body, name=name, grid=(T // tm,), in_specs=[row, row],
        out_specs=pl.BlockSpec((tm, LANE), lambda i: (i, 0)),
        out_shape=jax.ShapeDtypeStruct((T, LANE), F32), compiler_params=_cp("parallel"))(o, do)


def _fox_bwd_dq(q, k, v, f, ft, lse, delta, do, name):
    T = q.shape[0]
    B = _tile(T, FOX_B)
    n = T // B

    def body(q_ref, k_ref, v_ref, f_ref, ft_ref, lse_ref, dl_ref, do_ref, dq_ref, dfq_ref, dq_s, df_s):
        qi, ki = pl.program_id(0), pl.program_id(1)

        @pl.when(ki == 0)
        def _():
            dq_s[...] = jnp.zeros_like(dq_s)
            df_s[...] = jnp.zeros_like(df_s)

        @pl.when(ki <= qi)
        def _():
            qv, kv, vv, fv = q_ref[...], k_ref[...], v_ref[...], f_ref[...]
            lsev, dlv, dov = lse_ref[...], dl_ref[...], do_ref[...]
            lane = lax.broadcasted_iota(jnp.int32, (1, LANE), 1)
            for h in range(4):
                mk = _lane_mask(CH, h)
                qh = jnp.where(mk, qv, jnp.zeros_like(qv))
                s = _fox_scores(qh, kv, _lane_col(fv, h), ft_ref, h, qi, ki, B)
                p = jnp.exp(s - _lane_col(lsev, h))
                doh = jnp.where(mk, dov, 0.0)
                ds = p * (_dot(doh, vv, _NT) - _lane_col(dlv, h))
                dq_s[...] += jnp.where(mk, _dot(ds * (HEAD ** -0.5), kv), 0.0)
                df_s[...] += jnp.where(lane == h, jnp.sum(ds, axis=-1, keepdims=True), 0.0)

        @pl.when(ki == qi)
        def _():
            dq_ref[...] = dq_s[...]
            dfq_ref[...] = df_s[...]

    qspec = pl.BlockSpec((B, CH), lambda qi, ki: (qi, 0))
    kspec = pl.BlockSpec((B, CH), lambda qi, ki: (jnp.minimum(ki, qi), 0))
    lspec = pl.BlockSpec((B, LANE), lambda qi, ki: (qi, 0))
    return pl.pallas_call(
        body, name=name, grid=(n, n),
        in_specs=[qspec, kspec, kspec, lspec, pl.BlockSpec((8, B), lambda qi, ki: (0, jnp.minimum(ki, qi))),
                  lspec, lspec, qspec],
        out_specs=[qspec, lspec],
        out_shape=[jax.ShapeDtypeStruct((T, CH), F32), jax.ShapeDtypeStruct((T, LANE), F32)],
        scratch_shapes=[pltpu.VMEM((B, CH), F32), pltpu.VMEM((B, LANE), F32)],
        compiler_params=_cp("parallel", "arbitrary"))(q, k, v, f, ft, lse, delta, do)


def _fox_bwd_dkv(q, k, v, f, ft, lse, delta, do, name):
    T = q.shape[0]
    B = _tile(T, FOX_B)
    n = T // B

    def body(q_ref, k_ref, v_ref, f_ref, ft_ref, lse_ref, dl_ref, do_ref, dk_ref, dv_ref, dft_ref, dk_s, dv_s, df_s):
        ki, qi = pl.program_id(0), pl.program_id(1)

        @pl.when(qi == 0)
        def _():
            dk_s[...] = jnp.zeros_like(dk_s)
            dv_s[...] = jnp.zeros_like(dv_s)
            df_s[...] = jnp.zeros_like(df_s)

        @pl.when(qi >= ki)
        def _():
            qv, kv, vv, fv = q_ref[...], k_ref[...], v_ref[...], f_ref[...]
            lsev, dlv, dov = lse_ref[...], dl_ref[...], do_ref[...]
            for h in range(4):
                mk = _lane_mask(CH, h)
                qh = jnp.where(mk, qv, jnp.zeros_like(qv))
                s = _fox_scores(qh, kv, _lane_col(fv, h), ft_ref, h, qi, ki, B)
                p = jnp.exp(s - _lane_col(lsev, h))
                doh = jnp.where(mk, dov, 0.0)
                ds = p * (_dot(doh, vv, _NT) - _lane_col(dlv, h))
                dv_s[...] += _dot(p, doh, _TN)
                dk_s[...] += _dot(ds * (HEAD ** -0.5), qh, _TN)
                df_s[pl.ds(h, 1), :] -= jnp.sum(ds, axis=0, keepdims=True)

        @pl.when(qi == n - 1)
        def _():
            dk_ref[...] = dk_s[...]
            dv_ref[...] = dv_s[...]
            dft_ref[...] = jnp.zeros_like(dft_ref)
            dft_ref[pl.ds(0, 8), :] = df_s[...]

    qspec = pl.BlockSpec((B, CH), lambda ki, qi: (jnp.maximum(qi, ki), 0))
    kspec = pl.BlockSpec((B, CH), lambda ki, qi: (ki, 0))
    lspec = pl.BlockSpec((B, LANE), lambda ki, qi: (jnp.maximum(qi, ki), 0))
    return pl.pallas_call(
        body, name=name, grid=(n, n),
        in_specs=[qspec, kspec, kspec, lspec, pl.BlockSpec((8, B), lambda ki, qi: (0, ki)), lspec, lspec, qspec],
        out_specs=[kspec, kspec, pl.BlockSpec((LANE, B), lambda ki, qi: (0, ki))],
        out_shape=[jax.ShapeDtypeStruct((T, CH), F32), jax.ShapeDtypeStruct((T, CH), F32),
                   jax.ShapeDtypeStruct((LANE, T), F32)],
        scratch_shapes=[pltpu.VMEM((B, CH), F32), pltpu.VMEM((B, CH), F32), pltpu.VMEM((8, B), F32)],
        compiler_params=_cp("parallel", "arbitrary"))(q, k, v, f, ft, lse, delta, do)


def _fox_post(zf, dqn, dkn, dv, dfq, dft, gq, gk, bf, name):
    T = zf.shape[0]
    tm = _tile(T, FOX_TM)
    nt = T // tm
    upper = jnp.asarray(_tri(tm, False), _MXU)

    def body(z_ref, dq_ref, dk_ref, dv_ref, dfq_ref, dft_ref, gq_ref, gk_ref, bf_ref, u_ref, dz_ref, sm_ref, carry, rc_s):
        @pl.when(pl.program_id(0) == 0)
        def _():
            carry[...] = jnp.zeros_like(carry)
            sm_ref[...] = jnp.zeros_like(sm_ref)

        z = z_ref[...]
        q_raw, k_raw = z[:, :CH], z[:, CH:2 * CH]
        _, q_r = _head_rms(q_raw, gq_ref[...], 4)
        _, k_r = _head_rms(k_raw, gk_ref[...], 4)
        dq, dgq = _head_rms_bwd(dq_ref[...], q_raw, q_r, gq_ref[...], 4)
        dk, dgk = _head_rms_bwd(dk_ref[...], k_raw, k_r, gk_ref[...], 4)
        df = dfq_ref[...] + dft_ref[...].T
        rc_s[...] = _exact_dot(u_ref[...], df, _NN, "b") + carry[pl.ds(0, 1), :]
        carry[pl.ds(0, 1), :] = rc_s[pl.ds(0, 1), :]
        lane = lax.broadcasted_iota(jnp.int32, (1, LANE), 1)
        x = z[:, 3 * CH:] + bf_ref[...]
        dff = jnp.where(lane < 4, rc_s[...] * _sigmoid(-x), 0.0)
        dz_ref[:, :CH] = dq.astype(dz_ref.dtype)
        dz_ref[:, CH:2 * CH] = dk.astype(dz_ref.dtype)
        dz_ref[:, 2 * CH:3 * CH] = dv_ref[...].astype(dz_ref.dtype)
        dz_ref[:, 3 * CH:] = dff.astype(dz_ref.dtype)
        sm_ref[pl.ds(0, 1), :] += dgq
        sm_ref[pl.ds(1, 1), :] += dgk
        sm_ref[pl.ds(2, 1), :LANE] += jnp.sum(dff, axis=0, keepdims=True)

    rev = lambda i: (nt - 1 - i, 0)
    row = pl.BlockSpec((tm, CH), rev)
    lrow = pl.BlockSpec((tm, LANE), rev)
    vec = pl.BlockSpec((1, CH), lambda i: (0, 0))
    return pl.pallas_call(
        body, name=name, grid=(nt,),
        in_specs=[pl.BlockSpec((tm, 3 * CH + LANE), rev), row, row, row, lrow,
                  pl.BlockSpec((LANE, tm), lambda i: (0, nt - 1 - i)), vec, vec,
                  pl.BlockSpec((1, LANE), lambda i: (0, 0)), pl.BlockSpec((tm, tm), lambda i: (0, 0))],
        out_specs=[pl.BlockSpec((tm, 3 * CH + LANE), rev), pl.BlockSpec((8, CH), lambda i: (0, 0))],
        out_shape=[jax.ShapeDtypeStruct((T, 3 * CH + LANE), _MXU), jax.ShapeDtypeStruct((8, CH), F32)],
        scratch_shapes=[pltpu.VMEM((8, LANE), F32), pltpu.VMEM((tm, LANE), F32)],
        compiler_params=_cp("arbitrary"))(zf, dqn, dkn, dv, dfq, dft, gq, gk, bf, upper)


AUG_F, AUG_ONE, AUG_LSE = HEAD, HEAD + 3, HEAD + 6


def _pieces(x):
    hi = x.astype(_MXU).astype(F32)
    r1 = x - hi
    mid = r1.astype(_MXU).astype(F32)
    lo = (r1 - mid).astype(_MXU).astype(F32)
    return hi, mid, lo


def _put_pieces(base, first_lane, x, sign):
    lane = lax.broadcasted_iota(jnp.int32, (1, LANE), 1)
    for j, piece in enumerate(_pieces(x)):
        base = jnp.where(lane == first_lane + j, sign * piece, base)
    return base


def _head_select_matrix():
    p = np.zeros((4, 4 * HEAD, LANE), np.float32)
    for h in range(4):
        for d in range(HEAD):
            p[h, h * HEAD + d, d] = 1.0
    return p


def _tri_steps(n, by_key):
    if by_key:
        pairs = [(q, k) for k in range(n) for q in range(k, n)]
    else:
        pairs = [(q, k) for q in range(n) for k in range(q + 1)]
    return (jnp.asarray([p[0] for p in pairs], jnp.int32), jnp.asarray([p[1] for p in pairs], jnp.int32))


def _fox2_prep(zf, gq, gk, bf, sel, name):
    T = zf.shape[0]
    tm = _tile(T, FOX_TM)
    lower = jnp.asarray(_tri(tm, True), _MXU)

    def body(z_ref, gq_ref, gk_ref, bf_ref, l_ref, p_ref, qa_ref, ka_ref, va_ref, vat_ref, carry, f_s):
        @pl.when(pl.program_id(0) == 0)
        def _():
            carry[...] = jnp.zeros_like(carry)

        z = z_ref[...]
        q, _ = _head_rms(z[:, :CH], gq_ref[...], 4)
        k, _ = _head_rms(z[:, CH:2 * CH], gk_ref[...], 4)
        q = (q * (HEAD ** -0.5)).astype(_MXU)
        k = k.astype(_MXU)
        v = z[:, 2 * CH:3 * CH].astype(_MXU)
        lane = lax.broadcasted_iota(jnp.int32, (1, LANE), 1)
        lf = jnp.where(lane < 4, _log_sigmoid(z[:, 3 * CH:] + bf_ref[...]), 0.0)
        f_s[...] = _exact_dot(l_ref[...], lf, _NN, "b") + carry[pl.ds(0, 1), :]
        carry[pl.ds(0, 1), :] = f_s[pl.ds(tm - 1, 1), :]
        fv = f_s[...]
        q_ones = (lane >= AUG_ONE) & (lane < AUG_ONE + 3)
        k_ones = ((lane >= AUG_F) & (lane < AUG_F + 3)) | ((lane >= AUG_LSE) & (lane < AUG_LSE + 3))
        v_ones = (lane >= AUG_F) & (lane < AUG_F + 3)
        for h in range(4):
            fh = _lane_col(fv, h)
            qa = jnp.where(q_ones, 1.0, _dot(q, p_ref[h]))
            qa_ref[h] = _put_pieces(qa, AUG_F, fh, 1.0).astype(qa_ref.dtype)
            ka = jnp.where(k_ones, 1.0, _dot(k, p_ref[h]))
            ka_ref[h] = _put_pieces(ka, AUG_ONE, fh, -1.0).astype(ka_ref.dtype)
            va = jnp.where(v_ones, 1.0, _dot(v, p_ref[h]))
            va_ref[h] = va.astype(va_ref.dtype)
            vat_ref[h] = va.T.astype(vat_ref.dtype)

    vec = pl.BlockSpec((1, CH), lambda i: (0, 0))
    hspec = pl.BlockSpec((4, tm, LANE), lambda i: (0, i, 0))
    hsh = jax.ShapeDtypeStruct((4, T, LANE), _MXU)
    return pl.pallas_call(
        body, name=name, grid=(T // tm,),
        in_specs=[pl.BlockSpec((tm, 3 * CH + LANE), lambda i: (i, 0)), vec, vec,
                  pl.BlockSpec((1, LANE), lambda i: (0, 0)), pl.BlockSpec((tm, tm), lambda i: (0, 0)),
                  pl.BlockSpec(sel.shape, lambda i: (0, 0, 0))],
        out_specs=[hspec, hspec, hspec, pl.BlockSpec((4, LANE, tm), lambda i: (0, 0, i))],
        out_shape=[hsh, hsh, hsh, jax.ShapeDtypeStruct((4, LANE, T), _MXU)],
        scratch_shapes=[pltpu.VMEM((8, LANE), F32), pltpu.VMEM((tm, LANE), F32)],
        compiler_params=_cp("arbitrary"))(zf, gq, gk, bf, lower, sel)


def _causal(s, transposed):
    row = lax.broadcasted_iota(jnp.int32, s.shape, 0)
    col = lax.broadcasted_iota(jnp.int32, s.shape, 1)
    return jnp.where((row <= col) if transposed else (col <= row), s, NEG_INF)


def _mxu_dot(a, b, dims):
    return lax.dot_general(a, b, dims, preferred_element_type=F32)


def _fox2_fwd(qa, ka, vat, sel, name):
    T = qa.shape[1]
    B = _tile(T, FOX_B)
    n = T // B
    qt, kt = _tri_steps(n, False)

    def body(qt_ref, kt_ref, qa_ref, ka_ref, vat_ref, p_ref, o_ref, qb_ref, m_s, acc):
        step = pl.program_id(0)
        qi, ki = qt_ref[step], kt_ref[step]

        @pl.when(ki == 0)
        def _():
            m_s[...] = jnp.full_like(m_s, NEG_INF)
            acc[...] = jnp.zeros_like(acc)

        def update(diag):
            for h in range(4):
                st = _mxu_dot(ka_ref[h], qa_ref[h], _NT)
                if diag:
                    st = _causal(st, True)
                m_old = m_s[h, pl.ds(0, 1), :]
                m_new = jnp.maximum(m_old, jnp.max(st, axis=0, keepdims=True))
                pt = jnp.exp(st - m_new)
                acc[h] = acc[h] * jnp.exp(m_old - m_new) + _dot(vat_ref[h], pt)
                m_s[h, pl.ds(0, 1), :] = m_new

        @pl.when(ki < qi)
        def _():
            update(False)

        @pl.when(ki == qi)
        def _():
            update(True)
            row = lax.broadcasted_iota(jnp.int32, (LANE, 1), 0)
            out = jnp.zeros((B, CH), F32)
            for h in range(4):
                a = acc[h]
                l = a[AUG_F:AUG_F + 1, :]
                out = out + _exact_dot((a / l).T, p_ref[h], _NT, "a")
                lse = m_s[h, pl.ds(0, 1), :] + jnp.log(l)
                qbt = qa_ref[h].astype(F32).T
                for j, piece in enumerate(_pieces(lse)):
                    qbt = jnp.where(row == AUG_LSE + j, -piece, qbt)
                qb_ref[h] = qbt.T.astype(qb_ref.dtype)
            o_ref[...] = out

    qspec = pl.BlockSpec((4, B, LANE), lambda s, qt, kt: (0, qt[s], 0))
    kspec = pl.BlockSpec((4, B, LANE), lambda s, qt, kt: (0, kt[s], 0))
    grid_spec = pltpu.PrefetchScalarGridSpec(
        num_scalar_prefetch=2, grid=(qt.shape[0],),
        in_specs=[qspec, kspec, pl.BlockSpec((4, LANE, B), lambda s, qt, kt: (0, 0, kt[s])),
                  pl.BlockSpec(sel.shape, lambda s, qt, kt: (0, 0, 0))],
        out_specs=[pl.BlockSpec((B, CH), lambda s, qt, kt: (qt[s], 0)), qspec],
        scratch_shapes=[pltpu.VMEM((4, 8, B), F32), pltpu.VMEM((4, LANE, B), F32)])
    return pl.pallas_call(
        body, name=name, grid_spec=grid_spec,
        out_shape=[jax.ShapeDtypeStruct((T, CH), F32), jax.ShapeDtypeStruct((4, T, LANE), _MXU)],
        compiler_params=_cp("arbitrary"))(qt, kt, qa, ka, vat, sel)


def _fox2_bwd_prep(o, do, sel, name):
    T = o.shape[0]
    tm = _tile(T, ROW_TILE)

    def body(o_ref, d_ref, p_ref, out_ref):
        dov = d_ref[...]
        prod = o_ref[...] * dov
        dob = dov.astype(_MXU)
        for h in range(4):
            delta = jnp.sum(jnp.where(_lane_mask(CH, h), prod, 0.0), axis=-1, keepdims=True)
            out_ref[h] = _put_pieces(_dot(dob, p_ref[h]), AUG_F, delta, -1.0).astype(out_ref.dtype)

    row = pl.BlockSpec((tm, CH), lambda i: (i, 0))
    return pl.pallas_call(
        body, name=name, grid=(T // tm,),
        in_specs=[row, row, pl.BlockSpec(sel.shape, lambda i: (0, 0, 0))],
        out_specs=pl.BlockSpec((4, tm, LANE), lambda i: (0, i, 0)),
        out_shape=jax.ShapeDtypeStruct((4, T, LANE), _MXU), compiler_params=_cp("parallel"))(o, do, sel)


def _fox2_bwd_dq(qb, ka, va, doa, sel, name):
    T = qb.shape[1]
    B = _tile(T, FOX_B)
    n = T // B
    qt, kt = _tri_steps(n, False)

    def body(qt_ref, kt_ref, qb_ref, ka_ref, va_ref, do_ref, p_ref, dq_ref, dfq_ref, dq_s):
        step = pl.program_id(0)
        qi, ki = qt_ref[step], kt_ref[step]

        @pl.when(ki == 0)
        def _():
            dq_s[...] = jnp.zeros_like(dq_s)

        def update(diag):
            for h in range(4):
                s = _mxu_dot(qb_ref[h], ka_ref[h], _NT)
                if diag:
                    s = _causal(s, False)
                ds = jnp.exp(s) * _mxu_dot(do_ref[h], va_ref[h], _NT)
                dq_s[h] += _dot(ds, ka_ref[h])

        @pl.when(ki < qi)
        def _():
            update(False)

        @pl.when(ki == qi)
        def _():
            update(True)
            lane = lax.broadcasted_iota(jnp.int32, (1, LANE), 1)
            out = jnp.zeros((B, CH), F32)
            dfq = jnp.zeros((B, LANE), F32)
            for h in range(4):
                out = out + _exact_dot(dq_s[h] * (HEAD ** -0.5), p_ref[h], _NT, "a")
                dfq = jnp.where(lane == h, _lane_col(dq_s[h], AUG_F), dfq)
            dq_ref[...] = out
            dfq_ref[...] = dfq

    qspec = pl.BlockSpec((4, B, LANE), lambda s, qt, kt: (0, qt[s], 0))
    kspec = pl.BlockSpec((4, B, LANE), lambda s, qt, kt: (0, kt[s], 0))
    grid_spec = pltpu.PrefetchScalarGridSpec(
        num_scalar_prefetch=2, grid=(qt.shape[0],),
        in_specs=[qspec, kspec, kspec, qspec, pl.BlockSpec(sel.shape, lambda s, qt, kt: (0, 0, 0))],
        out_specs=[pl.BlockSpec((B, CH), lambda s, qt, kt: (qt[s], 0)),
                   pl.BlockSpec((B, LANE), lambda s, qt, kt: (qt[s], 0))],
        scratch_shapes=[pltpu.VMEM((4, B, LANE), F32)])
    return pl.pallas_call(
        body, name=name, grid_spec=grid_spec,
        out_shape=[jax.ShapeDtypeStruct((T, CH), F32), jax.ShapeDtypeStruct((T, LANE), F32)],
        compiler_params=_cp("arbitrary"))(qt, kt, qb, ka, va, doa, sel)


def _fox2_bwd_dkv(qb, ka, va, doa, sel, name):
    T = qb.shape[1]
    B = _tile(T, FOX_B)
    n = T // B
    qt, kt = _tri_steps(n, True)

    def body(qt_ref, kt_ref, qb_ref, ka_ref, va_ref, do_ref, p_ref, dk_ref, dv_ref, df_ref, dk_s, dv_s):
        step = pl.program_id(0)
        qi, ki = qt_ref[step], kt_ref[step]

        @pl.when(qi == ki)
        def _():
            dk_s[...] = jnp.zeros_like(dk_s)
            dv_s[...] = jnp.zeros_like(dv_s)

        def update(diag):
            for h in range(4):
                st = _mxu_dot(ka_ref[h], qb_ref[h], _NT)
                if diag:
                    st = _causal(st, True)
                pt = jnp.exp(st)
                dst = pt * _mxu_dot(va_ref[h], do_ref[h], _NT)
                dv_s[h] += _dot(pt, do_ref[h])
                dk_s[h] += _dot(dst, qb_ref[h])

        @pl.when(qi == ki)
        def _():
            update(True)

        @pl.when(qi > ki)
        def _():
            update(False)

        @pl.when(qi == n - 1)
        def _():
            lane = lax.broadcasted_iota(jnp.int32, (1, LANE), 1)
            dk = jnp.zeros((B, CH), F32)
            dv = jnp.zeros((B, CH), F32)
            dfk = jnp.zeros((B, LANE), F32)
            for h in range(4):
                dk = dk + _exact_dot(dk_s[h], p_ref[h], _NT, "a")
                dv = dv + _exact_dot(dv_s[h], p_ref[h], _NT, "a")
                dfk = jnp.where(lane == h, -_lane_col(dk_s[h], AUG_ONE), dfk)
            dk_ref[...] = dk
            dv_ref[...] = dv
            df_ref[...] = dfk

    qspec = pl.BlockSpec((4, B, LANE), lambda s, qt, kt: (0, qt[s], 0))
    kspec = pl.BlockSpec((4, B, LANE), lambda s, qt, kt: (0, kt[s], 0))
    ospec = pl.BlockSpec((B, CH), lambda s, qt, kt: (kt[s], 0))
    grid_spec = pltpu.PrefetchScalarGridSpec(
        num_scalar_prefetch=2, grid=(qt.shape[0],),
        in_specs=[qspec, kspec, kspec, qspec, pl.BlockSpec(sel.shape, lambda s, qt, kt: (0, 0, 0))],
        out_specs=[ospec, ospec, pl.BlockSpec((B, LANE), lambda s, qt, kt: (kt[s], 0))],
        scratch_shapes=[pltpu.VMEM((4, B, LANE), F32), pltpu.VMEM((4, B, LANE), F32)])
    return pl.pallas_call(
        body, name=name, grid_spec=grid_spec,
        out_shape=[jax.ShapeDtypeStruct((T, CH), F32), jax.ShapeDtypeStruct((T, CH), F32),
                   jax.ShapeDtypeStruct((T, LANE), F32)],
        compiler_params=_cp("arbitrary"))(qt, kt, qb, ka, va, doa, sel)


def _fox2_bwd(qb, ka, va, doa, sel, name):
    T = qb.shape[1]
    B = _tile(T, FOX_B)
    n = T // B
    qt, kt = _tri_steps(n, True)
    nsteps = qt.shape[0]

    def body(qt_ref, kt_ref, qb_ref, ka_ref, va_ref, do_ref, p_ref, dk_ref, dv_ref, df_ref, dq_hbm,
             dk_s, dv_s, kat_s, dq_s, sem):
        step = pl.program_id(0)
        qi, ki = qt_ref[step], kt_ref[step]

        @pl.when(step == 0)
        def _():
            dq_s[...] = jnp.zeros_like(dq_s)

        @pl.when(qi == ki)
        def _():
            dk_s[...] = jnp.zeros_like(dk_s)
            dv_s[...] = jnp.zeros_like(dv_s)
            for h in range(4):
                kat_s[h] = ka_ref[h].astype(F32).T.astype(kat_s.dtype)

        def update(diag):
            for h in range(4):
                st = _mxu_dot(ka_ref[h], qb_ref[h], _NT)
                if diag:
                    st = _causal(st, True)
                pt = jnp.exp(st)
                dst = (pt * _mxu_dot(va_ref[h], do_ref[h], _NT)).astype(_MXU)
                dv_s[h] += _dot(pt, do_ref[h])
                dk_s[h] += _mxu_dot(dst, qb_ref[h], _NN)
                dq_s[qi, h] += _mxu_dot(kat_s[h], dst, _NN)

        @pl.when(qi == ki)
        def _():
            update(True)

        @pl.when(qi > ki)
        def _():
            update(False)

        @pl.when(qi == n - 1)
        def _():
            lane = lax.broadcasted_iota(jnp.int32, (1, LANE), 1)
            dk = jnp.zeros((B, CH), F32)
            dv = jnp.zeros((B, CH), F32)
            dfk = jnp.zeros((B, LANE), F32)
            for h in range(4):
                dk = dk + _exact_dot(dk_s[h], p_ref[h], _NT, "a")
                dv = dv + _exact_dot(dv_s[h], p_ref[h], _NT, "a")
                dfk = jnp.where(lane == h, -_lane_col(dk_s[h], AUG_ONE), dfk)
            dk_ref[...] = dk
            dv_ref[...] = dv
            df_ref[...] = dfk

        @pl.when(step == nsteps - 1)
        def _():
            cp = pltpu.make_async_copy(dq_s, dq_hbm, sem)
            cp.start()
            cp.wait()

    qspec = pl.BlockSpec((4, B, LANE), lambda s, qt, kt: (0, qt[s], 0))
    kspec = pl.BlockSpec((4, B, LANE), lambda s, qt, kt: (0, kt[s], 0))
    ospec = pl.BlockSpec((B, CH), lambda s, qt, kt: (kt[s], 0))
    grid_spec = pltpu.PrefetchScalarGridSpec(
        num_scalar_prefetch=2, grid=(nsteps,),
        in_specs=[qspec, kspec, kspec, qspec, pl.BlockSpec(sel.shape, lambda s, qt, kt: (0, 0, 0))],
        out_specs=[ospec, ospec, pl.BlockSpec((B, LANE), lambda s, qt, kt: (kt[s], 0)),
                   pl.BlockSpec(memory_space=pl.ANY)],
        scratch_shapes=[pltpu.VMEM((4, B, LANE), F32), pltpu.VMEM((4, B, LANE), F32), pltpu.VMEM((4, LANE, B), _MXU),
                        pltpu.VMEM((n, 4, LANE, B), F32), pltpu.SemaphoreType.DMA])
    return pl.pallas_call(
        body, name=name, grid_spec=grid_spec,
        out_shape=[jax.ShapeDtypeStruct((T, CH), F32), jax.ShapeDtypeStruct((T, CH), F32),
                   jax.ShapeDtypeStruct((T, LANE), F32), jax.ShapeDtypeStruct((n, 4, LANE, B), F32)],
        compiler_params=_cp("arbitrary"))(qt, kt, qb, ka, va, doa, sel)


def _fox2_post(zf, dqt, dkn, dv, dfk, sel, gq, gk, bf, name):
    T = zf.shape[0]
    tm = _tile(T, FOX_TM)
    nt = T // tm
    B = dqt.shape[3]
    per = B // tm
    upper = jnp.asarray(_tri(tm, False), _MXU)

    def body(z_ref, dqt_ref, dk_ref, dv_ref, df_ref, p_ref, gq_ref, gk_ref, bf_ref, u_ref, dz_ref, sm_ref, carry, rc_s):
        @pl.when(pl.program_id(0) == 0)
        def _():
            carry[...] = jnp.zeros_like(carry)
            sm_ref[...] = jnp.zeros_like(sm_ref)

        lane = lax.broadcasted_iota(jnp.int32, (1, LANE), 1)
        dqn = jnp.zeros((tm, CH), F32)
        dfq = jnp.zeros((tm, LANE), F32)
        for h in range(4):
            blk = dqt_ref[0, h].T
            dqn = dqn + _exact_dot(blk * (HEAD ** -0.5), p_ref[h], _NT, "a")
            dfq = jnp.where(lane == h, _lane_col(blk, AUG_F), dfq)
        z = z_ref[...]
        q_raw, k_raw = z[:, :CH], z[:, CH:2 * CH]
        _, q_r = _head_rms(q_raw, gq_ref[...], 4)
        _, k_r = _head_rms(k_raw, gk_ref[...], 4)
        dq, dgq = _head_rms_bwd(dqn, q_raw, q_r, gq_ref[...], 4)
        dk, dgk = _head_rms_bwd(dk_ref[...], k_raw, k_r, gk_ref[...], 4)
        rc_s[...] = _exact_dot(u_ref[...], dfq + df_ref[...], _NN, "b") + carry[pl.ds(0, 1), :]
        carry[pl.ds(0, 1), :] = rc_s[pl.ds(0, 1), :]
        x = z[:, 3 * CH:] + bf_ref[...]
        dff = jnp.where(lane < 4, rc_s[...] * _sigmoid(-x), 0.0)
        dz_ref[:, :CH] = dq.astype(dz_ref.dtype)
        dz_ref[:, CH:2 * CH] = dk.astype(dz_ref.dtype)
        dz_ref[:, 2 * CH:3 * CH] = dv_ref[...].astype(dz_ref.dtype)
        dz_ref[:, 3 * CH:] = dff.astype(dz_ref.dtype)
        sm_ref[pl.ds(0, 1), :] += dgq
        sm_ref[pl.ds(1, 1), :] += dgk
        sm_ref[pl.ds(2, 1), :LANE] += jnp.sum(dff, axis=0, keepdims=True)

    rev = lambda i: (nt - 1 - i, 0)
    row = pl.BlockSpec((tm, CH), rev)
    lrow = pl.BlockSpec((tm, LANE), rev)
    vec = pl.BlockSpec((1, CH), lambda i: (0, 0))
    return pl.pallas_call(
        body, name=name, grid=(nt,),
        in_specs=[pl.BlockSpec((tm, 3 * CH + LANE), rev),
                  pl.BlockSpec((1, 4, LANE, tm), lambda i: ((nt - 1 - i) // per, 0, 0, (nt - 1 - i) % per)),
                  row, row, lrow, pl.BlockSpec(sel.shape, lambda i: (0, 0, 0)), vec, vec,
                  pl.BlockSpec((1, LANE), lambda i: (0, 0)), pl.BlockSpec((tm, tm), lambda i: (0, 0))],
        out_specs=[pl.BlockSpec((tm, 3 * CH + LANE), rev), pl.BlockSpec((8, CH), lambda i: (0, 0))],
        out_shape=[jax.ShapeDtypeStruct((T, 3 * CH + LANE), _MXU), jax.ShapeDtypeStruct((8, CH), F32)],
        scratch_shapes=[pltpu.VMEM((8, LANE), F32), pltpu.VMEM((tm, LANE), F32)],
        compiler_params=_cp("arbitrary"))(zf, dqt, dkn, dv, dfk, sel, gq, gk, bf, upper)


def _merge_fwd(acts, zg, wbr, wout, x1, name):
    T, D = x1.shape
    tm = _tile(T, 256)

    def body(a0, a1, a2, a3, zg_ref, wbr_ref, wout_ref, x_ref, o_ref, mg_ref):
        merged = None
        for i, a_ref in enumerate((a0, a1, a2, a3)):
            term = _sigmoid(zg_ref[:, i * D:(i + 1) * D]) * _dot(a_ref[...], wbr_ref[i])
            merged = term if merged is None else merged + term
        mg_ref[...] = merged.astype(mg_ref.dtype)
        o_ref[...] = x_ref[...] + _dot(merged, wout_ref[...])

    arow = pl.BlockSpec((tm, CH), lambda i: (i, 0))
    xrow = pl.BlockSpec((tm, D), lambda i: (i, 0))
    return pl.pallas_call(
        body, name=name, grid=(T // tm,),
        in_specs=[arow, arow, arow, arow, pl.BlockSpec((tm, 4 * D), lambda i: (i, 0)),
                  pl.BlockSpec((4, CH, D), lambda i: (0, 0, 0)), pl.BlockSpec((D, D), lambda i: (0, 0)), xrow],
        out_specs=[xrow, xrow],
        out_shape=[jax.ShapeDtypeStruct((T, D), F32), jax.ShapeDtypeStruct((T, D), _MXU)],
        compiler_params=_cp("parallel"))(*acts, zg, wbr, wout, x1)


def _merge_bwd(dx2, acts, zg, wbr, wout, name):
    T, D = dx2.shape
    tm = _tile(T, 256)
    nt = T // tm

    def body(dx_ref, a0, a1, a2, a3, zg_ref, wbr_ref, wout_ref, d0, d1, d2, d3, dzg_ref, dw_ref, dw_s):
        i = pl.program_id(0)

        @pl.when(i == 0)
        def _():
            dw_s[...] = jnp.zeros_like(dw_s)

        dm = _dot(dx_ref[...], wout_ref[...], _NT)
        for b, (a_ref, d_ref) in enumerate(((a0, d0), (a1, d1), (a2, d2), (a3, d3))):
            av = a_ref[...].astype(_MXU)
            g = _sigmoid(zg_ref[:, b * D:(b + 1) * D])
            p = _dot(av, wbr_ref[b])
            dzg_ref[:, b * D:(b + 1) * D] = (dm * p * (g * (1.0 - g))).astype(dzg_ref.dtype)
            dp = (dm * g).astype(_MXU)
            d_ref[...] = _dot(dp, wbr_ref[b], _NT)
            dw_s[b] += _dot(av, dp, _TN)

        @pl.when(i == nt - 1)
        def _():
            dw_ref[...] = dw_s[...].astype(dw_ref.dtype)

    arow = pl.BlockSpec((tm, CH), lambda i: (i, 0))
    xrow = pl.BlockSpec((tm, D), lambda i: (i, 0))
    grow = pl.BlockSpec((tm, 4 * D), lambda i: (i, 0))
    wspec = pl.BlockSpec((4, CH, D), lambda i: (0, 0, 0))
    ash = jax.ShapeDtypeStruct((T, CH), F32)
    return pl.pallas_call(
        body, name=name, grid=(nt,),
        in_specs=[xrow, arow, arow, arow, arow, grow, wspec, pl.BlockSpec((D, D), lambda i: (0, 0))],
        out_specs=[arow, arow, arow, arow, grow, wspec],
        out_shape=[ash, ash, ash, ash, jax.ShapeDtypeStruct((T, 4 * D), _MXU), jax.ShapeDtypeStruct((4, CH, D), _MXU)],
        scratch_shapes=[pltpu.VMEM((4, CH, D), F32)],
        compiler_params=_cp("arbitrary"))(dx2, *acts, zg, wbr, wout)


def _rows_2d(a):
    return a.reshape((-1, a.shape[-1])) if a.ndim > 1 else a.reshape((1, -1))


def _row_tile(rows, cols, n_bufs):
    padded = -(-cols // LANE) * LANE
    cap = max(8, (VMEM_LIMIT // 3) // (2 * n_bufs * 4 * padded))
    return _tile(rows, cap, 8)


def _sum8(recv, name):
    shape = recv.shape[1:]
    r2 = recv.reshape((N_DEV, -1, shape[-1]))
    rows, cols = r2.shape[1:]
    tr = _row_tile(rows, cols, N_DEV // 2 + 1)

    def body(r_ref, o_ref):
        acc = r_ref[0].astype(F32)
        for d in range(1, N_DEV):
            acc = acc + r_ref[d].astype(F32)
        o_ref[...] = acc

    out = pl.pallas_call(
        body, name=name, grid=(rows // tr,),
        in_specs=[pl.BlockSpec((N_DEV, tr, cols), lambda i: (0, i, 0))],
        out_specs=pl.BlockSpec((tr, cols), lambda i: (i, 0)),
        out_shape=jax.ShapeDtypeStruct((rows, cols), F32), compiler_params=_cp("parallel"))(r2)
    return out.reshape(shape)


def _adamw(w, g, m, v, name):
    shape = w.shape
    w2, g2, m2, v2 = (_rows_2d(a) for a in (w, g, m, v))
    rows, cols = w2.shape
    tr = _row_tile(rows, cols, 7)

    def body(w_ref, g_ref, m_ref, v_ref, d_ref, nm_ref, nv_ref):
        gv = g_ref[...]
        nm = ADAM_B1 * m_ref[...] + (1.0 - ADAM_B1) * gv
        nv = ADAM_B2 * v_ref[...] + (1.0 - ADAM_B2) * jnp.square(gv)
        m_hat = nm / (1.0 - ADAM_B1 ** ADAM_STEP)
        v_hat = nv / (1.0 - ADAM_B2 ** ADAM_STEP)
        d_ref[...] = -ADAM_LR * (m_hat / (jnp.sqrt(v_hat) + ADAM_EPS) + ADAM_WD * w_ref[...])
        nm_ref[...] = nm
        nv_ref[...] = nv

    spec = pl.BlockSpec((tr, cols), lambda i: (i, 0))
    osh = jax.ShapeDtypeStruct((rows, cols), F32)
    outs = pl.pallas_call(
        body, name=name, grid=(rows // tr,), in_specs=[spec] * 4, out_specs=[spec] * 3,
        out_shape=[osh] * 3, compiler_params=_cp("parallel"))(w2, g2, m2, v2)
    return tuple(o.reshape(shape) for o in outs)


def _exchange(items, name):
    n = len(items)
    widths, out_shapes = [], []
    for src, kind, ax in items:
        if kind == "gather":
            w = src.shape[ax]
            shp = list(src.shape)
            shp[ax] = N_DEV * w
        else:
            w = src.shape[ax] // N_DEV
            shp = list(src.shape)
            shp[ax] = w
            shp = [N_DEV] + shp
        widths.append(w)
        out_shapes.append(jax.ShapeDtypeStruct(tuple(shp), src.dtype))

    def body(*refs):
        srcs, outs = refs[:n], refs[n:2 * n]
        send, recv, lsem = refs[2 * n:]
        x, y, c = lax.axis_index("x"), lax.axis_index("y"), lax.axis_index("c")
        me = 4 * x + 2 * y + c

        def peer(k):
            b = k + 1
            px = 1 - x if b & 4 else x
            py = 1 - y if b & 2 else y
            pc = 1 - c if b & 1 else c
            return (px, py, pc), 4 * px + 2 * py + pc

        def win(ref, ax, idx, w):
            return ref.at[tuple([slice(None)] * ax + [pl.ds(idx * w, w)])]

        def ends(j, mine, theirs):
            _, kind, ax = items[j]
            if kind == "gather":
                return srcs[j], win(outs[j], ax, mine, widths[j])
            return win(srcs[j], ax, theirs, widths[j]), outs[j].at[mine]

        local, sent = [], []
        for j in range(n):
            s, d = ends(j, me, me)
            cp = pltpu.make_async_copy(s, d, lsem.at[j])
            cp.start()
            local.append(cp)
            for k in range(N_DEV - 1):
                dev, pid = peer(k)
                s, d = ends(j, me, pid)
                cp = pltpu.make_async_remote_copy(s, d, send.at[j, k], recv.at[j, k], device_id=dev,
                                                  device_id_type=pl.DeviceIdType.MESH)
                cp.start()
                sent.append(cp)
        for j in range(n):
            for k in range(N_DEV - 1):
                dev, pid = peer(k)
                s, d = ends(j, pid, me)
                pltpu.make_async_remote_copy(s, d, send.at[j, k], recv.at[j, k], device_id=dev,
                                             device_id_type=pl.DeviceIdType.MESH).wait_recv()
        for cp in sent:
            cp.wait_send()
        for cp in local:
            cp.wait()

    hbm = pl.BlockSpec(memory_space=pl.ANY)
    return pl.pallas_call(
        body, name=name, in_specs=[hbm] * n, out_specs=[hbm] * n, out_shape=out_shapes,
        scratch_shapes=[pltpu.SemaphoreType.DMA((n, N_DEV - 1)), pltpu.SemaphoreType.DMA((n, N_DEV - 1)),
                        pltpu.SemaphoreType.DMA((n,))],
        compiler_params=pltpu.CompilerParams(has_side_effects=True))(*[it[0] for it in items])


def _exchange_plan(items):
    widths, out_shapes = [], []
    for src, kind, ax in items:
        shp = list(src.shape)
        if kind == "gather":
            w = src.shape[ax]
            shp[ax] = N_DEV * w
        else:
            w = src.shape[ax] // N_DEV
            shp[ax] = w
            shp = [N_DEV] + shp
        widths.append(w)
        out_shapes.append((tuple(shp), src.dtype))
    return widths, out_shapes


def _exchange_refs(items, widths, srcs, outs):
    x, y, c = lax.axis_index("x"), lax.axis_index("y"), lax.axis_index("c")
    me = 4 * x + 2 * y + c

    def peer(k):
        b = k + 1
        px = 1 - x if b & 4 else x
        py = 1 - y if b & 2 else y
        pc = 1 - c if b & 1 else c
        return (px, py, pc), 4 * px + 2 * py + pc

    def win(ref, ax, idx, w):
        return ref.at[tuple([slice(None)] * ax + [pl.ds(idx * w, w)])]

    def ends(j, mine, theirs):
        _, kind, ax = items[j]
        if kind == "gather":
            return srcs[j], win(outs[j], ax, mine, widths[j])
        return win(srcs[j], ax, theirs, widths[j]), outs[j].at[mine]

    return me, peer, ends


_HBM = pl.BlockSpec(memory_space=pltpu.HBM)
_SEM = pl.BlockSpec(memory_space=pltpu.SEMAPHORE)


def _exchange_start(items, name):
    n = len(items)
    widths, out_shapes = _exchange_plan(items)
    meta = [(None, kind, ax) for _, kind, ax in items]

    def body(*refs):
        srcs, lands = refs[:n], refs[n:2 * n]
        send, recv, lsem = refs[2 * n], refs[2 * n + 1], refs[2 * n + 2]
        token = refs[-1]
        me, peer, ends = _exchange_refs(meta, widths, srcs, lands)
        for j in range(n):
            for k in range(N_DEV - 1):
                dev, pid = peer(k)
                s, d = ends(j, me, pid)
                q = j * (N_DEV - 1) + k
                pltpu.make_async_remote_copy(s, d, send.at[q], recv.at[q], device_id=dev,
                                             device_id_type=pl.DeviceIdType.MESH).start()
        for j in range(n):
            s, d = ends(j, me, me)
            pltpu.make_async_copy(s, d, lsem.at[j]).start()
        token[...] = jnp.zeros_like(token)

    srcs = [pltpu.with_memory_space_constraint(it[0], pltpu.HBM) for it in items]
    lands = [pltpu.with_memory_space_constraint(lax.empty(shp, dt), pltpu.HBM) for shp, dt in out_shapes]
    outs = pl.pallas_call(
        body, name=name,
        out_shape=(pltpu.SemaphoreType.DMA((n * (N_DEV - 1),)), pltpu.SemaphoreType.DMA((n * (N_DEV - 1),)),
                   pltpu.SemaphoreType.DMA((n,)),
                   *[pltpu.HBM(s.shape, s.dtype) for s in srcs], *[pltpu.HBM(shp, dt) for shp, dt in out_shapes],
                   jax.ShapeDtypeStruct((8, LANE), F32)),
        in_specs=[_HBM] * (2 * n),
        out_specs=(_SEM, _SEM, _SEM, *([_HBM] * (2 * n)), pl.BlockSpec(memory_space=pltpu.VMEM)),
        input_output_aliases={i: 3 + i for i in range(2 * n)},
        compiler_params=pltpu.CompilerParams(has_side_effects=pltpu.SideEffectType.DATAFLOW_SIDE_EFFECTING),
    )(*srcs, *lands)
    handle = (meta, widths, outs[0], outs[1], outs[2], outs[3:3 + n], outs[3 + n:3 + 2 * n])
    return handle, outs[-1]


def _exchange_wait(handle, after, name):
    meta, widths, send_sem, recv_sem, local_sem, src_thru, land_thru = handle
    n = len(meta)

    def body(*refs):
        srcs, lands = refs[:n], refs[n:2 * n]
        send, recv, lsem = refs[2 * n], refs[2 * n + 1], refs[2 * n + 2]
        me, peer, ends = _exchange_refs(meta, widths, srcs, lands)
        for j in range(n):
            for k in range(N_DEV - 1):
                dev, pid = peer(k)
                q = j * (N_DEV - 1) + k
                s, d = ends(j, me, pid)
                pltpu.make_async_remote_copy(s, d, send.at[q], recv.at[q], device_id=dev,
                                             device_id_type=pl.DeviceIdType.MESH).wait_send()
                s, d = ends(j, pid, me)
                pltpu.make_async_remote_copy(s, d, send.at[q], recv.at[q], device_id=dev,
                                             device_id_type=pl.DeviceIdType.MESH).wait_recv()
        for j in range(n):
            s, d = ends(j, me, me)
            pltpu.make_async_copy(s, d, lsem.at[j]).wait()

    outs = pl.pallas_call(
        body, name=name,
        out_shape=tuple(pltpu.HBM(a.shape, a.dtype) for a in (*src_thru, *land_thru)),
        in_specs=[_HBM] * (2 * n) + [_SEM, _SEM, _SEM, pl.BlockSpec(memory_space=pl.ANY)],
        out_specs=tuple([_HBM] * (2 * n)),
        input_output_aliases={i: i for i in range(2 * n)},
        compiler_params=pltpu.CompilerParams(has_side_effects=pltpu.SideEffectType.DATAFLOW_SIDE_EFFECTING),
    )(*src_thru, *land_thru, send_sem, recv_sem, local_sem, after)
    return list(outs[n:])


def _pack(arrs):
    flat = jnp.concatenate([a.reshape(-1).astype(F32) for a in arrs])
    n = flat.shape[0]
    rows = -(-n // (8 * LANE)) * 8
    return jnp.pad(flat, (0, rows * LANE - n)).reshape(rows, LANE)


def _unpack(buf, shapes):
    flat = buf.reshape(-1)
    out, off = [], 0
    for s in shapes:
        sz = int(np.prod(s))
        out.append(flat[off:off + sz].reshape(s))
        off += sz
    return out


def _pad_axis(a, axis, size):
    pad = [(0, 0)] * a.ndim
    pad[axis] = (0, size - a.shape[axis])
    return jnp.pad(a, pad)


def _ffn_forward(x, g, wg, wu, wd, tag):
    a = _rms_fwd(x, g, f"{tag}_rms")
    gate, up, hid = _ffn_up(a, wg, wu, f"{tag}_up")
    out = _mm([(hid, wd)], "nn", F32, f"{tag}_down", scale=0.5, res=x)
    return out, (x, a, gate, up, hid)


def _ffn_backward(dxp, saved, g, wg, wu, wd, tag, emit=None, after=None):
    x, a, gate, up, hid = saved
    d_gate, d_up = _ffn_bwd_hid(dxp, wd, gate, up, f"{tag}_bwd_hid", after)
    d_wd = _mm([(hid, dxp)], "tn", _MXU, f"{tag}_dwd", scale=0.5, tk=2048)
    tok = emit("down", d_wd) if emit is not None else None
    d_wg = _mm([(a, d_gate)], "tn", _MXU, f"{tag}_dwg", tk=2048, after=tok)
    tok = emit("gate", d_wg) if emit is not None else None
    d_wu = _mm([(a, d_up)], "tn", _MXU, f"{tag}_dwu", tk=2048, after=tok)
    tok = emit("up", d_wu) if emit is not None else None
    d_a = _mm([(d_gate, wg), (d_up, wu)], "nt", F32, f"{tag}_da", after=tok)
    dx, dg = _rms_bwd(d_a, x, g, dxp, f"{tag}_rms_bwd")
    return dx, dg, d_wg, d_wu, d_wd


def _tile_vec(v, reps):
    return jnp.tile(v.reshape(1, -1), (1, reps))


def _mixer_forward(x1, p, consts, tag):
    h = _rms_fwd(x1, p["mix_norm"], f"{tag}_rms")
    zg = _mm([(h, p["w_zg"])], "nn", F32, f"{tag}_in_g")
    zc = _mm([(h, p["w_conf"])], "nn", F32, f"{tag}_in_c")
    zs = _mm([(h, p["w_sc"])], "nn", F32, f"{tag}_in_s")
    zw = _mm([(h, p["w_swa"])], "nn", F32, f"{tag}_in_w")
    zf = _mm([(h, p["w_fox"])], "nn", F32, f"{tag}_in_f")
    u1, act_c = _conf_fwd(zc, p["conf_dw"], p["conf_dw_b"], p["conf_ln_g"], p["conf_ln_b"], f"{tag}_conf")
    act_s = _sc_fwd(zs, p["sc_conv"], f"{tag}_sc")
    act_w = _swa_fwd(zw, p["swa_q_norm"], p["swa_k_norm"], p["swa_sink"], consts["bias"], consts["expand"], f"{tag}_swa")
    qa, ka, va, vat = _fox2_prep(zf, p["fox_q_norm"], p["fox_k_norm"], p["b_forget"], consts["sel"], f"{tag}_fox_prep")
    act_f, qb = _fox2_fwd(qa, ka, vat, consts["sel"], f"{tag}_fox")
    acts = (act_c, act_s, act_w, act_f)
    x2, merged = _merge_fwd(acts, zg, p["w_br"], p["w_out"], x1, f"{tag}_merge")
    saved = (x1, h, zg, zc, zs, zw, zf, u1, acts, qb, ka, va, merged)
    return x2, saved


def _mixer_backward(dx2, saved, p, consts, tag, after=None):
    x1, h, zg, zc, zs, zw, zf, u1, acts, qb, ka, va, merged = saved
    g = {}
    g["w_out"] = _mm([(merged, dx2)], "tn", _MXU, f"{tag}_dwout", tk=2048, after=after)
    d_c, d_s, d_w, d_f, dzg, g["w_br"] = _merge_bwd(dx2, acts, zg, p["w_br"], p["w_out"], f"{tag}_merge_bwd")
    du1, sm_c = _conf_bwd_ln(d_c, u1, p["conf_ln_g"], p["conf_ln_b"], f"{tag}_conf_bwd_ln")
    dzc, g["conf_dw"] = _conf_bwd_conv(zc, du1, p["conf_dw"], f"{tag}_conf_bwd_conv")
    g["conf_ln_g"], g["conf_ln_b"], g["conf_dw_b"] = sm_c[0], sm_c[1], sm_c[2]
    dzs, g["sc_conv"] = _sc_bwd(zs, d_s, p["sc_conv"], f"{tag}_sc_bwd")
    dzw, dgq, dgk, g["swa_sink"], g["rel_bias"] = _swa_bwd(
        zw, d_w, p["swa_q_norm"], p["swa_k_norm"], p["swa_sink"], consts["bias"], consts["bucket"], consts["expand"],
        f"{tag}_swa_bwd")
    g["swa_q_norm"], g["swa_k_norm"] = dgq, dgk
    doa = _fox2_bwd_prep(acts[3], d_f, consts["sel"], f"{tag}_fox_bwd_prep")
    dkn, dv, dfk, dqt = _fox2_bwd(qb, ka, va, doa, consts["sel"], f"{tag}_fox_bwd")
    dzf, sm_f = _fox2_post(zf, dqt, dkn, dv, dfk, consts["sel"], p["fox_q_norm"], p["fox_k_norm"], p["b_forget"], f"{tag}_fox_post")
    g["fox_q_norm"], g["fox_k_norm"], g["b_forget"] = sm_f[0], sm_f[1], sm_f[2]
    parts = ((dzg, "w_zg"), (dzc, "w_conf"), (dzs, "w_sc"), (dzw, "w_swa"), (dzf, "w_fox"))
    for dz, wname in parts:
        g[wname] = _mm([(h, dz)], "tn", _MXU, f"{tag}_d{wname}", tk=2048)
    dh = _mm([(dz, p[wname]) for dz, wname in parts], "nt", F32, f"{tag}_dh", tm=512)
    dx1, g["mix_norm"] = _rms_bwd(dh, x1, p["mix_norm"], dx2, f"{tag}_rms_bwd")
    return dx1, g


W_NAMES = ['rel_bias', 'ffn1_norm', 'ffn1_w_gate', 'ffn1_w_up', 'ffn1_w_down', 'mix_norm', 'w_in', 'b_forget', 'conf_dw',
           'conf_dw_b', 'conf_ln_g', 'conf_ln_b', 'conf_w_out', 'sc_conv', 'sc_w_out', 'swa_q_norm', 'swa_k_norm',
           'swa_sink', 'swa_w_o', 'fox_q_norm', 'fox_k_norm', 'fox_w_o', 'w_out', 'ffn2_norm', 'ffn2_w_gate',
           'ffn2_w_up', 'ffn2_w_down']
SMALL = ['rel_bias', 'ffn1_norm', 'mix_norm', 'b_forget', 'conf_dw', 'conf_dw_b', 'conf_ln_g', 'conf_ln_b', 'sc_conv',
         'swa_q_norm', 'swa_k_norm', 'swa_sink', 'fox_q_norm', 'fox_k_norm', 'ffn2_norm']
BRANCH_W = ['conf_w_out', 'sc_w_out', 'swa_w_o', 'fox_w_o']
IN_CONF, IN_SC, IN_SWA, IN_FOX, IN_FF = (0, 512), (512, 1280), (1280, 1792), (1792, 2560), (2560, 2564)


def _step(w, m, v, x, loss_target):
    T, D = x.shape
    L = w["w_out"].shape[0]
    fs = w["ffn1_w_gate"].shape[2]
    fsp = -(-fs // LANE) * LANE
    dev = 4 * lax.axis_index("x") + 2 * lax.axis_index("y") + lax.axis_index("c")

    def cast(a):
        return a.astype(_MXU)

    win = w["w_in"]
    fox_cols = jnp.concatenate([win[..., IN_FOX[0]:IN_FF[1]],
                                jnp.zeros(win.shape[:2] + (LANE - (IN_FF[1] - IN_FF[0]),), win.dtype)], axis=-1)
    shards = {
        "ffn1_w_gate": (cast(_pad_axis(w["ffn1_w_gate"], 2, fsp)), 2),
        "ffn1_w_up": (cast(_pad_axis(w["ffn1_w_up"], 2, fsp)), 2),
        "ffn1_w_down": (cast(_pad_axis(w["ffn1_w_down"], 1, fsp)), 1),
        "ffn2_w_gate": (cast(_pad_axis(w["ffn2_w_gate"], 2, fsp)), 2),
        "ffn2_w_up": (cast(_pad_axis(w["ffn2_w_up"], 2, fsp)), 2),
        "ffn2_w_down": (cast(_pad_axis(w["ffn2_w_down"], 1, fsp)), 1),
        "w_zg": (cast(win[..., IN_FF[1]:]), 1),
        "w_conf": (cast(win[..., IN_CONF[0]:IN_CONF[1]]), 1),
        "w_sc": (cast(win[..., IN_SC[0]:IN_SC[1]]), 1),
        "w_swa": (cast(win[..., IN_SWA[0]:IN_SWA[1]]), 1),
        "w_fox": (cast(fox_cols), 1),
        "w_out": (cast(w["w_out"]), 1),
        "w_br": (cast(jnp.stack([w[n] for n in BRANCH_W], axis=1)), 3),
    }
    big = list(shards)
    conv_shard = jnp.concatenate([jnp.swapaxes(w["conf_dw"], 1, 2), jnp.swapaxes(w["sc_conv"], 1, 2)], axis=2)

    stages = (("ffn1", ["ffn1_w_gate", "ffn1_w_up"]), ("ffn1d", ["ffn1_w_down"]),
              ("mix", ["w_zg", "w_conf", "w_sc", "w_swa", "w_fox", "w_out", "w_br"]),
              ("ffn2", ["ffn2_w_gate", "ffn2_w_up", "ffn2_w_down"]))
    stage_names = dict(stages)
    flight = {}

    def depart(l, st, dep):
        items = [(shards[n][0][l], "gather", shards[n][1] - 1) for n in stage_names[st]]
        if (l, st) == (0, "mix"):
            items.append((conv_shard, "gather", 1))
        if dep is not None:
            src0 = items[0][0]
            zero = (dep[(0,) * dep.ndim].astype(F32) * 0.0).astype(src0.dtype)
            items[0] = (src0 + zero,) + items[0][1:]
        flight[l, st], tok = _exchange_start(items, f"gather_start_l{l}_{st}")
        return tok

    def arrive(l, st, after):
        got = _exchange_wait(flight.pop((l, st)), after, f"gather_wait_l{l}_{st}")
        params[l].update(zip(stage_names[st], got))
        if (l, st) == (0, "mix"):
            conv_full = jnp.swapaxes(got[-1], 1, 2)
            for i, q in enumerate(params):
                q["conf_dw"] = _pad_axis(conv_full[i, :CONV_K], 0, CONV_HALO)
                q["sc_conv"] = _pad_axis(conv_full[i, CONV_K:], 0, SC_HALO)
        return got[0]

    bucket = jnp.asarray(_swa_bucket_matrix(min(SWA_TQ, T)))
    consts = {"bucket": bucket, "expand": jnp.asarray(_kv_expand_matrix(), _MXU),
              "sel": jnp.asarray(_head_select_matrix(), _MXU),
              "bias": _swa_bias(w["rel_bias"], bucket, "swa_bias")}

    def layer_params(l):
        p = {}
        for n in ("ffn1_norm", "mix_norm", "ffn2_norm", "conf_dw_b", "conf_ln_g", "conf_ln_b"):
            p[n] = w[n][l].reshape(1, -1)
        p["swa_q_norm"], p["fox_q_norm"] = _tile_vec(w["swa_q_norm"][l], 4), _tile_vec(w["fox_q_norm"][l], 4)
        p["swa_k_norm"], p["fox_k_norm"] = _tile_vec(w["swa_k_norm"][l], 2), _tile_vec(w["fox_k_norm"][l], 4)
        p["swa_sink"] = w["swa_sink"][l].reshape(1, 4)
        p["b_forget"] = _pad_axis(w["b_forget"][l].reshape(1, 4), 1, LANE)
        return p

    params = [layer_params(l) for l in range(L)]
    saved = [None] * L
    cur = x
    first = depart(0, "ffn1", None)
    for l, p in enumerate(params):
        got = arrive(l, "ffn1", cur if l else first)
        zero = (depart(l, "ffn1d", got) + depart(l, "mix", got))[0:1, 0:1]
        a = _rms_fwd(cur, p["ffn1_norm"] + zero, f"l{l}_ffn1_rms")
        gate, up, hid = _ffn_up(a, p["ffn1_w_gate"], p["ffn1_w_up"], f"l{l}_ffn1_up")
        got = arrive(l, "ffn1d", hid)
        tok = depart(l, "ffn2", got)
        x1 = _mm([(hid, p["ffn1_w_down"])], "nn", F32, f"l{l}_ffn1_down", scale=0.5, res=cur, after=tok)
        s1 = (cur, a, gate, up, hid)
        got = arrive(l, "mix", x1)
        zero = depart(l + 1, "ffn1", got)[0:1, 0:1] if l + 1 < L else 0.0
        x2, s2 = _mixer_forward(x1, dict(p, mix_norm=p["mix_norm"] + zero), consts, f"l{l}_mix")
        arrive(l, "ffn2", x2)
        cur, s3 = _ffn_forward(x2, p["ffn2_norm"], p["ffn2_w_gate"], p["ffn2_w_up"], p["ffn2_w_down"], f"l{l}_ffn2")
        saved[l] = (s1, s2, s3)
    dcur, loss_part = _loss_grad(cur, loss_target)

    grads = [None] * L
    leaving = []

    def leave(l, st, names, g):
        h, tok = _exchange_start([(g[n], "scatter", shards[n][1] - 1) for n in names], f"scatter_start_l{l}_{st}")
        leaving.append((l, st, names, h))
        return tok

    tok = None
    for l in reversed(range(L)):
        p = params[l]
        s1, s2, s3 = saved[l]
        g = {}
        dcur, g["ffn2_norm"], g["ffn2_w_gate"], g["ffn2_w_up"], g["ffn2_w_down"] = _ffn_backward(
            dcur, s3, p["ffn2_norm"], p["ffn2_w_gate"], p["ffn2_w_up"], p["ffn2_w_down"], f"l{l}_ffn2", after=tok)
        tok = leave(l, "ffn2", stage_names["ffn2"], g)
        dcur, gm = _mixer_backward(dcur, s2, p, consts, f"l{l}_mix", after=tok)
        g.update(gm)
        tok = leave(l, "mix", stage_names["mix"], g)
        ffn1 = ["ffn1_w_gate", "ffn1_w_up", "ffn1_w_down"]
        emit = (lambda which, arr, l=l: leave(l, which, [f"ffn1_w_{which}"], {f"ffn1_w_{which}": arr})) if l == 0 else None
        dcur, g["ffn1_norm"], g["ffn1_w_gate"], g["ffn1_w_up"], g["ffn1_w_down"] = _ffn_backward(
            dcur, s1, p["ffn1_norm"], p["ffn1_w_gate"], p["ffn1_w_up"], p["ffn1_w_down"], f"l{l}_ffn1", emit, after=tok)
        if emit is None:
            tok = leave(l, "ffn1", ffn1, g)
        grads[l] = g
    grad_x = dcur

    gsum = {n: [None] * L for n in big}
    for l, st, names, h in leaving:
        for n, r in zip(names, _exchange_wait(h, grad_x, f"scatter_wait_l{l}_{st}")):
            gsum[n][l] = _sum8(r, f"sum_{n}_l{l}")
    gsum = {n: jnp.stack(parts) for n, parts in gsum.items()}
    gw = {}
    for n in ("ffn1_w_gate", "ffn1_w_up", "ffn2_w_gate", "ffn2_w_up"):
        gw[n] = gsum[n][:, :, :fs]
    for n in ("ffn1_w_down", "ffn2_w_down"):
        gw[n] = gsum[n][:, :fs, :]
    gw["w_out"] = gsum["w_out"]
    for i, n in enumerate(BRANCH_W):
        gw[n] = gsum["w_br"][:, i]
    gw["w_in"] = jnp.concatenate([gsum["w_conf"], gsum["w_sc"], gsum["w_swa"],
                                  gsum["w_fox"][..., :IN_FF[1] - IN_FOX[0]], gsum["w_zg"]], axis=-1)

    def small_partial(n):
        per_layer = [grads[l][n] for l in range(L)]
        if n == "rel_bias":
            return sum(pl_[:, :4] for pl_ in per_layer)
        if n in ("swa_sink", "b_forget"):
            return jnp.stack([a.reshape(-1)[:4] for a in per_layer])
        if n in ("swa_q_norm", "fox_q_norm", "fox_k_norm"):
            return jnp.stack([a.reshape(4, HEAD).sum(0) for a in per_layer])
        if n == "swa_k_norm":
            return jnp.stack([a.reshape(2, HEAD).sum(0) for a in per_layer])
        if n == "conf_dw":
            return jnp.stack([a[:CONV_K] for a in per_layer])
        if n == "sc_conv":
            return jnp.stack([a[:SC_K] for a in per_layer])
        return jnp.stack([a.reshape(-1) for a in per_layer])

    partial = [small_partial(n) for n in SMALL]
    small_shapes = [a.shape for a in partial]
    all_parts = _exchange([(_pack(partial), "gather", 0)], "gather_small_grads")[0]
    rows = all_parts.shape[0] // N_DEV
    small_sum = _unpack(_sum8(all_parts.reshape(N_DEV, rows, LANE), "sum_small"), small_shapes)
    for n, a in zip(SMALL, small_sum):
        if n in ("conf_dw", "sc_conv"):
            cs = w[n].shape[2]
            a = lax.dynamic_slice_in_dim(a, dev * cs, cs, axis=2)
        gw[n] = a

    delta, new_m, new_v = {}, {}, {}
    for n in W_NAMES:
        if n not in SMALL:
            delta[n], new_m[n], new_v[n] = _adamw(w[n], gw[n], m[n], v[n], f"adamw_{n}")
    shapes = [w[n].shape for n in SMALL]
    outs = _adamw(_pack([w[n] for n in SMALL]), _pack([gw[n] for n in SMALL]), _pack([m[n] for n in SMALL]),
                  _pack([v[n] for n in SMALL]), "adamw_small")
    for res, out in zip((delta, new_m, new_v), outs):
        for n, a in zip(SMALL, _unpack(out, shapes)):
            res[n] = a

    loss = lax.psum(loss_part[0, 0], ("x", "y", "c"))
    return loss, grad_x, gw, delta, new_m, new_v


def kernel(x, rel_bias, ffn1_norm, ffn1_w_gate, ffn1_w_up, ffn1_w_down, mix_norm, w_in, b_forget, conf_dw, conf_dw_b, conf_ln_g, conf_ln_b, conf_w_out, sc_conv, sc_w_out, swa_q_norm, swa_k_norm, swa_sink, swa_w_o, fox_q_norm, fox_k_norm, fox_w_o, w_out, ffn2_norm, ffn2_w_gate, ffn2_w_up, ffn2_w_down, loss_target, m_rel_bias, m_ffn1_norm, m_ffn1_w_gate, m_ffn1_w_up, m_ffn1_w_down, m_mix_norm, m_w_in, m_b_forget, m_conf_dw, m_conf_dw_b, m_conf_ln_g, m_conf_ln_b, m_conf_w_out, m_sc_conv, m_sc_w_out, m_swa_q_norm, m_swa_k_norm, m_swa_sink, m_swa_w_o, m_fox_q_norm, m_fox_k_norm, m_fox_w_o, m_w_out, m_ffn2_norm, m_ffn2_w_gate, m_ffn2_w_up, m_ffn2_w_down, v_rel_bias, v_ffn1_norm, v_ffn1_w_gate, v_ffn1_w_up, v_ffn1_w_down, v_mix_norm, v_w_in, v_b_forget, v_conf_dw, v_conf_dw_b, v_conf_ln_g, v_conf_ln_b, v_conf_w_out, v_sc_conv, v_sc_w_out, v_swa_q_norm, v_swa_k_norm, v_swa_sink, v_swa_w_o, v_fox_q_norm, v_fox_k_norm, v_fox_w_o, v_w_out, v_ffn2_norm, v_ffn2_w_gate, v_ffn2_w_up, v_ffn2_w_down):
    args = locals()
    w = {n: args[n] for n in W_NAMES}
    m = {n: args["m_" + n] for n in W_NAMES}
    v = {n: args["v_" + n] for n in W_NAMES}
    T, D = x.shape[-2:]
    loss, grad_x, gw, delta, new_m, new_v = _step(w, m, v, x.reshape(T, D), loss_target.reshape(T, D))
    return (loss, grad_x.reshape(x.shape), *[gw[n] for n in W_NAMES], *[delta[n] for n in W_NAMES],
            *[new_m[n] for n in W_NAMES], *[new_v[n] for n in W_NAMES])
```

```python
import math

import numpy as np
import jax
import jax.numpy as jnp
from jax import lax
from jax.experimental import pallas as pl
from jax.experimental.pallas import tpu as pltpu

F32 = jnp.float32
_MXU = jnp.bfloat16
EPS = 1e-6
NEG_INF = -1e30
HEAD = 64
CH = 256
WINDOW = 128
CONV_K = 31
SC_K = 3
CONV_HALO = 32
SC_HALO = 8
N_BUCKETS = 32
MAX_DISTANCE = 128
N_DEV = 8
LANE = 128
ROW_TILE = 512
VMEM_LIMIT = 48 * 1024 * 1024
ADAM_LR, ADAM_B1, ADAM_B2, ADAM_EPS, ADAM_WD, ADAM_STEP = 0.001, 0.9, 0.999, 1e-08, 0.01, 10

_NN = (((1,), (0,)), ((), ()))
_NT = (((1,), (1,)), ((), ()))
_TN = (((0,), (0,)), ((), ()))


def _cp(*sem):
    return pltpu.CompilerParams(dimension_semantics=sem, vmem_limit_bytes=VMEM_LIMIT)


def _tile(n, pref, align=LANE):
    t = (min(n, pref) // align) * align
    while t >= align:
        if n % t == 0:
            return t
        t -= align
    return n


def _dot(a, b, dims=_NN):
    return lax.dot_general(a.astype(_MXU), b.astype(_MXU), dims, preferred_element_type=F32)


def _split3(x):
    hi = x.astype(_MXU)
    r1 = x - hi.astype(F32)
    mid = r1.astype(_MXU)
    lo = (r1 - mid.astype(F32)).astype(_MXU)
    return hi, mid, lo


def _exact_dot(a, b, dims, data):
    if data == "a":
        return sum(lax.dot_general(p, b.astype(_MXU), dims, preferred_element_type=F32) for p in _split3(a))
    return sum(lax.dot_general(a.astype(_MXU), p, dims, preferred_element_type=F32) for p in _split3(b))


def _sigmoid(x):
    return jax.nn.sigmoid(x)


def _lane_mask(width, h):
    lane = lax.broadcasted_iota(jnp.int32, (1, width), 1)
    return (lane >= h * HEAD) & (lane < (h + 1) * HEAD)


def _head_rms(x, g, nh):
    xx = x * x
    ms = jnp.zeros_like(x)
    for h in range(nh):
        mk = _lane_mask(x.shape[-1], h)
        s = jnp.sum(jnp.where(mk, xx, 0.0), axis=-1, keepdims=True) * (1.0 / HEAD)
        ms = jnp.where(mk, s, ms)
    r = lax.rsqrt(ms + EPS)
    return x * r * g, r


def _head_rms_bwd(dy, x, r, g, nh):
    w = dy * g
    wx = w * x
    c = jnp.zeros_like(x)
    for h in range(nh):
        mk = _lane_mask(x.shape[-1], h)
        s = jnp.sum(jnp.where(mk, wx, 0.0), axis=-1, keepdims=True) * (1.0 / HEAD)
        c = jnp.where(mk, s, c)
    dx = r * w - x * (r * r * r) * c
    dg = jnp.sum(dy * x * r, axis=0, keepdims=True)
    return dx, dg


def _mm(pairs, mode, out_dtype, name, scale=None, res=None, tm=1024, tn=1024, tk=1024, after=None):
    a0, b0 = pairs[0]
    M = a0.shape[1] if mode == "tn" else a0.shape[0]
    N = b0.shape[0] if mode == "nt" else b0.shape[1]
    tm, tn = _tile(M, tm), _tile(N, tn)
    dims = {"nn": _NN, "nt": _NT, "tn": _TN}[mode]
    tks, nks, offs = [], [], []
    for a, _ in pairs:
        K = a.shape[0] if mode == "tn" else a.shape[1]
        t = _tile(K, tk)
        tks.append(t)
        nks.append(K // t)
        offs.append(sum(nks[:-1]))
    nk_tot = sum(nks)
    in_specs, args = [], []
    for (a, b), t, nk, off in zip(pairs, tks, nks, offs):
        def kk(k, off=off, nk=nk):
            return jnp.clip(k - off, 0, nk - 1)
        if mode == "tn":
            in_specs.append(pl.BlockSpec((t, tm), lambda i, j, k, kk=kk: (kk(k), i)))
        else:
            in_specs.append(pl.BlockSpec((tm, t), lambda i, j, k, kk=kk: (i, kk(k))))
        if mode == "nt":
            in_specs.append(pl.BlockSpec((tn, t), lambda i, j, k, kk=kk: (j, kk(k))))
        else:
            in_specs.append(pl.BlockSpec((t, tn), lambda i, j, k, kk=kk: (kk(k), j)))
        args += [a, b]
    if res is not None:
        in_specs.append(pl.BlockSpec((tm, tn), lambda i, j, k: (i, j)))
        args.append(res)
    if after is not None:
        in_specs.append(pl.BlockSpec(memory_space=pl.ANY))
        args.append(after)
    npairs = len(pairs)

    def body(*refs):
        ab = refs[:2 * npairs]
        res_ref = refs[2 * npairs] if res is not None else None
        o_ref = refs[2 * npairs + (res is not None) + (after is not None)]
        acc = refs[-1]
        k = pl.program_id(2)

        def finish(r):
            if scale is not None:
                r = r * scale
            if res_ref is not None:
                r = r + res_ref[...]
            o_ref[...] = r.astype(o_ref.dtype)

        if nk_tot == 1:
            finish(_dot(ab[0][...], ab[1][...], dims))
            return

        @pl.when(k == 0)
        def _():
            acc[...] = jnp.zeros_like(acc)

        for p in range(npairs):
            @pl.when(jnp.logical_and(k >= offs[p], k < offs[p] + nks[p]))
            def _(p=p):
                acc[...] += _dot(ab[2 * p][...], ab[2 * p + 1][...], dims)

        @pl.when(k == nk_tot - 1)
        def _():
            finish(acc[...])

    return pl.pallas_call(
        body, name=name, grid=(M // tm, N // tn, nk_tot), in_specs=in_specs,
        out_specs=pl.BlockSpec((tm, tn), lambda i, j, k: (i, j)),
        out_shape=jax.ShapeDtypeStruct((M, N), out_dtype),
        scratch_shapes=[pltpu.VMEM((tm, tn), F32)],
        compiler_params=_cp("parallel", "parallel", "arbitrary"))(*args)


def _rms_fwd(x, g, name):
    T, D = x.shape
    tm = _tile(T, ROW_TILE)

    def body(x_ref, g_ref, o_ref):
        xv = x_ref[...]
        r = lax.rsqrt(jnp.mean(xv * xv, axis=-1, keepdims=True) + EPS)
        o_ref[...] = (xv * r * g_ref[...]).astype(o_ref.dtype)

    return pl.pallas_call(
        body, name=name, grid=(T // tm,),
        in_specs=[pl.BlockSpec((tm, D), lambda i: (i, 0)), pl.BlockSpec((1, D), lambda i: (0, 0))],
        out_specs=pl.BlockSpec((tm, D), lambda i: (i, 0)),
        out_shape=jax.ShapeDtypeStruct((T, D), _MXU), compiler_params=_cp("parallel"))(x, g)


def _rms_bwd(da, x, g, dres, name):
    T, D = x.shape
    tm = _tile(T, ROW_TILE)

    def body(da_ref, x_ref, g_ref, dr_ref, dx_ref, dg_ref):
        @pl.when(pl.program_id(0) == 0)
        def _():
            dg_ref[...] = jnp.zeros_like(dg_ref)

        xv, dav = x_ref[...], da_ref[...]
        r = lax.rsqrt(jnp.mean(xv * xv, axis=-1, keepdims=True) + EPS)
        w = dav * g_ref[...]
        c = jnp.mean(w * xv, axis=-1, keepdims=True)
        dx_ref[...] = dr_ref[...] + (r * w - xv * (r * r * r) * c)
        dg_ref[...] += jnp.sum(dav * xv * r, axis=0, keepdims=True)

    row = pl.BlockSpec((tm, D), lambda i: (i, 0))
    vec = pl.BlockSpec((1, D), lambda i: (0, 0))
    return pl.pallas_call(
        body, name=name, grid=(T // tm,), in_specs=[row, row, vec, row], out_specs=[row, vec],
        out_shape=[jax.ShapeDtypeStruct((T, D), F32), jax.ShapeDtypeStruct((1, D), F32)],
        compiler_params=_cp("arbitrary"))(da, x, g, dres)


def _loss_grad(y, tgt):
    T, D = y.shape
    tm = _tile(T, ROW_TILE)

    def body(y_ref, t_ref, dy_ref, l_ref):
        @pl.when(pl.program_id(0) == 0)
        def _():
            l_ref[...] = jnp.zeros_like(l_ref)

        d = y_ref[...] - t_ref[...]
        dy_ref[...] = d * (1.0 / D)
        per_tok = jnp.mean(d * d, axis=-1, keepdims=True)
        l_ref[...] += 0.5 * jnp.sum(per_tok, axis=0, keepdims=True)

    row = pl.BlockSpec((tm, D), lambda i: (i, 0))
    return pl.pallas_call(
        body, name="loss_grad", grid=(T // tm,), in_specs=[row, row],
        out_specs=[row, pl.BlockSpec((1, 1), lambda i: (0, 0))],
        out_shape=[jax.ShapeDtypeStruct((T, D), F32), jax.ShapeDtypeStruct((1, 1), F32)],
        compiler_params=_cp("arbitrary"))(y, tgt)


def _ffn_up(a, wg, wu, name, after=None):
    T, D = a.shape
    Fp = wg.shape[1]
    tm, tn = _tile(T, ROW_TILE), _tile(Fp, 768)
    extra = [] if after is None else [after]

    def body(*refs):
        a_ref, wg_ref, wu_ref = refs[:3]
        g_ref, u_ref, h_ref = refs[-3:]
        av = a_ref[...]
        g = _dot(av, wg_ref[...])
        u = _dot(av, wu_ref[...])
        g_ref[...] = g.astype(g_ref.dtype)
        u_ref[...] = u.astype(u_ref.dtype)
        h_ref[...] = (g * _sigmoid(g) * u).astype(h_ref.dtype)

    wspec = pl.BlockSpec((D, tn), lambda j, i: (0, j))
    ospec = pl.BlockSpec((tm, tn), lambda j, i: (i, j))
    osh = jax.ShapeDtypeStruct((T, Fp), _MXU)
    return pl.pallas_call(
        body, name=name, grid=(Fp // tn, T // tm),
        in_specs=[pl.BlockSpec((tm, D), lambda j, i: (i, 0)), wspec, wspec] + [pl.BlockSpec(memory_space=pl.ANY)] * len(extra),
        out_specs=[ospec, ospec, ospec], out_shape=[osh, osh, osh],
        compiler_params=_cp("parallel", "parallel"))(a, wg, wu, *extra)


def _ffn_bwd_hid(dxp, wd, gate, up, name, after=None):
    T, D = dxp.shape
    Fp = wd.shape[0]
    tm, tn = _tile(T, ROW_TILE), _tile(Fp, 768)
    extra = [] if after is None else [after]

    def body(*refs):
        dx_ref, wd_ref, g_ref, u_ref = refs[:4]
        dg_ref, du_ref = refs[-2:]
        dh = 0.5 * _dot(dx_ref[...], wd_ref[...], _NT)
        g = g_ref[...].astype(F32)
        u = u_ref[...].astype(F32)
        s = _sigmoid(g)
        du_ref[...] = (dh * (g * s)).astype(du_ref.dtype)
        dg_ref[...] = (dh * u * (s * (1.0 + g * (1.0 - s)))).astype(dg_ref.dtype)

    tspec = pl.BlockSpec((tm, tn), lambda j, i: (i, j))
    osh = jax.ShapeDtypeStruct((T, Fp), _MXU)
    return pl.pallas_call(
        body, name=name, grid=(Fp // tn, T // tm),
        in_specs=[pl.BlockSpec((tm, D), lambda j, i: (i, 0)), pl.BlockSpec((tn, D), lambda j, i: (j, 0)), tspec, tspec]
        + [pl.BlockSpec(memory_space=pl.ANY)] * len(extra),
        out_specs=[tspec, tspec], out_shape=[osh, osh],
        compiler_params=_cp("parallel", "parallel"))(dxp, wd, gate, up, *extra)


def _conf_fwd(zc, dw, b, lng, lnb, name):
    T = zc.shape[0]
    tm = _tile(T, ROW_TILE)
    r = tm // CONV_HALO

    def body(z_ref, zh_ref, dw_ref, b_ref, g_ref, lb_ref, u1_ref, act_ref, ext):
        i = pl.program_id(0)
        cur = z_ref[...]
        ext[pl.ds(CONV_HALO, tm), :] = cur[:, :CH] * _sigmoid(cur[:, CH:])
        hal = zh_ref[...]
        ext[pl.ds(0, CONV_HALO), :] = jnp.where(i > 0, hal[:, :CH] * _sigmoid(hal[:, CH:]), 0.0)
        acc = jnp.zeros((tm, CH), F32)
        for k in range(CONV_K):
            acc = acc + dw_ref[pl.ds(k, 1), :] * ext[pl.ds(CONV_HALO - (CONV_K - 1) + k, tm), :]
        u1 = acc + b_ref[...]
        u1_ref[...] = u1
        mu = jnp.mean(u1, axis=-1, keepdims=True)
        var = jnp.mean(jnp.square(u1 - mu), axis=-1, keepdims=True)
        u2 = (u1 - mu) * lax.rsqrt(var + EPS) * g_ref[...] + lb_ref[...]
        act_ref[...] = u2 * _sigmoid(u2)

    vec = pl.BlockSpec((1, CH), lambda i: (0, 0))
    row = pl.BlockSpec((tm, CH), lambda i: (i, 0))
    osh = jax.ShapeDtypeStruct((T, CH), F32)
    return pl.pallas_call(
        body, name=name, grid=(T // tm,),
        in_specs=[pl.BlockSpec((tm, 2 * CH), lambda i: (i, 0)),
                  pl.BlockSpec((CONV_HALO, 2 * CH), lambda i: (jnp.maximum(i * r - 1, 0), 0)),
                  pl.BlockSpec((CONV_HALO, CH), lambda i: (0, 0)), vec, vec, vec],
        out_specs=[row, row], out_shape=[osh, osh],
        scratch_shapes=[pltpu.VMEM((tm + CONV_HALO, CH), F32)],
        compiler_params=_cp("parallel"))(zc, zc, dw, b, lng, lnb)


def _conf_bwd_ln(dact, u1, lng, lnb, name):
    T = u1.shape[0]
    tm = _tile(T, ROW_TILE)

    def body(da_ref, u_ref, g_ref, lb_ref, du_ref, sm_ref):
        @pl.when(pl.program_id(0) == 0)
        def _():
            sm_ref[...] = jnp.zeros_like(sm_ref)

        u1v = u_ref[...]
        mu = jnp.mean(u1v, axis=-1, keepdims=True)
        cen = u1v - mu
        rstd = lax.rsqrt(jnp.mean(cen * cen, axis=-1, keepdims=True) + EPS)
        y = cen * rstd
        u2 = y * g_ref[...] + lb_ref[...]
        s = _sigmoid(u2)
        du2 = da_ref[...] * (s * (1.0 + u2 * (1.0 - s)))
        dy = du2 * g_ref[...]
        du1 = rstd * (dy - jnp.mean(dy, axis=-1, keepdims=True) - y * jnp.mean(dy * y, axis=-1, keepdims=True))
        du_ref[...] = du1
        sm_ref[pl.ds(0, 1), :] += jnp.sum(du2 * y, axis=0, keepdims=True)
        sm_ref[pl.ds(1, 1), :] += jnp.sum(du2, axis=0, keepdims=True)
        sm_ref[pl.ds(2, 1), :] += jnp.sum(du1, axis=0, keepdims=True)

    vec = pl.BlockSpec((1, CH), lambda i: (0, 0))
    row = pl.BlockSpec((tm, CH), lambda i: (i, 0))
    return pl.pallas_call(
        body, name=name, grid=(T // tm,), in_specs=[row, row, vec, vec],
        out_specs=[row, pl.BlockSpec((8, CH), lambda i: (0, 0))],
        out_shape=[jax.ShapeDtypeStruct((T, CH), F32), jax.ShapeDtypeStruct((8, CH), F32)],
        compiler_params=_cp("arbitrary"))(dact, u1, lng, lnb)


def _conf_bwd_conv(zc, du1, dw, name):
    T = zc.shape[0]
    tm = _tile(T, ROW_TILE)
    r = tm // CONV_HALO
    nt = T // tm
    nh = T // CONV_HALO

    def body(z_ref, zh_ref, d_ref, dn_ref, dw_ref, dz_ref, ddw_ref, ext_u, ext_d):
        i = pl.program_id(0)

        @pl.when(i == 0)
        def _():
            ddw_ref[...] = jnp.zeros_like(ddw_ref)

        cur = z_ref[...]
        ca = cur[:, :CH]
        sg = _sigmoid(cur[:, CH:])
        ext_u[pl.ds(CONV_HALO, tm), :] = ca * sg
        hal = zh_ref[...]
        ext_u[pl.ds(0, CONV_HALO), :] = jnp.where(i > 0, hal[:, :CH] * _sigmoid(hal[:, CH:]), 0.0)
        d = d_ref[...]
        ext_d[pl.ds(0, tm), :] = d
        ext_d[pl.ds(tm, CONV_HALO), :] = jnp.where(i < nt - 1, dn_ref[...], 0.0)
        acc = jnp.zeros((tm, CH), F32)
        for k in range(CONV_K):
            acc = acc + dw_ref[pl.ds(k, 1), :] * ext_d[pl.ds(CONV_K - 1 - k, tm), :]
            ddw_ref[pl.ds(k, 1), :] += jnp.sum(
                d * ext_u[pl.ds(CONV_HALO - (CONV_K - 1) + k, tm), :], axis=0, keepdims=True)
        dz_ref[:, :CH] = (acc * sg).astype(dz_ref.dtype)
        dz_ref[:, CH:] = (acc * ca * sg * (1.0 - sg)).astype(dz_ref.dtype)

    return pl.pallas_call(
        body, name=name, grid=(nt,),
        in_specs=[pl.BlockSpec((tm, 2 * CH), lambda i: (i, 0)),
                  pl.BlockSpec((CONV_HALO, 2 * CH), lambda i: (jnp.maximum(i * r - 1, 0), 0)),
                  pl.BlockSpec((tm, CH), lambda i: (i, 0)),
                  pl.BlockSpec((CONV_HALO, CH), lambda i: (jnp.minimum((i + 1) * r, nh - 1), 0)),
                  pl.BlockSpec((CONV_HALO, CH), lambda i: (0, 0))],
        out_specs=[pl.BlockSpec((tm, 2 * CH), lambda i: (i, 0)), pl.BlockSpec((CONV_HALO, CH), lambda i: (0, 0))],
        out_shape=[jax.ShapeDtypeStruct((T, 2 * CH), _MXU), jax.ShapeDtypeStruct((CONV_HALO, CH), F32)],
        scratch_shapes=[pltpu.VMEM((tm + CONV_HALO, CH), F32), pltpu.VMEM((tm + CONV_HALO, CH), F32)],
        compiler_params=_cp("arbitrary"))(zc, zc, du1, du1, dw)


def _sc_fwd(zs, w, name):
    T = zs.shape[0]
    tm = _tile(T, ROW_TILE)
    r = tm // SC_HALO

    def body(z_ref, zh_ref, w_ref, act_ref, ext):
        i = pl.program_id(0)
        cur = z_ref[...]
        ext[pl.ds(SC_HALO, tm), :] = cur[:, CH:2 * CH] * cur[:, 2 * CH:]
        hal = zh_ref[...]
        ext[pl.ds(0, SC_HALO), :] = jnp.where(i > 0, hal[:, CH:2 * CH] * hal[:, 2 * CH:], 0.0)
        v1 = jnp.zeros((tm, CH), F32)
        for k in range(SC_K):
            v1 = v1 + w_ref[pl.ds(k, 1), :] * ext[pl.ds(SC_HALO - (SC_K - 1) + k, tm), :]
        act_ref[...] = cur[:, :CH] * v1

    return pl.pallas_call(
        body, name=name, grid=(T // tm,),
        in_specs=[pl.BlockSpec((tm, 3 * CH), lambda i: (i, 0)),
                  pl.BlockSpec((SC_HALO, 3 * CH), lambda i: (jnp.maximum(i * r - 1, 0), 0)),
                  pl.BlockSpec((SC_HALO, CH), lambda i: (0, 0))],
        out_specs=pl.BlockSpec((tm, CH), lambda i: (i, 0)),
        out_shape=jax.ShapeDtypeStruct((T, CH), F32),
        scratch_shapes=[pltpu.VMEM((tm + SC_HALO, CH), F32)],
        compiler_params=_cp("parallel"))(zs, zs, w)


def _sc_bwd(zs, dact, w, name):
    T = zs.shape[0]
    tm = _tile(T, ROW_TILE)
    r = tm // SC_HALO
    nt = T // tm
    nh = T // SC_HALO

    def body(z_ref, zh_ref, zn_ref, d_ref, dn_ref, w_ref, dz_ref, dw_ref, ext_v, ext_d):
        i = pl.program_id(0)

        @pl.when(i == 0)
        def _():
            dw_ref[...] = jnp.zeros_like(dw_ref)

        cur = z_ref[...]
        sb, sc, sx = cur[:, :CH], cur[:, CH:2 * CH], cur[:, 2 * CH:]
        ext_v[pl.ds(SC_HALO, tm), :] = sc * sx
        hal = zh_ref[...]
        ext_v[pl.ds(0, SC_HALO), :] = jnp.where(i > 0, hal[:, CH:2 * CH] * hal[:, 2 * CH:], 0.0)
        da = d_ref[...]
        dv1 = da * sb
        ext_d[pl.ds(0, tm), :] = dv1
        ext_d[pl.ds(tm, SC_HALO), :] = jnp.where(i < nt - 1, dn_ref[...] * zn_ref[...][:, :CH], 0.0)
        v1 = jnp.zeros((tm, CH), F32)
        dv0 = jnp.zeros((tm, CH), F32)
        for k in range(SC_K):
            shifted = ext_v[pl.ds(SC_HALO - (SC_K - 1) + k, tm), :]
            v1 = v1 + w_ref[pl.ds(k, 1), :] * shifted
            dv0 = dv0 + w_ref[pl.ds(k, 1), :] * ext_d[pl.ds(SC_K - 1 - k, tm), :]
            dw_ref[pl.ds(k, 1), :] += jnp.sum(dv1 * shifted, axis=0, keepdims=True)
        dz_ref[:, :CH] = (da * v1).astype(dz_ref.dtype)
        dz_ref[:, CH:2 * CH] = (dv0 * sx).astype(dz_ref.dtype)
        dz_ref[:, 2 * CH:] = (dv0 * sc).astype(dz_ref.dtype)

    return pl.pallas_call(
        body, name=name, grid=(nt,),
        in_specs=[pl.BlockSpec((tm, 3 * CH), lambda i: (i, 0)),
                  pl.BlockSpec((SC_HALO, 3 * CH), lambda i: (jnp.maximum(i * r - 1, 0), 0)),
                  pl.BlockSpec((SC_HALO, 3 * CH), lambda i: (jnp.minimum((i + 1) * r, nh - 1), 0)),
                  pl.BlockSpec((tm, CH), lambda i: (i, 0)),
                  pl.BlockSpec((SC_HALO, CH), lambda i: (jnp.minimum((i + 1) * r, nh - 1), 0)),
                  pl.BlockSpec((SC_HALO, CH), lambda i: (0, 0))],
        out_specs=[pl.BlockSpec((tm, 3 * CH), lambda i: (i, 0)), pl.BlockSpec((SC_HALO, CH), lambda i: (0, 0))],
        out_shape=[jax.ShapeDtypeStruct((T, 3 * CH), _MXU), jax.ShapeDtypeStruct((SC_HALO, CH), F32)],
        scratch_shapes=[pltpu.VMEM((tm + SC_HALO, CH), F32), pltpu.VMEM((tm + SC_HALO, CH), F32)],
        compiler_params=_cp("arbitrary"))(zs, zs, zs, dact, dact, w)


SWA_TQ = 256


def _t5_bucket_np(dist):
    max_exact = N_BUCKETS // 2
    d = np.maximum(dist, 1).astype(np.float32)
    large = max_exact + (np.log(d / np.float32(max_exact)) / np.float32(math.log(MAX_DISTANCE / max_exact))
                         * np.float32(N_BUCKETS - max_exact)).astype(np.int32)
    large = np.minimum(large, N_BUCKETS - 1)
    return np.where(dist < max_exact, dist, large).astype(np.int32)


def _swa_bucket_matrix(tq):
    dist = WINDOW + np.arange(tq)[:, None] - np.arange(tq + WINDOW)[None, :]
    ok = (dist >= 0) & (dist < WINDOW)
    return np.where(ok, _t5_bucket_np(np.maximum(dist, 0)), -1).astype(np.int32)


def _kv_expand_matrix():
    e = np.zeros((2 * HEAD, 4 * HEAD), np.float32)
    for h in range(4):
        for d in range(HEAD):
            e[(h // 2) * HEAD + d, h * HEAD + d] = 1.0
    return e


def _swa_bias(rel_bias, bucket, name):
    tq, tk = bucket.shape

    def body(rb_ref, bk_ref, o_ref):
        h = pl.program_id(0)
        bk = bk_ref[...]
        acc = jnp.full((tq, tk), NEG_INF, F32)
        for b in range(N_BUCKETS):
            acc = jnp.where(bk == b, rb_ref[b, h], acc)
        o_ref[0] = acc

    return pl.pallas_call(
        body, name=name, grid=(4,),
        in_specs=[pl.BlockSpec(memory_space=pltpu.SMEM), pl.BlockSpec((tq, tk), lambda h: (0, 0))],
        out_specs=pl.BlockSpec((1, tq, tk), lambda h: (h, 0, 0)),
        out_shape=jax.ShapeDtypeStruct((4, tq, tk), F32), compiler_params=_cp("parallel"))(rel_bias, bucket)


def _swa_probs(qh, kx, bm, first_col, sk):
    s = _dot(qh, kx, _NT) * (HEAD ** -0.5)
    col = lax.broadcasted_iota(jnp.int32, s.shape, 1)
    valid = (bm > 0.5 * NEG_INF) & (col >= first_col)
    s = jnp.where(valid, s + bm, NEG_INF)
    m = jnp.maximum(jnp.max(s, axis=-1, keepdims=True), sk)
    p = jnp.exp(s - m)
    den = jnp.sum(p, axis=-1, keepdims=True) + jnp.exp(sk - m)
    return p / den, m, den


def _swa_fwd(zw, gq, gk, sink, bias, expand, name):
    T = zw.shape[0]
    tq = bias.shape[1]
    r = tq // WINDOW

    def body(z_ref, zh_ref, gq_ref, gk_ref, sink_ref, b_ref, e_ref, o_ref, kext, vext):
        i = pl.program_id(0)
        cur = z_ref[...]
        qn, _ = _head_rms(cur[:, :4 * HEAD], gq_ref[...], 4)
        kc, _ = _head_rms(cur[:, 4 * HEAD:6 * HEAD], gk_ref[...], 2)
        hal = zh_ref[...]
        kp, _ = _head_rms(hal[:, :2 * HEAD], gk_ref[...], 2)
        kext[pl.ds(0, WINDOW), :] = kp
        kext[pl.ds(WINDOW, tq), :] = kc
        vext[pl.ds(0, WINDOW), :] = hal[:, 2 * HEAD:]
        vext[pl.ds(WINDOW, tq), :] = cur[:, 6 * HEAD:]
        kx = _dot(kext[...], e_ref[...]).astype(_MXU)
        vx = _dot(vext[...], e_ref[...]).astype(_MXU)
        first_col = jnp.where(i > 0, 0, WINDOW)
        out = jnp.zeros((tq, 4 * HEAD), F32)
        for h in range(4):
            mk = _lane_mask(4 * HEAD, h)
            qh = jnp.where(mk, qn, 0.0)
            pn, _, _ = _swa_probs(qh, kx, b_ref[h], first_col, sink_ref[0, h])
            out = jnp.where(mk, _dot(pn, vx), out)
        o_ref[...] = out

    return pl.pallas_call(
        body, name=name, grid=(T // tq,),
        in_specs=[pl.BlockSpec((tq, 8 * HEAD), lambda i: (i, 0)),
                  pl.BlockSpec((WINDOW, 4 * HEAD), lambda i: (jnp.maximum(i * r - 1, 0), 1)),
                  pl.BlockSpec((1, 4 * HEAD), lambda i: (0, 0)), pl.BlockSpec((1, 2 * HEAD), lambda i: (0, 0)),
                  pl.BlockSpec(memory_space=pltpu.SMEM),
                  pl.BlockSpec(bias.shape, lambda i: (0, 0, 0)),
                  pl.BlockSpec(expand.shape, lambda i: (0, 0))],
        out_specs=pl.BlockSpec((tq, 4 * HEAD), lambda i: (i, 0)),
        out_shape=jax.ShapeDtypeStruct((T, 4 * HEAD), F32),
        scratch_shapes=[pltpu.VMEM((tq + WINDOW, 2 * HEAD), F32), pltpu.VMEM((tq + WINDOW, 2 * HEAD), F32)],
        compiler_params=_cp("parallel"))(zw, zw, gq, gk, sink, bias, expand)


def _swa_bwd(zw, dact, gq, gk, sink, bias, bucket, expand, name):
    T = zw.shape[0]
    tq = bias.shape[1]
    tk = tq + WINDOW
    r = tq // WINDOW
    nt = T // tq
    nb = T // WINDOW
    scale = HEAD ** -0.5

    def body(z_ref, zh_ref, zn_ref, d_ref, dn_ref, gq_ref, gk_ref, sink_ref, b_ref, bk_ref, e_ref,
             dz_ref, dgq_ref, dgk_ref, dsk_ref, drb_ref, kext, vext, dk_s, dv_s, db_s):
        i = pl.program_id(0)

        @pl.when(i == 0)
        def _():
            dgq_ref[...] = jnp.zeros_like(dgq_ref)
            dgk_ref[...] = jnp.zeros_like(dgk_ref)
            dsk_ref[...] = jnp.zeros_like(dsk_ref)
            drb_ref[...] = jnp.zeros_like(drb_ref)
            db_s[...] = jnp.zeros_like(db_s)

        lane = lax.broadcasted_iota(jnp.int32, (1, LANE), 1)
        cur = z_ref[...]
        q_raw, k_raw = cur[:, :4 * HEAD], cur[:, 4 * HEAD:6 * HEAD]
        qn, q_r = _head_rms(q_raw, gq_ref[...], 4)
        kc, k_r = _head_rms(k_raw, gk_ref[...], 2)
        hal = zh_ref[...]
        kp, _ = _head_rms(hal[:, :2 * HEAD], gk_ref[...], 2)
        kext[pl.ds(0, WINDOW), :] = kp
        kext[pl.ds(WINDOW, tq), :] = kc
        vext[pl.ds(0, WINDOW), :] = hal[:, 2 * HEAD:]
        vext[pl.ds(WINDOW, tq), :] = cur[:, 6 * HEAD:]
        ev = e_ref[...]
        kx = _dot(kext[...], ev).astype(_MXU)
        vx = _dot(vext[...], ev).astype(_MXU)
        first_col = jnp.where(i > 0, 0, WINDOW)
        do = d_ref[...]
        dq = jnp.zeros((tq, 4 * HEAD), F32)
        dkx = jnp.zeros((tk, 4 * HEAD), F32)
        dvx = jnp.zeros((tk, 4 * HEAD), F32)
        dsk = jnp.zeros((1, LANE), F32)
        for h in range(4):
            mk = _lane_mask(4 * HEAD, h)
            qh = jnp.where(mk, qn, 0.0).astype(_MXU)
            sk = sink_ref[0, h]
            pn, m, den = _swa_probs(qh, kx, b_ref[h], first_col, sk)
            doh = jnp.where(mk, do, 0.0).astype(_MXU)
            dpn = _dot(doh, vx, _NT)
            delta = jnp.sum(pn * dpn, axis=-1, keepdims=True)
            ds = pn * (dpn - delta)
            psink = jnp.exp(sk - m) / den
            dsk = dsk + jnp.where(lane == h, jnp.sum(-psink * delta, axis=0, keepdims=True), 0.0)
            db_s[h] += ds
            dss = (ds * scale).astype(_MXU)
            dq = dq + jnp.where(mk, _dot(dss, kx), 0.0)
            dkx = dkx + _dot(dss, qh, _TN)
            dvx = dvx + _dot(pn, doh, _TN)
        dsk_ref[...] += dsk
        dk_ext = _exact_dot(dkx, ev, _NT, "a")
        dv_ext = _exact_dot(dvx, ev, _NT, "a")
        dk_s[...] = dk_ext[WINDOW:, :]
        dv_s[...] = dv_ext[WINDOW:, :]

        @pl.when(i < nt - 1)
        def _():
            nxt = zn_ref[...]
            q2, _ = _head_rms(nxt[:, :4 * HEAD], gq_ref[...], 4)
            k2n, _ = _head_rms(nxt[:, 4 * HEAD:6 * HEAD], gk_ref[...], 2)
            k2 = jnp.concatenate([kc[tq - WINDOW:, :], k2n], axis=0)
            v2 = jnp.concatenate([cur[tq - WINDOW:, 6 * HEAD:], nxt[:, 6 * HEAD:]], axis=0)
            k2x = _dot(k2, ev).astype(_MXU)
            v2x = _dot(v2, ev).astype(_MXU)
            do2 = dn_ref[...]
            dk2x = jnp.zeros((2 * WINDOW, 4 * HEAD), F32)
            dv2x = jnp.zeros((2 * WINDOW, 4 * HEAD), F32)
            for h in range(4):
                mk = _lane_mask(4 * HEAD, h)
                qh = jnp.where(mk, q2, 0.0).astype(_MXU)
                pn, _, _ = _swa_probs(qh, k2x, b_ref[h][:WINDOW, :2 * WINDOW], 0, sink_ref[0, h])
                doh = jnp.where(mk, do2, 0.0).astype(_MXU)
                dpn = _dot(doh, v2x, _NT)
                ds = pn * (dpn - jnp.sum(pn * dpn, axis=-1, keepdims=True))
                dk2x = dk2x + _dot((ds * scale).astype(_MXU), qh, _TN)
                dv2x = dv2x + _dot(pn, doh, _TN)
            dk_s[pl.ds(tq - WINDOW, WINDOW), :] += _exact_dot(dk2x, ev, _NT, "a")[:WINDOW, :]
            dv_s[pl.ds(tq - WINDOW, WINDOW), :] += _exact_dot(dv2x, ev, _NT, "a")[:WINDOW, :]

        dq_raw, dgq = _head_rms_bwd(dq, q_raw, q_r, gq_ref[...], 4)
        dk_raw, dgk = _head_rms_bwd(dk_s[...], k_raw, k_r, gk_ref[...], 2)
        dgq_ref[...] += dgq
        dgk_ref[...] += dgk
        dz_ref[:, :4 * HEAD] = dq_raw.astype(dz_ref.dtype)
        dz_ref[:, 4 * HEAD:6 * HEAD] = dk_raw.astype(dz_ref.dtype)
        dz_ref[:, 6 * HEAD:] = dv_s[...].astype(dz_ref.dtype)

        @pl.when(i == nt - 1)
        def _():
            bk = bk_ref[...]
            for b in range(N_BUCKETS):
                rowv = jnp.zeros((1, LANE), F32)
                for h in range(4):
                    s1 = jnp.sum(jnp.where(bk == b, db_s[h], 0.0), axis=0, keepdims=True)
                    rowv = jnp.where(lane == h, jnp.sum(s1, axis=1, keepdims=True), rowv)
                drb_ref[pl.ds(b, 1), :] = rowv

    const2 = lambda i: (0, 0)
    return pl.pallas_call(
        body, name=name, grid=(nt,),
        in_specs=[pl.BlockSpec((tq, 8 * HEAD), lambda i: (i, 0)),
                  pl.BlockSpec((WINDOW, 4 * HEAD), lambda i: (jnp.maximum(i * r - 1, 0), 1)),
                  pl.BlockSpec((WINDOW, 8 * HEAD), lambda i: (jnp.minimum((i + 1) * r, nb - 1), 0)),
                  pl.BlockSpec((tq, 4 * HEAD), lambda i: (i, 0)),
                  pl.BlockSpec((WINDOW, 4 * HEAD), lambda i: (jnp.minimum((i + 1) * r, nb - 1), 0)),
                  pl.BlockSpec((1, 4 * HEAD), const2), pl.BlockSpec((1, 2 * HEAD), const2),
                  pl.BlockSpec(memory_space=pltpu.SMEM),
                  pl.BlockSpec(bias.shape, lambda i: (0, 0, 0)),
                  pl.BlockSpec(bucket.shape, const2), pl.BlockSpec(expand.shape, const2)],
        out_specs=[pl.BlockSpec((tq, 8 * HEAD), lambda i: (i, 0)),
                   pl.BlockSpec((1, 4 * HEAD), const2), pl.BlockSpec((1, 2 * HEAD), const2),
                   pl.BlockSpec((1, LANE), const2), pl.BlockSpec((N_BUCKETS, LANE), const2)],
        out_shape=[jax.ShapeDtypeStruct((T, 8 * HEAD), _MXU), jax.ShapeDtypeStruct((1, 4 * HEAD), F32),
                   jax.ShapeDtypeStruct((1, 2 * HEAD), F32), jax.ShapeDtypeStruct((1, LANE), F32),
                   jax.ShapeDtypeStruct((N_BUCKETS, LANE), F32)],
        scratch_shapes=[pltpu.VMEM((tk, 2 * HEAD), F32), pltpu.VMEM((tk, 2 * HEAD), F32),
                        pltpu.VMEM((tq, 2 * HEAD), F32), pltpu.VMEM((tq, 2 * HEAD), F32),
                        pltpu.VMEM((4, tq, tk), F32)],
        compiler_params=_cp("arbitrary"))(zw, zw, zw, dact, dact, gq, gk, sink, bias, bucket, expand)


FOX_B = 512
FOX_TM = 256


def _tri(n, lower):
    m = np.tril(np.ones((n, n), np.float32)) if lower else np.triu(np.ones((n, n), np.float32))
    return m


def _log_sigmoid(x):
    return jnp.minimum(x, 0.0) - jnp.log1p(jnp.exp(-jnp.abs(x)))


def _fox_prep(zf, gq, gk, bf, name):
    T = zf.shape[0]
    tm = _tile(T, FOX_TM)
    lower = jnp.asarray(_tri(tm, True), _MXU)

    def body(z_ref, gq_ref, gk_ref, bf_ref, l_ref, q_ref, k_ref, v_ref, f_ref, ft_ref, carry):
        @pl.when(pl.program_id(0) == 0)
        def _():
            carry[...] = jnp.zeros_like(carry)

        z = z_ref[...]
        q, _ = _head_rms(z[:, :CH], gq_ref[...], 4)
        k, _ = _head_rms(z[:, CH:2 * CH], gk_ref[...], 4)
        q_ref[...] = q.astype(q_ref.dtype)
        k_ref[...] = k.astype(k_ref.dtype)
        v_ref[...] = z[:, 2 * CH:3 * CH].astype(v_ref.dtype)
        lane = lax.broadcasted_iota(jnp.int32, (1, LANE), 1)
        lf = jnp.where(lane < 4, _log_sigmoid(z[:, 3 * CH:] + bf_ref[...]), 0.0)
        fv = _exact_dot(l_ref[...], lf, _NN, "b") + carry[pl.ds(0, 1), :]
        f_ref[...] = fv
        ft_ref[...] = fv.T
        carry[pl.ds(0, 1), :] = f_ref[pl.ds(tm - 1, 1), :]

    row = pl.BlockSpec((tm, CH), lambda i: (i, 0))
    vec = pl.BlockSpec((1, CH), lambda i: (0, 0))
    qsh = jax.ShapeDtypeStruct((T, CH), _MXU)
    return pl.pallas_call(
        body, name=name, grid=(T // tm,),
        in_specs=[pl.BlockSpec((tm, 3 * CH + LANE), lambda i: (i, 0)), vec, vec,
                  pl.BlockSpec((1, LANE), lambda i: (0, 0)), pl.BlockSpec((tm, tm), lambda i: (0, 0))],
        out_specs=[row, row, row, pl.BlockSpec((tm, LANE), lambda i: (i, 0)), pl.BlockSpec((LANE, tm), lambda i: (0, i))],
        out_shape=[qsh, qsh, qsh, jax.ShapeDtypeStruct((T, LANE), F32), jax.ShapeDtypeStruct((LANE, T), F32)],
        scratch_shapes=[pltpu.VMEM((8, LANE), F32)],
        compiler_params=_cp("arbitrary"))(zf, gq, gk, bf, lower)


def _lane_col(x, h):
    lane = lax.broadcasted_iota(jnp.int32, (1, x.shape[-1]), 1)
    return jnp.sum(jnp.where(lane == h, x, 0.0), axis=-1, keepdims=True)


def _fox_scores(qh, k, fq, ft_ref, h, qi, ki, B):
    s = _dot(qh, k, _NT) * (HEAD ** -0.5)
    s = s + (fq - ft_ref[pl.ds(h, 1), :])
    row = qi * B + lax.broadcasted_iota(jnp.int32, s.shape, 0)
    col = ki * B + lax.broadcasted_iota(jnp.int32, s.shape, 1)
    return jnp.where(col <= row, s, NEG_INF)


def _fox_fwd(q, k, v, f, ft, name):
    T = q.shape[0]
    B = _tile(T, FOX_B)
    n = T // B

    def body(q_ref, k_ref, v_ref, f_ref, ft_ref, o_ref, lse_ref, m_s, l_s, acc):
        qi, ki = pl.program_id(0), pl.program_id(1)

        @pl.when(ki == 0)
        def _():
            m_s[...] = jnp.full_like(m_s, NEG_INF)
            l_s[...] = jnp.zeros_like(l_s)
            acc[...] = jnp.zeros_like(acc)

        @pl.when(ki <= qi)
        def _():
            qv, kv, vv, fv = q_ref[...], k_ref[...], v_ref[...], f_ref[...]
            for h in range(4):
                mk = _lane_mask(CH, h)
                qh = jnp.where(mk, qv, jnp.zeros_like(qv))
                s = _fox_scores(qh, kv, _lane_col(fv, h), ft_ref, h, qi, ki, B)
                m_old = m_s[h]
                m_new = jnp.maximum(m_old, jnp.max(s, axis=-1, keepdims=True))
                alpha = jnp.exp(m_old - m_new)
                p = jnp.exp(s - m_new)
                l_s[h] = alpha * l_s[h] + jnp.sum(p, axis=-1, keepdims=True)
                m_s[h] = m_new
                acc[...] = jnp.where(mk, acc[...] * alpha + _dot(p, vv), acc[...])

        @pl.when(ki == qi)
        def _():
            lane = lax.broadcasted_iota(jnp.int32, (1, LANE), 1)
            out = acc[...]
            lse = jnp.zeros((B, LANE), F32)
            for h in range(4):
                out = jnp.where(_lane_mask(CH, h), out / l_s[h], out)
                lse = jnp.where(lane == h, m_s[h] + jnp.log(l_s[h]), lse)
            o_ref[...] = out
            lse_ref[...] = lse

    qspec = pl.BlockSpec((B, CH), lambda qi, ki: (qi, 0))
    kspec = pl.BlockSpec((B, CH), lambda qi, ki: (jnp.minimum(ki, qi), 0))
    return pl.pallas_call(
        body, name=name, grid=(n, n),
        in_specs=[qspec, kspec, kspec, pl.BlockSpec((B, LANE), lambda qi, ki: (qi, 0)),
                  pl.BlockSpec((8, B), lambda qi, ki: (0, jnp.minimum(ki, qi)))],
        out_specs=[qspec, pl.BlockSpec((B, LANE), lambda qi, ki: (qi, 0))],
        out_shape=[jax.ShapeDtypeStruct((T, CH), F32), jax.ShapeDtypeStruct((T, LANE), F32)],
        scratch_shapes=[pltpu.VMEM((4, B, 1), F32), pltpu.VMEM((4, B, 1), F32), pltpu.VMEM((B, CH), F32)],
        compiler_params=_cp("parallel", "arbitrary"))(q, k, v, f, ft)


def _fox_delta(o, do, name):
    T = o.shape[0]
    tm = _tile(T, ROW_TILE)

    def body(o_ref, d_ref, out_ref):
        prod = o_ref[...] * d_ref[...]
        lane = lax.broadcasted_iota(jnp.int32, (1, LANE), 1)
        out = jnp.zeros((tm, LANE), F32)
        for h in range(4):
            s = jnp.sum(jnp.where(_lane_mask(CH, h), prod, 0.0), axis=-1, keepdims=True)
            out = jnp.where(lane == h, s, out)
        out_ref[...] = out

    row = pl.BlockSpec((tm, CH), lambda i: (i, 0))
    return pl.pallas_call(
        body, name=name, grid=(T // tm,), in_specs=[row, row],
        out_specs=pl.BlockSpec((tm, LANE), lambda i: (i, 0)),
        out_shape=jax.ShapeDtypeStruct((T, LANE), F32), compiler_params=_cp("parallel"))(o, do)


def _fox_bwd_dq(q, k, v, f, ft, lse, delta, do, name):
    T = q.shape[0]
    B = _tile(T, FOX_B)
    n = T // B

    def body(q_ref, k_ref, v_ref, f_ref, ft_ref, lse_ref, dl_ref, do_ref, dq_ref, dfq_ref, dq_s, df_s):
        qi, ki = pl.program_id(0), pl.program_id(1)

        @pl.when(ki == 0)
        def _():
            dq_s[...] = jnp.zeros_like(dq_s)
            df_s[...] = jnp.zeros_like(df_s)

        @pl.when(ki <= qi)
        def _():
            qv, kv, vv, fv = q_ref[...], k_ref[...], v_ref[...], f_ref[...]
            lsev, dlv, dov = lse_ref[...], dl_ref[...], do_ref[...]
            lane = lax.broadcasted_iota(jnp.int32, (1, LANE), 1)
            for h in range(4):
                mk = _lane_mask(CH, h)
                qh = jnp.where(mk, qv, jnp.zeros_like(qv))
                s = _fox_scores(qh, kv, _lane_col(fv, h), ft_ref, h, qi, ki, B)
                p = jnp.exp(s - _lane_col(lsev, h))
                doh = jnp.where(mk, dov, 0.0)
                ds = p * (_dot(doh, vv, _NT) - _lane_col(dlv, h))
                dq_s[...] += jnp.where(mk, _dot(ds * (HEAD ** -0.5), kv), 0.0)
                df_s[...] += jnp.where(lane == h, jnp.sum(ds, axis=-1, keepdims=True), 0.0)

        @pl.when(ki == qi)
        def _():
            dq_ref[...] = dq_s[...]
            dfq_ref[...] = df_s[...]

    qspec = pl.BlockSpec((B, CH), lambda qi, ki: (qi, 0))
    kspec = pl.BlockSpec((B, CH), lambda qi, ki: (jnp.minimum(ki, qi), 0))
    lspec = pl.BlockSpec((B, LANE), lambda qi, ki: (qi, 0))
    return pl.pallas_call(
        body, name=name, grid=(n, n),
        in_specs=[qspec, kspec, kspec, lspec, pl.BlockSpec((8, B), lambda qi, ki: (0, jnp.minimum(ki, qi))),
                  lspec, lspec, qspec],
        out_specs=[qspec, lspec],
        out_shape=[jax.ShapeDtypeStruct((T, CH), F32), jax.ShapeDtypeStruct((T, LANE), F32)],
        scratch_shapes=[pltpu.VMEM((B, CH), F32), pltpu.VMEM((B, LANE), F32)],
        compiler_params=_cp("parallel", "arbitrary"))(q, k, v, f, ft, lse, delta, do)


def _fox_bwd_dkv(q, k, v, f, ft, lse, delta, do, name):
    T = q.shape[0]
    B = _tile(T, FOX_B)
    n = T // B

    def body(q_ref, k_ref, v_ref, f_ref, ft_ref, lse_ref, dl_ref, do_ref, dk_ref, dv_ref, dft_ref, dk_s, dv_s, df_s):
        ki, qi = pl.program_id(0), pl.program_id(1)

        @pl.when(qi == 0)
        def _():
            dk_s[...] = jnp.zeros_like(dk_s)
            dv_s[...] = jnp.zeros_like(dv_s)
            df_s[...] = jnp.zeros_like(df_s)

        @pl.when(qi >= ki)
        def _():
            qv, kv, vv, fv = q_ref[...], k_ref[...], v_ref[...], f_ref[...]
            lsev, dlv, dov = lse_ref[...], dl_ref[...], do_ref[...]
            for h in range(4):
                mk = _lane_mask(CH, h)
                qh = jnp.where(mk, qv, jnp.zeros_like(qv))
                s = _fox_scores(qh, kv, _lane_col(fv, h), ft_ref, h, qi, ki, B)
                p = jnp.exp(s - _lane_col(lsev, h))
                doh = jnp.where(mk, dov, 0.0)
                ds = p * (_dot(doh, vv, _NT) - _lane_col(dlv, h))
                dv_s[...] += _dot(p, doh, _TN)
                dk_s[...] += _dot(ds * (HEAD ** -0.5), qh, _TN)
                df_s[pl.ds(h, 1), :] -= jnp.sum(ds, axis=0, keepdims=True)

        @pl.when(qi == n - 1)
        def _():
            dk_ref[...] = dk_s[...]
            dv_ref[...] = dv_s[...]
            dft_ref[...] = jnp.zeros_like(dft_ref)
            dft_ref[pl.ds(0, 8), :] = df_s[...]

    qspec = pl.BlockSpec((B, CH), lambda ki, qi: (jnp.maximum(qi, ki), 0))
    kspec = pl.BlockSpec((B, CH), lambda ki, qi: (ki, 0))
    lspec = pl.BlockSpec((B, LANE), lambda ki, qi: (jnp.maximum(qi, ki), 0))
    return pl.pallas_call(
        body, name=name, grid=(n, n),
        in_specs=[qspec, kspec, kspec, lspec, pl.BlockSpec((8, B), lambda ki, qi: (0, ki)), lspec, lspec, qspec],
        out_specs=[kspec, kspec, pl.BlockSpec((LANE, B), lambda ki, qi: (0, ki))],
        out_shape=[jax.ShapeDtypeStruct((T, CH), F32), jax.ShapeDtypeStruct((T, CH), F32),
                   jax.ShapeDtypeStruct((LANE, T), F32)],
        scratch_shapes=[pltpu.VMEM((B, CH), F32), pltpu.VMEM((B, CH), F32), pltpu.VMEM((8, B), F32)],
        compiler_params=_cp("parallel", "arbitrary"))(q, k, v, f, ft, lse, delta, do)


def _fox_post(zf, dqn, dkn, dv, dfq, dft, gq, gk, bf, name):
    T = zf.shape[0]
    tm = _tile(T, FOX_TM)
    nt = T // tm
    upper = jnp.asarray(_tri(tm, False), _MXU)

    def body(z_ref, dq_ref, dk_ref, dv_ref, dfq_ref, dft_ref, gq_ref, gk_ref, bf_ref, u_ref, dz_ref, sm_ref, carry, rc_s):
        @pl.when(pl.program_id(0) == 0)
        def _():
            carry[...] = jnp.zeros_like(carry)
            sm_ref[...] = jnp.zeros_like(sm_ref)

        z = z_ref[...]
        q_raw, k_raw = z[:, :CH], z[:, CH:2 * CH]
        _, q_r = _head_rms(q_raw, gq_ref[...], 4)
        _, k_r = _head_rms(k_raw, gk_ref[...], 4)
        dq, dgq = _head_rms_bwd(dq_ref[...], q_raw, q_r, gq_ref[...], 4)
        dk, dgk = _head_rms_bwd(dk_ref[...], k_raw, k_r, gk_ref[...], 4)
        df = dfq_ref[...] + dft_ref[...].T
        rc_s[...] = _exact_dot(u_ref[...], df, _NN, "b") + carry[pl.ds(0, 1), :]
        carry[pl.ds(0, 1), :] = rc_s[pl.ds(0, 1), :]
        lane = lax.broadcasted_iota(jnp.int32, (1, LANE), 1)
        x = z[:, 3 * CH:] + bf_ref[...]
        dff = jnp.where(lane < 4, rc_s[...] * _sigmoid(-x), 0.0)
        dz_ref[:, :CH] = dq.astype(dz_ref.dtype)
        dz_ref[:, CH:2 * CH] = dk.astype(dz_ref.dtype)
        dz_ref[:, 2 * CH:3 * CH] = dv_ref[...].astype(dz_ref.dtype)
        dz_ref[:, 3 * CH:] = dff.astype(dz_ref.dtype)
        sm_ref[pl.ds(0, 1), :] += dgq
        sm_ref[pl.ds(1, 1), :] += dgk
        sm_ref[pl.ds(2, 1), :LANE] += jnp.sum(dff, axis=0, keepdims=True)

    rev = lambda i: (nt - 1 - i, 0)
    row = pl.BlockSpec((tm, CH), rev)
    lrow = pl.BlockSpec((tm, LANE), rev)
    vec = pl.BlockSpec((1, CH), lambda i: (0, 0))
    return pl.pallas_call(
        body, name=name, grid=(nt,),
        in_specs=[pl.BlockSpec((tm, 3 * CH + LANE), rev), row, row, row, lrow,
                  pl.BlockSpec((LANE, tm), lambda i: (0, nt - 1 - i)), vec, vec,
                  pl.BlockSpec((1, LANE), lambda i: (0, 0)), pl.BlockSpec((tm, tm), lambda i: (0, 0))],
        out_specs=[pl.BlockSpec((tm, 3 * CH + LANE), rev), pl.BlockSpec((8, CH), lambda i: (0, 0))],
        out_shape=[jax.ShapeDtypeStruct((T, 3 * CH + LANE), _MXU), jax.ShapeDtypeStruct((8, CH), F32)],
        scratch_shapes=[pltpu.VMEM((8, LANE), F32), pltpu.VMEM((tm, LANE), F32)],
        compiler_params=_cp("arbitrary"))(zf, dqn, dkn, dv, dfq, dft, gq, gk, bf, upper)


AUG_F, AUG_ONE, AUG_LSE = HEAD, HEAD + 3, HEAD + 6


def _pieces(x):
    hi = x.astype(_MXU).astype(F32)
    r1 = x - hi
    mid = r1.astype(_MXU).astype(F32)
    lo = (r1 - mid).astype(_MXU).astype(F32)
    return hi, mid, lo


def _put_pieces(base, first_lane, x, sign):
    lane = lax.broadcasted_iota(jnp.int32, (1, LANE), 1)
    for j, piece in enumerate(_pieces(x)):
        base = jnp.where(lane == first_lane + j, sign * piece, base)
    return base


def _head_select_matrix():
    p = np.zeros((4, 4 * HEAD, LANE), np.float32)
    for h in range(4):
        for d in range(HEAD):
            p[h, h * HEAD + d, d] = 1.0
    return p


def _tri_steps(n, by_key):
    if by_key:
        pairs = [(q, k) for k in range(n) for q in range(k, n)]
    else:
        pairs = [(q, k) for q in range(n) for k in range(q + 1)]
    return (jnp.asarray([p[0] for p in pairs], jnp.int32), jnp.asarray([p[1] for p in pairs], jnp.int32))


def _fox2_prep(zf, gq, gk, bf, sel, name):
    T = zf.shape[0]
    tm = _tile(T, FOX_TM)
    lower = jnp.asarray(_tri(tm, True), _MXU)

    def body(z_ref, gq_ref, gk_ref, bf_ref, l_ref, p_ref, qa_ref, ka_ref, va_ref, vat_ref, carry, f_s):
        @pl.when(pl.program_id(0) == 0)
        def _():
            carry[...] = jnp.zeros_like(carry)

        z = z_ref[...]
        q, _ = _head_rms(z[:, :CH], gq_ref[...], 4)
        k, _ = _head_rms(z[:, CH:2 * CH], gk_ref[...], 4)
        q = (q * (HEAD ** -0.5)).astype(_MXU)
        k = k.astype(_MXU)
        v = z[:, 2 * CH:3 * CH].astype(_MXU)
        lane = lax.broadcasted_iota(jnp.int32, (1, LANE), 1)
        lf = jnp.where(lane < 4, _log_sigmoid(z[:, 3 * CH:] + bf_ref[...]), 0.0)
        f_s[...] = _exact_dot(l_ref[...], lf, _NN, "b") + carry[pl.ds(0, 1), :]
        carry[pl.ds(0, 1), :] = f_s[pl.ds(tm - 1, 1), :]
        fv = f_s[...]
        q_ones = (lane >= AUG_ONE) & (lane < AUG_ONE + 3)
        k_ones = ((lane >= AUG_F) & (lane < AUG_F + 3)) | ((lane >= AUG_LSE) & (lane < AUG_LSE + 3))
        v_ones = (lane >= AUG_F) & (lane < AUG_F + 3)
        for h in range(4):
            fh = _lane_col(fv, h)
            qa = jnp.where(q_ones, 1.0, _dot(q, p_ref[h]))
            qa_ref[h] = _put_pieces(qa, AUG_F, fh, 1.0).astype(qa_ref.dtype)
            ka = jnp.where(k_ones, 1.0, _dot(k, p_ref[h]))
            ka_ref[h] = _put_pieces(ka, AUG_ONE, fh, -1.0).astype(ka_ref.dtype)
            va = jnp.where(v_ones, 1.0, _dot(v, p_ref[h]))
            va_ref[h] = va.astype(va_ref.dtype)
            vat_ref[h] = va.T.astype(vat_ref.dtype)

    vec = pl.BlockSpec((1, CH), lambda i: (0, 0))
    hspec = pl.BlockSpec((4, tm, LANE), lambda i: (0, i, 0))
    hsh = jax.ShapeDtypeStruct((4, T, LANE), _MXU)
    return pl.pallas_call(
        body, name=name, grid=(T // tm,),
        in_specs=[pl.BlockSpec((tm, 3 * CH + LANE), lambda i: (i, 0)), vec, vec,
                  pl.BlockSpec((1, LANE), lambda i: (0, 0)), pl.BlockSpec((tm, tm), lambda i: (0, 0)),
                  pl.BlockSpec(sel.shape, lambda i: (0, 0, 0))],
        out_specs=[hspec, hspec, hspec, pl.BlockSpec((4, LANE, tm), lambda i: (0, 0, i))],
        out_shape=[hsh, hsh, hsh, jax.ShapeDtypeStruct((4, LANE, T), _MXU)],
        scratch_shapes=[pltpu.VMEM((8, LANE), F32), pltpu.VMEM((tm, LANE), F32)],
        compiler_params=_cp("arbitrary"))(zf, gq, gk, bf, lower, sel)


def _causal(s, transposed):
    row = lax.broadcasted_iota(jnp.int32, s.shape, 0)
    col = lax.broadcasted_iota(jnp.int32, s.shape, 1)
    return jnp.where((row <= col) if transposed else (col <= row), s, NEG_INF)


def _mxu_dot(a, b, dims):
    return lax.dot_general(a, b, dims, preferred_element_type=F32)


def _fox2_fwd(qa, ka, vat, sel, name):
    T = qa.shape[1]
    B = _tile(T, FOX_B)
    n = T // B
    qt, kt = _tri_steps(n, False)

    def body(qt_ref, kt_ref, qa_ref, ka_ref, vat_ref, p_ref, o_ref, qb_ref, m_s, acc):
        step = pl.program_id(0)
        qi, ki = qt_ref[step], kt_ref[step]

        @pl.when(ki == 0)
        def _():
            m_s[...] = jnp.full_like(m_s, NEG_INF)
            acc[...] = jnp.zeros_like(acc)

        def update(diag):
            for h in range(4):
                st = _mxu_dot(ka_ref[h], qa_ref[h], _NT)
                if diag:
                    st = _causal(st, True)
                m_old = m_s[h, pl.ds(0, 1), :]
                m_new = jnp.maximum(m_old, jnp.max(st, axis=0, keepdims=True))
                pt = jnp.exp(st - m_new)
                acc[h] = acc[h] * jnp.exp(m_old - m_new) + _dot(vat_ref[h], pt)
                m_s[h, pl.ds(0, 1), :] = m_new

        @pl.when(ki < qi)
        def _():
            update(False)

        @pl.when(ki == qi)
        def _():
            update(True)
            row = lax.broadcasted_iota(jnp.int32, (LANE, 1), 0)
            out = jnp.zeros((B, CH), F32)
            for h in range(4):
                a = acc[h]
                l = a[AUG_F:AUG_F + 1, :]
                out = out + _exact_dot((a / l).T, p_ref[h], _NT, "a")
                lse = m_s[h, pl.ds(0, 1), :] + jnp.log(l)
                qbt = qa_ref[h].astype(F32).T
                for j, piece in enumerate(_pieces(lse)):
                    qbt = jnp.where(row == AUG_LSE + j, -piece, qbt)
                qb_ref[h] = qbt.T.astype(qb_ref.dtype)
            o_ref[...] = out

    qspec = pl.BlockSpec((4, B, LANE), lambda s, qt, kt: (0, qt[s], 0))
    kspec = pl.BlockSpec((4, B, LANE), lambda s, qt, kt: (0, kt[s], 0))
    grid_spec = pltpu.PrefetchScalarGridSpec(
        num_scalar_prefetch=2, grid=(qt.shape[0],),
        in_specs=[qspec, kspec, pl.BlockSpec((4, LANE, B), lambda s, qt, kt: (0, 0, kt[s])),
                  pl.BlockSpec(sel.shape, lambda s, qt, kt: (0, 0, 0))],
        out_specs=[pl.BlockSpec((B, CH), lambda s, qt, kt: (qt[s], 0)), qspec],
        scratch_shapes=[pltpu.VMEM((4, 8, B), F32), pltpu.VMEM((4, LANE, B), F32)])
    return pl.pallas_call(
        body, name=name, grid_spec=grid_spec,
        out_shape=[jax.ShapeDtypeStruct((T, CH), F32), jax.ShapeDtypeStruct((4, T, LANE), _MXU)],
        compiler_params=_cp("arbitrary"))(qt, kt, qa, ka, vat, sel)


def _fox2_bwd_prep(o, do, sel, name):
    T = o.shape[0]
    tm = _tile(T, ROW_TILE)

    def body(o_ref, d_ref, p_ref, out_ref):
        dov = d_ref[...]
        prod = o_ref[...] * dov
        dob = dov.astype(_MXU)
        for h in range(4):
            delta = jnp.sum(jnp.where(_lane_mask(CH, h), prod, 0.0), axis=-1, keepdims=True)
            out_ref[h] = _put_pieces(_dot(dob, p_ref[h]), AUG_F, delta, -1.0).astype(out_ref.dtype)

    row = pl.BlockSpec((tm, CH), lambda i: (i, 0))
    return pl.pallas_call(
        body, name=name, grid=(T // tm,),
        in_specs=[row, row, pl.BlockSpec(sel.shape, lambda i: (0, 0, 0))],
        out_specs=pl.BlockSpec((4, tm, LANE), lambda i: (0, i, 0)),
        out_shape=jax.ShapeDtypeStruct((4, T, LANE), _MXU), compiler_params=_cp("parallel"))(o, do, sel)


def _fox2_bwd_dq(qb, ka, va, doa, sel, name):
    T = qb.shape[1]
    B = _tile(T, FOX_B)
    n = T // B
    qt, kt = _tri_steps(n, False)

    def body(qt_ref, kt_ref, qb_ref, ka_ref, va_ref, do_ref, p_ref, dq_ref, dfq_ref, dq_s):
        step = pl.program_id(0)
        qi, ki = qt_ref[step], kt_ref[step]

        @pl.when(ki == 0)
        def _():
            dq_s[...] = jnp.zeros_like(dq_s)

        def update(diag):
            for h in range(4):
                s = _mxu_dot(qb_ref[h], ka_ref[h], _NT)
                if diag:
                    s = _causal(s, False)
                ds = jnp.exp(s) * _mxu_dot(do_ref[h], va_ref[h], _NT)
                dq_s[h] += _dot(ds, ka_ref[h])

        @pl.when(ki < qi)
        def _():
            update(False)

        @pl.when(ki == qi)
        def _():
            update(True)
            lane = lax.broadcasted_iota(jnp.int32, (1, LANE), 1)
            out = jnp.zeros((B, CH), F32)
            dfq = jnp.zeros((B, LANE), F32)
            for h in range(4):
                out = out + _exact_dot(dq_s[h] * (HEAD ** -0.5), p_ref[h], _NT, "a")
                dfq = jnp.where(lane == h, _lane_col(dq_s[h], AUG_F), dfq)
            dq_ref[...] = out
            dfq_ref[...] = dfq

    qspec = pl.BlockSpec((4, B, LANE), lambda s, qt, kt: (0, qt[s], 0))
    kspec = pl.BlockSpec((4, B, LANE), lambda s, qt, kt: (0, kt[s], 0))
    grid_spec = pltpu.PrefetchScalarGridSpec(
        num_scalar_prefetch=2, grid=(qt.shape[0],),
        in_specs=[qspec, kspec, kspec, qspec, pl.BlockSpec(sel.shape, lambda s, qt, kt: (0, 0, 0))],
        out_specs=[pl.BlockSpec((B, CH), lambda s, qt, kt: (qt[s], 0)),
                   pl.BlockSpec((B, LANE), lambda s, qt, kt: (qt[s], 0))],
        scratch_shapes=[pltpu.VMEM((4, B, LANE), F32)])
    return pl.pallas_call(
        body, name=name, grid_spec=grid_spec,
        out_shape=[jax.ShapeDtypeStruct((T, CH), F32), jax.ShapeDtypeStruct((T, LANE), F32)],
        compiler_params=_cp("arbitrary"))(qt, kt, qb, ka, va, doa, sel)


def _fox2_bwd_dkv(qb, ka, va, doa, sel, name):
    T = qb.shape[1]
    B = _tile(T, FOX_B)
    n = T // B
    qt, kt = _tri_steps(n, True)

    def body(qt_ref, kt_ref, qb_ref, ka_ref, va_ref, do_ref, p_ref, dk_ref, dv_ref, df_ref, dk_s, dv_s):
        step = pl.program_id(0)
        qi, ki = qt_ref[step], kt_ref[step]

        @pl.when(qi == ki)
        def _():
            dk_s[...] = jnp.zeros_like(dk_s)
            dv_s[...] = jnp.zeros_like(dv_s)

        def update(diag):
            for h in range(4):
                st = _mxu_dot(ka_ref[h], qb_ref[h], _NT)
                if diag:
                    st = _causal(st, True)
                pt = jnp.exp(st)
                dst = pt * _mxu_dot(va_ref[h], do_ref[h], _NT)
                dv_s[h] += _dot(pt, do_ref[h])
                dk_s[h] += _dot(dst, qb_ref[h])

        @pl.when(qi == ki)
        def _():
            update(True)

        @pl.when(qi > ki)
        def _():
            update(False)

        @pl.when(qi == n - 1)
        def _():
            lane = lax.broadcasted_iota(jnp.int32, (1, LANE), 1)
            dk = jnp.zeros((B, CH), F32)
            dv = jnp.zeros((B, CH), F32)
            dfk = jnp.zeros((B, LANE), F32)
            for h in range(4):
                dk = dk + _exact_dot(dk_s[h], p_ref[h], _NT, "a")
                dv = dv + _exact_dot(dv_s[h], p_ref[h], _NT, "a")
                dfk = jnp.where(lane == h, -_lane_col(dk_s[h], AUG_ONE), dfk)
            dk_ref[...] = dk
            dv_ref[...] = dv
            df_ref[...] = dfk

    qspec = pl.BlockSpec((4, B, LANE), lambda s, qt, kt: (0, qt[s], 0))
    kspec = pl.BlockSpec((4, B, LANE), lambda s, qt, kt: (0, kt[s], 0))
    ospec = pl.BlockSpec((B, CH), lambda s, qt, kt: (kt[s], 0))
    grid_spec = pltpu.PrefetchScalarGridSpec(
        num_scalar_prefetch=2, grid=(qt.shape[0],),
        in_specs=[qspec, kspec, kspec, qspec, pl.BlockSpec(sel.shape, lambda s, qt, kt: (0, 0, 0))],
        out_specs=[ospec, ospec, pl.BlockSpec((B, LANE), lambda s, qt, kt: (kt[s], 0))],
        scratch_shapes=[pltpu.VMEM((4, B, LANE), F32), pltpu.VMEM((4, B, LANE), F32)])
    return pl.pallas_call(
        body, name=name, grid_spec=grid_spec,
        out_shape=[jax.ShapeDtypeStruct((T, CH), F32), jax.ShapeDtypeStruct((T, CH), F32),
                   jax.ShapeDtypeStruct((T, LANE), F32)],
        compiler_params=_cp("arbitrary"))(qt, kt, qb, ka, va, doa, sel)


def _fox2_bwd(qb, ka, va, doa, sel, name):
    T = qb.shape[1]
    B = _tile(T, FOX_B)
    n = T // B
    qt, kt = _tri_steps(n, True)
    nsteps = qt.shape[0]

    def body(qt_ref, kt_ref, qb_ref, ka_ref, va_ref, do_ref, p_ref, dk_ref, dv_ref, df_ref, dq_hbm,
             dk_s, dv_s, kat_s, dq_s, sem):
        step = pl.program_id(0)
        qi, ki = qt_ref[step], kt_ref[step]

        @pl.when(step == 0)
        def _():
            dq_s[...] = jnp.zeros_like(dq_s)

        @pl.when(qi == ki)
        def _():
            dk_s[...] = jnp.zeros_like(dk_s)
            dv_s[...] = jnp.zeros_like(dv_s)
            for h in range(4):
                kat_s[h] = ka_ref[h].astype(F32).T.astype(kat_s.dtype)

        def update(diag):
            for h in range(4):
                st = _mxu_dot(ka_ref[h], qb_ref[h], _NT)
                if diag:
                    st = _causal(st, True)
                pt = jnp.exp(st)
                dst = (pt * _mxu_dot(va_ref[h], do_ref[h], _NT)).astype(_MXU)
                dv_s[h] += _dot(pt, do_ref[h])
                dk_s[h] += _mxu_dot(dst, qb_ref[h], _NN)
                dq_s[qi, h] += _mxu_dot(kat_s[h], dst, _NN)

        @pl.when(qi == ki)
        def _():
            update(True)

        @pl.when(qi > ki)
        def _():
            update(False)

        @pl.when(qi == n - 1)
        def _():
            lane = lax.broadcasted_iota(jnp.int32, (1, LANE), 1)
            dk = jnp.zeros((B, CH), F32)
            dv = jnp.zeros((B, CH), F32)
            dfk = jnp.zeros((B, LANE), F32)
            for h in range(4):
                dk = dk + _exact_dot(dk_s[h], p_ref[h], _NT, "a")
                dv = dv + _exact_dot(dv_s[h], p_ref[h], _NT, "a")
                dfk = jnp.where(lane == h, -_lane_col(dk_s[h], AUG_ONE), dfk)
            dk_ref[...] = dk
            dv_ref[...] = dv
            df_ref[...] = dfk

        @pl.when(step == nsteps - 1)
        def _():
            cp = pltpu.make_async_copy(dq_s, dq_hbm, sem)
            cp.start()
            cp.wait()

    qspec = pl.BlockSpec((4, B, LANE), lambda s, qt, kt: (0, qt[s], 0))
    kspec = pl.BlockSpec((4, B, LANE), lambda s, qt, kt: (0, kt[s], 0))
    ospec = pl.BlockSpec((B, CH), lambda s, qt, kt: (kt[s], 0))
    grid_spec = pltpu.PrefetchScalarGridSpec(
        num_scalar_prefetch=2, grid=(nsteps,),
        in_specs=[qspec, kspec, kspec, qspec, pl.BlockSpec(sel.shape, lambda s, qt, kt: (0, 0, 0))],
        out_specs=[ospec, ospec, pl.BlockSpec((B, LANE), lambda s, qt, kt: (kt[s], 0)),
                   pl.BlockSpec(memory_space=pl.ANY)],
        scratch_shapes=[pltpu.VMEM((4, B, LANE), F32), pltpu.VMEM((4, B, LANE), F32), pltpu.VMEM((4, LANE, B), _MXU),
                        pltpu.VMEM((n, 4, LANE, B), F32), pltpu.SemaphoreType.DMA])
    return pl.pallas_call(
        body, name=name, grid_spec=grid_spec,
        out_shape=[jax.ShapeDtypeStruct((T, CH), F32), jax.ShapeDtypeStruct((T, CH), F32),
                   jax.ShapeDtypeStruct((T, LANE), F32), jax.ShapeDtypeStruct((n, 4, LANE, B), F32)],
        compiler_params=_cp("arbitrary"))(qt, kt, qb, ka, va, doa, sel)


def _fox2_post(zf, dqt, dkn, dv, dfk, sel, gq, gk, bf, name):
    T = zf.shape[0]
    tm = _tile(T, FOX_TM)
    nt = T // tm
    B = dqt.shape[3]
    per = B // tm
    upper = jnp.asarray(_tri(tm, False), _MXU)

    def body(z_ref, dqt_ref, dk_ref, dv_ref, df_ref, p_ref, gq_ref, gk_ref, bf_ref, u_ref, dz_ref, sm_ref, carry, rc_s):
        @pl.when(pl.program_id(0) == 0)
        def _():
            carry[...] = jnp.zeros_like(carry)
            sm_ref[...] = jnp.zeros_like(sm_ref)

        lane = lax.broadcasted_iota(jnp.int32, (1, LANE), 1)
        dqn = jnp.zeros((tm, CH), F32)
        dfq = jnp.zeros((tm, LANE), F32)
        for h in range(4):
            blk = dqt_ref[0, h].T
            dqn = dqn + _exact_dot(blk * (HEAD ** -0.5), p_ref[h], _NT, "a")
            dfq = jnp.where(lane == h, _lane_col(blk, AUG_F), dfq)
        z = z_ref[...]
        q_raw, k_raw = z[:, :CH], z[:, CH:2 * CH]
        _, q_r = _head_rms(q_raw, gq_ref[...], 4)
        _, k_r = _head_rms(k_raw, gk_ref[...], 4)
        dq, dgq = _head_rms_bwd(dqn, q_raw, q_r, gq_ref[...], 4)
        dk, dgk = _head_rms_bwd(dk_ref[...], k_raw, k_r, gk_ref[...], 4)
        rc_s[...] = _exact_dot(u_ref[...], dfq + df_ref[...], _NN, "b") + carry[pl.ds(0, 1), :]
        carry[pl.ds(0, 1), :] = rc_s[pl.ds(0, 1), :]
        x = z[:, 3 * CH:] + bf_ref[...]
        dff = jnp.where(lane < 4, rc_s[...] * _sigmoid(-x), 0.0)
        dz_ref[:, :CH] = dq.astype(dz_ref.dtype)
        dz_ref[:, CH:2 * CH] = dk.astype(dz_ref.dtype)
        dz_ref[:, 2 * CH:3 * CH] = dv_ref[...].astype(dz_ref.dtype)
        dz_ref[:, 3 * CH:] = dff.astype(dz_ref.dtype)
        sm_ref[pl.ds(0, 1), :] += dgq
        sm_ref[pl.ds(1, 1), :] += dgk
        sm_ref[pl.ds(2, 1), :LANE] += jnp.sum(dff, axis=0, keepdims=True)

    rev = lambda i: (nt - 1 - i, 0)
    row = pl.BlockSpec((tm, CH), rev)
    lrow = pl.BlockSpec((tm, LANE), rev)
    vec = pl.BlockSpec((1, CH), lambda i: (0, 0))
    return pl.pallas_call(
        body, name=name, grid=(nt,),
        in_specs=[pl.BlockSpec((tm, 3 * CH + LANE), rev),
                  pl.BlockSpec((1, 4, LANE, tm), lambda i: ((nt - 1 - i) // per, 0, 0, (nt - 1 - i) % per)),
                  row, row, lrow, pl.BlockSpec(sel.shape, lambda i: (0, 0, 0)), vec, vec,
                  pl.BlockSpec((1, LANE), lambda i: (0, 0)), pl.BlockSpec((tm, tm), lambda i: (0, 0))],
        out_specs=[pl.BlockSpec((tm, 3 * CH + LANE), rev), pl.BlockSpec((8, CH), lambda i: (0, 0))],
        out_shape=[jax.ShapeDtypeStruct((T, 3 * CH + LANE), _MXU), jax.ShapeDtypeStruct((8, CH), F32)],
        scratch_shapes=[pltpu.VMEM((8, LANE), F32), pltpu.VMEM((tm, LANE), F32)],
        compiler_params=_cp("arbitrary"))(zf, dqt, dkn, dv, dfk, sel, gq, gk, bf, upper)


def _merge_fwd(acts, zg, wbr, wout, x1, name):
    T, D = x1.shape
    tm = _tile(T, 256)

    def body(a0, a1, a2, a3, zg_ref, wbr_ref, wout_ref, x_ref, o_ref, mg_ref):
        merged = None
        for i, a_ref in enumerate((a0, a1, a2, a3)):
            term = _sigmoid(zg_ref[:, i * D:(i + 1) * D]) * _dot(a_ref[...], wbr_ref[i])
            merged = term if merged is None else merged + term
        mg_ref[...] = merged.astype(mg_ref.dtype)
        o_ref[...] = x_ref[...] + _dot(merged, wout_ref[...])

    arow = pl.BlockSpec((tm, CH), lambda i: (i, 0))
    xrow = pl.BlockSpec((tm, D), lambda i: (i, 0))
    return pl.pallas_call(
        body, name=name, grid=(T // tm,),
        in_specs=[arow, arow, arow, arow, pl.BlockSpec((tm, 4 * D), lambda i: (i, 0)),
                  pl.BlockSpec((4, CH, D), lambda i: (0, 0, 0)), pl.BlockSpec((D, D), lambda i: (0, 0)), xrow],
        out_specs=[xrow, xrow],
        out_shape=[jax.ShapeDtypeStruct((T, D), F32), jax.ShapeDtypeStruct((T, D), _MXU)],
        compiler_params=_cp("parallel"))(*acts, zg, wbr, wout, x1)


def _merge_bwd(dx2, acts, zg, wbr, wout, name):
    T, D = dx2.shape
    tm = _tile(T, 256)
    nt = T // tm

    def body(dx_ref, a0, a1, a2, a3, zg_ref, wbr_ref, wout_ref, d0, d1, d2, d3, dzg_ref, dw_ref, dw_s):
        i = pl.program_id(0)

        @pl.when(i == 0)
        def _():
            dw_s[...] = jnp.zeros_like(dw_s)

        dm = _dot(dx_ref[...], wout_ref[...], _NT)
        for b, (a_ref, d_ref) in enumerate(((a0, d0), (a1, d1), (a2, d2), (a3, d3))):
            av = a_ref[...].astype(_MXU)
            g = _sigmoid(zg_ref[:, b * D:(b + 1) * D])
            p = _dot(av, wbr_ref[b])
            dzg_ref[:, b * D:(b + 1) * D] = (dm * p * (g * (1.0 - g))).astype(dzg_ref.dtype)
            dp = (dm * g).astype(_MXU)
            d_ref[...] = _dot(dp, wbr_ref[b], _NT)
            dw_s[b] += _dot(av, dp, _TN)

        @pl.when(i == nt - 1)
        def _():
            dw_ref[...] = dw_s[...].astype(dw_ref.dtype)

    arow = pl.BlockSpec((tm, CH), lambda i: (i, 0))
    xrow = pl.BlockSpec((tm, D), lambda i: (i, 0))
    grow = pl.BlockSpec((tm, 4 * D), lambda i: (i, 0))
    wspec = pl.BlockSpec((4, CH, D), lambda i: (0, 0, 0))
    ash = jax.ShapeDtypeStruct((T, CH), F32)
    return pl.pallas_call(
        body, name=name, grid=(nt,),
        in_specs=[xrow, arow, arow, arow, arow, grow, wspec, pl.BlockSpec((D, D), lambda i: (0, 0))],
        out_specs=[arow, arow, arow, arow, grow, wspec],
        out_shape=[ash, ash, ash, ash, jax.ShapeDtypeStruct((T, 4 * D), _MXU), jax.ShapeDtypeStruct((4, CH, D), _MXU)],
        scratch_shapes=[pltpu.VMEM((4, CH, D), F32)],
        compiler_params=_cp("arbitrary"))(dx2, *acts, zg, wbr, wout)


def _rows_2d(a):
    return a.reshape((-1, a.shape[-1])) if a.ndim > 1 else a.reshape((1, -1))


def _row_tile(rows, cols, n_bufs):
    padded = -(-cols // LANE) * LANE
    cap = max(8, (VMEM_LIMIT // 3) // (2 * n_bufs * 4 * padded))
    return _tile(rows, cap, 8)


def _sum8(recv, name):
    shape = recv.shape[1:]
    r2 = recv.reshape((N_DEV, -1, shape[-1]))
    rows, cols = r2.shape[1:]
    tr = _row_tile(rows, cols, N_DEV // 2 + 1)

    def body(r_ref, o_ref):
        acc = r_ref[0].astype(F32)
        for d in range(1, N_DEV):
            acc = acc + r_ref[d].astype(F32)
        o_ref[...] = acc

    out = pl.pallas_call(
        body, name=name, grid=(rows // tr,),
        in_specs=[pl.BlockSpec((N_DEV, tr, cols), lambda i: (0, i, 0))],
        out_specs=pl.BlockSpec((tr, cols), lambda i: (i, 0)),
        out_shape=jax.ShapeDtypeStruct((rows, cols), F32), compiler_params=_cp("parallel"))(r2)
    return out.reshape(shape)


def _adamw(w, g, m, v, name):
    if w.ndim == 3:
        lead, rows, cols = w.shape
    else:
        lead, (rows, cols) = None, w.shape
    tr = _row_tile(rows, cols, 7)

    def body(w_ref, g_ref, m_ref, v_ref, d_ref, nm_ref, nv_ref):
        gv = g_ref[...]
        nm = ADAM_B1 * m_ref[...] + (1.0 - ADAM_B1) * gv
        nv = ADAM_B2 * v_ref[...] + (1.0 - ADAM_B2) * jnp.square(gv)
        m_hat = nm / (1.0 - ADAM_B1 ** ADAM_STEP)
        v_hat = nv / (1.0 - ADAM_B2 ** ADAM_STEP)
        d_ref[...] = -ADAM_LR * (m_hat / (jnp.sqrt(v_hat) + ADAM_EPS) + ADAM_WD * w_ref[...])
        nm_ref[...] = nm
        nv_ref[...] = nv

    if lead is None:
        grid, sem = (rows // tr,), ("parallel",)
        spec = pl.BlockSpec((tr, cols), lambda i: (i, 0))
    else:
        grid, sem = (lead, rows // tr), ("parallel", "parallel")
        spec = pl.BlockSpec((None, tr, cols), lambda l, i: (l, i, 0))
    osh = jax.ShapeDtypeStruct(w.shape, F32)
    return tuple(pl.pallas_call(
        body, name=name, grid=grid, in_specs=[spec] * 4, out_specs=[spec] * 3,
        out_shape=[osh] * 3, compiler_params=_cp(*sem))(w, g, m, v))


def _exchange(items, name):
    n = len(items)
    widths, out_shapes = [], []
    for src, kind, ax in items:
        if kind == "gather":
            w = src.shape[ax]
            shp = list(src.shape)
            shp[ax] = N_DEV * w
        else:
            w = src.shape[ax] // N_DEV
            shp = list(src.shape)
            shp[ax] = w
            shp = [N_DEV] + shp
        widths.append(w)
        out_shapes.append(jax.ShapeDtypeStruct(tuple(shp), src.dtype))

    def body(*refs):
        srcs, outs = refs[:n], refs[n:2 * n]
        send, recv, lsem = refs[2 * n:]
        x, y, c = lax.axis_index("x"), lax.axis_index("y"), lax.axis_index("c")
        me = 4 * x + 2 * y + c

        def peer(k):
            b = k + 1
            px = 1 - x if b & 4 else x
            py = 1 - y if b & 2 else y
            pc = 1 - c if b & 1 else c
            return (px, py, pc), 4 * px + 2 * py + pc

        def win(ref, ax, idx, w):
            return ref.at[tuple([slice(None)] * ax + [pl.ds(idx * w, w)])]

        def ends(j, mine, theirs):
            _, kind, ax = items[j]
            if kind == "gather":
                return srcs[j], win(outs[j], ax, mine, widths[j])
            return win(srcs[j], ax, theirs, widths[j]), outs[j].at[mine]

        local, sent = [], []
        for j in range(n):
            s, d = ends(j, me, me)
            cp = pltpu.make_async_copy(s, d, lsem.at[j])
            cp.start()
            local.append(cp)
            for k in range(N_DEV - 1):
                dev, pid = peer(k)
                s, d = ends(j, me, pid)
                cp = pltpu.make_async_remote_copy(s, d, send.at[j, k], recv.at[j, k], device_id=dev,
                                                  device_id_type=pl.DeviceIdType.MESH)
                cp.start()
                sent.append(cp)
        for j in range(n):
            for k in range(N_DEV - 1):
                dev, pid = peer(k)
                s, d = ends(j, pid, me)
                pltpu.make_async_remote_copy(s, d, send.at[j, k], recv.at[j, k], device_id=dev,
                                             device_id_type=pl.DeviceIdType.MESH).wait_recv()
        for cp in sent:
            cp.wait_send()
        for cp in local:
            cp.wait()

    hbm = pl.BlockSpec(memory_space=pl.ANY)
    return pl.pallas_call(
        body, name=name, in_specs=[hbm] * n, out_specs=[hbm] * n, out_shape=out_shapes,
        scratch_shapes=[pltpu.SemaphoreType.DMA((n, N_DEV - 1)), pltpu.SemaphoreType.DMA((n, N_DEV - 1)),
                        pltpu.SemaphoreType.DMA((n,))],
        compiler_params=pltpu.CompilerParams(has_side_effects=True))(*[it[0] for it in items])


def _exchange_plan(items):
    widths, out_shapes = [], []
    for src, kind, ax in items:
        shp = list(src.shape)
        if kind == "gather":
            w = src.shape[ax]
            shp[ax] = N_DEV * w
        else:
            w = src.shape[ax] // N_DEV
            shp[ax] = w
            shp = [N_DEV] + shp
        widths.append(w)
        out_shapes.append((tuple(shp), src.dtype))
    return widths, out_shapes


def _exchange_refs(items, widths, srcs, outs):
    x, y, c = lax.axis_index("x"), lax.axis_index("y"), lax.axis_index("c")
    me = 4 * x + 2 * y + c

    def peer(k):
        b = k + 1
        px = 1 - x if b & 4 else x
        py = 1 - y if b & 2 else y
        pc = 1 - c if b & 1 else c
        return (px, py, pc), 4 * px + 2 * py + pc

    def win(ref, ax, idx, w):
        return ref.at[tuple([slice(None)] * ax + [pl.ds(idx * w, w)])]

    def ends(j, mine, theirs):
        _, kind, ax = items[j]
        if kind == "gather":
            return srcs[j], win(outs[j], ax, mine, widths[j])
        return win(srcs[j], ax, theirs, widths[j]), outs[j].at[mine]

    return me, peer, ends


_HBM = pl.BlockSpec(memory_space=pltpu.HBM)
_SEM = pl.BlockSpec(memory_space=pltpu.SEMAPHORE)


def _exchange_start(items, name):
    n = len(items)
    widths, out_shapes = _exchange_plan(items)
    meta = [(None, kind, ax) for _, kind, ax in items]

    def body(*refs):
        srcs, lands = refs[:n], refs[n:2 * n]
        send, recv, lsem = refs[2 * n], refs[2 * n + 1], refs[2 * n + 2]
        token = refs[-1]
        me, peer, ends = _exchange_refs(meta, widths, srcs, lands)
        for j in range(n):
            for k in range(N_DEV - 1):
                dev, pid = peer(k)
                s, d = ends(j, me, pid)
                q = j * (N_DEV - 1) + k
                pltpu.make_async_remote_copy(s, d, send.at[q], recv.at[q], device_id=dev,
                                             device_id_type=pl.DeviceIdType.MESH).start()
        for j in range(n):
            s, d = ends(j, me, me)
            pltpu.make_async_copy(s, d, lsem.at[j]).start()
        token[...] = jnp.zeros_like(token)

    srcs = [pltpu.with_memory_space_constraint(it[0], pltpu.HBM) for it in items]
    lands = [pltpu.with_memory_space_constraint(lax.empty(shp, dt), pltpu.HBM) for shp, dt in out_shapes]
    outs = pl.pallas_call(
        body, name=name,
        out_shape=(pltpu.SemaphoreType.DMA((n * (N_DEV - 1),)), pltpu.SemaphoreType.DMA((n * (N_DEV - 1),)),
                   pltpu.SemaphoreType.DMA((n,)),
                   *[pltpu.HBM(s.shape, s.dtype) for s in srcs], *[pltpu.HBM(shp, dt) for shp, dt in out_shapes],
                   jax.ShapeDtypeStruct((8, LANE), F32)),
        in_specs=[_HBM] * (2 * n),
        out_specs=(_SEM, _SEM, _SEM, *([_HBM] * (2 * n)), pl.BlockSpec(memory_space=pltpu.VMEM)),
        input_output_aliases={i: 3 + i for i in range(2 * n)},
        compiler_params=pltpu.CompilerParams(has_side_effects=pltpu.SideEffectType.DATAFLOW_SIDE_EFFECTING),
    )(*srcs, *lands)
    handle = (meta, widths, outs[0], outs[1], outs[2], outs[3:3 + n], outs[3 + n:3 + 2 * n])
    return handle, outs[-1]


def _exchange_wait(handle, after, name):
    meta, widths, send_sem, recv_sem, local_sem, src_thru, land_thru = handle
    n = len(meta)

    def body(*refs):
        srcs, lands = refs[:n], refs[n:2 * n]
        send, recv, lsem = refs[2 * n], refs[2 * n + 1], refs[2 * n + 2]
        me, peer, ends = _exchange_refs(meta, widths, srcs, lands)
        for j in range(n):
            for k in range(N_DEV - 1):
                dev, pid = peer(k)
                q = j * (N_DEV - 1) + k
                s, d = ends(j, me, pid)
                pltpu.make_async_remote_copy(s, d, send.at[q], recv.at[q], device_id=dev,
                                             device_id_type=pl.DeviceIdType.MESH).wait_send()
                s, d = ends(j, pid, me)
                pltpu.make_async_remote_copy(s, d, send.at[q], recv.at[q], device_id=dev,
                                             device_id_type=pl.DeviceIdType.MESH).wait_recv()
        for j in range(n):
            s, d = ends(j, me, me)
            pltpu.make_async_copy(s, d, lsem.at[j]).wait()

    outs = pl.pallas_call(
        body, name=name,
        out_shape=tuple(pltpu.HBM(a.shape, a.dtype) for a in (*src_thru, *land_thru)),
        in_specs=[_HBM] * (2 * n) + [_SEM, _SEM, _SEM, pl.BlockSpec(memory_space=pl.ANY)],
        out_specs=tuple([_HBM] * (2 * n)),
        input_output_aliases={i: i for i in range(2 * n)},
        compiler_params=pltpu.CompilerParams(has_side_effects=pltpu.SideEffectType.DATAFLOW_SIDE_EFFECTING),
    )(*src_thru, *land_thru, send_sem, recv_sem, local_sem, after)
    return list(outs[n:])


def _pack(arrs):
    flat = jnp.concatenate([a.reshape(-1).astype(F32) for a in arrs])
    n = flat.shape[0]
    rows = -(-n // (8 * LANE)) * 8
    return jnp.pad(flat, (0, rows * LANE - n)).reshape(rows, LANE)


def _unpack(buf, shapes):
    flat = buf.reshape(-1)
    out, off = [], 0
    for s in shapes:
        sz = int(np.prod(s))
        out.append(flat[off:off + sz].reshape(s))
        off += sz
    return out


def _pad_axis(a, axis, size):
    pad = [(0, 0)] * a.ndim
    pad[axis] = (0, size - a.shape[axis])
    return jnp.pad(a, pad)


def _ffn_forward(x, g, wg, wu, wd, tag):
    a = _rms_fwd(x, g, f"{tag}_rms")
    gate, up, hid = _ffn_up(a, wg, wu, f"{tag}_up")
    out = _mm([(hid, wd)], "nn", F32, f"{tag}_down", scale=0.5, res=x)
    return out, (x, a, gate, up, hid)


def _ffn_backward(dxp, saved, g, wg, wu, wd, tag, emit=None, after=None):
    x, a, gate, up, hid = saved
    d_gate, d_up = _ffn_bwd_hid(dxp, wd, gate, up, f"{tag}_bwd_hid", after)
    d_wd = _mm([(hid, dxp)], "tn", _MXU, f"{tag}_dwd", scale=0.5, tk=2048)
    tok = emit("down", d_wd) if emit is not None else None
    d_wg = _mm([(a, d_gate)], "tn", _MXU, f"{tag}_dwg", tk=2048, after=tok)
    tok = emit("gate", d_wg) if emit is not None else None
    d_wu = _mm([(a, d_up)], "tn", _MXU, f"{tag}_dwu", tk=2048, after=tok)
    tok = emit("up", d_wu) if emit is not None else None
    d_a = _mm([(d_gate, wg), (d_up, wu)], "nt", F32, f"{tag}_da", after=tok)
    dx, dg = _rms_bwd(d_a, x, g, dxp, f"{tag}_rms_bwd")
    return dx, dg, d_wg, d_wu, d_wd


def _tile_vec(v, reps):
    return jnp.tile(v.reshape(1, -1), (1, reps))


def _mixer_forward(x1, p, consts, tag):
    h = _rms_fwd(x1, p["mix_norm"], f"{tag}_rms")
    zg = _mm([(h, p["w_zg"])], "nn", F32, f"{tag}_in_g")
    zc = _mm([(h, p["w_conf"])], "nn", F32, f"{tag}_in_c")
    zs = _mm([(h, p["w_sc"])], "nn", F32, f"{tag}_in_s")
    zw = _mm([(h, p["w_swa"])], "nn", F32, f"{tag}_in_w")
    zf = _mm([(h, p["w_fox"])], "nn", F32, f"{tag}_in_f")
    u1, act_c = _conf_fwd(zc, p["conf_dw"], p["conf_dw_b"], p["conf_ln_g"], p["conf_ln_b"], f"{tag}_conf")
    act_s = _sc_fwd(zs, p["sc_conv"], f"{tag}_sc")
    act_w = _swa_fwd(zw, p["swa_q_norm"], p["swa_k_norm"], p["swa_sink"], consts["bias"], consts["expand"], f"{tag}_swa")
    qa, ka, va, vat = _fox2_prep(zf, p["fox_q_norm"], p["fox_k_norm"], p["b_forget"], consts["sel"], f"{tag}_fox_prep")
    act_f, qb = _fox2_fwd(qa, ka, vat, consts["sel"], f"{tag}_fox")
    acts = (act_c, act_s, act_w, act_f)
    x2, merged = _merge_fwd(acts, zg, p["w_br"], p["w_out"], x1, f"{tag}_merge")
    saved = (x1, h, zg, zc, zs, zw, zf, u1, acts, qb, ka, va, merged)
    return x2, saved


def _mixer_backward(dx2, saved, p, consts, tag, after=None):
    x1, h, zg, zc, zs, zw, zf, u1, acts, qb, ka, va, merged = saved
    g = {}
    g["w_out"] = _mm([(merged, dx2)], "tn", _MXU, f"{tag}_dwout", tk=2048, after=after)
    d_c, d_s, d_w, d_f, dzg, g["w_br"] = _merge_bwd(dx2, acts, zg, p["w_br"], p["w_out"], f"{tag}_merge_bwd")
    du1, sm_c = _conf_bwd_ln(d_c, u1, p["conf_ln_g"], p["conf_ln_b"], f"{tag}_conf_bwd_ln")
    dzc, g["conf_dw"] = _conf_bwd_conv(zc, du1, p["conf_dw"], f"{tag}_conf_bwd_conv")
    g["conf_ln_g"], g["conf_ln_b"], g["conf_dw_b"] = sm_c[0], sm_c[1], sm_c[2]
    dzs, g["sc_conv"] = _sc_bwd(zs, d_s, p["sc_conv"], f"{tag}_sc_bwd")
    dzw, dgq, dgk, g["swa_sink"], g["rel_bias"] = _swa_bwd(
        zw, d_w, p["swa_q_norm"], p["swa_k_norm"], p["swa_sink"], consts["bias"], consts["bucket"], consts["expand"],
        f"{tag}_swa_bwd")
    g["swa_q_norm"], g["swa_k_norm"] = dgq, dgk
    doa = _fox2_bwd_prep(acts[3], d_f, consts["sel"], f"{tag}_fox_bwd_prep")
    dkn, dv, dfk, dqt = _fox2_bwd(qb, ka, va, doa, consts["sel"], f"{tag}_fox_bwd")
    dzf, sm_f = _fox2_post(zf, dqt, dkn, dv, dfk, consts["sel"], p["fox_q_norm"], p["fox_k_norm"], p["b_forget"], f"{tag}_fox_post")
    g["fox_q_norm"], g["fox_k_norm"], g["b_forget"] = sm_f[0], sm_f[1], sm_f[2]
    parts = ((dzg, "w_zg"), (dzc, "w_conf"), (dzs, "w_sc"), (dzw, "w_swa"), (dzf, "w_fox"))
    for dz, wname in parts:
        g[wname] = _mm([(h, dz)], "tn", _MXU, f"{tag}_d{wname}", tk=2048)
    dh = _mm([(dz, p[wname]) for dz, wname in parts], "nt", F32, f"{tag}_dh", tm=512)
    dx1, g["mix_norm"] = _rms_bwd(dh, x1, p["mix_norm"], dx2, f"{tag}_rms_bwd")
    return dx1, g


W_NAMES = ['rel_bias', 'ffn1_norm', 'ffn1_w_gate', 'ffn1_w_up', 'ffn1_w_down', 'mix_norm', 'w_in', 'b_forget', 'conf_dw',
           'conf_dw_b', 'conf_ln_g', 'conf_ln_b', 'conf_w_out', 'sc_conv', 'sc_w_out', 'swa_q_norm', 'swa_k_norm',
           'swa_sink', 'swa_w_o', 'fox_q_norm', 'fox_k_norm', 'fox_w_o', 'w_out', 'ffn2_norm', 'ffn2_w_gate',
           'ffn2_w_up', 'ffn2_w_down']
SMALL = ['rel_bias', 'ffn1_norm', 'mix_norm', 'b_forget', 'conf_dw', 'conf_dw_b', 'conf_ln_g', 'conf_ln_b', 'sc_conv',
         'swa_q_norm', 'swa_k_norm', 'swa_sink', 'fox_q_norm', 'fox_k_norm', 'ffn2_norm']
BRANCH_W = ['conf_w_out', 'sc_w_out', 'swa_w_o', 'fox_w_o']
IN_CONF, IN_SC, IN_SWA, IN_FOX, IN_FF = (0, 512), (512, 1280), (1280, 1792), (1792, 2560), (2560, 2564)


def _step(w, m, v, x, loss_target):
    T, D = x.shape
    L = w["w_out"].shape[0]
    fs = w["ffn1_w_gate"].shape[2]
    fsp = -(-fs // LANE) * LANE
    dev = 4 * lax.axis_index("x") + 2 * lax.axis_index("y") + lax.axis_index("c")

    def cast(a):
        return a.astype(_MXU)

    win = w["w_in"]
    fox_cols = jnp.concatenate([win[..., IN_FOX[0]:IN_FF[1]],
                                jnp.zeros(win.shape[:2] + (LANE - (IN_FF[1] - IN_FF[0]),), win.dtype)], axis=-1)
    shards = {
        "ffn1_w_gate": (cast(_pad_axis(w["ffn1_w_gate"], 2, fsp)), 2),
        "ffn1_w_up": (cast(_pad_axis(w["ffn1_w_up"], 2, fsp)), 2),
        "ffn1_w_down": (cast(_pad_axis(w["ffn1_w_down"], 1, fsp)), 1),
        "ffn2_w_gate": (cast(_pad_axis(w["ffn2_w_gate"], 2, fsp)), 2),
        "ffn2_w_up": (cast(_pad_axis(w["ffn2_w_up"], 2, fsp)), 2),
        "ffn2_w_down": (cast(_pad_axis(w["ffn2_w_down"], 1, fsp)), 1),
        "w_zg": (cast(win[..., IN_FF[1]:]), 1),
        "w_conf": (cast(win[..., IN_CONF[0]:IN_CONF[1]]), 1),
        "w_sc": (cast(win[..., IN_SC[0]:IN_SC[1]]), 1),
        "w_swa": (cast(win[..., IN_SWA[0]:IN_SWA[1]]), 1),
        "w_fox": (cast(fox_cols), 1),
        "w_out": (cast(w["w_out"]), 1),
        "w_br": (cast(jnp.stack([w[n] for n in BRANCH_W], axis=1)), 3),
    }
    big = list(shards)
    conv_shard = jnp.concatenate([jnp.swapaxes(w["conf_dw"], 1, 2), jnp.swapaxes(w["sc_conv"], 1, 2)], axis=2)

    stages = (("ffn1", ["ffn1_w_gate", "ffn1_w_up"]), ("ffn1d", ["ffn1_w_down"]),
              ("mix", ["w_zg", "w_conf", "w_sc", "w_swa", "w_fox", "w_out", "w_br"]),
              ("ffn2", ["ffn2_w_gate", "ffn2_w_up", "ffn2_w_down"]))
    stage_names = dict(stages)
    flight = {}

    def depart(l, st, dep):
        items = [(shards[n][0][l], "gather", shards[n][1] - 1) for n in stage_names[st]]
        if (l, st) == (0, "mix"):
            items.append((conv_shard, "gather", 1))
        if dep is not None:
            src0 = items[0][0]
            zero = (dep[(0,) * dep.ndim].astype(F32) * 0.0).astype(src0.dtype)
            items[0] = (src0 + zero,) + items[0][1:]
        flight[l, st], tok = _exchange_start(items, f"gather_start_l{l}_{st}")
        return tok

    def arrive(l, st, after):
        got = _exchange_wait(flight.pop((l, st)), after, f"gather_wait_l{l}_{st}")
        params[l].update(zip(stage_names[st], got))
        if (l, st) == (0, "mix"):
            conv_full = jnp.swapaxes(got[-1], 1, 2)
            for i, q in enumerate(params):
                q["conf_dw"] = _pad_axis(conv_full[i, :CONV_K], 0, CONV_HALO)
                q["sc_conv"] = _pad_axis(conv_full[i, CONV_K:], 0, SC_HALO)
        return got[0]

    bucket = jnp.asarray(_swa_bucket_matrix(min(SWA_TQ, T)))
    consts = {"bucket": bucket, "expand": jnp.asarray(_kv_expand_matrix(), _MXU),
              "sel": jnp.asarray(_head_select_matrix(), _MXU),
              "bias": _swa_bias(w["rel_bias"], bucket, "swa_bias")}

    def layer_params(l):
        p = {}
        for n in ("ffn1_norm", "mix_norm", "ffn2_norm", "conf_dw_b", "conf_ln_g", "conf_ln_b"):
            p[n] = w[n][l].reshape(1, -1)
        p["swa_q_norm"], p["fox_q_norm"] = _tile_vec(w["swa_q_norm"][l], 4), _tile_vec(w["fox_q_norm"][l], 4)
        p["swa_k_norm"], p["fox_k_norm"] = _tile_vec(w["swa_k_norm"][l], 2), _tile_vec(w["fox_k_norm"][l], 4)
        p["swa_sink"] = w["swa_sink"][l].reshape(1, 4)
        p["b_forget"] = _pad_axis(w["b_forget"][l].reshape(1, 4), 1, LANE)
        return p

    params = [layer_params(l) for l in range(L)]
    saved = [None] * L
    cur = x
    first = depart(0, "ffn1", None)
    for l, p in enumerate(params):
        a = _rms_fwd(cur, p["ffn1_norm"] + (0.0 if l else first[0:1, 0:1]), f"l{l}_ffn1_rms")
        got = arrive(l, "ffn1", a)
        tok = depart(l, "mix", depart(l, "ffn1d", got))
        gate, up, hid = _ffn_up(a, p["ffn1_w_gate"], p["ffn1_w_up"], f"l{l}_ffn1_up", after=tok)
        got = arrive(l, "ffn1d", hid)
        tok = depart(l, "ffn2", got)
        x1 = _mm([(hid, p["ffn1_w_down"])], "nn", F32, f"l{l}_ffn1_down", scale=0.5, res=cur, after=tok)
        s1 = (cur, a, gate, up, hid)
        got = arrive(l, "mix", x1)
        zero = depart(l + 1, "ffn1", got)[0:1, 0:1] if l + 1 < L else 0.0
        x2, s2 = _mixer_forward(x1, dict(p, mix_norm=p["mix_norm"] + zero), consts, f"l{l}_mix")
        arrive(l, "ffn2", x2)
        cur, s3 = _ffn_forward(x2, p["ffn2_norm"], p["ffn2_w_gate"], p["ffn2_w_up"], p["ffn2_w_down"], f"l{l}_ffn2")
        saved[l] = (s1, s2, s3)
    dcur, loss_part = _loss_grad(cur, loss_target)

    grads = [None] * L
    leaving = []

    def leave(l, st, names, g):
        h, tok = _exchange_start([(g[n], "scatter", shards[n][1] - 1) for n in names], f"scatter_start_l{l}_{st}")
        leaving.append((l, st, names, h))
        return tok

    tok = None
    for l in reversed(range(L)):
        p = params[l]
        s1, s2, s3 = saved[l]
        g = {}
        dcur, g["ffn2_norm"], g["ffn2_w_gate"], g["ffn2_w_up"], g["ffn2_w_down"] = _ffn_backward(
            dcur, s3, p["ffn2_norm"], p["ffn2_w_gate"], p["ffn2_w_up"], p["ffn2_w_down"], f"l{l}_ffn2", after=tok)
        tok = leave(l, "ffn2", stage_names["ffn2"], g)
        dcur, gm = _mixer_backward(dcur, s2, p, consts, f"l{l}_mix", after=tok)
        g.update(gm)
        tok = leave(l, "mix", stage_names["mix"], g)
        ffn1 = ["ffn1_w_gate", "ffn1_w_up", "ffn1_w_down"]
        emit = (lambda which, arr, l=l: leave(l, which, [f"ffn1_w_{which}"], {f"ffn1_w_{which}": arr})) if l == 0 else None
        dcur, g["ffn1_norm"], g["ffn1_w_gate"], g["ffn1_w_up"], g["ffn1_w_down"] = _ffn_backward(
            dcur, s1, p["ffn1_norm"], p["ffn1_w_gate"], p["ffn1_w_up"], p["ffn1_w_down"], f"l{l}_ffn1", emit, after=tok)
        if emit is None:
            tok = leave(l, "ffn1", ffn1, g)
        grads[l] = g
    grad_x = dcur

    gsum = {n: [None] * L for n in big}
    for l, st, names, h in leaving:
        for n, r in zip(names, _exchange_wait(h, grad_x, f"scatter_wait_l{l}_{st}")):
            gsum[n][l] = _sum8(r, f"sum_{n}_l{l}")
    gsum = {n: jnp.stack(parts) for n, parts in gsum.items()}
    gw = {}
    for n in ("ffn1_w_gate", "ffn1_w_up", "ffn2_w_gate", "ffn2_w_up"):
        gw[n] = gsum[n][:, :, :fs]
    for n in ("ffn1_w_down", "ffn2_w_down"):
        gw[n] = gsum[n][:, :fs, :]
    gw["w_out"] = gsum["w_out"]
    for i, n in enumerate(BRANCH_W):
        gw[n] = gsum["w_br"][:, i]
    gw["w_in"] = jnp.concatenate([gsum["w_conf"], gsum["w_sc"], gsum["w_swa"],
                                  gsum["w_fox"][..., :IN_FF[1] - IN_FOX[0]], gsum["w_zg"]], axis=-1)

    def small_partial(n):
        per_layer = [grads[l][n] for l in range(L)]
        if n == "rel_bias":
            return sum(pl_[:, :4] for pl_ in per_layer)
        if n in ("swa_sink", "b_forget"):
            return jnp.stack([a.reshape(-1)[:4] for a in per_layer])
        if n in ("swa_q_norm", "fox_q_norm", "fox_k_norm"):
            return jnp.stack([a.reshape(4, HEAD).sum(0) for a in per_layer])
        if n == "swa_k_norm":
            return jnp.stack([a.reshape(2, HEAD).sum(0) for a in per_layer])
        if n == "conf_dw":
            return jnp.stack([a[:CONV_K] for a in per_layer])
        if n == "sc_conv":
            return jnp.stack([a[:SC_K] for a in per_layer])
        return jnp.stack([a.reshape(-1) for a in per_layer])

    partial = [small_partial(n) for n in SMALL]
    small_shapes = [a.shape for a in partial]
    all_parts = _exchange([(_pack(partial), "gather", 0)], "gather_small_grads")[0]
    rows = all_parts.shape[0] // N_DEV
    small_sum = _unpack(_sum8(all_parts.reshape(N_DEV, rows, LANE), "sum_small"), small_shapes)
    for n, a in zip(SMALL, small_sum):
        if n in ("conf_dw", "sc_conv"):
            cs = w[n].shape[2]
            a = lax.dynamic_slice_in_dim(a, dev * cs, cs, axis=2)
        gw[n] = a

    delta, new_m, new_v = {}, {}, {}
    for n in W_NAMES:
        if n not in SMALL:
            delta[n], new_m[n], new_v[n] = _adamw(w[n], gw[n], m[n], v[n], f"adamw_{n}")
    shapes = [w[n].shape for n in SMALL]
    outs = _adamw(_pack([w[n] for n in SMALL]), _pack([gw[n] for n in SMALL]), _pack([m[n] for n in SMALL]),
                  _pack([v[n] for n in SMALL]), "adamw_small")
    for res, out in zip((delta, new_m, new_v), outs):
        for n, a in zip(SMALL, _unpack(out, shapes)):
            res[n] = a

    loss = lax.psum(loss_part[0, 0], ("x", "y", "c"))
    return loss, grad_x, gw, delta, new_m, new_v


def kernel(x, rel_bias, ffn1_norm, ffn1_w_gate, ffn1_w_up, ffn1_w_down, mix_norm, w_in, b_forget, conf_dw, conf_dw_b, conf_ln_g, conf_ln_b, conf_w_out, sc_conv, sc_w_out, swa_q_norm, swa_k_norm, swa_sink, swa_w_o, fox_q_norm, fox_k_norm, fox_w_o, w_out, ffn2_norm, ffn2_w_gate, ffn2_w_up, ffn2_w_down, loss_target, m_rel_bias, m_ffn1_norm, m_ffn1_w_gate, m_ffn1_w_up, m_ffn1_w_down, m_mix_norm, m_w_in, m_b_forget, m_conf_dw, m_conf_dw_b, m_conf_ln_g, m_conf_ln_b, m_conf_w_out, m_sc_conv, m_sc_w_out, m_swa_q_norm, m_swa_k_norm, m_swa_sink, m_swa_w_o, m_fox_q_norm, m_fox_k_norm, m_fox_w_o, m_w_out, m_ffn2_norm, m_ffn2_w_gate, m_ffn2_w_up, m_ffn2_w_down, v_rel_bias, v_ffn1_norm, v_ffn1_w_gate, v_ffn1_w_up, v_ffn1_w_down, v_mix_norm, v_w_in, v_b_forget, v_conf_dw, v_conf_dw_b, v_conf_ln_g, v_conf_ln_b, v_conf_w_out, v_sc_conv, v_sc_w_out, v_swa_q_norm, v_swa_k_norm, v_swa_sink, v_swa_w_o, v_fox_q_norm, v_fox_k_norm, v_fox_w_o, v_w_out, v_ffn2_norm, v_ffn2_w_gate, v_ffn2_w_up, v_ffn2_w_down):
    args = locals()
    w = {n: args[n] for n in W_NAMES}
    m = {n: args["m_" + n] for n in W_NAMES}
    v = {n: args["v_" + n] for n in W_NAMES}
    T, D = x.shape[-2:]
    loss, grad_x, gw, delta, new_m, new_v = _step(w, m, v, x.reshape(T, D), loss_target.reshape(T, D))
    return (loss, grad_x.reshape(x.shape), *[gw[n] for n in W_NAMES], *[delta[n] for n in W_NAMES],
            *[new_m[n] for n in W_NAMES], *[new_v[n] for n in W_NAMES])
```

```python
import math

import numpy as np
import jax
import jax.numpy as jnp
from jax import lax
from jax.experimental import pallas as pl
from jax.experimental.pallas import tpu as pltpu

F32 = jnp.float32
_MXU = jnp.bfloat16
EPS = 1e-6
NEG_INF = -1e30
HEAD = 64
CH = 256
WINDOW = 128
CONV_K = 31
SC_K = 3
CONV_HALO = 32
SC_HALO = 8
N_BUCKETS = 32
MAX_DISTANCE = 128
N_DEV = 8
LANE = 128
ROW_TILE = 512
VMEM_LIMIT = 48 * 1024 * 1024
ADAM_LR, ADAM_B1, ADAM_B2, ADAM_EPS, ADAM_WD, ADAM_STEP = 0.001, 0.9, 0.999, 1e-08, 0.01, 10

_NN = (((1,), (0,)), ((), ()))
_NT = (((1,), (1,)), ((), ()))
_TN = (((0,), (0,)), ((), ()))


def _cp(*sem):
    return pltpu.CompilerParams(dimension_semantics=sem, vmem_limit_bytes=VMEM_LIMIT)


def _tile(n, pref, align=LANE):
    t = (min(n, pref) // align) * align
    while t >= align:
        if n % t == 0:
            return t
        t -= align
    return n


def _dot(a, b, dims=_NN):
    return lax.dot_general(a.astype(_MXU), b.astype(_MXU), dims, preferred_element_type=F32)


def _split3(x):
    hi = x.astype(_MXU)
    r1 = x - hi.astype(F32)
    mid = r1.astype(_MXU)
    lo = (r1 - mid.astype(F32)).astype(_MXU)
    return hi, mid, lo


def _exact_dot(a, b, dims, data):
    if data == "a":
        return sum(lax.dot_general(p, b.astype(_MXU), dims, preferred_element_type=F32) for p in _split3(a))
    return sum(lax.dot_general(a.astype(_MXU), p, dims, preferred_element_type=F32) for p in _split3(b))


def _sigmoid(x):
    return jax.nn.sigmoid(x)


def _lane_mask(width, h):
    lane = lax.broadcasted_iota(jnp.int32, (1, width), 1)
    return (lane >= h * HEAD) & (lane < (h + 1) * HEAD)


def _head_rms(x, g, nh):
    xx = x * x
    ms = jnp.zeros_like(x)
    for h in range(nh):
        mk = _lane_mask(x.shape[-1], h)
        s = jnp.sum(jnp.where(mk, xx, 0.0), axis=-1, keepdims=True) * (1.0 / HEAD)
        ms = jnp.where(mk, s, ms)
    r = lax.rsqrt(ms + EPS)
    return x * r * g, r


def _head_rms_bwd(dy, x, r, g, nh):
    w = dy * g
    wx = w * x
    c = jnp.zeros_like(x)
    for h in range(nh):
        mk = _lane_mask(x.shape[-1], h)
        s = jnp.sum(jnp.where(mk, wx, 0.0), axis=-1, keepdims=True) * (1.0 / HEAD)
        c = jnp.where(mk, s, c)
    dx = r * w - x * (r * r * r) * c
    dg = jnp.sum(dy * x * r, axis=0, keepdims=True)
    return dx, dg


def _mm(pairs, mode, out_dtype, name, scale=None, res=None, tm=1024, tn=1024, tk=1024, after=None):
    a0, b0 = pairs[0]
    M = a0.shape[1] if mode == "tn" else a0.shape[0]
    N = b0.shape[0] if mode == "nt" else b0.shape[1]
    tm, tn = _tile(M, tm), _tile(N, tn)
    dims = {"nn": _NN, "nt": _NT, "tn": _TN}[mode]
    tks, nks, offs = [], [], []
    for a, _ in pairs:
        K = a.shape[0] if mode == "tn" else a.shape[1]
        t = _tile(K, tk)
        tks.append(t)
        nks.append(K // t)
        offs.append(sum(nks[:-1]))
    nk_tot = sum(nks)
    in_specs, args = [], []
    for (a, b), t, nk, off in zip(pairs, tks, nks, offs):
        def kk(k, off=off, nk=nk):
            return jnp.clip(k - off, 0, nk - 1)
        if mode == "tn":
            in_specs.append(pl.BlockSpec((t, tm), lambda i, j, k, kk=kk: (kk(k), i)))
        else:
            in_specs.append(pl.BlockSpec((tm, t), lambda i, j, k, kk=kk: (i, kk(k))))
        if mode == "nt":
            in_specs.append(pl.BlockSpec((tn, t), lambda i, j, k, kk=kk: (j, kk(k))))
        else:
            in_specs.append(pl.BlockSpec((t, tn), lambda i, j, k, kk=kk: (kk(k), j)))
        args += [a, b]
    if res is not None:
        in_specs.append(pl.BlockSpec((tm, tn), lambda i, j, k: (i, j)))
        args.append(res)
    if after is not None:
        in_specs.append(pl.BlockSpec(memory_space=pl.ANY))
        args.append(after)
    npairs = len(pairs)

    def body(*refs):
        ab = refs[:2 * npairs]
        res_ref = refs[2 * npairs] if res is not None else None
        o_ref = refs[2 * npairs + (res is not None) + (after is not None)]
        acc = refs[-1]
        k = pl.program_id(2)

        def finish(r):
            if scale is not None:
                r = r * scale
            if res_ref is not None:
                r = r + res_ref[...]
            o_ref[...] = r.astype(o_ref.dtype)

        if nk_tot == 1:
            finish(_dot(ab[0][...], ab[1][...], dims))
            return

        @pl.when(k == 0)
        def _():
            acc[...] = jnp.zeros_like(acc)

        for p in range(npairs):
            @pl.when(jnp.logical_and(k >= offs[p], k < offs[p] + nks[p]))
            def _(p=p):
                acc[...] += _dot(ab[2 * p][...], ab[2 * p + 1][...], dims)

        @pl.when(k == nk_tot - 1)
        def _():
            finish(acc[...])

    return pl.pallas_call(
        body, name=name, grid=(M // tm, N // tn, nk_tot), in_specs=in_specs,
        out_specs=pl.BlockSpec((tm, tn), lambda i, j, k: (i, j)),
        out_shape=jax.ShapeDtypeStruct((M, N), out_dtype),
        scratch_shapes=[pltpu.VMEM((tm, tn), F32)],
        compiler_params=_cp("parallel", "parallel", "arbitrary"))(*args)


def _rms_fwd(x, g, name):
    T, D = x.shape
    tm = _tile(T, ROW_TILE)

    def body(x_ref, g_ref, o_ref):
        xv = x_ref[...]
        r = lax.rsqrt(jnp.mean(xv * xv, axis=-1, keepdims=True) + EPS)
        o_ref[...] = (xv * r * g_ref[...]).astype(o_ref.dtype)

    return pl.pallas_call(
        body, name=name, grid=(T // tm,),
        in_specs=[pl.BlockSpec((tm, D), lambda i: (i, 0)), pl.BlockSpec((1, D), lambda i: (0, 0))],
        out_specs=pl.BlockSpec((tm, D), lambda i: (i, 0)),
        out_shape=jax.ShapeDtypeStruct((T, D), _MXU), compiler_params=_cp("parallel"))(x, g)


def _rms_bwd(da, x, g, dres, name):
    T, D = x.shape
    tm = _tile(T, ROW_TILE)

    def body(da_ref, x_ref, g_ref, dr_ref, dx_ref, dg_ref):
        @pl.when(pl.program_id(0) == 0)
        def _():
            dg_ref[...] = jnp.zeros_like(dg_ref)

        xv, dav = x_ref[...], da_ref[...]
        r = lax.rsqrt(jnp.mean(xv * xv, axis=-1, keepdims=True) + EPS)
        w = dav * g_ref[...]
        c = jnp.mean(w * xv, axis=-1, keepdims=True)
        dx_ref[...] = dr_ref[...] + (r * w - xv * (r * r * r) * c)
        dg_ref[...] += jnp.sum(dav * xv * r, axis=0, keepdims=True)

    row = pl.BlockSpec((tm, D), lambda i: (i, 0))
    vec = pl.BlockSpec((1, D), lambda i: (0, 0))
    return pl.pallas_call(
        body, name=name, grid=(T // tm,), in_specs=[row, row, vec, row], out_specs=[row, vec],
        out_shape=[jax.ShapeDtypeStruct((T, D), F32), jax.ShapeDtypeStruct((1, D), F32)],
        compiler_params=_cp("arbitrary"))(da, x, g, dres)


def _loss_grad(y, tgt):
    T, D = y.shape
    tm = _tile(T, ROW_TILE)

    def body(y_ref, t_ref, dy_ref, l_ref):
        @pl.when(pl.program_id(0) == 0)
        def _():
            l_ref[...] = jnp.zeros_like(l_ref)

        d = y_ref[...] - t_ref[...]
        dy_ref[...] = d * (1.0 / D)
        per_tok = jnp.mean(d * d, axis=-1, keepdims=True)
        l_ref[...] += 0.5 * jnp.sum(per_tok, axis=0, keepdims=True)

    row = pl.BlockSpec((tm, D), lambda i: (i, 0))
    return pl.pallas_call(
        body, name="loss_grad", grid=(T // tm,), in_specs=[row, row],
        out_specs=[row, pl.BlockSpec((1, 1), lambda i: (0, 0))],
        out_shape=[jax.ShapeDtypeStruct((T, D), F32), jax.ShapeDtypeStruct((1, 1), F32)],
        compiler_params=_cp("arbitrary"))(y, tgt)


def _ffn_up(a, wg, wu, name, after=None):
    T, D = a.shape
    Fp = wg.shape[1]
    tm, tn = _tile(T, ROW_TILE), _tile(Fp, 768)
    extra = [] if after is None else [after]

    def body(*refs):
        a_ref, wg_ref, wu_ref = refs[:3]
        g_ref, u_ref, h_ref = refs[-3:]
        av = a_ref[...]
        g = _dot(av, wg_ref[...])
        u = _dot(av, wu_ref[...])
        g_ref[...] = g.astype(g_ref.dtype)
        u_ref[...] = u.astype(u_ref.dtype)
        h_ref[...] = (g * _sigmoid(g) * u).astype(h_ref.dtype)

    wspec = pl.BlockSpec((D, tn), lambda j, i: (0, j))
    ospec = pl.BlockSpec((tm, tn), lambda j, i: (i, j))
    osh = jax.ShapeDtypeStruct((T, Fp), _MXU)
    return pl.pallas_call(
        body, name=name, grid=(Fp // tn, T // tm),
        in_specs=[pl.BlockSpec((tm, D), lambda j, i: (i, 0)), wspec, wspec] + [pl.BlockSpec(memory_space=pl.ANY)] * len(extra),
        out_specs=[ospec, ospec, ospec], out_shape=[osh, osh, osh],
        compiler_params=_cp("parallel", "parallel"))(a, wg, wu, *extra)


def _ffn_bwd_hid(dxp, wd, gate, up, name, after=None):
    T, D = dxp.shape
    Fp = wd.shape[0]
    tm, tn = _tile(T, ROW_TILE), _tile(Fp, 768)
    extra = [] if after is None else [after]

    def body(*refs):
        dx_ref, wd_ref, g_ref, u_ref = refs[:4]
        dg_ref, du_ref = refs[-2:]
        dh = 0.5 * _dot(dx_ref[...], wd_ref[...], _NT)
        g = g_ref[...].astype(F32)
        u = u_ref[...].astype(F32)
        s = _sigmoid(g)
        du_ref[...] = (dh * (g * s)).astype(du_ref.dtype)
        dg_ref[...] = (dh * u * (s * (1.0 + g * (1.0 - s)))).astype(dg_ref.dtype)

    tspec = pl.BlockSpec((tm, tn), lambda j, i: (i, j))
    osh = jax.ShapeDtypeStruct((T, Fp), _MXU)
    return pl.pallas_call(
        body, name=name, grid=(Fp // tn, T // tm),
        in_specs=[pl.BlockSpec((tm, D), lambda j, i: (i, 0)), pl.BlockSpec((tn, D), lambda j, i: (j, 0)), tspec, tspec]
        + [pl.BlockSpec(memory_space=pl.ANY)] * len(extra),
        out_specs=[tspec, tspec], out_shape=[osh, osh],
        compiler_params=_cp("parallel", "parallel"))(dxp, wd, gate, up, *extra)


def _conf_fwd(zc, dw, b, lng, lnb, name):
    T = zc.shape[0]
    tm = _tile(T, ROW_TILE)
    r = tm // CONV_HALO

    def body(z_ref, zh_ref, dw_ref, b_ref, g_ref, lb_ref, u1_ref, act_ref, ext):
        i = pl.program_id(0)
        cur = z_ref[...]
        ext[pl.ds(CONV_HALO, tm), :] = cur[:, :CH] * _sigmoid(cur[:, CH:])
        hal = zh_ref[...]
        ext[pl.ds(0, CONV_HALO), :] = jnp.where(i > 0, hal[:, :CH] * _sigmoid(hal[:, CH:]), 0.0)
        acc = jnp.zeros((tm, CH), F32)
        for k in range(CONV_K):
            acc = acc + dw_ref[pl.ds(k, 1), :] * ext[pl.ds(CONV_HALO - (CONV_K - 1) + k, tm), :]
        u1 = acc + b_ref[...]
        u1_ref[...] = u1
        mu = jnp.mean(u1, axis=-1, keepdims=True)
        var = jnp.mean(jnp.square(u1 - mu), axis=-1, keepdims=True)
        u2 = (u1 - mu) * lax.rsqrt(var + EPS) * g_ref[...] + lb_ref[...]
        act_ref[...] = u2 * _sigmoid(u2)

    vec = pl.BlockSpec((1, CH), lambda i: (0, 0))
    row = pl.BlockSpec((tm, CH), lambda i: (i, 0))
    osh = jax.ShapeDtypeStruct((T, CH), F32)
    return pl.pallas_call(
        body, name=name, grid=(T // tm,),
        in_specs=[pl.BlockSpec((tm, 2 * CH), lambda i: (i, 0)),
                  pl.BlockSpec((CONV_HALO, 2 * CH), lambda i: (jnp.maximum(i * r - 1, 0), 0)),
                  pl.BlockSpec((CONV_HALO, CH), lambda i: (0, 0)), vec, vec, vec],
        out_specs=[row, row], out_shape=[osh, osh],
        scratch_shapes=[pltpu.VMEM((tm + CONV_HALO, CH), F32)],
        compiler_params=_cp("parallel"))(zc, zc, dw, b, lng, lnb)


def _conf_bwd_ln(dact, u1, lng, lnb, name):
    T = u1.shape[0]
    tm = _tile(T, ROW_TILE)

    def body(da_ref, u_ref, g_ref, lb_ref, du_ref, sm_ref):
        @pl.when(pl.program_id(0) == 0)
        def _():
            sm_ref[...] = jnp.zeros_like(sm_ref)

        u1v = u_ref[...]
        mu = jnp.mean(u1v, axis=-1, keepdims=True)
        cen = u1v - mu
        rstd = lax.rsqrt(jnp.mean(cen * cen, axis=-1, keepdims=True) + EPS)
        y = cen * rstd
        u2 = y * g_ref[...] + lb_ref[...]
        s = _sigmoid(u2)
        du2 = da_ref[...] * (s * (1.0 + u2 * (1.0 - s)))
        dy = du2 * g_ref[...]
        du1 = rstd * (dy - jnp.mean(dy, axis=-1, keepdims=True) - y * jnp.mean(dy * y, axis=-1, keepdims=True))
        du_ref[...] = du1
        sm_ref[pl.ds(0, 1), :] += jnp.sum(du2 * y, axis=0, keepdims=True)
        sm_ref[pl.ds(1, 1), :] += jnp.sum(du2, axis=0, keepdims=True)
        sm_ref[pl.ds(2, 1), :] += jnp.sum(du1, axis=0, keepdims=True)

    vec = pl.BlockSpec((1, CH), lambda i: (0, 0))
    row = pl.BlockSpec((tm, CH), lambda i: (i, 0))
    return pl.pallas_call(
        body, name=name, grid=(T // tm,), in_specs=[row, row, vec, vec],
        out_specs=[row, pl.BlockSpec((8, CH), lambda i: (0, 0))],
        out_shape=[jax.ShapeDtypeStruct((T, CH), F32), jax.ShapeDtypeStruct((8, CH), F32)],
        compiler_params=_cp("arbitrary"))(dact, u1, lng, lnb)


def _conf_bwd_conv(zc, du1, dw, name):
    T = zc.shape[0]
    tm = _tile(T, ROW_TILE)
    r = tm // CONV_HALO
    nt = T // tm
    nh = T // CONV_HALO

    def body(z_ref, zh_ref, d_ref, dn_ref, dw_ref, dz_ref, ddw_ref, ext_u, ext_d):
        i = pl.program_id(0)

        @pl.when(i == 0)
        def _():
            ddw_ref[...] = jnp.zeros_like(ddw_ref)

        cur = z_ref[...]
        ca = cur[:, :CH]
        sg = _sigmoid(cur[:, CH:])
        ext_u[pl.ds(CONV_HALO, tm), :] = ca * sg
        hal = zh_ref[...]
        ext_u[pl.ds(0, CONV_HALO), :] = jnp.where(i > 0, hal[:, :CH] * _sigmoid(hal[:, CH:]), 0.0)
        d = d_ref[...]
        ext_d[pl.ds(0, tm), :] = d
        ext_d[pl.ds(tm, CONV_HALO), :] = jnp.where(i < nt - 1, dn_ref[...], 0.0)
        acc = jnp.zeros((tm, CH), F32)
        for k in range(CONV_K):
            acc = acc + dw_ref[pl.ds(k, 1), :] * ext_d[pl.ds(CONV_K - 1 - k, tm), :]
            ddw_ref[pl.ds(k, 1), :] += jnp.sum(
                d * ext_u[pl.ds(CONV_HALO - (CONV_K - 1) + k, tm), :], axis=0, keepdims=True)
        dz_ref[:, :CH] = (acc * sg).astype(dz_ref.dtype)
        dz_ref[:, CH:] = (acc * ca * sg * (1.0 - sg)).astype(dz_ref.dtype)

    return pl.pallas_call(
        body, name=name, grid=(nt,),
        in_specs=[pl.BlockSpec((tm, 2 * CH), lambda i: (i, 0)),
                  pl.BlockSpec((CONV_HALO, 2 * CH), lambda i: (jnp.maximum(i * r - 1, 0), 0)),
                  pl.BlockSpec((tm, CH), lambda i: (i, 0)),
                  pl.BlockSpec((CONV_HALO, CH), lambda i: (jnp.minimum((i + 1) * r, nh - 1), 0)),
                  pl.BlockSpec((CONV_HALO, CH), lambda i: (0, 0))],
        out_specs=[pl.BlockSpec((tm, 2 * CH), lambda i: (i, 0)), pl.BlockSpec((CONV_HALO, CH), lambda i: (0, 0))],
        out_shape=[jax.ShapeDtypeStruct((T, 2 * CH), _MXU), jax.ShapeDtypeStruct((CONV_HALO, CH), F32)],
        scratch_shapes=[pltpu.VMEM((tm + CONV_HALO, CH), F32), pltpu.VMEM((tm + CONV_HALO, CH), F32)],
        compiler_params=_cp("arbitrary"))(zc, zc, du1, du1, dw)


def _sc_fwd(zs, w, name):
    T = zs.shape[0]
    tm = _tile(T, ROW_TILE)
    r = tm // SC_HALO

    def body(z_ref, zh_ref, w_ref, act_ref, ext):
        i = pl.program_id(0)
        cur = z_ref[...]
        ext[pl.ds(SC_HALO, tm), :] = cur[:, CH:2 * CH] * cur[:, 2 * CH:]
        hal = zh_ref[...]
        ext[pl.ds(0, SC_HALO), :] = jnp.where(i > 0, hal[:, CH:2 * CH] * hal[:, 2 * CH:], 0.0)
        v1 = jnp.zeros((tm, CH), F32)
        for k in range(SC_K):
            v1 = v1 + w_ref[pl.ds(k, 1), :] * ext[pl.ds(SC_HALO - (SC_K - 1) + k, tm), :]
        act_ref[...] = cur[:, :CH] * v1

    return pl.pallas_call(
        body, name=name, grid=(T // tm,),
        in_specs=[pl.BlockSpec((tm, 3 * CH), lambda i: (i, 0)),
                  pl.BlockSpec((SC_HALO, 3 * CH), lambda i: (jnp.maximum(i * r - 1, 0), 0)),
                  pl.BlockSpec((SC_HALO, CH), lambda i: (0, 0))],
        out_specs=pl.BlockSpec((tm, CH), lambda i: (i, 0)),
        out_shape=jax.ShapeDtypeStruct((T, CH), F32),
        scratch_shapes=[pltpu.VMEM((tm + SC_HALO, CH), F32)],
        compiler_params=_cp("parallel"))(zs, zs, w)


def _sc_bwd(zs, dact, w, name):
    T = zs.shape[0]
    tm = _tile(T, ROW_TILE)
    r = tm // SC_HALO
    nt = T // tm
    nh = T // SC_HALO

    def body(z_ref, zh_ref, zn_ref, d_ref, dn_ref, w_ref, dz_ref, dw_ref, ext_v, ext_d):
        i = pl.program_id(0)

        @pl.when(i == 0)
        def _():
            dw_ref[...] = jnp.zeros_like(dw_ref)

        cur = z_ref[...]
        sb, sc, sx = cur[:, :CH], cur[:, CH:2 * CH], cur[:, 2 * CH:]
        ext_v[pl.ds(SC_HALO, tm), :] = sc * sx
        hal = zh_ref[...]
        ext_v[pl.ds(0, SC_HALO), :] = jnp.where(i > 0, hal[:, CH:2 * CH] * hal[:, 2 * CH:], 0.0)
        da = d_ref[...]
        dv1 = da * sb
        ext_d[pl.ds(0, tm), :] = dv1
        ext_d[pl.ds(tm, SC_HALO), :] = jnp.where(i < nt - 1, dn_ref[...] * zn_ref[...][:, :CH], 0.0)
        v1 = jnp.zeros((tm, CH), F32)
        dv0 = jnp.zeros((tm, CH), F32)
        for k in range(SC_K):
            shifted = ext_v[pl.ds(SC_HALO - (SC_K - 1) + k, tm), :]
            v1 = v1 + w_ref[pl.ds(k, 1), :] * shifted
            dv0 = dv0 + w_ref[pl.ds(k, 1), :] * ext_d[pl.ds(SC_K - 1 - k, tm), :]
            dw_ref[pl.ds(k, 1), :] += jnp.sum(dv1 * shifted, axis=0, keepdims=True)
        dz_ref[:, :CH] = (da * v1).astype(dz_ref.dtype)
        dz_ref[:, CH:2 * CH] = (dv0 * sx).astype(dz_ref.dtype)
        dz_ref[:, 2 * CH:] = (dv0 * sc).astype(dz_ref.dtype)

    return pl.pallas_call(
        body, name=name, grid=(nt,),
        in_specs=[pl.BlockSpec((tm, 3 * CH), lambda i: (i, 0)),
                  pl.BlockSpec((SC_HALO, 3 * CH), lambda i: (jnp.maximum(i * r - 1, 0), 0)),
                  pl.BlockSpec((SC_HALO, 3 * CH), lambda i: (jnp.minimum((i + 1) * r, nh - 1), 0)),
                  pl.BlockSpec((tm, CH), lambda i: (i, 0)),
                  pl.BlockSpec((SC_HALO, CH), lambda i: (jnp.minimum((i + 1) * r, nh - 1), 0)),
                  pl.BlockSpec((SC_HALO, CH), lambda i: (0, 0))],
        out_specs=[pl.BlockSpec((tm, 3 * CH), lambda i: (i, 0)), pl.BlockSpec((SC_HALO, CH), lambda i: (0, 0))],
        out_shape=[jax.ShapeDtypeStruct((T, 3 * CH), _MXU), jax.ShapeDtypeStruct((SC_HALO, CH), F32)],
        scratch_shapes=[pltpu.VMEM((tm + SC_HALO, CH), F32), pltpu.VMEM((tm + SC_HALO, CH), F32)],
        compiler_params=_cp("arbitrary"))(zs, zs, zs, dact, dact, w)


SWA_TQ = 256


def _t5_bucket_np(dist):
    max_exact = N_BUCKETS // 2
    d = np.maximum(dist, 1).astype(np.float32)
    large = max_exact + (np.log(d / np.float32(max_exact)) / np.float32(math.log(MAX_DISTANCE / max_exact))
                         * np.float32(N_BUCKETS - max_exact)).astype(np.int32)
    large = np.minimum(large, N_BUCKETS - 1)
    return np.where(dist < max_exact, dist, large).astype(np.int32)


def _swa_bucket_matrix(tq):
    dist = WINDOW + np.arange(tq)[:, None] - np.arange(tq + WINDOW)[None, :]
    ok = (dist >= 0) & (dist < WINDOW)
    return np.where(ok, _t5_bucket_np(np.maximum(dist, 0)), -1).astype(np.int32)


def _kv_expand_matrix():
    e = np.zeros((2 * HEAD, 4 * HEAD), np.float32)
    for h in range(4):
        for d in range(HEAD):
            e[(h // 2) * HEAD + d, h * HEAD + d] = 1.0
    return e


def _swa_bias(rel_bias, bucket, name):
    tq, tk = bucket.shape

    def body(rb_ref, bk_ref, o_ref):
        h = pl.program_id(0)
        bk = bk_ref[...]
        acc = jnp.full((tq, tk), NEG_INF, F32)
        for b in range(N_BUCKETS):
            acc = jnp.where(bk == b, rb_ref[b, h], acc)
        o_ref[0] = acc

    return pl.pallas_call(
        body, name=name, grid=(4,),
        in_specs=[pl.BlockSpec(memory_space=pltpu.SMEM), pl.BlockSpec((tq, tk), lambda h: (0, 0))],
        out_specs=pl.BlockSpec((1, tq, tk), lambda h: (h, 0, 0)),
        out_shape=jax.ShapeDtypeStruct((4, tq, tk), F32), compiler_params=_cp("parallel"))(rel_bias, bucket)


def _swa_probs(qh, kx, bm, first_col, sk):
    s = _dot(qh, kx, _NT) * (HEAD ** -0.5)
    col = lax.broadcasted_iota(jnp.int32, s.shape, 1)
    valid = (bm > 0.5 * NEG_INF) & (col >= first_col)
    s = jnp.where(valid, s + bm, NEG_INF)
    m = jnp.maximum(jnp.max(s, axis=-1, keepdims=True), sk)
    p = jnp.exp(s - m)
    den = jnp.sum(p, axis=-1, keepdims=True) + jnp.exp(sk - m)
    return p / den, m, den


def _swa_fwd(zw, gq, gk, sink, bias, expand, name):
    T = zw.shape[0]
    tq = bias.shape[1]
    r = tq // WINDOW

    def body(z_ref, zh_ref, gq_ref, gk_ref, sink_ref, b_ref, e_ref, o_ref, kext, vext):
        i = pl.program_id(0)
        cur = z_ref[...]
        qn, _ = _head_rms(cur[:, :4 * HEAD], gq_ref[...], 4)
        kc, _ = _head_rms(cur[:, 4 * HEAD:6 * HEAD], gk_ref[...], 2)
        hal = zh_ref[...]
        kp, _ = _head_rms(hal[:, :2 * HEAD], gk_ref[...], 2)
        kext[pl.ds(0, WINDOW), :] = kp
        kext[pl.ds(WINDOW, tq), :] = kc
        vext[pl.ds(0, WINDOW), :] = hal[:, 2 * HEAD:]
        vext[pl.ds(WINDOW, tq), :] = cur[:, 6 * HEAD:]
        kx = _dot(kext[...], e_ref[...]).astype(_MXU)
        vx = _dot(vext[...], e_ref[...]).astype(_MXU)
        first_col = jnp.where(i > 0, 0, WINDOW)
        out = jnp.zeros((tq, 4 * HEAD), F32)
        for h in range(4):
            mk = _lane_mask(4 * HEAD, h)
            qh = jnp.where(mk, qn, 0.0)
            pn, _, _ = _swa_probs(qh, kx, b_ref[h], first_col, sink_ref[0, h])
            out = jnp.where(mk, _dot(pn, vx), out)
        o_ref[...] = out

    return pl.pallas_call(
        body, name=name, grid=(T // tq,),
        in_specs=[pl.BlockSpec((tq, 8 * HEAD), lambda i: (i, 0)),
                  pl.BlockSpec((WINDOW, 4 * HEAD), lambda i: (jnp.maximum(i * r - 1, 0), 1)),
                  pl.BlockSpec((1, 4 * HEAD), lambda i: (0, 0)), pl.BlockSpec((1, 2 * HEAD), lambda i: (0, 0)),
                  pl.BlockSpec(memory_space=pltpu.SMEM),
                  pl.BlockSpec(bias.shape, lambda i: (0, 0, 0)),
                  pl.BlockSpec(expand.shape, lambda i: (0, 0))],
        out_specs=pl.BlockSpec((tq, 4 * HEAD), lambda i: (i, 0)),
        out_shape=jax.ShapeDtypeStruct((T, 4 * HEAD), F32),
        scratch_shapes=[pltpu.VMEM((tq + WINDOW, 2 * HEAD), F32), pltpu.VMEM((tq + WINDOW, 2 * HEAD), F32)],
        compiler_params=_cp("parallel"))(zw, zw, gq, gk, sink, bias, expand)


def _swa_bwd(zw, dact, gq, gk, sink, bias, bucket, expand, name):
    T = zw.shape[0]
    tq = bias.shape[1]
    tk = tq + WINDOW
    r = tq // WINDOW
    nt = T // tq
    nb = T // WINDOW
    scale = HEAD ** -0.5

    def body(z_ref, zh_ref, zn_ref, d_ref, dn_ref, gq_ref, gk_ref, sink_ref, b_ref, bk_ref, e_ref,
             dz_ref, dgq_ref, dgk_ref, dsk_ref, drb_ref, kext, vext, dk_s, dv_s, db_s):
        i = pl.program_id(0)

        @pl.when(i == 0)
        def _():
            dgq_ref[...] = jnp.zeros_like(dgq_ref)
            dgk_ref[...] = jnp.zeros_like(dgk_ref)
            dsk_ref[...] = jnp.zeros_like(dsk_ref)
            drb_ref[...] = jnp.zeros_like(drb_ref)
            db_s[...] = jnp.zeros_like(db_s)

        lane = lax.broadcasted_iota(jnp.int32, (1, LANE), 1)
        cur = z_ref[...]
        q_raw, k_raw = cur[:, :4 * HEAD], cur[:, 4 * HEAD:6 * HEAD]
        qn, q_r = _head_rms(q_raw, gq_ref[...], 4)
        kc, k_r = _head_rms(k_raw, gk_ref[...], 2)
        hal = zh_ref[...]
        kp, _ = _head_rms(hal[:, :2 * HEAD], gk_ref[...], 2)
        kext[pl.ds(0, WINDOW), :] = kp
        kext[pl.ds(WINDOW, tq), :] = kc
        vext[pl.ds(0, WINDOW), :] = hal[:, 2 * HEAD:]
        vext[pl.ds(WINDOW, tq), :] = cur[:, 6 * HEAD:]
        ev = e_ref[...]
        kx = _dot(kext[...], ev).astype(_MXU)
        vx = _dot(vext[...], ev).astype(_MXU)
        first_col = jnp.where(i > 0, 0, WINDOW)
        do = d_ref[...]
        dq = jnp.zeros((tq, 4 * HEAD), F32)
        dkx = jnp.zeros((tk, 4 * HEAD), F32)
        dvx = jnp.zeros((tk, 4 * HEAD), F32)
        dsk = jnp.zeros((1, LANE), F32)
        for h in range(4):
            mk = _lane_mask(4 * HEAD, h)
            qh = jnp.where(mk, qn, 0.0).astype(_MXU)
            sk = sink_ref[0, h]
            pn, m, den = _swa_probs(qh, kx, b_ref[h], first_col, sk)
            doh = jnp.where(mk, do, 0.0).astype(_MXU)
            dpn = _dot(doh, vx, _NT)
            delta = jnp.sum(pn * dpn, axis=-1, keepdims=True)
            ds = pn * (dpn - delta)
            psink = jnp.exp(sk - m) / den
            dsk = dsk + jnp.where(lane == h, jnp.sum(-psink * delta, axis=0, keepdims=True), 0.0)
            db_s[h] += ds
            dss = (ds * scale).astype(_MXU)
            dq = dq + jnp.where(mk, _dot(dss, kx), 0.0)
            dkx = dkx + _dot(dss, qh, _TN)
            dvx = dvx + _dot(pn, doh, _TN)
        dsk_ref[...] += dsk
        dk_ext = _exact_dot(dkx, ev, _NT, "a")
        dv_ext = _exact_dot(dvx, ev, _NT, "a")
        dk_s[...] = dk_ext[WINDOW:, :]
        dv_s[...] = dv_ext[WINDOW:, :]

        @pl.when(i < nt - 1)
        def _():
            nxt = zn_ref[...]
            q2, _ = _head_rms(nxt[:, :4 * HEAD], gq_ref[...], 4)
            k2n, _ = _head_rms(nxt[:, 4 * HEAD:6 * HEAD], gk_ref[...], 2)
            k2 = jnp.concatenate([kc[tq - WINDOW:, :], k2n], axis=0)
            v2 = jnp.concatenate([cur[tq - WINDOW:, 6 * HEAD:], nxt[:, 6 * HEAD:]], axis=0)
            k2x = _dot(k2, ev).astype(_MXU)
            v2x = _dot(v2, ev).astype(_MXU)
            do2 = dn_ref[...]
            dk2x = jnp.zeros((2 * WINDOW, 4 * HEAD), F32)
            dv2x = jnp.zeros((2 * WINDOW, 4 * HEAD), F32)
            for h in range(4):
                mk = _lane_mask(4 * HEAD, h)
                qh = jnp.where(mk, q2, 0.0).astype(_MXU)
                pn, _, _ = _swa_probs(qh, k2x, b_ref[h][:WINDOW, :2 * WINDOW], 0, sink_ref[0, h])
                doh = jnp.where(mk, do2, 0.0).astype(_MXU)
                dpn = _dot(doh, v2x, _NT)
                ds = pn * (dpn - jnp.sum(pn * dpn, axis=-1, keepdims=True))
                dk2x = dk2x + _dot((ds * scale).astype(_MXU), qh, _TN)
                dv2x = dv2x + _dot(pn, doh, _TN)
            dk_s[pl.ds(tq - WINDOW, WINDOW), :] += _exact_dot(dk2x, ev, _NT, "a")[:WINDOW, :]
            dv_s[pl.ds(tq - WINDOW, WINDOW), :] += _exact_dot(dv2x, ev, _NT, "a")[:WINDOW, :]

        dq_raw, dgq = _head_rms_bwd(dq, q_raw, q_r, gq_ref[...], 4)
        dk_raw, dgk = _head_rms_bwd(dk_s[...], k_raw, k_r, gk_ref[...], 2)
        dgq_ref[...] += dgq
        dgk_ref[...] += dgk
        dz_ref[:, :4 * HEAD] = dq_raw.astype(dz_ref.dtype)
        dz_ref[:, 4 * HEAD:6 * HEAD] = dk_raw.astype(dz_ref.dtype)
        dz_ref[:, 6 * HEAD:] = dv_s[...].astype(dz_ref.dtype)

        @pl.when(i == nt - 1)
        def _():
            bk = bk_ref[...]
            for b in range(N_BUCKETS):
                rowv = jnp.zeros((1, LANE), F32)
                for h in range(4):
                    s1 = jnp.sum(jnp.where(bk == b, db_s[h], 0.0), axis=0, keepdims=True)
                    rowv = jnp.where(lane == h, jnp.sum(s1, axis=1, keepdims=True), rowv)
                drb_ref[pl.ds(b, 1), :] = rowv

    const2 = lambda i: (0, 0)
    return pl.pallas_call(
        body, name=name, grid=(nt,),
        in_specs=[pl.BlockSpec((tq, 8 * HEAD), lambda i: (i, 0)),
                  pl.BlockSpec((WINDOW, 4 * HEAD), lambda i: (jnp.maximum(i * r - 1, 0), 1)),
                  pl.BlockSpec((WINDOW, 8 * HEAD), lambda i: (jnp.minimum((i + 1) * r, nb - 1), 0)),
                  pl.BlockSpec((tq, 4 * HEAD), lambda i: (i, 0)),
                  pl.BlockSpec((WINDOW, 4 * HEAD), lambda i: (jnp.minimum((i + 1) * r, nb - 1), 0)),
                  pl.BlockSpec((1, 4 * HEAD), const2), pl.BlockSpec((1, 2 * HEAD), const2),
                  pl.BlockSpec(memory_space=pltpu.SMEM),
                  pl.BlockSpec(bias.shape, lambda i: (0, 0, 0)),
                  pl.BlockSpec(bucket.shape, const2), pl.BlockSpec(expand.shape, const2)],
        out_specs=[pl.BlockSpec((tq, 8 * HEAD), lambda i: (i, 0)),
                   pl.BlockSpec((1, 4 * HEAD), const2), pl.BlockSpec((1, 2 * HEAD), const2),
                   pl.BlockSpec((1, LANE), const2), pl.BlockSpec((N_BUCKETS, LANE), const2)],
        out_shape=[jax.ShapeDtypeStruct((T, 8 * HEAD), _MXU), jax.ShapeDtypeStruct((1, 4 * HEAD), F32),
                   jax.ShapeDtypeStruct((1, 2 * HEAD), F32), jax.ShapeDtypeStruct((1, LANE), F32),
                   jax.ShapeDtypeStruct((N_BUCKETS, LANE), F32)],
        scratch_shapes=[pltpu.VMEM((tk, 2 * HEAD), F32), pltpu.VMEM((tk, 2 * HEAD), F32),
                        pltpu.VMEM((tq, 2 * HEAD), F32), pltpu.VMEM((tq, 2 * HEAD), F32),
                        pltpu.VMEM((4, tq, tk), F32)],
        compiler_params=_cp("arbitrary"))(zw, zw, zw, dact, dact, gq, gk, sink, bias, bucket, expand)


FOX_B = 512
FOX_TM = 256


def _tri(n, lower):
    m = np.tril(np.ones((n, n), np.float32)) if lower else np.triu(np.ones((n, n), np.float32))
    return m


def _log_sigmoid(x):
    return jnp.minimum(x, 0.0) - jnp.log1p(jnp.exp(-jnp.abs(x)))


def _fox_prep(zf, gq, gk, bf, name):
    T = zf.shape[0]
    tm = _tile(T, FOX_TM)
    lower = jnp.asarray(_tri(tm, True), _MXU)

    def body(z_ref, gq_ref, gk_ref, bf_ref, l_ref, q_ref, k_ref, v_ref, f_ref, ft_ref, carry):
        @pl.when(pl.program_id(0) == 0)
        def _():
            carry[...] = jnp.zeros_like(carry)

        z = z_ref[...]
        q, _ = _head_rms(z[:, :CH], gq_ref[...], 4)
        k, _ = _head_rms(z[:, CH:2 * CH], gk_ref[...], 4)
        q_ref[...] = q.astype(q_ref.dtype)
        k_ref[...] = k.astype(k_ref.dtype)
        v_ref[...] = z[:, 2 * CH:3 * CH].astype(v_ref.dtype)
        lane = lax.broadcasted_iota(jnp.int32, (1, LANE), 1)
        lf = jnp.where(lane < 4, _log_sigmoid(z[:, 3 * CH:] + bf_ref[...]), 0.0)
        fv = _exact_dot(l_ref[...], lf, _NN, "b") + carry[pl.ds(0, 1), :]
        f_ref[...] = fv
        ft_ref[...] = fv.T
        carry[pl.ds(0, 1), :] = f_ref[pl.ds(tm - 1, 1), :]

    row = pl.BlockSpec((tm, CH), lambda i: (i, 0))
    vec = pl.BlockSpec((1, CH), lambda i: (0, 0))
    qsh = jax.ShapeDtypeStruct((T, CH), _MXU)
    return pl.pallas_call(
        body, name=name, grid=(T // tm,),
        in_specs=[pl.BlockSpec((tm, 3 * CH + LANE), lambda i: (i, 0)), vec, vec,
                  pl.BlockSpec((1, LANE), lambda i: (0, 0)), pl.BlockSpec((tm, tm), lambda i: (0, 0))],
        out_specs=[row, row, row, pl.BlockSpec((tm, LANE), lambda i: (i, 0)), pl.BlockSpec((LANE, tm), lambda i: (0, i))],
        out_shape=[qsh, qsh, qsh, jax.ShapeDtypeStruct((T, LANE), F32), jax.ShapeDtypeStruct((LANE, T), F32)],
        scratch_shapes=[pltpu.VMEM((8, LANE), F32)],
        compiler_params=_cp("arbitrary"))(zf, gq, gk, bf, lower)


def _lane_col(x, h):
    lane = lax.broadcasted_iota(jnp.int32, (1, x.shape[-1]), 1)
    return jnp.sum(jnp.where(lane == h, x, 0.0), axis=-1, keepdims=True)


def _fox_scores(qh, k, fq, ft_ref, h, qi, ki, B):
    s = _dot(qh, k, _NT) * (HEAD ** -0.5)
    s = s + (fq - ft_ref[pl.ds(h, 1), :])
    row = qi * B + lax.broadcasted_iota(jnp.int32, s.shape, 0)
    col = ki * B + lax.broadcasted_iota(jnp.int32, s.shape, 1)
    return jnp.where(col <= row, s, NEG_INF)


def _fox_fwd(q, k, v, f, ft, name):
    T = q.shape[0]
    B = _tile(T, FOX_B)
    n = T // B

    def body(q_ref, k_ref, v_ref, f_ref, ft_ref, o_ref, lse_ref, m_s, l_s, acc):
        qi, ki = pl.program_id(0), pl.program_id(1)

        @pl.when(ki == 0)
        def _():
            m_s[...] = jnp.full_like(m_s, NEG_INF)
            l_s[...] = jnp.zeros_like(l_s)
            acc[...] = jnp.zeros_like(acc)

        @pl.when(ki <= qi)
        def _():
            qv, kv, vv, fv = q_ref[...], k_ref[...], v_ref[...], f_ref[...]
            for h in range(4):
                mk = _lane_mask(CH, h)
                qh = jnp.where(mk, qv, jnp.zeros_like(qv))
                s = _fox_scores(qh, kv, _lane_col(fv, h), ft_ref, h, qi, ki, B)
                m_old = m_s[h]
                m_new = jnp.maximum(m_old, jnp.max(s, axis=-1, keepdims=True))
                alpha = jnp.exp(m_old - m_new)
                p = jnp.exp(s - m_new)
                l_s[h] = alpha * l_s[h] + jnp.sum(p, axis=-1, keepdims=True)
                m_s[h] = m_new
                acc[...] = jnp.where(mk, acc[...] * alpha + _dot(p, vv), acc[...])

        @pl.when(ki == qi)
        def _():
            lane = lax.broadcasted_iota(jnp.int32, (1, LANE), 1)
            out = acc[...]
            lse = jnp.zeros((B, LANE), F32)
            for h in range(4):
                out = jnp.where(_lane_mask(CH, h), out / l_s[h], out)
                lse = jnp.where(lane == h, m_s[h] + jnp.log(l_s[h]), lse)
            o_ref[...] = out
            lse_ref[...] = lse

    qspec = pl.BlockSpec((B, CH), lambda qi, ki: (qi, 0))
    kspec = pl.BlockSpec((B, CH), lambda qi, ki: (jnp.minimum(ki, qi), 0))
    return pl.pallas_call(
        body, name=name, grid=(n, n),
        in_specs=[qspec, kspec, kspec, pl.BlockSpec((B, LANE), lambda qi, ki: (qi, 0)),
                  pl.BlockSpec((8, B), lambda qi, ki: (0, jnp.minimum(ki, qi)))],
        out_specs=[qspec, pl.BlockSpec((B, LANE), lambda qi, ki: (qi, 0))],
        out_shape=[jax.ShapeDtypeStruct((T, CH), F32), jax.ShapeDtypeStruct((T, LANE), F32)],
        scratch_shapes=[pltpu.VMEM((4, B, 1), F32), pltpu.VMEM((4, B, 1), F32), pltpu.VMEM((B, CH), F32)],
        compiler_params=_cp("parallel", "arbitrary"))(q, k, v, f, ft)


def _fox_delta(o, do, name):
    T = o.shape[0]
    tm = _tile(T, ROW_TILE)

    def body(o_ref, d_ref, out_ref):
        prod = o_ref[...] * d_ref[...]
        lane = lax.broadcasted_iota(jnp.int32, (1, LANE), 1)
        out = jnp.zeros((tm, LANE), F32)
        for h in range(4):
            s = jnp.sum(jnp.where(_lane_mask(CH, h), prod, 0.0), axis=-1, keepdims=True)
            out = jnp.where(lane == h, s, out)
        out_ref[...] = out

    row = pl.BlockSpec((tm, CH), lambda i: (i, 0))
    return pl.pallas_call(
        body, name=name, grid=(T // tm,), in_specs=[row, row],
        out_specs=pl.BlockSpec((tm, LANE), lambda i: (i, 0)),
        out_shape=jax.ShapeDtypeStruct((T, LANE), F32), compiler_params=_cp("parallel"))(o, do)


def _fox_bwd_dq(q, k, v, f, ft, lse, delta, do, name):
    T = q.shape[0]
    B = _tile(T, FOX_B)
    n = T // B

    def body(q_ref, k_ref, v_ref, f_ref, ft_ref, lse_ref, dl_ref, do_ref, dq_ref, dfq_ref, dq_s, df_s):
        qi, ki = pl.program_id(0), pl.program_id(1)

        @pl.when(ki == 0)
        def _():
            dq_s[...] = jnp.zeros_like(dq_s)
            df_s[...] = jnp.zeros_like(df_s)

        @pl.when(ki <= qi)
        def _():
            qv, kv, vv, fv = q_ref[...], k_ref[...], v_ref[...], f_ref[...]
            lsev, dlv, dov = lse_ref[...], dl_ref[...], do_ref[...]
            lane = lax.broadcasted_iota(jnp.int32, (1, LANE), 1)
            for h in range(4):
                mk = _lane_mask(CH, h)
                qh = jnp.where(mk, qv, jnp.zeros_like(qv))
                s = _fox_scores(qh, kv, _lane_col(fv, h), ft_ref, h, qi, ki, B)
                p = jnp.exp(s - _lane_col(lsev, h))
                doh = jnp.where(mk, dov, 0.0)
                ds = p * (_dot(doh, vv, _NT) - _lane_col(dlv, h))
                dq_s[...] += jnp.where(mk, _dot(ds * (HEAD ** -0.5), kv), 0.0)
                df_s[...] += jnp.where(lane == h, jnp.sum(ds, axis=-1, keepdims=True), 0.0)

        @pl.when(ki == qi)
        def _():
            dq_ref[...] = dq_s[...]
            dfq_ref[...] = df_s[...]

    qspec = pl.BlockSpec((B, CH), lambda qi, ki: (qi, 0))
    kspec = pl.BlockSpec((B, CH), lambda qi, ki: (jnp.minimum(ki, qi), 0))
    lspec = pl.BlockSpec((B, LANE), lambda qi, ki: (qi, 0))
    return pl.pallas_call(
        body, name=name, grid=(n, n),
        in_specs=[qspec, kspec, kspec, lspec, pl.BlockSpec((8, B), lambda qi, ki: (0, jnp.minimum(ki, qi))),
                  lspec, lspec, qspec],
        out_specs=[qspec, lspec],
        out_shape=[jax.ShapeDtypeStruct((T, CH), F32), jax.ShapeDtypeStruct((T, LANE), F32)],
        scratch_shapes=[pltpu.VMEM((B, CH), F32), pltpu.VMEM((B, LANE), F32)],
        compiler_params=_cp("parallel", "arbitrary"))(q, k, v, f, ft, lse, delta, do)


def _fox_bwd_dkv(q, k, v, f, ft, lse, delta, do, name):
    T = q.shape[0]
    B = _tile(T, FOX_B)
    n = T // B

    def body(q_ref, k_ref, v_ref, f_ref, ft_ref, lse_ref, dl_ref, do_ref, dk_ref, dv_ref, dft_ref, dk_s, dv_s, df_s):
        ki, qi = pl.program_id(0), pl.program_id(1)

        @pl.when(qi == 0)
        def _():
            dk_s[...] = jnp.zeros_like(dk_s)
            dv_s[...] = jnp.zeros_like(dv_s)
            df_s[...] = jnp.zeros_like(df_s)

        @pl.when(qi >= ki)
        def _():
            qv, kv, vv, fv = q_ref[...], k_ref[...], v_ref[...], f_ref[...]
            lsev, dlv, dov = lse_ref[...], dl_ref[...], do_ref[...]
            for h in range(4):
                mk = _lane_mask(CH, h)
                qh = jnp.where(mk, qv, jnp.zeros_like(qv))
                s = _fox_scores(qh, kv, _lane_col(fv, h), ft_ref, h, qi, ki, B)
                p = jnp.exp(s - _lane_col(lsev, h))
                doh = jnp.where(mk, dov, 0.0)
                ds = p * (_dot(doh, vv, _NT) - _lane_col(dlv, h))
                dv_s[...] += _dot(p, doh, _TN)
                dk_s[...] += _dot(ds * (HEAD ** -0.5), qh, _TN)
                df_s[pl.ds(h, 1), :] -= jnp.sum(ds, axis=0, keepdims=True)

        @pl.when(qi == n - 1)
        def _():
            dk_ref[...] = dk_s[...]
            dv_ref[...] = dv_s[...]
            dft_ref[...] = jnp.zeros_like(dft_ref)
            dft_ref[pl.ds(0, 8), :] = df_s[...]

    qspec = pl.BlockSpec((B, CH), lambda ki, qi: (jnp.maximum(qi, ki), 0))
    kspec = pl.BlockSpec((B, CH), lambda ki, qi: (ki, 0))
    lspec = pl.BlockSpec((B, LANE), lambda ki, qi: (jnp.maximum(qi, ki), 0))
    return pl.pallas_call(
        body, name=name, grid=(n, n),
        in_specs=[qspec, kspec, kspec, lspec, pl.BlockSpec((8, B), lambda ki, qi: (0, ki)), lspec, lspec, qspec],
        out_specs=[kspec, kspec, pl.BlockSpec((LANE, B), lambda ki, qi: (0, ki))],
        out_shape=[jax.ShapeDtypeStruct((T, CH), F32), jax.ShapeDtypeStruct((T, CH), F32),
                   jax.ShapeDtypeStruct((LANE, T), F32)],
        scratch_shapes=[pltpu.VMEM((B, CH), F32), pltpu.VMEM((B, CH), F32), pltpu.VMEM((8, B), F32)],
        compiler_params=_cp("parallel", "arbitrary"))(q, k, v, f, ft, lse, delta, do)


def _fox_post(zf, dqn, dkn, dv, dfq, dft, gq, gk, bf, name):
    T = zf.shape[0]
    tm = _tile(T, FOX_TM)
    nt = T // tm
    upper = jnp.asarray(_tri(tm, False), _MXU)

    def body(z_ref, dq_ref, dk_ref, dv_ref, dfq_ref, dft_ref, gq_ref, gk_ref, bf_ref, u_ref, dz_ref, sm_ref, carry, rc_s):
        @pl.when(pl.program_id(0) == 0)
        def _():
            carry[...] = jnp.zeros_like(carry)
            sm_ref[...] = jnp.zeros_like(sm_ref)

        z = z_ref[...]
        q_raw, k_raw = z[:, :CH], z[:, CH:2 * CH]
        _, q_r = _head_rms(q_raw, gq_ref[...], 4)
        _, k_r = _head_rms(k_raw, gk_ref[...], 4)
        dq, dgq = _head_rms_bwd(dq_ref[...], q_raw, q_r, gq_ref[...], 4)
        dk, dgk = _head_rms_bwd(dk_ref[...], k_raw, k_r, gk_ref[...], 4)
        df = dfq_ref[...] + dft_ref[...].T
        rc_s[...] = _exact_dot(u_ref[...], df, _NN, "b") + carry[pl.ds(0, 1), :]
        carry[pl.ds(0, 1), :] = rc_s[pl.ds(0, 1), :]
        lane = lax.broadcasted_iota(jnp.int32, (1, LANE), 1)
        x = z[:, 3 * CH:] + bf_ref[...]
        dff = jnp.where(lane < 4, rc_s[...] * _sigmoid(-x), 0.0)
        dz_ref[:, :CH] = dq.astype(dz_ref.dtype)
        dz_ref[:, CH:2 * CH] = dk.astype(dz_ref.dtype)
        dz_ref[:, 2 * CH:3 * CH] = dv_ref[...].astype(dz_ref.dtype)
        dz_ref[:, 3 * CH:] = dff.astype(dz_ref.dtype)
        sm_ref[pl.ds(0, 1), :] += dgq
        sm_ref[pl.ds(1, 1), :] += dgk
        sm_ref[pl.ds(2, 1), :LANE] += jnp.sum(dff, axis=0, keepdims=True)

    rev = lambda i: (nt - 1 - i, 0)
    row = pl.BlockSpec((tm, CH), rev)
    lrow = pl.BlockSpec((tm, LANE), rev)
    vec = pl.BlockSpec((1, CH), lambda i: (0, 0))
    return pl.pallas_call(
        body, name=name, grid=(nt,),
        in_specs=[pl.BlockSpec((tm, 3 * CH + LANE), rev), row, row, row, lrow,
                  pl.BlockSpec((LANE, tm), lambda i: (0, nt - 1 - i)), vec, vec,
                  pl.BlockSpec((1, LANE), lambda i: (0, 0)), pl.BlockSpec((tm, tm), lambda i: (0, 0))],
        out_specs=[pl.BlockSpec((tm, 3 * CH + LANE), rev), pl.BlockSpec((8, CH), lambda i: (0, 0))],
        out_shape=[jax.ShapeDtypeStruct((T, 3 * CH + LANE), _MXU), jax.ShapeDtypeStruct((8, CH), F32)],
        scratch_shapes=[pltpu.VMEM((8, LANE), F32), pltpu.VMEM((tm, LANE), F32)],
        compiler_params=_cp("arbitrary"))(zf, dqn, dkn, dv, dfq, dft, gq, gk, bf, upper)


AUG_F, AUG_ONE, AUG_LSE = HEAD, HEAD + 3, HEAD + 6


def _pieces(x):
    hi = x.astype(_MXU).astype(F32)
    r1 = x - hi
    mid = r1.astype(_MXU).astype(F32)
    lo = (r1 - mid).astype(_MXU).astype(F32)
    return hi, mid, lo


def _put_pieces(base, first_lane, x, sign):
    lane = lax.broadcasted_iota(jnp.int32, (1, LANE), 1)
    for j, piece in enumerate(_pieces(x)):
        base = jnp.where(lane == first_lane + j, sign * piece, base)
    return base


def _head_select_matrix():
    p = np.zeros((4, 4 * HEAD, LANE), np.float32)
    for h in range(4):
        for d in range(HEAD):
            p[h, h * HEAD + d, d] = 1.0
    return p


def _tri_steps(n, by_key):
    if by_key:
        pairs = [(q, k) for k in range(n) for q in range(k, n)]
    else:
        pairs = [(q, k) for q in range(n) for k in range(q + 1)]
    return (jnp.asarray([p[0] for p in pairs], jnp.int32), jnp.asarray([p[1] for p in pairs], jnp.int32))


def _fox2_prep(zf, gq, gk, bf, sel, name):
    T = zf.shape[0]
    tm = _tile(T, FOX_TM)
    lower = jnp.asarray(_tri(tm, True), _MXU)

    def body(z_ref, gq_ref, gk_ref, bf_ref, l_ref, p_ref, qa_ref, ka_ref, va_ref, vat_ref, carry, f_s):
        @pl.when(pl.program_id(0) == 0)
        def _():
            carry[...] = jnp.zeros_like(carry)

        z = z_ref[...]
        q, _ = _head_rms(z[:, :CH], gq_ref[...], 4)
        k, _ = _head_rms(z[:, CH:2 * CH], gk_ref[...], 4)
        q = (q * (HEAD ** -0.5)).astype(_MXU)
        k = k.astype(_MXU)
        v = z[:, 2 * CH:3 * CH].astype(_MXU)
        lane = lax.broadcasted_iota(jnp.int32, (1, LANE), 1)
        lf = jnp.where(lane < 4, _log_sigmoid(z[:, 3 * CH:] + bf_ref[...]), 0.0)
        f_s[...] = _exact_dot(l_ref[...], lf, _NN, "b") + carry[pl.ds(0, 1), :]
        carry[pl.ds(0, 1), :] = f_s[pl.ds(tm - 1, 1), :]
        fv = f_s[...]
        q_ones = (lane >= AUG_ONE) & (lane < AUG_ONE + 3)
        k_ones = ((lane >= AUG_F) & (lane < AUG_F + 3)) | ((lane >= AUG_LSE) & (lane < AUG_LSE + 3))
        v_ones = (lane >= AUG_F) & (lane < AUG_F + 3)
        for h in range(4):
            fh = _lane_col(fv, h)
            qa = jnp.where(q_ones, 1.0, _dot(q, p_ref[h]))
            qa_ref[h] = _put_pieces(qa, AUG_F, fh, 1.0).astype(qa_ref.dtype)
            ka = jnp.where(k_ones, 1.0, _dot(k, p_ref[h]))
            ka_ref[h] = _put_pieces(ka, AUG_ONE, fh, -1.0).astype(ka_ref.dtype)
            va = jnp.where(v_ones, 1.0, _dot(v, p_ref[h]))
            va_ref[h] = va.astype(va_ref.dtype)
            vat_ref[h] = va.T.astype(vat_ref.dtype)

    vec = pl.BlockSpec((1, CH), lambda i: (0, 0))
    hspec = pl.BlockSpec((4, tm, LANE), lambda i: (0, i, 0))
    hsh = jax.ShapeDtypeStruct((4, T, LANE), _MXU)
    return pl.pallas_call(
        body, name=name, grid=(T // tm,),
        in_specs=[pl.BlockSpec((tm, 3 * CH + LANE), lambda i: (i, 0)), vec, vec,
                  pl.BlockSpec((1, LANE), lambda i: (0, 0)), pl.BlockSpec((tm, tm), lambda i: (0, 0)),
                  pl.BlockSpec(sel.shape, lambda i: (0, 0, 0))],
        out_specs=[hspec, hspec, hspec, pl.BlockSpec((4, LANE, tm), lambda i: (0, 0, i))],
        out_shape=[hsh, hsh, hsh, jax.ShapeDtypeStruct((4, LANE, T), _MXU)],
        scratch_shapes=[pltpu.VMEM((8, LANE), F32), pltpu.VMEM((tm, LANE), F32)],
        compiler_params=_cp("arbitrary"))(zf, gq, gk, bf, lower, sel)


def _causal(s, transposed):
    row = lax.broadcasted_iota(jnp.int32, s.shape, 0)
    col = lax.broadcasted_iota(jnp.int32, s.shape, 1)
    return jnp.where((row <= col) if transposed else (col <= row), s, NEG_INF)


def _mxu_dot(a, b, dims):
    return lax.dot_general(a, b, dims, preferred_element_type=F32)


def _fox2_fwd(qa, ka, vat, sel, name):
    T = qa.shape[1]
    B = _tile(T, FOX_B)
    n = T // B
    qt, kt = _tri_steps(n, False)

    def body(qt_ref, kt_ref, qa_ref, ka_ref, vat_ref, p_ref, o_ref, qb_ref, m_s, acc):
        step = pl.program_id(0)
        qi, ki = qt_ref[step], kt_ref[step]

        @pl.when(ki == 0)
        def _():
            m_s[...] = jnp.full_like(m_s, NEG_INF)
            acc[...] = jnp.zeros_like(acc)

        def update(diag):
            for h in range(4):
                st = _mxu_dot(ka_ref[h], qa_ref[h], _NT)
                if diag:
                    st = _causal(st, True)
                m_old = m_s[h, pl.ds(0, 1), :]
                m_new = jnp.maximum(m_old, jnp.max(st, axis=0, keepdims=True))
                pt = jnp.exp(st - m_new)
                acc[h] = acc[h] * jnp.exp(m_old - m_new) + _dot(vat_ref[h], pt)
                m_s[h, pl.ds(0, 1), :] = m_new

        @pl.when(ki < qi)
        def _():
            update(False)

        @pl.when(ki == qi)
        def _():
            update(True)
            row = lax.broadcasted_iota(jnp.int32, (LANE, 1), 0)
            out = jnp.zeros((B, CH), F32)
            for h in range(4):
                a = acc[h]
                l = a[AUG_F:AUG_F + 1, :]
                out = out + _exact_dot((a / l).T, p_ref[h], _NT, "a")
                lse = m_s[h, pl.ds(0, 1), :] + jnp.log(l)
                qbt = qa_ref[h].astype(F32).T
                for j, piece in enumerate(_pieces(lse)):
                    qbt = jnp.where(row == AUG_LSE + j, -piece, qbt)
                qb_ref[h] = qbt.T.astype(qb_ref.dtype)
            o_ref[...] = out

    qspec = pl.BlockSpec((4, B, LANE), lambda s, qt, kt: (0, qt[s], 0))
    kspec = pl.BlockSpec((4, B, LANE), lambda s, qt, kt: (0, kt[s], 0))
    grid_spec = pltpu.PrefetchScalarGridSpec(
        num_scalar_prefetch=2, grid=(qt.shape[0],),
        in_specs=[qspec, kspec, pl.BlockSpec((4, LANE, B), lambda s, qt, kt: (0, 0, kt[s])),
                  pl.BlockSpec(sel.shape, lambda s, qt, kt: (0, 0, 0))],
        out_specs=[pl.BlockSpec((B, CH), lambda s, qt, kt: (qt[s], 0)), qspec],
        scratch_shapes=[pltpu.VMEM((4, 8, B), F32), pltpu.VMEM((4, LANE, B), F32)])
    return pl.pallas_call(
        body, name=name, grid_spec=grid_spec,
        out_shape=[jax.ShapeDtypeStruct((T, CH), F32), jax.ShapeDtypeStruct((4, T, LANE), _MXU)],
        compiler_params=_cp("arbitrary"))(qt, kt, qa, ka, vat, sel)


def _fox2_bwd_prep(o, do, sel, name):
    T = o.shape[0]
    tm = _tile(T, ROW_TILE)

    def body(o_ref, d_ref, p_ref, out_ref):
        dov = d_ref[...]
        prod = o_ref[...] * dov
        dob = dov.astype(_MXU)
        for h in range(4):
            delta = jnp.sum(jnp.where(_lane_mask(CH, h), prod, 0.0), axis=-1, keepdims=True)
            out_ref[h] = _put_pieces(_dot(dob, p_ref[h]), AUG_F, delta, -1.0).astype(out_ref.dtype)

    row = pl.BlockSpec((tm, CH), lambda i: (i, 0))
    return pl.pallas_call(
        body, name=name, grid=(T // tm,),
        in_specs=[row, row, pl.BlockSpec(sel.shape, lambda i: (0, 0, 0))],
        out_specs=pl.BlockSpec((4, tm, LANE), lambda i: (0, i, 0)),
        out_shape=jax.ShapeDtypeStruct((4, T, LANE), _MXU), compiler_params=_cp("parallel"))(o, do, sel)


def _fox2_bwd_dq(qb, ka, va, doa, sel, name):
    T = qb.shape[1]
    B = _tile(T, FOX_B)
    n = T // B
    qt, kt = _tri_steps(n, False)

    def body(qt_ref, kt_ref, qb_ref, ka_ref, va_ref, do_ref, p_ref, dq_ref, dfq_ref, dq_s):
        step = pl.program_id(0)
        qi, ki = qt_ref[step], kt_ref[step]

        @pl.when(ki == 0)
        def _():
            dq_s[...] = jnp.zeros_like(dq_s)

        def update(diag):
            for h in range(4):
                s = _mxu_dot(qb_ref[h], ka_ref[h], _NT)
                if diag:
                    s = _causal(s, False)
                ds = jnp.exp(s) * _mxu_dot(do_ref[h], va_ref[h], _NT)
                dq_s[h] += _dot(ds, ka_ref[h])

        @pl.when(ki < qi)
        def _():
            update(False)

        @pl.when(ki == qi)
        def _():
            update(True)
            lane = lax.broadcasted_iota(jnp.int32, (1, LANE), 1)
            out = jnp.zeros((B, CH), F32)
            dfq = jnp.zeros((B, LANE), F32)
            for h in range(4):
                out = out + _exact_dot(dq_s[h] * (HEAD ** -0.5), p_ref[h], _NT, "a")
                dfq = jnp.where(lane == h, _lane_col(dq_s[h], AUG_F), dfq)
            dq_ref[...] = out
            dfq_ref[...] = dfq

    qspec = pl.BlockSpec((4, B, LANE), lambda s, qt, kt: (0, qt[s], 0))
    kspec = pl.BlockSpec((4, B, LANE), lambda s, qt, kt: (0, kt[s], 0))
    grid_spec = pltpu.PrefetchScalarGridSpec(
        num_scalar_prefetch=2, grid=(qt.shape[0],),
        in_specs=[qspec, kspec, kspec, qspec, pl.BlockSpec(sel.shape, lambda s, qt, kt: (0, 0, 0))],
        out_specs=[pl.BlockSpec((B, CH), lambda s, qt, kt: (qt[s], 0)),
                   pl.BlockSpec((B, LANE), lambda s, qt, kt: (qt[s], 0))],
        scratch_shapes=[pltpu.VMEM((4, B, LANE), F32)])
    return pl.pallas_call(
        body, name=name, grid_spec=grid_spec,
        out_shape=[jax.ShapeDtypeStruct((T, CH), F32), jax.ShapeDtypeStruct((T, LANE), F32)],
        compiler_params=_cp("arbitrary"))(qt, kt, qb, ka, va, doa, sel)


def _fox2_bwd_dkv(qb, ka, va, doa, sel, name):
    T = qb.shape[1]
    B = _tile(T, FOX_B)
    n = T // B
    qt, kt = _tri_steps(n, True)

    def body(qt_ref, kt_ref, qb_ref, ka_ref, va_ref, do_ref, p_ref, dk_ref, dv_ref, df_ref, dk_s, dv_s):
        step = pl.program_id(0)
        qi, ki = qt_ref[step], kt_ref[step]

        @pl.when(qi == ki)
        def _():
            dk_s[...] = jnp.zeros_like(dk_s)
            dv_s[...] = jnp.zeros_like(dv_s)

        def update(diag):
            for h in range(4):
                st = _mxu_dot(ka_ref[h], qb_ref[h], _NT)
                if diag:
                    st = _causal(st, True)
                pt = jnp.exp(st)
                dst = pt * _mxu_dot(va_ref[h], do_ref[h], _NT)
                dv_s[h] += _dot(pt, do_ref[h])
                dk_s[h] += _dot(dst, qb_ref[h])

        @pl.when(qi == ki)
        def _():
            update(True)

        @pl.when(qi > ki)
        def _():
            update(False)

        @pl.when(qi == n - 1)
        def _():
            lane = lax.broadcasted_iota(jnp.int32, (1, LANE), 1)
            dk = jnp.zeros((B, CH), F32)
            dv = jnp.zeros((B, CH), F32)
            dfk = jnp.zeros((B, LANE), F32)
            for h in range(4):
                dk = dk + _exact_dot(dk_s[h], p_ref[h], _NT, "a")
                dv = dv + _exact_dot(dv_s[h], p_ref[h], _NT, "a")
                dfk = jnp.where(lane == h, -_lane_col(dk_s[h], AUG_ONE), dfk)
            dk_ref[...] = dk
            dv_ref[...] = dv
            df_ref[...] = dfk

    qspec = pl.BlockSpec((4, B, LANE), lambda s, qt, kt: (0, qt[s], 0))
    kspec = pl.BlockSpec((4, B, LANE), lambda s, qt, kt: (0, kt[s], 0))
    ospec = pl.BlockSpec((B, CH), lambda s, qt, kt: (kt[s], 0))
    grid_spec = pltpu.PrefetchScalarGridSpec(
        num_scalar_prefetch=2, grid=(qt.shape[0],),
        in_specs=[qspec, kspec, kspec, qspec, pl.BlockSpec(sel.shape, lambda s, qt, kt: (0, 0, 0))],
        out_specs=[ospec, ospec, pl.BlockSpec((B, LANE), lambda s, qt, kt: (kt[s], 0))],
        scratch_shapes=[pltpu.VMEM((4, B, LANE), F32), pltpu.VMEM((4, B, LANE), F32)])
    return pl.pallas_call(
        body, name=name, grid_spec=grid_spec,
        out_shape=[jax.ShapeDtypeStruct((T, CH), F32), jax.ShapeDtypeStruct((T, CH), F32),
                   jax.ShapeDtypeStruct((T, LANE), F32)],
        compiler_params=_cp("arbitrary"))(qt, kt, qb, ka, va, doa, sel)


def _fox2_bwd(qb, ka, va, doa, sel, name):
    T = qb.shape[1]
    B = _tile(T, FOX_B)
    n = T // B
    qt, kt = _tri_steps(n, True)
    nsteps = qt.shape[0]

    def body(qt_ref, kt_ref, qb_ref, ka_ref, va_ref, do_ref, p_ref, dk_ref, dv_ref, df_ref, dq_hbm,
             dk_s, dv_s, kat_s, dq_s, sem):
        step = pl.program_id(0)
        qi, ki = qt_ref[step], kt_ref[step]

        @pl.when(step == 0)
        def _():
            dq_s[...] = jnp.zeros_like(dq_s)

        @pl.when(qi == ki)
        def _():
            dk_s[...] = jnp.zeros_like(dk_s)
            dv_s[...] = jnp.zeros_like(dv_s)
            for h in range(4):
                kat_s[h] = ka_ref[h].astype(F32).T.astype(kat_s.dtype)

        def update(diag):
            for h in range(4):
                st = _mxu_dot(ka_ref[h], qb_ref[h], _NT)
                if diag:
                    st = _causal(st, True)
                pt = jnp.exp(st)
                dst = (pt * _mxu_dot(va_ref[h], do_ref[h], _NT)).astype(_MXU)
                dv_s[h] += _dot(pt, do_ref[h])
                dk_s[h] += _mxu_dot(dst, qb_ref[h], _NN)
                dq_s[qi, h] += _mxu_dot(kat_s[h], dst, _NN)

        @pl.when(qi == ki)
        def _():
            update(True)

        @pl.when(qi > ki)
        def _():
            update(False)

        @pl.when(qi == n - 1)
        def _():
            lane = lax.broadcasted_iota(jnp.int32, (1, LANE), 1)
            dk = jnp.zeros((B, CH), F32)
            dv = jnp.zeros((B, CH), F32)
            dfk = jnp.zeros((B, LANE), F32)
            for h in range(4):
                dk = dk + _exact_dot(dk_s[h], p_ref[h], _NT, "a")
                dv = dv + _exact_dot(dv_s[h], p_ref[h], _NT, "a")
                dfk = jnp.where(lane == h, -_lane_col(dk_s[h], AUG_ONE), dfk)
            dk_ref[...] = dk
            dv_ref[...] = dv
            df_ref[...] = dfk

        @pl.when(step == nsteps - 1)
        def _():
            cp = pltpu.make_async_copy(dq_s, dq_hbm, sem)
            cp.start()
            cp.wait()

    qspec = pl.BlockSpec((4, B, LANE), lambda s, qt, kt: (0, qt[s], 0))
    kspec = pl.BlockSpec((4, B, LANE), lambda s, qt, kt: (0, kt[s], 0))
    ospec = pl.BlockSpec((B, CH), lambda s, qt, kt: (kt[s], 0))
    grid_spec = pltpu.PrefetchScalarGridSpec(
        num_scalar_prefetch=2, grid=(nsteps,),
        in_specs=[qspec, kspec, kspec, qspec, pl.BlockSpec(sel.shape, lambda s, qt, kt: (0, 0, 0))],
        out_specs=[ospec, ospec, pl.BlockSpec((B, LANE), lambda s, qt, kt: (kt[s], 0)),
                   pl.BlockSpec(memory_space=pl.ANY)],
        scratch_shapes=[pltpu.VMEM((4, B, LANE), F32), pltpu.VMEM((4, B, LANE), F32), pltpu.VMEM((4, LANE, B), _MXU),
                        pltpu.VMEM((n, 4, LANE, B), F32), pltpu.SemaphoreType.DMA])
    return pl.pallas_call(
        body, name=name, grid_spec=grid_spec,
        out_shape=[jax.ShapeDtypeStruct((T, CH), F32), jax.ShapeDtypeStruct((T, CH), F32),
                   jax.ShapeDtypeStruct((T, LANE), F32), jax.ShapeDtypeStruct((n, 4, LANE, B), F32)],
        compiler_params=_cp("arbitrary"))(qt, kt, qb, ka, va, doa, sel)


def _fox2_post(zf, dqt, dkn, dv, dfk, sel, gq, gk, bf, name):
    T = zf.shape[0]
    tm = _tile(T, FOX_TM)
    nt = T // tm
    B = dqt.shape[3]
    per = B // tm
    upper = jnp.asarray(_tri(tm, False), _MXU)

    def body(z_ref, dqt_ref, dk_ref, dv_ref, df_ref, p_ref, gq_ref, gk_ref, bf_ref, u_ref, dz_ref, sm_ref, carry, rc_s):
        @pl.when(pl.program_id(0) == 0)
        def _():
            carry[...] = jnp.zeros_like(carry)
            sm_ref[...] = jnp.zeros_like(sm_ref)

        lane = lax.broadcasted_iota(jnp.int32, (1, LANE), 1)
        dqn = jnp.zeros((tm, CH), F32)
        dfq = jnp.zeros((tm, LANE), F32)
        for h in range(4):
            blk = dqt_ref[0, h].T
            dqn = dqn + _exact_dot(blk * (HEAD ** -0.5), p_ref[h], _NT, "a")
            dfq = jnp.where(lane == h, _lane_col(blk, AUG_F), dfq)
        z = z_ref[...]
        q_raw, k_raw = z[:, :CH], z[:, CH:2 * CH]
        _, q_r = _head_rms(q_raw, gq_ref[...], 4)
        _, k_r = _head_rms(k_raw, gk_ref[...], 4)
        dq, dgq = _head_rms_bwd(dqn, q_raw, q_r, gq_ref[...], 4)
        dk, dgk = _head_rms_bwd(dk_ref[...], k_raw, k_r, gk_ref[...], 4)
        rc_s[...] = _exact_dot(u_ref[...], dfq + df_ref[...], _NN, "b") + carry[pl.ds(0, 1), :]
        carry[pl.ds(0, 1), :] = rc_s[pl.ds(0, 1), :]
        x = z[:, 3 * CH:] + bf_ref[...]
        dff = jnp.where(lane < 4, rc_s[...] * _sigmoid(-x), 0.0)
        dz_ref[:, :CH] = dq.astype(dz_ref.dtype)
        dz_ref[:, CH:2 * CH] = dk.astype(dz_ref.dtype)
        dz_ref[:, 2 * CH:3 * CH] = dv_ref[...].astype(dz_ref.dtype)
        dz_ref[:, 3 * CH:] = dff.astype(dz_ref.dtype)
        sm_ref[pl.ds(0, 1), :] += dgq
        sm_ref[pl.ds(1, 1), :] += dgk
        sm_ref[pl.ds(2, 1), :LANE] += jnp.sum(dff, axis=0, keepdims=True)

    rev = lambda i: (nt - 1 - i, 0)
    row = pl.BlockSpec((tm, CH), rev)
    lrow = pl.BlockSpec((tm, LANE), rev)
    vec = pl.BlockSpec((1, CH), lambda i: (0, 0))
    return pl.pallas_call(
        body, name=name, grid=(nt,),
        in_specs=[pl.BlockSpec((tm, 3 * CH + LANE), rev),
                  pl.BlockSpec((1, 4, LANE, tm), lambda i: ((nt - 1 - i) // per, 0, 0, (nt - 1 - i) % per)),
                  row, row, lrow, pl.BlockSpec(sel.shape, lambda i: (0, 0, 0)), vec, vec,
                  pl.BlockSpec((1, LANE), lambda i: (0, 0)), pl.BlockSpec((tm, tm), lambda i: (0, 0))],
        out_specs=[pl.BlockSpec((tm, 3 * CH + LANE), rev), pl.BlockSpec((8, CH), lambda i: (0, 0))],
        out_shape=[jax.ShapeDtypeStruct((T, 3 * CH + LANE), _MXU), jax.ShapeDtypeStruct((8, CH), F32)],
        scratch_shapes=[pltpu.VMEM((8, LANE), F32), pltpu.VMEM((tm, LANE), F32)],
        compiler_params=_cp("arbitrary"))(zf, dqt, dkn, dv, dfk, sel, gq, gk, bf, upper)


def _merge_fwd(acts, zg, wbr, wout, x1, name):
    T, D = x1.shape
    tm = _tile(T, 256)

    def body(a0, a1, a2, a3, zg_ref, wbr_ref, wout_ref, x_ref, o_ref, mg_ref):
        merged = None
        for i, a_ref in enumerate((a0, a1, a2, a3)):
            term = _sigmoid(zg_ref[:, i * D:(i + 1) * D]) * _dot(a_ref[...], wbr_ref[i])
            merged = term if merged is None else merged + term
        mg_ref[...] = merged.astype(mg_ref.dtype)
        o_ref[...] = x_ref[...] + _dot(merged, wout_ref[...])

    arow = pl.BlockSpec((tm, CH), lambda i: (i, 0))
    xrow = pl.BlockSpec((tm, D), lambda i: (i, 0))
    return pl.pallas_call(
        body, name=name, grid=(T // tm,),
        in_specs=[arow, arow, arow, arow, pl.BlockSpec((tm, 4 * D), lambda i: (i, 0)),
                  pl.BlockSpec((4, CH, D), lambda i: (0, 0, 0)), pl.BlockSpec((D, D), lambda i: (0, 0)), xrow],
        out_specs=[xrow, xrow],
        out_shape=[jax.ShapeDtypeStruct((T, D), F32), jax.ShapeDtypeStruct((T, D), _MXU)],
        compiler_params=_cp("parallel"))(*acts, zg, wbr, wout, x1)


def _merge_bwd(dx2, acts, zg, wbr, wout, name):
    T, D = dx2.shape
    tm = _tile(T, 256)
    nt = T // tm

    def body(dx_ref, a0, a1, a2, a3, zg_ref, wbr_ref, wout_ref, d0, d1, d2, d3, dzg_ref, dw_ref, dw_s):
        i = pl.program_id(0)

        @pl.when(i == 0)
        def _():
            dw_s[...] = jnp.zeros_like(dw_s)

        dm = _dot(dx_ref[...], wout_ref[...], _NT)
        for b, (a_ref, d_ref) in enumerate(((a0, d0), (a1, d1), (a2, d2), (a3, d3))):
            av = a_ref[...].astype(_MXU)
            g = _sigmoid(zg_ref[:, b * D:(b + 1) * D])
            p = _dot(av, wbr_ref[b])
            dzg_ref[:, b * D:(b + 1) * D] = (dm * p * (g * (1.0 - g))).astype(dzg_ref.dtype)
            dp = (dm * g).astype(_MXU)
            d_ref[...] = _dot(dp, wbr_ref[b], _NT)
            dw_s[b] += _dot(av, dp, _TN)

        @pl.when(i == nt - 1)
        def _():
            dw_ref[...] = dw_s[...].astype(dw_ref.dtype)

    arow = pl.BlockSpec((tm, CH), lambda i: (i, 0))
    xrow = pl.BlockSpec((tm, D), lambda i: (i, 0))
    grow = pl.BlockSpec((tm, 4 * D), lambda i: (i, 0))
    wspec = pl.BlockSpec((4, CH, D), lambda i: (0, 0, 0))
    ash = jax.ShapeDtypeStruct((T, CH), F32)
    return pl.pallas_call(
        body, name=name, grid=(nt,),
        in_specs=[xrow, arow, arow, arow, arow, grow, wspec, pl.BlockSpec((D, D), lambda i: (0, 0))],
        out_specs=[arow, arow, arow, arow, grow, wspec],
        out_shape=[ash, ash, ash, ash, jax.ShapeDtypeStruct((T, 4 * D), _MXU), jax.ShapeDtypeStruct((4, CH, D), _MXU)],
        scratch_shapes=[pltpu.VMEM((4, CH, D), F32)],
        compiler_params=_cp("arbitrary"))(dx2, *acts, zg, wbr, wout)


def _rows_2d(a):
    return a.reshape((-1, a.shape[-1])) if a.ndim > 1 else a.reshape((1, -1))


def _row_tile(rows, cols, n_bufs):
    padded = -(-cols // LANE) * LANE
    cap = max(8, (VMEM_LIMIT // 3) // (2 * n_bufs * 4 * padded))
    return _tile(rows, cap, 8)


def _sum8(recv, name):
    shape = recv.shape[1:]
    r2 = recv.reshape((N_DEV, -1, shape[-1]))
    rows, cols = r2.shape[1:]
    tr = _row_tile(rows, cols, N_DEV // 2 + 1)

    def body(r_ref, o_ref):
        acc = r_ref[0].astype(F32)
        for d in range(1, N_DEV):
            acc = acc + r_ref[d].astype(F32)
        o_ref[...] = acc

    out = pl.pallas_call(
        body, name=name, grid=(rows // tr,),
        in_specs=[pl.BlockSpec((N_DEV, tr, cols), lambda i: (0, i, 0))],
        out_specs=pl.BlockSpec((tr, cols), lambda i: (i, 0)),
        out_shape=jax.ShapeDtypeStruct((rows, cols), F32), compiler_params=_cp("parallel"))(r2)
    return out.reshape(shape)


def _adamw(w, g, m, v, name):
    if w.ndim == 3:
        lead, rows, cols = w.shape
    else:
        lead, (rows, cols) = None, w.shape
    tr = _row_tile(rows, cols, 7)

    def body(w_ref, g_ref, m_ref, v_ref, d_ref, nm_ref, nv_ref):
        gv = g_ref[...]
        nm = ADAM_B1 * m_ref[...] + (1.0 - ADAM_B1) * gv
        nv = ADAM_B2 * v_ref[...] + (1.0 - ADAM_B2) * jnp.square(gv)
        m_hat = nm / (1.0 - ADAM_B1 ** ADAM_STEP)
        v_hat = nv / (1.0 - ADAM_B2 ** ADAM_STEP)
        d_ref[...] = -ADAM_LR * (m_hat / (jnp.sqrt(v_hat) + ADAM_EPS) + ADAM_WD * w_ref[...])
        nm_ref[...] = nm
        nv_ref[...] = nv

    if lead is None:
        grid, sem = (rows // tr,), ("parallel",)
        spec = pl.BlockSpec((tr, cols), lambda i: (i, 0))
    else:
        grid, sem = (lead, rows // tr), ("parallel", "parallel")
        spec = pl.BlockSpec((None, tr, cols), lambda l, i: (l, i, 0))
    osh = jax.ShapeDtypeStruct(w.shape, F32)
    return tuple(pl.pallas_call(
        body, name=name, grid=grid, in_specs=[spec] * 4, out_specs=[spec] * 3,
        out_shape=[osh] * 3, compiler_params=_cp(*sem))(w, g, m, v))


def _exchange(items, name):
    n = len(items)
    widths, out_shapes = [], []
    for src, kind, ax in items:
        if kind == "gather":
            w = src.shape[ax]
            shp = list(src.shape)
            shp[ax] = N_DEV * w
        else:
            w = src.shape[ax] // N_DEV
            shp = list(src.shape)
            shp[ax] = w
            shp = [N_DEV] + shp
        widths.append(w)
        out_shapes.append(jax.ShapeDtypeStruct(tuple(shp), src.dtype))

    def body(*refs):
        srcs, outs = refs[:n], refs[n:2 * n]
        send, recv, lsem = refs[2 * n:]
        x, y, c = lax.axis_index("x"), lax.axis_index("y"), lax.axis_index("c")
        me = 4 * x + 2 * y + c

        def peer(k):
            b = k + 1
            px = 1 - x if b & 4 else x
            py = 1 - y if b & 2 else y
            pc = 1 - c if b & 1 else c
            return (px, py, pc), 4 * px + 2 * py + pc

        def win(ref, ax, idx, w):
            return ref.at[tuple([slice(None)] * ax + [pl.ds(idx * w, w)])]

        def ends(j, mine, theirs):
            _, kind, ax = items[j]
            if kind == "gather":
                return srcs[j], win(outs[j], ax, mine, widths[j])
            return win(srcs[j], ax, theirs, widths[j]), outs[j].at[mine]

        local, sent = [], []
        for j in range(n):
            s, d = ends(j, me, me)
            cp = pltpu.make_async_copy(s, d, lsem.at[j])
            cp.start()
            local.append(cp)
            for k in range(N_DEV - 1):
                dev, pid = peer(k)
                s, d = ends(j, me, pid)
                cp = pltpu.make_async_remote_copy(s, d, send.at[j, k], recv.at[j, k], device_id=dev,
                                                  device_id_type=pl.DeviceIdType.MESH)
                cp.start()
                sent.append(cp)
        for j in range(n):
            for k in range(N_DEV - 1):
                dev, pid = peer(k)
                s, d = ends(j, pid, me)
                pltpu.make_async_remote_copy(s, d, send.at[j, k], recv.at[j, k], device_id=dev,
                                             device_id_type=pl.DeviceIdType.MESH).wait_recv()
        for cp in sent:
            cp.wait_send()
        for cp in local:
            cp.wait()

    hbm = pl.BlockSpec(memory_space=pl.ANY)
    return pl.pallas_call(
        body, name=name, in_specs=[hbm] * n, out_specs=[hbm] * n, out_shape=out_shapes,
        scratch_shapes=[pltpu.SemaphoreType.DMA((n, N_DEV - 1)), pltpu.SemaphoreType.DMA((n, N_DEV - 1)),
                        pltpu.SemaphoreType.DMA((n,))],
        compiler_params=pltpu.CompilerParams(has_side_effects=True))(*[it[0] for it in items])


def _exchange_plan(items):
    widths, out_shapes = [], []
    for src, kind, ax in items:
        shp = list(src.shape)
        if kind == "gather":
            w = src.shape[ax]
            shp[ax] = N_DEV * w
        else:
            w = src.shape[ax] // N_DEV
            shp[ax] = w
            shp = [N_DEV] + shp
        widths.append(w)
        out_shapes.append((tuple(shp), src.dtype))
    return widths, out_shapes


def _exchange_refs(items, widths, srcs, outs):
    x, y, c = lax.axis_index("x"), lax.axis_index("y"), lax.axis_index("c")
    me = 4 * x + 2 * y + c

    def peer(k):
        b = k + 1
        px = 1 - x if b & 4 else x
        py = 1 - y if b & 2 else y
        pc = 1 - c if b & 1 else c
        return (px, py, pc), 4 * px + 2 * py + pc

    def win(ref, ax, idx, w):
        return ref.at[tuple([slice(None)] * ax + [pl.ds(idx * w, w)])]

    def ends(j, mine, theirs):
        _, kind, ax = items[j]
        if kind == "gather":
            return srcs[j], win(outs[j], ax, mine, widths[j])
        return win(srcs[j], ax, theirs, widths[j]), outs[j].at[mine]

    return me, peer, ends


_HBM = pl.BlockSpec(memory_space=pltpu.HBM)
_SEM = pl.BlockSpec(memory_space=pltpu.SEMAPHORE)


def _exchange_start(items, name):
    n = len(items)
    widths, out_shapes = _exchange_plan(items)
    meta = [(None, kind, ax) for _, kind, ax in items]

    def body(*refs):
        srcs, lands = refs[:n], refs[n:2 * n]
        send, recv, lsem = refs[2 * n], refs[2 * n + 1], refs[2 * n + 2]
        token = refs[-1]
        me, peer, ends = _exchange_refs(meta, widths, srcs, lands)
        for j in range(n):
            for k in range(N_DEV - 1):
                dev, pid = peer(k)
                s, d = ends(j, me, pid)
                q = j * (N_DEV - 1) + k
                pltpu.make_async_remote_copy(s, d, send.at[q], recv.at[q], device_id=dev,
                                             device_id_type=pl.DeviceIdType.MESH).start()
        for j in range(n):
            s, d = ends(j, me, me)
            pltpu.make_async_copy(s, d, lsem.at[j]).start()
        token[...] = jnp.zeros_like(token)

    srcs = [pltpu.with_memory_space_constraint(it[0], pltpu.HBM) for it in items]
    lands = [pltpu.with_memory_space_constraint(lax.empty(shp, dt), pltpu.HBM) for shp, dt in out_shapes]
    outs = pl.pallas_call(
        body, name=name,
        out_shape=(pltpu.SemaphoreType.DMA((n * (N_DEV - 1),)), pltpu.SemaphoreType.DMA((n * (N_DEV - 1),)),
                   pltpu.SemaphoreType.DMA((n,)),
                   *[pltpu.HBM(s.shape, s.dtype) for s in srcs], *[pltpu.HBM(shp, dt) for shp, dt in out_shapes],
                   jax.ShapeDtypeStruct((8, LANE), F32)),
        in_specs=[_HBM] * (2 * n),
        out_specs=(_SEM, _SEM, _SEM, *([_HBM] * (2 * n)), pl.BlockSpec(memory_space=pltpu.VMEM)),
        input_output_aliases={i: 3 + i for i in range(2 * n)},
        compiler_params=pltpu.CompilerParams(has_side_effects=pltpu.SideEffectType.DATAFLOW_SIDE_EFFECTING),
    )(*srcs, *lands)
    handle = (meta, widths, outs[0], outs[1], outs[2], outs[3:3 + n], outs[3 + n:3 + 2 * n])
    return handle, outs[-1]


def _exchange_wait(handle, after, name):
    meta, widths, send_sem, recv_sem, local_sem, src_thru, land_thru = handle
    n = len(meta)

    def body(*refs):
        srcs, lands = refs[:n], refs[n:2 * n]
        send, recv, lsem = refs[2 * n], refs[2 * n + 1], refs[2 * n + 2]
        me, peer, ends = _exchange_refs(meta, widths, srcs, lands)
        for j in range(n):
            for k in range(N_DEV - 1):
                dev, pid = peer(k)
                q = j * (N_DEV - 1) + k
                s, d = ends(j, me, pid)
                pltpu.make_async_remote_copy(s, d, send.at[q], recv.at[q], device_id=dev,
                                             device_id_type=pl.DeviceIdType.MESH).wait_send()
                s, d = ends(j, pid, me)
                pltpu.make_async_remote_copy(s, d, send.at[q], recv.at[q], device_id=dev,
                                             device_id_type=pl.DeviceIdType.MESH).wait_recv()
        for j in range(n):
            s, d = ends(j, me, me)
            pltpu.make_async_copy(s, d, lsem.at[j]).wait()

    outs = pl.pallas_call(
        body, name=name,
        out_shape=tuple(pltpu.HBM(a.shape, a.dtype) for a in (*src_thru, *land_thru)),
        in_specs=[_HBM] * (2 * n) + [_SEM, _SEM, _SEM, pl.BlockSpec(memory_space=pl.ANY)],
        out_specs=tuple([_HBM] * (2 * n)),
        input_output_aliases={i: i for i in range(2 * n)},
        compiler_params=pltpu.CompilerParams(has_side_effects=pltpu.SideEffectType.DATAFLOW_SIDE_EFFECTING),
    )(*src_thru, *land_thru, send_sem, recv_sem, local_sem, after)
    return list(outs[n:])


def _pack(arrs):
    flat = jnp.concatenate([a.reshape(-1).astype(F32) for a in arrs])
    n = flat.shape[0]
    rows = -(-n // (8 * LANE)) * 8
    return jnp.pad(flat, (0, rows * LANE - n)).reshape(rows, LANE)


def _unpack(buf, shapes):
    flat = buf.reshape(-1)
    out, off = [], 0
    for s in shapes:
        sz = int(np.prod(s))
        out.append(flat[off:off + sz].reshape(s))
        off += sz
    return out


def _pad_axis(a, axis, size):
    pad = [(0, 0)] * a.ndim
    pad[axis] = (0, size - a.shape[axis])
    return jnp.pad(a, pad)


def _ffn_forward(x, g, wg, wu, wd, tag):
    a = _rms_fwd(x, g, f"{tag}_rms")
    gate, up, hid = _ffn_up(a, wg, wu, f"{tag}_up")
    out = _mm([(hid, wd)], "nn", F32, f"{tag}_down", scale=0.5, res=x)
    return out, (x, a, gate, up, hid)


def _ffn_backward(dxp, saved, g, wg, wu, wd, tag, emit=None, after=None):
    x, a, gate, up, hid = saved
    d_gate, d_up = _ffn_bwd_hid(dxp, wd, gate, up, f"{tag}_bwd_hid", after)
    d_wd = _mm([(hid, dxp)], "tn", _MXU, f"{tag}_dwd", scale=0.5, tk=2048)
    tok = emit("down", d_wd) if emit is not None else None
    d_wg = _mm([(a, d_gate)], "tn", _MXU, f"{tag}_dwg", tk=2048, after=tok)
    tok = emit("gate", d_wg) if emit is not None else None
    d_wu = _mm([(a, d_up)], "tn", _MXU, f"{tag}_dwu", tk=2048, after=tok)
    tok = emit("up", d_wu) if emit is not None else None
    d_a = _mm([(d_gate, wg), (d_up, wu)], "nt", F32, f"{tag}_da", after=tok)
    dx, dg = _rms_bwd(d_a, x, g, dxp, f"{tag}_rms_bwd")
    return dx, dg, d_wg, d_wu, d_wd


def _tile_vec(v, reps):
    return jnp.tile(v.reshape(1, -1), (1, reps))


def _mixer_forward(x1, p, consts, tag):
    h = _rms_fwd(x1, p["mix_norm"], f"{tag}_rms")
    zg = _mm([(h, p["w_zg"])], "nn", F32, f"{tag}_in_g")
    zc = _mm([(h, p["w_conf"])], "nn", F32, f"{tag}_in_c")
    zs = _mm([(h, p["w_sc"])], "nn", F32, f"{tag}_in_s")
    zw = _mm([(h, p["w_swa"])], "nn", F32, f"{tag}_in_w")
    zf = _mm([(h, p["w_fox"])], "nn", F32, f"{tag}_in_f")
    u1, act_c = _conf_fwd(zc, p["conf_dw"], p["conf_dw_b"], p["conf_ln_g"], p["conf_ln_b"], f"{tag}_conf")
    act_s = _sc_fwd(zs, p["sc_conv"], f"{tag}_sc")
    act_w = _swa_fwd(zw, p["swa_q_norm"], p["swa_k_norm"], p["swa_sink"], consts["bias"], consts["expand"], f"{tag}_swa")
    qa, ka, va, vat = _fox2_prep(zf, p["fox_q_norm"], p["fox_k_norm"], p["b_forget"], consts["sel"], f"{tag}_fox_prep")
    act_f, qb = _fox2_fwd(qa, ka, vat, consts["sel"], f"{tag}_fox")
    acts = (act_c, act_s, act_w, act_f)
    x2, merged = _merge_fwd(acts, zg, p["w_br"], p["w_out"], x1, f"{tag}_merge")
    saved = (x1, h, zg, zc, zs, zw, zf, u1, acts, qb, ka, va, merged)
    return x2, saved


def _mixer_backward(dx2, saved, p, consts, tag, after=None):
    x1, h, zg, zc, zs, zw, zf, u1, acts, qb, ka, va, merged = saved
    g = {}
    g["w_out"] = _mm([(merged, dx2)], "tn", _MXU, f"{tag}_dwout", tk=2048, after=after)
    d_c, d_s, d_w, d_f, dzg, g["w_br"] = _merge_bwd(dx2, acts, zg, p["w_br"], p["w_out"], f"{tag}_merge_bwd")
    du1, sm_c = _conf_bwd_ln(d_c, u1, p["conf_ln_g"], p["conf_ln_b"], f"{tag}_conf_bwd_ln")
    dzc, g["conf_dw"] = _conf_bwd_conv(zc, du1, p["conf_dw"], f"{tag}_conf_bwd_conv")
    g["conf_ln_g"], g["conf_ln_b"], g["conf_dw_b"] = sm_c[0], sm_c[1], sm_c[2]
    dzs, g["sc_conv"] = _sc_bwd(zs, d_s, p["sc_conv"], f"{tag}_sc_bwd")
    dzw, dgq, dgk, g["swa_sink"], g["rel_bias"] = _swa_bwd(
        zw, d_w, p["swa_q_norm"], p["swa_k_norm"], p["swa_sink"], consts["bias"], consts["bucket"], consts["expand"],
        f"{tag}_swa_bwd")
    g["swa_q_norm"], g["swa_k_norm"] = dgq, dgk
    doa = _fox2_bwd_prep(acts[3], d_f, consts["sel"], f"{tag}_fox_bwd_prep")
    dkn, dv, dfk, dqt = _fox2_bwd(qb, ka, va, doa, consts["sel"], f"{tag}_fox_bwd")
    dzf, sm_f = _fox2_post(zf, dqt, dkn, dv, dfk, consts["sel"], p["fox_q_norm"], p["fox_k_norm"], p["b_forget"], f"{tag}_fox_post")
    g["fox_q_norm"], g["fox_k_norm"], g["b_forget"] = sm_f[0], sm_f[1], sm_f[2]
    parts = ((dzg, "w_zg"), (dzc, "w_conf"), (dzs, "w_sc"), (dzw, "w_swa"), (dzf, "w_fox"))
    for dz, wname in parts:
        g[wname] = _mm([(h, dz)], "tn", _MXU, f"{tag}_d{wname}", tk=2048)
    dh = _mm([(dz, p[wname]) for dz, wname in parts], "nt", F32, f"{tag}_dh", tm=512)
    dx1, g["mix_norm"] = _rms_bwd(dh, x1, p["mix_norm"], dx2, f"{tag}_rms_bwd")
    return dx1, g


W_NAMES = ['rel_bias', 'ffn1_norm', 'ffn1_w_gate', 'ffn1_w_up', 'ffn1_w_down', 'mix_norm', 'w_in', 'b_forget', 'conf_dw',
           'conf_dw_b', 'conf_ln_g', 'conf_ln_b', 'conf_w_out', 'sc_conv', 'sc_w_out', 'swa_q_norm', 'swa_k_norm',
           'swa_sink', 'swa_w_o', 'fox_q_norm', 'fox_k_norm', 'fox_w_o', 'w_out', 'ffn2_norm', 'ffn2_w_gate',
           'ffn2_w_up', 'ffn2_w_down']
SMALL = ['rel_bias', 'ffn1_norm', 'mix_norm', 'b_forget', 'conf_dw', 'conf_dw_b', 'conf_ln_g', 'conf_ln_b', 'sc_conv',
         'swa_q_norm', 'swa_k_norm', 'swa_sink', 'fox_q_norm', 'fox_k_norm', 'ffn2_norm']
BRANCH_W = ['conf_w_out', 'sc_w_out', 'swa_w_o', 'fox_w_o']
IN_CONF, IN_SC, IN_SWA, IN_FOX, IN_FF = (0, 512), (512, 1280), (1280, 1792), (1792, 2560), (2560, 2564)


def _step(w, m, v, x, loss_target):
    T, D = x.shape
    L = w["w_out"].shape[0]
    fs = w["ffn1_w_gate"].shape[2]
    fsp = -(-fs // LANE) * LANE
    dev = 4 * lax.axis_index("x") + 2 * lax.axis_index("y") + lax.axis_index("c")

    def cast(a):
        return a.astype(_MXU)

    win = w["w_in"]
    fox_cols = jnp.concatenate([win[..., IN_FOX[0]:IN_FF[1]],
                                jnp.zeros(win.shape[:2] + (LANE - (IN_FF[1] - IN_FF[0]),), win.dtype)], axis=-1)
    shards = {
        "ffn1_w_gate": (cast(_pad_axis(w["ffn1_w_gate"], 2, fsp)), 2),
        "ffn1_w_up": (cast(_pad_axis(w["ffn1_w_up"], 2, fsp)), 2),
        "ffn1_w_down": (cast(_pad_axis(w["ffn1_w_down"], 1, fsp)), 1),
        "ffn2_w_gate": (cast(_pad_axis(w["ffn2_w_gate"], 2, fsp)), 2),
        "ffn2_w_up": (cast(_pad_axis(w["ffn2_w_up"], 2, fsp)), 2),
        "ffn2_w_down": (cast(_pad_axis(w["ffn2_w_down"], 1, fsp)), 1),
        "w_zg": (cast(win[..., IN_FF[1]:]), 1),
        "w_conf": (cast(win[..., IN_CONF[0]:IN_CONF[1]]), 1),
        "w_sc": (cast(win[..., IN_SC[0]:IN_SC[1]]), 1),
        "w_swa": (cast(win[..., IN_SWA[0]:IN_SWA[1]]), 1),
        "w_fox": (cast(fox_cols), 1),
        "w_out": (cast(w["w_out"]), 1),
        "w_br": (cast(jnp.stack([w[n] for n in BRANCH_W], axis=1)), 3),
    }
    big = list(shards)
    conv_shard = jnp.concatenate([jnp.swapaxes(w["conf_dw"], 1, 2), jnp.swapaxes(w["sc_conv"], 1, 2)], axis=2)

    stages = (("ffn1", ["ffn1_w_gate", "ffn1_w_up"]), ("ffn1d", ["ffn1_w_down"]),
              ("mix", ["w_zg", "w_conf", "w_sc", "w_swa", "w_fox", "w_out", "w_br"]),
              ("ffn2", ["ffn2_w_gate", "ffn2_w_up", "ffn2_w_down"]))
    stage_names = dict(stages)
    flight = {}

    def depart(l, st, dep):
        items = [(shards[n][0][l], "gather", shards[n][1] - 1) for n in stage_names[st]]
        if (l, st) == (0, "mix"):
            items.append((conv_shard, "gather", 1))
        if dep is not None:
            src0 = items[0][0]
            zero = (dep[(0,) * dep.ndim].astype(F32) * 0.0).astype(src0.dtype)
            items[0] = (src0 + zero,) + items[0][1:]
        flight[l, st], tok = _exchange_start(items, f"gather_start_l{l}_{st}")
        return tok

    def arrive(l, st, after):
        got = _exchange_wait(flight.pop((l, st)), after, f"gather_wait_l{l}_{st}")
        params[l].update(zip(stage_names[st], got))
        if (l, st) == (0, "mix"):
            conv_full = jnp.swapaxes(got[-1], 1, 2)
            for i, q in enumerate(params):
                q["conf_dw"] = _pad_axis(conv_full[i, :CONV_K], 0, CONV_HALO)
                q["sc_conv"] = _pad_axis(conv_full[i, CONV_K:], 0, SC_HALO)
        return got[0]

    bucket = jnp.asarray(_swa_bucket_matrix(min(SWA_TQ, T)))
    consts = {"bucket": bucket, "expand": jnp.asarray(_kv_expand_matrix(), _MXU),
              "sel": jnp.asarray(_head_select_matrix(), _MXU),
              "bias": _swa_bias(w["rel_bias"], bucket, "swa_bias")}

    def layer_params(l):
        p = {}
        for n in ("ffn1_norm", "mix_norm", "ffn2_norm", "conf_dw_b", "conf_ln_g", "conf_ln_b"):
            p[n] = w[n][l].reshape(1, -1)
        p["swa_q_norm"], p["fox_q_norm"] = _tile_vec(w["swa_q_norm"][l], 4), _tile_vec(w["fox_q_norm"][l], 4)
        p["swa_k_norm"], p["fox_k_norm"] = _tile_vec(w["swa_k_norm"][l], 2), _tile_vec(w["fox_k_norm"][l], 4)
        p["swa_sink"] = w["swa_sink"][l].reshape(1, 4)
        p["b_forget"] = _pad_axis(w["b_forget"][l].reshape(1, 4), 1, LANE)
        return p

    params = [layer_params(l) for l in range(L)]
    saved = [None] * L
    cur = x
    first = depart(0, "ffn1", None)
    for l, p in enumerate(params):
        a = _rms_fwd(cur, p["ffn1_norm"] + (0.0 if l else first[0:1, 0:1]), f"l{l}_ffn1_rms")
        got = arrive(l, "ffn1", a)
        tok = depart(l, "mix", depart(l, "ffn1d", got))
        gate, up, hid = _ffn_up(a, p["ffn1_w_gate"], p["ffn1_w_up"], f"l{l}_ffn1_up", after=tok)
        got = arrive(l, "ffn1d", hid)
        tok = depart(l, "ffn2", got)
        x1 = _mm([(hid, p["ffn1_w_down"])], "nn", F32, f"l{l}_ffn1_down", scale=0.5, res=cur, after=tok)
        s1 = (cur, a, gate, up, hid)
        got = arrive(l, "mix", x1)
        zero = depart(l + 1, "ffn1", got)[0:1, 0:1] if l + 1 < L else 0.0
        x2, s2 = _mixer_forward(x1, dict(p, mix_norm=p["mix_norm"] + zero), consts, f"l{l}_mix")
        arrive(l, "ffn2", x2)
        cur, s3 = _ffn_forward(x2, p["ffn2_norm"], p["ffn2_w_gate"], p["ffn2_w_up"], p["ffn2_w_down"], f"l{l}_ffn2")
        saved[l] = (s1, s2, s3)
    dcur, loss_part = _loss_grad(cur, loss_target)

    grads = [None] * L
    leaving = []

    def leave(l, st, names, g):
        h, tok = _exchange_start([(g[n], "scatter", shards[n][1] - 1) for n in names], f"scatter_start_l{l}_{st}")
        leaving.append((l, st, names, h))
        return tok

    tok = None
    for l in reversed(range(L)):
        p = params[l]
        s1, s2, s3 = saved[l]
        g = {}
        dcur, g["ffn2_norm"], g["ffn2_w_gate"], g["ffn2_w_up"], g["ffn2_w_down"] = _ffn_backward(
            dcur, s3, p["ffn2_norm"], p["ffn2_w_gate"], p["ffn2_w_up"], p["ffn2_w_down"], f"l{l}_ffn2", after=tok)
        tok = leave(l, "ffn2", stage_names["ffn2"], g)
        dcur, gm = _mixer_backward(dcur, s2, p, consts, f"l{l}_mix", after=tok)
        g.update(gm)
        tok = leave(l, "mix", stage_names["mix"], g)
        ffn1 = ["ffn1_w_gate", "ffn1_w_up", "ffn1_w_down"]
        emit = (lambda which, arr, l=l: leave(l, which, [f"ffn1_w_{which}"], {f"ffn1_w_{which}": arr})) if l == 0 else None
        dcur, g["ffn1_norm"], g["ffn1_w_gate"], g["ffn1_w_up"], g["ffn1_w_down"] = _ffn_backward(
            dcur, s1, p["ffn1_norm"], p["ffn1_w_gate"], p["ffn1_w_up"], p["ffn1_w_down"], f"l{l}_ffn1", emit, after=tok)
        if emit is None:
            tok = leave(l, "ffn1", ffn1, g)
        grads[l] = g
    grad_x = dcur

    gsum = {n: [None] * L for n in big}
    for l, st, names, h in leaving:
        for n, r in zip(names, _exchange_wait(h, grad_x, f"scatter_wait_l{l}_{st}")):
            gsum[n][l] = _sum8(r, f"sum_{n}_l{l}")
    gsum = {n: jnp.stack(parts) for n, parts in gsum.items()}
    gw = {}
    wide = ("ffn1_w_gate", "ffn1_w_up", "ffn2_w_gate", "ffn2_w_up")
    gw_t = {n: jnp.swapaxes(gsum[n][:, :, :fs], 1, 2) for n in wide}
    for n in wide:
        gw[n] = jnp.swapaxes(gw_t[n], 1, 2)
    for n in ("ffn1_w_down", "ffn2_w_down"):
        gw[n] = gsum[n][:, :fs, :]
    gw["w_out"] = gsum["w_out"]
    for i, n in enumerate(BRANCH_W):
        gw[n] = gsum["w_br"][:, i]
    gw["w_in"] = jnp.concatenate([gsum["w_conf"], gsum["w_sc"], gsum["w_swa"],
                                  gsum["w_fox"][..., :IN_FF[1] - IN_FOX[0]], gsum["w_zg"]], axis=-1)

    def small_partial(n):
        per_layer = [grads[l][n] for l in range(L)]
        if n == "rel_bias":
            return sum(pl_[:, :4] for pl_ in per_layer)
        if n in ("swa_sink", "b_forget"):
            return jnp.stack([a.reshape(-1)[:4] for a in per_layer])
        if n in ("swa_q_norm", "fox_q_norm", "fox_k_norm"):
            return jnp.stack([a.reshape(4, HEAD).sum(0) for a in per_layer])
        if n == "swa_k_norm":
            return jnp.stack([a.reshape(2, HEAD).sum(0) for a in per_layer])
        if n == "conf_dw":
            return jnp.stack([a[:CONV_K] for a in per_layer])
        if n == "sc_conv":
            return jnp.stack([a[:SC_K] for a in per_layer])
        return jnp.stack([a.reshape(-1) for a in per_layer])

    partial = [small_partial(n) for n in SMALL]
    small_shapes = [a.shape for a in partial]
    all_parts = _exchange([(_pack(partial), "gather", 0)], "gather_small_grads")[0]
    rows = all_parts.shape[0] // N_DEV
    small_sum = _unpack(_sum8(all_parts.reshape(N_DEV, rows, LANE), "sum_small"), small_shapes)
    for n, a in zip(SMALL, small_sum):
        if n in ("conf_dw", "sc_conv"):
            cs = w[n].shape[2]
            a = lax.dynamic_slice_in_dim(a, dev * cs, cs, axis=2)
        gw[n] = a

    delta, new_m, new_v = {}, {}, {}
    for n in W_NAMES:
        if n in wide:
            outs = _adamw(jnp.swapaxes(w[n], 1, 2), gw_t[n], jnp.swapaxes(m[n], 1, 2), jnp.swapaxes(v[n], 1, 2),
                          f"adamw_{n}")
            delta[n], new_m[n], new_v[n] = (jnp.swapaxes(o, 1, 2) for o in outs)
        elif n not in SMALL:
            delta[n], new_m[n], new_v[n] = _adamw(w[n], gw[n], m[n], v[n], f"adamw_{n}")
    shapes = [w[n].shape for n in SMALL]
    outs = _adamw(_pack([w[n] for n in SMALL]), _pack([gw[n] for n in SMALL]), _pack([m[n] for n in SMALL]),
                  _pack([v[n] for n in SMALL]), "adamw_small")
    for res, out in zip((delta, new_m, new_v), outs):
        for n, a in zip(SMALL, _unpack(out, shapes)):
            res[n] = a

    loss = lax.psum(loss_part[0, 0], ("x", "y", "c"))
    return loss, grad_x, gw, delta, new_m, new_v


def kernel(x, rel_bias, ffn1_norm, ffn1_w_gate, ffn1_w_up, ffn1_w_down, mix_norm, w_in, b_forget, conf_dw, conf_dw_b, conf_ln_g, conf_ln_b, conf_w_out, sc_conv, sc_w_out, swa_q_norm, swa_k_norm, swa_sink, swa_w_o, fox_q_norm, fox_k_norm, fox_w_o, w_out, ffn2_norm, ffn2_w_gate, ffn2_w_up, ffn2_w_down, loss_target, m_rel_bias, m_ffn1_norm, m_ffn1_w_gate, m_ffn1_w_up, m_ffn1_w_down, m_mix_norm, m_w_in, m_b_forget, m_conf_dw, m_conf_dw_b, m_conf_ln_g, m_conf_ln_b, m_conf_w_out, m_sc_conv, m_sc_w_out, m_swa_q_norm, m_swa_k_norm, m_swa_sink, m_swa_w_o, m_fox_q_norm, m_fox_k_norm, m_fox_w_o, m_w_out, m_ffn2_norm, m_ffn2_w_gate, m_ffn2_w_up, m_ffn2_w_down, v_rel_bias, v_ffn1_norm, v_ffn1_w_gate, v_ffn1_w_up, v_ffn1_w_down, v_mix_norm, v_w_in, v_b_forget, v_conf_dw, v_conf_dw_b, v_conf_ln_g, v_conf_ln_b, v_conf_w_out, v_sc_conv, v_sc_w_out, v_swa_q_norm, v_swa_k_norm, v_swa_sink, v_swa_w_o, v_fox_q_norm, v_fox_k_norm, v_fox_w_o, v_w_out, v_ffn2_norm, v_ffn2_w_gate, v_ffn2_w_up, v_ffn2_w_down):
    args = locals()
    w = {n: args[n] for n in W_NAMES}
    m = {n: args["m_" + n] for n in W_NAMES}
    v = {n: args["v_" + n] for n in W_NAMES}
    T, D = x.shape[-2:]
    loss, grad_x, gw, delta, new_m, new_v = _step(w, m, v, x.reshape(T, D), loss_target.reshape(T, D))
    return (loss, grad_x.reshape(x.shape), *[gw[n] for n in W_NAMES], *[delta[n] for n in W_NAMES],
            *[new_m[n] for n in W_NAMES], *[new_v[n] for n in W_NAMES])
```

```python
import math

import numpy as np
import jax
import jax.numpy as jnp
from jax import lax
from jax.experimental import pallas as pl
from jax.experimental.pallas import tpu as pltpu

F32 = jnp.float32
_MXU = jnp.bfloat16
EPS = 1e-6
NEG_INF = -1e30
HEAD = 64
CH = 256
WINDOW = 128
CONV_K = 31
SC_K = 3
CONV_HALO = 32
SC_HALO = 8
N_BUCKETS = 32
MAX_DISTANCE = 128
N_DEV = 8
LANE = 128
ROW_TILE = 512
VMEM_LIMIT = 48 * 1024 * 1024
ADAM_LR, ADAM_B1, ADAM_B2, ADAM_EPS, ADAM_WD, ADAM_STEP = 0.001, 0.9, 0.999, 1e-08, 0.01, 10

_NN = (((1,), (0,)), ((), ()))
_NT = (((1,), (1,)), ((), ()))
_TN = (((0,), (0,)), ((), ()))


def _cp(*sem):
    return pltpu.CompilerParams(dimension_semantics=sem, vmem_limit_bytes=VMEM_LIMIT)


def _tile(n, pref, align=LANE):
    t = (min(n, pref) // align) * align
    while t >= align:
        if n % t == 0:
            return t
        t -= align
    return n


def _dot(a, b, dims=_NN):
    return lax.dot_general(a.astype(_MXU), b.astype(_MXU), dims, preferred_element_type=F32)


def _split3(x):
    hi = x.astype(_MXU)
    r1 = x - hi.astype(F32)
    mid = r1.astype(_MXU)
    lo = (r1 - mid.astype(F32)).astype(_MXU)
    return hi, mid, lo


def _exact_dot(a, b, dims, data):
    if data == "a":
        return sum(lax.dot_general(p, b.astype(_MXU), dims, preferred_element_type=F32) for p in _split3(a))
    return sum(lax.dot_general(a.astype(_MXU), p, dims, preferred_element_type=F32) for p in _split3(b))


def _sigmoid(x):
    return jax.nn.sigmoid(x)


def _lane_mask(width, h):
    lane = lax.broadcasted_iota(jnp.int32, (1, width), 1)
    return (lane >= h * HEAD) & (lane < (h + 1) * HEAD)


def _head_rms(x, g, nh):
    xx = x * x
    ms = jnp.zeros_like(x)
    for h in range(nh):
        mk = _lane_mask(x.shape[-1], h)
        s = jnp.sum(jnp.where(mk, xx, 0.0), axis=-1, keepdims=True) * (1.0 / HEAD)
        ms = jnp.where(mk, s, ms)
    r = lax.rsqrt(ms + EPS)
    return x * r * g, r


def _head_rms_bwd(dy, x, r, g, nh):
    w = dy * g
    wx = w * x
    c = jnp.zeros_like(x)
    for h in range(nh):
        mk = _lane_mask(x.shape[-1], h)
        s = jnp.sum(jnp.where(mk, wx, 0.0), axis=-1, keepdims=True) * (1.0 / HEAD)
        c = jnp.where(mk, s, c)
    dx = r * w - x * (r * r * r) * c
    dg = jnp.sum(dy * x * r, axis=0, keepdims=True)
    return dx, dg


def _mm(pairs, mode, out_dtype, name, scale=None, res=None, tm=1024, tn=1024, tk=1024, after=None):
    a0, b0 = pairs[0]
    M = a0.shape[1] if mode == "tn" else a0.shape[0]
    N = b0.shape[0] if mode == "nt" else b0.shape[1]
    tm, tn = _tile(M, tm), _tile(N, tn)
    dims = {"nn": _NN, "nt": _NT, "tn": _TN}[mode]
    tks, nks, offs = [], [], []
    for a, _ in pairs:
        K = a.shape[0] if mode == "tn" else a.shape[1]
        t = _tile(K, tk)
        tks.append(t)
        nks.append(K // t)
        offs.append(sum(nks[:-1]))
    nk_tot = sum(nks)
    in_specs, args = [], []
    for (a, b), t, nk, off in zip(pairs, tks, nks, offs):
        def kk(k, off=off, nk=nk):
            return jnp.clip(k - off, 0, nk - 1)
        if mode == "tn":
            in_specs.append(pl.BlockSpec((t, tm), lambda i, j, k, kk=kk: (kk(k), i)))
        else:
            in_specs.append(pl.BlockSpec((tm, t), lambda i, j, k, kk=kk: (i, kk(k))))
        if mode == "nt":
            in_specs.append(pl.BlockSpec((tn, t), lambda i, j, k, kk=kk: (j, kk(k))))
        else:
            in_specs.append(pl.BlockSpec((t, tn), lambda i, j, k, kk=kk: (kk(k), j)))
        args += [a, b]
    if res is not None:
        in_specs.append(pl.BlockSpec((tm, tn), lambda i, j, k: (i, j)))
        args.append(res)
    if after is not None:
        in_specs.append(pl.BlockSpec(memory_space=pl.ANY))
        args.append(after)
    npairs = len(pairs)

    def body(*refs):
        ab = refs[:2 * npairs]
        res_ref = refs[2 * npairs] if res is not None else None
        o_ref = refs[2 * npairs + (res is not None) + (after is not None)]
        acc = refs[-1]
        k = pl.program_id(2)

        def finish(r):
            if scale is not None:
                r = r * scale
            if res_ref is not None:
                r = r + res_ref[...]
            o_ref[...] = r.astype(o_ref.dtype)

        if nk_tot == 1:
            finish(_dot(ab[0][...], ab[1][...], dims))
            return

        @pl.when(k == 0)
        def _():
            acc[...] = jnp.zeros_like(acc)

        for p in range(npairs):
            @pl.when(jnp.logical_and(k >= offs[p], k < offs[p] + nks[p]))
            def _(p=p):
                acc[...] += _dot(ab[2 * p][...], ab[2 * p + 1][...], dims)

        @pl.when(k == nk_tot - 1)
        def _():
            finish(acc[...])

    return pl.pallas_call(
        body, name=name, grid=(M // tm, N // tn, nk_tot), in_specs=in_specs,
        out_specs=pl.BlockSpec((tm, tn), lambda i, j, k: (i, j)),
        out_shape=jax.ShapeDtypeStruct((M, N), out_dtype),
        scratch_shapes=[pltpu.VMEM((tm, tn), F32)],
        compiler_params=_cp("parallel", "parallel", "arbitrary"))(*args)


def _rms_fwd(x, g, name):
    T, D = x.shape
    tm = _tile(T, ROW_TILE)

    def body(x_ref, g_ref, o_ref):
        xv = x_ref[...]
        r = lax.rsqrt(jnp.mean(xv * xv, axis=-1, keepdims=True) + EPS)
        o_ref[...] = (xv * r * g_ref[...]).astype(o_ref.dtype)

    return pl.pallas_call(
        body, name=name, grid=(T // tm,),
        in_specs=[pl.BlockSpec((tm, D), lambda i: (i, 0)), pl.BlockSpec((1, D), lambda i: (0, 0))],
        out_specs=pl.BlockSpec((tm, D), lambda i: (i, 0)),
        out_shape=jax.ShapeDtypeStruct((T, D), _MXU), compiler_params=_cp("parallel"))(x, g)


def _rms_bwd(da, x, g, dres, name):
    T, D = x.shape
    tm = _tile(T, ROW_TILE)

    def body(da_ref, x_ref, g_ref, dr_ref, dx_ref, dg_ref):
        @pl.when(pl.program_id(0) == 0)
        def _():
            dg_ref[...] = jnp.zeros_like(dg_ref)

        xv, dav = x_ref[...], da_ref[...]
        r = lax.rsqrt(jnp.mean(xv * xv, axis=-1, keepdims=True) + EPS)
        w = dav * g_ref[...]
        c = jnp.mean(w * xv, axis=-1, keepdims=True)
        dx_ref[...] = dr_ref[...] + (r * w - xv * (r * r * r) * c)
        dg_ref[...] += jnp.sum(dav * xv * r, axis=0, keepdims=True)

    row = pl.BlockSpec((tm, D), lambda i: (i, 0))
    vec = pl.BlockSpec((1, D), lambda i: (0, 0))
    return pl.pallas_call(
        body, name=name, grid=(T // tm,), in_specs=[row, row, vec, row], out_specs=[row, vec],
        out_shape=[jax.ShapeDtypeStruct((T, D), F32), jax.ShapeDtypeStruct((1, D), F32)],
        compiler_params=_cp("arbitrary"))(da, x, g, dres)


def _loss_grad(y, tgt):
    T, D = y.shape
    tm = _tile(T, ROW_TILE)

    def body(y_ref, t_ref, dy_ref, l_ref):
        @pl.when(pl.program_id(0) == 0)
        def _():
            l_ref[...] = jnp.zeros_like(l_ref)

        d = y_ref[...] - t_ref[...]
        dy_ref[...] = d * (1.0 / D)
        per_tok = jnp.mean(d * d, axis=-1, keepdims=True)
        l_ref[...] += 0.5 * jnp.sum(per_tok, axis=0, keepdims=True)

    row = pl.BlockSpec((tm, D), lambda i: (i, 0))
    return pl.pallas_call(
        body, name="loss_grad", grid=(T // tm,), in_specs=[row, row],
        out_specs=[row, pl.BlockSpec((1, 1), lambda i: (0, 0))],
        out_shape=[jax.ShapeDtypeStruct((T, D), F32), jax.ShapeDtypeStruct((1, 1), F32)],
        compiler_params=_cp("arbitrary"))(y, tgt)


def _ffn_up(a, wg, wu, name, after=None):
    T, D = a.shape
    Fp = wg.shape[1]
    tm, tn = _tile(T, ROW_TILE), _tile(Fp, 768)
    extra = [] if after is None else [after]

    def body(*refs):
        a_ref, wg_ref, wu_ref = refs[:3]
        g_ref, u_ref, h_ref = refs[-3:]
        av = a_ref[...]
        g = _dot(av, wg_ref[...])
        u = _dot(av, wu_ref[...])
        g_ref[...] = g.astype(g_ref.dtype)
        u_ref[...] = u.astype(u_ref.dtype)
        h_ref[...] = (g * _sigmoid(g) * u).astype(h_ref.dtype)

    wspec = pl.BlockSpec((D, tn), lambda j, i: (0, j))
    ospec = pl.BlockSpec((tm, tn), lambda j, i: (i, j))
    osh = jax.ShapeDtypeStruct((T, Fp), _MXU)
    return pl.pallas_call(
        body, name=name, grid=(Fp // tn, T // tm),
        in_specs=[pl.BlockSpec((tm, D), lambda j, i: (i, 0)), wspec, wspec] + [pl.BlockSpec(memory_space=pl.ANY)] * len(extra),
        out_specs=[ospec, ospec, ospec], out_shape=[osh, osh, osh],
        compiler_params=_cp("parallel", "parallel"))(a, wg, wu, *extra)


def _ffn_bwd_hid(dxp, wd, gate, up, name, after=None):
    T, D = dxp.shape
    Fp = wd.shape[0]
    tm, tn = _tile(T, ROW_TILE), _tile(Fp, 768)
    extra = [] if after is None else [after]

    def body(*refs):
        dx_ref, wd_ref, g_ref, u_ref = refs[:4]
        dg_ref, du_ref = refs[-2:]
        dh = 0.5 * _dot(dx_ref[...], wd_ref[...], _NT)
        g = g_ref[...].astype(F32)
        u = u_ref[...].astype(F32)
        s = _sigmoid(g)
        du_ref[...] = (dh * (g * s)).astype(du_ref.dtype)
        dg_ref[...] = (dh * u * (s * (1.0 + g * (1.0 - s)))).astype(dg_ref.dtype)

    tspec = pl.BlockSpec((tm, tn), lambda j, i: (i, j))
    osh = jax.ShapeDtypeStruct((T, Fp), _MXU)
    return pl.pallas_call(
        body, name=name, grid=(Fp // tn, T // tm),
        in_specs=[pl.BlockSpec((tm, D), lambda j, i: (i, 0)), pl.BlockSpec((tn, D), lambda j, i: (j, 0)), tspec, tspec]
        + [pl.BlockSpec(memory_space=pl.ANY)] * len(extra),
        out_specs=[tspec, tspec], out_shape=[osh, osh],
        compiler_params=_cp("parallel", "parallel"))(dxp, wd, gate, up, *extra)


def _conf_fwd(zc, dw, b, lng, lnb, name):
    T = zc.shape[0]
    tm = _tile(T, ROW_TILE)
    r = tm // CONV_HALO

    def body(z_ref, zh_ref, dw_ref, b_ref, g_ref, lb_ref, u1_ref, act_ref, ext):
        i = pl.program_id(0)
        cur = z_ref[...]
        ext[pl.ds(CONV_HALO, tm), :] = cur[:, :CH] * _sigmoid(cur[:, CH:])
        hal = zh_ref[...]
        ext[pl.ds(0, CONV_HALO), :] = jnp.where(i > 0, hal[:, :CH] * _sigmoid(hal[:, CH:]), 0.0)
        acc = jnp.zeros((tm, CH), F32)
        for k in range(CONV_K):
            acc = acc + dw_ref[pl.ds(k, 1), :] * ext[pl.ds(CONV_HALO - (CONV_K - 1) + k, tm), :]
        u1 = acc + b_ref[...]
        u1_ref[...] = u1
        mu = jnp.mean(u1, axis=-1, keepdims=True)
        var = jnp.mean(jnp.square(u1 - mu), axis=-1, keepdims=True)
        u2 = (u1 - mu) * lax.rsqrt(var + EPS) * g_ref[...] + lb_ref[...]
        act_ref[...] = u2 * _sigmoid(u2)

    vec = pl.BlockSpec((1, CH), lambda i: (0, 0))
    row = pl.BlockSpec((tm, CH), lambda i: (i, 0))
    osh = jax.ShapeDtypeStruct((T, CH), F32)
    return pl.pallas_call(
        body, name=name, grid=(T // tm,),
        in_specs=[pl.BlockSpec((tm, 2 * CH), lambda i: (i, 0)),
                  pl.BlockSpec((CONV_HALO, 2 * CH), lambda i: (jnp.maximum(i * r - 1, 0), 0)),
                  pl.BlockSpec((CONV_HALO, CH), lambda i: (0, 0)), vec, vec, vec],
        out_specs=[row, row], out_shape=[osh, osh],
        scratch_shapes=[pltpu.VMEM((tm + CONV_HALO, CH), F32)],
        compiler_params=_cp("parallel"))(zc, zc, dw, b, lng, lnb)


def _conf_bwd_ln(dact, u1, lng, lnb, name):
    T = u1.shape[0]
    tm = _tile(T, ROW_TILE)

    def body(da_ref, u_ref, g_ref, lb_ref, du_ref, sm_ref):
        @pl.when(pl.program_id(0) == 0)
        def _():
            sm_ref[...] = jnp.zeros_like(sm_ref)

        u1v = u_ref[...]
        mu = jnp.mean(u1v, axis=-1, keepdims=True)
        cen = u1v - mu
        rstd = lax.rsqrt(jnp.mean(cen * cen, axis=-1, keepdims=True) + EPS)
        y = cen * rstd
        u2 = y * g_ref[...] + lb_ref[...]
        s = _sigmoid(u2)
        du2 = da_ref[...] * (s * (1.0 + u2 * (1.0 - s)))
        dy = du2 * g_ref[...]
        du1 = rstd * (dy - jnp.mean(dy, axis=-1, keepdims=True) - y * jnp.mean(dy * y, axis=-1, keepdims=True))
        du_ref[...] = du1
        sm_ref[pl.ds(0, 1), :] += jnp.sum(du2 * y, axis=0, keepdims=True)
        sm_ref[pl.ds(1, 1), :] += jnp.sum(du2, axis=0, keepdims=True)
        sm_ref[pl.ds(2, 1), :] += jnp.sum(du1, axis=0, keepdims=True)

    vec = pl.BlockSpec((1, CH), lambda i: (0, 0))
    row = pl.BlockSpec((tm, CH), lambda i: (i, 0))
    return pl.pallas_call(
        body, name=name, grid=(T // tm,), in_specs=[row, row, vec, vec],
        out_specs=[row, pl.BlockSpec((8, CH), lambda i: (0, 0))],
        out_shape=[jax.ShapeDtypeStruct((T, CH), F32), jax.ShapeDtypeStruct((8, CH), F32)],
        compiler_params=_cp("arbitrary"))(dact, u1, lng, lnb)


def _conf_bwd_conv(zc, du1, dw, name):
    T = zc.shape[0]
    tm = _tile(T, ROW_TILE)
    r = tm // CONV_HALO
    nt = T // tm
    nh = T // CONV_HALO

    def body(z_ref, zh_ref, d_ref, dn_ref, dw_ref, dz_ref, ddw_ref, ext_u, ext_d):
        i = pl.program_id(0)

        @pl.when(i == 0)
        def _():
            ddw_ref[...] = jnp.zeros_like(ddw_ref)

        cur = z_ref[...]
        ca = cur[:, :CH]
        sg = _sigmoid(cur[:, CH:])
        ext_u[pl.ds(CONV_HALO, tm), :] = ca * sg
        hal = zh_ref[...]
        ext_u[pl.ds(0, CONV_HALO), :] = jnp.where(i > 0, hal[:, :CH] * _sigmoid(hal[:, CH:]), 0.0)
        d = d_ref[...]
        ext_d[pl.ds(0, tm), :] = d
        ext_d[pl.ds(tm, CONV_HALO), :] = jnp.where(i < nt - 1, dn_ref[...], 0.0)
        acc = jnp.zeros((tm, CH), F32)
        for k in range(CONV_K):
            acc = acc + dw_ref[pl.ds(k, 1), :] * ext_d[pl.ds(CONV_K - 1 - k, tm), :]
            ddw_ref[pl.ds(k, 1), :] += jnp.sum(
                d * ext_u[pl.ds(CONV_HALO - (CONV_K - 1) + k, tm), :], axis=0, keepdims=True)
        dz_ref[:, :CH] = (acc * sg).astype(dz_ref.dtype)
        dz_ref[:, CH:] = (acc * ca * sg * (1.0 - sg)).astype(dz_ref.dtype)

    return pl.pallas_call(
        body, name=name, grid=(nt,),
        in_specs=[pl.BlockSpec((tm, 2 * CH), lambda i: (i, 0)),
                  pl.BlockSpec((CONV_HALO, 2 * CH), lambda i: (jnp.maximum(i * r - 1, 0), 0)),
                  pl.BlockSpec((tm, CH), lambda i: (i, 0)),
                  pl.BlockSpec((CONV_HALO, CH), lambda i: (jnp.minimum((i + 1) * r, nh - 1), 0)),
                  pl.BlockSpec((CONV_HALO, CH), lambda i: (0, 0))],
        out_specs=[pl.BlockSpec((tm, 2 * CH), lambda i: (i, 0)), pl.BlockSpec((CONV_HALO, CH), lambda i: (0, 0))],
        out_shape=[jax.ShapeDtypeStruct((T, 2 * CH), _MXU), jax.ShapeDtypeStruct((CONV_HALO, CH), F32)],
        scratch_shapes=[pltpu.VMEM((tm + CONV_HALO, CH), F32), pltpu.VMEM((tm + CONV_HALO, CH), F32)],
        compiler_params=_cp("arbitrary"))(zc, zc, du1, du1, dw)


def _sc_fwd(zs, w, name):
    T = zs.shape[0]
    tm = _tile(T, ROW_TILE)
    r = tm // SC_HALO

    def body(z_ref, zh_ref, w_ref, act_ref, ext):
        i = pl.program_id(0)
        cur = z_ref[...]
        ext[pl.ds(SC_HALO, tm), :] = cur[:, CH:2 * CH] * cur[:, 2 * CH:]
        hal = zh_ref[...]
        ext[pl.ds(0, SC_HALO), :] = jnp.where(i > 0, hal[:, CH:2 * CH] * hal[:, 2 * CH:], 0.0)
        v1 = jnp.zeros((tm, CH), F32)
        for k in range(SC_K):
            v1 = v1 + w_ref[pl.ds(k, 1), :] * ext[pl.ds(SC_HALO - (SC_K - 1) + k, tm), :]
        act_ref[...] = cur[:, :CH] * v1

    return pl.pallas_call(
        body, name=name, grid=(T // tm,),
        in_specs=[pl.BlockSpec((tm, 3 * CH), lambda i: (i, 0)),
                  pl.BlockSpec((SC_HALO, 3 * CH), lambda i: (jnp.maximum(i * r - 1, 0), 0)),
                  pl.BlockSpec((SC_HALO, CH), lambda i: (0, 0))],
        out_specs=pl.BlockSpec((tm, CH), lambda i: (i, 0)),
        out_shape=jax.ShapeDtypeStruct((T, CH), F32),
        scratch_shapes=[pltpu.VMEM((tm + SC_HALO, CH), F32)],
        compiler_params=_cp("parallel"))(zs, zs, w)


def _sc_bwd(zs, dact, w, name):
    T = zs.shape[0]
    tm = _tile(T, ROW_TILE)
    r = tm // SC_HALO
    nt = T // tm
    nh = T // SC_HALO

    def body(z_ref, zh_ref, zn_ref, d_ref, dn_ref, w_ref, dz_ref, dw_ref, ext_v, ext_d):
        i = pl.program_id(0)

        @pl.when(i == 0)
        def _():
            dw_ref[...] = jnp.zeros_like(dw_ref)

        cur = z_ref[...]
        sb, sc, sx = cur[:, :CH], cur[:, CH:2 * CH], cur[:, 2 * CH:]
        ext_v[pl.ds(SC_HALO, tm), :] = sc * sx
        hal = zh_ref[...]
        ext_v[pl.ds(0, SC_HALO), :] = jnp.where(i > 0, hal[:, CH:2 * CH] * hal[:, 2 * CH:], 0.0)
        da = d_ref[...]
        dv1 = da * sb
        ext_d[pl.ds(0, tm), :] = dv1
        ext_d[pl.ds(tm, SC_HALO), :] = jnp.where(i < nt - 1, dn_ref[...] * zn_ref[...][:, :CH], 0.0)
        v1 = jnp.zeros((tm, CH), F32)
        dv0 = jnp.zeros((tm, CH), F32)
        for k in range(SC_K):
            shifted = ext_v[pl.ds(SC_HALO - (SC_K - 1) + k, tm), :]
            v1 = v1 + w_ref[pl.ds(k, 1), :] * shifted
            dv0 = dv0 + w_ref[pl.ds(k, 1), :] * ext_d[pl.ds(SC_K - 1 - k, tm), :]
            dw_ref[pl.ds(k, 1), :] += jnp.sum(dv1 * shifted, axis=0, keepdims=True)
        dz_ref[:, :CH] = (da * v1).astype(dz_ref.dtype)
        dz_ref[:, CH:2 * CH] = (dv0 * sx).astype(dz_ref.dtype)
        dz_ref[:, 2 * CH:] = (dv0 * sc).astype(dz_ref.dtype)

    return pl.pallas_call(
        body, name=name, grid=(nt,),
        in_specs=[pl.BlockSpec((tm, 3 * CH), lambda i: (i, 0)),
                  pl.BlockSpec((SC_HALO, 3 * CH), lambda i: (jnp.maximum(i * r - 1, 0), 0)),
                  pl.BlockSpec((SC_HALO, 3 * CH), lambda i: (jnp.minimum((i + 1) * r, nh - 1), 0)),
                  pl.BlockSpec((tm, CH), lambda i: (i, 0)),
                  pl.BlockSpec((SC_HALO, CH), lambda i: (jnp.minimum((i + 1) * r, nh - 1), 0)),
                  pl.BlockSpec((SC_HALO, CH), lambda i: (0, 0))],
        out_specs=[pl.BlockSpec((tm, 3 * CH), lambda i: (i, 0)), pl.BlockSpec((SC_HALO, CH), lambda i: (0, 0))],
        out_shape=[jax.ShapeDtypeStruct((T, 3 * CH), _MXU), jax.ShapeDtypeStruct((SC_HALO, CH), F32)],
        scratch_shapes=[pltpu.VMEM((tm + SC_HALO, CH), F32), pltpu.VMEM((tm + SC_HALO, CH), F32)],
        compiler_params=_cp("arbitrary"))(zs, zs, zs, dact, dact, w)


SWA_TQ = 256


def _t5_bucket_np(dist):
    max_exact = N_BUCKETS // 2
    d = np.maximum(dist, 1).astype(np.float32)
    large = max_exact + (np.log(d / np.float32(max_exact)) / np.float32(math.log(MAX_DISTANCE / max_exact))
                         * np.float32(N_BUCKETS - max_exact)).astype(np.int32)
    large = np.minimum(large, N_BUCKETS - 1)
    return np.where(dist < max_exact, dist, large).astype(np.int32)


def _swa_bucket_matrix(tq):
    dist = WINDOW + np.arange(tq)[:, None] - np.arange(tq + WINDOW)[None, :]
    ok = (dist >= 0) & (dist < WINDOW)
    return np.where(ok, _t5_bucket_np(np.maximum(dist, 0)), -1).astype(np.int32)


def _kv_expand_matrix():
    e = np.zeros((2 * HEAD, 4 * HEAD), np.float32)
    for h in range(4):
        for d in range(HEAD):
            e[(h // 2) * HEAD + d, h * HEAD + d] = 1.0
    return e


def _swa_bias(rel_bias, bucket, name):
    tq, tk = bucket.shape

    def body(rb_ref, bk_ref, o_ref):
        h = pl.program_id(0)
        bk = bk_ref[...]
        acc = jnp.full((tq, tk), NEG_INF, F32)
        for b in range(N_BUCKETS):
            acc = jnp.where(bk == b, rb_ref[b, h], acc)
        o_ref[0] = acc

    return pl.pallas_call(
        body, name=name, grid=(4,),
        in_specs=[pl.BlockSpec(memory_space=pltpu.SMEM), pl.BlockSpec((tq, tk), lambda h: (0, 0))],
        out_specs=pl.BlockSpec((1, tq, tk), lambda h: (h, 0, 0)),
        out_shape=jax.ShapeDtypeStruct((4, tq, tk), F32), compiler_params=_cp("parallel"))(rel_bias, bucket)


def _swa_probs(qh, kx, bm, first_col, sk):
    s = _dot(qh, kx, _NT) * (HEAD ** -0.5)
    col = lax.broadcasted_iota(jnp.int32, s.shape, 1)
    valid = (bm > 0.5 * NEG_INF) & (col >= first_col)
    s = jnp.where(valid, s + bm, NEG_INF)
    m = jnp.maximum(jnp.max(s, axis=-1, keepdims=True), sk)
    p = jnp.exp(s - m)
    den = jnp.sum(p, axis=-1, keepdims=True) + jnp.exp(sk - m)
    return p / den, m, den


def _swa_fwd(zw, gq, gk, sink, bias, expand, name):
    T = zw.shape[0]
    tq = bias.shape[1]
    r = tq // WINDOW

    def body(z_ref, zh_ref, gq_ref, gk_ref, sink_ref, b_ref, e_ref, o_ref, kext, vext):
        i = pl.program_id(0)
        cur = z_ref[...]
        qn, _ = _head_rms(cur[:, :4 * HEAD], gq_ref[...], 4)
        kc, _ = _head_rms(cur[:, 4 * HEAD:6 * HEAD], gk_ref[...], 2)
        hal = zh_ref[...]
        kp, _ = _head_rms(hal[:, :2 * HEAD], gk_ref[...], 2)
        kext[pl.ds(0, WINDOW), :] = kp
        kext[pl.ds(WINDOW, tq), :] = kc
        vext[pl.ds(0, WINDOW), :] = hal[:, 2 * HEAD:]
        vext[pl.ds(WINDOW, tq), :] = cur[:, 6 * HEAD:]
        kx = _dot(kext[...], e_ref[...]).astype(_MXU)
        vx = _dot(vext[...], e_ref[...]).astype(_MXU)
        first_col = jnp.where(i > 0, 0, WINDOW)
        out = jnp.zeros((tq, 4 * HEAD), F32)
        for h in range(4):
            mk = _lane_mask(4 * HEAD, h)
            qh = jnp.where(mk, qn, 0.0)
            pn, _, _ = _swa_probs(qh, kx, b_ref[h], first_col, sink_ref[0, h])
            out = jnp.where(mk, _dot(pn, vx), out)
        o_ref[...] = out

    return pl.pallas_call(
        body, name=name, grid=(T // tq,),
        in_specs=[pl.BlockSpec((tq, 8 * HEAD), lambda i: (i, 0)),
                  pl.BlockSpec((WINDOW, 4 * HEAD), lambda i: (jnp.maximum(i * r - 1, 0), 1)),
                  pl.BlockSpec((1, 4 * HEAD), lambda i: (0, 0)), pl.BlockSpec((1, 2 * HEAD), lambda i: (0, 0)),
                  pl.BlockSpec(memory_space=pltpu.SMEM),
                  pl.BlockSpec(bias.shape, lambda i: (0, 0, 0)),
                  pl.BlockSpec(expand.shape, lambda i: (0, 0))],
        out_specs=pl.BlockSpec((tq, 4 * HEAD), lambda i: (i, 0)),
        out_shape=jax.ShapeDtypeStruct((T, 4 * HEAD), F32),
        scratch_shapes=[pltpu.VMEM((tq + WINDOW, 2 * HEAD), F32), pltpu.VMEM((tq + WINDOW, 2 * HEAD), F32)],
        compiler_params=_cp("parallel"))(zw, zw, gq, gk, sink, bias, expand)


def _swa_bwd(zw, dact, gq, gk, sink, bias, bucket, expand, name):
    T = zw.shape[0]
    tq = bias.shape[1]
    tk = tq + WINDOW
    r = tq // WINDOW
    nt = T // tq
    nb = T // WINDOW
    scale = HEAD ** -0.5

    def body(z_ref, zh_ref, zn_ref, d_ref, dn_ref, gq_ref, gk_ref, sink_ref, b_ref, bk_ref, e_ref,
             dz_ref, dgq_ref, dgk_ref, dsk_ref, drb_ref, kext, vext, dk_s, dv_s, db_s):
        i = pl.program_id(0)

        @pl.when(i == 0)
        def _():
            dgq_ref[...] = jnp.zeros_like(dgq_ref)
            dgk_ref[...] = jnp.zeros_like(dgk_ref)
            dsk_ref[...] = jnp.zeros_like(dsk_ref)
            drb_ref[...] = jnp.zeros_like(drb_ref)
            db_s[...] = jnp.zeros_like(db_s)

        lane = lax.broadcasted_iota(jnp.int32, (1, LANE), 1)
        cur = z_ref[...]
        q_raw, k_raw = cur[:, :4 * HEAD], cur[:, 4 * HEAD:6 * HEAD]
        qn, q_r = _head_rms(q_raw, gq_ref[...], 4)
        kc, k_r = _head_rms(k_raw, gk_ref[...], 2)
        hal = zh_ref[...]
        kp, _ = _head_rms(hal[:, :2 * HEAD], gk_ref[...], 2)
        kext[pl.ds(0, WINDOW), :] = kp
        kext[pl.ds(WINDOW, tq), :] = kc
        vext[pl.ds(0, WINDOW), :] = hal[:, 2 * HEAD:]
        vext[pl.ds(WINDOW, tq), :] = cur[:, 6 * HEAD:]
        ev = e_ref[...]
        kx = _dot(kext[...], ev).astype(_MXU)
        vx = _dot(vext[...], ev).astype(_MXU)
        first_col = jnp.where(i > 0, 0, WINDOW)
        do = d_ref[...]
        dq = jnp.zeros((tq, 4 * HEAD), F32)
        dkx = jnp.zeros((tk, 4 * HEAD), F32)
        dvx = jnp.zeros((tk, 4 * HEAD), F32)
        dsk = jnp.zeros((1, LANE), F32)
        for h in range(4):
            mk = _lane_mask(4 * HEAD, h)
            qh = jnp.where(mk, qn, 0.0).astype(_MXU)
            sk = sink_ref[0, h]
            pn, m, den = _swa_probs(qh, kx, b_ref[h], first_col, sk)
            doh = jnp.where(mk, do, 0.0).astype(_MXU)
            dpn = _dot(doh, vx, _NT)
            delta = jnp.sum(pn * dpn, axis=-1, keepdims=True)
            ds = pn * (dpn - delta)
            psink = jnp.exp(sk - m) / den
            dsk = dsk + jnp.where(lane == h, jnp.sum(-psink * delta, axis=0, keepdims=True), 0.0)
            db_s[h] += ds
            dss = (ds * scale).astype(_MXU)
            dq = dq + jnp.where(mk, _dot(dss, kx), 0.0)
            dkx = dkx + _dot(dss, qh, _TN)
            dvx = dvx + _dot(pn, doh, _TN)
        dsk_ref[...] += dsk
        dk_ext = _exact_dot(dkx, ev, _NT, "a")
        dv_ext = _exact_dot(dvx, ev, _NT, "a")
        dk_s[...] = dk_ext[WINDOW:, :]
        dv_s[...] = dv_ext[WINDOW:, :]

        @pl.when(i < nt - 1)
        def _():
            nxt = zn_ref[...]
            q2, _ = _head_rms(nxt[:, :4 * HEAD], gq_ref[...], 4)
            k2n, _ = _head_rms(nxt[:, 4 * HEAD:6 * HEAD], gk_ref[...], 2)
            k2 = jnp.concatenate([kc[tq - WINDOW:, :], k2n], axis=0)
            v2 = jnp.concatenate([cur[tq - WINDOW:, 6 * HEAD:], nxt[:, 6 * HEAD:]], axis=0)
            k2x = _dot(k2, ev).astype(_MXU)
            v2x = _dot(v2, ev).astype(_MXU)
            do2 = dn_ref[...]
            dk2x = jnp.zeros((2 * WINDOW, 4 * HEAD), F32)
            dv2x = jnp.zeros((2 * WINDOW, 4 * HEAD), F32)
            for h in range(4):
                mk = _lane_mask(4 * HEAD, h)
                qh = jnp.where(mk, q2, 0.0).astype(_MXU)
                pn, _, _ = _swa_probs(qh, k2x, b_ref[h][:WINDOW, :2 * WINDOW], 0, sink_ref[0, h])
                doh = jnp.where(mk, do2, 0.0).astype(_MXU)
                dpn = _dot(doh, v2x, _NT)
                ds = pn * (dpn - jnp.sum(pn * dpn, axis=-1, keepdims=True))
                dk2x = dk2x + _dot((ds * scale).astype(_MXU), qh, _TN)
                dv2x = dv2x + _dot(pn, doh, _TN)
            dk_s[pl.ds(tq - WINDOW, WINDOW), :] += _exact_dot(dk2x, ev, _NT, "a")[:WINDOW, :]
            dv_s[pl.ds(tq - WINDOW, WINDOW), :] += _exact_dot(dv2x, ev, _NT, "a")[:WINDOW, :]

        dq_raw, dgq = _head_rms_bwd(dq, q_raw, q_r, gq_ref[...], 4)
        dk_raw, dgk = _head_rms_bwd(dk_s[...], k_raw, k_r, gk_ref[...], 2)
        dgq_ref[...] += dgq
        dgk_ref[...] += dgk
        dz_ref[:, :4 * HEAD] = dq_raw.astype(dz_ref.dtype)
        dz_ref[:, 4 * HEAD:6 * HEAD] = dk_raw.astype(dz_ref.dtype)
        dz_ref[:, 6 * HEAD:] = dv_s[...].astype(dz_ref.dtype)

        @pl.when(i == nt - 1)
        def _():
            bk = bk_ref[...]
            for b in range(N_BUCKETS):
                rowv = jnp.zeros((1, LANE), F32)
                for h in range(4):
                    s1 = jnp.sum(jnp.where(bk == b, db_s[h], 0.0), axis=0, keepdims=True)
                    rowv = jnp.where(lane == h, jnp.sum(s1, axis=1, keepdims=True), rowv)
                drb_ref[pl.ds(b, 1), :] = rowv

    const2 = lambda i: (0, 0)
    return pl.pallas_call(
        body, name=name, grid=(nt,),
        in_specs=[pl.BlockSpec((tq, 8 * HEAD), lambda i: (i, 0)),
                  pl.BlockSpec((WINDOW, 4 * HEAD), lambda i: (jnp.maximum(i * r - 1, 0), 1)),
                  pl.BlockSpec((WINDOW, 8 * HEAD), lambda i: (jnp.minimum((i + 1) * r, nb - 1), 0)),
                  pl.BlockSpec((tq, 4 * HEAD), lambda i: (i, 0)),
                  pl.BlockSpec((WINDOW, 4 * HEAD), lambda i: (jnp.minimum((i + 1) * r, nb - 1), 0)),
                  pl.BlockSpec((1, 4 * HEAD), const2), pl.BlockSpec((1, 2 * HEAD), const2),
                  pl.BlockSpec(memory_space=pltpu.SMEM),
                  pl.BlockSpec(bias.shape, lambda i: (0, 0, 0)),
                  pl.BlockSpec(bucket.shape, const2), pl.BlockSpec(expand.shape, const2)],
        out_specs=[pl.BlockSpec((tq, 8 * HEAD), lambda i: (i, 0)),
                   pl.BlockSpec((1, 4 * HEAD), const2), pl.BlockSpec((1, 2 * HEAD), const2),
                   pl.BlockSpec((1, LANE), const2), pl.BlockSpec((N_BUCKETS, LANE), const2)],
        out_shape=[jax.ShapeDtypeStruct((T, 8 * HEAD), _MXU), jax.ShapeDtypeStruct((1, 4 * HEAD), F32),
                   jax.ShapeDtypeStruct((1, 2 * HEAD), F32), jax.ShapeDtypeStruct((1, LANE), F32),
                   jax.ShapeDtypeStruct((N_BUCKETS, LANE), F32)],
        scratch_shapes=[pltpu.VMEM((tk, 2 * HEAD), F32), pltpu.VMEM((tk, 2 * HEAD), F32),
                        pltpu.VMEM((tq, 2 * HEAD), F32), pltpu.VMEM((tq, 2 * HEAD), F32),
                        pltpu.VMEM((4, tq, tk), F32)],
        compiler_params=_cp("arbitrary"))(zw, zw, zw, dact, dact, gq, gk, sink, bias, bucket, expand)


FOX_B = 512
FOX_TM = 256


def _tri(n, lower):
    m = np.tril(np.ones((n, n), np.float32)) if lower else np.triu(np.ones((n, n), np.float32))
    return m


def _log_sigmoid(x):
    return jnp.minimum(x, 0.0) - jnp.log1p(jnp.exp(-jnp.abs(x)))


def _fox_prep(zf, gq, gk, bf, name):
    T = zf.shape[0]
    tm = _tile(T, FOX_TM)
    lower = jnp.asarray(_tri(tm, True), _MXU)

    def body(z_ref, gq_ref, gk_ref, bf_ref, l_ref, q_ref, k_ref, v_ref, f_ref, ft_ref, carry):
        @pl.when(pl.program_id(0) == 0)
        def _():
            carry[...] = jnp.zeros_like(carry)

        z = z_ref[...]
        q, _ = _head_rms(z[:, :CH], gq_ref[...], 4)
        k, _ = _head_rms(z[:, CH:2 * CH], gk_ref[...], 4)
        q_ref[...] = q.astype(q_ref.dtype)
        k_ref[...] = k.astype(k_ref.dtype)
        v_ref[...] = z[:, 2 * CH:3 * CH].astype(v_ref.dtype)
        lane = lax.broadcasted_iota(jnp.int32, (1, LANE), 1)
        lf = jnp.where(lane < 4, _log_sigmoid(z[:, 3 * CH:] + bf_ref[...]), 0.0)
        fv = _exact_dot(l_ref[...], lf, _NN, "b") + carry[pl.ds(0, 1), :]
        f_ref[...] = fv
        ft_ref[...] = fv.T
        carry[pl.ds(0, 1), :] = f_ref[pl.ds(tm - 1, 1), :]

    row = pl.BlockSpec((tm, CH), lambda i: (i, 0))
    vec = pl.BlockSpec((1, CH), lambda i: (0, 0))
    qsh = jax.ShapeDtypeStruct((T, CH), _MXU)
    return pl.pallas_call(
        body, name=name, grid=(T // tm,),
        in_specs=[pl.BlockSpec((tm, 3 * CH + LANE), lambda i: (i, 0)), vec, vec,
                  pl.BlockSpec((1, LANE), lambda i: (0, 0)), pl.BlockSpec((tm, tm), lambda i: (0, 0))],
        out_specs=[row, row, row, pl.BlockSpec((tm, LANE), lambda i: (i, 0)), pl.BlockSpec((LANE, tm), lambda i: (0, i))],
        out_shape=[qsh, qsh, qsh, jax.ShapeDtypeStruct((T, LANE), F32), jax.ShapeDtypeStruct((LANE, T), F32)],
        scratch_shapes=[pltpu.VMEM((8, LANE), F32)],
        compiler_params=_cp("arbitrary"))(zf, gq, gk, bf, lower)


def _lane_col(x, h):
    lane = lax.broadcasted_iota(jnp.int32, (1, x.shape[-1]), 1)
    return jnp.sum(jnp.where(lane == h, x, 0.0), axis=-1, keepdims=True)


def _fox_scores(qh, k, fq, ft_ref, h, qi, ki, B):
    s = _dot(qh, k, _NT) * (HEAD ** -0.5)
    s = s + (fq - ft_ref[pl.ds(h, 1), :])
    row = qi * B + lax.broadcasted_iota(jnp.int32, s.shape, 0)
    col = ki * B + lax.broadcasted_iota(jnp.int32, s.shape, 1)
    return jnp.where(col <= row, s, NEG_INF)


def _fox_fwd(q, k, v, f, ft, name):
    T = q.shape[0]
    B = _tile(T, FOX_B)
    n = T // B

    def body(q_ref, k_ref, v_ref, f_ref, ft_ref, o_ref, lse_ref, m_s, l_s, acc):
        qi, ki = pl.program_id(0), pl.program_id(1)

        @pl.when(ki == 0)
        def _():
            m_s[...] = jnp.full_like(m_s, NEG_INF)
            l_s[...] = jnp.zeros_like(l_s)
            acc[...] = jnp.zeros_like(acc)

        @pl.when(ki <= qi)
        def _():
            qv, kv, vv, fv = q_ref[...], k_ref[...], v_ref[...], f_ref[...]
            for h in range(4):
                mk = _lane_mask(CH, h)
                qh = jnp.where(mk, qv, jnp.zeros_like(qv))
                s = _fox_scores(qh, kv, _lane_col(fv, h), ft_ref, h, qi, ki, B)
                m_old = m_s[h]
                m_new = jnp.maximum(m_old, jnp.max(s, axis=-1, keepdims=True))
                alpha = jnp.exp(m_old - m_new)
                p = jnp.exp(s - m_new)
                l_s[h] = alpha * l_s[h] + jnp.sum(p, axis=-1, keepdims=True)
                m_s[h] = m_new
                acc[...] = jnp.where(mk, acc[...] * alpha + _dot(p, vv), acc[...])

        @pl.when(ki == qi)
        def _():
            lane = lax.broadcasted_iota(jnp.int32, (1, LANE), 1)
            out = acc[...]
            lse = jnp.zeros((B, LANE), F32)
            for h in range(4):
                out = jnp.where(_lane_mask(CH, h), out / l_s[h], out)
                lse = jnp.where(lane == h, m_s[h] + jnp.log(l_s[h]), lse)
            o_ref[...] = out
            lse_ref[...] = lse

    qspec = pl.BlockSpec((B, CH), lambda qi, ki: (qi, 0))
    kspec = pl.BlockSpec((B, CH), lambda qi, ki: (jnp.minimum(ki, qi), 0))
    return pl.pallas_call(
        body, name=name, grid=(n, n),
        in_specs=[qspec, kspec, kspec, pl.BlockSpec((B, LANE), lambda qi, ki: (qi, 0)),
                  pl.BlockSpec((8, B), lambda qi, ki: (0, jnp.minimum(ki, qi)))],
        out_specs=[qspec, pl.BlockSpec((B, LANE), lambda qi, ki: (qi, 0))],
        out_shape=[jax.ShapeDtypeStruct((T, CH), F32), jax.ShapeDtypeStruct((T, LANE), F32)],
        scratch_shapes=[pltpu.VMEM((4, B, 1), F32), pltpu.VMEM((4, B, 1), F32), pltpu.VMEM((B, CH), F32)],
        compiler_params=_cp("parallel", "arbitrary"))(q, k, v, f, ft)


def _fox_delta(o, do, name):
    T = o.shape[0]
    tm = _tile(T, ROW_TILE)

    def body(o_ref, d_ref, out_ref):
        prod = o_ref[...] * d_ref[...]
        lane = lax.broadcasted_iota(jnp.int32, (1, LANE), 1)
        out = jnp.zeros((tm, LANE), F32)
        for h in range(4):
            s = jnp.sum(jnp.where(_lane_mask(CH, h), prod, 0.0), axis=-1, keepdims=True)
            out = jnp.where(lane == h, s, out)
        out_ref[...] = out

    row = pl.BlockSpec((tm, CH), lambda i: (i, 0))
    return pl.pallas_call(
        body, name=name, grid=(T // tm,), in_specs=[row, row],
        out_specs=pl.BlockSpec((tm, LANE), lambda i: (i, 0)),
        out_shape=jax.ShapeDtypeStruct((T, LANE), F32), compiler_params=_cp("parallel"))(o, do)


def _fox_bwd_dq(q, k, v, f, ft, lse, delta, do, name):
    T = q.shape[0]
    B = _tile(T, FOX_B)
    n = T // B

    def body(q_ref, k_ref, v_ref, f_ref, ft_ref, lse_ref, dl_ref, do_ref, dq_ref, dfq_ref, dq_s, df_s):
        qi, ki = pl.program_id(0), pl.program_id(1)

        @pl.when(ki == 0)
        def _():
            dq_s[...] = jnp.zeros_like(dq_s)
            df_s[...] = jnp.zeros_like(df_s)

        @pl.when(ki <= qi)
        def _():
            qv, kv, vv, fv = q_ref[...], k_ref[...], v_ref[...], f_ref[...]
            lsev, dlv, dov = lse_ref[...], dl_ref[...], do_ref[...]
            lane = lax.broadcasted_iota(jnp.int32, (1, LANE), 1)
            for h in range(4):
                mk = _lane_mask(CH, h)
                qh = jnp.where(mk, qv, jnp.zeros_like(qv))
                s = _fox_scores(qh, kv, _lane_col(fv, h), ft_ref, h, qi, ki, B)
                p = jnp.exp(s - _lane_col(lsev, h))
                doh = jnp.where(mk, dov, 0.0)
                ds = p * (_dot(doh, vv, _NT) - _lane_col(dlv, h))
                dq_s[...] += jnp.where(mk, _dot(ds * (HEAD ** -0.5), kv), 0.0)
                df_s[...] += jnp.where(lane == h, jnp.sum(ds, axis=-1, keepdims=True), 0.0)

        @pl.when(ki == qi)
        def _():
            dq_ref[...] = dq_s[...]
            dfq_ref[...] = df_s[...]

    qspec = pl.BlockSpec((B, CH), lambda qi, ki: (qi, 0))
    kspec = pl.BlockSpec((B, CH), lambda qi, ki: (jnp.minimum(ki, qi), 0))
    lspec = pl.BlockSpec((B, LANE), lambda qi, ki: (qi, 0))
    return pl.pallas_call(
        body, name=name, grid=(n, n),
        in_specs=[qspec, kspec, kspec, lspec, pl.BlockSpec((8, B), lambda qi, ki: (0, jnp.minimum(ki, qi))),
                  lspec, lspec, qspec],
        out_specs=[qspec, lspec],
        out_shape=[jax.ShapeDtypeStruct((T, CH), F32), jax.ShapeDtypeStruct((T, LANE), F32)],
        scratch_shapes=[pltpu.VMEM((B, CH), F32), pltpu.VMEM((B, LANE), F32)],
        compiler_params=_cp("parallel", "arbitrary"))(q, k, v, f, ft, lse, delta, do)


def _fox_bwd_dkv(q, k, v, f, ft, lse, delta, do, name):
    T = q.shape[0]
    B = _tile(T, FOX_B)
    n = T // B

    def body(q_ref, k_ref, v_ref, f_ref, ft_ref, lse_ref, dl_ref, do_ref, dk_ref, dv_ref, dft_ref, dk_s, dv_s, df_s):
        ki, qi = pl.program_id(0), pl.program_id(1)

        @pl.when(qi == 0)
        def _():
            dk_s[...] = jnp.zeros_like(dk_s)
            dv_s[...] = jnp.zeros_like(dv_s)
            df_s[...] = jnp.zeros_like(df_s)

        @pl.when(qi >= ki)
        def _():
            qv, kv, vv, fv = q_ref[...], k_ref[...], v_ref[...], f_ref[...]
            lsev, dlv, dov = lse_ref[...], dl_ref[...], do_ref[...]
            for h in range(4):
                mk = _lane_mask(CH, h)
                qh = jnp.where(mk, qv, jnp.zeros_like(qv))
                s = _fox_scores(qh, kv, _lane_col(fv, h), ft_ref, h, qi, ki, B)
                p = jnp.exp(s - _lane_col(lsev, h))
                doh = jnp.where(mk, dov, 0.0)
                ds = p * (_dot(doh, vv, _NT) - _lane_col(dlv, h))
                dv_s[...] += _dot(p, doh, _TN)
                dk_s[...] += _dot(ds * (HEAD ** -0.5), qh, _TN)
                df_s[pl.ds(h, 1), :] -= jnp.sum(ds, axis=0, keepdims=True)

        @pl.when(qi == n - 1)
        def _():
            dk_ref[...] = dk_s[...]
            dv_ref[...] = dv_s[...]
            dft_ref[...] = jnp.zeros_like(dft_ref)
            dft_ref[pl.ds(0, 8), :] = df_s[...]

    qspec = pl.BlockSpec((B, CH), lambda ki, qi: (jnp.maximum(qi, ki), 0))
    kspec = pl.BlockSpec((B, CH), lambda ki, qi: (ki, 0))
    lspec = pl.BlockSpec((B, LANE), lambda ki, qi: (jnp.maximum(qi, ki), 0))
    return pl.pallas_call(
        body, name=name, grid=(n, n),
        in_specs=[qspec, kspec, kspec, lspec, pl.BlockSpec((8, B), lambda ki, qi: (0, ki)), lspec, lspec, qspec],
        out_specs=[kspec, kspec, pl.BlockSpec((LANE, B), lambda ki, qi: (0, ki))],
        out_shape=[jax.ShapeDtypeStruct((T, CH), F32), jax.ShapeDtypeStruct((T, CH), F32),
                   jax.ShapeDtypeStruct((LANE, T), F32)],
        scratch_shapes=[pltpu.VMEM((B, CH), F32), pltpu.VMEM((B, CH), F32), pltpu.VMEM((8, B), F32)],
        compiler_params=_cp("parallel", "arbitrary"))(q, k, v, f, ft, lse, delta, do)


def _fox_post(zf, dqn, dkn, dv, dfq, dft, gq, gk, bf, name):
    T = zf.shape[0]
    tm = _tile(T, FOX_TM)
    nt = T // tm
    upper = jnp.asarray(_tri(tm, False), _MXU)

    def body(z_ref, dq_ref, dk_ref, dv_ref, dfq_ref, dft_ref, gq_ref, gk_ref, bf_ref, u_ref, dz_ref, sm_ref, carry, rc_s):
        @pl.when(pl.program_id(0) == 0)
        def _():
            carry[...] = jnp.zeros_like(carry)
            sm_ref[...] = jnp.zeros_like(sm_ref)

        z = z_ref[...]
        q_raw, k_raw = z[:, :CH], z[:, CH:2 * CH]
        _, q_r = _head_rms(q_raw, gq_ref[...], 4)
        _, k_r = _head_rms(k_raw, gk_ref[...], 4)
        dq, dgq = _head_rms_bwd(dq_ref[...], q_raw, q_r, gq_ref[...], 4)
        dk, dgk = _head_rms_bwd(dk_ref[...], k_raw, k_r, gk_ref[...], 4)
        df = dfq_ref[...] + dft_ref[...].T
        rc_s[...] = _exact_dot(u_ref[...], df, _NN, "b") + carry[pl.ds(0, 1), :]
        carry[pl.ds(0, 1), :] = rc_s[pl.ds(0, 1), :]
        lane = lax.broadcasted_iota(jnp.int32, (1, LANE), 1)
        x = z[:, 3 * CH:] + bf_ref[...]
        dff = jnp.where(lane < 4, rc_s[...] * _sigmoid(-x), 0.0)
        dz_ref[:, :CH] = dq.astype(dz_ref.dtype)
        dz_ref[:, CH:2 * CH] = dk.astype(dz_ref.dtype)
        dz_ref[:, 2 * CH:3 * CH] = dv_ref[...].astype(dz_ref.dtype)
        dz_ref[:, 3 * CH:] = dff.astype(dz_ref.dtype)
        sm_ref[pl.ds(0, 1), :] += dgq
        sm_ref[pl.ds(1, 1), :] += dgk
        sm_ref[pl.ds(2, 1), :LANE] += jnp.sum(dff, axis=0, keepdims=True)

    rev = lambda i: (nt - 1 - i, 0)
    row = pl.BlockSpec((tm, CH), rev)
    lrow = pl.BlockSpec((tm, LANE), rev)
    vec = pl.BlockSpec((1, CH), lambda i: (0, 0))
    return pl.pallas_call(
        body, name=name, grid=(nt,),
        in_specs=[pl.BlockSpec((tm, 3 * CH + LANE), rev), row, row, row, lrow,
                  pl.BlockSpec((LANE, tm), lambda i: (0, nt - 1 - i)), vec, vec,
                  pl.BlockSpec((1, LANE), lambda i: (0, 0)), pl.BlockSpec((tm, tm), lambda i: (0, 0))],
        out_specs=[pl.BlockSpec((tm, 3 * CH + LANE), rev), pl.BlockSpec((8, CH), lambda i: (0, 0))],
        out_shape=[jax.ShapeDtypeStruct((T, 3 * CH + LANE), _MXU), jax.ShapeDtypeStruct((8, CH), F32)],
        scratch_shapes=[pltpu.VMEM((8, LANE), F32), pltpu.VMEM((tm, LANE), F32)],
        compiler_params=_cp("arbitrary"))(zf, dqn, dkn, dv, dfq, dft, gq, gk, bf, upper)


AUG_F, AUG_ONE, AUG_LSE = HEAD, HEAD + 3, HEAD + 6


def _pieces(x):
    hi = x.astype(_MXU).astype(F32)
    r1 = x - hi
    mid = r1.astype(_MXU).astype(F32)
    lo = (r1 - mid).astype(_MXU).astype(F32)
    return hi, mid, lo


def _put_pieces(base, first_lane, x, sign):
    lane = lax.broadcasted_iota(jnp.int32, (1, LANE), 1)
    for j, piece in enumerate(_pieces(x)):
        base = jnp.where(lane == first_lane + j, sign * piece, base)
    return base


def _head_select_matrix():
    p = np.zeros((4, 4 * HEAD, LANE), np.float32)
    for h in range(4):
        for d in range(HEAD):
            p[h, h * HEAD + d, d] = 1.0
    return p


def _tri_steps(n, by_key):
    if by_key:
        pairs = [(q, k) for k in range(n) for q in range(k, n)]
    else:
        pairs = [(q, k) for q in range(n) for k in range(q + 1)]
    return (jnp.asarray([p[0] for p in pairs], jnp.int32), jnp.asarray([p[1] for p in pairs], jnp.int32))


def _fox2_prep(zf, gq, gk, bf, sel, name):
    T = zf.shape[0]
    tm = _tile(T, FOX_TM)
    lower = jnp.asarray(_tri(tm, True), _MXU)

    def body(z_ref, gq_ref, gk_ref, bf_ref, l_ref, p_ref, qa_ref, ka_ref, va_ref, vat_ref, carry, f_s):
        @pl.when(pl.program_id(0) == 0)
        def _():
            carry[...] = jnp.zeros_like(carry)

        z = z_ref[...]
        q, _ = _head_rms(z[:, :CH], gq_ref[...], 4)
        k, _ = _head_rms(z[:, CH:2 * CH], gk_ref[...], 4)
        q = (q * (HEAD ** -0.5)).astype(_MXU)
        k = k.astype(_MXU)
        v = z[:, 2 * CH:3 * CH].astype(_MXU)
        lane = lax.broadcasted_iota(jnp.int32, (1, LANE), 1)
        lf = jnp.where(lane < 4, _log_sigmoid(z[:, 3 * CH:] + bf_ref[...]), 0.0)
        f_s[...] = _exact_dot(l_ref[...], lf, _NN, "b") + carry[pl.ds(0, 1), :]
        carry[pl.ds(0, 1), :] = f_s[pl.ds(tm - 1, 1), :]
        fv = f_s[...]
        q_ones = (lane >= AUG_ONE) & (lane < AUG_ONE + 3)
        k_ones = ((lane >= AUG_F) & (lane < AUG_F + 3)) | ((lane >= AUG_LSE) & (lane < AUG_LSE + 3))
        v_ones = (lane >= AUG_F) & (lane < AUG_F + 3)
        for h in range(4):
            fh = _lane_col(fv, h)
            qa = jnp.where(q_ones, 1.0, _dot(q, p_ref[h]))
            qa_ref[h] = _put_pieces(qa, AUG_F, fh, 1.0).astype(qa_ref.dtype)
            ka = jnp.where(k_ones, 1.0, _dot(k, p_ref[h]))
            ka_ref[h] = _put_pieces(ka, AUG_ONE, fh, -1.0).astype(ka_ref.dtype)
            va = jnp.where(v_ones, 1.0, _dot(v, p_ref[h]))
            va_ref[h] = va.astype(va_ref.dtype)
            vat_ref[h] = va.T.astype(vat_ref.dtype)

    vec = pl.BlockSpec((1, CH), lambda i: (0, 0))
    hspec = pl.BlockSpec((4, tm, LANE), lambda i: (0, i, 0))
    hsh = jax.ShapeDtypeStruct((4, T, LANE), _MXU)
    return pl.pallas_call(
        body, name=name, grid=(T // tm,),
        in_specs=[pl.BlockSpec((tm, 3 * CH + LANE), lambda i: (i, 0)), vec, vec,
                  pl.BlockSpec((1, LANE), lambda i: (0, 0)), pl.BlockSpec((tm, tm), lambda i: (0, 0)),
                  pl.BlockSpec(sel.shape, lambda i: (0, 0, 0))],
        out_specs=[hspec, hspec, hspec, pl.BlockSpec((4, LANE, tm), lambda i: (0, 0, i))],
        out_shape=[hsh, hsh, hsh, jax.ShapeDtypeStruct((4, LANE, T), _MXU)],
        scratch_shapes=[pltpu.VMEM((8, LANE), F32), pltpu.VMEM((tm, LANE), F32)],
        compiler_params=_cp("arbitrary"))(zf, gq, gk, bf, lower, sel)


def _causal(s, transposed):
    row = lax.broadcasted_iota(jnp.int32, s.shape, 0)
    col = lax.broadcasted_iota(jnp.int32, s.shape, 1)
    return jnp.where((row <= col) if transposed else (col <= row), s, NEG_INF)


def _mxu_dot(a, b, dims):
    return lax.dot_general(a, b, dims, preferred_element_type=F32)


def _fox2_fwd(qa, ka, vat, sel, name):
    T = qa.shape[1]
    B = _tile(T, FOX_B)
    n = T // B
    qt, kt = _tri_steps(n, False)

    def body(qt_ref, kt_ref, qa_ref, ka_ref, vat_ref, p_ref, o_ref, qb_ref, m_s, acc):
        step = pl.program_id(0)
        qi, ki = qt_ref[step], kt_ref[step]

        @pl.when(ki == 0)
        def _():
            m_s[...] = jnp.full_like(m_s, NEG_INF)
            acc[...] = jnp.zeros_like(acc)

        def update(diag):
            for h in range(4):
                st = _mxu_dot(ka_ref[h], qa_ref[h], _NT)
                if diag:
                    st = _causal(st, True)
                m_old = m_s[h, pl.ds(0, 1), :]
                m_new = jnp.maximum(m_old, jnp.max(st, axis=0, keepdims=True))
                pt = jnp.exp(st - m_new)
                acc[h] = acc[h] * jnp.exp(m_old - m_new) + _dot(vat_ref[h], pt)
                m_s[h, pl.ds(0, 1), :] = m_new

        @pl.when(ki < qi)
        def _():
            update(False)

        @pl.when(ki == qi)
        def _():
            update(True)
            row = lax.broadcasted_iota(jnp.int32, (LANE, 1), 0)
            out = jnp.zeros((B, CH), F32)
            for h in range(4):
                a = acc[h]
                l = a[AUG_F:AUG_F + 1, :]
                out = out + _exact_dot((a / l).T, p_ref[h], _NT, "a")
                lse = m_s[h, pl.ds(0, 1), :] + jnp.log(l)
                qbt = qa_ref[h].astype(F32).T
                for j, piece in enumerate(_pieces(lse)):
                    qbt = jnp.where(row == AUG_LSE + j, -piece, qbt)
                qb_ref[h] = qbt.T.astype(qb_ref.dtype)
            o_ref[...] = out

    qspec = pl.BlockSpec((4, B, LANE), lambda s, qt, kt: (0, qt[s], 0))
    kspec = pl.BlockSpec((4, B, LANE), lambda s, qt, kt: (0, kt[s], 0))
    grid_spec = pltpu.PrefetchScalarGridSpec(
        num_scalar_prefetch=2, grid=(qt.shape[0],),
        in_specs=[qspec, kspec, pl.BlockSpec((4, LANE, B), lambda s, qt, kt: (0, 0, kt[s])),
                  pl.BlockSpec(sel.shape, lambda s, qt, kt: (0, 0, 0))],
        out_specs=[pl.BlockSpec((B, CH), lambda s, qt, kt: (qt[s], 0)), qspec],
        scratch_shapes=[pltpu.VMEM((4, 8, B), F32), pltpu.VMEM((4, LANE, B), F32)])
    return pl.pallas_call(
        body, name=name, grid_spec=grid_spec,
        out_shape=[jax.ShapeDtypeStruct((T, CH), F32), jax.ShapeDtypeStruct((4, T, LANE), _MXU)],
        compiler_params=_cp("arbitrary"))(qt, kt, qa, ka, vat, sel)


def _fox2_bwd_prep(o, do, sel, name):
    T = o.shape[0]
    tm = _tile(T, ROW_TILE)

    def body(o_ref, d_ref, p_ref, out_ref):
        dov = d_ref[...]
        prod = o_ref[...] * dov
        dob = dov.astype(_MXU)
        for h in range(4):
            delta = jnp.sum(jnp.where(_lane_mask(CH, h), prod, 0.0), axis=-1, keepdims=True)
            out_ref[h] = _put_pieces(_dot(dob, p_ref[h]), AUG_F, delta, -1.0).astype(out_ref.dtype)

    row = pl.BlockSpec((tm, CH), lambda i: (i, 0))
    return pl.pallas_call(
        body, name=name, grid=(T // tm,),
        in_specs=[row, row, pl.BlockSpec(sel.shape, lambda i: (0, 0, 0))],
        out_specs=pl.BlockSpec((4, tm, LANE), lambda i: (0, i, 0)),
        out_shape=jax.ShapeDtypeStruct((4, T, LANE), _MXU), compiler_params=_cp("parallel"))(o, do, sel)


def _fox2_bwd_dq(qb, ka, va, doa, sel, name):
    T = qb.shape[1]
    B = _tile(T, FOX_B)
    n = T // B
    qt, kt = _tri_steps(n, False)

    def body(qt_ref, kt_ref, qb_ref, ka_ref, va_ref, do_ref, p_ref, dq_ref, dfq_ref, dq_s):
        step = pl.program_id(0)
        qi, ki = qt_ref[step], kt_ref[step]

        @pl.when(ki == 0)
        def _():
            dq_s[...] = jnp.zeros_like(dq_s)

        def update(diag):
            for h in range(4):
                s = _mxu_dot(qb_ref[h], ka_ref[h], _NT)
                if diag:
                    s = _causal(s, False)
                ds = jnp.exp(s) * _mxu_dot(do_ref[h], va_ref[h], _NT)
                dq_s[h] += _dot(ds, ka_ref[h])

        @pl.when(ki < qi)
        def _():
            update(False)

        @pl.when(ki == qi)
        def _():
            update(True)
            lane = lax.broadcasted_iota(jnp.int32, (1, LANE), 1)
            out = jnp.zeros((B, CH), F32)
            dfq = jnp.zeros((B, LANE), F32)
            for h in range(4):
                out = out + _exact_dot(dq_s[h] * (HEAD ** -0.5), p_ref[h], _NT, "a")
                dfq = jnp.where(lane == h, _lane_col(dq_s[h], AUG_F), dfq)
            dq_ref[...] = out
            dfq_ref[...] = dfq

    qspec = pl.BlockSpec((4, B, LANE), lambda s, qt, kt: (0, qt[s], 0))
    kspec = pl.BlockSpec((4, B, LANE), lambda s, qt, kt: (0, kt[s], 0))
    grid_spec = pltpu.PrefetchScalarGridSpec(
        num_scalar_prefetch=2, grid=(qt.shape[0],),
        in_specs=[qspec, kspec, kspec, qspec, pl.BlockSpec(sel.shape, lambda s, qt, kt: (0, 0, 0))],
        out_specs=[pl.BlockSpec((B, CH), lambda s, qt, kt: (qt[s], 0)),
                   pl.BlockSpec((B, LANE), lambda s, qt, kt: (qt[s], 0))],
        scratch_shapes=[pltpu.VMEM((4, B, LANE), F32)])
    return pl.pallas_call(
        body, name=name, grid_spec=grid_spec,
        out_shape=[jax.ShapeDtypeStruct((T, CH), F32), jax.ShapeDtypeStruct((T, LANE), F32)],
        compiler_params=_cp("arbitrary"))(qt, kt, qb, ka, va, doa, sel)


def _fox2_bwd_dkv(qb, ka, va, doa, sel, name):
    T = qb.shape[1]
    B = _tile(T, FOX_B)
    n = T // B
    qt, kt = _tri_steps(n, True)

    def body(qt_ref, kt_ref, qb_ref, ka_ref, va_ref, do_ref, p_ref, dk_ref, dv_ref, df_ref, dk_s, dv_s):
        step = pl.program_id(0)
        qi, ki = qt_ref[step], kt_ref[step]

        @pl.when(qi == ki)
        def _():
            dk_s[...] = jnp.zeros_like(dk_s)
            dv_s[...] = jnp.zeros_like(dv_s)

        def update(diag):
            for h in range(4):
                st = _mxu_dot(ka_ref[h], qb_ref[h], _NT)
                if diag:
                    st = _causal(st, True)
                pt = jnp.exp(st)
                dst = pt * _mxu_dot(va_ref[h], do_ref[h], _NT)
                dv_s[h] += _dot(pt, do_ref[h])
                dk_s[h] += _dot(dst, qb_ref[h])

        @pl.when(qi == ki)
        def _():
            update(True)

        @pl.when(qi > ki)
        def _():
            update(False)

        @pl.when(qi == n - 1)
        def _():
            lane = lax.broadcasted_iota(jnp.int32, (1, LANE), 1)
            dk = jnp.zeros((B, CH), F32)
            dv = jnp.zeros((B, CH), F32)
            dfk = jnp.zeros((B, LANE), F32)
            for h in range(4):
                dk = dk + _exact_dot(dk_s[h], p_ref[h], _NT, "a")
                dv = dv + _exact_dot(dv_s[h], p_ref[h], _NT, "a")
                dfk = jnp.where(lane == h, -_lane_col(dk_s[h], AUG_ONE), dfk)
            dk_ref[...] = dk
            dv_ref[...] = dv
            df_ref[...] = dfk

    qspec = pl.BlockSpec((4, B, LANE), lambda s, qt, kt: (0, qt[s], 0))
    kspec = pl.BlockSpec((4, B, LANE), lambda s, qt, kt: (0, kt[s], 0))
    ospec = pl.BlockSpec((B, CH), lambda s, qt, kt: (kt[s], 0))
    grid_spec = pltpu.PrefetchScalarGridSpec(
        num_scalar_prefetch=2, grid=(qt.shape[0],),
        in_specs=[qspec, kspec, kspec, qspec, pl.BlockSpec(sel.shape, lambda s, qt, kt: (0, 0, 0))],
        out_specs=[ospec, ospec, pl.BlockSpec((B, LANE), lambda s, qt, kt: (kt[s], 0))],
        scratch_shapes=[pltpu.VMEM((4, B, LANE), F32), pltpu.VMEM((4, B, LANE), F32)])
    return pl.pallas_call(
        body, name=name, grid_spec=grid_spec,
        out_shape=[jax.ShapeDtypeStruct((T, CH), F32), jax.ShapeDtypeStruct((T, CH), F32),
                   jax.ShapeDtypeStruct((T, LANE), F32)],
        compiler_params=_cp("arbitrary"))(qt, kt, qb, ka, va, doa, sel)


def _fox2_bwd(qb, ka, va, doa, sel, name):
    T = qb.shape[1]
    B = _tile(T, FOX_B)
    n = T // B
    qt, kt = _tri_steps(n, True)
    nsteps = qt.shape[0]

    def body(qt_ref, kt_ref, qb_ref, ka_ref, va_ref, do_ref, p_ref, dk_ref, dv_ref, df_ref, dq_hbm,
             dk_s, dv_s, kat_s, dq_s, sem):
        step = pl.program_id(0)
        qi, ki = qt_ref[step], kt_ref[step]

        @pl.when(step == 0)
        def _():
            dq_s[...] = jnp.zeros_like(dq_s)

        @pl.when(qi == ki)
        def _():
            dk_s[...] = jnp.zeros_like(dk_s)
            dv_s[...] = jnp.zeros_like(dv_s)
            for h in range(4):
                kat_s[h] = ka_ref[h].astype(F32).T.astype(kat_s.dtype)

        def update(diag):
            for h in range(4):
                st = _mxu_dot(ka_ref[h], qb_ref[h], _NT)
                if diag:
                    st = _causal(st, True)
                pt = jnp.exp(st)
                dst = (pt * _mxu_dot(va_ref[h], do_ref[h], _NT)).astype(_MXU)
                dv_s[h] += _dot(pt, do_ref[h])
                dk_s[h] += _mxu_dot(dst, qb_ref[h], _NN)
                dq_s[qi, h] += _mxu_dot(kat_s[h], dst, _NN)

        @pl.when(qi == ki)
        def _():
            update(True)

        @pl.when(qi > ki)
        def _():
            update(False)

        @pl.when(qi == n - 1)
        def _():
            lane = lax.broadcasted_iota(jnp.int32, (1, LANE), 1)
            dk = jnp.zeros((B, CH), F32)
            dv = jnp.zeros((B, CH), F32)
            dfk = jnp.zeros((B, LANE), F32)
            for h in range(4):
                dk = dk + _exact_dot(dk_s[h], p_ref[h], _NT, "a")
                dv = dv + _exact_dot(dv_s[h], p_ref[h], _NT, "a")
                dfk = jnp.where(lane == h, -_lane_col(dk_s[h], AUG_ONE), dfk)
            dk_ref[...] = dk
            dv_ref[...] = dv
            df_ref[...] = dfk

        @pl.when(step == nsteps - 1)
        def _():
            cp = pltpu.make_async_copy(dq_s, dq_hbm, sem)
            cp.start()
            cp.wait()

    qspec = pl.BlockSpec((4, B, LANE), lambda s, qt, kt: (0, qt[s], 0))
    kspec = pl.BlockSpec((4, B, LANE), lambda s, qt, kt: (0, kt[s], 0))
    ospec = pl.BlockSpec((B, CH), lambda s, qt, kt: (kt[s], 0))
    grid_spec = pltpu.PrefetchScalarGridSpec(
        num_scalar_prefetch=2, grid=(nsteps,),
        in_specs=[qspec, kspec, kspec, qspec, pl.BlockSpec(sel.shape, lambda s, qt, kt: (0, 0, 0))],
        out_specs=[ospec, ospec, pl.BlockSpec((B, LANE), lambda s, qt, kt: (kt[s], 0)),
                   pl.BlockSpec(memory_space=pl.ANY)],
        scratch_shapes=[pltpu.VMEM((4, B, LANE), F32), pltpu.VMEM((4, B, LANE), F32), pltpu.VMEM((4, LANE, B), _MXU),
                        pltpu.VMEM((n, 4, LANE, B), F32), pltpu.SemaphoreType.DMA])
    return pl.pallas_call(
        body, name=name, grid_spec=grid_spec,
        out_shape=[jax.ShapeDtypeStruct((T, CH), F32), jax.ShapeDtypeStruct((T, CH), F32),
                   jax.ShapeDtypeStruct((T, LANE), F32), jax.ShapeDtypeStruct((n, 4, LANE, B), F32)],
        compiler_params=_cp("arbitrary"))(qt, kt, qb, ka, va, doa, sel)


def _fox2_post(zf, dqt, dkn, dv, dfk, sel, gq, gk, bf, name):
    T = zf.shape[0]
    tm = _tile(T, FOX_TM)
    nt = T // tm
    B = dqt.shape[3]
    per = B // tm
    upper = jnp.asarray(_tri(tm, False), _MXU)

    def body(z_ref, dqt_ref, dk_ref, dv_ref, df_ref, p_ref, gq_ref, gk_ref, bf_ref, u_ref, dz_ref, sm_ref, carry, rc_s):
        @pl.when(pl.program_id(0) == 0)
        def _():
            carry[...] = jnp.zeros_like(carry)
            sm_ref[...] = jnp.zeros_like(sm_ref)

        lane = lax.broadcasted_iota(jnp.int32, (1, LANE), 1)
        dqn = jnp.zeros((tm, CH), F32)
        dfq = jnp.zeros((tm, LANE), F32)
        for h in range(4):
            blk = dqt_ref[0, h].T
            dqn = dqn + _exact_dot(blk * (HEAD ** -0.5), p_ref[h], _NT, "a")
            dfq = jnp.where(lane == h, _lane_col(blk, AUG_F), dfq)
        z = z_ref[...]
        q_raw, k_raw = z[:, :CH], z[:, CH:2 * CH]
        _, q_r = _head_rms(q_raw, gq_ref[...], 4)
        _, k_r = _head_rms(k_raw, gk_ref[...], 4)
        dq, dgq = _head_rms_bwd(dqn, q_raw, q_r, gq_ref[...], 4)
        dk, dgk = _head_rms_bwd(dk_ref[...], k_raw, k_r, gk_ref[...], 4)
        rc_s[...] = _exact_dot(u_ref[...], dfq + df_ref[...], _NN, "b") + carry[pl.ds(0, 1), :]
        carry[pl.ds(0, 1), :] = rc_s[pl.ds(0, 1), :]
        x = z[:, 3 * CH:] + bf_ref[...]
        dff = jnp.where(lane < 4, rc_s[...] * _sigmoid(-x), 0.0)
        dz_ref[:, :CH] = dq.astype(dz_ref.dtype)
        dz_ref[:, CH:2 * CH] = dk.astype(dz_ref.dtype)
        dz_ref[:, 2 * CH:3 * CH] = dv_ref[...].astype(dz_ref.dtype)
        dz_ref[:, 3 * CH:] = dff.astype(dz_ref.dtype)
        sm_ref[pl.ds(0, 1), :] += dgq
        sm_ref[pl.ds(1, 1), :] += dgk
        sm_ref[pl.ds(2, 1), :LANE] += jnp.sum(dff, axis=0, keepdims=True)

    rev = lambda i: (nt - 1 - i, 0)
    row = pl.BlockSpec((tm, CH), rev)
    lrow = pl.BlockSpec((tm, LANE), rev)
    vec = pl.BlockSpec((1, CH), lambda i: (0, 0))
    return pl.pallas_call(
        body, name=name, grid=(nt,),
        in_specs=[pl.BlockSpec((tm, 3 * CH + LANE), rev),
                  pl.BlockSpec((1, 4, LANE, tm), lambda i: ((nt - 1 - i) // per, 0, 0, (nt - 1 - i) % per)),
                  row, row, lrow, pl.BlockSpec(sel.shape, lambda i: (0, 0, 0)), vec, vec,
                  pl.BlockSpec((1, LANE), lambda i: (0, 0)), pl.BlockSpec((tm, tm), lambda i: (0, 0))],
        out_specs=[pl.BlockSpec((tm, 3 * CH + LANE), rev), pl.BlockSpec((8, CH), lambda i: (0, 0))],
        out_shape=[jax.ShapeDtypeStruct((T, 3 * CH + LANE), _MXU), jax.ShapeDtypeStruct((8, CH), F32)],
        scratch_shapes=[pltpu.VMEM((8, LANE), F32), pltpu.VMEM((tm, LANE), F32)],
        compiler_params=_cp("arbitrary"))(zf, dqt, dkn, dv, dfk, sel, gq, gk, bf, upper)


def _merge_fwd(acts, zg, wbr, wout, x1, name):
    T, D = x1.shape
    tm = _tile(T, 256)

    def body(a0, a1, a2, a3, zg_ref, wbr_ref, wout_ref, x_ref, o_ref, mg_ref):
        merged = None
        for i, a_ref in enumerate((a0, a1, a2, a3)):
            term = _sigmoid(zg_ref[:, i * D:(i + 1) * D]) * _dot(a_ref[...], wbr_ref[i])
            merged = term if merged is None else merged + term
        mg_ref[...] = merged.astype(mg_ref.dtype)
        o_ref[...] = x_ref[...] + _dot(merged, wout_ref[...])

    arow = pl.BlockSpec((tm, CH), lambda i: (i, 0))
    xrow = pl.BlockSpec((tm, D), lambda i: (i, 0))
    return pl.pallas_call(
        body, name=name, grid=(T // tm,),
        in_specs=[arow, arow, arow, arow, pl.BlockSpec((tm, 4 * D), lambda i: (i, 0)),
                  pl.BlockSpec((4, CH, D), lambda i: (0, 0, 0)), pl.BlockSpec((D, D), lambda i: (0, 0)), xrow],
        out_specs=[xrow, xrow],
        out_shape=[jax.ShapeDtypeStruct((T, D), F32), jax.ShapeDtypeStruct((T, D), _MXU)],
        compiler_params=_cp("parallel"))(*acts, zg, wbr, wout, x1)


def _merge_bwd(dx2, acts, zg, wbr, wout, name):
    T, D = dx2.shape
    tm = _tile(T, 256)
    nt = T // tm

    def body(dx_ref, a0, a1, a2, a3, zg_ref, wbr_ref, wout_ref, d0, d1, d2, d3, dzg_ref, dw_ref, dw_s):
        i = pl.program_id(0)

        @pl.when(i == 0)
        def _():
            dw_s[...] = jnp.zeros_like(dw_s)

        dm = _dot(dx_ref[...], wout_ref[...], _NT)
        for b, (a_ref, d_ref) in enumerate(((a0, d0), (a1, d1), (a2, d2), (a3, d3))):
            av = a_ref[...].astype(_MXU)
            g = _sigmoid(zg_ref[:, b * D:(b + 1) * D])
            p = _dot(av, wbr_ref[b])
            dzg_ref[:, b * D:(b + 1) * D] = (dm * p * (g * (1.0 - g))).astype(dzg_ref.dtype)
            dp = (dm * g).astype(_MXU)
            d_ref[...] = _dot(dp, wbr_ref[b], _NT)
            dw_s[b] += _dot(av, dp, _TN)

        @pl.when(i == nt - 1)
        def _():
            dw_ref[...] = dw_s[...].astype(dw_ref.dtype)

    arow = pl.BlockSpec((tm, CH), lambda i: (i, 0))
    xrow = pl.BlockSpec((tm, D), lambda i: (i, 0))
    grow = pl.BlockSpec((tm, 4 * D), lambda i: (i, 0))
    wspec = pl.BlockSpec((4, CH, D), lambda i: (0, 0, 0))
    ash = jax.ShapeDtypeStruct((T, CH), F32)
    return pl.pallas_call(
        body, name=name, grid=(nt,),
        in_specs=[xrow, arow, arow, arow, arow, grow, wspec, pl.BlockSpec((D, D), lambda i: (0, 0))],
        out_specs=[arow, arow, arow, arow, grow, wspec],
        out_shape=[ash, ash, ash, ash, jax.ShapeDtypeStruct((T, 4 * D), _MXU), jax.ShapeDtypeStruct((4, CH, D), _MXU)],
        scratch_shapes=[pltpu.VMEM((4, CH, D), F32)],
        compiler_params=_cp("arbitrary"))(dx2, *acts, zg, wbr, wout)


def _rows_2d(a):
    return a.reshape((-1, a.shape[-1])) if a.ndim > 1 else a.reshape((1, -1))


def _row_tile(rows, cols, n_bufs):
    padded = -(-cols // LANE) * LANE
    cap = max(8, (VMEM_LIMIT // 3) // (2 * n_bufs * 4 * padded))
    return _tile(rows, cap, 8)


def _sum8(recv, name):
    shape = recv.shape[1:]
    r2 = recv.reshape((N_DEV, -1, shape[-1]))
    rows, cols = r2.shape[1:]
    tr = _row_tile(rows, cols, N_DEV // 2 + 1)

    def body(r_ref, o_ref):
        acc = r_ref[0].astype(F32)
        for d in range(1, N_DEV):
            acc = acc + r_ref[d].astype(F32)
        o_ref[...] = acc

    out = pl.pallas_call(
        body, name=name, grid=(rows // tr,),
        in_specs=[pl.BlockSpec((N_DEV, tr, cols), lambda i: (0, i, 0))],
        out_specs=pl.BlockSpec((tr, cols), lambda i: (i, 0)),
        out_shape=jax.ShapeDtypeStruct((rows, cols), F32), compiler_params=_cp("parallel"))(r2)
    return out.reshape(shape)


def _adamw(w, g, m, v, name):
    slabs = w.ndim == 3 and w.shape[1] < 8
    if slabs:
        lead, rows, cols = None, w.shape[0], w.shape[1] * w.shape[2]
    elif w.ndim == 3:
        lead, rows, cols = w.shape
    else:
        lead, (rows, cols) = None, w.shape
    tr = _row_tile(rows, cols, 7) if not slabs else max(t for t in range(1, 513) if rows % t == 0)

    def body(w_ref, g_ref, m_ref, v_ref, d_ref, nm_ref, nv_ref):
        gv = g_ref[...]
        nm = ADAM_B1 * m_ref[...] + (1.0 - ADAM_B1) * gv
        nv = ADAM_B2 * v_ref[...] + (1.0 - ADAM_B2) * jnp.square(gv)
        m_hat = nm / (1.0 - ADAM_B1 ** ADAM_STEP)
        v_hat = nv / (1.0 - ADAM_B2 ** ADAM_STEP)
        d_ref[...] = -ADAM_LR * (m_hat / (jnp.sqrt(v_hat) + ADAM_EPS) + ADAM_WD * w_ref[...])
        nm_ref[...] = nm
        nv_ref[...] = nv

    if slabs:
        grid, sem = (rows // tr,), ("parallel",)
        spec = pl.BlockSpec((tr,) + w.shape[1:], lambda i: (i, 0, 0))
    elif lead is None:
        grid, sem = (rows // tr,), ("parallel",)
        spec = pl.BlockSpec((tr, cols), lambda i: (i, 0))
    else:
        grid, sem = (lead, rows // tr), ("parallel", "parallel")
        spec = pl.BlockSpec((None, tr, cols), lambda l, i: (l, i, 0))
    osh = jax.ShapeDtypeStruct(w.shape, F32)
    return tuple(pl.pallas_call(
        body, name=name, grid=grid, in_specs=[spec] * 4, out_specs=[spec] * 3,
        out_shape=[osh] * 3, compiler_params=_cp(*sem))(w, g, m, v))


def _exchange(items, name):
    n = len(items)
    widths, out_shapes = [], []
    for src, kind, ax in items:
        if kind == "gather":
            w = src.shape[ax]
            shp = list(src.shape)
            shp[ax] = N_DEV * w
        else:
            w = src.shape[ax] // N_DEV
            shp = list(src.shape)
            shp[ax] = w
            shp = [N_DEV] + shp
        widths.append(w)
        out_shapes.append(jax.ShapeDtypeStruct(tuple(shp), src.dtype))

    def body(*refs):
        srcs, outs = refs[:n], refs[n:2 * n]
        send, recv, lsem = refs[2 * n:]
        x, y, c = lax.axis_index("x"), lax.axis_index("y"), lax.axis_index("c")
        me = 4 * x + 2 * y + c

        def peer(k):
            b = k + 1
            px = 1 - x if b & 4 else x
            py = 1 - y if b & 2 else y
            pc = 1 - c if b & 1 else c
            return (px, py, pc), 4 * px + 2 * py + pc

        def win(ref, ax, idx, w):
            return ref.at[tuple([slice(None)] * ax + [pl.ds(idx * w, w)])]

        def ends(j, mine, theirs):
            _, kind, ax = items[j]
            if kind == "gather":
                return srcs[j], win(outs[j], ax, mine, widths[j])
            return win(srcs[j], ax, theirs, widths[j]), outs[j].at[mine]

        local, sent = [], []
        for j in range(n):
            s, d = ends(j, me, me)
            cp = pltpu.make_async_copy(s, d, lsem.at[j])
            cp.start()
            local.append(cp)
            for k in range(N_DEV - 1):
                dev, pid = peer(k)
                s, d = ends(j, me, pid)
                cp = pltpu.make_async_remote_copy(s, d, send.at[j, k], recv.at[j, k], device_id=dev,
                                                  device_id_type=pl.DeviceIdType.MESH)
                cp.start()
                sent.append(cp)
        for j in range(n):
            for k in range(N_DEV - 1):
                dev, pid = peer(k)
                s, d = ends(j, pid, me)
                pltpu.make_async_remote_copy(s, d, send.at[j, k], recv.at[j, k], device_id=dev,
                                             device_id_type=pl.DeviceIdType.MESH).wait_recv()
        for cp in sent:
            cp.wait_send()
        for cp in local:
            cp.wait()

    hbm = pl.BlockSpec(memory_space=pl.ANY)
    return pl.pallas_call(
        body, name=name, in_specs=[hbm] * n, out_specs=[hbm] * n, out_shape=out_shapes,
        scratch_shapes=[pltpu.SemaphoreType.DMA((n, N_DEV - 1)), pltpu.SemaphoreType.DMA((n, N_DEV - 1)),
                        pltpu.SemaphoreType.DMA((n,))],
        compiler_params=pltpu.CompilerParams(has_side_effects=True))(*[it[0] for it in items])


def _exchange_plan(items):
    widths, out_shapes = [], []
    for src, kind, ax in items:
        shp = list(src.shape)
        if kind == "gather":
            w = src.shape[ax]
            shp[ax] = N_DEV * w
        else:
            w = src.shape[ax] // N_DEV
            shp[ax] = w
            shp = [N_DEV] + shp
        widths.append(w)
        out_shapes.append((tuple(shp), src.dtype))
    return widths, out_shapes


def _exchange_refs(items, widths, srcs, outs):
    x, y, c = lax.axis_index("x"), lax.axis_index("y"), lax.axis_index("c")
    me = 4 * x + 2 * y + c

    def peer(k):
        b = k + 1
        px = 1 - x if b & 4 else x
        py = 1 - y if b & 2 else y
        pc = 1 - c if b & 1 else c
        return (px, py, pc), 4 * px + 2 * py + pc

    def win(ref, ax, idx, w):
        return ref.at[tuple([slice(None)] * ax + [pl.ds(idx * w, w)])]

    def ends(j, mine, theirs):
        _, kind, ax = items[j]
        if kind == "gather":
            return srcs[j], win(outs[j], ax, mine, widths[j])
        return win(srcs[j], ax, theirs, widths[j]), outs[j].at[mine]

    return me, peer, ends


_HBM = pl.BlockSpec(memory_space=pltpu.HBM)
_SEM = pl.BlockSpec(memory_space=pltpu.SEMAPHORE)


def _exchange_start(items, name):
    n = len(items)
    widths, out_shapes = _exchange_plan(items)
    meta = [(None, kind, ax) for _, kind, ax in items]

    def body(*refs):
        srcs, lands = refs[:n], refs[n:2 * n]
        send, recv, lsem = refs[2 * n], refs[2 * n + 1], refs[2 * n + 2]
        token = refs[-1]
        me, peer, ends = _exchange_refs(meta, widths, srcs, lands)
        for j in range(n):
            for k in range(N_DEV - 1):
                dev, pid = peer(k)
                s, d = ends(j, me, pid)
                q = j * (N_DEV - 1) + k
                pltpu.make_async_remote_copy(s, d, send.at[q], recv.at[q], device_id=dev,
                                             device_id_type=pl.DeviceIdType.MESH).start()
        for j in range(n):
            s, d = ends(j, me, me)
            pltpu.make_async_copy(s, d, lsem.at[j]).start()
        token[...] = jnp.zeros_like(token)

    srcs = [pltpu.with_memory_space_constraint(it[0], pltpu.HBM) for it in items]
    lands = [pltpu.with_memory_space_constraint(lax.empty(shp, dt), pltpu.HBM) for shp, dt in out_shapes]
    outs = pl.pallas_call(
        body, name=name,
        out_shape=(pltpu.SemaphoreType.DMA((n * (N_DEV - 1),)), pltpu.SemaphoreType.DMA((n * (N_DEV - 1),)),
                   pltpu.SemaphoreType.DMA((n,)),
                   *[pltpu.HBM(s.shape, s.dtype) for s in srcs], *[pltpu.HBM(shp, dt) for shp, dt in out_shapes],
                   jax.ShapeDtypeStruct((8, LANE), F32)),
        in_specs=[_HBM] * (2 * n),
        out_specs=(_SEM, _SEM, _SEM, *([_HBM] * (2 * n)), pl.BlockSpec(memory_space=pltpu.VMEM)),
        input_output_aliases={i: 3 + i for i in range(2 * n)},
        compiler_params=pltpu.CompilerParams(has_side_effects=pltpu.SideEffectType.DATAFLOW_SIDE_EFFECTING),
    )(*srcs, *lands)
    handle = (meta, widths, outs[0], outs[1], outs[2], outs[3:3 + n], outs[3 + n:3 + 2 * n])
    return handle, outs[-1]


def _exchange_wait(handle, after, name):
    meta, widths, send_sem, recv_sem, local_sem, src_thru, land_thru = handle
    n = len(meta)

    def body(*refs):
        srcs, lands = refs[:n], refs[n:2 * n]
        send, recv, lsem = refs[2 * n], refs[2 * n + 1], refs[2 * n + 2]
        me, peer, ends = _exchange_refs(meta, widths, srcs, lands)
        for j in range(n):
            for k in range(N_DEV - 1):
                dev, pid = peer(k)
                q = j * (N_DEV - 1) + k
                s, d = ends(j, me, pid)
                pltpu.make_async_remote_copy(s, d, send.at[q], recv.at[q], device_id=dev,
                                             device_id_type=pl.DeviceIdType.MESH).wait_send()
                s, d = ends(j, pid, me)
                pltpu.make_async_remote_copy(s, d, send.at[q], recv.at[q], device_id=dev,
                                             device_id_type=pl.DeviceIdType.MESH).wait_recv()
        for j in range(n):
            s, d = ends(j, me, me)
            pltpu.make_async_copy(s, d, lsem.at[j]).wait()

    outs = pl.pallas_call(
        body, name=name,
        out_shape=tuple(pltpu.HBM(a.shape, a.dtype) for a in (*src_thru, *land_thru)),
        in_specs=[_HBM] * (2 * n) + [_SEM, _SEM, _SEM, pl.BlockSpec(memory_space=pl.ANY)],
        out_specs=tuple([_HBM] * (2 * n)),
        input_output_aliases={i: i for i in range(2 * n)},
        compiler_params=pltpu.CompilerParams(has_side_effects=pltpu.SideEffectType.DATAFLOW_SIDE_EFFECTING),
    )(*src_thru, *land_thru, send_sem, recv_sem, local_sem, after)
    return list(outs[n:])


def _pack(arrs):
    flat = jnp.concatenate([a.reshape(-1).astype(F32) for a in arrs])
    n = flat.shape[0]
    rows = -(-n // (8 * LANE)) * 8
    return jnp.pad(flat, (0, rows * LANE - n)).reshape(rows, LANE)


def _unpack(buf, shapes):
    flat = buf.reshape(-1)
    out, off = [], 0
    for s in shapes:
        sz = int(np.prod(s))
        out.append(flat[off:off + sz].reshape(s))
        off += sz
    return out


def _pad_axis(a, axis, size):
    pad = [(0, 0)] * a.ndim
    pad[axis] = (0, size - a.shape[axis])
    return jnp.pad(a, pad)


def _ffn_forward(x, g, wg, wu, wd, tag):
    a = _rms_fwd(x, g, f"{tag}_rms")
    gate, up, hid = _ffn_up(a, wg, wu, f"{tag}_up")
    out = _mm([(hid, wd)], "nn", F32, f"{tag}_down", scale=0.5, res=x)
    return out, (x, a, gate, up, hid)


def _ffn_backward(dxp, saved, g, wg, wu, wd, tag, emit=None, after=None):
    x, a, gate, up, hid = saved
    d_gate, d_up = _ffn_bwd_hid(dxp, wd, gate, up, f"{tag}_bwd_hid", after)
    d_wd = _mm([(hid, dxp)], "tn", _MXU, f"{tag}_dwd", scale=0.5, tk=2048)
    tok = emit("down", d_wd) if emit is not None else None
    d_wg = _mm([(a, d_gate)], "tn", _MXU, f"{tag}_dwg", tk=2048, after=tok)
    tok = emit("gate", d_wg) if emit is not None else None
    d_wu = _mm([(a, d_up)], "tn", _MXU, f"{tag}_dwu", tk=2048, after=tok)
    tok = emit("up", d_wu) if emit is not None else None
    d_a = _mm([(d_gate, wg), (d_up, wu)], "nt", F32, f"{tag}_da", after=tok)
    dx, dg = _rms_bwd(d_a, x, g, dxp, f"{tag}_rms_bwd")
    return dx, dg, d_wg, d_wu, d_wd


def _tile_vec(v, reps):
    return jnp.tile(v.reshape(1, -1), (1, reps))


def _mixer_forward(x1, p, consts, tag):
    h = _rms_fwd(x1, p["mix_norm"], f"{tag}_rms")
    zg = _mm([(h, p["w_zg"])], "nn", F32, f"{tag}_in_g")
    zc = _mm([(h, p["w_conf"])], "nn", F32, f"{tag}_in_c")
    zs = _mm([(h, p["w_sc"])], "nn", F32, f"{tag}_in_s")
    zw = _mm([(h, p["w_swa"])], "nn", F32, f"{tag}_in_w")
    zf = _mm([(h, p["w_fox"])], "nn", F32, f"{tag}_in_f")
    u1, act_c = _conf_fwd(zc, p["conf_dw"], p["conf_dw_b"], p["conf_ln_g"], p["conf_ln_b"], f"{tag}_conf")
    act_s = _sc_fwd(zs, p["sc_conv"], f"{tag}_sc")
    act_w = _swa_fwd(zw, p["swa_q_norm"], p["swa_k_norm"], p["swa_sink"], consts["bias"], consts["expand"], f"{tag}_swa")
    qa, ka, va, vat = _fox2_prep(zf, p["fox_q_norm"], p["fox_k_norm"], p["b_forget"], consts["sel"], f"{tag}_fox_prep")
    act_f, qb = _fox2_fwd(qa, ka, vat, consts["sel"], f"{tag}_fox")
    acts = (act_c, act_s, act_w, act_f)
    x2, merged = _merge_fwd(acts, zg, p["w_br"], p["w_out"], x1, f"{tag}_merge")
    saved = (x1, h, zg, zc, zs, zw, zf, u1, acts, qb, ka, va, merged)
    return x2, saved


def _mixer_backward(dx2, saved, p, consts, tag, after=None):
    x1, h, zg, zc, zs, zw, zf, u1, acts, qb, ka, va, merged = saved
    g = {}
    g["w_out"] = _mm([(merged, dx2)], "tn", _MXU, f"{tag}_dwout", tk=2048, after=after)
    d_c, d_s, d_w, d_f, dzg, g["w_br"] = _merge_bwd(dx2, acts, zg, p["w_br"], p["w_out"], f"{tag}_merge_bwd")
    du1, sm_c = _conf_bwd_ln(d_c, u1, p["conf_ln_g"], p["conf_ln_b"], f"{tag}_conf_bwd_ln")
    dzc, g["conf_dw"] = _conf_bwd_conv(zc, du1, p["conf_dw"], f"{tag}_conf_bwd_conv")
    g["conf_ln_g"], g["conf_ln_b"], g["conf_dw_b"] = sm_c[0], sm_c[1], sm_c[2]
    dzs, g["sc_conv"] = _sc_bwd(zs, d_s, p["sc_conv"], f"{tag}_sc_bwd")
    dzw, dgq, dgk, g["swa_sink"], g["rel_bias"] = _swa_bwd(
        zw, d_w, p["swa_q_norm"], p["swa_k_norm"], p["swa_sink"], consts["bias"], consts["bucket"], consts["expand"],
        f"{tag}_swa_bwd")
    g["swa_q_norm"], g["swa_k_norm"] = dgq, dgk
    doa = _fox2_bwd_prep(acts[3], d_f, consts["sel"], f"{tag}_fox_bwd_prep")
    dkn, dv, dfk, dqt = _fox2_bwd(qb, ka, va, doa, consts["sel"], f"{tag}_fox_bwd")
    dzf, sm_f = _fox2_post(zf, dqt, dkn, dv, dfk, consts["sel"], p["fox_q_norm"], p["fox_k_norm"], p["b_forget"], f"{tag}_fox_post")
    g["fox_q_norm"], g["fox_k_norm"], g["b_forget"] = sm_f[0], sm_f[1], sm_f[2]
    parts = ((dzg, "w_zg"), (dzc, "w_conf"), (dzs, "w_sc"), (dzw, "w_swa"), (dzf, "w_fox"))
    for dz, wname in parts:
        g[wname] = _mm([(h, dz)], "tn", _MXU, f"{tag}_d{wname}", tk=2048)
    dh = _mm([(dz, p[wname]) for dz, wname in parts], "nt", F32, f"{tag}_dh", tm=512)
    dx1, g["mix_norm"] = _rms_bwd(dh, x1, p["mix_norm"], dx2, f"{tag}_rms_bwd")
    return dx1, g


W_NAMES = ['rel_bias', 'ffn1_norm', 'ffn1_w_gate', 'ffn1_w_up', 'ffn1_w_down', 'mix_norm', 'w_in', 'b_forget', 'conf_dw',
           'conf_dw_b', 'conf_ln_g', 'conf_ln_b', 'conf_w_out', 'sc_conv', 'sc_w_out', 'swa_q_norm', 'swa_k_norm',
           'swa_sink', 'swa_w_o', 'fox_q_norm', 'fox_k_norm', 'fox_w_o', 'w_out', 'ffn2_norm', 'ffn2_w_gate',
           'ffn2_w_up', 'ffn2_w_down']
SMALL = ['rel_bias', 'ffn1_norm', 'mix_norm', 'b_forget', 'conf_dw', 'conf_dw_b', 'conf_ln_g', 'conf_ln_b', 'sc_conv',
         'swa_q_norm', 'swa_k_norm', 'swa_sink', 'fox_q_norm', 'fox_k_norm', 'ffn2_norm']
BRANCH_W = ['conf_w_out', 'sc_w_out', 'swa_w_o', 'fox_w_o']
IN_CONF, IN_SC, IN_SWA, IN_FOX, IN_FF = (0, 512), (512, 1280), (1280, 1792), (1792, 2560), (2560, 2564)


def _step(w, m, v, x, loss_target):
    T, D = x.shape
    L = w["w_out"].shape[0]
    fs = w["ffn1_w_gate"].shape[2]
    fsp = -(-fs // LANE) * LANE
    dev = 4 * lax.axis_index("x") + 2 * lax.axis_index("y") + lax.axis_index("c")

    def cast(a):
        return a.astype(_MXU)

    win = w["w_in"]
    fox_cols = jnp.concatenate([win[..., IN_FOX[0]:IN_FF[1]],
                                jnp.zeros(win.shape[:2] + (LANE - (IN_FF[1] - IN_FF[0]),), win.dtype)], axis=-1)
    shards = {
        "ffn1_w_gate": (cast(_pad_axis(w["ffn1_w_gate"], 2, fsp)), 2),
        "ffn1_w_up": (cast(_pad_axis(w["ffn1_w_up"], 2, fsp)), 2),
        "ffn1_w_down": (cast(_pad_axis(w["ffn1_w_down"], 1, fsp)), 1),
        "ffn2_w_gate": (cast(_pad_axis(w["ffn2_w_gate"], 2, fsp)), 2),
        "ffn2_w_up": (cast(_pad_axis(w["ffn2_w_up"], 2, fsp)), 2),
        "ffn2_w_down": (cast(_pad_axis(w["ffn2_w_down"], 1, fsp)), 1),
        "w_zg": (cast(win[..., IN_FF[1]:]), 1),
        "w_conf": (cast(win[..., IN_CONF[0]:IN_CONF[1]]), 1),
        "w_sc": (cast(win[..., IN_SC[0]:IN_SC[1]]), 1),
        "w_swa": (cast(win[..., IN_SWA[0]:IN_SWA[1]]), 1),
        "w_fox": (cast(fox_cols), 1),
        "w_out": (cast(w["w_out"]), 1),
        "w_br": (cast(jnp.stack([w[n] for n in BRANCH_W], axis=1)), 3),
    }
    big = list(shards)
    conv_shard = jnp.concatenate([jnp.swapaxes(w["conf_dw"], 1, 2), jnp.swapaxes(w["sc_conv"], 1, 2)], axis=2)

    stages = (("ffn1", ["ffn1_w_gate", "ffn1_w_up"]), ("ffn1d", ["ffn1_w_down"]),
              ("mix", ["w_zg", "w_conf", "w_sc", "w_swa", "w_fox", "w_out", "w_br"]),
              ("ffn2", ["ffn2_w_gate", "ffn2_w_up", "ffn2_w_down"]))
    stage_names = dict(stages)
    flight = {}

    def depart(l, st, dep):
        items = [(shards[n][0][l], "gather", shards[n][1] - 1) for n in stage_names[st]]
        if (l, st) == (0, "mix"):
            items.append((conv_shard, "gather", 1))
        if dep is not None:
            src0 = items[0][0]
            zero = (dep[(0,) * dep.ndim].astype(F32) * 0.0).astype(src0.dtype)
            items[0] = (src0 + zero,) + items[0][1:]
        flight[l, st], tok = _exchange_start(items, f"gather_start_l{l}_{st}")
        return tok

    def arrive(l, st, after):
        got = _exchange_wait(flight.pop((l, st)), after, f"gather_wait_l{l}_{st}")
        params[l].update(zip(stage_names[st], got))
        if (l, st) == (0, "mix"):
            conv_full = jnp.swapaxes(got[-1], 1, 2)
            for i, q in enumerate(params):
                q["conf_dw"] = _pad_axis(conv_full[i, :CONV_K], 0, CONV_HALO)
                q["sc_conv"] = _pad_axis(conv_full[i, CONV_K:], 0, SC_HALO)
        return got[0]

    bucket = jnp.asarray(_swa_bucket_matrix(min(SWA_TQ, T)))
    consts = {"bucket": bucket, "expand": jnp.asarray(_kv_expand_matrix(), _MXU),
              "sel": jnp.asarray(_head_select_matrix(), _MXU),
              "bias": _swa_bias(w["rel_bias"], bucket, "swa_bias")}

    def layer_params(l):
        p = {}
        for n in ("ffn1_norm", "mix_norm", "ffn2_norm", "conf_dw_b", "conf_ln_g", "conf_ln_b"):
            p[n] = w[n][l].reshape(1, -1)
        p["swa_q_norm"], p["fox_q_norm"] = _tile_vec(w["swa_q_norm"][l], 4), _tile_vec(w["fox_q_norm"][l], 4)
        p["swa_k_norm"], p["fox_k_norm"] = _tile_vec(w["swa_k_norm"][l], 2), _tile_vec(w["fox_k_norm"][l], 4)
        p["swa_sink"] = w["swa_sink"][l].reshape(1, 4)
        p["b_forget"] = _pad_axis(w["b_forget"][l].reshape(1, 4), 1, LANE)
        return p

    params = [layer_params(l) for l in range(L)]
    saved = [None] * L
    cur = x
    first = depart(0, "ffn1", None)
    for l, p in enumerate(params):
        a = _rms_fwd(cur, p["ffn1_norm"] + (0.0 if l else first[0:1, 0:1]), f"l{l}_ffn1_rms")
        got = arrive(l, "ffn1", a)
        tok = depart(l, "mix", depart(l, "ffn1d", got))
        gate, up, hid = _ffn_up(a, p["ffn1_w_gate"], p["ffn1_w_up"], f"l{l}_ffn1_up", after=tok)
        got = arrive(l, "ffn1d", hid)
        tok = depart(l, "ffn2", got)
        x1 = _mm([(hid, p["ffn1_w_down"])], "nn", F32, f"l{l}_ffn1_down", scale=0.5, res=cur, after=tok)
        s1 = (cur, a, gate, up, hid)
        got = arrive(l, "mix", x1)
        zero = depart(l + 1, "ffn1", got)[0:1, 0:1] if l + 1 < L else 0.0
        x2, s2 = _mixer_forward(x1, dict(p, mix_norm=p["mix_norm"] + zero), consts, f"l{l}_mix")
        arrive(l, "ffn2", x2)
        cur, s3 = _ffn_forward(x2, p["ffn2_norm"], p["ffn2_w_gate"], p["ffn2_w_up"], p["ffn2_w_down"], f"l{l}_ffn2")
        saved[l] = (s1, s2, s3)
    dcur, loss_part = _loss_grad(cur, loss_target)

    grads = [None] * L
    leaving = []

    def leave(l, st, names, g):
        h, tok = _exchange_start([(g[n], "scatter", shards[n][1] - 1) for n in names], f"scatter_start_l{l}_{st}")
        leaving.append((l, st, names, h))
        return tok

    tok = None
    for l in reversed(range(L)):
        p = params[l]
        s1, s2, s3 = saved[l]
        g = {}
        dcur, g["ffn2_norm"], g["ffn2_w_gate"], g["ffn2_w_up"], g["ffn2_w_down"] = _ffn_backward(
            dcur, s3, p["ffn2_norm"], p["ffn2_w_gate"], p["ffn2_w_up"], p["ffn2_w_down"], f"l{l}_ffn2", after=tok)
        tok = leave(l, "ffn2", stage_names["ffn2"], g)
        dcur, gm = _mixer_backward(dcur, s2, p, consts, f"l{l}_mix", after=tok)
        g.update(gm)
        tok = leave(l, "mix", stage_names["mix"], g)
        ffn1 = ["ffn1_w_gate", "ffn1_w_up", "ffn1_w_down"]
        emit = (lambda which, arr, l=l: leave(l, which, [f"ffn1_w_{which}"], {f"ffn1_w_{which}": arr})) if l == 0 else None
        dcur, g["ffn1_norm"], g["ffn1_w_gate"], g["ffn1_w_up"], g["ffn1_w_down"] = _ffn_backward(
            dcur, s1, p["ffn1_norm"], p["ffn1_w_gate"], p["ffn1_w_up"], p["ffn1_w_down"], f"l{l}_ffn1", emit, after=tok)
        if emit is None:
            tok = leave(l, "ffn1", ffn1, g)
        grads[l] = g
    grad_x = dcur

    gsum = {n: [None] * L for n in big}
    for l, st, names, h in leaving:
        for n, r in zip(names, _exchange_wait(h, grad_x, f"scatter_wait_l{l}_{st}")):
            gsum[n][l] = _sum8(r, f"sum_{n}_l{l}")
    gsum = {n: jnp.stack(parts) for n, parts in gsum.items()}
    gw = {}
    wide = ("ffn1_w_gate", "ffn1_w_up", "ffn2_w_gate", "ffn2_w_up")
    gw_t = {n: jnp.swapaxes(gsum[n][:, :, :fs], 1, 2) for n in wide}
    for n in wide:
        gw[n] = jnp.swapaxes(gw_t[n], 1, 2)
    for n in ("ffn1_w_down", "ffn2_w_down"):
        gw[n] = gsum[n][:, :fs, :]
    gw["w_out"] = gsum["w_out"]
    for i, n in enumerate(BRANCH_W):
        gw[n] = gsum["w_br"][:, i]
    gw["w_in"] = jnp.concatenate([gsum["w_conf"], gsum["w_sc"], gsum["w_swa"],
                                  gsum["w_fox"][..., :IN_FF[1] - IN_FOX[0]], gsum["w_zg"]], axis=-1)

    def small_partial(n):
        per_layer = [grads[l][n] for l in range(L)]
        if n == "rel_bias":
            return sum(pl_[:, :4] for pl_ in per_layer)
        if n in ("swa_sink", "b_forget"):
            return jnp.stack([a.reshape(-1)[:4] for a in per_layer])
        if n in ("swa_q_norm", "fox_q_norm", "fox_k_norm"):
            return jnp.stack([a.reshape(4, HEAD).sum(0) for a in per_layer])
        if n == "swa_k_norm":
            return jnp.stack([a.reshape(2, HEAD).sum(0) for a in per_layer])
        if n == "conf_dw":
            return jnp.stack([a[:CONV_K] for a in per_layer])
        if n == "sc_conv":
            return jnp.stack([a[:SC_K] for a in per_layer])
        return jnp.stack([a.reshape(-1) for a in per_layer])

    partial = [small_partial(n) for n in SMALL]
    small_shapes = [a.shape for a in partial]
    all_parts = _exchange([(_pack(partial), "gather", 0)], "gather_small_grads")[0]
    rows = all_parts.shape[0] // N_DEV
    small_sum = _unpack(_sum8(all_parts.reshape(N_DEV, rows, LANE), "sum_small"), small_shapes)
    for n, a in zip(SMALL, small_sum):
        if n in ("conf_dw", "sc_conv"):
            cs = w[n].shape[2]
            a = lax.dynamic_slice_in_dim(a, dev * cs, cs, axis=2)
        gw[n] = a

    delta, new_m, new_v = {}, {}, {}
    for n in W_NAMES:
        if n in wide:
            outs = _adamw(jnp.swapaxes(w[n], 1, 2), gw_t[n], jnp.swapaxes(m[n], 1, 2), jnp.swapaxes(v[n], 1, 2),
                          f"adamw_{n}")
            delta[n], new_m[n], new_v[n] = (jnp.swapaxes(o, 1, 2) for o in outs)
        elif n == "w_in":
            outs = _adamw(*(jnp.transpose(a, (2, 0, 1)) for a in (w[n], gw[n], m[n], v[n])), f"adamw_{n}")
            delta[n], new_m[n], new_v[n] = (jnp.transpose(o, (1, 2, 0)) for o in outs)
        elif n not in SMALL:
            delta[n], new_m[n], new_v[n] = _adamw(w[n], gw[n], m[n], v[n], f"adamw_{n}")
    shapes = [w[n].shape for n in SMALL]
    outs = _adamw(_pack([w[n] for n in SMALL]), _pack([gw[n] for n in SMALL]), _pack([m[n] for n in SMALL]),
                  _pack([v[n] for n in SMALL]), "adamw_small")
    for res, out in zip((delta, new_m, new_v), outs):
        for n, a in zip(SMALL, _unpack(out, shapes)):
            res[n] = a

    loss = lax.psum(loss_part[0, 0], ("x", "y", "c"))
    return loss, grad_x, gw, delta, new_m, new_v


def kernel(x, rel_bias, ffn1_norm, ffn1_w_gate, ffn1_w_up, ffn1_w_down, mix_norm, w_in, b_forget, conf_dw, conf_dw_b, conf_ln_g, conf_ln_b, conf_w_out, sc_conv, sc_w_out, swa_q_norm, swa_k_norm, swa_sink, swa_w_o, fox_q_norm, fox_k_norm, fox_w_o, w_out, ffn2_norm, ffn2_w_gate, ffn2_w_up, ffn2_w_down, loss_target, m_rel_bias, m_ffn1_norm, m_ffn1_w_gate, m_ffn1_w_up, m_ffn1_w_down, m_mix_norm, m_w_in, m_b_forget, m_conf_dw, m_conf_dw_b, m_conf_ln_g, m_conf_ln_b, m_conf_w_out, m_sc_conv, m_sc_w_out, m_swa_q_norm, m_swa_k_norm, m_swa_sink, m_swa_w_o, m_fox_q_norm, m_fox_k_norm, m_fox_w_o, m_w_out, m_ffn2_norm, m_ffn2_w_gate, m_ffn2_w_up, m_ffn2_w_down, v_rel_bias, v_ffn1_norm, v_ffn1_w_gate, v_ffn1_w_up, v_ffn1_w_down, v_mix_norm, v_w_in, v_b_forget, v_conf_dw, v_conf_dw_b, v_conf_ln_g, v_conf_ln_b, v_conf_w_out, v_sc_conv, v_sc_w_out, v_swa_q_norm, v_swa_k_norm, v_swa_sink, v_swa_w_o, v_fox_q_norm, v_fox_k_norm, v_fox_w_o, v_w_out, v_ffn2_norm, v_ffn2_w_gate, v_ffn2_w_up, v_ffn2_w_down):
    args = locals()
    w = {n: args[n] for n in W_NAMES}
    m = {n: args["m_" + n] for n in W_NAMES}
    v = {n: args["v_" + n] for n in W_NAMES}
    T, D = x.shape[-2:]
    loss, grad_x, gw, delta, new_m, new_v = _step(w, m, v, x.reshape(T, D), loss_target.reshape(T, D))
    return (loss, grad_x.reshape(x.shape), *[gw[n] for n in W_NAMES], *[delta[n] for n in W_NAMES],
            *[new_m[n] for n in W_NAMES], *[new_v[n] for n in W_NAMES])
```

```python
import math

import numpy as np
import jax
import jax.numpy as jnp
from jax import lax
from jax.experimental import pallas as pl
from jax.experimental.pallas import tpu as pltpu

F32 = jnp.float32
_MXU = jnp.bfloat16
EPS = 1e-6
NEG_INF = -1e30
HEAD = 64
CH = 256
WINDOW = 128
CONV_K = 31
SC_K = 3
CONV_HALO = 32
SC_HALO = 8
N_BUCKETS = 32
MAX_DISTANCE = 128
N_DEV = 8
LANE = 128
ROW_TILE = 512
VMEM_LIMIT = 48 * 1024 * 1024
ADAM_LR, ADAM_B1, ADAM_B2, ADAM_EPS, ADAM_WD, ADAM_STEP = 0.001, 0.9, 0.999, 1e-08, 0.01, 10

_NN = (((1,), (0,)), ((), ()))
_NT = (((1,), (1,)), ((), ()))
_TN = (((0,), (0,)), ((), ()))


def _cp(*sem):
    return pltpu.CompilerParams(dimension_semantics=sem, vmem_limit_bytes=VMEM_LIMIT)


def _tile(n, pref, align=LANE):
    t = (min(n, pref) // align) * align
    while t >= align:
        if n % t == 0:
            return t
        t -= align
    return n


def _dot(a, b, dims=_NN):
    return lax.dot_general(a.astype(_MXU), b.astype(_MXU), dims, preferred_element_type=F32)


def _split3(x):
    hi = x.astype(_MXU)
    r1 = x - hi.astype(F32)
    mid = r1.astype(_MXU)
    lo = (r1 - mid.astype(F32)).astype(_MXU)
    return hi, mid, lo


def _exact_dot(a, b, dims, data):
    if data == "a":
        return sum(lax.dot_general(p, b.astype(_MXU), dims, preferred_element_type=F32) for p in _split3(a))
    return sum(lax.dot_general(a.astype(_MXU), p, dims, preferred_element_type=F32) for p in _split3(b))


def _sigmoid(x):
    return jax.nn.sigmoid(x)


def _lane_mask(width, h):
    lane = lax.broadcasted_iota(jnp.int32, (1, width), 1)
    return (lane >= h * HEAD) & (lane < (h + 1) * HEAD)


def _head_rms(x, g, nh):
    xx = x * x
    ms = jnp.zeros_like(x)
    for h in range(nh):
        mk = _lane_mask(x.shape[-1], h)
        s = jnp.sum(jnp.where(mk, xx, 0.0), axis=-1, keepdims=True) * (1.0 / HEAD)
        ms = jnp.where(mk, s, ms)
    r = lax.rsqrt(ms + EPS)
    return x * r * g, r


def _head_rms_bwd(dy, x, r, g, nh):
    w = dy * g
    wx = w * x
    c = jnp.zeros_like(x)
    for h in range(nh):
        mk = _lane_mask(x.shape[-1], h)
        s = jnp.sum(jnp.where(mk, wx, 0.0), axis=-1, keepdims=True) * (1.0 / HEAD)
        c = jnp.where(mk, s, c)
    dx = r * w - x * (r * r * r) * c
    dg = jnp.sum(dy * x * r, axis=0, keepdims=True)
    return dx, dg


def _mm(pairs, mode, out_dtype, name, scale=None, res=None, tm=1024, tn=1024, tk=1024, after=None):
    a0, b0 = pairs[0]
    M = a0.shape[1] if mode == "tn" else a0.shape[0]
    N = b0.shape[0] if mode == "nt" else b0.shape[1]
    tm, tn = _tile(M, tm), _tile(N, tn)
    dims = {"nn": _NN, "nt": _NT, "tn": _TN}[mode]
    tks, nks, offs = [], [], []
    for a, _ in pairs:
        K = a.shape[0] if mode == "tn" else a.shape[1]
        t = _tile(K, tk)
        tks.append(t)
        nks.append(K // t)
        offs.append(sum(nks[:-1]))
    nk_tot = sum(nks)
    in_specs, args = [], []
    for (a, b), t, nk, off in zip(pairs, tks, nks, offs):
        def kk(k, off=off, nk=nk):
            return jnp.clip(k - off, 0, nk - 1)
        if mode == "tn":
            in_specs.append(pl.BlockSpec((t, tm), lambda i, j, k, kk=kk: (kk(k), i)))
        else:
            in_specs.append(pl.BlockSpec((tm, t), lambda i, j, k, kk=kk: (i, kk(k))))
        if mode == "nt":
            in_specs.append(pl.BlockSpec((tn, t), lambda i, j, k, kk=kk: (j, kk(k))))
        else:
            in_specs.append(pl.BlockSpec((t, tn), lambda i, j, k, kk=kk: (kk(k), j)))
        args += [a, b]
    if res is not None:
        in_specs.append(pl.BlockSpec((tm, tn), lambda i, j, k: (i, j)))
        args.append(res)
    if after is not None:
        in_specs.append(pl.BlockSpec(memory_space=pl.ANY))
        args.append(after)
    npairs = len(pairs)

    def body(*refs):
        ab = refs[:2 * npairs]
        res_ref = refs[2 * npairs] if res is not None else None
        o_ref = refs[2 * npairs + (res is not None) + (after is not None)]
        acc = refs[-1]
        k = pl.program_id(2)

        def finish(r):
            if scale is not None:
                r = r * scale
            if res_ref is not None:
                r = r + res_ref[...]
            o_ref[...] = r.astype(o_ref.dtype)

        if nk_tot == 1:
            finish(_dot(ab[0][...], ab[1][...], dims))
            return

        @pl.when(k == 0)
        def _():
            acc[...] = jnp.zeros_like(acc)

        for p in range(npairs):
            @pl.when(jnp.logical_and(k >= offs[p], k < offs[p] + nks[p]))
            def _(p=p):
                acc[...] += _dot(ab[2 * p][...], ab[2 * p + 1][...], dims)

        @pl.when(k == nk_tot - 1)
        def _():
            finish(acc[...])

    return pl.pallas_call(
        body, name=name, grid=(M // tm, N // tn, nk_tot), in_specs=in_specs,
        out_specs=pl.BlockSpec((tm, tn), lambda i, j, k: (i, j)),
        out_shape=jax.ShapeDtypeStruct((M, N), out_dtype),
        scratch_shapes=[pltpu.VMEM((tm, tn), F32)],
        compiler_params=_cp("parallel", "parallel", "arbitrary"))(*args)


def _rms_fwd(x, g, name):
    T, D = x.shape
    tm = _tile(T, ROW_TILE)

    def body(x_ref, g_ref, o_ref):
        xv = x_ref[...]
        r = lax.rsqrt(jnp.mean(xv * xv, axis=-1, keepdims=True) + EPS)
        o_ref[...] = (xv * r * g_ref[...]).astype(o_ref.dtype)

    return pl.pallas_call(
        body, name=name, grid=(T // tm,),
        in_specs=[pl.BlockSpec((tm, D), lambda i: (i, 0)), pl.BlockSpec((1, D), lambda i: (0, 0))],
        out_specs=pl.BlockSpec((tm, D), lambda i: (i, 0)),
        out_shape=jax.ShapeDtypeStruct((T, D), _MXU), compiler_params=_cp("parallel"))(x, g)


def _rms_bwd(da, x, g, dres, name):
    T, D = x.shape
    tm = _tile(T, ROW_TILE)

    def body(da_ref, x_ref, g_ref, dr_ref, dx_ref, dg_ref, dxb_ref):
        @pl.when(pl.program_id(0) == 0)
        def _():
            dg_ref[...] = jnp.zeros_like(dg_ref)

        xv, dav = x_ref[...], da_ref[...]
        r = lax.rsqrt(jnp.mean(xv * xv, axis=-1, keepdims=True) + EPS)
        w = dav * g_ref[...]
        c = jnp.mean(w * xv, axis=-1, keepdims=True)
        dx = dr_ref[...] + (r * w - xv * (r * r * r) * c)
        dx_ref[...] = dx
        dxb_ref[...] = dx.astype(dxb_ref.dtype)
        dg_ref[...] += jnp.sum(dav * xv * r, axis=0, keepdims=True)

    row = pl.BlockSpec((tm, D), lambda i: (i, 0))
    vec = pl.BlockSpec((1, D), lambda i: (0, 0))
    dx, dg, dxb = pl.pallas_call(
        body, name=name, grid=(T // tm,), in_specs=[row, row, vec, row], out_specs=[row, vec, row],
        out_shape=[jax.ShapeDtypeStruct((T, D), F32), jax.ShapeDtypeStruct((1, D), F32),
                   jax.ShapeDtypeStruct((T, D), _MXU)],
        compiler_params=_cp("arbitrary"))(da, x, g, dres)
    return (dx, dxb), dg


def _loss_grad(y, tgt):
    T, D = y.shape
    tm = _tile(T, ROW_TILE)

    def body(y_ref, t_ref, dy_ref, l_ref, dyb_ref):
        @pl.when(pl.program_id(0) == 0)
        def _():
            l_ref[...] = jnp.zeros_like(l_ref)

        d = y_ref[...] - t_ref[...]
        dy = d * (1.0 / D)
        dy_ref[...] = dy
        dyb_ref[...] = dy.astype(dyb_ref.dtype)
        per_tok = jnp.mean(d * d, axis=-1, keepdims=True)
        l_ref[...] += 0.5 * jnp.sum(per_tok, axis=0, keepdims=True)

    row = pl.BlockSpec((tm, D), lambda i: (i, 0))
    dy, loss, dyb = pl.pallas_call(
        body, name="loss_grad", grid=(T // tm,), in_specs=[row, row],
        out_specs=[row, pl.BlockSpec((1, 1), lambda i: (0, 0)), row],
        out_shape=[jax.ShapeDtypeStruct((T, D), F32), jax.ShapeDtypeStruct((1, 1), F32),
                   jax.ShapeDtypeStruct((T, D), _MXU)],
        compiler_params=_cp("arbitrary"))(y, tgt)
    return (dy, dyb), loss


def _ffn_up(a, wg, wu, name, after=None):
    T, D = a.shape
    Fp = wg.shape[1]
    tm, tn = _tile(T, ROW_TILE), _tile(Fp, 768)
    extra = [] if after is None else [after]

    def body(*refs):
        a_ref, wg_ref, wu_ref = refs[:3]
        g_ref, u_ref, h_ref = refs[-3:]
        av = a_ref[...]
        g = _dot(av, wg_ref[...])
        u = _dot(av, wu_ref[...])
        g_ref[...] = g.astype(g_ref.dtype)
        u_ref[...] = u.astype(u_ref.dtype)
        h_ref[...] = (g * _sigmoid(g) * u).astype(h_ref.dtype)

    wspec = pl.BlockSpec((D, tn), lambda j, i: (0, j))
    ospec = pl.BlockSpec((tm, tn), lambda j, i: (i, j))
    osh = jax.ShapeDtypeStruct((T, Fp), _MXU)
    return pl.pallas_call(
        body, name=name, grid=(Fp // tn, T // tm),
        in_specs=[pl.BlockSpec((tm, D), lambda j, i: (i, 0)), wspec, wspec] + [pl.BlockSpec(memory_space=pl.ANY)] * len(extra),
        out_specs=[ospec, ospec, ospec], out_shape=[osh, osh, osh],
        compiler_params=_cp("parallel", "parallel"))(a, wg, wu, *extra)


def _ffn_bwd_hid(dxp, wd, gate, up, name, after=None):
    T, D = dxp.shape
    Fp = wd.shape[0]
    tm, tn = _tile(T, ROW_TILE), _tile(Fp, 768)
    extra = [] if after is None else [after]

    def body(*refs):
        dx_ref, wd_ref, g_ref, u_ref = refs[:4]
        dg_ref, du_ref = refs[-2:]
        dh = 0.5 * _dot(dx_ref[...], wd_ref[...], _NT)
        g = g_ref[...].astype(F32)
        u = u_ref[...].astype(F32)
        s = _sigmoid(g)
        du_ref[...] = (dh * (g * s)).astype(du_ref.dtype)
        dg_ref[...] = (dh * u * (s * (1.0 + g * (1.0 - s)))).astype(dg_ref.dtype)

    tspec = pl.BlockSpec((tm, tn), lambda j, i: (i, j))
    osh = jax.ShapeDtypeStruct((T, Fp), _MXU)
    return pl.pallas_call(
        body, name=name, grid=(Fp // tn, T // tm),
        in_specs=[pl.BlockSpec((tm, D), lambda j, i: (i, 0)), pl.BlockSpec((tn, D), lambda j, i: (j, 0)), tspec, tspec]
        + [pl.BlockSpec(memory_space=pl.ANY)] * len(extra),
        out_specs=[tspec, tspec], out_shape=[osh, osh],
        compiler_params=_cp("parallel", "parallel"))(dxp, wd, gate, up, *extra)


def _conf_fwd(zc, dw, b, lng, lnb, name):
    T = zc.shape[0]
    tm = _tile(T, ROW_TILE)
    r = tm // CONV_HALO

    def body(z_ref, zh_ref, dw_ref, b_ref, g_ref, lb_ref, u1_ref, act_ref, ext):
        i = pl.program_id(0)
        cur = z_ref[...]
        ext[pl.ds(CONV_HALO, tm), :] = cur[:, :CH] * _sigmoid(cur[:, CH:])
        hal = zh_ref[...]
        ext[pl.ds(0, CONV_HALO), :] = jnp.where(i > 0, hal[:, :CH] * _sigmoid(hal[:, CH:]), 0.0)
        acc = jnp.zeros((tm, CH), F32)
        for k in range(CONV_K):
            acc = acc + dw_ref[pl.ds(k, 1), :] * ext[pl.ds(CONV_HALO - (CONV_K - 1) + k, tm), :]
        u1 = acc + b_ref[...]
        u1_ref[...] = u1
        mu = jnp.mean(u1, axis=-1, keepdims=True)
        var = jnp.mean(jnp.square(u1 - mu), axis=-1, keepdims=True)
        u2 = (u1 - mu) * lax.rsqrt(var + EPS) * g_ref[...] + lb_ref[...]
        act_ref[...] = u2 * _sigmoid(u2)

    vec = pl.BlockSpec((1, CH), lambda i: (0, 0))
    row = pl.BlockSpec((tm, CH), lambda i: (i, 0))
    osh = jax.ShapeDtypeStruct((T, CH), F32)
    return pl.pallas_call(
        body, name=name, grid=(T // tm,),
        in_specs=[pl.BlockSpec((tm, 2 * CH), lambda i: (i, 0)),
                  pl.BlockSpec((CONV_HALO, 2 * CH), lambda i: (jnp.maximum(i * r - 1, 0), 0)),
                  pl.BlockSpec((CONV_HALO, CH), lambda i: (0, 0)), vec, vec, vec],
        out_specs=[row, row], out_shape=[osh, osh],
        scratch_shapes=[pltpu.VMEM((tm + CONV_HALO, CH), F32)],
        compiler_params=_cp("parallel"))(zc, zc, dw, b, lng, lnb)


def _conf_bwd_ln(dact, u1, lng, lnb, name):
    T = u1.shape[0]
    tm = _tile(T, ROW_TILE)

    def body(da_ref, u_ref, g_ref, lb_ref, du_ref, sm_ref):
        @pl.when(pl.program_id(0) == 0)
        def _():
            sm_ref[...] = jnp.zeros_like(sm_ref)

        u1v = u_ref[...]
        mu = jnp.mean(u1v, axis=-1, keepdims=True)
        cen = u1v - mu
        rstd = lax.rsqrt(jnp.mean(cen * cen, axis=-1, keepdims=True) + EPS)
        y = cen * rstd
        u2 = y * g_ref[...] + lb_ref[...]
        s = _sigmoid(u2)
        du2 = da_ref[...] * (s * (1.0 + u2 * (1.0 - s)))
        dy = du2 * g_ref[...]
        du1 = rstd * (dy - jnp.mean(dy, axis=-1, keepdims=True) - y * jnp.mean(dy * y, axis=-1, keepdims=True))
        du_ref[...] = du1
        sm_ref[pl.ds(0, 1), :] += jnp.sum(du2 * y, axis=0, keepdims=True)
        sm_ref[pl.ds(1, 1), :] += jnp.sum(du2, axis=0, keepdims=True)
        sm_ref[pl.ds(2, 1), :] += jnp.sum(du1, axis=0, keepdims=True)

    vec = pl.BlockSpec((1, CH), lambda i: (0, 0))
    row = pl.BlockSpec((tm, CH), lambda i: (i, 0))
    return pl.pallas_call(
        body, name=name, grid=(T // tm,), in_specs=[row, row, vec, vec],
        out_specs=[row, pl.BlockSpec((8, CH), lambda i: (0, 0))],
        out_shape=[jax.ShapeDtypeStruct((T, CH), F32), jax.ShapeDtypeStruct((8, CH), F32)],
        compiler_params=_cp("arbitrary"))(dact, u1, lng, lnb)


def _conf_bwd_conv(zc, du1, dw, name):
    T = zc.shape[0]
    tm = _tile(T, ROW_TILE)
    r = tm // CONV_HALO
    nt = T // tm
    nh = T // CONV_HALO

    def body(z_ref, zh_ref, d_ref, dn_ref, dw_ref, dz_ref, ddw_ref, ext_u, ext_d):
        i = pl.program_id(0)

        @pl.when(i == 0)
        def _():
            ddw_ref[...] = jnp.zeros_like(ddw_ref)

        cur = z_ref[...]
        ca = cur[:, :CH]
        sg = _sigmoid(cur[:, CH:])
        ext_u[pl.ds(CONV_HALO, tm), :] = ca * sg
        hal = zh_ref[...]
        ext_u[pl.ds(0, CONV_HALO), :] = jnp.where(i > 0, hal[:, :CH] * _sigmoid(hal[:, CH:]), 0.0)
        d = d_ref[...]
        ext_d[pl.ds(0, tm), :] = d
        ext_d[pl.ds(tm, CONV_HALO), :] = jnp.where(i < nt - 1, dn_ref[...], 0.0)
        acc = jnp.zeros((tm, CH), F32)
        for k in range(CONV_K):
            acc = acc + dw_ref[pl.ds(k, 1), :] * ext_d[pl.ds(CONV_K - 1 - k, tm), :]
            ddw_ref[pl.ds(k, 1), :] += jnp.sum(
                d * ext_u[pl.ds(CONV_HALO - (CONV_K - 1) + k, tm), :], axis=0, keepdims=True)
        dz_ref[:, :CH] = (acc * sg).astype(dz_ref.dtype)
        dz_ref[:, CH:] = (acc * ca * sg * (1.0 - sg)).astype(dz_ref.dtype)

    return pl.pallas_call(
        body, name=name, grid=(nt,),
        in_specs=[pl.BlockSpec((tm, 2 * CH), lambda i: (i, 0)),
                  pl.BlockSpec((CONV_HALO, 2 * CH), lambda i: (jnp.maximum(i * r - 1, 0), 0)),
                  pl.BlockSpec((tm, CH), lambda i: (i, 0)),
                  pl.BlockSpec((CONV_HALO, CH), lambda i: (jnp.minimum((i + 1) * r, nh - 1), 0)),
                  pl.BlockSpec((CONV_HALO, CH), lambda i: (0, 0))],
        out_specs=[pl.BlockSpec((tm, 2 * CH), lambda i: (i, 0)), pl.BlockSpec((CONV_HALO, CH), lambda i: (0, 0))],
        out_shape=[jax.ShapeDtypeStruct((T, 2 * CH), _MXU), jax.ShapeDtypeStruct((CONV_HALO, CH), F32)],
        scratch_shapes=[pltpu.VMEM((tm + CONV_HALO, CH), F32), pltpu.VMEM((tm + CONV_HALO, CH), F32)],
        compiler_params=_cp("arbitrary"))(zc, zc, du1, du1, dw)


def _sc_fwd(zs, w, name):
    T = zs.shape[0]
    tm = _tile(T, ROW_TILE)
    r = tm // SC_HALO

    def body(z_ref, zh_ref, w_ref, act_ref, ext):
        i = pl.program_id(0)
        cur = z_ref[...]
        ext[pl.ds(SC_HALO, tm), :] = cur[:, CH:2 * CH] * cur[:, 2 * CH:]
        hal = zh_ref[...]
        ext[pl.ds(0, SC_HALO), :] = jnp.where(i > 0, hal[:, CH:2 * CH] * hal[:, 2 * CH:], 0.0)
        v1 = jnp.zeros((tm, CH), F32)
        for k in range(SC_K):
            v1 = v1 + w_ref[pl.ds(k, 1), :] * ext[pl.ds(SC_HALO - (SC_K - 1) + k, tm), :]
        act_ref[...] = cur[:, :CH] * v1

    return pl.pallas_call(
        body, name=name, grid=(T // tm,),
        in_specs=[pl.BlockSpec((tm, 3 * CH), lambda i: (i, 0)),
                  pl.BlockSpec((SC_HALO, 3 * CH), lambda i: (jnp.maximum(i * r - 1, 0), 0)),
                  pl.BlockSpec((SC_HALO, CH), lambda i: (0, 0))],
        out_specs=pl.BlockSpec((tm, CH), lambda i: (i, 0)),
        out_shape=jax.ShapeDtypeStruct((T, CH), F32),
        scratch_shapes=[pltpu.VMEM((tm + SC_HALO, CH), F32)],
        compiler_params=_cp("parallel"))(zs, zs, w)


def _sc_bwd(zs, dact, w, name):
    T = zs.shape[0]
    tm = _tile(T, ROW_TILE)
    r = tm // SC_HALO
    nt = T // tm
    nh = T // SC_HALO

    def body(z_ref, zh_ref, zn_ref, d_ref, dn_ref, w_ref, dz_ref, dw_ref, ext_v, ext_d):
        i = pl.program_id(0)

        @pl.when(i == 0)
        def _():
            dw_ref[...] = jnp.zeros_like(dw_ref)

        cur = z_ref[...]
        sb, sc, sx = cur[:, :CH], cur[:, CH:2 * CH], cur[:, 2 * CH:]
        ext_v[pl.ds(SC_HALO, tm), :] = sc * sx
        hal = zh_ref[...]
        ext_v[pl.ds(0, SC_HALO), :] = jnp.where(i > 0, hal[:, CH:2 * CH] * hal[:, 2 * CH:], 0.0)
        da = d_ref[...]
        dv1 = da * sb
        ext_d[pl.ds(0, tm), :] = dv1
        ext_d[pl.ds(tm, SC_HALO), :] = jnp.where(i < nt - 1, dn_ref[...] * zn_ref[...][:, :CH], 0.0)
        v1 = jnp.zeros((tm, CH), F32)
        dv0 = jnp.zeros((tm, CH), F32)
        for k in range(SC_K):
            shifted = ext_v[pl.ds(SC_HALO - (SC_K - 1) + k, tm), :]
            v1 = v1 + w_ref[pl.ds(k, 1), :] * shifted
            dv0 = dv0 + w_ref[pl.ds(k, 1), :] * ext_d[pl.ds(SC_K - 1 - k, tm), :]
            dw_ref[pl.ds(k, 1), :] += jnp.sum(dv1 * shifted, axis=0, keepdims=True)
        dz_ref[:, :CH] = (da * v1).astype(dz_ref.dtype)
        dz_ref[:, CH:2 * CH] = (dv0 * sx).astype(dz_ref.dtype)
        dz_ref[:, 2 * CH:] = (dv0 * sc).astype(dz_ref.dtype)

    return pl.pallas_call(
        body, name=name, grid=(nt,),
        in_specs=[pl.BlockSpec((tm, 3 * CH), lambda i: (i, 0)),
                  pl.BlockSpec((SC_HALO, 3 * CH), lambda i: (jnp.maximum(i * r - 1, 0), 0)),
                  pl.BlockSpec((SC_HALO, 3 * CH), lambda i: (jnp.minimum((i + 1) * r, nh - 1), 0)),
                  pl.BlockSpec((tm, CH), lambda i: (i, 0)),
                  pl.BlockSpec((SC_HALO, CH), lambda i: (jnp.minimum((i + 1) * r, nh - 1), 0)),
                  pl.BlockSpec((SC_HALO, CH), lambda i: (0, 0))],
        out_specs=[pl.BlockSpec((tm, 3 * CH), lambda i: (i, 0)), pl.BlockSpec((SC_HALO, CH), lambda i: (0, 0))],
        out_shape=[jax.ShapeDtypeStruct((T, 3 * CH), _MXU), jax.ShapeDtypeStruct((SC_HALO, CH), F32)],
        scratch_shapes=[pltpu.VMEM((tm + SC_HALO, CH), F32), pltpu.VMEM((tm + SC_HALO, CH), F32)],
        compiler_params=_cp("arbitrary"))(zs, zs, zs, dact, dact, w)


SWA_TQ = 256


def _t5_bucket_np(dist):
    max_exact = N_BUCKETS // 2
    d = np.maximum(dist, 1).astype(np.float32)
    large = max_exact + (np.log(d / np.float32(max_exact)) / np.float32(math.log(MAX_DISTANCE / max_exact))
                         * np.float32(N_BUCKETS - max_exact)).astype(np.int32)
    large = np.minimum(large, N_BUCKETS - 1)
    return np.where(dist < max_exact, dist, large).astype(np.int32)


def _swa_bucket_matrix(tq):
    dist = WINDOW + np.arange(tq)[:, None] - np.arange(tq + WINDOW)[None, :]
    ok = (dist >= 0) & (dist < WINDOW)
    return np.where(ok, _t5_bucket_np(np.maximum(dist, 0)), -1).astype(np.int32)


def _kv_expand_matrix():
    e = np.zeros((2 * HEAD, 4 * HEAD), np.float32)
    for h in range(4):
        for d in range(HEAD):
            e[(h // 2) * HEAD + d, h * HEAD + d] = 1.0
    return e


def _swa_bias(rel_bias, bucket, name):
    tq, tk = bucket.shape

    def body(rb_ref, bk_ref, o_ref):
        h = pl.program_id(0)
        bk = bk_ref[...]
        acc = jnp.full((tq, tk), NEG_INF, F32)
        for b in range(N_BUCKETS):
            acc = jnp.where(bk == b, rb_ref[b, h], acc)
        o_ref[0] = acc

    return pl.pallas_call(
        body, name=name, grid=(4,),
        in_specs=[pl.BlockSpec(memory_space=pltpu.SMEM), pl.BlockSpec((tq, tk), lambda h: (0, 0))],
        out_specs=pl.BlockSpec((1, tq, tk), lambda h: (h, 0, 0)),
        out_shape=jax.ShapeDtypeStruct((4, tq, tk), F32), compiler_params=_cp("parallel"))(rel_bias, bucket)


def _swa_probs(qh, kx, bm, first_col, sk):
    s = _dot(qh, kx, _NT) * (HEAD ** -0.5)
    col = lax.broadcasted_iota(jnp.int32, s.shape, 1)
    valid = (bm > 0.5 * NEG_INF) & (col >= first_col)
    s = jnp.where(valid, s + bm, NEG_INF)
    m = jnp.maximum(jnp.max(s, axis=-1, keepdims=True), sk)
    p = jnp.exp(s - m)
    den = jnp.sum(p, axis=-1, keepdims=True) + jnp.exp(sk - m)
    return p / den, m, den


def _swa_fwd(zw, gq, gk, sink, bias, expand, name):
    T = zw.shape[0]
    tq = bias.shape[1]
    r = tq // WINDOW

    def body(z_ref, zh_ref, gq_ref, gk_ref, sink_ref, b_ref, e_ref, o_ref, kext, vext):
        i = pl.program_id(0)
        cur = z_ref[...]
        qn, _ = _head_rms(cur[:, :4 * HEAD], gq_ref[...], 4)
        kc, _ = _head_rms(cur[:, 4 * HEAD:6 * HEAD], gk_ref[...], 2)
        hal = zh_ref[...]
        kp, _ = _head_rms(hal[:, :2 * HEAD], gk_ref[...], 2)
        kext[pl.ds(0, WINDOW), :] = kp
        kext[pl.ds(WINDOW, tq), :] = kc
        vext[pl.ds(0, WINDOW), :] = hal[:, 2 * HEAD:]
        vext[pl.ds(WINDOW, tq), :] = cur[:, 6 * HEAD:]
        kx = _dot(kext[...], e_ref[...]).astype(_MXU)
        vx = _dot(vext[...], e_ref[...]).astype(_MXU)
        first_col = jnp.where(i > 0, 0, WINDOW)
        out = jnp.zeros((tq, 4 * HEAD), F32)
        for h in range(4):
            mk = _lane_mask(4 * HEAD, h)
            qh = jnp.where(mk, qn, 0.0)
            pn, _, _ = _swa_probs(qh, kx, b_ref[h], first_col, sink_ref[0, h])
            out = jnp.where(mk, _dot(pn, vx), out)
        o_ref[...] = out

    return pl.pallas_call(
        body, name=name, grid=(T // tq,),
        in_specs=[pl.BlockSpec((tq, 8 * HEAD), lambda i: (i, 0)),
                  pl.BlockSpec((WINDOW, 4 * HEAD), lambda i: (jnp.maximum(i * r - 1, 0), 1)),
                  pl.BlockSpec((1, 4 * HEAD), lambda i: (0, 0)), pl.BlockSpec((1, 2 * HEAD), lambda i: (0, 0)),
                  pl.BlockSpec(memory_space=pltpu.SMEM),
                  pl.BlockSpec(bias.shape, lambda i: (0, 0, 0)),
                  pl.BlockSpec(expand.shape, lambda i: (0, 0))],
        out_specs=pl.BlockSpec((tq, 4 * HEAD), lambda i: (i, 0)),
        out_shape=jax.ShapeDtypeStruct((T, 4 * HEAD), F32),
        scratch_shapes=[pltpu.VMEM((tq + WINDOW, 2 * HEAD), F32), pltpu.VMEM((tq + WINDOW, 2 * HEAD), F32)],
        compiler_params=_cp("parallel"))(zw, zw, gq, gk, sink, bias, expand)


def _swa_bwd(zw, dact, gq, gk, sink, bias, bucket, expand, name):
    T = zw.shape[0]
    tq = bias.shape[1]
    tk = tq + WINDOW
    r = tq // WINDOW
    nt = T // tq
    nb = T // WINDOW
    scale = HEAD ** -0.5

    def body(z_ref, zh_ref, zn_ref, d_ref, dn_ref, gq_ref, gk_ref, sink_ref, b_ref, bk_ref, e_ref,
             dz_ref, dgq_ref, dgk_ref, dsk_ref, drb_ref, kext, vext, dk_s, dv_s, db_s):
        i = pl.program_id(0)

        @pl.when(i == 0)
        def _():
            dgq_ref[...] = jnp.zeros_like(dgq_ref)
            dgk_ref[...] = jnp.zeros_like(dgk_ref)
            dsk_ref[...] = jnp.zeros_like(dsk_ref)
            drb_ref[...] = jnp.zeros_like(drb_ref)
            db_s[...] = jnp.zeros_like(db_s)

        lane = lax.broadcasted_iota(jnp.int32, (1, LANE), 1)
        cur = z_ref[...]
        q_raw, k_raw = cur[:, :4 * HEAD], cur[:, 4 * HEAD:6 * HEAD]
        qn, q_r = _head_rms(q_raw, gq_ref[...], 4)
        kc, k_r = _head_rms(k_raw, gk_ref[...], 2)
        hal = zh_ref[...]
        kp, _ = _head_rms(hal[:, :2 * HEAD], gk_ref[...], 2)
        kext[pl.ds(0, WINDOW), :] = kp
        kext[pl.ds(WINDOW, tq), :] = kc
        vext[pl.ds(0, WINDOW), :] = hal[:, 2 * HEAD:]
        vext[pl.ds(WINDOW, tq), :] = cur[:, 6 * HEAD:]
        ev = e_ref[...]
        kx = _dot(kext[...], ev).astype(_MXU)
        vx = _dot(vext[...], ev).astype(_MXU)
        first_col = jnp.where(i > 0, 0, WINDOW)
        do = d_ref[...]
        dq = jnp.zeros((tq, 4 * HEAD), F32)
        dkx = jnp.zeros((tk, 4 * HEAD), F32)
        dvx = jnp.zeros((tk, 4 * HEAD), F32)
        dsk = jnp.zeros((1, LANE), F32)
        for h in range(4):
            mk = _lane_mask(4 * HEAD, h)
            qh = jnp.where(mk, qn, 0.0).astype(_MXU)
            sk = sink_ref[0, h]
            pn, m, den = _swa_probs(qh, kx, b_ref[h], first_col, sk)
            doh = jnp.where(mk, do, 0.0).astype(_MXU)
            dpn = _dot(doh, vx, _NT)
            delta = jnp.sum(pn * dpn, axis=-1, keepdims=True)
            ds = pn * (dpn - delta)
            psink = jnp.exp(sk - m) / den
            dsk = dsk + jnp.where(lane == h, jnp.sum(-psink * delta, axis=0, keepdims=True), 0.0)
            db_s[h] += ds
            dss = (ds * scale).astype(_MXU)
            dq = dq + jnp.where(mk, _dot(dss, kx), 0.0)
            dkx = dkx + _dot(dss, qh, _TN)
            dvx = dvx + _dot(pn, doh, _TN)
        dsk_ref[...] += dsk
        dk_ext = _exact_dot(dkx, ev, _NT, "a")
        dv_ext = _exact_dot(dvx, ev, _NT, "a")
        dk_s[...] = dk_ext[WINDOW:, :]
        dv_s[...] = dv_ext[WINDOW:, :]

        @pl.when(i < nt - 1)
        def _():
            nxt = zn_ref[...]
            q2, _ = _head_rms(nxt[:, :4 * HEAD], gq_ref[...], 4)
            k2n, _ = _head_rms(nxt[:, 4 * HEAD:6 * HEAD], gk_ref[...], 2)
            k2 = jnp.concatenate([kc[tq - WINDOW:, :], k2n], axis=0)
            v2 = jnp.concatenate([cur[tq - WINDOW:, 6 * HEAD:], nxt[:, 6 * HEAD:]], axis=0)
            k2x = _dot(k2, ev).astype(_MXU)
            v2x = _dot(v2, ev).astype(_MXU)
            do2 = dn_ref[...]
            dk2x = jnp.zeros((2 * WINDOW, 4 * HEAD), F32)
            dv2x = jnp.zeros((2 * WINDOW, 4 * HEAD), F32)
            for h in range(4):
                mk = _lane_mask(4 * HEAD, h)
                qh = jnp.where(mk, q2, 0.0).astype(_MXU)
                pn, _, _ = _swa_probs(qh, k2x, b_ref[h][:WINDOW, :2 * WINDOW], 0, sink_ref[0, h])
                doh = jnp.where(mk, do2, 0.0).astype(_MXU)
                dpn = _dot(doh, v2x, _NT)
                ds = pn * (dpn - jnp.sum(pn * dpn, axis=-1, keepdims=True))
                dk2x = dk2x + _dot((ds * scale).astype(_MXU), qh, _TN)
                dv2x = dv2x + _dot(pn, doh, _TN)
            dk_s[pl.ds(tq - WINDOW, WINDOW), :] += _exact_dot(dk2x, ev, _NT, "a")[:WINDOW, :]
            dv_s[pl.ds(tq - WINDOW, WINDOW), :] += _exact_dot(dv2x, ev, _NT, "a")[:WINDOW, :]

        dq_raw, dgq = _head_rms_bwd(dq, q_raw, q_r, gq_ref[...], 4)
        dk_raw, dgk = _head_rms_bwd(dk_s[...], k_raw, k_r, gk_ref[...], 2)
        dgq_ref[...] += dgq
        dgk_ref[...] += dgk
        dz_ref[:, :4 * HEAD] = dq_raw.astype(dz_ref.dtype)
        dz_ref[:, 4 * HEAD:6 * HEAD] = dk_raw.astype(dz_ref.dtype)
        dz_ref[:, 6 * HEAD:] = dv_s[...].astype(dz_ref.dtype)

        @pl.when(i == nt - 1)
        def _():
            bk = bk_ref[...]
            for b in range(N_BUCKETS):
                rowv = jnp.zeros((1, LANE), F32)
                for h in range(4):
                    s1 = jnp.sum(jnp.where(bk == b, db_s[h], 0.0), axis=0, keepdims=True)
                    rowv = jnp.where(lane == h, jnp.sum(s1, axis=1, keepdims=True), rowv)
                drb_ref[pl.ds(b, 1), :] = rowv

    const2 = lambda i: (0, 0)
    return pl.pallas_call(
        body, name=name, grid=(nt,),
        in_specs=[pl.BlockSpec((tq, 8 * HEAD), lambda i: (i, 0)),
                  pl.BlockSpec((WINDOW, 4 * HEAD), lambda i: (jnp.maximum(i * r - 1, 0), 1)),
                  pl.BlockSpec((WINDOW, 8 * HEAD), lambda i: (jnp.minimum((i + 1) * r, nb - 1), 0)),
                  pl.BlockSpec((tq, 4 * HEAD), lambda i: (i, 0)),
                  pl.BlockSpec((WINDOW, 4 * HEAD), lambda i: (jnp.minimum((i + 1) * r, nb - 1), 0)),
                  pl.BlockSpec((1, 4 * HEAD), const2), pl.BlockSpec((1, 2 * HEAD), const2),
                  pl.BlockSpec(memory_space=pltpu.SMEM),
                  pl.BlockSpec(bias.shape, lambda i: (0, 0, 0)),
                  pl.BlockSpec(bucket.shape, const2), pl.BlockSpec(expand.shape, const2)],
        out_specs=[pl.BlockSpec((tq, 8 * HEAD), lambda i: (i, 0)),
                   pl.BlockSpec((1, 4 * HEAD), const2), pl.BlockSpec((1, 2 * HEAD), const2),
                   pl.BlockSpec((1, LANE), const2), pl.BlockSpec((N_BUCKETS, LANE), const2)],
        out_shape=[jax.ShapeDtypeStruct((T, 8 * HEAD), _MXU), jax.ShapeDtypeStruct((1, 4 * HEAD), F32),
                   jax.ShapeDtypeStruct((1, 2 * HEAD), F32), jax.ShapeDtypeStruct((1, LANE), F32),
                   jax.ShapeDtypeStruct((N_BUCKETS, LANE), F32)],
        scratch_shapes=[pltpu.VMEM((tk, 2 * HEAD), F32), pltpu.VMEM((tk, 2 * HEAD), F32),
                        pltpu.VMEM((tq, 2 * HEAD), F32), pltpu.VMEM((tq, 2 * HEAD), F32),
                        pltpu.VMEM((4, tq, tk), F32)],
        compiler_params=_cp("arbitrary"))(zw, zw, zw, dact, dact, gq, gk, sink, bias, bucket, expand)


FOX_B = 512
FOX_TM = 256


def _tri(n, lower):
    m = np.tril(np.ones((n, n), np.float32)) if lower else np.triu(np.ones((n, n), np.float32))
    return m


def _log_sigmoid(x):
    return jnp.minimum(x, 0.0) - jnp.log1p(jnp.exp(-jnp.abs(x)))


def _fox_prep(zf, gq, gk, bf, name):
    T = zf.shape[0]
    tm = _tile(T, FOX_TM)
    lower = jnp.asarray(_tri(tm, True), _MXU)

    def body(z_ref, gq_ref, gk_ref, bf_ref, l_ref, q_ref, k_ref, v_ref, f_ref, ft_ref, carry):
        @pl.when(pl.program_id(0) == 0)
        def _():
            carry[...] = jnp.zeros_like(carry)

        z = z_ref[...]
        q, _ = _head_rms(z[:, :CH], gq_ref[...], 4)
        k, _ = _head_rms(z[:, CH:2 * CH], gk_ref[...], 4)
        q_ref[...] = q.astype(q_ref.dtype)
        k_ref[...] = k.astype(k_ref.dtype)
        v_ref[...] = z[:, 2 * CH:3 * CH].astype(v_ref.dtype)
        lane = lax.broadcasted_iota(jnp.int32, (1, LANE), 1)
        lf = jnp.where(lane < 4, _log_sigmoid(z[:, 3 * CH:] + bf_ref[...]), 0.0)
        fv = _exact_dot(l_ref[...], lf, _NN, "b") + carry[pl.ds(0, 1), :]
        f_ref[...] = fv
        ft_ref[...] = fv.T
        carry[pl.ds(0, 1), :] = f_ref[pl.ds(tm - 1, 1), :]

    row = pl.BlockSpec((tm, CH), lambda i: (i, 0))
    vec = pl.BlockSpec((1, CH), lambda i: (0, 0))
    qsh = jax.ShapeDtypeStruct((T, CH), _MXU)
    return pl.pallas_call(
        body, name=name, grid=(T // tm,),
        in_specs=[pl.BlockSpec((tm, 3 * CH + LANE), lambda i: (i, 0)), vec, vec,
                  pl.BlockSpec((1, LANE), lambda i: (0, 0)), pl.BlockSpec((tm, tm), lambda i: (0, 0))],
        out_specs=[row, row, row, pl.BlockSpec((tm, LANE), lambda i: (i, 0)), pl.BlockSpec((LANE, tm), lambda i: (0, i))],
        out_shape=[qsh, qsh, qsh, jax.ShapeDtypeStruct((T, LANE), F32), jax.ShapeDtypeStruct((LANE, T), F32)],
        scratch_shapes=[pltpu.VMEM((8, LANE), F32)],
        compiler_params=_cp("arbitrary"))(zf, gq, gk, bf, lower)


def _lane_col(x, h):
    lane = lax.broadcasted_iota(jnp.int32, (1, x.shape[-1]), 1)
    return jnp.sum(jnp.where(lane == h, x, 0.0), axis=-1, keepdims=True)


def _fox_scores(qh, k, fq, ft_ref, h, qi, ki, B):
    s = _dot(qh, k, _NT) * (HEAD ** -0.5)
    s = s + (fq - ft_ref[pl.ds(h, 1), :])
    row = qi * B + lax.broadcasted_iota(jnp.int32, s.shape, 0)
    col = ki * B + lax.broadcasted_iota(jnp.int32, s.shape, 1)
    return jnp.where(col <= row, s, NEG_INF)


def _fox_fwd(q, k, v, f, ft, name):
    T = q.shape[0]
    B = _tile(T, FOX_B)
    n = T // B

    def body(q_ref, k_ref, v_ref, f_ref, ft_ref, o_ref, lse_ref, m_s, l_s, acc):
        qi, ki = pl.program_id(0), pl.program_id(1)

        @pl.when(ki == 0)
        def _():
            m_s[...] = jnp.full_like(m_s, NEG_INF)
            l_s[...] = jnp.zeros_like(l_s)
            acc[...] = jnp.zeros_like(acc)

        @pl.when(ki <= qi)
        def _():
            qv, kv, vv, fv = q_ref[...], k_ref[...], v_ref[...], f_ref[...]
            for h in range(4):
                mk = _lane_mask(CH, h)
                qh = jnp.where(mk, qv, jnp.zeros_like(qv))
                s = _fox_scores(qh, kv, _lane_col(fv, h), ft_ref, h, qi, ki, B)
                m_old = m_s[h]
                m_new = jnp.maximum(m_old, jnp.max(s, axis=-1, keepdims=True))
                alpha = jnp.exp(m_old - m_new)
                p = jnp.exp(s - m_new)
                l_s[h] = alpha * l_s[h] + jnp.sum(p, axis=-1, keepdims=True)
                m_s[h] = m_new
                acc[...] = jnp.where(mk, acc[...] * alpha + _dot(p, vv), acc[...])

        @pl.when(ki == qi)
        def _():
            lane = lax.broadcasted_iota(jnp.int32, (1, LANE), 1)
            out = acc[...]
            lse = jnp.zeros((B, LANE), F32)
            for h in range(4):
                out = jnp.where(_lane_mask(CH, h), out / l_s[h], out)
                lse = jnp.where(lane == h, m_s[h] + jnp.log(l_s[h]), lse)
            o_ref[...] = out
            lse_ref[...] = lse

    qspec = pl.BlockSpec((B, CH), lambda qi, ki: (qi, 0))
    kspec = pl.BlockSpec((B, CH), lambda qi, ki: (jnp.minimum(ki, qi), 0))
    return pl.pallas_call(
        body, name=name, grid=(n, n),
        in_specs=[qspec, kspec, kspec, pl.BlockSpec((B, LANE), lambda qi, ki: (qi, 0)),
                  pl.BlockSpec((8, B), lambda qi, ki: (0, jnp.minimum(ki, qi)))],
        out_specs=[qspec, pl.BlockSpec((B, LANE), lambda qi, ki: (qi, 0))],
        out_shape=[jax.ShapeDtypeStruct((T, CH), F32), jax.ShapeDtypeStruct((T, LANE), F32)],
        scratch_shapes=[pltpu.VMEM((4, B, 1), F32), pltpu.VMEM((4, B, 1), F32), pltpu.VMEM((B, CH), F32)],
        compiler_params=_cp("parallel", "arbitrary"))(q, k, v, f, ft)


def _fox_delta(o, do, name):
    T = o.shape[0]
    tm = _tile(T, ROW_TILE)

    def body(o_ref, d_ref, out_ref):
        prod = o_ref[...] * d_ref[...]
        lane = lax.broadcasted_iota(jnp.int32, (1, LANE), 1)
        out = jnp.zeros((tm, LANE), F32)
        for h in range(4):
            s = jnp.sum(jnp.where(_lane_mask(CH, h), prod, 0.0), axis=-1, keepdims=True)
            out = jnp.where(lane == h, s, out)
        out_ref[...] = out

    row = pl.BlockSpec((tm, CH), lambda i: (i, 0))
    return pl.pallas_call(
        body, name=name, grid=(T // tm,), in_specs=[row, row],
        out_specs=pl.BlockSpec((tm, LANE), lambda i: (i, 0)),
        out_shape=jax.ShapeDtypeStruct((T, LANE), F32), compiler_params=_cp("parallel"))(o, do)


def _fox_bwd_dq(q, k, v, f, ft, lse, delta, do, name):
    T = q.shape[0]
    B = _tile(T, FOX_B)
    n = T // B

    def body(q_ref, k_ref, v_ref, f_ref, ft_ref, lse_ref, dl_ref, do_ref, dq_ref, dfq_ref, dq_s, df_s):
        qi, ki = pl.program_id(0), pl.program_id(1)

        @pl.when(ki == 0)
        def _():
            dq_s[...] = jnp.zeros_like(dq_s)
            df_s[...] = jnp.zeros_like(df_s)

        @pl.when(ki <= qi)
        def _():
            qv, kv, vv, fv = q_ref[...], k_ref[...], v_ref[...], f_ref[...]
            lsev, dlv, dov = lse_ref[...], dl_ref[...], do_ref[...]
            lane = lax.broadcasted_iota(jnp.int32, (1, LANE), 1)
            for h in range(4):
                mk = _lane_mask(CH, h)
                qh = jnp.where(mk, qv, jnp.zeros_like(qv))
                s = _fox_scores(qh, kv, _lane_col(fv, h), ft_ref, h, qi, ki, B)
                p = jnp.exp(s - _lane_col(lsev, h))
                doh = jnp.where(mk, dov, 0.0)
                ds = p * (_dot(doh, vv, _NT) - _lane_col(dlv, h))
                dq_s[...] += jnp.where(mk, _dot(ds * (HEAD ** -0.5), kv), 0.0)
                df_s[...] += jnp.where(lane == h, jnp.sum(ds, axis=-1, keepdims=True), 0.0)

        @pl.when(ki == qi)
        def _():
            dq_ref[...] = dq_s[...]
            dfq_ref[...] = df_s[...]

    qspec = pl.BlockSpec((B, CH), lambda qi, ki: (qi, 0))
    kspec = pl.BlockSpec((B, CH), lambda qi, ki: (jnp.minimum(ki, qi), 0))
    lspec = pl.BlockSpec((B, LANE), lambda qi, ki: (qi, 0))
    return pl.pallas_call(
        body, name=name, grid=(n, n),
        in_specs=[qspec, kspec, kspec, lspec, pl.BlockSpec((8, B), lambda qi, ki: (0, jnp.minimum(ki, qi))),
                  lspec, lspec, qspec],
        out_specs=[qspec, lspec],
        out_shape=[jax.ShapeDtypeStruct((T, CH), F32), jax.ShapeDtypeStruct((T, LANE), F32)],
        scratch_shapes=[pltpu.VMEM((B, CH), F32), pltpu.VMEM((B, LANE), F32)],
        compiler_params=_cp("parallel", "arbitrary"))(q, k, v, f, ft, lse, delta, do)


def _fox_bwd_dkv(q, k, v, f, ft, lse, delta, do, name):
    T = q.shape[0]
    B = _tile(T, FOX_B)
    n = T // B

    def body(q_ref, k_ref, v_ref, f_ref, ft_ref, lse_ref, dl_ref, do_ref, dk_ref, dv_ref, dft_ref, dk_s, dv_s, df_s):
        ki, qi = pl.program_id(0), pl.program_id(1)

        @pl.when(qi == 0)
        def _():
            dk_s[...] = jnp.zeros_like(dk_s)
            dv_s[...] = jnp.zeros_like(dv_s)
            df_s[...] = jnp.zeros_like(df_s)

        @pl.when(qi >= ki)
        def _():
            qv, kv, vv, fv = q_ref[...], k_ref[...], v_ref[...], f_ref[...]
            lsev, dlv, dov = lse_ref[...], dl_ref[...], do_ref[...]
            for h in range(4):
                mk = _lane_mask(CH, h)
                qh = jnp.where(mk, qv, jnp.zeros_like(qv))
                s = _fox_scores(qh, kv, _lane_col(fv, h), ft_ref, h, qi, ki, B)
                p = jnp.exp(s - _lane_col(lsev, h))
                doh = jnp.where(mk, dov, 0.0)
                ds = p * (_dot(doh, vv, _NT) - _lane_col(dlv, h))
                dv_s[...] += _dot(p, doh, _TN)
                dk_s[...] += _dot(ds * (HEAD ** -0.5), qh, _TN)
                df_s[pl.ds(h, 1), :] -= jnp.sum(ds, axis=0, keepdims=True)

        @pl.when(qi == n - 1)
        def _():
            dk_ref[...] = dk_s[...]
            dv_ref[...] = dv_s[...]
            dft_ref[...] = jnp.zeros_like(dft_ref)
            dft_ref[pl.ds(0, 8), :] = df_s[...]

    qspec = pl.BlockSpec((B, CH), lambda ki, qi: (jnp.maximum(qi, ki), 0))
    kspec = pl.BlockSpec((B, CH), lambda ki, qi: (ki, 0))
    lspec = pl.BlockSpec((B, LANE), lambda ki, qi: (jnp.maximum(qi, ki), 0))
    return pl.pallas_call(
        body, name=name, grid=(n, n),
        in_specs=[qspec, kspec, kspec, lspec, pl.BlockSpec((8, B), lambda ki, qi: (0, ki)), lspec, lspec, qspec],
        out_specs=[kspec, kspec, pl.BlockSpec((LANE, B), lambda ki, qi: (0, ki))],
        out_shape=[jax.ShapeDtypeStruct((T, CH), F32), jax.ShapeDtypeStruct((T, CH), F32),
                   jax.ShapeDtypeStruct((LANE, T), F32)],
        scratch_shapes=[pltpu.VMEM((B, CH), F32), pltpu.VMEM((B, CH), F32), pltpu.VMEM((8, B), F32)],
        compiler_params=_cp("parallel", "arbitrary"))(q, k, v, f, ft, lse, delta, do)


def _fox_post(zf, dqn, dkn, dv, dfq, dft, gq, gk, bf, name):
    T = zf.shape[0]
    tm = _tile(T, FOX_TM)
    nt = T // tm
    upper = jnp.asarray(_tri(tm, False), _MXU)

    def body(z_ref, dq_ref, dk_ref, dv_ref, dfq_ref, dft_ref, gq_ref, gk_ref, bf_ref, u_ref, dz_ref, sm_ref, carry, rc_s):
        @pl.when(pl.program_id(0) == 0)
        def _():
            carry[...] = jnp.zeros_like(carry)
            sm_ref[...] = jnp.zeros_like(sm_ref)

        z = z_ref[...]
        q_raw, k_raw = z[:, :CH], z[:, CH:2 * CH]
        _, q_r = _head_rms(q_raw, gq_ref[...], 4)
        _, k_r = _head_rms(k_raw, gk_ref[...], 4)
        dq, dgq = _head_rms_bwd(dq_ref[...], q_raw, q_r, gq_ref[...], 4)
        dk, dgk = _head_rms_bwd(dk_ref[...], k_raw, k_r, gk_ref[...], 4)
        df = dfq_ref[...] + dft_ref[...].T
        rc_s[...] = _exact_dot(u_ref[...], df, _NN, "b") + carry[pl.ds(0, 1), :]
        carry[pl.ds(0, 1), :] = rc_s[pl.ds(0, 1), :]
        lane = lax.broadcasted_iota(jnp.int32, (1, LANE), 1)
        x = z[:, 3 * CH:] + bf_ref[...]
        dff = jnp.where(lane < 4, rc_s[...] * _sigmoid(-x), 0.0)
        dz_ref[:, :CH] = dq.astype(dz_ref.dtype)
        dz_ref[:, CH:2 * CH] = dk.astype(dz_ref.dtype)
        dz_ref[:, 2 * CH:3 * CH] = dv_ref[...].astype(dz_ref.dtype)
        dz_ref[:, 3 * CH:] = dff.astype(dz_ref.dtype)
        sm_ref[pl.ds(0, 1), :] += dgq
        sm_ref[pl.ds(1, 1), :] += dgk
        sm_ref[pl.ds(2, 1), :LANE] += jnp.sum(dff, axis=0, keepdims=True)

    rev = lambda i: (nt - 1 - i, 0)
    row = pl.BlockSpec((tm, CH), rev)
    lrow = pl.BlockSpec((tm, LANE), rev)
    vec = pl.BlockSpec((1, CH), lambda i: (0, 0))
    return pl.pallas_call(
        body, name=name, grid=(nt,),
        in_specs=[pl.BlockSpec((tm, 3 * CH + LANE), rev), row, row, row, lrow,
                  pl.BlockSpec((LANE, tm), lambda i: (0, nt - 1 - i)), vec, vec,
                  pl.BlockSpec((1, LANE), lambda i: (0, 0)), pl.BlockSpec((tm, tm), lambda i: (0, 0))],
        out_specs=[pl.BlockSpec((tm, 3 * CH + LANE), rev), pl.BlockSpec((8, CH), lambda i: (0, 0))],
        out_shape=[jax.ShapeDtypeStruct((T, 3 * CH + LANE), _MXU), jax.ShapeDtypeStruct((8, CH), F32)],
        scratch_shapes=[pltpu.VMEM((8, LANE), F32), pltpu.VMEM((tm, LANE), F32)],
        compiler_params=_cp("arbitrary"))(zf, dqn, dkn, dv, dfq, dft, gq, gk, bf, upper)


AUG_F, AUG_ONE, AUG_LSE = HEAD, HEAD + 3, HEAD + 6


def _pieces(x):
    hi = x.astype(_MXU).astype(F32)
    r1 = x - hi
    mid = r1.astype(_MXU).astype(F32)
    lo = (r1 - mid).astype(_MXU).astype(F32)
    return hi, mid, lo


def _put_pieces(base, first_lane, x, sign):
    lane = lax.broadcasted_iota(jnp.int32, (1, LANE), 1)
    for j, piece in enumerate(_pieces(x)):
        base = jnp.where(lane == first_lane + j, sign * piece, base)
    return base


def _head_select_matrix():
    p = np.zeros((4, 4 * HEAD, LANE), np.float32)
    for h in range(4):
        for d in range(HEAD):
            p[h, h * HEAD + d, d] = 1.0
    return p


def _tri_steps(n, by_key):
    if by_key:
        pairs = [(q, k) for k in range(n) for q in range(k, n)]
    else:
        pairs = [(q, k) for q in range(n) for k in range(q + 1)]
    return (jnp.asarray([p[0] for p in pairs], jnp.int32), jnp.asarray([p[1] for p in pairs], jnp.int32))


def _fox2_prep(zf, gq, gk, bf, sel, name):
    T = zf.shape[0]
    tm = _tile(T, FOX_TM)
    lower = jnp.asarray(_tri(tm, True), _MXU)

    def body(z_ref, gq_ref, gk_ref, bf_ref, l_ref, p_ref, qa_ref, ka_ref, va_ref, vat_ref, carry, f_s):
        @pl.when(pl.program_id(0) == 0)
        def _():
            carry[...] = jnp.zeros_like(carry)

        z = z_ref[...]
        q, _ = _head_rms(z[:, :CH], gq_ref[...], 4)
        k, _ = _head_rms(z[:, CH:2 * CH], gk_ref[...], 4)
        q = (q * (HEAD ** -0.5)).astype(_MXU)
        k = k.astype(_MXU)
        v = z[:, 2 * CH:3 * CH].astype(_MXU)
        lane = lax.broadcasted_iota(jnp.int32, (1, LANE), 1)
        lf = jnp.where(lane < 4, _log_sigmoid(z[:, 3 * CH:] + bf_ref[...]), 0.0)
        f_s[...] = _exact_dot(l_ref[...], lf, _NN, "b") + carry[pl.ds(0, 1), :]
        carry[pl.ds(0, 1), :] = f_s[pl.ds(tm - 1, 1), :]
        fv = f_s[...]
        q_ones = (lane >= AUG_ONE) & (lane < AUG_ONE + 3)
        k_ones = ((lane >= AUG_F) & (lane < AUG_F + 3)) | ((lane >= AUG_LSE) & (lane < AUG_LSE + 3))
        v_ones = (lane >= AUG_F) & (lane < AUG_F + 3)
        for h in range(4):
            fh = _lane_col(fv, h)
            qa = jnp.where(q_ones, 1.0, _dot(q, p_ref[h]))
            qa_ref[h] = _put_pieces(qa, AUG_F, fh, 1.0).astype(qa_ref.dtype)
            ka = jnp.where(k_ones, 1.0, _dot(k, p_ref[h]))
            ka_ref[h] = _put_pieces(ka, AUG_ONE, fh, -1.0).astype(ka_ref.dtype)
            va = jnp.where(v_ones, 1.0, _dot(v, p_ref[h]))
            va_ref[h] = va.astype(va_ref.dtype)
            vat_ref[h] = va.T.astype(vat_ref.dtype)

    vec = pl.BlockSpec((1, CH), lambda i: (0, 0))
    hspec = pl.BlockSpec((4, tm, LANE), lambda i: (0, i, 0))
    hsh = jax.ShapeDtypeStruct((4, T, LANE), _MXU)
    return pl.pallas_call(
        body, name=name, grid=(T // tm,),
        in_specs=[pl.BlockSpec((tm, 3 * CH + LANE), lambda i: (i, 0)), vec, vec,
                  pl.BlockSpec((1, LANE), lambda i: (0, 0)), pl.BlockSpec((tm, tm), lambda i: (0, 0)),
                  pl.BlockSpec(sel.shape, lambda i: (0, 0, 0))],
        out_specs=[hspec, hspec, hspec, pl.BlockSpec((4, LANE, tm), lambda i: (0, 0, i))],
        out_shape=[hsh, hsh, hsh, jax.ShapeDtypeStruct((4, LANE, T), _MXU)],
        scratch_shapes=[pltpu.VMEM((8, LANE), F32), pltpu.VMEM((tm, LANE), F32)],
        compiler_params=_cp("arbitrary"))(zf, gq, gk, bf, lower, sel)


def _causal(s, transposed):
    row = lax.broadcasted_iota(jnp.int32, s.shape, 0)
    col = lax.broadcasted_iota(jnp.int32, s.shape, 1)
    return jnp.where((row <= col) if transposed else (col <= row), s, NEG_INF)


def _mxu_dot(a, b, dims):
    return lax.dot_general(a, b, dims, preferred_element_type=F32)


def _fox2_fwd(qa, ka, vat, sel, name):
    T = qa.shape[1]
    B = _tile(T, FOX_B)
    n = T // B
    qt, kt = _tri_steps(n, False)

    def body(qt_ref, kt_ref, qa_ref, ka_ref, vat_ref, p_ref, o_ref, qb_ref, m_s, acc):
        step = pl.program_id(0)
        qi, ki = qt_ref[step], kt_ref[step]

        @pl.when(ki == 0)
        def _():
            m_s[...] = jnp.full_like(m_s, NEG_INF)
            acc[...] = jnp.zeros_like(acc)

        def update(diag):
            for h in range(4):
                st = _mxu_dot(ka_ref[h], qa_ref[h], _NT)
                if diag:
                    st = _causal(st, True)
                m_old = m_s[h, pl.ds(0, 1), :]
                m_new = jnp.maximum(m_old, jnp.max(st, axis=0, keepdims=True))
                pt = jnp.exp(st - m_new)
                acc[h] = acc[h] * jnp.exp(m_old - m_new) + _dot(vat_ref[h], pt)
                m_s[h, pl.ds(0, 1), :] = m_new

        @pl.when(ki < qi)
        def _():
            update(False)

        @pl.when(ki == qi)
        def _():
            update(True)
            row = lax.broadcasted_iota(jnp.int32, (LANE, 1), 0)
            out = jnp.zeros((B, CH), F32)
            for h in range(4):
                a = acc[h]
                l = a[AUG_F:AUG_F + 1, :]
                out = out + _exact_dot((a / l).T, p_ref[h], _NT, "a")
                lse = m_s[h, pl.ds(0, 1), :] + jnp.log(l)
                qbt = qa_ref[h].astype(F32).T
                for j, piece in enumerate(_pieces(lse)):
                    qbt = jnp.where(row == AUG_LSE + j, -piece, qbt)
                qb_ref[h] = qbt.T.astype(qb_ref.dtype)
            o_ref[...] = out

    qspec = pl.BlockSpec((4, B, LANE), lambda s, qt, kt: (0, qt[s], 0))
    kspec = pl.BlockSpec((4, B, LANE), lambda s, qt, kt: (0, kt[s], 0))
    grid_spec = pltpu.PrefetchScalarGridSpec(
        num_scalar_prefetch=2, grid=(qt.shape[0],),
        in_specs=[qspec, kspec, pl.BlockSpec((4, LANE, B), lambda s, qt, kt: (0, 0, kt[s])),
                  pl.BlockSpec(sel.shape, lambda s, qt, kt: (0, 0, 0))],
        out_specs=[pl.BlockSpec((B, CH), lambda s, qt, kt: (qt[s], 0)), qspec],
        scratch_shapes=[pltpu.VMEM((4, 8, B), F32), pltpu.VMEM((4, LANE, B), F32)])
    return pl.pallas_call(
        body, name=name, grid_spec=grid_spec,
        out_shape=[jax.ShapeDtypeStruct((T, CH), F32), jax.ShapeDtypeStruct((4, T, LANE), _MXU)],
        compiler_params=_cp("arbitrary"))(qt, kt, qa, ka, vat, sel)


def _fox2_bwd_prep(o, do, sel, name):
    T = o.shape[0]
    tm = _tile(T, ROW_TILE)

    def body(o_ref, d_ref, p_ref, out_ref):
        dov = d_ref[...]
        prod = o_ref[...] * dov
        dob = dov.astype(_MXU)
        for h in range(4):
            delta = jnp.sum(jnp.where(_lane_mask(CH, h), prod, 0.0), axis=-1, keepdims=True)
            out_ref[h] = _put_pieces(_dot(dob, p_ref[h]), AUG_F, delta, -1.0).astype(out_ref.dtype)

    row = pl.BlockSpec((tm, CH), lambda i: (i, 0))
    return pl.pallas_call(
        body, name=name, grid=(T // tm,),
        in_specs=[row, row, pl.BlockSpec(sel.shape, lambda i: (0, 0, 0))],
        out_specs=pl.BlockSpec((4, tm, LANE), lambda i: (0, i, 0)),
        out_shape=jax.ShapeDtypeStruct((4, T, LANE), _MXU), compiler_params=_cp("parallel"))(o, do, sel)


def _fox2_bwd_dq(qb, ka, va, doa, sel, name):
    T = qb.shape[1]
    B = _tile(T, FOX_B)
    n = T // B
    qt, kt = _tri_steps(n, False)

    def body(qt_ref, kt_ref, qb_ref, ka_ref, va_ref, do_ref, p_ref, dq_ref, dfq_ref, dq_s):
        step = pl.program_id(0)
        qi, ki = qt_ref[step], kt_ref[step]

        @pl.when(ki == 0)
        def _():
            dq_s[...] = jnp.zeros_like(dq_s)

        def update(diag):
            for h in range(4):
                s = _mxu_dot(qb_ref[h], ka_ref[h], _NT)
                if diag:
                    s = _causal(s, False)
                ds = jnp.exp(s) * _mxu_dot(do_ref[h], va_ref[h], _NT)
                dq_s[h] += _dot(ds, ka_ref[h])

        @pl.when(ki < qi)
        def _():
            update(False)

        @pl.when(ki == qi)
        def _():
            update(True)
            lane = lax.broadcasted_iota(jnp.int32, (1, LANE), 1)
            out = jnp.zeros((B, CH), F32)
            dfq = jnp.zeros((B, LANE), F32)
            for h in range(4):
                out = out + _exact_dot(dq_s[h] * (HEAD ** -0.5), p_ref[h], _NT, "a")
                dfq = jnp.where(lane == h, _lane_col(dq_s[h], AUG_F), dfq)
            dq_ref[...] = out
            dfq_ref[...] = dfq

    qspec = pl.BlockSpec((4, B, LANE), lambda s, qt, kt: (0, qt[s], 0))
    kspec = pl.BlockSpec((4, B, LANE), lambda s, qt, kt: (0, kt[s], 0))
    grid_spec = pltpu.PrefetchScalarGridSpec(
        num_scalar_prefetch=2, grid=(qt.shape[0],),
        in_specs=[qspec, kspec, kspec, qspec, pl.BlockSpec(sel.shape, lambda s, qt, kt: (0, 0, 0))],
        out_specs=[pl.BlockSpec((B, CH), lambda s, qt, kt: (qt[s], 0)),
                   pl.BlockSpec((B, LANE), lambda s, qt, kt: (qt[s], 0))],
        scratch_shapes=[pltpu.VMEM((4, B, LANE), F32)])
    return pl.pallas_call(
        body, name=name, grid_spec=grid_spec,
        out_shape=[jax.ShapeDtypeStruct((T, CH), F32), jax.ShapeDtypeStruct((T, LANE), F32)],
        compiler_params=_cp("arbitrary"))(qt, kt, qb, ka, va, doa, sel)


def _fox2_bwd_dkv(qb, ka, va, doa, sel, name):
    T = qb.shape[1]
    B = _tile(T, FOX_B)
    n = T // B
    qt, kt = _tri_steps(n, True)

    def body(qt_ref, kt_ref, qb_ref, ka_ref, va_ref, do_ref, p_ref, dk_ref, dv_ref, df_ref, dk_s, dv_s):
        step = pl.program_id(0)
        qi, ki = qt_ref[step], kt_ref[step]

        @pl.when(qi == ki)
        def _():
            dk_s[...] = jnp.zeros_like(dk_s)
            dv_s[...] = jnp.zeros_like(dv_s)

        def update(diag):
            for h in range(4):
                st = _mxu_dot(ka_ref[h], qb_ref[h], _NT)
                if diag:
                    st = _causal(st, True)
                pt = jnp.exp(st)
                dst = pt * _mxu_dot(va_ref[h], do_ref[h], _NT)
                dv_s[h] += _dot(pt, do_ref[h])
                dk_s[h] += _dot(dst, qb_ref[h])

        @pl.when(qi == ki)
        def _():
            update(True)

        @pl.when(qi > ki)
        def _():
            update(False)

        @pl.when(qi == n - 1)
        def _():
            lane = lax.broadcasted_iota(jnp.int32, (1, LANE), 1)
            dk = jnp.zeros((B, CH), F32)
            dv = jnp.zeros((B, CH), F32)
            dfk = jnp.zeros((B, LANE), F32)
            for h in range(4):
                dk = dk + _exact_dot(dk_s[h], p_ref[h], _NT, "a")
                dv = dv + _exact_dot(dv_s[h], p_ref[h], _NT, "a")
                dfk = jnp.where(lane == h, -_lane_col(dk_s[h], AUG_ONE), dfk)
            dk_ref[...] = dk
            dv_ref[...] = dv
            df_ref[...] = dfk

    qspec = pl.BlockSpec((4, B, LANE), lambda s, qt, kt: (0, qt[s], 0))
    kspec = pl.BlockSpec((4, B, LANE), lambda s, qt, kt: (0, kt[s], 0))
    ospec = pl.BlockSpec((B, CH), lambda s, qt, kt: (kt[s], 0))
    grid_spec = pltpu.PrefetchScalarGridSpec(
        num_scalar_prefetch=2, grid=(qt.shape[0],),
        in_specs=[qspec, kspec, kspec, qspec, pl.BlockSpec(sel.shape, lambda s, qt, kt: (0, 0, 0))],
        out_specs=[ospec, ospec, pl.BlockSpec((B, LANE), lambda s, qt, kt: (kt[s], 0))],
        scratch_shapes=[pltpu.VMEM((4, B, LANE), F32), pltpu.VMEM((4, B, LANE), F32)])
    return pl.pallas_call(
        body, name=name, grid_spec=grid_spec,
        out_shape=[jax.ShapeDtypeStruct((T, CH), F32), jax.ShapeDtypeStruct((T, CH), F32),
                   jax.ShapeDtypeStruct((T, LANE), F32)],
        compiler_params=_cp("arbitrary"))(qt, kt, qb, ka, va, doa, sel)


def _fox2_bwd(qb, ka, va, doa, sel, name):
    T = qb.shape[1]
    B = _tile(T, FOX_B)
    n = T // B
    qt, kt = _tri_steps(n, True)
    nsteps = qt.shape[0]

    def body(qt_ref, kt_ref, qb_ref, ka_ref, va_ref, do_ref, p_ref, dk_ref, dv_ref, df_ref, dq_hbm,
             dk_s, dv_s, kat_s, dq_s, sem):
        step = pl.program_id(0)
        qi, ki = qt_ref[step], kt_ref[step]

        @pl.when(step == 0)
        def _():
            dq_s[...] = jnp.zeros_like(dq_s)

        @pl.when(qi == ki)
        def _():
            dk_s[...] = jnp.zeros_like(dk_s)
            dv_s[...] = jnp.zeros_like(dv_s)
            for h in range(4):
                kat_s[h] = ka_ref[h].astype(F32).T.astype(kat_s.dtype)

        def update(diag):
            for h in range(4):
                st = _mxu_dot(ka_ref[h], qb_ref[h], _NT)
                if diag:
                    st = _causal(st, True)
                pt = jnp.exp(st)
                dst = (pt * _mxu_dot(va_ref[h], do_ref[h], _NT)).astype(_MXU)
                dv_s[h] += _dot(pt, do_ref[h])
                dk_s[h] += _mxu_dot(dst, qb_ref[h], _NN)
                dq_s[qi, h] += _mxu_dot(kat_s[h], dst, _NN)

        @pl.when(qi == ki)
        def _():
            update(True)

        @pl.when(qi > ki)
        def _():
            update(False)

        @pl.when(qi == n - 1)
        def _():
            lane = lax.broadcasted_iota(jnp.int32, (1, LANE), 1)
            dk = jnp.zeros((B, CH), F32)
            dv = jnp.zeros((B, CH), F32)
            dfk = jnp.zeros((B, LANE), F32)
            for h in range(4):
                dk = dk + _exact_dot(dk_s[h], p_ref[h], _NT, "a")
                dv = dv + _exact_dot(dv_s[h], p_ref[h], _NT, "a")
                dfk = jnp.where(lane == h, -_lane_col(dk_s[h], AUG_ONE), dfk)
            dk_ref[...] = dk
            dv_ref[...] = dv
            df_ref[...] = dfk

        @pl.when(step == nsteps - 1)
        def _():
            cp = pltpu.make_async_copy(dq_s, dq_hbm, sem)
            cp.start()
            cp.wait()

    qspec = pl.BlockSpec((4, B, LANE), lambda s, qt, kt: (0, qt[s], 0))
    kspec = pl.BlockSpec((4, B, LANE), lambda s, qt, kt: (0, kt[s], 0))
    ospec = pl.BlockSpec((B, CH), lambda s, qt, kt: (kt[s], 0))
    grid_spec = pltpu.PrefetchScalarGridSpec(
        num_scalar_prefetch=2, grid=(nsteps,),
        in_specs=[qspec, kspec, kspec, qspec, pl.BlockSpec(sel.shape, lambda s, qt, kt: (0, 0, 0))],
        out_specs=[ospec, ospec, pl.BlockSpec((B, LANE), lambda s, qt, kt: (kt[s], 0)),
                   pl.BlockSpec(memory_space=pl.ANY)],
        scratch_shapes=[pltpu.VMEM((4, B, LANE), F32), pltpu.VMEM((4, B, LANE), F32), pltpu.VMEM((4, LANE, B), _MXU),
                        pltpu.VMEM((n, 4, LANE, B), F32), pltpu.SemaphoreType.DMA])
    return pl.pallas_call(
        body, name=name, grid_spec=grid_spec,
        out_shape=[jax.ShapeDtypeStruct((T, CH), F32), jax.ShapeDtypeStruct((T, CH), F32),
                   jax.ShapeDtypeStruct((T, LANE), F32), jax.ShapeDtypeStruct((n, 4, LANE, B), F32)],
        compiler_params=_cp("arbitrary"))(qt, kt, qb, ka, va, doa, sel)


def _fox2_post(zf, dqt, dkn, dv, dfk, sel, gq, gk, bf, name):
    T = zf.shape[0]
    tm = _tile(T, FOX_TM)
    nt = T // tm
    B = dqt.shape[3]
    per = B // tm
    upper = jnp.asarray(_tri(tm, False), _MXU)

    def body(z_ref, dqt_ref, dk_ref, dv_ref, df_ref, p_ref, gq_ref, gk_ref, bf_ref, u_ref, dz_ref, sm_ref, carry, rc_s):
        @pl.when(pl.program_id(0) == 0)
        def _():
            carry[...] = jnp.zeros_like(carry)
            sm_ref[...] = jnp.zeros_like(sm_ref)

        lane = lax.broadcasted_iota(jnp.int32, (1, LANE), 1)
        dqn = jnp.zeros((tm, CH), F32)
        dfq = jnp.zeros((tm, LANE), F32)
        for h in range(4):
            blk = dqt_ref[0, h].T
            dqn = dqn + _exact_dot(blk * (HEAD ** -0.5), p_ref[h], _NT, "a")
            dfq = jnp.where(lane == h, _lane_col(blk, AUG_F), dfq)
        z = z_ref[...]
        q_raw, k_raw = z[:, :CH], z[:, CH:2 * CH]
        _, q_r = _head_rms(q_raw, gq_ref[...], 4)
        _, k_r = _head_rms(k_raw, gk_ref[...], 4)
        dq, dgq = _head_rms_bwd(dqn, q_raw, q_r, gq_ref[...], 4)
        dk, dgk = _head_rms_bwd(dk_ref[...], k_raw, k_r, gk_ref[...], 4)
        rc_s[...] = _exact_dot(u_ref[...], dfq + df_ref[...], _NN, "b") + carry[pl.ds(0, 1), :]
        carry[pl.ds(0, 1), :] = rc_s[pl.ds(0, 1), :]
        x = z[:, 3 * CH:] + bf_ref[...]
        dff = jnp.where(lane < 4, rc_s[...] * _sigmoid(-x), 0.0)
        dz_ref[:, :CH] = dq.astype(dz_ref.dtype)
        dz_ref[:, CH:2 * CH] = dk.astype(dz_ref.dtype)
        dz_ref[:, 2 * CH:3 * CH] = dv_ref[...].astype(dz_ref.dtype)
        dz_ref[:, 3 * CH:] = dff.astype(dz_ref.dtype)
        sm_ref[pl.ds(0, 1), :] += dgq
        sm_ref[pl.ds(1, 1), :] += dgk
        sm_ref[pl.ds(2, 1), :LANE] += jnp.sum(dff, axis=0, keepdims=True)

    rev = lambda i: (nt - 1 - i, 0)
    row = pl.BlockSpec((tm, CH), rev)
    lrow = pl.BlockSpec((tm, LANE), rev)
    vec = pl.BlockSpec((1, CH), lambda i: (0, 0))
    return pl.pallas_call(
        body, name=name, grid=(nt,),
        in_specs=[pl.BlockSpec((tm, 3 * CH + LANE), rev),
                  pl.BlockSpec((1, 4, LANE, tm), lambda i: ((nt - 1 - i) // per, 0, 0, (nt - 1 - i) % per)),
                  row, row, lrow, pl.BlockSpec(sel.shape, lambda i: (0, 0, 0)), vec, vec,
                  pl.BlockSpec((1, LANE), lambda i: (0, 0)), pl.BlockSpec((tm, tm), lambda i: (0, 0))],
        out_specs=[pl.BlockSpec((tm, 3 * CH + LANE), rev), pl.BlockSpec((8, CH), lambda i: (0, 0))],
        out_shape=[jax.ShapeDtypeStruct((T, 3 * CH + LANE), _MXU), jax.ShapeDtypeStruct((8, CH), F32)],
        scratch_shapes=[pltpu.VMEM((8, LANE), F32), pltpu.VMEM((tm, LANE), F32)],
        compiler_params=_cp("arbitrary"))(zf, dqt, dkn, dv, dfk, sel, gq, gk, bf, upper)


def _merge_fwd(acts, zg, wbr, wout, x1, name):
    T, D = x1.shape
    tm = _tile(T, 256)

    def body(a0, a1, a2, a3, zg_ref, wbr_ref, wout_ref, x_ref, o_ref, mg_ref):
        merged = None
        for i, a_ref in enumerate((a0, a1, a2, a3)):
            term = _sigmoid(zg_ref[:, i * D:(i + 1) * D]) * _dot(a_ref[...], wbr_ref[i])
            merged = term if merged is None else merged + term
        mg_ref[...] = merged.astype(mg_ref.dtype)
        o_ref[...] = x_ref[...] + _dot(merged, wout_ref[...])

    arow = pl.BlockSpec((tm, CH), lambda i: (i, 0))
    xrow = pl.BlockSpec((tm, D), lambda i: (i, 0))
    return pl.pallas_call(
        body, name=name, grid=(T // tm,),
        in_specs=[arow, arow, arow, arow, pl.BlockSpec((tm, 4 * D), lambda i: (i, 0)),
                  pl.BlockSpec((4, CH, D), lambda i: (0, 0, 0)), pl.BlockSpec((D, D), lambda i: (0, 0)), xrow],
        out_specs=[xrow, xrow],
        out_shape=[jax.ShapeDtypeStruct((T, D), F32), jax.ShapeDtypeStruct((T, D), _MXU)],
        compiler_params=_cp("parallel"))(*acts, zg, wbr, wout, x1)


def _merge_bwd(dx2, acts, zg, wbr, wout, name):
    T, D = dx2.shape
    tm = _tile(T, 256)
    nt = T // tm

    def body(dx_ref, a0, a1, a2, a3, zg_ref, wbr_ref, wout_ref, d0, d1, d2, d3, dzg_ref, dw_ref, dw_s):
        i = pl.program_id(0)

        @pl.when(i == 0)
        def _():
            dw_s[...] = jnp.zeros_like(dw_s)

        dm = _dot(dx_ref[...], wout_ref[...], _NT)
        for b, (a_ref, d_ref) in enumerate(((a0, d0), (a1, d1), (a2, d2), (a3, d3))):
            av = a_ref[...].astype(_MXU)
            g = _sigmoid(zg_ref[:, b * D:(b + 1) * D])
            p = _dot(av, wbr_ref[b])
            dzg_ref[:, b * D:(b + 1) * D] = (dm * p * (g * (1.0 - g))).astype(dzg_ref.dtype)
            dp = (dm * g).astype(_MXU)
            d_ref[...] = _dot(dp, wbr_ref[b], _NT)
            dw_s[b] += _dot(av, dp, _TN)

        @pl.when(i == nt - 1)
        def _():
            dw_ref[...] = dw_s[...].astype(dw_ref.dtype)

    arow = pl.BlockSpec((tm, CH), lambda i: (i, 0))
    xrow = pl.BlockSpec((tm, D), lambda i: (i, 0))
    grow = pl.BlockSpec((tm, 4 * D), lambda i: (i, 0))
    wspec = pl.BlockSpec((4, CH, D), lambda i: (0, 0, 0))
    ash = jax.ShapeDtypeStruct((T, CH), F32)
    return pl.pallas_call(
        body, name=name, grid=(nt,),
        in_specs=[xrow, arow, arow, arow, arow, grow, wspec, pl.BlockSpec((D, D), lambda i: (0, 0))],
        out_specs=[arow, arow, arow, arow, grow, wspec],
        out_shape=[ash, ash, ash, ash, jax.ShapeDtypeStruct((T, 4 * D), _MXU), jax.ShapeDtypeStruct((4, CH, D), _MXU)],
        scratch_shapes=[pltpu.VMEM((4, CH, D), F32)],
        compiler_params=_cp("arbitrary"))(dx2, *acts, zg, wbr, wout)


def _rows_2d(a):
    return a.reshape((-1, a.shape[-1])) if a.ndim > 1 else a.reshape((1, -1))


def _row_tile(rows, cols, n_bufs):
    padded = -(-cols // LANE) * LANE
    cap = max(8, (VMEM_LIMIT // 3) // (2 * n_bufs * 4 * padded))
    return _tile(rows, cap, 8)


def _sum8(recv, name):
    shape = recv.shape[1:]
    r2 = recv.reshape((N_DEV, -1, shape[-1]))
    rows, cols = r2.shape[1:]
    tr = _row_tile(rows, cols, N_DEV // 2 + 1)

    def body(r_ref, o_ref):
        acc = r_ref[0].astype(F32)
        for d in range(1, N_DEV):
            acc = acc + r_ref[d].astype(F32)
        o_ref[...] = acc

    out = pl.pallas_call(
        body, name=name, grid=(rows // tr,),
        in_specs=[pl.BlockSpec((N_DEV, tr, cols), lambda i: (0, i, 0))],
        out_specs=pl.BlockSpec((tr, cols), lambda i: (i, 0)),
        out_shape=jax.ShapeDtypeStruct((rows, cols), F32), compiler_params=_cp("parallel"))(r2)
    return out.reshape(shape)


def _adamw(w, g, m, v, name):
    slabs = w.ndim == 3 and w.shape[1] < 8
    if slabs:
        lead, rows, cols = None, w.shape[0], w.shape[1] * w.shape[2]
    elif w.ndim == 3:
        lead, rows, cols = w.shape
    else:
        lead, (rows, cols) = None, w.shape
    tr = _row_tile(rows, cols, 7) if not slabs else max(t for t in range(1, 513) if rows % t == 0)

    def body(w_ref, g_ref, m_ref, v_ref, d_ref, nm_ref, nv_ref):
        gv = g_ref[...]
        nm = ADAM_B1 * m_ref[...] + (1.0 - ADAM_B1) * gv
        nv = ADAM_B2 * v_ref[...] + (1.0 - ADAM_B2) * jnp.square(gv)
        m_hat = nm / (1.0 - ADAM_B1 ** ADAM_STEP)
        v_hat = nv / (1.0 - ADAM_B2 ** ADAM_STEP)
        d_ref[...] = -ADAM_LR * (m_hat / (jnp.sqrt(v_hat) + ADAM_EPS) + ADAM_WD * w_ref[...])
        nm_ref[...] = nm
        nv_ref[...] = nv

    if slabs:
        grid, sem = (rows // tr,), ("parallel",)
        spec = pl.BlockSpec((tr,) + w.shape[1:], lambda i: (i, 0, 0))
    elif lead is None:
        grid, sem = (rows // tr,), ("parallel",)
        spec = pl.BlockSpec((tr, cols), lambda i: (i, 0))
    else:
        grid, sem = (lead, rows // tr), ("parallel", "parallel")
        spec = pl.BlockSpec((None, tr, cols), lambda l, i: (l, i, 0))
    osh = jax.ShapeDtypeStruct(w.shape, F32)
    return tuple(pl.pallas_call(
        body, name=name, grid=grid, in_specs=[spec] * 4, out_specs=[spec] * 3,
        out_shape=[osh] * 3, compiler_params=_cp(*sem))(w, g, m, v))


def _exchange(items, name):
    n = len(items)
    widths, out_shapes = [], []
    for src, kind, ax in items:
        if kind == "gather":
            w = src.shape[ax]
            shp = list(src.shape)
            shp[ax] = N_DEV * w
        else:
            w = src.shape[ax] // N_DEV
            shp = list(src.shape)
            shp[ax] = w
            shp = [N_DEV] + shp
        widths.append(w)
        out_shapes.append(jax.ShapeDtypeStruct(tuple(shp), src.dtype))

    def body(*refs):
        srcs, outs = refs[:n], refs[n:2 * n]
        send, recv, lsem = refs[2 * n:]
        x, y, c = lax.axis_index("x"), lax.axis_index("y"), lax.axis_index("c")
        me = 4 * x + 2 * y + c

        def peer(k):
            b = k + 1
            px = 1 - x if b & 4 else x
            py = 1 - y if b & 2 else y
            pc = 1 - c if b & 1 else c
            return (px, py, pc), 4 * px + 2 * py + pc

        def win(ref, ax, idx, w):
            return ref.at[tuple([slice(None)] * ax + [pl.ds(idx * w, w)])]

        def ends(j, mine, theirs):
            _, kind, ax = items[j]
            if kind == "gather":
                return srcs[j], win(outs[j], ax, mine, widths[j])
            return win(srcs[j], ax, theirs, widths[j]), outs[j].at[mine]

        local, sent = [], []
        for j in range(n):
            s, d = ends(j, me, me)
            cp = pltpu.make_async_copy(s, d, lsem.at[j])
            cp.start()
            local.append(cp)
            for k in range(N_DEV - 1):
                dev, pid = peer(k)
                s, d = ends(j, me, pid)
                cp = pltpu.make_async_remote_copy(s, d, send.at[j, k], recv.at[j, k], device_id=dev,
                                                  device_id_type=pl.DeviceIdType.MESH)
                cp.start()
                sent.append(cp)
        for j in range(n):
            for k in range(N_DEV - 1):
                dev, pid = peer(k)
                s, d = ends(j, pid, me)
                pltpu.make_async_remote_copy(s, d, send.at[j, k], recv.at[j, k], device_id=dev,
                                             device_id_type=pl.DeviceIdType.MESH).wait_recv()
        for cp in sent:
            cp.wait_send()
        for cp in local:
            cp.wait()

    hbm = pl.BlockSpec(memory_space=pl.ANY)
    return pl.pallas_call(
        body, name=name, in_specs=[hbm] * n, out_specs=[hbm] * n, out_shape=out_shapes,
        scratch_shapes=[pltpu.SemaphoreType.DMA((n, N_DEV - 1)), pltpu.SemaphoreType.DMA((n, N_DEV - 1)),
                        pltpu.SemaphoreType.DMA((n,))],
        compiler_params=pltpu.CompilerParams(has_side_effects=True))(*[it[0] for it in items])


def _exchange_plan(items):
    widths, out_shapes = [], []
    for src, kind, ax in items:
        shp = list(src.shape)
        if kind == "gather":
            w = src.shape[ax]
            shp[ax] = N_DEV * w
        else:
            w = src.shape[ax] // N_DEV
            shp[ax] = w
            shp = [N_DEV] + shp
        widths.append(w)
        out_shapes.append((tuple(shp), src.dtype))
    return widths, out_shapes


def _exchange_refs(items, widths, srcs, outs):
    x, y, c = lax.axis_index("x"), lax.axis_index("y"), lax.axis_index("c")
    me = 4 * x + 2 * y + c

    def peer(k):
        b = k + 1
        px = 1 - x if b & 4 else x
        py = 1 - y if b & 2 else y
        pc = 1 - c if b & 1 else c
        return (px, py, pc), 4 * px + 2 * py + pc

    def win(ref, ax, idx, w):
        return ref.at[tuple([slice(None)] * ax + [pl.ds(idx * w, w)])]

    def ends(j, mine, theirs):
        _, kind, ax = items[j]
        if kind == "gather":
            return srcs[j], win(outs[j], ax, mine, widths[j])
        return win(srcs[j], ax, theirs, widths[j]), outs[j].at[mine]

    return me, peer, ends


_HBM = pl.BlockSpec(memory_space=pltpu.HBM)
_SEM = pl.BlockSpec(memory_space=pltpu.SEMAPHORE)


def _exchange_start(items, name):
    n = len(items)
    widths, out_shapes = _exchange_plan(items)
    meta = [(None, kind, ax) for _, kind, ax in items]

    def body(*refs):
        srcs, lands = refs[:n], refs[n:2 * n]
        send, recv, lsem = refs[2 * n], refs[2 * n + 1], refs[2 * n + 2]
        token = refs[-1]
        me, peer, ends = _exchange_refs(meta, widths, srcs, lands)
        for j in range(n):
            for k in range(N_DEV - 1):
                dev, pid = peer(k)
                s, d = ends(j, me, pid)
                q = j * (N_DEV - 1) + k
                pltpu.make_async_remote_copy(s, d, send.at[q], recv.at[q], device_id=dev,
                                             device_id_type=pl.DeviceIdType.MESH).start()
        for j in range(n):
            s, d = ends(j, me, me)
            pltpu.make_async_copy(s, d, lsem.at[j]).start()
        token[...] = jnp.zeros_like(token)

    srcs = [pltpu.with_memory_space_constraint(it[0], pltpu.HBM) for it in items]
    lands = [pltpu.with_memory_space_constraint(lax.empty(shp, dt), pltpu.HBM) for shp, dt in out_shapes]
    outs = pl.pallas_call(
        body, name=name,
        out_shape=(pltpu.SemaphoreType.DMA((n * (N_DEV - 1),)), pltpu.SemaphoreType.DMA((n * (N_DEV - 1),)),
                   pltpu.SemaphoreType.DMA((n,)),
                   *[pltpu.HBM(s.shape, s.dtype) for s in srcs], *[pltpu.HBM(shp, dt) for shp, dt in out_shapes],
                   jax.ShapeDtypeStruct((8, LANE), F32)),
        in_specs=[_HBM] * (2 * n),
        out_specs=(_SEM, _SEM, _SEM, *([_HBM] * (2 * n)), pl.BlockSpec(memory_space=pltpu.VMEM)),
        input_output_aliases={i: 3 + i for i in range(2 * n)},
        compiler_params=pltpu.CompilerParams(has_side_effects=pltpu.SideEffectType.DATAFLOW_SIDE_EFFECTING),
    )(*srcs, *lands)
    handle = (meta, widths, outs[0], outs[1], outs[2], outs[3:3 + n], outs[3 + n:3 + 2 * n])
    return handle, outs[-1]


def _exchange_wait(handle, after, name):
    meta, widths, send_sem, recv_sem, local_sem, src_thru, land_thru = handle
    n = len(meta)

    def body(*refs):
        srcs, lands = refs[:n], refs[n:2 * n]
        send, recv, lsem = refs[2 * n], refs[2 * n + 1], refs[2 * n + 2]
        me, peer, ends = _exchange_refs(meta, widths, srcs, lands)
        for j in range(n):
            for k in range(N_DEV - 1):
                dev, pid = peer(k)
                q = j * (N_DEV - 1) + k
                s, d = ends(j, me, pid)
                pltpu.make_async_remote_copy(s, d, send.at[q], recv.at[q], device_id=dev,
                                             device_id_type=pl.DeviceIdType.MESH).wait_send()
                s, d = ends(j, pid, me)
                pltpu.make_async_remote_copy(s, d, send.at[q], recv.at[q], device_id=dev,
                                             device_id_type=pl.DeviceIdType.MESH).wait_recv()
        for j in range(n):
            s, d = ends(j, me, me)
            pltpu.make_async_copy(s, d, lsem.at[j]).wait()

    outs = pl.pallas_call(
        body, name=name,
        out_shape=tuple(pltpu.HBM(a.shape, a.dtype) for a in (*src_thru, *land_thru)),
        in_specs=[_HBM] * (2 * n) + [_SEM, _SEM, _SEM, pl.BlockSpec(memory_space=pl.ANY)],
        out_specs=tuple([_HBM] * (2 * n)),
        input_output_aliases={i: i for i in range(2 * n)},
        compiler_params=pltpu.CompilerParams(has_side_effects=pltpu.SideEffectType.DATAFLOW_SIDE_EFFECTING),
    )(*src_thru, *land_thru, send_sem, recv_sem, local_sem, after)
    return list(outs[n:])


def _pack(arrs):
    flat = jnp.concatenate([a.reshape(-1).astype(F32) for a in arrs])
    n = flat.shape[0]
    rows = -(-n // (8 * LANE)) * 8
    return jnp.pad(flat, (0, rows * LANE - n)).reshape(rows, LANE)


def _unpack(buf, shapes):
    flat = buf.reshape(-1)
    out, off = [], 0
    for s in shapes:
        sz = int(np.prod(s))
        out.append(flat[off:off + sz].reshape(s))
        off += sz
    return out


def _pad_axis(a, axis, size):
    pad = [(0, 0)] * a.ndim
    pad[axis] = (0, size - a.shape[axis])
    return jnp.pad(a, pad)


def _ffn_forward(x, g, wg, wu, wd, tag):
    a = _rms_fwd(x, g, f"{tag}_rms")
    gate, up, hid = _ffn_up(a, wg, wu, f"{tag}_up")
    out = _mm([(hid, wd)], "nn", F32, f"{tag}_down", scale=0.5, res=x)
    return out, (x, a, gate, up, hid)


def _ffn_backward(dxp, saved, g, wg, wu, wd, tag, emit=None, after=None):
    x, a, gate, up, hid = saved
    dxp, dxp_n = dxp
    d_gate, d_up = _ffn_bwd_hid(dxp_n, wd, gate, up, f"{tag}_bwd_hid", after)
    d_wd = _mm([(hid, dxp_n)], "tn", _MXU, f"{tag}_dwd", scale=0.5, tk=2048)
    tok = emit("down", d_wd) if emit is not None else None
    d_wg = _mm([(a, d_gate)], "tn", _MXU, f"{tag}_dwg", tk=2048, after=tok)
    tok = emit("gate", d_wg) if emit is not None else None
    d_wu = _mm([(a, d_up)], "tn", _MXU, f"{tag}_dwu", tk=2048, after=tok)
    tok = emit("up", d_wu) if emit is not None else None
    d_a = _mm([(d_gate, wg), (d_up, wu)], "nt", F32, f"{tag}_da", after=tok)
    dx, dg = _rms_bwd(d_a, x, g, dxp, f"{tag}_rms_bwd")
    return dx, dg, d_wg, d_wu, d_wd


def _tile_vec(v, reps):
    return jnp.tile(v.reshape(1, -1), (1, reps))


def _mixer_forward(x1, p, consts, tag):
    h = _rms_fwd(x1, p["mix_norm"], f"{tag}_rms")
    zg = _mm([(h, p["w_zg"])], "nn", F32, f"{tag}_in_g")
    zc = _mm([(h, p["w_conf"])], "nn", F32, f"{tag}_in_c")
    zs = _mm([(h, p["w_sc"])], "nn", F32, f"{tag}_in_s")
    zw = _mm([(h, p["w_swa"])], "nn", F32, f"{tag}_in_w")
    zf = _mm([(h, p["w_fox"])], "nn", F32, f"{tag}_in_f")
    u1, act_c = _conf_fwd(zc, p["conf_dw"], p["conf_dw_b"], p["conf_ln_g"], p["conf_ln_b"], f"{tag}_conf")
    act_s = _sc_fwd(zs, p["sc_conv"], f"{tag}_sc")
    act_w = _swa_fwd(zw, p["swa_q_norm"], p["swa_k_norm"], p["swa_sink"], consts["bias"], consts["expand"], f"{tag}_swa")
    qa, ka, va, vat = _fox2_prep(zf, p["fox_q_norm"], p["fox_k_norm"], p["b_forget"], consts["sel"], f"{tag}_fox_prep")
    act_f, qb = _fox2_fwd(qa, ka, vat, consts["sel"], f"{tag}_fox")
    acts = (act_c, act_s, act_w, act_f)
    x2, merged = _merge_fwd(acts, zg, p["w_br"], p["w_out"], x1, f"{tag}_merge")
    saved = (x1, h, zg, zc, zs, zw, zf, u1, acts, qb, ka, va, merged)
    return x2, saved


def _mixer_backward(dx2, saved, p, consts, tag, after=None):
    x1, h, zg, zc, zs, zw, zf, u1, acts, qb, ka, va, merged = saved
    dx2, dx2_n = dx2
    g = {}
    g["w_out"] = _mm([(merged, dx2_n)], "tn", _MXU, f"{tag}_dwout", tk=2048, after=after)
    d_c, d_s, d_w, d_f, dzg, g["w_br"] = _merge_bwd(dx2_n, acts, zg, p["w_br"], p["w_out"], f"{tag}_merge_bwd")
    du1, sm_c = _conf_bwd_ln(d_c, u1, p["conf_ln_g"], p["conf_ln_b"], f"{tag}_conf_bwd_ln")
    dzc, g["conf_dw"] = _conf_bwd_conv(zc, du1, p["conf_dw"], f"{tag}_conf_bwd_conv")
    g["conf_ln_g"], g["conf_ln_b"], g["conf_dw_b"] = sm_c[0], sm_c[1], sm_c[2]
    dzs, g["sc_conv"] = _sc_bwd(zs, d_s, p["sc_conv"], f"{tag}_sc_bwd")
    dzw, dgq, dgk, g["swa_sink"], g["rel_bias"] = _swa_bwd(
        zw, d_w, p["swa_q_norm"], p["swa_k_norm"], p["swa_sink"], consts["bias"], consts["bucket"], consts["expand"],
        f"{tag}_swa_bwd")
    g["swa_q_norm"], g["swa_k_norm"] = dgq, dgk
    doa = _fox2_bwd_prep(acts[3], d_f, consts["sel"], f"{tag}_fox_bwd_prep")
    dkn, dv, dfk, dqt = _fox2_bwd(qb, ka, va, doa, consts["sel"], f"{tag}_fox_bwd")
    dzf, sm_f = _fox2_post(zf, dqt, dkn, dv, dfk, consts["sel"], p["fox_q_norm"], p["fox_k_norm"], p["b_forget"], f"{tag}_fox_post")
    g["fox_q_norm"], g["fox_k_norm"], g["b_forget"] = sm_f[0], sm_f[1], sm_f[2]
    parts = ((dzg, "w_zg"), (dzc, "w_conf"), (dzs, "w_sc"), (dzw, "w_swa"), (dzf, "w_fox"))
    for dz, wname in parts:
        g[wname] = _mm([(h, dz)], "tn", _MXU, f"{tag}_d{wname}", tk=2048)
    dh = _mm([(dz, p[wname]) for dz, wname in parts], "nt", F32, f"{tag}_dh", tm=512)
    dx1, g["mix_norm"] = _rms_bwd(dh, x1, p["mix_norm"], dx2, f"{tag}_rms_bwd")
    return dx1, g


W_NAMES = ['rel_bias', 'ffn1_norm', 'ffn1_w_gate', 'ffn1_w_up', 'ffn1_w_down', 'mix_norm', 'w_in', 'b_forget', 'conf_dw',
           'conf_dw_b', 'conf_ln_g', 'conf_ln_b', 'conf_w_out', 'sc_conv', 'sc_w_out', 'swa_q_norm', 'swa_k_norm',
           'swa_sink', 'swa_w_o', 'fox_q_norm', 'fox_k_norm', 'fox_w_o', 'w_out', 'ffn2_norm', 'ffn2_w_gate',
           'ffn2_w_up', 'ffn2_w_down']
SMALL = ['rel_bias', 'ffn1_norm', 'mix_norm', 'b_forget', 'conf_dw', 'conf_dw_b', 'conf_ln_g', 'conf_ln_b', 'sc_conv',
         'swa_q_norm', 'swa_k_norm', 'swa_sink', 'fox_q_norm', 'fox_k_norm', 'ffn2_norm']
BRANCH_W = ['conf_w_out', 'sc_w_out', 'swa_w_o', 'fox_w_o']
IN_CONF, IN_SC, IN_SWA, IN_FOX, IN_FF = (0, 512), (512, 1280), (1280, 1792), (1792, 2560), (2560, 2564)


def _step(w, m, v, x, loss_target):
    T, D = x.shape
    L = w["w_out"].shape[0]
    fs = w["ffn1_w_gate"].shape[2]
    fsp = -(-fs // LANE) * LANE
    dev = 4 * lax.axis_index("x") + 2 * lax.axis_index("y") + lax.axis_index("c")

    def cast(a):
        return a.astype(_MXU)

    win = w["w_in"]
    fox_cols = jnp.concatenate([win[..., IN_FOX[0]:IN_FF[1]],
                                jnp.zeros(win.shape[:2] + (LANE - (IN_FF[1] - IN_FF[0]),), win.dtype)], axis=-1)
    shards = {
        "ffn1_w_gate": (cast(_pad_axis(w["ffn1_w_gate"], 2, fsp)), 2),
        "ffn1_w_up": (cast(_pad_axis(w["ffn1_w_up"], 2, fsp)), 2),
        "ffn1_w_down": (cast(_pad_axis(w["ffn1_w_down"], 1, fsp)), 1),
        "ffn2_w_gate": (cast(_pad_axis(w["ffn2_w_gate"], 2, fsp)), 2),
        "ffn2_w_up": (cast(_pad_axis(w["ffn2_w_up"], 2, fsp)), 2),
        "ffn2_w_down": (cast(_pad_axis(w["ffn2_w_down"], 1, fsp)), 1),
        "w_zg": (cast(win[..., IN_FF[1]:]), 1),
        "w_conf": (cast(win[..., IN_CONF[0]:IN_CONF[1]]), 1),
        "w_sc": (cast(win[..., IN_SC[0]:IN_SC[1]]), 1),
        "w_swa": (cast(win[..., IN_SWA[0]:IN_SWA[1]]), 1),
        "w_fox": (cast(fox_cols), 1),
        "w_out": (cast(w["w_out"]), 1),
        "w_br": (cast(jnp.stack([w[n] for n in BRANCH_W], axis=1)), 3),
    }
    big = list(shards)
    conv_shard = jnp.concatenate([jnp.swapaxes(w["conf_dw"], 1, 2), jnp.swapaxes(w["sc_conv"], 1, 2)], axis=2)

    stages = (("ffn1", ["ffn1_w_gate", "ffn1_w_up"]), ("ffn1d", ["ffn1_w_down"]),
              ("mix", ["w_zg", "w_conf", "w_sc", "w_swa", "w_fox", "w_out", "w_br"]),
              ("ffn2", ["ffn2_w_gate", "ffn2_w_up", "ffn2_w_down"]))
    stage_names = dict(stages)
    flight = {}

    def depart(l, st, dep):
        items = [(shards[n][0][l], "gather", shards[n][1] - 1) for n in stage_names[st]]
        if (l, st) == (0, "mix"):
            items.append((conv_shard, "gather", 1))
        if dep is not None:
            src0 = items[0][0]
            zero = (dep[(0,) * dep.ndim].astype(F32) * 0.0).astype(src0.dtype)
            items[0] = (src0 + zero,) + items[0][1:]
        flight[l, st], tok = _exchange_start(items, f"gather_start_l{l}_{st}")
        return tok

    def arrive(l, st, after):
        got = _exchange_wait(flight.pop((l, st)), after, f"gather_wait_l{l}_{st}")
        params[l].update(zip(stage_names[st], got))
        if (l, st) == (0, "mix"):
            conv_full = jnp.swapaxes(got[-1], 1, 2)
            for i, q in enumerate(params):
                q["conf_dw"] = _pad_axis(conv_full[i, :CONV_K], 0, CONV_HALO)
                q["sc_conv"] = _pad_axis(conv_full[i, CONV_K:], 0, SC_HALO)
        return got[0]

    bucket = jnp.asarray(_swa_bucket_matrix(min(SWA_TQ, T)))
    consts = {"bucket": bucket, "expand": jnp.asarray(_kv_expand_matrix(), _MXU),
              "sel": jnp.asarray(_head_select_matrix(), _MXU),
              "bias": _swa_bias(w["rel_bias"], bucket, "swa_bias")}

    def layer_params(l):
        p = {}
        for n in ("ffn1_norm", "mix_norm", "ffn2_norm", "conf_dw_b", "conf_ln_g", "conf_ln_b"):
            p[n] = w[n][l].reshape(1, -1)
        p["swa_q_norm"], p["fox_q_norm"] = _tile_vec(w["swa_q_norm"][l], 4), _tile_vec(w["fox_q_norm"][l], 4)
        p["swa_k_norm"], p["fox_k_norm"] = _tile_vec(w["swa_k_norm"][l], 2), _tile_vec(w["fox_k_norm"][l], 4)
        p["swa_sink"] = w["swa_sink"][l].reshape(1, 4)
        p["b_forget"] = _pad_axis(w["b_forget"][l].reshape(1, 4), 1, LANE)
        return p

    params = [layer_params(l) for l in range(L)]
    saved = [None] * L
    cur = x
    first = depart(0, "ffn1", None)
    for l, p in enumerate(params):
        a = _rms_fwd(cur, p["ffn1_norm"] + (0.0 if l else first[0:1, 0:1]), f"l{l}_ffn1_rms")
        got = arrive(l, "ffn1", a)
        tok = depart(l, "mix", depart(l, "ffn1d", got))
        gate, up, hid = _ffn_up(a, p["ffn1_w_gate"], p["ffn1_w_up"], f"l{l}_ffn1_up", after=tok)
        got = arrive(l, "ffn1d", hid)
        tok = depart(l, "ffn2", got)
        x1 = _mm([(hid, p["ffn1_w_down"])], "nn", F32, f"l{l}_ffn1_down", scale=0.5, res=cur, after=tok)
        s1 = (cur, a, gate, up, hid)
        got = arrive(l, "mix", x1)
        zero = depart(l + 1, "ffn1", got)[0:1, 0:1] if l + 1 < L else 0.0
        x2, s2 = _mixer_forward(x1, dict(p, mix_norm=p["mix_norm"] + zero), consts, f"l{l}_mix")
        arrive(l, "ffn2", x2)
        cur, s3 = _ffn_forward(x2, p["ffn2_norm"], p["ffn2_w_gate"], p["ffn2_w_up"], p["ffn2_w_down"], f"l{l}_ffn2")
        saved[l] = (s1, s2, s3)
    dcur, loss_part = _loss_grad(cur, loss_target)

    grads = [None] * L
    leaving = []

    def leave(l, st, names, g):
        h, tok = _exchange_start([(g[n], "scatter", shards[n][1] - 1) for n in names], f"scatter_start_l{l}_{st}")
        leaving.append((l, st, names, h))
        return tok

    tok = None
    for l in reversed(range(L)):
        p = params[l]
        s1, s2, s3 = saved[l]
        g = {}
        dcur, g["ffn2_norm"], g["ffn2_w_gate"], g["ffn2_w_up"], g["ffn2_w_down"] = _ffn_backward(
            dcur, s3, p["ffn2_norm"], p["ffn2_w_gate"], p["ffn2_w_up"], p["ffn2_w_down"], f"l{l}_ffn2", after=tok)
        tok = leave(l, "ffn2", stage_names["ffn2"], g)
        dcur, gm = _mixer_backward(dcur, s2, p, consts, f"l{l}_mix", after=tok)
        g.update(gm)
        tok = leave(l, "mix", stage_names["mix"], g)
        ffn1 = ["ffn1_w_gate", "ffn1_w_up", "ffn1_w_down"]
        emit = (lambda which, arr, l=l: leave(l, which, [f"ffn1_w_{which}"], {f"ffn1_w_{which}": arr})) if l == 0 else None
        dcur, g["ffn1_norm"], g["ffn1_w_gate"], g["ffn1_w_up"], g["ffn1_w_down"] = _ffn_backward(
            dcur, s1, p["ffn1_norm"], p["ffn1_w_gate"], p["ffn1_w_up"], p["ffn1_w_down"], f"l{l}_ffn1", emit, after=tok)
        if emit is None:
            tok = leave(l, "ffn1", ffn1, g)
        grads[l] = g
    grad_x = dcur[0]

    gsum = {n: [None] * L for n in big}
    for l, st, names, h in leaving:
        for n, r in zip(names, _exchange_wait(h, grad_x, f"scatter_wait_l{l}_{st}")):
            gsum[n][l] = _sum8(r, f"sum_{n}_l{l}")
    gsum = {n: jnp.stack(parts) for n, parts in gsum.items()}
    gw = {}
    wide = ("ffn1_w_gate", "ffn1_w_up", "ffn2_w_gate", "ffn2_w_up")
    gw_t = {n: jnp.swapaxes(gsum[n][:, :, :fs], 1, 2) for n in wide}
    for n in wide:
        gw[n] = jnp.swapaxes(gw_t[n], 1, 2)
    for n in ("ffn1_w_down", "ffn2_w_down"):
        gw[n] = gsum[n][:, :fs, :]
    gw["w_out"] = gsum["w_out"]
    for i, n in enumerate(BRANCH_W):
        gw[n] = gsum["w_br"][:, i]
    gw["w_in"] = jnp.concatenate([gsum["w_conf"], gsum["w_sc"], gsum["w_swa"],
                                  gsum["w_fox"][..., :IN_FF[1] - IN_FOX[0]], gsum["w_zg"]], axis=-1)

    def small_partial(n):
        per_layer = [grads[l][n] for l in range(L)]
        if n == "rel_bias":
            return sum(pl_[:, :4] for pl_ in per_layer)
        if n in ("swa_sink", "b_forget"):
            return jnp.stack([a.reshape(-1)[:4] for a in per_layer])
        if n in ("swa_q_norm", "fox_q_norm", "fox_k_norm"):
            return jnp.stack([a.reshape(4, HEAD).sum(0) for a in per_layer])
        if n == "swa_k_norm":
            return jnp.stack([a.reshape(2, HEAD).sum(0) for a in per_layer])
        if n == "conf_dw":
            return jnp.stack([a[:CONV_K] for a in per_layer])
        if n == "sc_conv":
            return jnp.stack([a[:SC_K] for a in per_layer])
        return jnp.stack([a.reshape(-1) for a in per_layer])

    partial = [small_partial(n) for n in SMALL]
    small_shapes = [a.shape for a in partial]
    all_parts = _exchange([(_pack(partial), "gather", 0)], "gather_small_grads")[0]
    rows = all_parts.shape[0] // N_DEV
    small_sum = _unpack(_sum8(all_parts.reshape(N_DEV, rows, LANE), "sum_small"), small_shapes)
    for n, a in zip(SMALL, small_sum):
        if n in ("conf_dw", "sc_conv"):
            cs = w[n].shape[2]
            a = lax.dynamic_slice_in_dim(a, dev * cs, cs, axis=2)
        gw[n] = a

    delta, new_m, new_v = {}, {}, {}
    for n in W_NAMES:
        if n in wide:
            outs = _adamw(jnp.swapaxes(w[n], 1, 2), gw_t[n], jnp.swapaxes(m[n], 1, 2), jnp.swapaxes(v[n], 1, 2),
                          f"adamw_{n}")
            delta[n], new_m[n], new_v[n] = (jnp.swapaxes(o, 1, 2) for o in outs)
        elif n == "w_in":
            outs = _adamw(*(jnp.transpose(a, (2, 0, 1)) for a in (w[n], gw[n], m[n], v[n])), f"adamw_{n}")
            delta[n], new_m[n], new_v[n] = (jnp.transpose(o, (1, 2, 0)) for o in outs)
        elif n not in SMALL:
            delta[n], new_m[n], new_v[n] = _adamw(w[n], gw[n], m[n], v[n], f"adamw_{n}")
    shapes = [w[n].shape for n in SMALL]
    outs = _adamw(_pack([w[n] for n in SMALL]), _pack([gw[n] for n in SMALL]), _pack([m[n] for n in SMALL]),
                  _pack([v[n] for n in SMALL]), "adamw_small")
    for res, out in zip((delta, new_m, new_v), outs):
        for n, a in zip(SMALL, _unpack(out, shapes)):
            res[n] = a

    loss = lax.psum(loss_part[0, 0], ("x", "y", "c"))
    return loss, grad_x, gw, delta, new_m, new_v


def kernel(x, rel_bias, ffn1_norm, ffn1_w_gate, ffn1_w_up, ffn1_w_down, mix_norm, w_in, b_forget, conf_dw, conf_dw_b, conf_ln_g, conf_ln_b, conf_w_out, sc_conv, sc_w_out, swa_q_norm, swa_k_norm, swa_sink, swa_w_o, fox_q_norm, fox_k_norm, fox_w_o, w_out, ffn2_norm, ffn2_w_gate, ffn2_w_up, ffn2_w_down, loss_target, m_rel_bias, m_ffn1_norm, m_ffn1_w_gate, m_ffn1_w_up, m_ffn1_w_down, m_mix_norm, m_w_in, m_b_forget, m_conf_dw, m_conf_dw_b, m_conf_ln_g, m_conf_ln_b, m_conf_w_out, m_sc_conv, m_sc_w_out, m_swa_q_norm, m_swa_k_norm, m_swa_sink, m_swa_w_o, m_fox_q_norm, m_fox_k_norm, m_fox_w_o, m_w_out, m_ffn2_norm, m_ffn2_w_gate, m_ffn2_w_up, m_ffn2_w_down, v_rel_bias, v_ffn1_norm, v_ffn1_w_gate, v_ffn1_w_up, v_ffn1_w_down, v_mix_norm, v_w_in, v_b_forget, v_conf_dw, v_conf_dw_b, v_conf_ln_g, v_conf_ln_b, v_conf_w_out, v_sc_conv, v_sc_w_out, v_swa_q_norm, v_swa_k_norm, v_swa_sink, v_swa_w_o, v_fox_q_norm, v_fox_k_norm, v_fox_w_o, v_w_out, v_ffn2_norm, v_ffn2_w_gate, v_ffn2_w_up, v_ffn2_w_down):
    args = locals()
    w = {n: args[n] for n in W_NAMES}
    m = {n: args["m_" + n] for n in W_NAMES}
    v = {n: args["v_" + n] for n in W_NAMES}
    T, D = x.shape[-2:]
    loss, grad_x, gw, delta, new_m, new_v = _step(w, m, v, x.reshape(T, D), loss_target.reshape(T, D))
    return (loss, grad_x.reshape(x.shape), *[gw[n] for n in W_NAMES], *[delta[n] for n in W_NAMES],
            *[new_m[n] for n in W_NAMES], *[new_v[n] for n in W_NAMES])
```
